```python
import math
import jax, jax.numpy as jnp
from jax import lax
import numpy as np

D_MODEL = 1024
BATCH = 8
SEQ = 2048
DEPTH = 2
DEC_BATCH = 128
DEC_SEQ = 1
PAST_LEN = 16384
PAGE_SIZE = 128

N_EVEN = (DEPTH + 1) // 2
N_ODD = DEPTH // 2

MIX_WIDTH = D_MODEL
GLA_HEADS = 4
GLA_VALUE = MIX_WIDTH // 2
GLA_DV = GLA_VALUE // GLA_HEADS
GLA_DK = GLA_DV // 2
GLA_LOWRANK = 16
GATE_NORMALIZER = 16.0
GLA_CHUNK = 64
CONV_CH = MIX_WIDTH - GLA_VALUE
CONV_K = 3

IN0_WIDTHS = (GLA_HEADS * GLA_DK, GLA_HEADS * GLA_DK, GLA_VALUE, GLA_VALUE, GLA_LOWRANK,
              CONV_CH, CONV_CH, CONV_CH)
IN0_DIM = sum(IN0_WIDTHS)

S5_WIDTH = D_MODEL
S5_GROUP = 16
S5_GROUPS = S5_WIDTH // S5_GROUP
S5_STATE = 64

D_FF = 11 * 256
N_EXPERTS = 8
TOP_K = 2
D_FF_EXPERT = 7 * D_MODEL // 2
EPS = 1e-6

kernel_name = "hybrid_gla_shortconv_s5_moe_decode_step"


def rmsnorm(x, g):
    xf = x.astype(jnp.float32)
    y = xf * lax.rsqrt(jnp.mean(xf * xf, axis=-1, keepdims=True) + EPS)
    return (y * g.astype(jnp.float32)).astype(x.dtype)


def _split_in0(p):
    bounds, acc = [], 0
    for w in IN0_WIDTHS[:-1]:
        acc += w
        bounds.append(acc)
    return jnp.split(p, bounds, axis=-1)


def gla_chunked(q, k, v, logg, s0):
    f32 = jnp.float32
    bsz, t = q.shape[:2]
    L = math.gcd(t, GLA_CHUNK)
    n = t // L
    shp = lambda z, d: z.astype(f32).reshape(bsz, n, L, GLA_HEADS, d)
    q = shp(q, GLA_DK) * (GLA_DK ** -0.5)
    k = shp(k, GLA_DK)
    v = shp(v, GLA_DV)
    b = jnp.cumsum(shp(logg, GLA_DK), axis=2)
    b_last = b[:, :, -1]
    q_in = q * jnp.exp(b)
    att = jnp.einsum('bnihd,bnjhd->bnhij', q_in, k * jnp.exp(-b))
    att = jnp.where(jnp.tril(jnp.ones((L, L), bool)), att, 0.0)
    o = jnp.einsum('bnhij,bnjhv->bnihv', att, v)
    ds = jnp.einsum('bnjhd,bnjhv->bnhdv', k * jnp.exp(b_last[:, :, None] - b), v)
    decay = jnp.exp(b_last)

    def step(s, inp):
        d, dsn = inp
        return s * d[..., None] + dsn, s

    s_final, s_start = lax.scan(step, s0.astype(f32),
                                (jnp.moveaxis(decay, 1, 0), jnp.moveaxis(ds, 1, 0)))
    s_start = jnp.moveaxis(s_start, 0, 1)
    o = o + jnp.einsum('bnihd,bnhdv->bnihv', q_in, s_start)
    return o.reshape(bsz, t, GLA_HEADS, GLA_DV), s_final


def gla_conv_mixer(h, w_in, w_gate_up, b_gate_up, g_head, w_conv, w_out, s_gla, conv_buf):
    f32 = jnp.float32
    bsz, t, _ = h.shape
    q, k, v, og, glr, ch, cb, cc = _split_in0(h @ w_in)
    logg = jax.nn.log_sigmoid((glr @ w_gate_up + b_gate_up).astype(f32)) / GATE_NORMALIZER
    hd = lambda z, d: z.reshape(bsz, t, GLA_HEADS, d)
    o, s_new = gla_chunked(hd(q, GLA_DK), hd(k, GLA_DK), hd(v, GLA_DV), hd(logg, GLA_DK), s_gla)
    o = rmsnorm(o, g_head).reshape(bsz, t, GLA_VALUE) * jax.nn.silu(og.astype(f32))
    u = cc * ch
    up = jnp.concatenate([conv_buf.astype(u.dtype), u], axis=1)
    wc = w_conv.astype(u.dtype)
    y = sum(wc[j] * up[:, j:j + t] for j in range(CONV_K))
    y = cb * y
    mixed = jnp.concatenate([o.astype(h.dtype), y], axis=-1)
    return mixed @ w_out, s_new, up[:, t:]


def s5_mixer(h, w_in, a_re, a_im, log_dt, b_re, b_im, c_re, c_im, d_skip, w_glu, s_re, s_im):
    f32 = jnp.float32
    bsz, t, _ = h.shape
    u = (h @ w_in).astype(f32).reshape(bsz, t, S5_GROUPS, S5_GROUP)
    a_re, a_im = a_re.astype(f32), a_im.astype(f32)
    b_re, b_im, c_re, c_im = b_re.astype(f32), b_im.astype(f32), c_re.astype(f32), c_im.astype(f32)
    dt = jnp.exp(log_dt.astype(f32))[:, None]
    mag = jnp.exp(dt * a_re)
    ab_re, ab_im = mag * jnp.cos(dt * a_im), mag * jnp.sin(dt * a_im)
    den = a_re * a_re + a_im * a_im
    nr, ni = ab_re - 1.0, ab_im
    f_re = (nr * a_re + ni * a_im) / den
    f_im = (ni * a_re - nr * a_im) / den
    bb_re = f_re[..., None] * b_re - f_im[..., None] * b_im
    bb_im = f_re[..., None] * b_im + f_im[..., None] * b_re
    bu_re = jnp.einsum('btgi,gpi->btgp', u, bb_re)
    bu_im = jnp.einsum('btgi,gpi->btgp', u, bb_im)
    s_re, s_im = s_re.astype(f32), s_im.astype(f32)
    bu_re = bu_re.at[:, 0].add(ab_re * s_re - ab_im * s_im)
    bu_im = bu_im.at[:, 0].add(ab_re * s_im + ab_im * s_re)
    a_re_t = jnp.broadcast_to(ab_re, bu_re.shape)
    a_im_t = jnp.broadcast_to(ab_im, bu_im.shape)

    def combine(e1, e2):
        a1r, a1i, b1r, b1i = e1
        a2r, a2i, b2r, b2i = e2
        return (a1r * a2r - a1i * a2i, a1r * a2i + a1i * a2r,
                a2r * b1r - a2i * b1i + b2r, a2r * b1i + a2i * b1r + b2i)

    _, _, x_re, x_im = lax.associative_scan(combine, (a_re_t, a_im_t, bu_re, bu_im), axis=1)
    y = (jnp.einsum('gip,btgp->btgi', c_re, x_re) - jnp.einsum('gip,btgp->btgi', c_im, x_im)
         + d_skip.astype(f32) * u).reshape(bsz, t, S5_WIDTH)
    za, zg = jnp.split(jax.nn.gelu(y).astype(h.dtype) @ w_glu, 2, axis=-1)
    return za * jax.nn.sigmoid(zg), x_re[:, -1], x_im[:, -1]


def swiglu(h, w_gate, w_up, w_down):
    return (jax.nn.silu(h @ w_gate) * (h @ w_up)) @ w_down


def moe_swiglu(h, w_router, w_gate, w_up, w_down):
    f32 = jnp.float32
    bsz, t, d = h.shape
    xt = h.reshape(-1, d)
    logits = (xt @ w_router).astype(f32)
    top_v, top_i = lax.top_k(logits, TOP_K)
    wts = jax.nn.softmax(top_v, axis=-1)
    comb = jnp.einsum('tk,tke->te', wts, jax.nn.one_hot(top_i, N_EXPERTS, dtype=f32))
    y = jnp.zeros(xt.shape, f32)
    for e in range(N_EXPERTS):
        he = jax.nn.silu(xt @ w_gate[e]) * (xt @ w_up[e])
        y = y + comb[:, e:e + 1] * (he @ w_down[e])
    return y.astype(h.dtype).reshape(bsz, t, d)


def trunk(x, c, s_gla, s_conv, s_re, s_im, params):
    (w_ada, b_ada, g_norm, w_in0, w_gate_up, b_gate_up, g_head_norm, w_conv, w_out0,
     w_ffn_gate, w_ffn_up, w_ffn_down, w_in1, s5_a_re, s5_a_im, s5_log_dt, s5_b_re, s5_b_im,
     s5_c_re, s5_c_im, s5_d, w_glu, w_router, w_exp_gate, w_exp_up, w_exp_down) = params
    cs = jax.nn.silu(c)
    new_gla, new_conv, new_re, new_im = [], [], [], []
    for layer in range(DEPTH):
        i = layer // 2
        mod = (cs @ w_ada[layer] + b_ada[layer])[:, None, :]
        sh_m, sc_m, gt_m, sh_f, sc_f, gt_f = jnp.split(mod, 6, axis=-1)
        h = rmsnorm(x, g_norm[layer, 0]) * (1 + sc_m) + sh_m
        if layer % 2 == 0:
            o, st_g, st_c = gla_conv_mixer(h, w_in0[i], w_gate_up[i], b_gate_up[i], g_head_norm[i],
                                           w_conv[i], w_out0[i], s_gla[:, i], s_conv[:, i])
            new_gla.append(st_g)
            new_conv.append(st_c)
        else:
            o, st_r, st_i = s5_mixer(h, w_in1[i], s5_a_re[i], s5_a_im[i], s5_log_dt[i], s5_b_re[i],
                                     s5_b_im[i], s5_c_re[i], s5_c_im[i], s5_d[i], w_glu[i],
                                     s_re[:, i], s_im[:, i])
            new_re.append(st_r)
            new_im.append(st_i)
        x = x + gt_m * rmsnorm(o, g_norm[layer, 1])
        h = rmsnorm(x, g_norm[layer, 2]) * (1 + sc_f) + sh_f
        if layer % 2 == 0:
            f = swiglu(h, w_ffn_gate[i], w_ffn_up[i], w_ffn_down[i])
        else:
            f = moe_swiglu(h, w_router[i], w_exp_gate[i], w_exp_up[i], w_exp_down[i])
        x = x + gt_f * rmsnorm(f, g_norm[layer, 3])
    return (x, jnp.stack(new_gla, 1), jnp.stack(new_conv, 1),
            jnp.stack(new_re, 1), jnp.stack(new_im, 1))


def setup_inputs(seed: int = 0) -> dict:
    f32 = jnp.float32
    key = jax.random.key(seed)
    ks = iter(jax.random.split(key, 48))
    nrm = lambda shape, scale: jax.random.normal(next(ks), shape, f32) * scale
    D, G, P = D_MODEL, S5_GROUPS, S5_STATE
    a_im_base = jnp.pi * jnp.arange(P, dtype=f32)[None, None, :]
    return {
        "x_prompt": nrm((BATCH, SEQ, D), 1.0),
        "x_sample": nrm((DEC_BATCH, DEC_SEQ, D), 1.0),
        "c_prompt": nrm((BATCH, D), 1.0),
        "c_sample": nrm((DEC_BATCH, D), 1.0),
        "state_gla": nrm((DEC_BATCH, N_EVEN, GLA_HEADS, GLA_DK, GLA_DV), 1.0),
        "state_conv": nrm((DEC_BATCH, N_EVEN, CONV_K - 1, CONV_CH), 1.0),
        "state_s5_re": nrm((DEC_BATCH, N_ODD, G, P), 0.1),
        "state_s5_im": nrm((DEC_BATCH, N_ODD, G, P), 0.1),
        "w_ada": nrm((DEPTH, D, 6 * D), 0.5 * D ** -0.5),
        "b_ada": nrm((DEPTH, 6 * D), 0.02),
        "g_norm": 1.0 + nrm((DEPTH, 4, D), 0.05),
        "w_in0": nrm((N_EVEN, D, IN0_DIM), D ** -0.5),
        "w_gate_up": nrm((N_EVEN, GLA_LOWRANK, GLA_HEADS * GLA_DK), GLA_LOWRANK ** -0.5),
        "b_gate_up": nrm((N_EVEN, GLA_HEADS * GLA_DK), 0.1),
        "g_head_norm": 1.0 + nrm((N_EVEN, GLA_DV), 0.05),
        "w_conv": nrm((N_EVEN, CONV_K, CONV_CH), CONV_K ** -0.5),
        "w_out0": nrm((N_EVEN, MIX_WIDTH, D), MIX_WIDTH ** -0.5),
        "w_ffn_gate": nrm((N_EVEN, D, D_FF), D ** -0.5),
        "w_ffn_up": nrm((N_EVEN, D, D_FF), D ** -0.5),
        "w_ffn_down": nrm((N_EVEN, D_FF, D), D_FF ** -0.5),
        "w_in1": nrm((N_ODD, D, S5_WIDTH), D ** -0.5),
        "s5_a_re": -0.5 + nrm((N_ODD, G, P), 0.01),
        "s5_a_im": a_im_base + nrm((N_ODD, G, P), 0.01),
        "s5_log_dt": jax.random.uniform(next(ks), (N_ODD, G), f32, math.log(1e-3), math.log(1e-1)),
        "s5_b_re": nrm((N_ODD, G, P, S5_GROUP), (2 * S5_GROUP) ** -0.5),
        "s5_b_im": nrm((N_ODD, G, P, S5_GROUP), (2 * S5_GROUP) ** -0.5),
        "s5_c_re": nrm((N_ODD, G, S5_GROUP, P), (2 * P) ** -0.5),
        "s5_c_im": nrm((N_ODD, G, S5_GROUP, P), (2 * P) ** -0.5),
        "s5_d": nrm((N_ODD, G, S5_GROUP), 1.0),
        "w_glu": nrm((N_ODD, S5_WIDTH, 2 * D), S5_WIDTH ** -0.5),
        "w_router": nrm((N_ODD, D, N_EXPERTS), D ** -0.5),
        "w_exp_gate": nrm((N_ODD, N_EXPERTS, D, D_FF_EXPERT), D ** -0.5),
        "w_exp_up": nrm((N_ODD, N_EXPERTS, D, D_FF_EXPERT), D ** -0.5),
        "w_exp_down": nrm((N_ODD, N_EXPERTS, D_FF_EXPERT, D), D_FF_EXPERT ** -0.5),
    }


def reference(x_prompt, x_sample, c_prompt, c_sample, state_gla, state_conv, state_s5_re,
              state_s5_im, w_ada, b_ada, g_norm, w_in0, w_gate_up, b_gate_up, g_head_norm,
              w_conv, w_out0, w_ffn_gate, w_ffn_up, w_ffn_down, w_in1, s5_a_re, s5_a_im,
              s5_log_dt, s5_b_re, s5_b_im, s5_c_re, s5_c_im, s5_d, w_glu, w_router,
              w_exp_gate, w_exp_up, w_exp_down):
    f32 = jnp.float32
    params = (w_ada, b_ada, g_norm, w_in0, w_gate_up, b_gate_up, g_head_norm, w_conv, w_out0,
              w_ffn_gate, w_ffn_up, w_ffn_down, w_in1, s5_a_re, s5_a_im, s5_log_dt, s5_b_re,
              s5_b_im, s5_c_re, s5_c_im, s5_d, w_glu, w_router, w_exp_gate, w_exp_up, w_exp_down)
    z_gla = jnp.zeros((BATCH, N_EVEN, GLA_HEADS, GLA_DK, GLA_DV), f32)
    z_conv = jnp.zeros((BATCH, N_EVEN, CONV_K - 1, CONV_CH), x_prompt.dtype)
    z_s5 = jnp.zeros((BATCH, N_ODD, S5_GROUPS, S5_STATE), f32)
    y_p, gla_p, conv_p, re_p, im_p = trunk(x_prompt, c_prompt, z_gla, z_conv, z_s5, z_s5, params)
    y_s, gla_s, conv_s, re_s, im_s = trunk(x_sample, c_sample, state_gla, state_conv,
                                           state_s5_re, state_s5_im, params)
    return (y_p, y_s, gla_p, conv_p, re_p, im_p, gla_s, conv_s, re_s, im_s)
```

```python
import functools
import math

import jax
import jax.numpy as jnp
from jax import lax
from jax.experimental import pallas as pl
from jax.experimental.pallas import tpu as pltpu

F32 = jnp.float32
BF16 = jnp.bfloat16
EPS = 1e-6
LANES = 128
GLA_CHUNK = 64
GATE_NORMALIZER = 16.0
N_MOD = 6
S5_GROUP = 16
S5_CHUNK_GROUPS = 8


def _dot(a, b, dims=None):
    if dims is None:
        return jnp.dot(a, b, preferred_element_type=F32)
    return lax.dot_general(a, b, (dims, ((), ())), preferred_element_type=F32)


def _split(a):
    a = a.astype(F32)
    hi = a.astype(BF16)
    lo = (a - hi.astype(F32)).astype(BF16)
    return hi, lo


def _mm(a, w, prec):
    if prec == "bf16":
        return _dot(a.astype(BF16), w.astype(BF16))
    a_hi, a_lo = _split(a)
    w_hi, w_lo = _split(w)
    return _dot(a_hi, w_hi) + (_dot(a_lo, w_hi) + _dot(a_hi, w_lo))


def _silu(x):
    return x * jax.nn.sigmoid(x)


def _gelu_tanh(x):
    return 0.5 * x * (1.0 + jnp.tanh(math.sqrt(2.0 / math.pi) * (x + 0.044715 * (x * x * x))))


def _log_sigmoid(x):
    return -(jnp.maximum(-x, 0.0) + jnp.log1p(jnp.exp(-jnp.abs(x))))


def _rms(x):
    return x * lax.rsqrt(jnp.mean(x * x, axis=-1, keepdims=True) + EPS)


def _mod_row(ref):
    return ref[0] if len(ref.shape) == 3 else ref[...]


def _normmod(x, g, sc, sh):
    return (_rms(x) * g) * (1.0 + sc) + sh


def _mod_spec(per_token, tm, d, col, tiles_per_batch):
    if per_token:
        return pl.BlockSpec((tm, d), lambda i, *_: (i, col))
    return pl.BlockSpec((1, 1, d), lambda i, *_: (i // tiles_per_batch, 0, col))


def _params(sem, vmem_mb=None):
    kw = dict(dimension_semantics=sem)
    if vmem_mb is not None:
        kw["vmem_limit_bytes"] = vmem_mb << 20
    return pltpu.CompilerParams(**kw)


def _ada_kernel(c_ref, w_ref, b_ref, o_ref):
    o_ref[0] = _mm(_silu(c_ref[...]), w_ref[0], "x3") + b_ref[0]


def _ada(c_all, w_ada, b_ada):
    depth, d, n6 = w_ada.shape
    rows = c_all.shape[0]
    return pl.pallas_call(
        _ada_kernel,
        grid=(depth, n6 // d),
        in_specs=[pl.BlockSpec((rows, d), lambda l, j: (0, 0)),
                  pl.BlockSpec((1, d, d), lambda l, j: (l, 0, j)),
                  pl.BlockSpec((1, 1, d), lambda l, j: (l, 0, j))],
        out_specs=pl.BlockSpec((1, rows, d), lambda l, j: (l, 0, j)),
        out_shape=jax.ShapeDtypeStruct((depth, rows, n6), F32),
        compiler_params=_params(("parallel", "parallel")),
    )(c_all, w_ada, b_ada.reshape(depth, 1, n6))


def _inproj_kernel(x_ref, g_ref, sc_ref, sh_ref, w_ref, o_ref, *, prec):
    h = _normmod(x_ref[...], g_ref[...], _mod_row(sc_ref), _mod_row(sh_ref))
    o_ref[...] = _mm(h, w_ref[...], prec).astype(o_ref.dtype)


def _inproj(x2d, g, mod, sc_col, sh_col, w, *, tm, bn, prec, per_token, tiles_per_batch,
            out_dtype=F32, vmem_mb=None):
    n_tok, d = x2d.shape
    n_out = w.shape[1]
    return pl.pallas_call(
        functools.partial(_inproj_kernel, prec=prec),
        grid=(n_tok // tm, n_out // bn),
        in_specs=[pl.BlockSpec((tm, d), lambda i, j: (i, 0)),
                  pl.BlockSpec((1, d), lambda i, j: (0, 0)),
                  _mod_spec(per_token, tm, d, sc_col, tiles_per_batch),
                  _mod_spec(per_token, tm, d, sh_col, tiles_per_batch),
                  pl.BlockSpec((d, bn), lambda i, j: (0, j))],
        out_specs=pl.BlockSpec((tm, bn), lambda i, j: (i, j)),
        out_shape=jax.ShapeDtypeStruct((n_tok, n_out), out_dtype),
        compiler_params=_params(("parallel", "arbitrary"), vmem_mb),
    )(x2d, g, mod, mod, w)


def _outproj_kernel(a_ref, w_ref, x_ref, gt_ref, g_ref, o_ref, *, prec, glu):
    z = _mm(a_ref[...], w_ref[...], prec)
    if glu:
        d = z.shape[1] // 2
        z = z[:, :d] * jax.nn.sigmoid(z[:, d:])
    o_ref[...] = x_ref[...] + _mod_row(gt_ref) * (_rms(z) * g_ref[...])


def _outproj(a2d, w, x2d, mod, gt_col, g, *, tm, prec, glu, per_token, tiles_per_batch,
             vmem_mb=None):
    n_tok, d = x2d.shape
    k, n_out = w.shape
    return pl.pallas_call(
        functools.partial(_outproj_kernel, prec=prec, glu=glu),
        grid=(n_tok // tm,),
        in_specs=[pl.BlockSpec((tm, k), lambda i: (i, 0)),
                  pl.BlockSpec((k, n_out), lambda i: (0, 0)),
                  pl.BlockSpec((tm, d), lambda i: (i, 0)),
                  _mod_spec(per_token, tm, d, gt_col, tiles_per_batch),
                  pl.BlockSpec((1, d), lambda i: (0, 0))],
        out_specs=pl.BlockSpec((tm, d), lambda i: (i, 0)),
        out_shape=jax.ShapeDtypeStruct((n_tok, d), F32),
        compiler_params=_params(("parallel",), vmem_mb),
    )(a2d, w, x2d, mod, g)


def _ffn_kernel(*refs, prec, has_comb, n_e, n_c):
    if has_comb:
        (x_ref, g1_ref, sc_ref, sh_ref, comb_ref, wg_ref, wu_ref, wd_ref, gt_ref, g2_ref,
         o_ref, h_scr, acc_scr) = refs
    else:
        (x_ref, g1_ref, sc_ref, sh_ref, wg_ref, wu_ref, wd_ref, gt_ref, g2_ref,
         o_ref, h_scr, acc_scr) = refs
    e = pl.program_id(1)
    c = pl.program_id(2)

    @pl.when((e == 0) & (c == 0))
    def _():
        h = _normmod(x_ref[...], g1_ref[...], _mod_row(sc_ref), _mod_row(sh_ref))
        h_scr[...] = h.astype(h_scr.dtype)
        acc_scr[...] = jnp.zeros_like(acc_scr)

    h = h_scr[...]
    act = _silu(_mm(h, wg_ref[0], prec)) * _mm(h, wu_ref[0], prec)
    y = _mm(act, wd_ref[0], prec)
    if has_comb:
        comb = comb_ref[...]
        lane = lax.broadcasted_iota(jnp.int32, comb.shape, 1)
        y = jnp.sum(jnp.where(lane == e, comb, 0.0), axis=-1, keepdims=True) * y
    acc_scr[...] += y

    @pl.when((e == n_e - 1) & (c == n_c - 1))
    def _():
        o_ref[...] = x_ref[...] + _mod_row(gt_ref) * (_rms(acc_scr[...]) * g2_ref[...])


def _ffn(x2d, g1, mod, sc_col, sh_col, gt_col, g2, wg, wu, wd, comb, *, tm, fc, prec,
         per_token, tiles_per_batch, vmem_mb=None):
    n_tok, d = x2d.shape
    n_e, _, f = wg.shape
    n_c = f // fc
    has_comb = comb is not None
    tok = pl.BlockSpec((tm, d), lambda i, e, c: (i, 0))
    vec = pl.BlockSpec((1, d), lambda i, e, c: (0, 0))
    in_specs = [tok, vec,
                _mod_spec(per_token, tm, d, sc_col, tiles_per_batch),
                _mod_spec(per_token, tm, d, sh_col, tiles_per_batch)]
    args = [x2d, g1, mod, mod]
    if has_comb:
        in_specs.append(pl.BlockSpec((tm, LANES), lambda i, e, c: (i, 0)))
        args.append(comb)
    in_specs += [pl.BlockSpec((1, d, fc), lambda i, e, c: (e, 0, c)),
                 pl.BlockSpec((1, d, fc), lambda i, e, c: (e, 0, c)),
                 pl.BlockSpec((1, fc, d), lambda i, e, c: (e, c, 0)),
                 _mod_spec(per_token, tm, d, gt_col, tiles_per_batch), vec]
    args += [wg, wu, wd, mod, g2]
    h_dtype = BF16 if prec == "bf16" else F32
    return pl.pallas_call(
        functools.partial(_ffn_kernel, prec=prec, has_comb=has_comb, n_e=n_e, n_c=n_c),
        grid=(n_tok // tm, n_e, n_c),
        in_specs=in_specs,
        out_specs=tok,
        out_shape=jax.ShapeDtypeStruct((n_tok, d), F32),
        scratch_shapes=[pltpu.VMEM((tm, d), h_dtype), pltpu.VMEM((tm, d), F32)],
        compiler_params=_params(("parallel", "arbitrary", "arbitrary"), vmem_mb),
    )(*args)


def _router_kernel(x_ref, g_ref, sc_ref, sh_ref, wr_ref, comb_ref, *, n_experts):
    h = _normmod(x_ref[...], g_ref[...], _mod_row(sc_ref), _mod_row(sh_ref))
    logits = _mm(h, wr_ref[...], "x3")
    lane = lax.broadcasted_iota(jnp.int32, logits.shape, 1).astype(F32)
    neg = -jnp.inf
    l1 = jnp.where(lane < n_experts, logits, neg)
    m1 = jnp.max(l1, axis=-1, keepdims=True)
    i1 = jnp.min(jnp.where(l1 == m1, lane, float(LANES)), axis=-1, keepdims=True)
    l2 = jnp.where(lane == i1, neg, l1)
    m2 = jnp.max(l2, axis=-1, keepdims=True)
    i2 = jnp.min(jnp.where(l2 == m2, lane, float(LANES)), axis=-1, keepdims=True)
    e2 = jnp.exp(m2 - m1)
    den = 1.0 + e2
    comb_ref[...] = jnp.where(lane == i1, 1.0 / den, 0.0) + jnp.where(lane == i2, e2 / den, 0.0)


def _router(x2d, g, mod, sc_col, sh_col, wr_pad, *, tm, n_experts, per_token, tiles_per_batch):
    n_tok, d = x2d.shape
    return pl.pallas_call(
        functools.partial(_router_kernel, n_experts=n_experts),
        grid=(n_tok // tm,),
        in_specs=[pl.BlockSpec((tm, d), lambda i: (i, 0)),
                  pl.BlockSpec((1, d), lambda i: (0, 0)),
                  _mod_spec(per_token, tm, d, sc_col, tiles_per_batch),
                  _mod_spec(per_token, tm, d, sh_col, tiles_per_batch),
                  pl.BlockSpec((d, LANES), lambda i: (0, 0))],
        out_specs=pl.BlockSpec((tm, LANES), lambda i: (i, 0)),
        out_shape=jax.ShapeDtypeStruct((n_tok, LANES), F32),
        compiler_params=_params(("parallel",)),
    )(x2d, g, mod, mod, wr_pad)


def _gla_conv_kernel(p_ref, wgu_ref, bgu_ref, gh_ref, wc_ref, mixed_ref, st_ref, cst_ref,
                     s_scr, uext_scr, *, tt, n_t, heads, dk, dv, cw):
    ti = pl.program_id(1)
    hk = heads * dk
    hv = heads * dv
    o_q, o_k, o_v, o_og = 0, hk, 2 * hk, 2 * hk + hv
    o_ch = o_og + hv
    o_cb, o_cc, o_lr = o_ch + cw, o_ch + 2 * cw, o_ch + 3 * cw
    L = GLA_CHUNK

    @pl.when(ti == 0)
    def _():
        s_scr[...] = jnp.zeros_like(s_scr)
        uext_scr[0:8, :] = jnp.zeros((8, cw), F32)

    logit = _dot(p_ref[:, o_lr:o_lr + LANES].astype(BF16), wgu_ref[...]) + bgu_ref[...]
    logg = _log_sigmoid(logit) / GATE_NORMALIZER

    r = lax.broadcasted_iota(jnp.int32, (tt, tt), 0)
    c = lax.broadcasted_iota(jnp.int32, (tt, tt), 1)
    tri = jnp.where((r // L == c // L) & (c <= r), 1.0, 0.0).astype(BF16)
    g_hi = logg.astype(BF16)
    rem = logg - g_hi.astype(F32)
    g_mid = rem.astype(BF16)
    g_lo = (rem - g_mid.astype(F32)).astype(BF16)
    bc = _dot(tri, g_hi) + (_dot(tri, g_mid) + _dot(tri, g_lo))

    rl = lax.broadcasted_iota(jnp.int32, (L, L), 0)
    cl = lax.broadcasted_iota(jnp.int32, (L, L), 1)
    tril = cl <= rl
    lane = lax.broadcasted_iota(jnp.int32, (L, LANES), 1)
    heads_per_blk = LANES // dk
    nt_dims = ((1,), (1,))

    for ck in range(tt // L):
        rows = slice(L * ck, L * ck + L)
        b = bc[rows]
        bl = b[L - 1:L]
        q_in = (p_ref[rows, o_q:o_q + hk] * (dk ** -0.5)) * jnp.exp(b)
        k = p_ref[rows, o_k:o_k + hk]
        k_out = (k * jnp.exp(-b)).astype(BF16)
        k_dec = (k * jnp.exp(bl - b)).astype(BF16)
        dec = jnp.exp(bl)
        q_in = q_in.astype(BF16)
        for h in range(heads):
            blk = slice(LANES * (h // heads_per_blk), LANES * (h // heads_per_blk) + LANES)
            in_head = (lane // dk) == (h % heads_per_blk)
            qm = jnp.where(in_head, q_in[:, blk], jnp.zeros_like(q_in[:, blk]))
            att = _dot(qm, k_out[:, blk], nt_dims)
            att = jnp.where(tril, att, 0.0).astype(BF16)
            vh = p_ref[rows, o_v + dv * h:o_v + dv * h + dv].astype(BF16)
            s_t = s_scr[h]
            o = _dot(att, vh) + _dot(qm, s_t.astype(BF16), nt_dims)
            ds_t = _dot(vh, k_dec[:, blk], ((0,), (0,)))
            s_scr[h] = s_t * dec[:, blk] + ds_t
            og = p_ref[rows, o_og + dv * h:o_og + dv * h + dv]
            res = (_rms(o) * gh_ref[...]) * _silu(og)
            mixed_ref[rows, dv * h:dv * h + dv] = res.astype(mixed_ref.dtype)

    u = p_ref[:, o_cc:o_cc + cw] * p_ref[:, o_ch:o_ch + cw]
    uext_scr[8:8 + tt, :] = u
    y = (wc_ref[0:1, :] * uext_scr[6:6 + tt, :] + wc_ref[1:2, :] * uext_scr[7:7 + tt, :]
         + wc_ref[2:3, :] * u)
    mixed_ref[:, hv:hv + cw] = (p_ref[:, o_cb:o_cb + cw] * y).astype(mixed_ref.dtype)
    tail = uext_scr[tt + 6:tt + 8, :]
    uext_scr[6:8, :] = tail

    @pl.when(ti == n_t - 1)
    def _():
        st_ref[0] = s_scr[...]
        cst_ref[0] = tail


def _gla_conv_prompt(proj, wgu_pad, bgu, g_head, w_conv, *, bsz, t, tt, heads, dk, dv, cw):
    n_t = t // tt
    n_in = proj.shape[1]
    width = heads * dv + cw
    return pl.pallas_call(
        functools.partial(_gla_conv_kernel, tt=tt, n_t=n_t, heads=heads, dk=dk, dv=dv, cw=cw),
        grid=(bsz, n_t),
        in_specs=[pl.BlockSpec((tt, n_in), lambda b, i: (b * n_t + i, 0)),
                  pl.BlockSpec(wgu_pad.shape, lambda b, i: (0, 0)),
                  pl.BlockSpec(bgu.shape, lambda b, i: (0, 0)),
                  pl.BlockSpec(g_head.shape, lambda b, i: (0, 0)),
                  pl.BlockSpec(w_conv.shape, lambda b, i: (0, 0))],
        out_specs=[pl.BlockSpec((tt, width), lambda b, i: (b * n_t + i, 0)),
                   pl.BlockSpec((1, heads, dv, LANES), lambda b, i: (b, 0, 0, 0)),
                   pl.BlockSpec((1, 2, cw), lambda b, i: (b, 0, 0))],
        out_shape=[jax.ShapeDtypeStruct((bsz * t, width), BF16),
                   jax.ShapeDtypeStruct((bsz, heads, dv, LANES), F32),
                   jax.ShapeDtypeStruct((bsz, 2, cw), F32)],
        scratch_shapes=[pltpu.VMEM((heads, dv, LANES), F32), pltpu.VMEM((tt + 8, cw), F32)],
        compiler_params=_params(("parallel", "arbitrary"), 48),
    )(proj, wgu_pad, bgu, g_head, w_conv)


def _gla_conv_step_kernel(q_ref, k_ref, v_ref, og_ref, ch_ref, cb_ref, cc_ref, lr_ref,
                          wgu_ref, bgu_ref, gh_ref, wc_ref, s_ref, cbuf_ref,
                          o_ref, y_ref, sn_ref, cn_ref, a_scr, *, dk):
    logit = _mm(wgu_ref[...], lr_ref[...], "x3") + bgu_ref[...]
    a_scr[...] = jnp.exp(_log_sigmoid(logit) / GATE_NORMALIZER)
    v_t = v_ref[...]

    def body(d, acc):
        a = a_scr[pl.ds(d, 1), :]
        kd = k_ref[pl.ds(d, 1), :]
        qd = q_ref[pl.ds(d, 1), :]
        s_new = a * s_ref[0, d] + kd * v_t
        sn_ref[0, d] = s_new
        return acc + (qd * (dk ** -0.5)) * s_new

    o = lax.fori_loop(0, dk, body, jnp.zeros(v_t.shape, F32))
    o = o * lax.rsqrt(jnp.mean(o * o, axis=0, keepdims=True) + EPS) * gh_ref[...]
    o_ref[...] = o * _silu(og_ref[...])
    u = cc_ref[...] * ch_ref[...]
    y = wc_ref[0] * cbuf_ref[0] + wc_ref[1] * cbuf_ref[1] + wc_ref[2] * u
    y_ref[...] = cb_ref[...] * y
    cn_ref[0] = cbuf_ref[1]
    cn_ref[1] = u


def _gla_conv_step(proj_t, wgu_t_pad, bgu_col, gh_col, wc_col, s_t, cbuf_t, *, heads, dk, dv, cw):
    bsz = proj_t.shape[1]
    hk, hv = heads * dk, heads * dv
    cs = cw // heads
    assert dv == LANES and cs == LANES and dk * 2 == LANES
    o_k, o_v, o_og = hk, 2 * hk, 2 * hk + hv
    o_ch = o_og + hv
    o_cb, o_cc, o_lr = o_ch + cw, o_ch + 2 * cw, o_ch + 3 * cw
    row = lambda off, size: (lambda h: (off // size + h, 0))
    blk = lambda size, off: pl.BlockSpec((size, bsz), row(off, size))
    return pl.pallas_call(
        functools.partial(_gla_conv_step_kernel, dk=dk),
        grid=(heads,),
        in_specs=[blk(dk, 0), blk(dk, o_k), blk(dv, o_v), blk(dv, o_og),
                  blk(cs, o_ch), blk(cs, o_cb), blk(cs, o_cc),
                  pl.BlockSpec((LANES, bsz), lambda h: (o_lr // LANES, 0)),
                  pl.BlockSpec((dk, LANES), lambda h: (h, 0)),
                  pl.BlockSpec((dk, 1), lambda h: (h, 0)),
                  pl.BlockSpec((dv, 1), lambda h: (0, 0)),
                  pl.BlockSpec((3, cs, 1), lambda h: (0, h, 0)),
                  pl.BlockSpec((1, dk, dv, bsz), lambda h: (h, 0, 0, 0)),
                  pl.BlockSpec((2, cs, bsz), lambda h: (0, h, 0))],
        out_specs=[pl.BlockSpec((dv, bsz), lambda h: (h, 0)),
                   pl.BlockSpec((cs, bsz), lambda h: (h, 0)),
                   pl.BlockSpec((1, dk, dv, bsz), lambda h: (h, 0, 0, 0)),
                   pl.BlockSpec((2, cs, bsz), lambda h: (0, h, 0))],
        out_shape=[jax.ShapeDtypeStruct((hv, bsz), F32),
                   jax.ShapeDtypeStruct((cw, bsz), F32),
                   jax.ShapeDtypeStruct(s_t.shape, F32),
                   jax.ShapeDtypeStruct(cbuf_t.shape, F32)],
        scratch_shapes=[pltpu.VMEM((dk, bsz), F32)],
        compiler_params=_params(("parallel",)),
    )(proj_t, proj_t, proj_t, proj_t, proj_t, proj_t, proj_t, proj_t,
      wgu_t_pad, bgu_col, gh_col, wc_col, s_t, cbuf_t)


def _s5_scan_kernel(u_ref, bb_ref, cm_ref, abr_ref, abi_ref, d_ref, y_ref, sre_ref, sim_ref,
                    scr, xr_scr, xi_scr, *, tt, n_t, nb, half):
    ti = pl.program_id(1)

    @pl.when(ti == 0)
    def _():
        xr_scr[...] = jnp.zeros_like(xr_scr)
        xi_scr[...] = jnp.zeros_like(xi_scr)

    u = u_ref[...].reshape(nb * tt, LANES)
    bu = _dot(u.astype(BF16), bb_ref[0])
    n_blk = half // LANES
    for cb in range(2 * n_blk):
        scr[cb] = bu[:, cb * LANES:(cb + 1) * LANES]
    a_re = [jnp.broadcast_to(abr_ref[0][:, cb * LANES:(cb + 1) * LANES], (nb, LANES))
            for cb in range(n_blk)]
    a_im = [jnp.broadcast_to(abi_ref[0][:, cb * LANES:(cb + 1) * LANES], (nb, LANES))
            for cb in range(n_blk)]

    def body(t, carry):
        rows = pl.ds(t, nb, stride=tt)
        out = []
        for cb in range(n_blk):
            xr, xi = carry[2 * cb], carry[2 * cb + 1]
            nr = (a_re[cb] * xr - a_im[cb] * xi) + scr[cb, rows, :]
            ni = (a_re[cb] * xi + a_im[cb] * xr) + scr[n_blk + cb, rows, :]
            scr[cb, rows, :] = nr
            scr[n_blk + cb, rows, :] = ni
            out += [nr, ni]
        return tuple(out)

    init = []
    for cb in range(n_blk):
        init += [xr_scr[:, cb * LANES:(cb + 1) * LANES], xi_scr[:, cb * LANES:(cb + 1) * LANES]]
    fin = lax.fori_loop(0, tt, body, tuple(init))
    xr = jnp.concatenate([fin[2 * cb] for cb in range(n_blk)], axis=1)
    xi = jnp.concatenate([fin[2 * cb + 1] for cb in range(n_blk)], axis=1)
    xr_scr[...] = xr
    xi_scr[...] = xi
    xs = jnp.concatenate([scr[cb] for cb in range(2 * n_blk)], axis=1)
    y = _dot(xs.astype(BF16), cm_ref[0]) + d_ref[0] * u
    y_ref[...] = _gelu_tanh(y).reshape(nb, tt, LANES).astype(y_ref.dtype)

    @pl.when(ti == n_t - 1)
    def _():
        sre_ref[...] = xr
        sim_ref[...] = xi


def _s5_prompt(u3d, bb_blk, c_blk, ab_re, ab_im, d_blk, *, tt):
    nb, t, d = u3d.shape
    n_j = d // LANES
    n_t = t // tt
    half = bb_blk.shape[2] // 2
    return pl.pallas_call(
        functools.partial(_s5_scan_kernel, tt=tt, n_t=n_t, nb=nb, half=half),
        grid=(n_j, n_t),
        in_specs=[pl.BlockSpec((nb, tt, LANES), lambda j, i: (0, i, j)),
                  pl.BlockSpec((1, LANES, 2 * half), lambda j, i: (j, 0, 0)),
                  pl.BlockSpec((1, 2 * half, LANES), lambda j, i: (j, 0, 0)),
                  pl.BlockSpec((1, 1, half), lambda j, i: (j, 0, 0)),
                  pl.BlockSpec((1, 1, half), lambda j, i: (j, 0, 0)),
                  pl.BlockSpec((1, 1, LANES), lambda j, i: (j, 0, 0))],
        out_specs=[pl.BlockSpec((nb, tt, LANES), lambda j, i: (0, i, j)),
                   pl.BlockSpec((nb, half), lambda j, i: (0, j)),
                   pl.BlockSpec((nb, half), lambda j, i: (0, j))],
        out_shape=[jax.ShapeDtypeStruct((nb, t, d), BF16),
                   jax.ShapeDtypeStruct((nb, n_j * half), F32),
                   jax.ShapeDtypeStruct((nb, n_j * half), F32)],
        scratch_shapes=[pltpu.VMEM((2 * half // LANES, nb * tt, LANES), F32),
                        pltpu.VMEM((nb, half), F32), pltpu.VMEM((nb, half), F32)],
        compiler_params=_params(("parallel", "arbitrary")),
    )(u3d, bb_blk, c_blk, ab_re, ab_im, d_blk)


def _s5_step_kernel(u_ref, bb_ref, cm_ref, abr_ref, abi_ref, d_ref, sre_ref, sim_ref,
                    y_ref, nre_ref, nim_ref, *, half):
    u = u_ref[...]
    bu = _mm(u, bb_ref[0], "x3")
    ar, ai = abr_ref[0], abi_ref[0]
    sr, si = sre_ref[...], sim_ref[...]
    nr = (ar * sr - ai * si) + bu[:, 0:half]
    ni = (ar * si + ai * sr) + bu[:, half:2 * half]
    nre_ref[...] = nr
    nim_ref[...] = ni
    y = _mm(jnp.concatenate([nr, ni], axis=1), cm_ref[0], "x3") + d_ref[0] * u
    y_ref[...] = _gelu_tanh(y)


def _s5_step(u2d, bb_blk, c_blk, ab_re, ab_im, d_blk, s_re, s_im):
    bsz, d = u2d.shape
    n_j = d // LANES
    half = bb_blk.shape[2] // 2
    st = pl.BlockSpec((bsz, half), lambda j: (0, j))
    return pl.pallas_call(
        functools.partial(_s5_step_kernel, half=half),
        grid=(n_j,),
        in_specs=[pl.BlockSpec((bsz, LANES), lambda j: (0, j)),
                  pl.BlockSpec((1, LANES, 2 * half), lambda j: (j, 0, 0)),
                  pl.BlockSpec((1, 2 * half, LANES), lambda j: (j, 0, 0)),
                  pl.BlockSpec((1, 1, half), lambda j: (j, 0, 0)),
                  pl.BlockSpec((1, 1, half), lambda j: (j, 0, 0)),
                  pl.BlockSpec((1, 1, LANES), lambda j: (j, 0, 0)),
                  st, st],
        out_specs=[pl.BlockSpec((bsz, LANES), lambda j: (0, j)), st, st],
        out_shape=[jax.ShapeDtypeStruct((bsz, d), F32),
                   jax.ShapeDtypeStruct(s_re.shape, F32),
                   jax.ShapeDtypeStruct(s_im.shape, F32)],
        compiler_params=_params(("parallel",)),
    )(u2d, bb_blk, c_blk, ab_re, ab_im, d_blk, s_re, s_im)


def _s5_discretize(a_re, a_im, log_dt, b_re, b_im, c_re, c_im, d_skip):
    g, p = a_re.shape
    cg = S5_CHUNK_GROUPS
    n_j = g // cg
    dt = jnp.exp(log_dt)[:, None]
    mag = jnp.exp(dt * a_re)
    ab_re, ab_im = mag * jnp.cos(dt * a_im), mag * jnp.sin(dt * a_im)
    den = a_re * a_re + a_im * a_im
    nr, ni = ab_re - 1.0, ab_im
    f_re = (nr * a_re + ni * a_im) / den
    f_im = (ni * a_re - nr * a_im) / den
    bb_re = f_re[..., None] * b_re - f_im[..., None] * b_im
    bb_im = f_re[..., None] * b_im + f_im[..., None] * b_re
    eye = jnp.eye(cg, dtype=F32)

    def in_blk(m):
        m = m.reshape(n_j, cg, p, S5_GROUP)
        return jnp.einsum("jgpi,gh->jgihp", m, eye).reshape(n_j, cg * S5_GROUP, cg * p)

    def out_blk(m):
        m = m.reshape(n_j, cg, S5_GROUP, p)
        return jnp.einsum("jgip,gh->jgphi", m, eye).reshape(n_j, cg * p, cg * S5_GROUP)

    bb_blk = jnp.concatenate([in_blk(bb_re), in_blk(bb_im)], axis=2)
    c_blk = jnp.concatenate([out_blk(c_re), out_blk(-c_im)], axis=1)
    return (bb_blk, c_blk, ab_re.reshape(n_j, 1, cg * p), ab_im.reshape(n_j, 1, cg * p),
            d_skip.reshape(n_j, 1, cg * S5_GROUP))


def kernel(x_prompt, x_sample, c_prompt, c_sample, state_gla, state_conv, state_s5_re, state_s5_im, w_ada, b_ada, g_norm, w_in0, w_gate_up, b_gate_up, g_head_norm, w_conv, w_out0, w_ffn_gate, w_ffn_up, w_ffn_down, w_in1, s5_a_re, s5_a_im, s5_log_dt, s5_b_re, s5_b_im, s5_c_re, s5_c_im, s5_d, w_glu, w_router, w_exp_gate, w_exp_up, w_exp_down):
    bsz, t, d = x_prompt.shape
    bs = x_sample.shape[0]
    heads, dk, dv = state_gla.shape[2:]
    cw = state_conv.shape[3]
    lowrank = w_gate_up.shape[1]
    hk, hv = heads * dk, heads * dv
    n_experts = w_router.shape[2]
    g_s5, p_s5 = s5_a_re.shape[1:]

    mod = _ada(jnp.concatenate([c_prompt, c_sample], axis=0), w_ada, b_ada)
    mod_p = [mod[l, :bsz].reshape(bsz, 1, N_MOD * d) for l in range(2)]
    mod_s = [mod[l, bsz:] for l in range(2)]
    SH_M, SC_M, GT_M, SH_F, SC_F, GT_F = range(N_MOD)
    gn = lambda l, k: g_norm[l, k].reshape(1, d)

    w0 = w_in0[0]
    o_lr = 2 * hk + 2 * hv
    w0r = jnp.concatenate([w0[:, :o_lr], w0[:, o_lr + lowrank:], w0[:, o_lr:o_lr + lowrank],
                           jnp.zeros((d, LANES - lowrank), F32)], axis=1)
    wgu_pad = jnp.concatenate([w_gate_up[0], jnp.zeros((LANES - lowrank, hk), F32)], axis=0)
    bb_blk, c_blk, ab_re, ab_im, d_blk = _s5_discretize(
        s5_a_re[0], s5_a_im[0], s5_log_dt[0], s5_b_re[0], s5_b_im[0], s5_c_re[0], s5_c_im[0],
        s5_d[0])
    wr_pad = jnp.concatenate([w_router[0], jnp.zeros((d, LANES - n_experts), F32)], axis=1)
    bf = lambda a: a.astype(BF16)
    w_eg, w_eu, w_ed = bf(w_exp_gate[0]), bf(w_exp_up[0]), bf(w_exp_down[0])

    tm = 512
    tpb = t // tm
    pk = dict(per_token=False, tiles_per_batch=tpb)
    xp = x_prompt.reshape(bsz * t, d)
    proj = _inproj(xp, gn(0, 0), mod_p[0], SC_M, SH_M, bf(w0r), tm=tm, bn=w0r.shape[1],
                   prec="bf16", vmem_mb=48, **pk)
    mixed, st_t, cst = _gla_conv_prompt(
        proj, bf(wgu_pad), b_gate_up[0].reshape(1, hk), g_head_norm[0].reshape(1, dv), w_conv[0],
        bsz=bsz, t=t, tt=256, heads=heads, dk=dk, dv=dv, cw=cw)
    st_t = st_t.reshape(bsz, heads // 2, 2, dv, 2, dk)
    gla_p = jnp.stack([st_t[:, :, 0, :, 0, :], st_t[:, :, 1, :, 1, :]], axis=2)
    gla_p = jnp.swapaxes(gla_p.reshape(bsz, heads, dv, dk), 2, 3)
    x1 = _outproj(mixed, bf(w_out0[0]), xp, mod_p[0], GT_M, gn(0, 1), tm=tm, prec="bf16",
                  glu=False, **pk)
    x2 = _ffn(x1, gn(0, 2), mod_p[0], SC_F, SH_F, GT_F, gn(0, 3), bf(w_ffn_gate), bf(w_ffn_up),
              bf(w_ffn_down), None, tm=tm, fc=1408, prec="bf16", vmem_mb=48, **pk)
    u = _inproj(x2, gn(1, 0), mod_p[1], SC_M, SH_M, bf(w_in1[0]), tm=tm, bn=d, prec="bf16", **pk)
    yg, re_p, im_p = _s5_prompt(u.reshape(bsz, t, d), bf(bb_blk), bf(c_blk), ab_re, ab_im, d_blk,
                                tt=128)
    x3 = _outproj(yg.reshape(bsz * t, d), bf(w_glu[0]), x2, mod_p[1], GT_M, gn(1, 1), tm=tm,
                  prec="bf16", glu=True, vmem_mb=48, **pk)
    comb = _router(x3, gn(1, 2), mod_p[1], SC_F, SH_F, wr_pad, tm=tm, n_experts=n_experts, **pk)
    x4 = _ffn(x3, gn(1, 2), mod_p[1], SC_F, SH_F, GT_F, gn(1, 3), w_eg, w_eu, w_ed, comb,
              tm=tm, fc=896, prec="bf16", vmem_mb=48, **pk)

    sk = dict(per_token=True, tiles_per_batch=1)
    xs = x_sample.reshape(bs, d)
    proj_s = _inproj(xs, gn(0, 0), mod_s[0], SC_M, SH_M, w0r, tm=bs, bn=640, prec="x3", **sk)
    s_t = jnp.transpose(state_gla[:, 0], (1, 2, 3, 0))
    cbuf_t = jnp.transpose(state_conv[:, 0], (1, 2, 0))
    o_t, y_t, sn_t, cn_t = _gla_conv_step(
        proj_s.T, wgu_pad.T, b_gate_up[0].reshape(hk, 1), g_head_norm[0].reshape(dv, 1),
        w_conv[0].reshape(3, cw, 1), s_t, cbuf_t, heads=heads, dk=dk, dv=dv, cw=cw)
    mixed_s = jnp.concatenate([o_t, y_t], axis=0).T
    gla_s = jnp.transpose(sn_t, (3, 0, 1, 2))
    conv_s = jnp.transpose(cn_t, (2, 0, 1))
    x1s = _outproj(mixed_s, w_out0[0], xs, mod_s[0], GT_M, gn(0, 1), tm=bs, prec="x3", glu=False,
                   **sk)
    x2s = _ffn(x1s, gn(0, 2), mod_s[0], SC_F, SH_F, GT_F, gn(0, 3), w_ffn_gate, w_ffn_up,
               w_ffn_down, None, tm=bs, fc=256, prec="x3", **sk)
    us = _inproj(x2s, gn(1, 0), mod_s[1], SC_M, SH_M, w_in1[0], tm=bs, bn=256, prec="x3", **sk)
    ygs, re_s, im_s = _s5_step(us, bb_blk, c_blk, ab_re, ab_im, d_blk,
                               state_s5_re[:, 0].reshape(bs, g_s5 * p_s5),
                               state_s5_im[:, 0].reshape(bs, g_s5 * p_s5))
    x3s = _outproj(ygs, w_glu[0], x2s, mod_s[1], GT_M, gn(1, 1), tm=bs, prec="x3", glu=True,
                   vmem_mb=48, **sk)
    comb_s = _router(x3s, gn(1, 2), mod_s[1], SC_F, SH_F, wr_pad, tm=bs, n_experts=n_experts, **sk)
    x4s = _ffn(x3s, gn(1, 2), mod_s[1], SC_F, SH_F, GT_F, gn(1, 3), w_eg, w_eu, w_ed, comb_s,
               tm=bs, fc=896, prec="bf16", **sk)

    return (x4.reshape(bsz, t, d), x4s.reshape(bs, 1, d),
            gla_p[:, None], cst[:, None],
            re_p.reshape(bsz, 1, g_s5, p_s5), im_p.reshape(bsz, 1, g_s5, p_s5),
            gla_s[:, None], conv_s[:, None],
            re_s.reshape(bs, 1, g_s5, p_s5), im_s.reshape(bs, 1, g_s5, p_s5))
```

```python
import functools
import math

import jax
import jax.numpy as jnp
from jax import lax
from jax.experimental import pallas as pl
from jax.experimental.pallas import tpu as pltpu

F32 = jnp.float32
BF16 = jnp.bfloat16
EPS = 1e-6
LANES = 128
GLA_CHUNK = 64
GATE_NORMALIZER = 16.0
N_MOD = 6
S5_GROUP = 16
S5_CHUNK_GROUPS = 8


def _dot(a, b, dims=None):
    if dims is None:
        return jnp.dot(a, b, preferred_element_type=F32)
    return lax.dot_general(a, b, (dims, ((), ())), preferred_element_type=F32)


def _split(a):
    a = a.astype(F32)
    hi = a.astype(BF16)
    lo = (a - hi.astype(F32)).astype(BF16)
    return hi, lo


def _mm(a, w, prec):
    if prec == "bf16":
        return _dot(a.astype(BF16), w.astype(BF16))
    a_hi, a_lo = _split(a)
    w_hi, w_lo = _split(w)
    return _dot(a_hi, w_hi) + (_dot(a_lo, w_hi) + _dot(a_hi, w_lo))


def _silu(x):
    return x * jax.nn.sigmoid(x)


def _gelu_tanh(x):
    return 0.5 * x * (1.0 + jnp.tanh(math.sqrt(2.0 / math.pi) * (x + 0.044715 * (x * x * x))))


def _log_sigmoid(x):
    return -(jnp.maximum(-x, 0.0) + jnp.log1p(jnp.exp(-jnp.abs(x))))


def _rms(x):
    return x * lax.rsqrt(jnp.mean(x * x, axis=-1, keepdims=True) + EPS)


def _mod_row(ref):
    return ref[0] if len(ref.shape) == 3 else ref[...]


def _normmod(x, g, sc, sh):
    return (_rms(x) * g) * (1.0 + sc) + sh


def _mod_spec(per_token, tm, d, col, tiles_per_batch):
    if per_token:
        return pl.BlockSpec((tm, d), lambda i, *_: (i, col))
    return pl.BlockSpec((1, 1, d), lambda i, *_: (i // tiles_per_batch, 0, col))


def _params(sem, vmem_mb=None):
    kw = dict(dimension_semantics=sem)
    if vmem_mb is not None:
        kw["vmem_limit_bytes"] = vmem_mb << 20
    return pltpu.CompilerParams(**kw)


def _ada_kernel(c_ref, w_ref, b_ref, o_ref):
    o_ref[0] = _mm(_silu(c_ref[...]), w_ref[0], "x3") + b_ref[0]


def _ada(c_all, w_ada, b_ada):
    depth, d, n6 = w_ada.shape
    rows = c_all.shape[0]
    return pl.pallas_call(
        _ada_kernel,
        grid=(depth, n6 // d),
        in_specs=[pl.BlockSpec((rows, d), lambda l, j: (0, 0)),
                  pl.BlockSpec((1, d, d), lambda l, j: (l, 0, j)),
                  pl.BlockSpec((1, 1, d), lambda l, j: (l, 0, j))],
        out_specs=pl.BlockSpec((1, rows, d), lambda l, j: (l, 0, j)),
        out_shape=jax.ShapeDtypeStruct((depth, rows, n6), F32),
        compiler_params=_params(("parallel", "parallel")),
    )(c_all, w_ada, b_ada.reshape(depth, 1, n6))


def _inproj_kernel(x_ref, g_ref, sc_ref, sh_ref, w_ref, o_ref, *, prec):
    h = _normmod(x_ref[...], g_ref[...], _mod_row(sc_ref), _mod_row(sh_ref))
    o_ref[...] = _mm(h, w_ref[...], prec).astype(o_ref.dtype)


def _inproj(x2d, g, mod, sc_col, sh_col, w, *, tm, bn, prec, per_token, tiles_per_batch,
            out_dtype=F32, vmem_mb=None):
    n_tok, d = x2d.shape
    n_out = w.shape[1]
    return pl.pallas_call(
        functools.partial(_inproj_kernel, prec=prec),
        grid=(n_tok // tm, n_out // bn),
        in_specs=[pl.BlockSpec((tm, d), lambda i, j: (i, 0)),
                  pl.BlockSpec((1, d), lambda i, j: (0, 0)),
                  _mod_spec(per_token, tm, d, sc_col, tiles_per_batch),
                  _mod_spec(per_token, tm, d, sh_col, tiles_per_batch),
                  pl.BlockSpec((d, bn), lambda i, j: (0, j))],
        out_specs=pl.BlockSpec((tm, bn), lambda i, j: (i, j)),
        out_shape=jax.ShapeDtypeStruct((n_tok, n_out), out_dtype),
        compiler_params=_params(("parallel", "arbitrary"), vmem_mb),
    )(x2d, g, mod, mod, w)


def _outproj_kernel(a_ref, w_ref, x_ref, gt_ref, g_ref, o_ref, *, prec, glu):
    z = _mm(a_ref[...], w_ref[...], prec)
    if glu:
        d = z.shape[1] // 2
        z = z[:, :d] * jax.nn.sigmoid(z[:, d:])
    o_ref[...] = x_ref[...] + _mod_row(gt_ref) * (_rms(z) * g_ref[...])


def _outproj(a2d, w, x2d, mod, gt_col, g, *, tm, prec, glu, per_token, tiles_per_batch,
             vmem_mb=None):
    n_tok, d = x2d.shape
    k, n_out = w.shape
    return pl.pallas_call(
        functools.partial(_outproj_kernel, prec=prec, glu=glu),
        grid=(n_tok // tm,),
        in_specs=[pl.BlockSpec((tm, k), lambda i: (i, 0)),
                  pl.BlockSpec((k, n_out), lambda i: (0, 0)),
                  pl.BlockSpec((tm, d), lambda i: (i, 0)),
                  _mod_spec(per_token, tm, d, gt_col, tiles_per_batch),
                  pl.BlockSpec((1, d), lambda i: (0, 0))],
        out_specs=pl.BlockSpec((tm, d), lambda i: (i, 0)),
        out_shape=jax.ShapeDtypeStruct((n_tok, d), F32),
        compiler_params=_params(("parallel",), vmem_mb),
    )(a2d, w, x2d, mod, g)


def _ffn_kernel(*refs, prec, has_comb, n_e, n_c):
    if has_comb:
        (x_ref, g1_ref, sc_ref, sh_ref, comb_ref, wg_ref, wu_ref, wd_ref, gt_ref, g2_ref,
         o_ref, h_scr, acc_scr) = refs
    else:
        (x_ref, g1_ref, sc_ref, sh_ref, wg_ref, wu_ref, wd_ref, gt_ref, g2_ref,
         o_ref, h_scr, acc_scr) = refs
    e = pl.program_id(1)
    c = pl.program_id(2)

    @pl.when((e == 0) & (c == 0))
    def _():
        h = _normmod(x_ref[...], g1_ref[...], _mod_row(sc_ref), _mod_row(sh_ref))
        h_scr[...] = h.astype(h_scr.dtype)
        acc_scr[...] = jnp.zeros_like(acc_scr)

    h = h_scr[...]
    act = _silu(_mm(h, wg_ref[0], prec)) * _mm(h, wu_ref[0], prec)
    y = _mm(act, wd_ref[0], prec)
    if has_comb:
        comb = comb_ref[...]
        lane = lax.broadcasted_iota(jnp.int32, comb.shape, 1)
        y = jnp.sum(jnp.where(lane == e, comb, 0.0), axis=-1, keepdims=True) * y
    acc_scr[...] += y

    @pl.when((e == n_e - 1) & (c == n_c - 1))
    def _():
        o_ref[...] = x_ref[...] + _mod_row(gt_ref) * (_rms(acc_scr[...]) * g2_ref[...])


def _ffn(x2d, g1, mod, sc_col, sh_col, gt_col, g2, wg, wu, wd, comb, *, tm, fc, prec,
         per_token, tiles_per_batch, vmem_mb=None):
    n_tok, d = x2d.shape
    n_e, _, f = wg.shape
    n_c = f // fc
    has_comb = comb is not None
    tok = pl.BlockSpec((tm, d), lambda i, e, c: (i, 0))
    vec = pl.BlockSpec((1, d), lambda i, e, c: (0, 0))
    in_specs = [tok, vec,
                _mod_spec(per_token, tm, d, sc_col, tiles_per_batch),
                _mod_spec(per_token, tm, d, sh_col, tiles_per_batch)]
    args = [x2d, g1, mod, mod]
    if has_comb:
        in_specs.append(pl.BlockSpec((tm, LANES), lambda i, e, c: (i, 0)))
        args.append(comb)
    in_specs += [pl.BlockSpec((1, d, fc), lambda i, e, c: (e, 0, c)),
                 pl.BlockSpec((1, d, fc), lambda i, e, c: (e, 0, c)),
                 pl.BlockSpec((1, fc, d), lambda i, e, c: (e, c, 0)),
                 _mod_spec(per_token, tm, d, gt_col, tiles_per_batch), vec]
    args += [wg, wu, wd, mod, g2]
    h_dtype = BF16 if prec == "bf16" else F32
    return pl.pallas_call(
        functools.partial(_ffn_kernel, prec=prec, has_comb=has_comb, n_e=n_e, n_c=n_c),
        grid=(n_tok // tm, n_e, n_c),
        in_specs=in_specs,
        out_specs=tok,
        out_shape=jax.ShapeDtypeStruct((n_tok, d), F32),
        scratch_shapes=[pltpu.VMEM((tm, d), h_dtype), pltpu.VMEM((tm, d), F32)],
        compiler_params=_params(("parallel", "arbitrary", "arbitrary"), vmem_mb),
    )(*args)


def _router_kernel(x_ref, g_ref, sc_ref, sh_ref, wr_ref, comb_ref, hb_ref, *, n_experts):
    h = _normmod(x_ref[...], g_ref[...], _mod_row(sc_ref), _mod_row(sh_ref))
    hb_ref[...] = h.astype(hb_ref.dtype)
    logits = _mm(h, wr_ref[...], "x3")
    lane = lax.broadcasted_iota(jnp.int32, logits.shape, 1).astype(F32)
    neg = -jnp.inf
    l1 = jnp.where(lane < n_experts, logits, neg)
    m1 = jnp.max(l1, axis=-1, keepdims=True)
    i1 = jnp.min(jnp.where(l1 == m1, lane, float(LANES)), axis=-1, keepdims=True)
    l2 = jnp.where(lane == i1, neg, l1)
    m2 = jnp.max(l2, axis=-1, keepdims=True)
    i2 = jnp.min(jnp.where(l2 == m2, lane, float(LANES)), axis=-1, keepdims=True)
    e2 = jnp.exp(m2 - m1)
    den = 1.0 + e2
    comb_ref[...] = jnp.where(lane == i1, 1.0 / den, 0.0) + jnp.where(lane == i2, e2 / den, 0.0)


def _router(x2d, g, mod, sc_col, sh_col, wr_pad, *, tm, n_experts, per_token, tiles_per_batch):
    n_tok, d = x2d.shape
    return pl.pallas_call(
        functools.partial(_router_kernel, n_experts=n_experts),
        grid=(n_tok // tm,),
        in_specs=[pl.BlockSpec((tm, d), lambda i: (i, 0)),
                  pl.BlockSpec((1, d), lambda i: (0, 0)),
                  _mod_spec(per_token, tm, d, sc_col, tiles_per_batch),
                  _mod_spec(per_token, tm, d, sh_col, tiles_per_batch),
                  pl.BlockSpec((d, LANES), lambda i: (0, 0))],
        out_specs=[pl.BlockSpec((tm, LANES), lambda i: (i, 0)),
                   pl.BlockSpec((tm, d), lambda i: (i, 0))],
        out_shape=[jax.ShapeDtypeStruct((n_tok, LANES), F32),
                   jax.ShapeDtypeStruct((n_tok, d), BF16)],
        compiler_params=_params(("parallel",)),
    )(x2d, g, mod, mod, wr_pad)


MOE_TB = 256
MOE_SEG = 16
MOE_TM = 512


def _moe_block_rows(n_experts):
    return -(-(2 * MOE_TB + n_experts * (MOE_SEG - 1)) // LANES) * LANES


def _scatter_matrix(dest, comb, weighted, n_rows, n_experts):
    tb = dest.shape[0]
    lane = lax.broadcasted_iota(jnp.int32, (tb, n_rows), 1).astype(F32)
    m = jnp.zeros((tb, n_rows), F32)
    for e in range(n_experts):
        w = comb[:, e:e + 1]
        val = jnp.where(w > 0.0, w if weighted else 1.0, 0.0)
        m = m + jnp.where(lane == dest[:, e:e + 1], val, 0.0)
    return m


def _moe_sort_kernel(hb_ref, comb_ref, xb_ref, dest_ref, cnt_ref, off_ref, *, n_rows, n_experts):
    comb = comb_ref[...]
    tb = comb.shape[0]
    sel = jnp.where(comb > 0.0, 1.0, 0.0)
    r = lax.broadcasted_iota(jnp.int32, (tb, tb), 0)
    c = lax.broadcasted_iota(jnp.int32, (tb, tb), 1)
    rank = _dot(jnp.where(c < r, 1.0, 0.0).astype(BF16), sel.astype(BF16))
    cnt = jnp.sum(sel, axis=0, keepdims=True)
    cnt = jnp.floor((cnt + (MOE_SEG - 1)) / MOE_SEG) * MOE_SEG
    ru = lax.broadcasted_iota(jnp.int32, (LANES, LANES), 0)
    cu = lax.broadcasted_iota(jnp.int32, (LANES, LANES), 1)
    off = _dot(jnp.broadcast_to(cnt, (8, LANES)).astype(BF16),
               jnp.where(ru < cu, 1.0, 0.0).astype(BF16))[0:1]
    dest = off + rank
    dest_ref[...] = dest
    cnt_ref[0] = cnt
    off_ref[0] = off
    pt = _scatter_matrix(dest, comb, False, n_rows, n_experts).astype(BF16)
    xb_ref[0] = _dot(pt, hb_ref[...], ((0,), (0,))).astype(xb_ref.dtype)


def _moe_sort(hb, comb, *, n_experts):
    n_tok, d = hb.shape
    n_blk = n_tok // MOE_TB
    n_rows = _moe_block_rows(n_experts)
    return pl.pallas_call(
        functools.partial(_moe_sort_kernel, n_rows=n_rows, n_experts=n_experts),
        grid=(n_blk,),
        in_specs=[pl.BlockSpec((MOE_TB, d), lambda b: (b, 0)),
                  pl.BlockSpec((MOE_TB, LANES), lambda b: (b, 0))],
        out_specs=[pl.BlockSpec((1, n_rows, d), lambda b: (b, 0, 0)),
                   pl.BlockSpec((MOE_TB, LANES), lambda b: (b, 0)),
                   pl.BlockSpec((1, 1, LANES), lambda b: (b, 0, 0)),
                   pl.BlockSpec((1, 1, LANES), lambda b: (b, 0, 0))],
        out_shape=[jax.ShapeDtypeStruct((n_blk, n_rows, d), BF16),
                   jax.ShapeDtypeStruct((n_tok, LANES), F32),
                   jax.ShapeDtypeStruct((n_blk, 1, LANES), F32),
                   jax.ShapeDtypeStruct((n_blk, 1, LANES), F32)],
        compiler_params=_params(("parallel",)),
    )(hb, comb)


_SEG_SIZES = (256, 128, 64, 32, 16)


def _seg_copy_kernel(src_ref, dst_ref, cnt_ref, in_hbm, init_hbm, out_hbm, sems):
    del init_hbm
    n_seg = src_ref.shape[0]

    def copy(src, dst, size, k):
        return pltpu.make_async_copy(in_hbm.at[pl.ds(src, size)], out_hbm.at[pl.ds(dst, size)],
                                     sems.at[k])

    def body(s, started):
        src, dst, n = src_ref[s], dst_ref[s], cnt_ref[s]
        new = []
        for k, size in enumerate(_SEG_SIZES):
            has = (n & size) != 0

            @pl.when(has)
            def _(src=src, dst=dst, size=size, k=k):
                copy(pl.multiple_of(src, MOE_SEG), pl.multiple_of(dst, MOE_SEG), size, k).start()

            step = jnp.where(has, size, 0)
            src, dst = src + step, dst + step
            new.append(started[k] + jnp.where(has, 1, 0))
        return tuple(new)

    started = lax.fori_loop(0, n_seg, body, tuple(jnp.int32(0) for _ in _SEG_SIZES))
    for k, size in enumerate(_SEG_SIZES):
        def wait_one(i, carry, size=size, k=k):
            copy(0, 0, size, k).wait()
            return carry

        lax.fori_loop(0, started[k], wait_one, 0)


def _seg_copy(src, dst, cnt, rows_in, n_out_rows):
    d = rows_in.shape[1]
    init = jnp.zeros((n_out_rows, d), rows_in.dtype)
    any_spec = pl.BlockSpec(memory_space=pl.ANY)
    return pl.pallas_call(
        _seg_copy_kernel,
        grid_spec=pltpu.PrefetchScalarGridSpec(
            num_scalar_prefetch=3, grid=(1,),
            in_specs=[any_spec, any_spec], out_specs=any_spec,
            scratch_shapes=[pltpu.SemaphoreType.DMA((len(_SEG_SIZES),))]),
        out_shape=jax.ShapeDtypeStruct((n_out_rows, d), rows_in.dtype),
        input_output_aliases={4: 0},
        compiler_params=_params(("arbitrary",)),
    )(src, dst, cnt, rows_in, init)


def _experts_kernel(te_ref, nt_ref, x_ref, wg_ref, wu_ref, wd_ref, o_ref, acc_scr, *, n_c):
    del te_ref
    k = pl.program_id(0)
    c = pl.program_id(1)
    active = k < nt_ref[0]

    @pl.when(active)
    def _():
        h = x_ref[...]
        act = _silu(_dot(h, wg_ref[0])) * _dot(h, wu_ref[0])
        y = _dot(act.astype(BF16), wd_ref[0])

        @pl.when(c == 0)
        def _():
            acc_scr[...] = y

        @pl.when(c > 0)
        def _():
            acc_scr[...] += y

    @pl.when(c == n_c - 1)
    def _():
        o_ref[...] = jnp.where(active, acc_scr[...], 0.0).astype(o_ref.dtype)


def _experts(tile_expert, n_tiles, xs, wg, wu, wd, *, fc):
    n_rows, d = xs.shape
    n_c = wg.shape[2] // fc

    def w_map(k, c, te, nt):
        return (te[k], 0, jnp.where(k < nt[0], c, n_c - 1))

    def wd_map(k, c, te, nt):
        return (te[k], jnp.where(k < nt[0], c, n_c - 1), 0)

    return pl.pallas_call(
        functools.partial(_experts_kernel, n_c=n_c),
        grid_spec=pltpu.PrefetchScalarGridSpec(
            num_scalar_prefetch=2, grid=(n_rows // MOE_TM, n_c),
            in_specs=[pl.BlockSpec((MOE_TM, d), lambda k, c, te, nt: (k, 0)),
                      pl.BlockSpec((1, d, fc), w_map), pl.BlockSpec((1, d, fc), w_map),
                      pl.BlockSpec((1, fc, d), wd_map)],
            out_specs=pl.BlockSpec((MOE_TM, d), lambda k, c, te, nt: (k, 0)),
            scratch_shapes=[pltpu.VMEM((MOE_TM, d), F32)]),
        out_shape=jax.ShapeDtypeStruct((n_rows, d), BF16),
        compiler_params=_params(("arbitrary", "arbitrary"), 48),
    )(tile_expert, n_tiles, xs, wg, wu, wd)


def _moe_combine_kernel(yb_ref, comb_ref, dest_ref, x_ref, gt_ref, g_ref, o_ref, *, n_experts):
    yb = yb_ref[0]
    wt = _scatter_matrix(dest_ref[...], comb_ref[...], True, yb.shape[0], n_experts)
    w_hi, w_lo = _split(wt)
    y = _dot(w_hi, yb) + _dot(w_lo, yb)
    o_ref[...] = x_ref[...] + _mod_row(gt_ref) * (_rms(y) * g_ref[...])


def _moe_combine(yb, comb, dest, x2d, mod, gt_col, g, *, n_experts, per_token, tiles_per_batch):
    n_tok, d = x2d.shape
    n_rows = yb.shape[1]
    return pl.pallas_call(
        functools.partial(_moe_combine_kernel, n_experts=n_experts),
        grid=(n_tok // MOE_TB,),
        in_specs=[pl.BlockSpec((1, n_rows, d), lambda b: (b, 0, 0)),
                  pl.BlockSpec((MOE_TB, LANES), lambda b: (b, 0)),
                  pl.BlockSpec((MOE_TB, LANES), lambda b: (b, 0)),
                  pl.BlockSpec((MOE_TB, d), lambda b: (b, 0)),
                  _mod_spec(per_token, MOE_TB, d, gt_col, tiles_per_batch),
                  pl.BlockSpec((1, d), lambda b: (0, 0))],
        out_specs=pl.BlockSpec((MOE_TB, d), lambda b: (b, 0)),
        out_shape=jax.ShapeDtypeStruct((n_tok, d), F32),
        compiler_params=_params(("parallel",)),
    )(yb, comb, dest, x2d, mod, g)


def _moe_schedule(cnt, off, n_rows, n_tiles_max):
    n_blk, n_e = cnt.shape
    tot = jnp.sum(cnt, axis=0)
    grp = -(-tot // MOE_TM) * MOE_TM
    g_end = jnp.cumsum(grp)
    sorted_pos = (g_end - grp)[None, :] + (jnp.cumsum(cnt, axis=0) - cnt)
    block_pos = jnp.arange(n_blk, dtype=jnp.int32)[:, None] * n_rows + off
    n_tiles = g_end[-1] // MOE_TM
    first_row = jnp.arange(n_tiles_max, dtype=jnp.int32) * MOE_TM
    tile_expert = jnp.sum(first_row[:, None] >= g_end[None, :], axis=1)
    last = jnp.sum((n_tiles - 1) * MOE_TM >= g_end)
    tile_expert = jnp.minimum(tile_expert, last).astype(jnp.int32)
    flat = lambda a: a.reshape(-1).astype(jnp.int32)
    return flat(block_pos), flat(sorted_pos), flat(cnt), tile_expert, n_tiles.reshape(1).astype(jnp.int32)


def _gla_conv_kernel(p_ref, wgu_ref, bgu_ref, gh_ref, wc_ref, mixed_ref, st_ref, cst_ref,
                     s_scr, uext_scr, *, tt, n_t, heads, dk, dv, cw):
    ti = pl.program_id(1)
    hk = heads * dk
    hv = heads * dv
    o_q, o_k, o_v, o_og = 0, hk, 2 * hk, 2 * hk + hv
    o_ch = o_og + hv
    o_cb, o_cc, o_lr = o_ch + cw, o_ch + 2 * cw, o_ch + 3 * cw
    L = GLA_CHUNK

    @pl.when(ti == 0)
    def _():
        s_scr[...] = jnp.zeros_like(s_scr)
        uext_scr[0:8, :] = jnp.zeros((8, cw), F32)

    logit = _dot(p_ref[:, o_lr:o_lr + LANES].astype(BF16), wgu_ref[...]) + bgu_ref[...]
    logg = _log_sigmoid(logit) / GATE_NORMALIZER

    r = lax.broadcasted_iota(jnp.int32, (tt, tt), 0)
    c = lax.broadcasted_iota(jnp.int32, (tt, tt), 1)
    tri = jnp.where((r // L == c // L) & (c <= r), 1.0, 0.0).astype(BF16)
    g_hi = logg.astype(BF16)
    rem = logg - g_hi.astype(F32)
    g_mid = rem.astype(BF16)
    g_lo = (rem - g_mid.astype(F32)).astype(BF16)
    bc = _dot(tri, g_hi) + (_dot(tri, g_mid) + _dot(tri, g_lo))

    rl = lax.broadcasted_iota(jnp.int32, (L, L), 0)
    cl = lax.broadcasted_iota(jnp.int32, (L, L), 1)
    tril = cl <= rl
    lane = lax.broadcasted_iota(jnp.int32, (L, LANES), 1)
    heads_per_blk = LANES // dk
    nt_dims = ((1,), (1,))

    for ck in range(tt // L):
        rows = slice(L * ck, L * ck + L)
        b = bc[rows]
        bl = b[L - 1:L]
        q_in = (p_ref[rows, o_q:o_q + hk] * (dk ** -0.5)) * jnp.exp(b)
        k = p_ref[rows, o_k:o_k + hk]
        k_out = (k * jnp.exp(-b)).astype(BF16)
        k_dec = (k * jnp.exp(bl - b)).astype(BF16)
        dec = jnp.exp(bl)
        q_in = q_in.astype(BF16)
        for h in range(heads):
            blk = slice(LANES * (h // heads_per_blk), LANES * (h // heads_per_blk) + LANES)
            in_head = (lane // dk) == (h % heads_per_blk)
            qm = jnp.where(in_head, q_in[:, blk], jnp.zeros_like(q_in[:, blk]))
            att = _dot(qm, k_out[:, blk], nt_dims)
            att = jnp.where(tril, att, 0.0).astype(BF16)
            vh = p_ref[rows, o_v + dv * h:o_v + dv * h + dv].astype(BF16)
            s_t = s_scr[h]
            o = _dot(att, vh) + _dot(qm, s_t.astype(BF16), nt_dims)
            ds_t = _dot(vh, k_dec[:, blk], ((0,), (0,)))
            s_scr[h] = s_t * dec[:, blk] + ds_t
            og = p_ref[rows, o_og + dv * h:o_og + dv * h + dv]
            res = (_rms(o) * gh_ref[...]) * _silu(og)
            mixed_ref[rows, dv * h:dv * h + dv] = res.astype(mixed_ref.dtype)

    u = p_ref[:, o_cc:o_cc + cw] * p_ref[:, o_ch:o_ch + cw]
    uext_scr[8:8 + tt, :] = u
    y = (wc_ref[0:1, :] * uext_scr[6:6 + tt, :] + wc_ref[1:2, :] * uext_scr[7:7 + tt, :]
         + wc_ref[2:3, :] * u)
    mixed_ref[:, hv:hv + cw] = (p_ref[:, o_cb:o_cb + cw] * y).astype(mixed_ref.dtype)
    tail = uext_scr[tt + 6:tt + 8, :]
    uext_scr[6:8, :] = tail

    @pl.when(ti == n_t - 1)
    def _():
        st_ref[0] = s_scr[...]
        cst_ref[0] = tail


def _gla_conv_prompt(proj, wgu_pad, bgu, g_head, w_conv, *, bsz, t, tt, heads, dk, dv, cw):
    n_t = t // tt
    n_in = proj.shape[1]
    width = heads * dv + cw
    return pl.pallas_call(
        functools.partial(_gla_conv_kernel, tt=tt, n_t=n_t, heads=heads, dk=dk, dv=dv, cw=cw),
        grid=(bsz, n_t),
        in_specs=[pl.BlockSpec((tt, n_in), lambda b, i: (b * n_t + i, 0)),
                  pl.BlockSpec(wgu_pad.shape, lambda b, i: (0, 0)),
                  pl.BlockSpec(bgu.shape, lambda b, i: (0, 0)),
                  pl.BlockSpec(g_head.shape, lambda b, i: (0, 0)),
                  pl.BlockSpec(w_conv.shape, lambda b, i: (0, 0))],
        out_specs=[pl.BlockSpec((tt, width), lambda b, i: (b * n_t + i, 0)),
                   pl.BlockSpec((1, heads, dv, LANES), lambda b, i: (b, 0, 0, 0)),
                   pl.BlockSpec((1, 2, cw), lambda b, i: (b, 0, 0))],
        out_shape=[jax.ShapeDtypeStruct((bsz * t, width), BF16),
                   jax.ShapeDtypeStruct((bsz, heads, dv, LANES), F32),
                   jax.ShapeDtypeStruct((bsz, 2, cw), F32)],
        scratch_shapes=[pltpu.VMEM((heads, dv, LANES), F32), pltpu.VMEM((tt + 8, cw), F32)],
        compiler_params=_params(("parallel", "arbitrary"), 48),
    )(proj, wgu_pad, bgu, g_head, w_conv)


def _gla_conv_step_kernel(q_ref, k_ref, v_ref, og_ref, ch_ref, cb_ref, cc_ref, lr_ref,
                          wgu_ref, bgu_ref, gh_ref, wc_ref, s_ref, cbuf_ref,
                          o_ref, y_ref, sn_ref, cn_ref, a_scr, *, dk):
    logit = _mm(wgu_ref[...], lr_ref[...], "x3") + bgu_ref[...]
    a_scr[...] = jnp.exp(_log_sigmoid(logit) / GATE_NORMALIZER)
    v_t = v_ref[...]

    def body(d, acc):
        a = a_scr[pl.ds(d, 1), :]
        kd = k_ref[pl.ds(d, 1), :]
        qd = q_ref[pl.ds(d, 1), :]
        s_new = a * s_ref[0, d] + kd * v_t
        sn_ref[0, d] = s_new
        return acc + (qd * (dk ** -0.5)) * s_new

    o = lax.fori_loop(0, dk, body, jnp.zeros(v_t.shape, F32))
    o = o * lax.rsqrt(jnp.mean(o * o, axis=0, keepdims=True) + EPS) * gh_ref[...]
    o_ref[...] = o * _silu(og_ref[...])
    u = cc_ref[...] * ch_ref[...]
    y = wc_ref[0] * cbuf_ref[0] + wc_ref[1] * cbuf_ref[1] + wc_ref[2] * u
    y_ref[...] = cb_ref[...] * y
    cn_ref[0] = cbuf_ref[1]
    cn_ref[1] = u


def _gla_conv_step(proj_t, wgu_t_pad, bgu_col, gh_col, wc_col, s_t, cbuf_t, *, heads, dk, dv, cw):
    bsz = proj_t.shape[1]
    hk, hv = heads * dk, heads * dv
    cs = cw // heads
    assert dv == LANES and cs == LANES and dk * 2 == LANES
    o_k, o_v, o_og = hk, 2 * hk, 2 * hk + hv
    o_ch = o_og + hv
    o_cb, o_cc, o_lr = o_ch + cw, o_ch + 2 * cw, o_ch + 3 * cw
    row = lambda off, size: (lambda h: (off // size + h, 0))
    blk = lambda size, off: pl.BlockSpec((size, bsz), row(off, size))
    return pl.pallas_call(
        functools.partial(_gla_conv_step_kernel, dk=dk),
        grid=(heads,),
        in_specs=[blk(dk, 0), blk(dk, o_k), blk(dv, o_v), blk(dv, o_og),
                  blk(cs, o_ch), blk(cs, o_cb), blk(cs, o_cc),
                  pl.BlockSpec((LANES, bsz), lambda h: (o_lr // LANES, 0)),
                  pl.BlockSpec((dk, LANES), lambda h: (h, 0)),
                  pl.BlockSpec((dk, 1), lambda h: (h, 0)),
                  pl.BlockSpec((dv, 1), lambda h: (0, 0)),
                  pl.BlockSpec((3, cs, 1), lambda h: (0, h, 0)),
                  pl.BlockSpec((1, dk, dv, bsz), lambda h: (h, 0, 0, 0)),
                  pl.BlockSpec((2, cs, bsz), lambda h: (0, h, 0))],
        out_specs=[pl.BlockSpec((dv, bsz), lambda h: (h, 0)),
                   pl.BlockSpec((cs, bsz), lambda h: (h, 0)),
                   pl.BlockSpec((1, dk, dv, bsz), lambda h: (h, 0, 0, 0)),
                   pl.BlockSpec((2, cs, bsz), lambda h: (0, h, 0))],
        out_shape=[jax.ShapeDtypeStruct((hv, bsz), F32),
                   jax.ShapeDtypeStruct((cw, bsz), F32),
                   jax.ShapeDtypeStruct(s_t.shape, F32),
                   jax.ShapeDtypeStruct(cbuf_t.shape, F32)],
        scratch_shapes=[pltpu.VMEM((dk, bsz), F32)],
        compiler_params=_params(("parallel",)),
    )(proj_t, proj_t, proj_t, proj_t, proj_t, proj_t, proj_t, proj_t,
      wgu_t_pad, bgu_col, gh_col, wc_col, s_t, cbuf_t)


def _s5_scan_kernel(u_ref, bb_ref, cm_ref, abr_ref, abi_ref, d_ref, y_ref, sre_ref, sim_ref,
                    scr, xr_scr, xi_scr, *, tt, n_t, nb, half):
    ti = pl.program_id(1)

    @pl.when(ti == 0)
    def _():
        xr_scr[...] = jnp.zeros_like(xr_scr)
        xi_scr[...] = jnp.zeros_like(xi_scr)

    u = u_ref[...].reshape(nb * tt, LANES)
    bu = _dot(u.astype(BF16), bb_ref[0])
    n_blk = half // LANES
    for cb in range(2 * n_blk):
        scr[cb] = bu[:, cb * LANES:(cb + 1) * LANES]
    a_re = [jnp.broadcast_to(abr_ref[0][:, cb * LANES:(cb + 1) * LANES], (nb, LANES))
            for cb in range(n_blk)]
    a_im = [jnp.broadcast_to(abi_ref[0][:, cb * LANES:(cb + 1) * LANES], (nb, LANES))
            for cb in range(n_blk)]

    def body(t, carry):
        rows = pl.ds(t, nb, stride=tt)
        out = []
        for cb in range(n_blk):
            xr, xi = carry[2 * cb], carry[2 * cb + 1]
            nr = (a_re[cb] * xr - a_im[cb] * xi) + scr[cb, rows, :]
            ni = (a_re[cb] * xi + a_im[cb] * xr) + scr[n_blk + cb, rows, :]
            scr[cb, rows, :] = nr
            scr[n_blk + cb, rows, :] = ni
            out += [nr, ni]
        return tuple(out)

    init = []
    for cb in range(n_blk):
        init += [xr_scr[:, cb * LANES:(cb + 1) * LANES], xi_scr[:, cb * LANES:(cb + 1) * LANES]]
    fin = lax.fori_loop(0, tt, body, tuple(init))
    xr = jnp.concatenate([fin[2 * cb] for cb in range(n_blk)], axis=1)
    xi = jnp.concatenate([fin[2 * cb + 1] for cb in range(n_blk)], axis=1)
    xr_scr[...] = xr
    xi_scr[...] = xi
    xs = jnp.concatenate([scr[cb] for cb in range(2 * n_blk)], axis=1)
    y = _dot(xs.astype(BF16), cm_ref[0]) + d_ref[0] * u
    y_ref[...] = _gelu_tanh(y).reshape(nb, tt, LANES).astype(y_ref.dtype)

    @pl.when(ti == n_t - 1)
    def _():
        sre_ref[...] = xr
        sim_ref[...] = xi


def _s5_prompt(u3d, bb_blk, c_blk, ab_re, ab_im, d_blk, *, tt):
    nb, t, d = u3d.shape
    n_j = d // LANES
    n_t = t // tt
    half = bb_blk.shape[2] // 2
    return pl.pallas_call(
        functools.partial(_s5_scan_kernel, tt=tt, n_t=n_t, nb=nb, half=half),
        grid=(n_j, n_t),
        in_specs=[pl.BlockSpec((nb, tt, LANES), lambda j, i: (0, i, j)),
                  pl.BlockSpec((1, LANES, 2 * half), lambda j, i: (j, 0, 0)),
                  pl.BlockSpec((1, 2 * half, LANES), lambda j, i: (j, 0, 0)),
                  pl.BlockSpec((1, 1, half), lambda j, i: (j, 0, 0)),
                  pl.BlockSpec((1, 1, half), lambda j, i: (j, 0, 0)),
                  pl.BlockSpec((1, 1, LANES), lambda j, i: (j, 0, 0))],
        out_specs=[pl.BlockSpec((nb, tt, LANES), lambda j, i: (0, i, j)),
                   pl.BlockSpec((nb, half), lambda j, i: (0, j)),
                   pl.BlockSpec((nb, half), lambda j, i: (0, j))],
        out_shape=[jax.ShapeDtypeStruct((nb, t, d), BF16),
                   jax.ShapeDtypeStruct((nb, n_j * half), F32),
                   jax.ShapeDtypeStruct((nb, n_j * half), F32)],
        scratch_shapes=[pltpu.VMEM((2 * half // LANES, nb * tt, LANES), F32),
                        pltpu.VMEM((nb, half), F32), pltpu.VMEM((nb, half), F32)],
        compiler_params=_params(("parallel", "arbitrary")),
    )(u3d, bb_blk, c_blk, ab_re, ab_im, d_blk)


def _s5_step_kernel(u_ref, bb_ref, cm_ref, abr_ref, abi_ref, d_ref, sre_ref, sim_ref,
                    y_ref, nre_ref, nim_ref, *, half):
    u = u_ref[...]
    bu = _mm(u, bb_ref[0], "x3")
    ar, ai = abr_ref[0], abi_ref[0]
    sr, si = sre_ref[...], sim_ref[...]
    nr = (ar * sr - ai * si) + bu[:, 0:half]
    ni = (ar * si + ai * sr) + bu[:, half:2 * half]
    nre_ref[...] = nr
    nim_ref[...] = ni
    y = _mm(jnp.concatenate([nr, ni], axis=1), cm_ref[0], "x3") + d_ref[0] * u
    y_ref[...] = _gelu_tanh(y)


def _s5_step(u2d, bb_blk, c_blk, ab_re, ab_im, d_blk, s_re, s_im):
    bsz, d = u2d.shape
    n_j = d // LANES
    half = bb_blk.shape[2] // 2
    st = pl.BlockSpec((bsz, half), lambda j: (0, j))
    return pl.pallas_call(
        functools.partial(_s5_step_kernel, half=half),
        grid=(n_j,),
        in_specs=[pl.BlockSpec((bsz, LANES), lambda j: (0, j)),
                  pl.BlockSpec((1, LANES, 2 * half), lambda j: (j, 0, 0)),
                  pl.BlockSpec((1, 2 * half, LANES), lambda j: (j, 0, 0)),
                  pl.BlockSpec((1, 1, half), lambda j: (j, 0, 0)),
                  pl.BlockSpec((1, 1, half), lambda j: (j, 0, 0)),
                  pl.BlockSpec((1, 1, LANES), lambda j: (j, 0, 0)),
                  st, st],
        out_specs=[pl.BlockSpec((bsz, LANES), lambda j: (0, j)), st, st],
        out_shape=[jax.ShapeDtypeStruct((bsz, d), F32),
                   jax.ShapeDtypeStruct(s_re.shape, F32),
                   jax.ShapeDtypeStruct(s_im.shape, F32)],
        compiler_params=_params(("parallel",)),
    )(u2d, bb_blk, c_blk, ab_re, ab_im, d_blk, s_re, s_im)


def _s5_discretize(a_re, a_im, log_dt, b_re, b_im, c_re, c_im, d_skip):
    g, p = a_re.shape
    cg = S5_CHUNK_GROUPS
    n_j = g // cg
    dt = jnp.exp(log_dt)[:, None]
    mag = jnp.exp(dt * a_re)
    ab_re, ab_im = mag * jnp.cos(dt * a_im), mag * jnp.sin(dt * a_im)
    den = a_re * a_re + a_im * a_im
    nr, ni = ab_re - 1.0, ab_im
    f_re = (nr * a_re + ni * a_im) / den
    f_im = (ni * a_re - nr * a_im) / den
    bb_re = f_re[..., None] * b_re - f_im[..., None] * b_im
    bb_im = f_re[..., None] * b_im + f_im[..., None] * b_re
    eye = jnp.eye(cg, dtype=F32)

    def in_blk(m):
        m = m.reshape(n_j, cg, p, S5_GROUP)
        return jnp.einsum("jgpi,gh->jgihp", m, eye).reshape(n_j, cg * S5_GROUP, cg * p)

    def out_blk(m):
        m = m.reshape(n_j, cg, S5_GROUP, p)
        return jnp.einsum("jgip,gh->jgphi", m, eye).reshape(n_j, cg * p, cg * S5_GROUP)

    bb_blk = jnp.concatenate([in_blk(bb_re), in_blk(bb_im)], axis=2)
    c_blk = jnp.concatenate([out_blk(c_re), out_blk(-c_im)], axis=1)
    return (bb_blk, c_blk, ab_re.reshape(n_j, 1, cg * p), ab_im.reshape(n_j, 1, cg * p),
            d_skip.reshape(n_j, 1, cg * S5_GROUP))


def kernel(x_prompt, x_sample, c_prompt, c_sample, state_gla, state_conv, state_s5_re, state_s5_im, w_ada, b_ada, g_norm, w_in0, w_gate_up, b_gate_up, g_head_norm, w_conv, w_out0, w_ffn_gate, w_ffn_up, w_ffn_down, w_in1, s5_a_re, s5_a_im, s5_log_dt, s5_b_re, s5_b_im, s5_c_re, s5_c_im, s5_d, w_glu, w_router, w_exp_gate, w_exp_up, w_exp_down):
    bsz, t, d = x_prompt.shape
    bs = x_sample.shape[0]
    heads, dk, dv = state_gla.shape[2:]
    cw = state_conv.shape[3]
    lowrank = w_gate_up.shape[1]
    hk, hv = heads * dk, heads * dv
    n_experts = w_router.shape[2]
    g_s5, p_s5 = s5_a_re.shape[1:]

    mod = _ada(jnp.concatenate([c_prompt, c_sample], axis=0), w_ada, b_ada)
    mod_p = [mod[l, :bsz].reshape(bsz, 1, N_MOD * d) for l in range(2)]
    mod_s = [mod[l, bsz:] for l in range(2)]
    SH_M, SC_M, GT_M, SH_F, SC_F, GT_F = range(N_MOD)
    gn = lambda l, k: g_norm[l, k].reshape(1, d)

    w0 = w_in0[0]
    o_lr = 2 * hk + 2 * hv
    w0r = jnp.concatenate([w0[:, :o_lr], w0[:, o_lr + lowrank:], w0[:, o_lr:o_lr + lowrank],
                           jnp.zeros((d, LANES - lowrank), F32)], axis=1)
    wgu_pad = jnp.concatenate([w_gate_up[0], jnp.zeros((LANES - lowrank, hk), F32)], axis=0)
    bb_blk, c_blk, ab_re, ab_im, d_blk = _s5_discretize(
        s5_a_re[0], s5_a_im[0], s5_log_dt[0], s5_b_re[0], s5_b_im[0], s5_c_re[0], s5_c_im[0],
        s5_d[0])
    wr_pad = jnp.concatenate([w_router[0], jnp.zeros((d, LANES - n_experts), F32)], axis=1)
    bf = lambda a: a.astype(BF16)
    w_eg, w_eu, w_ed = bf(w_exp_gate[0]), bf(w_exp_up[0]), bf(w_exp_down[0])

    tm = 512
    tpb = t // tm
    pk = dict(per_token=False, tiles_per_batch=tpb)
    xp = x_prompt.reshape(bsz * t, d)
    proj = _inproj(xp, gn(0, 0), mod_p[0], SC_M, SH_M, bf(w0r), tm=tm, bn=w0r.shape[1],
                   prec="bf16", vmem_mb=48, **pk)
    mixed, st_t, cst = _gla_conv_prompt(
        proj, bf(wgu_pad), b_gate_up[0].reshape(1, hk), g_head_norm[0].reshape(1, dv), w_conv[0],
        bsz=bsz, t=t, tt=256, heads=heads, dk=dk, dv=dv, cw=cw)
    st_t = st_t.reshape(bsz, heads // 2, 2, dv, 2, dk)
    gla_p = jnp.stack([st_t[:, :, 0, :, 0, :], st_t[:, :, 1, :, 1, :]], axis=2)
    gla_p = jnp.swapaxes(gla_p.reshape(bsz, heads, dv, dk), 2, 3)
    x1 = _outproj(mixed, bf(w_out0[0]), xp, mod_p[0], GT_M, gn(0, 1), tm=tm, prec="bf16",
                  glu=False, **pk)
    x2 = _ffn(x1, gn(0, 2), mod_p[0], SC_F, SH_F, GT_F, gn(0, 3), bf(w_ffn_gate), bf(w_ffn_up),
              bf(w_ffn_down), None, tm=tm, fc=1408, prec="bf16", vmem_mb=48, **pk)
    u = _inproj(x2, gn(1, 0), mod_p[1], SC_M, SH_M, bf(w_in1[0]), tm=tm, bn=d, prec="bf16", **pk)
    yg, re_p, im_p = _s5_prompt(u.reshape(bsz, t, d), bf(bb_blk), bf(c_blk), ab_re, ab_im, d_blk,
                                tt=128)
    x3 = _outproj(yg.reshape(bsz * t, d), bf(w_glu[0]), x2, mod_p[1], GT_M, gn(1, 1), tm=tm,
                  prec="bf16", glu=True, vmem_mb=48, **pk)
    comb, hb = _router(x3, gn(1, 2), mod_p[1], SC_F, SH_F, wr_pad, tm=tm, n_experts=n_experts,
                       **pk)

    sk = dict(per_token=True, tiles_per_batch=1)
    xs = x_sample.reshape(bs, d)
    proj_s = _inproj(xs, gn(0, 0), mod_s[0], SC_M, SH_M, w0r, tm=bs, bn=640, prec="x3", **sk)
    s_t = jnp.transpose(state_gla[:, 0], (1, 2, 3, 0))
    cbuf_t = jnp.transpose(state_conv[:, 0], (1, 2, 0))
    o_t, y_t, sn_t, cn_t = _gla_conv_step(
        proj_s.T, wgu_pad.T, b_gate_up[0].reshape(hk, 1), g_head_norm[0].reshape(dv, 1),
        w_conv[0].reshape(3, cw, 1), s_t, cbuf_t, heads=heads, dk=dk, dv=dv, cw=cw)
    mixed_s = jnp.concatenate([o_t, y_t], axis=0).T
    gla_s = jnp.transpose(sn_t, (3, 0, 1, 2))
    conv_s = jnp.transpose(cn_t, (2, 0, 1))
    x1s = _outproj(mixed_s, w_out0[0], xs, mod_s[0], GT_M, gn(0, 1), tm=bs, prec="x3", glu=False,
                   **sk)
    x2s = _ffn(x1s, gn(0, 2), mod_s[0], SC_F, SH_F, GT_F, gn(0, 3), w_ffn_gate, w_ffn_up,
               w_ffn_down, None, tm=bs, fc=256, prec="x3", **sk)
    us = _inproj(x2s, gn(1, 0), mod_s[1], SC_M, SH_M, w_in1[0], tm=bs, bn=256, prec="x3", **sk)
    ygs, re_s, im_s = _s5_step(us, bb_blk, c_blk, ab_re, ab_im, d_blk,
                               state_s5_re[:, 0].reshape(bs, g_s5 * p_s5),
                               state_s5_im[:, 0].reshape(bs, g_s5 * p_s5))
    x3s = _outproj(ygs, w_glu[0], x2s, mod_s[1], GT_M, gn(1, 1), tm=bs, prec="x3", glu=True,
                   vmem_mb=48, **sk)
    comb_s, hb_s = _router(x3s, gn(1, 2), mod_s[1], SC_F, SH_F, wr_pad, tm=bs, n_experts=n_experts,
                           **sk)

    n_p = bsz * t
    pad = -(bs) % MOE_TB
    n_all = n_p + bs + pad
    n_blk = n_all // MOE_TB
    rows_blk = _moe_block_rows(n_experts)
    n_tiles_max = -(-(n_blk * rows_blk) // MOE_TM) + n_experts
    zrow = lambda w, dt: jnp.zeros((pad, w), dt)
    hb_all = jnp.concatenate([hb, hb_s, zrow(d, BF16)], axis=0)
    comb_all = jnp.concatenate([comb, comb_s, zrow(LANES, F32)], axis=0)
    xb, dest, cnt, off = _moe_sort(hb_all, comb_all, n_experts=n_experts)
    block_pos, sorted_pos, seg_cnt, tile_expert, n_tiles = _moe_schedule(
        cnt[:, 0, :n_experts].astype(jnp.int32), off[:, 0, :n_experts].astype(jnp.int32),
        rows_blk, n_tiles_max)
    xs_sorted = _seg_copy(block_pos, sorted_pos, seg_cnt, xb.reshape(n_blk * rows_blk, d),
                          n_tiles_max * MOE_TM)
    ys_sorted = _experts(tile_expert, n_tiles, xs_sorted, w_eg, w_eu, w_ed, fc=896)
    yb = _seg_copy(sorted_pos, block_pos, seg_cnt, ys_sorted, n_blk * rows_blk)
    yb = yb.reshape(n_blk, rows_blk, d)
    x4 = _moe_combine(yb, comb_all, dest, x3, mod_p[1], GT_F, gn(1, 3), n_experts=n_experts,
                      per_token=False, tiles_per_batch=t // MOE_TB)
    pad_rows = lambda a: jnp.concatenate([a, jnp.zeros((pad, a.shape[1]), a.dtype)], axis=0)
    x4s = _moe_combine(yb[n_p // MOE_TB:], comb_all[n_p:], dest[n_p:], pad_rows(x3s),
                       pad_rows(mod_s[1]), GT_F, gn(1, 3), n_experts=n_experts,
                       per_token=True, tiles_per_batch=1)[:bs]

    return (x4.reshape(bsz, t, d), x4s.reshape(bs, 1, d),
            gla_p[:, None], cst[:, None],
            re_p.reshape(bsz, 1, g_s5, p_s5), im_p.reshape(bsz, 1, g_s5, p_s5),
            gla_s[:, None], conv_s[:, None],
            re_s.reshape(bs, 1, g_s5, p_s5), im_s.reshape(bs, 1, g_s5, p_s5))
```

```python
import functools
import math

import jax
import jax.numpy as jnp
from jax import lax
from jax.experimental import pallas as pl
from jax.experimental.pallas import tpu as pltpu

F32 = jnp.float32
BF16 = jnp.bfloat16
EPS = 1e-6
LANES = 128
GLA_CHUNK = 64
GATE_NORMALIZER = 16.0
N_MOD = 6
S5_GROUP = 16
S5_CHUNK_GROUPS = 8


def _dot(a, b, dims=None):
    if dims is None:
        return jnp.dot(a, b, preferred_element_type=F32)
    return lax.dot_general(a, b, (dims, ((), ())), preferred_element_type=F32)


def _split(a, parts=2):
    rem = a.astype(F32)
    out = []
    for _ in range(parts - 1):
        piece = rem.astype(BF16)
        out.append(piece)
        rem = rem - piece.astype(F32)
    out.append(rem.astype(BF16))
    return out


def _mm(a, w, prec):
    if prec == "bf16":
        return _dot(a.astype(BF16), w.astype(BF16))
    a1, a2, a3 = _split(a, 3)
    w1, w2, w3 = _split(w, 3)
    small = (_dot(a1, w3) + _dot(a3, w1)) + _dot(a2, w2)
    return _dot(a1, w1) + ((_dot(a1, w2) + _dot(a2, w1)) + small)


def _silu(x):
    return x * jax.nn.sigmoid(x)


def _gelu_tanh(x):
    return 0.5 * x * (1.0 + jnp.tanh(math.sqrt(2.0 / math.pi) * (x + 0.044715 * (x * x * x))))


def _log_sigmoid(x):
    return -(jnp.maximum(-x, 0.0) + jnp.log1p(jnp.exp(-jnp.abs(x))))


def _rms(x):
    return x * lax.rsqrt(jnp.mean(x * x, axis=-1, keepdims=True) + EPS)


def _mod_row(ref):
    return ref[0] if len(ref.shape) == 3 else ref[...]


def _normmod(x, g, sc, sh):
    return (_rms(x) * g) * (1.0 + sc) + sh


def _mod_spec(per_token, tm, d, col, tiles_per_batch):
    if per_token:
        return pl.BlockSpec((tm, d), lambda i, *_: (i, col))
    return pl.BlockSpec((1, 1, d), lambda i, *_: (i // tiles_per_batch, 0, col))


def _params(sem, vmem_mb=None):
    kw = dict(dimension_semantics=sem)
    if vmem_mb is not None:
        kw["vmem_limit_bytes"] = vmem_mb << 20
    return pltpu.CompilerParams(**kw)


def _ada_kernel(c_ref, w_ref, b_ref, o_ref):
    o_ref[0] = _mm(_silu(c_ref[...]), w_ref[0], "f32") + b_ref[0]


def _ada(c_all, w_ada, b_ada):
    depth, d, n6 = w_ada.shape
    rows = c_all.shape[0]
    return pl.pallas_call(
        _ada_kernel,
        grid=(depth, n6 // d),
        in_specs=[pl.BlockSpec((rows, d), lambda l, j: (0, 0)),
                  pl.BlockSpec((1, d, d), lambda l, j: (l, 0, j)),
                  pl.BlockSpec((1, 1, d), lambda l, j: (l, 0, j))],
        out_specs=pl.BlockSpec((1, rows, d), lambda l, j: (l, 0, j)),
        out_shape=jax.ShapeDtypeStruct((depth, rows, n6), F32),
        compiler_params=_params(("parallel", "parallel")),
    )(c_all, w_ada, b_ada.reshape(depth, 1, n6))


def _inproj_kernel(x_ref, g_ref, sc_ref, sh_ref, w_ref, o_ref, *, prec):
    h = _normmod(x_ref[...], g_ref[...], _mod_row(sc_ref), _mod_row(sh_ref))
    o_ref[...] = _mm(h, w_ref[...], prec).astype(o_ref.dtype)


def _inproj(x2d, g, mod, sc_col, sh_col, w, *, tm, bn, prec, per_token, tiles_per_batch,
            out_dtype=F32, vmem_mb=None):
    n_tok, d = x2d.shape
    n_out = w.shape[1]
    return pl.pallas_call(
        functools.partial(_inproj_kernel, prec=prec),
        grid=(n_tok // tm, n_out // bn),
        in_specs=[pl.BlockSpec((tm, d), lambda i, j: (i, 0)),
                  pl.BlockSpec((1, d), lambda i, j: (0, 0)),
                  _mod_spec(per_token, tm, d, sc_col, tiles_per_batch),
                  _mod_spec(per_token, tm, d, sh_col, tiles_per_batch),
                  pl.BlockSpec((d, bn), lambda i, j: (0, j))],
        out_specs=pl.BlockSpec((tm, bn), lambda i, j: (i, j)),
        out_shape=jax.ShapeDtypeStruct((n_tok, n_out), out_dtype),
        compiler_params=_params(("parallel", "arbitrary"), vmem_mb),
    )(x2d, g, mod, mod, w)


def _outproj_kernel(a_ref, w_ref, x_ref, gt_ref, g_ref, o_ref, *, prec, glu):
    z = _mm(a_ref[...], w_ref[...], prec)
    if glu:
        d = z.shape[1] // 2
        z = z[:, :d] * jax.nn.sigmoid(z[:, d:])
    o_ref[...] = x_ref[...] + _mod_row(gt_ref) * (_rms(z) * g_ref[...])


def _outproj(a2d, w, x2d, mod, gt_col, g, *, tm, prec, glu, per_token, tiles_per_batch,
             vmem_mb=None):
    n_tok, d = x2d.shape
    k, n_out = w.shape
    return pl.pallas_call(
        functools.partial(_outproj_kernel, prec=prec, glu=glu),
        grid=(n_tok // tm,),
        in_specs=[pl.BlockSpec((tm, k), lambda i: (i, 0)),
                  pl.BlockSpec((k, n_out), lambda i: (0, 0)),
                  pl.BlockSpec((tm, d), lambda i: (i, 0)),
                  _mod_spec(per_token, tm, d, gt_col, tiles_per_batch),
                  pl.BlockSpec((1, d), lambda i: (0, 0))],
        out_specs=pl.BlockSpec((tm, d), lambda i: (i, 0)),
        out_shape=jax.ShapeDtypeStruct((n_tok, d), F32),
        compiler_params=_params(("parallel",), vmem_mb),
    )(a2d, w, x2d, mod, g)


def _ffn_kernel(x_ref, g1_ref, sc_ref, sh_ref, wg_ref, wu_ref, wd_ref, gt_ref, g2_ref,
                o_ref, h_scr, acc_scr, *, prec, n_c):
    c = pl.program_id(1)

    @pl.when(c == 0)
    def _():
        h = _normmod(x_ref[...], g1_ref[...], _mod_row(sc_ref), _mod_row(sh_ref))
        h_scr[...] = h.astype(h_scr.dtype)
        acc_scr[...] = jnp.zeros_like(acc_scr)

    h = h_scr[...]
    act = _silu(_mm(h, wg_ref[...], prec)) * _mm(h, wu_ref[...], prec)
    acc_scr[...] += _mm(act, wd_ref[...], prec)

    @pl.when(c == n_c - 1)
    def _():
        o_ref[...] = x_ref[...] + _mod_row(gt_ref) * (_rms(acc_scr[...]) * g2_ref[...])


def _ffn(x2d, g1, mod, sc_col, sh_col, gt_col, g2, wg, wu, wd, *, tm, fc, prec,
         per_token, tiles_per_batch, vmem_mb=None):
    n_tok, d = x2d.shape
    n_c = wg.shape[1] // fc
    tok = pl.BlockSpec((tm, d), lambda i, c: (i, 0))
    vec = pl.BlockSpec((1, d), lambda i, c: (0, 0))
    h_dtype = BF16 if prec == "bf16" else F32
    return pl.pallas_call(
        functools.partial(_ffn_kernel, prec=prec, n_c=n_c),
        grid=(n_tok // tm, n_c),
        in_specs=[tok, vec,
                  _mod_spec(per_token, tm, d, sc_col, tiles_per_batch),
                  _mod_spec(per_token, tm, d, sh_col, tiles_per_batch),
                  pl.BlockSpec((d, fc), lambda i, c: (0, c)),
                  pl.BlockSpec((d, fc), lambda i, c: (0, c)),
                  pl.BlockSpec((fc, d), lambda i, c: (c, 0)),
                  _mod_spec(per_token, tm, d, gt_col, tiles_per_batch), vec],
        out_specs=tok,
        out_shape=jax.ShapeDtypeStruct((n_tok, d), F32),
        scratch_shapes=[pltpu.VMEM((tm, d), h_dtype), pltpu.VMEM((tm, d), F32)],
        compiler_params=_params(("parallel", "arbitrary"), vmem_mb),
    )(x2d, g1, mod, mod, wg, wu, wd, mod, g2)


def _router_kernel(x_ref, g_ref, sc_ref, sh_ref, wr_ref, comb_ref, hb_ref, *, n_experts):
    h = _normmod(x_ref[...], g_ref[...], _mod_row(sc_ref), _mod_row(sh_ref))
    hb_ref[...] = h.astype(hb_ref.dtype)
    logits = _mm(h, wr_ref[...], "f32")
    lane = lax.broadcasted_iota(jnp.int32, logits.shape, 1).astype(F32)
    neg = -jnp.inf
    l1 = jnp.where(lane < n_experts, logits, neg)
    m1 = jnp.max(l1, axis=-1, keepdims=True)
    i1 = jnp.min(jnp.where(l1 == m1, lane, float(LANES)), axis=-1, keepdims=True)
    l2 = jnp.where(lane == i1, neg, l1)
    m2 = jnp.max(l2, axis=-1, keepdims=True)
    i2 = jnp.min(jnp.where(l2 == m2, lane, float(LANES)), axis=-1, keepdims=True)
    e2 = jnp.exp(m2 - m1)
    den = 1.0 + e2
    comb_ref[...] = jnp.where(lane == i1, 1.0 / den, 0.0) + jnp.where(lane == i2, e2 / den, 0.0)


def _router(x2d, g, mod, sc_col, sh_col, wr_pad, *, tm, n_experts, per_token, tiles_per_batch):
    n_tok, d = x2d.shape
    return pl.pallas_call(
        functools.partial(_router_kernel, n_experts=n_experts),
        grid=(n_tok // tm,),
        in_specs=[pl.BlockSpec((tm, d), lambda i: (i, 0)),
                  pl.BlockSpec((1, d), lambda i: (0, 0)),
                  _mod_spec(per_token, tm, d, sc_col, tiles_per_batch),
                  _mod_spec(per_token, tm, d, sh_col, tiles_per_batch),
                  pl.BlockSpec((d, LANES), lambda i: (0, 0))],
        out_specs=[pl.BlockSpec((tm, LANES), lambda i: (i, 0)),
                   pl.BlockSpec((tm, d), lambda i: (i, 0))],
        out_shape=[jax.ShapeDtypeStruct((n_tok, LANES), F32),
                   jax.ShapeDtypeStruct((n_tok, d), BF16)],
        compiler_params=_params(("parallel",)),
    )(x2d, g, mod, mod, wr_pad)


MOE_TB = 256
MOE_SEG = 16
MOE_TM = 512


def _moe_block_rows(n_experts):
    return -(-(2 * MOE_TB + n_experts * (MOE_SEG - 1)) // LANES) * LANES


def _scatter_matrix(dest, comb, weighted, n_rows, n_experts):
    tb = dest.shape[0]
    lane = lax.broadcasted_iota(jnp.int32, (tb, n_rows), 1).astype(F32)
    m = jnp.zeros((tb, n_rows), F32)
    for e in range(n_experts):
        w = comb[:, e:e + 1]
        val = jnp.where(w > 0.0, w if weighted else 1.0, 0.0)
        m = m + jnp.where(lane == dest[:, e:e + 1], val, 0.0)
    return m


_SEG_BITS = (256, 128, 64, 32, 16)


def _segment_copies(make_copy, blk, pos_ref, off_ref, cnt_ref, n_experts, start):
    for e in range(n_experts):
        seg = blk * n_experts + e
        in_blk, in_sorted, n = off_ref[seg], pos_ref[seg], cnt_ref[seg]
        for size in _SEG_BITS:
            has = (n & size) != 0

            @pl.when(has)
            def _(in_blk=in_blk, in_sorted=in_sorted, size=size):
                cp = make_copy(pl.multiple_of(in_blk, MOE_SEG), pl.multiple_of(in_sorted, MOE_SEG), size)
                cp.start() if start else cp.wait()

            step = jnp.where(has, size, 0)
            in_blk, in_sorted = in_blk + step, in_sorted + step


def _moe_sort_kernel(pos_ref, off_ref, cnt_ref, hb_ref, comb_ref, init_hbm, dest_ref, xs_hbm,
                     buf, sems, *, n_rows, n_experts, n_blk):
    del init_hbm
    b = pl.program_id(0)
    slot = b % 2

    def writes(blk, slot, start):
        def make_copy(block_row, sorted_row, size):
            return pltpu.make_async_copy(buf.at[slot, pl.ds(block_row, size)],
                                         xs_hbm.at[pl.ds(sorted_row, size)], sems.at[slot])
        _segment_copies(make_copy, blk, pos_ref, off_ref, cnt_ref, n_experts, start)

    @pl.when(b >= 2)
    def _():
        writes(b - 2, slot, False)

    comb = comb_ref[...]
    tb = comb.shape[0]
    sel = jnp.where(comb > 0.0, 1.0, 0.0)
    r = lax.broadcasted_iota(jnp.int32, (tb, tb), 0)
    c = lax.broadcasted_iota(jnp.int32, (tb, tb), 1)
    rank = _dot(jnp.where(c < r, 1.0, 0.0).astype(BF16), sel.astype(BF16))
    cnt = jnp.sum(sel, axis=0, keepdims=True)
    cnt = jnp.floor((cnt + (MOE_SEG - 1)) / MOE_SEG) * MOE_SEG
    ru = lax.broadcasted_iota(jnp.int32, (LANES, LANES), 0)
    cu = lax.broadcasted_iota(jnp.int32, (LANES, LANES), 1)
    off = _dot(jnp.broadcast_to(cnt, (8, LANES)).astype(BF16),
               jnp.where(ru < cu, 1.0, 0.0).astype(BF16))[0:1]
    dest = off + rank
    dest_ref[...] = dest
    pt = _scatter_matrix(dest, comb, False, n_rows, n_experts).astype(BF16)
    buf[slot] = _dot(pt, hb_ref[...], ((0,), (0,))).astype(buf.dtype)
    writes(b, slot, True)

    @pl.when(b == n_blk - 1)
    def _():
        writes(b, slot, False)
        if n_blk > 1:
            writes(b - 1, 1 - slot, False)


def _moe_sort(sorted_pos, block_off, seg_cnt, hb, comb, n_sorted_rows, *, n_experts):
    n_tok, d = hb.shape
    n_blk = n_tok // MOE_TB
    n_rows = _moe_block_rows(n_experts)
    init = jnp.zeros((n_sorted_rows, d), BF16)
    any_spec = pl.BlockSpec(memory_space=pl.ANY)
    return pl.pallas_call(
        functools.partial(_moe_sort_kernel, n_rows=n_rows, n_experts=n_experts, n_blk=n_blk),
        grid_spec=pltpu.PrefetchScalarGridSpec(
            num_scalar_prefetch=3, grid=(n_blk,),
            in_specs=[pl.BlockSpec((MOE_TB, d), lambda b, *_: (b, 0)),
                      pl.BlockSpec((MOE_TB, LANES), lambda b, *_: (b, 0)),
                      any_spec],
            out_specs=[pl.BlockSpec((MOE_TB, LANES), lambda b, *_: (b, 0)), any_spec],
            scratch_shapes=[pltpu.VMEM((2, n_rows, d), BF16), pltpu.SemaphoreType.DMA((2,))]),
        out_shape=[jax.ShapeDtypeStruct((n_tok, LANES), F32),
                   jax.ShapeDtypeStruct((n_sorted_rows, d), BF16)],
        input_output_aliases={5: 1},
        compiler_params=_params(("arbitrary",)),
    )(sorted_pos, block_off, seg_cnt, hb, comb, init)


def _experts_kernel(te_ref, nt_ref, x_ref, wg_ref, wu_ref, wd_ref, o_ref, acc_scr, *, n_c):
    del te_ref
    k = pl.program_id(0)
    c = pl.program_id(1)
    active = k < nt_ref[0]

    @pl.when(active)
    def _():
        h = x_ref[...]
        act = _silu(_dot(h, wg_ref[0])) * _dot(h, wu_ref[0])
        y = _dot(act.astype(BF16), wd_ref[0])

        @pl.when(c == 0)
        def _():
            acc_scr[...] = y

        @pl.when(c > 0)
        def _():
            acc_scr[...] += y

    @pl.when(c == n_c - 1)
    def _():
        o_ref[...] = jnp.where(active, acc_scr[...], 0.0).astype(o_ref.dtype)


def _experts(tile_expert, n_tiles, xs, wg, wu, wd, *, fc):
    n_rows, d = xs.shape
    n_c = wg.shape[2] // fc

    def w_map(k, c, te, nt):
        return (te[k], 0, jnp.where(k < nt[0], c, n_c - 1))

    def wd_map(k, c, te, nt):
        return (te[k], jnp.where(k < nt[0], c, n_c - 1), 0)

    return pl.pallas_call(
        functools.partial(_experts_kernel, n_c=n_c),
        grid_spec=pltpu.PrefetchScalarGridSpec(
            num_scalar_prefetch=2, grid=(n_rows // MOE_TM, n_c),
            in_specs=[pl.BlockSpec((MOE_TM, d), lambda k, c, te, nt: (k, 0)),
                      pl.BlockSpec((1, d, fc), w_map), pl.BlockSpec((1, d, fc), w_map),
                      pl.BlockSpec((1, fc, d), wd_map)],
            out_specs=pl.BlockSpec((MOE_TM, d), lambda k, c, te, nt: (k, 0)),
            scratch_shapes=[pltpu.VMEM((MOE_TM, d), F32)]),
        out_shape=jax.ShapeDtypeStruct((n_rows, d), BF16),
        compiler_params=_params(("arbitrary", "arbitrary"), 48),
    )(tile_expert, n_tiles, xs, wg, wu, wd)


def _moe_combine_kernel(pos_ref, off_ref, cnt_ref, ys_hbm, comb_ref, dest_ref, x_ref, gt_ref, g_ref,
                        o_ref, buf, sems, *, n_experts, n_blk, blk0):
    b = pl.program_id(0)
    slot = b % 2

    def reads(blk, slot, start):
        def make_copy(block_row, sorted_row, size):
            return pltpu.make_async_copy(ys_hbm.at[pl.ds(sorted_row, size)],
                                         buf.at[slot, pl.ds(block_row, size)], sems.at[slot])
        _segment_copies(make_copy, blk0 + blk, pos_ref, off_ref, cnt_ref, n_experts, start)

    @pl.when(b == 0)
    def _():
        buf[...] = jnp.zeros_like(buf)
        reads(b, slot, True)

    if n_blk > 1:
        @pl.when(b + 1 < n_blk)
        def _():
            reads(b + 1, 1 - slot, True)

    reads(b, slot, False)
    yb = buf[slot]
    wt = _scatter_matrix(dest_ref[...], comb_ref[...], True, yb.shape[0], n_experts)
    w_hi, w_lo = _split(wt)
    y = _dot(w_hi, yb) + _dot(w_lo, yb)
    o_ref[...] = x_ref[...] + _mod_row(gt_ref) * (_rms(y) * g_ref[...])


def _moe_combine(sorted_pos, block_off, seg_cnt, ys, comb, dest, x2d, mod, gt_col, g, *, blk0,
                 n_experts, per_token, tiles_per_batch):
    n_tok, d = x2d.shape
    n_blk = n_tok // MOE_TB
    n_rows = _moe_block_rows(n_experts)
    return pl.pallas_call(
        functools.partial(_moe_combine_kernel, n_experts=n_experts, n_blk=n_blk, blk0=blk0),
        grid_spec=pltpu.PrefetchScalarGridSpec(
            num_scalar_prefetch=3, grid=(n_blk,),
            in_specs=[pl.BlockSpec(memory_space=pl.ANY),
                      pl.BlockSpec((MOE_TB, LANES), lambda b, *_: (blk0 + b, 0)),
                      pl.BlockSpec((MOE_TB, LANES), lambda b, *_: (blk0 + b, 0)),
                      pl.BlockSpec((MOE_TB, d), lambda b, *_: (b, 0)),
                      _mod_spec(per_token, MOE_TB, d, gt_col, tiles_per_batch),
                      pl.BlockSpec((1, d), lambda b, *_: (0, 0))],
            out_specs=pl.BlockSpec((MOE_TB, d), lambda b, *_: (b, 0)),
            scratch_shapes=[pltpu.VMEM((2, n_rows, d), BF16), pltpu.SemaphoreType.DMA((2,))]),
        out_shape=jax.ShapeDtypeStruct((n_tok, d), F32),
        compiler_params=_params(("arbitrary",)),
    )(sorted_pos, block_off, seg_cnt, ys, comb, dest, x2d, mod, g)


def _moe_schedule(comb, n_experts, n_tiles_max):
    n_blk = comb.shape[0] // MOE_TB
    cnt = jnp.sum((comb[:, :n_experts] > 0.0).reshape(n_blk, MOE_TB, n_experts).astype(jnp.int32),
                  axis=1)
    cnt = -(-cnt // MOE_SEG) * MOE_SEG
    off = jnp.cumsum(cnt, axis=1) - cnt
    tot = jnp.sum(cnt, axis=0)
    grp = -(-tot // MOE_TM) * MOE_TM
    g_end = jnp.cumsum(grp)
    sorted_pos = (g_end - grp)[None, :] + (jnp.cumsum(cnt, axis=0) - cnt)
    n_tiles = g_end[-1] // MOE_TM
    first_row = jnp.arange(n_tiles_max, dtype=jnp.int32) * MOE_TM
    tile_expert = jnp.sum(first_row[:, None] >= g_end[None, :], axis=1)
    last = jnp.sum((n_tiles - 1) * MOE_TM >= g_end)
    tile_expert = jnp.minimum(tile_expert, last).astype(jnp.int32)
    flat = lambda a: a.reshape(-1).astype(jnp.int32)
    return (flat(sorted_pos), flat(off), flat(cnt), tile_expert,
            n_tiles.reshape(1).astype(jnp.int32))


def _gla_conv_kernel(p_ref, wgu_ref, bgu_ref, gh_ref, wc_ref, mixed_ref, st_ref, cst_ref,
                     s_scr, uext_scr, *, tt, n_t, heads, dk, dv, cw):
    ti = pl.program_id(1)
    hk = heads * dk
    hv = heads * dv
    o_q, o_k, o_v, o_og = 0, hk, 2 * hk, 2 * hk + hv
    o_ch = o_og + hv
    o_cb, o_cc, o_lr = o_ch + cw, o_ch + 2 * cw, o_ch + 3 * cw
    L = GLA_CHUNK

    @pl.when(ti == 0)
    def _():
        s_scr[...] = jnp.zeros_like(s_scr)
        uext_scr[0:8, :] = jnp.zeros((8, cw), F32)

    logit = _dot(p_ref[:, o_lr:o_lr + LANES].astype(BF16), wgu_ref[...]) + bgu_ref[...]
    logg = _log_sigmoid(logit) / GATE_NORMALIZER

    r = lax.broadcasted_iota(jnp.int32, (tt, tt), 0)
    c = lax.broadcasted_iota(jnp.int32, (tt, tt), 1)
    tri = jnp.where((r // L == c // L) & (c <= r), 1.0, 0.0).astype(BF16)
    g_hi = logg.astype(BF16)
    rem = logg - g_hi.astype(F32)
    g_mid = rem.astype(BF16)
    g_lo = (rem - g_mid.astype(F32)).astype(BF16)
    bc = _dot(tri, g_hi) + (_dot(tri, g_mid) + _dot(tri, g_lo))

    rl = lax.broadcasted_iota(jnp.int32, (L, L), 0)
    cl = lax.broadcasted_iota(jnp.int32, (L, L), 1)
    tril = cl <= rl
    lane = lax.broadcasted_iota(jnp.int32, (L, LANES), 1)
    heads_per_blk = LANES // dk
    nt_dims = ((1,), (1,))

    for ck in range(tt // L):
        rows = slice(L * ck, L * ck + L)
        b = bc[rows]
        bl = b[L - 1:L]
        q_in = (p_ref[rows, o_q:o_q + hk] * (dk ** -0.5)) * jnp.exp(b)
        k = p_ref[rows, o_k:o_k + hk]
        k_out = (k * jnp.exp(-b)).astype(BF16)
        k_dec = (k * jnp.exp(bl - b)).astype(BF16)
        dec = jnp.exp(bl)
        q_in = q_in.astype(BF16)
        for h in range(heads):
            blk = slice(LANES * (h // heads_per_blk), LANES * (h // heads_per_blk) + LANES)
            in_head = (lane // dk) == (h % heads_per_blk)
            qm = jnp.where(in_head, q_in[:, blk], jnp.zeros_like(q_in[:, blk]))
            att = _dot(qm, k_out[:, blk], nt_dims)
            att = jnp.where(tril, att, 0.0).astype(BF16)
            vh = p_ref[rows, o_v + dv * h:o_v + dv * h + dv].astype(BF16)
            s_t = s_scr[h]
            o = _dot(att, vh) + _dot(qm, s_t.astype(BF16), nt_dims)
            ds_t = _dot(vh, k_dec[:, blk], ((0,), (0,)))
            s_scr[h] = s_t * dec[:, blk] + ds_t
            og = p_ref[rows, o_og + dv * h:o_og + dv * h + dv]
            res = (_rms(o) * gh_ref[...]) * _silu(og)
            mixed_ref[rows, dv * h:dv * h + dv] = res.astype(mixed_ref.dtype)

    u = p_ref[:, o_cc:o_cc + cw] * p_ref[:, o_ch:o_ch + cw]
    uext_scr[8:8 + tt, :] = u
    y = (wc_ref[0:1, :] * uext_scr[6:6 + tt, :] + wc_ref[1:2, :] * uext_scr[7:7 + tt, :]
         + wc_ref[2:3, :] * u)
    mixed_ref[:, hv:hv + cw] = (p_ref[:, o_cb:o_cb + cw] * y).astype(mixed_ref.dtype)
    tail = uext_scr[tt + 6:tt + 8, :]
    uext_scr[6:8, :] = tail

    @pl.when(ti == n_t - 1)
    def _():
        st_ref[0] = s_scr[...]
        cst_ref[0] = tail


def _gla_conv_prompt(proj, wgu_pad, bgu, g_head, w_conv, *, bsz, t, tt, heads, dk, dv, cw):
    n_t = t // tt
    n_in = proj.shape[1]
    width = heads * dv + cw
    return pl.pallas_call(
        functools.partial(_gla_conv_kernel, tt=tt, n_t=n_t, heads=heads, dk=dk, dv=dv, cw=cw),
        grid=(bsz, n_t),
        in_specs=[pl.BlockSpec((tt, n_in), lambda b, i: (b * n_t + i, 0)),
                  pl.BlockSpec(wgu_pad.shape, lambda b, i: (0, 0)),
                  pl.BlockSpec(bgu.shape, lambda b, i: (0, 0)),
                  pl.BlockSpec(g_head.shape, lambda b, i: (0, 0)),
                  pl.BlockSpec(w_conv.shape, lambda b, i: (0, 0))],
        out_specs=[pl.BlockSpec((tt, width), lambda b, i: (b * n_t + i, 0)),
                   pl.BlockSpec((1, heads, dv, LANES), lambda b, i: (b, 0, 0, 0)),
                   pl.BlockSpec((1, 2, cw), lambda b, i: (b, 0, 0))],
        out_shape=[jax.ShapeDtypeStruct((bsz * t, width), BF16),
                   jax.ShapeDtypeStruct((bsz, heads, dv, LANES), F32),
                   jax.ShapeDtypeStruct((bsz, 2, cw), F32)],
        scratch_shapes=[pltpu.VMEM((heads, dv, LANES), F32), pltpu.VMEM((tt + 8, cw), F32)],
        compiler_params=_params(("parallel", "arbitrary"), 48),
    )(proj, wgu_pad, bgu, g_head, w_conv)


def _gla_conv_step_kernel(q_ref, k_ref, v_ref, og_ref, ch_ref, cb_ref, cc_ref, lr_ref,
                          wgu_ref, bgu_ref, gh_ref, wc_ref, s_ref, cbuf_ref,
                          o_ref, y_ref, sn_ref, cn_ref, a_scr, *, dk):
    logit = _mm(wgu_ref[...], lr_ref[...], "f32") + bgu_ref[...]
    a_scr[...] = jnp.exp(_log_sigmoid(logit) / GATE_NORMALIZER)
    v_t = v_ref[...]

    def body(d, acc):
        a = a_scr[pl.ds(d, 1), :]
        kd = k_ref[pl.ds(d, 1), :]
        qd = q_ref[pl.ds(d, 1), :]
        s_new = a * s_ref[0, d] + kd * v_t
        sn_ref[0, d] = s_new
        return acc + (qd * (dk ** -0.5)) * s_new

    o = lax.fori_loop(0, dk, body, jnp.zeros(v_t.shape, F32))
    o = o * lax.rsqrt(jnp.mean(o * o, axis=0, keepdims=True) + EPS) * gh_ref[...]
    o_ref[...] = o * _silu(og_ref[...])
    u = cc_ref[...] * ch_ref[...]
    y = wc_ref[0] * cbuf_ref[0] + wc_ref[1] * cbuf_ref[1] + wc_ref[2] * u
    y_ref[...] = cb_ref[...] * y
    cn_ref[0] = cbuf_ref[1]
    cn_ref[1] = u


def _gla_conv_step(proj_t, wgu_t_pad, bgu_col, gh_col, wc_col, s_t, cbuf_t, *, heads, dk, dv, cw):
    bsz = proj_t.shape[1]
    hk, hv = heads * dk, heads * dv
    cs = cw // heads
    assert dv == LANES and cs == LANES and dk * 2 == LANES
    o_k, o_v, o_og = hk, 2 * hk, 2 * hk + hv
    o_ch = o_og + hv
    o_cb, o_cc, o_lr = o_ch + cw, o_ch + 2 * cw, o_ch + 3 * cw
    row = lambda off, size: (lambda h: (off // size + h, 0))
    blk = lambda size, off: pl.BlockSpec((size, bsz), row(off, size))
    return pl.pallas_call(
        functools.partial(_gla_conv_step_kernel, dk=dk),
        grid=(heads,),
        in_specs=[blk(dk, 0), blk(dk, o_k), blk(dv, o_v), blk(dv, o_og),
                  blk(cs, o_ch), blk(cs, o_cb), blk(cs, o_cc),
                  pl.BlockSpec((LANES, bsz), lambda h: (o_lr // LANES, 0)),
                  pl.BlockSpec((dk, LANES), lambda h: (h, 0)),
                  pl.BlockSpec((dk, 1), lambda h: (h, 0)),
                  pl.BlockSpec((dv, 1), lambda h: (0, 0)),
                  pl.BlockSpec((3, cs, 1), lambda h: (0, h, 0)),
                  pl.BlockSpec((1, dk, dv, bsz), lambda h: (h, 0, 0, 0)),
                  pl.BlockSpec((2, cs, bsz), lambda h: (0, h, 0))],
        out_specs=[pl.BlockSpec((dv, bsz), lambda h: (h, 0)),
                   pl.BlockSpec((cs, bsz), lambda h: (h, 0)),
                   pl.BlockSpec((1, dk, dv, bsz), lambda h: (h, 0, 0, 0)),
                   pl.BlockSpec((2, cs, bsz), lambda h: (0, h, 0))],
        out_shape=[jax.ShapeDtypeStruct((hv, bsz), F32),
                   jax.ShapeDtypeStruct((cw, bsz), F32),
                   jax.ShapeDtypeStruct(s_t.shape, F32),
                   jax.ShapeDtypeStruct(cbuf_t.shape, F32)],
        scratch_shapes=[pltpu.VMEM((dk, bsz), F32)],
        compiler_params=_params(("parallel",)),
    )(proj_t, proj_t, proj_t, proj_t, proj_t, proj_t, proj_t, proj_t,
      wgu_t_pad, bgu_col, gh_col, wc_col, s_t, cbuf_t)


def _s5_scan_kernel(u_ref, bb_ref, cm_ref, abr_ref, abi_ref, d_ref, y_ref, sre_ref, sim_ref,
                    scr, xr_scr, xi_scr, *, tt, n_t, nb, half):
    ti = pl.program_id(1)

    @pl.when(ti == 0)
    def _():
        xr_scr[...] = jnp.zeros_like(xr_scr)
        xi_scr[...] = jnp.zeros_like(xi_scr)

    u = u_ref[...].reshape(nb * tt, LANES)
    bu = _dot(u.astype(BF16), bb_ref[0])
    n_blk = half // LANES
    for cb in range(2 * n_blk):
        scr[cb] = bu[:, cb * LANES:(cb + 1) * LANES]
    a_re = [jnp.broadcast_to(abr_ref[0][:, cb * LANES:(cb + 1) * LANES], (nb, LANES))
            for cb in range(n_blk)]
    a_im = [jnp.broadcast_to(abi_ref[0][:, cb * LANES:(cb + 1) * LANES], (nb, LANES))
            for cb in range(n_blk)]

    def body(t, carry):
        rows = pl.ds(t, nb, stride=tt)
        out = []
        for cb in range(n_blk):
            xr, xi = carry[2 * cb], carry[2 * cb + 1]
            nr = (a_re[cb] * xr - a_im[cb] * xi) + scr[cb, rows, :]
            ni = (a_re[cb] * xi + a_im[cb] * xr) + scr[n_blk + cb, rows, :]
            scr[cb, rows, :] = nr
            scr[n_blk + cb, rows, :] = ni
            out += [nr, ni]
        return tuple(out)

    init = []
    for cb in range(n_blk):
        init += [xr_scr[:, cb * LANES:(cb + 1) * LANES], xi_scr[:, cb * LANES:(cb + 1) * LANES]]
    fin = lax.fori_loop(0, tt, body, tuple(init))
    xr = jnp.concatenate([fin[2 * cb] for cb in range(n_blk)], axis=1)
    xi = jnp.concatenate([fin[2 * cb + 1] for cb in range(n_blk)], axis=1)
    xr_scr[...] = xr
    xi_scr[...] = xi
    xs = jnp.concatenate([scr[cb] for cb in range(2 * n_blk)], axis=1)
    y = _dot(xs.astype(BF16), cm_ref[0]) + d_ref[0] * u
    y_ref[...] = _gelu_tanh(y).reshape(nb, tt, LANES).astype(y_ref.dtype)

    @pl.when(ti == n_t - 1)
    def _():
        sre_ref[...] = xr
        sim_ref[...] = xi


def _s5_prompt(u3d, bb_blk, c_blk, ab_re, ab_im, d_blk, *, tt):
    nb, t, d = u3d.shape
    n_j = d // LANES
    n_t = t // tt
    half = bb_blk.shape[2] // 2
    return pl.pallas_call(
        functools.partial(_s5_scan_kernel, tt=tt, n_t=n_t, nb=nb, half=half),
        grid=(n_j, n_t),
        in_specs=[pl.BlockSpec((nb, tt, LANES), lambda j, i: (0, i, j)),
                  pl.BlockSpec((1, LANES, 2 * half), lambda j, i: (j, 0, 0)),
                  pl.BlockSpec((1, 2 * half, LANES), lambda j, i: (j, 0, 0)),
                  pl.BlockSpec((1, 1, half), lambda j, i: (j, 0, 0)),
                  pl.BlockSpec((1, 1, half), lambda j, i: (j, 0, 0)),
                  pl.BlockSpec((1, 1, LANES), lambda j, i: (j, 0, 0))],
        out_specs=[pl.BlockSpec((nb, tt, LANES), lambda j, i: (0, i, j)),
                   pl.BlockSpec((nb, half), lambda j, i: (0, j)),
                   pl.BlockSpec((nb, half), lambda j, i: (0, j))],
        out_shape=[jax.ShapeDtypeStruct((nb, t, d), BF16),
                   jax.ShapeDtypeStruct((nb, n_j * half), F32),
                   jax.ShapeDtypeStruct((nb, n_j * half), F32)],
        scratch_shapes=[pltpu.VMEM((2 * half // LANES, nb * tt, LANES), F32),
                        pltpu.VMEM((nb, half), F32), pltpu.VMEM((nb, half), F32)],
        compiler_params=_params(("parallel", "arbitrary")),
    )(u3d, bb_blk, c_blk, ab_re, ab_im, d_blk)


def _s5_step_kernel(u_ref, bb_ref, cm_ref, abr_ref, abi_ref, d_ref, sre_ref, sim_ref,
                    y_ref, nre_ref, nim_ref, *, half):
    u = u_ref[...]
    bu = _mm(u, bb_ref[0], "f32")
    ar, ai = abr_ref[0], abi_ref[0]
    sr, si = sre_ref[...], sim_ref[...]
    nr = (ar * sr - ai * si) + bu[:, 0:half]
    ni = (ar * si + ai * sr) + bu[:, half:2 * half]
    nre_ref[...] = nr
    nim_ref[...] = ni
    y = _mm(jnp.concatenate([nr, ni], axis=1), cm_ref[0], "f32") + d_ref[0] * u
    y_ref[...] = _gelu_tanh(y)


def _s5_step(u2d, bb_blk, c_blk, ab_re, ab_im, d_blk, s_re, s_im):
    bsz, d = u2d.shape
    n_j = d // LANES
    half = bb_blk.shape[2] // 2
    st = pl.BlockSpec((bsz, half), lambda j: (0, j))
    return pl.pallas_call(
        functools.partial(_s5_step_kernel, half=half),
        grid=(n_j,),
        in_specs=[pl.BlockSpec((bsz, LANES), lambda j: (0, j)),
                  pl.BlockSpec((1, LANES, 2 * half), lambda j: (j, 0, 0)),
                  pl.BlockSpec((1, 2 * half, LANES), lambda j: (j, 0, 0)),
                  pl.BlockSpec((1, 1, half), lambda j: (j, 0, 0)),
                  pl.BlockSpec((1, 1, half), lambda j: (j, 0, 0)),
                  pl.BlockSpec((1, 1, LANES), lambda j: (j, 0, 0)),
                  st, st],
        out_specs=[pl.BlockSpec((bsz, LANES), lambda j: (0, j)), st, st],
        out_shape=[jax.ShapeDtypeStruct((bsz, d), F32),
                   jax.ShapeDtypeStruct(s_re.shape, F32),
                   jax.ShapeDtypeStruct(s_im.shape, F32)],
        compiler_params=_params(("parallel",)),
    )(u2d, bb_blk, c_blk, ab_re, ab_im, d_blk, s_re, s_im)


def _s5_discretize(a_re, a_im, log_dt, b_re, b_im, c_re, c_im, d_skip):
    g, p = a_re.shape
    cg = S5_CHUNK_GROUPS
    n_j = g // cg
    dt = jnp.exp(log_dt)[:, None]
    mag = jnp.exp(dt * a_re)
    ab_re, ab_im = mag * jnp.cos(dt * a_im), mag * jnp.sin(dt * a_im)
    den = a_re * a_re + a_im * a_im
    nr, ni = ab_re - 1.0, ab_im
    f_re = (nr * a_re + ni * a_im) / den
    f_im = (ni * a_re - nr * a_im) / den
    bb_re = f_re[..., None] * b_re - f_im[..., None] * b_im
    bb_im = f_re[..., None] * b_im + f_im[..., None] * b_re
    eye = jnp.eye(cg, dtype=F32)

    def in_blk(m):
        m = m.reshape(n_j, cg, p, S5_GROUP)
        return jnp.einsum("jgpi,gh->jgihp", m, eye).reshape(n_j, cg * S5_GROUP, cg * p)

    def out_blk(m):
        m = m.reshape(n_j, cg, S5_GROUP, p)
        return jnp.einsum("jgip,gh->jgphi", m, eye).reshape(n_j, cg * p, cg * S5_GROUP)

    bb_blk = jnp.concatenate([in_blk(bb_re), in_blk(bb_im)], axis=2)
    c_blk = jnp.concatenate([out_blk(c_re), out_blk(-c_im)], axis=1)
    return (bb_blk, c_blk, ab_re.reshape(n_j, 1, cg * p), ab_im.reshape(n_j, 1, cg * p),
            d_skip.reshape(n_j, 1, cg * S5_GROUP))


def kernel(x_prompt, x_sample, c_prompt, c_sample, state_gla, state_conv, state_s5_re, state_s5_im, w_ada, b_ada, g_norm, w_in0, w_gate_up, b_gate_up, g_head_norm, w_conv, w_out0, w_ffn_gate, w_ffn_up, w_ffn_down, w_in1, s5_a_re, s5_a_im, s5_log_dt, s5_b_re, s5_b_im, s5_c_re, s5_c_im, s5_d, w_glu, w_router, w_exp_gate, w_exp_up, w_exp_down):
    bsz, t, d = x_prompt.shape
    bs = x_sample.shape[0]
    heads, dk, dv = state_gla.shape[2:]
    cw = state_conv.shape[3]
    lowrank = w_gate_up.shape[1]
    hk, hv = heads * dk, heads * dv
    n_experts = w_router.shape[2]
    g_s5, p_s5 = s5_a_re.shape[1:]

    mod = _ada(jnp.concatenate([c_prompt, c_sample], axis=0), w_ada, b_ada)
    mod_p = [mod[l, :bsz].reshape(bsz, 1, N_MOD * d) for l in range(2)]
    mod_s = [mod[l, bsz:] for l in range(2)]
    SH_M, SC_M, GT_M, SH_F, SC_F, GT_F = range(N_MOD)
    gn = lambda l, k: g_norm[l, k].reshape(1, d)

    w0 = w_in0[0]
    o_lr = 2 * hk + 2 * hv
    w0r = jnp.concatenate([w0[:, :o_lr], w0[:, o_lr + lowrank:], w0[:, o_lr:o_lr + lowrank],
                           jnp.zeros((d, LANES - lowrank), F32)], axis=1)
    wgu_pad = jnp.concatenate([w_gate_up[0], jnp.zeros((LANES - lowrank, hk), F32)], axis=0)
    bb_blk, c_blk, ab_re, ab_im, d_blk = _s5_discretize(
        s5_a_re[0], s5_a_im[0], s5_log_dt[0], s5_b_re[0], s5_b_im[0], s5_c_re[0], s5_c_im[0],
        s5_d[0])
    wr_pad = jnp.concatenate([w_router[0], jnp.zeros((d, LANES - n_experts), F32)], axis=1)
    bf = lambda a: a.astype(BF16)
    w_eg, w_eu, w_ed = bf(w_exp_gate[0]), bf(w_exp_up[0]), bf(w_exp_down[0])

    tm = 512
    tpb = t // tm
    pk = dict(per_token=False, tiles_per_batch=tpb)
    xp = x_prompt.reshape(bsz * t, d)
    proj = _inproj(xp, gn(0, 0), mod_p[0], SC_M, SH_M, bf(w0r), tm=tm, bn=w0r.shape[1],
                   prec="bf16", vmem_mb=48, **pk)
    mixed, st_t, cst = _gla_conv_prompt(
        proj, bf(wgu_pad), b_gate_up[0].reshape(1, hk), g_head_norm[0].reshape(1, dv), w_conv[0],
        bsz=bsz, t=t, tt=256, heads=heads, dk=dk, dv=dv, cw=cw)
    st_t = st_t.reshape(bsz, heads // 2, 2, dv, 2, dk)
    gla_p = jnp.stack([st_t[:, :, 0, :, 0, :], st_t[:, :, 1, :, 1, :]], axis=2)
    gla_p = jnp.swapaxes(gla_p.reshape(bsz, heads, dv, dk), 2, 3)
    x1 = _outproj(mixed, bf(w_out0[0]), xp, mod_p[0], GT_M, gn(0, 1), tm=tm, prec="bf16",
                  glu=False, **pk)
    x2 = _ffn(x1, gn(0, 2), mod_p[0], SC_F, SH_F, GT_F, gn(0, 3), bf(w_ffn_gate[0]),
              bf(w_ffn_up[0]), bf(w_ffn_down[0]), tm=tm, fc=1408, prec="bf16", vmem_mb=48, **pk)
    u = _inproj(x2, gn(1, 0), mod_p[1], SC_M, SH_M, bf(w_in1[0]), tm=tm, bn=d, prec="bf16", **pk)
    yg, re_p, im_p = _s5_prompt(u.reshape(bsz, t, d), bf(bb_blk), bf(c_blk), ab_re, ab_im, d_blk,
                                tt=128)
    x3 = _outproj(yg.reshape(bsz * t, d), bf(w_glu[0]), x2, mod_p[1], GT_M, gn(1, 1), tm=tm,
                  prec="bf16", glu=True, vmem_mb=48, **pk)
    comb, hb = _router(x3, gn(1, 2), mod_p[1], SC_F, SH_F, wr_pad, tm=tm, n_experts=n_experts,
                       **pk)

    sk = dict(per_token=True, tiles_per_batch=1)
    xs = x_sample.reshape(bs, d)
    proj_s = _inproj(xs, gn(0, 0), mod_s[0], SC_M, SH_M, w0r, tm=bs, bn=640, prec="f32", **sk)
    s_t = jnp.transpose(state_gla[:, 0], (1, 2, 3, 0))
    cbuf_t = jnp.transpose(state_conv[:, 0], (1, 2, 0))
    o_t, y_t, sn_t, cn_t = _gla_conv_step(
        proj_s.T, wgu_pad.T, b_gate_up[0].reshape(hk, 1), g_head_norm[0].reshape(dv, 1),
        w_conv[0].reshape(3, cw, 1), s_t, cbuf_t, heads=heads, dk=dk, dv=dv, cw=cw)
    mixed_s = jnp.concatenate([o_t, y_t], axis=0).T
    gla_s = jnp.transpose(sn_t, (3, 0, 1, 2))
    conv_s = jnp.transpose(cn_t, (2, 0, 1))
    x1s = _outproj(mixed_s, w_out0[0], xs, mod_s[0], GT_M, gn(0, 1), tm=bs, prec="f32", glu=False,
                   **sk)
    x2s = _ffn(x1s, gn(0, 2), mod_s[0], SC_F, SH_F, GT_F, gn(0, 3), w_ffn_gate[0], w_ffn_up[0],
               w_ffn_down[0], tm=bs, fc=256, prec="f32", **sk)
    us = _inproj(x2s, gn(1, 0), mod_s[1], SC_M, SH_M, w_in1[0], tm=bs, bn=256, prec="f32", **sk)
    ygs, re_s, im_s = _s5_step(us, bb_blk, c_blk, ab_re, ab_im, d_blk,
                               state_s5_re[:, 0].reshape(bs, g_s5 * p_s5),
                               state_s5_im[:, 0].reshape(bs, g_s5 * p_s5))
    x3s = _outproj(ygs, w_glu[0], x2s, mod_s[1], GT_M, gn(1, 1), tm=bs, prec="f32", glu=True,
                   vmem_mb=48, **sk)
    comb_s, hb_s = _router(x3s, gn(1, 2), mod_s[1], SC_F, SH_F, wr_pad, tm=bs, n_experts=n_experts,
                           **sk)

    n_p = bsz * t
    pad = -(bs) % MOE_TB
    n_all = n_p + bs + pad
    n_blk = n_all // MOE_TB
    rows_blk = _moe_block_rows(n_experts)
    n_tiles_max = -(-(n_blk * rows_blk) // MOE_TM) + n_experts
    zrow = lambda w, dt: jnp.zeros((pad, w), dt)
    hb_all = jnp.concatenate([hb, hb_s, zrow(d, BF16)], axis=0)
    comb_all = jnp.concatenate([comb, comb_s, zrow(LANES, F32)], axis=0)
    sorted_pos, block_off, seg_cnt, tile_expert, n_tiles = _moe_schedule(comb_all, n_experts,
                                                                         n_tiles_max)
    seg = (sorted_pos, block_off, seg_cnt)
    dest, xs_sorted = _moe_sort(*seg, hb_all, comb_all, n_tiles_max * MOE_TM, n_experts=n_experts)
    ys_sorted = _experts(tile_expert, n_tiles, xs_sorted, w_eg, w_eu, w_ed, fc=896)
    x4 = _moe_combine(*seg, ys_sorted, comb_all, dest, x3, mod_p[1], GT_F, gn(1, 3), blk0=0,
                      n_experts=n_experts, per_token=False, tiles_per_batch=t // MOE_TB)
    pad_rows = lambda a: jnp.concatenate([a, jnp.zeros((pad, a.shape[1]), a.dtype)], axis=0)
    x4s = _moe_combine(*seg, ys_sorted, comb_all, dest, pad_rows(x3s), pad_rows(mod_s[1]), GT_F,
                       gn(1, 3), blk0=n_p // MOE_TB, n_experts=n_experts, per_token=True,
                       tiles_per_batch=1)[:bs]

    return (x4.reshape(bsz, t, d), x4s.reshape(bs, 1, d),
            gla_p[:, None], cst[:, None],
            re_p.reshape(bsz, 1, g_s5, p_s5), im_p.reshape(bsz, 1, g_s5, p_s5),
            gla_s[:, None], conv_s[:, None],
            re_s.reshape(bs, 1, g_s5, p_s5), im_s.reshape(bs, 1, g_s5, p_s5))
```

```python
import functools
import math

import jax
import jax.numpy as jnp
from jax import lax
from jax.experimental import pallas as pl
from jax.experimental.pallas import tpu as pltpu

F32 = jnp.float32
BF16 = jnp.bfloat16
EPS = 1e-6
LANES = 128
GLA_CHUNK = 64
GATE_NORMALIZER = 16.0
N_MOD = 6
S5_GROUP = 16
S5_CHUNK_GROUPS = 8


def _dot(a, b, dims=None):
    if dims is None:
        return jnp.dot(a, b, preferred_element_type=F32)
    return lax.dot_general(a, b, (dims, ((), ())), preferred_element_type=F32)


def _split(a, parts=2):
    rem = a.astype(F32)
    out = []
    for _ in range(parts - 1):
        piece = rem.astype(BF16)
        out.append(piece)
        rem = rem - piece.astype(F32)
    out.append(rem.astype(BF16))
    return out


def _mm(a, w, prec):
    if prec == "bf16":
        return _dot(a.astype(BF16), w.astype(BF16))
    a1, a2, a3 = _split(a, 3)
    w1, w2, w3 = _split(w, 3)
    small = (_dot(a1, w3) + _dot(a3, w1)) + _dot(a2, w2)
    return _dot(a1, w1) + ((_dot(a1, w2) + _dot(a2, w1)) + small)


def _silu(x):
    return x * jax.nn.sigmoid(x)


def _gelu_tanh(x):
    return 0.5 * x * (1.0 + jnp.tanh(math.sqrt(2.0 / math.pi) * (x + 0.044715 * (x * x * x))))


def _log_sigmoid(x):
    return -(jnp.maximum(-x, 0.0) + jnp.log1p(jnp.exp(-jnp.abs(x))))


def _rms(x):
    return x * lax.rsqrt(jnp.mean(x * x, axis=-1, keepdims=True) + EPS)


def _mod_row(ref):
    return ref[0] if len(ref.shape) == 3 else ref[...]


def _normmod(x, g, sc, sh):
    return (_rms(x) * g) * (1.0 + sc) + sh


def _mod_spec(per_token, tm, d, col, tiles_per_batch):
    if per_token:
        return pl.BlockSpec((tm, d), lambda i, *_: (i, col))
    return pl.BlockSpec((1, 1, d), lambda i, *_: (i // tiles_per_batch, 0, col))


def _params(sem, vmem_mb=None):
    kw = dict(dimension_semantics=sem)
    if vmem_mb is not None:
        kw["vmem_limit_bytes"] = vmem_mb << 20
    return pltpu.CompilerParams(**kw)


def _ada_kernel(c_ref, w_ref, b_ref, o_ref):
    o_ref[0] = _mm(_silu(c_ref[...]), w_ref[0], "f32") + b_ref[0]


def _ada(c_all, w_ada, b_ada):
    depth, d, n6 = w_ada.shape
    rows = c_all.shape[0]
    return pl.pallas_call(
        _ada_kernel,
        grid=(depth, n6 // d),
        in_specs=[pl.BlockSpec((rows, d), lambda l, j: (0, 0)),
                  pl.BlockSpec((1, d, d), lambda l, j: (l, 0, j)),
                  pl.BlockSpec((1, 1, d), lambda l, j: (l, 0, j))],
        out_specs=pl.BlockSpec((1, rows, d), lambda l, j: (l, 0, j)),
        out_shape=jax.ShapeDtypeStruct((depth, rows, n6), F32),
        compiler_params=_params(("parallel", "parallel")),
    )(c_all, w_ada, b_ada.reshape(depth, 1, n6))


def _inproj_kernel(x_ref, g_ref, sc_ref, sh_ref, w_ref, o_ref, *, prec):
    h = _normmod(x_ref[...], g_ref[...], _mod_row(sc_ref), _mod_row(sh_ref))
    o_ref[...] = _mm(h, w_ref[...], prec).astype(o_ref.dtype)


def _inproj(x2d, g, mod, sc_col, sh_col, w, *, tm, bn, prec, per_token, tiles_per_batch,
            out_dtype=F32, vmem_mb=None):
    n_tok, d = x2d.shape
    n_out = w.shape[1]
    return pl.pallas_call(
        functools.partial(_inproj_kernel, prec=prec),
        grid=(n_tok // tm, n_out // bn),
        in_specs=[pl.BlockSpec((tm, d), lambda i, j: (i, 0)),
                  pl.BlockSpec((1, d), lambda i, j: (0, 0)),
                  _mod_spec(per_token, tm, d, sc_col, tiles_per_batch),
                  _mod_spec(per_token, tm, d, sh_col, tiles_per_batch),
                  pl.BlockSpec((d, bn), lambda i, j: (0, j))],
        out_specs=pl.BlockSpec((tm, bn), lambda i, j: (i, j)),
        out_shape=jax.ShapeDtypeStruct((n_tok, n_out), out_dtype),
        compiler_params=_params(("parallel", "arbitrary"), vmem_mb),
    )(x2d, g, mod, mod, w)


def _outproj_kernel(a_ref, w_ref, x_ref, gt_ref, g_ref, o_ref, *, prec, glu):
    z = _mm(a_ref[...], w_ref[...], prec)
    if glu:
        d = z.shape[1] // 2
        z = z[:, :d] * jax.nn.sigmoid(z[:, d:])
    o_ref[...] = x_ref[...] + _mod_row(gt_ref) * (_rms(z) * g_ref[...])


def _outproj(a2d, w, x2d, mod, gt_col, g, *, tm, prec, glu, per_token, tiles_per_batch,
             vmem_mb=None):
    n_tok, d = x2d.shape
    k, n_out = w.shape
    return pl.pallas_call(
        functools.partial(_outproj_kernel, prec=prec, glu=glu),
        grid=(n_tok // tm,),
        in_specs=[pl.BlockSpec((tm, k), lambda i: (i, 0)),
                  pl.BlockSpec((k, n_out), lambda i: (0, 0)),
                  pl.BlockSpec((tm, d), lambda i: (i, 0)),
                  _mod_spec(per_token, tm, d, gt_col, tiles_per_batch),
                  pl.BlockSpec((1, d), lambda i: (0, 0))],
        out_specs=pl.BlockSpec((tm, d), lambda i: (i, 0)),
        out_shape=jax.ShapeDtypeStruct((n_tok, d), F32),
        compiler_params=_params(("parallel",), vmem_mb),
    )(a2d, w, x2d, mod, g)


def _ffn_kernel(x_ref, g1_ref, sc_ref, sh_ref, wg_ref, wu_ref, wd_ref, gt_ref, g2_ref,
                o_ref, h_scr, acc_scr, *, prec, n_c):
    c = pl.program_id(1)

    @pl.when(c == 0)
    def _():
        h = _normmod(x_ref[...], g1_ref[...], _mod_row(sc_ref), _mod_row(sh_ref))
        h_scr[...] = h.astype(h_scr.dtype)
        acc_scr[...] = jnp.zeros_like(acc_scr)

    h = h_scr[...]
    act = _silu(_mm(h, wg_ref[...], prec)) * _mm(h, wu_ref[...], prec)
    acc_scr[...] += _mm(act, wd_ref[...], prec)

    @pl.when(c == n_c - 1)
    def _():
        o_ref[...] = x_ref[...] + _mod_row(gt_ref) * (_rms(acc_scr[...]) * g2_ref[...])


def _ffn(x2d, g1, mod, sc_col, sh_col, gt_col, g2, wg, wu, wd, *, tm, fc, prec,
         per_token, tiles_per_batch, vmem_mb=None):
    n_tok, d = x2d.shape
    n_c = wg.shape[1] // fc
    tok = pl.BlockSpec((tm, d), lambda i, c: (i, 0))
    vec = pl.BlockSpec((1, d), lambda i, c: (0, 0))
    h_dtype = BF16 if prec == "bf16" else F32
    return pl.pallas_call(
        functools.partial(_ffn_kernel, prec=prec, n_c=n_c),
        grid=(n_tok // tm, n_c),
        in_specs=[tok, vec,
                  _mod_spec(per_token, tm, d, sc_col, tiles_per_batch),
                  _mod_spec(per_token, tm, d, sh_col, tiles_per_batch),
                  pl.BlockSpec((d, fc), lambda i, c: (0, c)),
                  pl.BlockSpec((d, fc), lambda i, c: (0, c)),
                  pl.BlockSpec((fc, d), lambda i, c: (c, 0)),
                  _mod_spec(per_token, tm, d, gt_col, tiles_per_batch), vec],
        out_specs=tok,
        out_shape=jax.ShapeDtypeStruct((n_tok, d), F32),
        scratch_shapes=[pltpu.VMEM((tm, d), h_dtype), pltpu.VMEM((tm, d), F32)],
        compiler_params=_params(("parallel", "arbitrary"), vmem_mb),
    )(x2d, g1, mod, mod, wg, wu, wd, mod, g2)


def _router_kernel(x_ref, g_ref, sc_ref, sh_ref, wr_ref, comb_ref, hb_ref, *, n_experts):
    h = _normmod(x_ref[...], g_ref[...], _mod_row(sc_ref), _mod_row(sh_ref))
    hb_ref[...] = h.astype(hb_ref.dtype)
    logits = _mm(h, wr_ref[...], "f32")
    lane = lax.broadcasted_iota(jnp.int32, logits.shape, 1).astype(F32)
    neg = -jnp.inf
    l1 = jnp.where(lane < n_experts, logits, neg)
    m1 = jnp.max(l1, axis=-1, keepdims=True)
    i1 = jnp.min(jnp.where(l1 == m1, lane, float(LANES)), axis=-1, keepdims=True)
    l2 = jnp.where(lane == i1, neg, l1)
    m2 = jnp.max(l2, axis=-1, keepdims=True)
    i2 = jnp.min(jnp.where(l2 == m2, lane, float(LANES)), axis=-1, keepdims=True)
    e2 = jnp.exp(m2 - m1)
    den = 1.0 + e2
    comb_ref[...] = jnp.where(lane == i1, 1.0 / den, 0.0) + jnp.where(lane == i2, e2 / den, 0.0)


def _router(x2d, g, mod, sc_col, sh_col, wr_pad, *, tm, n_experts, per_token, tiles_per_batch):
    n_tok, d = x2d.shape
    return pl.pallas_call(
        functools.partial(_router_kernel, n_experts=n_experts),
        grid=(n_tok // tm,),
        in_specs=[pl.BlockSpec((tm, d), lambda i: (i, 0)),
                  pl.BlockSpec((1, d), lambda i: (0, 0)),
                  _mod_spec(per_token, tm, d, sc_col, tiles_per_batch),
                  _mod_spec(per_token, tm, d, sh_col, tiles_per_batch),
                  pl.BlockSpec((d, LANES), lambda i: (0, 0))],
        out_specs=[pl.BlockSpec((tm, LANES), lambda i: (i, 0)),
                   pl.BlockSpec((tm, d), lambda i: (i, 0))],
        out_shape=[jax.ShapeDtypeStruct((n_tok, LANES), F32),
                   jax.ShapeDtypeStruct((n_tok, d), BF16)],
        compiler_params=_params(("parallel",)),
    )(x2d, g, mod, mod, wr_pad)


MOE_TB = 256
MOE_SEG = 16
MOE_TM = 512


def _moe_block_rows(n_experts):
    return -(-(2 * MOE_TB + n_experts * (MOE_SEG - 1)) // LANES) * LANES


def _scatter_matrix(dest, comb, weighted, n_rows, n_experts):
    tb = dest.shape[0]
    lane = lax.broadcasted_iota(jnp.int32, (tb, n_rows), 1).astype(F32)
    m = jnp.zeros((tb, n_rows), F32)
    for e in range(n_experts):
        w = comb[:, e:e + 1]
        val = jnp.where(w > 0.0, w if weighted else 1.0, 0.0)
        m = m + jnp.where(lane == dest[:, e:e + 1], val, 0.0)
    return m


_SEG_BITS = (256, 128, 64, 32, 16)


def _segment_copies(make_copy, blk, pos_ref, off_ref, cnt_ref, n_experts, start):
    for e in range(n_experts):
        seg = blk * n_experts + e
        in_blk, in_sorted, n = off_ref[seg], pos_ref[seg], cnt_ref[seg]
        for size in _SEG_BITS:
            has = (n & size) != 0

            @pl.when(has)
            def _(in_blk=in_blk, in_sorted=in_sorted, size=size):
                cp = make_copy(pl.multiple_of(in_blk, MOE_SEG), pl.multiple_of(in_sorted, MOE_SEG), size)
                cp.start() if start else cp.wait()

            step = jnp.where(has, size, 0)
            in_blk, in_sorted = in_blk + step, in_sorted + step


def _moe_sort_kernel(pos_ref, off_ref, cnt_ref, hb_ref, comb_ref, init_hbm, dest_ref, xs_hbm,
                     buf, sems, *, n_rows, n_experts, n_blk):
    del init_hbm
    b = pl.program_id(0)
    slot = b % 2

    def writes(blk, slot, start):
        def make_copy(block_row, sorted_row, size):
            return pltpu.make_async_copy(buf.at[slot, pl.ds(block_row, size)],
                                         xs_hbm.at[pl.ds(sorted_row, size)], sems.at[slot])
        _segment_copies(make_copy, blk, pos_ref, off_ref, cnt_ref, n_experts, start)

    @pl.when(b >= 2)
    def _():
        writes(b - 2, slot, False)

    comb = comb_ref[...]
    tb = comb.shape[0]
    sel = jnp.where(comb > 0.0, 1.0, 0.0)
    r = lax.broadcasted_iota(jnp.int32, (tb, tb), 0)
    c = lax.broadcasted_iota(jnp.int32, (tb, tb), 1)
    rank = _dot(jnp.where(c < r, 1.0, 0.0).astype(BF16), sel.astype(BF16))
    cnt = jnp.sum(sel, axis=0, keepdims=True)
    cnt = jnp.floor((cnt + (MOE_SEG - 1)) / MOE_SEG) * MOE_SEG
    ru = lax.broadcasted_iota(jnp.int32, (LANES, LANES), 0)
    cu = lax.broadcasted_iota(jnp.int32, (LANES, LANES), 1)
    off = _dot(jnp.broadcast_to(cnt, (8, LANES)).astype(BF16),
               jnp.where(ru < cu, 1.0, 0.0).astype(BF16))[0:1]
    dest = off + rank
    dest_ref[...] = dest
    pt = _scatter_matrix(dest, comb, False, n_rows, n_experts).astype(BF16)
    buf[slot] = _dot(pt, hb_ref[...], ((0,), (0,))).astype(buf.dtype)
    writes(b, slot, True)

    @pl.when(b == n_blk - 1)
    def _():
        writes(b, slot, False)
        if n_blk > 1:
            writes(b - 1, 1 - slot, False)


def _moe_sort(sorted_pos, block_off, seg_cnt, hb, comb, n_sorted_rows, *, n_experts):
    n_tok, d = hb.shape
    n_blk = n_tok // MOE_TB
    n_rows = _moe_block_rows(n_experts)
    init = jnp.zeros((n_sorted_rows, d), BF16)
    any_spec = pl.BlockSpec(memory_space=pl.ANY)
    return pl.pallas_call(
        functools.partial(_moe_sort_kernel, n_rows=n_rows, n_experts=n_experts, n_blk=n_blk),
        grid_spec=pltpu.PrefetchScalarGridSpec(
            num_scalar_prefetch=3, grid=(n_blk,),
            in_specs=[pl.BlockSpec((MOE_TB, d), lambda b, *_: (b, 0)),
                      pl.BlockSpec((MOE_TB, LANES), lambda b, *_: (b, 0)),
                      any_spec],
            out_specs=[pl.BlockSpec((MOE_TB, LANES), lambda b, *_: (b, 0)), any_spec],
            scratch_shapes=[pltpu.VMEM((2, n_rows, d), BF16), pltpu.SemaphoreType.DMA((2,))]),
        out_shape=[jax.ShapeDtypeStruct((n_tok, LANES), F32),
                   jax.ShapeDtypeStruct((n_sorted_rows, d), BF16)],
        input_output_aliases={5: 1},
        compiler_params=_params(("arbitrary",)),
    )(sorted_pos, block_off, seg_cnt, hb, comb, init)


def _experts_kernel(te_ref, nt_ref, x_ref, wg_ref, wu_ref, wd_ref, o_ref, acc_scr, *, n_c):
    del te_ref
    k = pl.program_id(0)
    c = pl.program_id(1)
    active = k < nt_ref[0]

    @pl.when(active)
    def _():
        h = x_ref[...]
        act = _silu(_dot(h, wg_ref[0])) * _dot(h, wu_ref[0])
        y = _dot(act.astype(BF16), wd_ref[0])

        @pl.when(c == 0)
        def _():
            acc_scr[...] = y

        @pl.when(c > 0)
        def _():
            acc_scr[...] += y

    @pl.when(c == n_c - 1)
    def _():
        o_ref[...] = jnp.where(active, acc_scr[...], 0.0).astype(o_ref.dtype)


def _experts(tile_expert, n_tiles, xs, wg, wu, wd, *, fc):
    n_rows, d = xs.shape
    n_c = wg.shape[2] // fc

    def w_map(k, c, te, nt):
        return (te[k], 0, jnp.where(k < nt[0], c, n_c - 1))

    def wd_map(k, c, te, nt):
        return (te[k], jnp.where(k < nt[0], c, n_c - 1), 0)

    return pl.pallas_call(
        functools.partial(_experts_kernel, n_c=n_c),
        grid_spec=pltpu.PrefetchScalarGridSpec(
            num_scalar_prefetch=2, grid=(n_rows // MOE_TM, n_c),
            in_specs=[pl.BlockSpec((MOE_TM, d), lambda k, c, te, nt: (k, 0)),
                      pl.BlockSpec((1, d, fc), w_map), pl.BlockSpec((1, d, fc), w_map),
                      pl.BlockSpec((1, fc, d), wd_map)],
            out_specs=pl.BlockSpec((MOE_TM, d), lambda k, c, te, nt: (k, 0)),
            scratch_shapes=[pltpu.VMEM((MOE_TM, d), F32)]),
        out_shape=jax.ShapeDtypeStruct((n_rows, d), BF16),
        compiler_params=_params(("arbitrary", "arbitrary"), 48),
    )(tile_expert, n_tiles, xs, wg, wu, wd)


def _moe_combine_kernel(pos_ref, off_ref, cnt_ref, ys_hbm, comb_ref, dest_ref, x_ref, gt_ref, g_ref,
                        o_ref, buf, sems, *, n_experts, n_blk, blk0):
    b = pl.program_id(0)
    slot = b % 2

    def reads(blk, slot, start):
        def make_copy(block_row, sorted_row, size):
            return pltpu.make_async_copy(ys_hbm.at[pl.ds(sorted_row, size)],
                                         buf.at[slot, pl.ds(block_row, size)], sems.at[slot])
        _segment_copies(make_copy, blk0 + blk, pos_ref, off_ref, cnt_ref, n_experts, start)

    @pl.when(b == 0)
    def _():
        buf[...] = jnp.zeros_like(buf)
        reads(b, slot, True)

    if n_blk > 1:
        @pl.when(b + 1 < n_blk)
        def _():
            reads(b + 1, 1 - slot, True)

    reads(b, slot, False)
    yb = buf[slot]
    wt = _scatter_matrix(dest_ref[...], comb_ref[...], True, yb.shape[0], n_experts)
    w_hi, w_lo = _split(wt)
    y = _dot(w_hi, yb) + _dot(w_lo, yb)
    o_ref[...] = x_ref[...] + _mod_row(gt_ref) * (_rms(y) * g_ref[...])


def _moe_combine(sorted_pos, block_off, seg_cnt, ys, comb, dest, x2d, mod, gt_col, g, *, blk0,
                 n_experts, per_token, tiles_per_batch):
    n_tok, d = x2d.shape
    n_blk = n_tok // MOE_TB
    n_rows = _moe_block_rows(n_experts)
    return pl.pallas_call(
        functools.partial(_moe_combine_kernel, n_experts=n_experts, n_blk=n_blk, blk0=blk0),
        grid_spec=pltpu.PrefetchScalarGridSpec(
            num_scalar_prefetch=3, grid=(n_blk,),
            in_specs=[pl.BlockSpec(memory_space=pl.ANY),
                      pl.BlockSpec((MOE_TB, LANES), lambda b, *_: (blk0 + b, 0)),
                      pl.BlockSpec((MOE_TB, LANES), lambda b, *_: (blk0 + b, 0)),
                      pl.BlockSpec((MOE_TB, d), lambda b, *_: (b, 0)),
                      _mod_spec(per_token, MOE_TB, d, gt_col, tiles_per_batch),
                      pl.BlockSpec((1, d), lambda b, *_: (0, 0))],
            out_specs=pl.BlockSpec((MOE_TB, d), lambda b, *_: (b, 0)),
            scratch_shapes=[pltpu.VMEM((2, n_rows, d), BF16), pltpu.SemaphoreType.DMA((2,))]),
        out_shape=jax.ShapeDtypeStruct((n_tok, d), F32),
        compiler_params=_params(("arbitrary",)),
    )(sorted_pos, block_off, seg_cnt, ys, comb, dest, x2d, mod, g)


def _moe_schedule(comb, n_experts, n_tiles_max):
    n_blk = comb.shape[0] // MOE_TB
    cnt = jnp.sum((comb[:, :n_experts] > 0.0).reshape(n_blk, MOE_TB, n_experts).astype(jnp.int32),
                  axis=1)
    cnt = -(-cnt // MOE_SEG) * MOE_SEG
    off = jnp.cumsum(cnt, axis=1) - cnt
    tot = jnp.sum(cnt, axis=0)
    grp = -(-tot // MOE_TM) * MOE_TM
    g_end = jnp.cumsum(grp)
    sorted_pos = (g_end - grp)[None, :] + (jnp.cumsum(cnt, axis=0) - cnt)
    n_tiles = g_end[-1] // MOE_TM
    first_row = jnp.arange(n_tiles_max, dtype=jnp.int32) * MOE_TM
    tile_expert = jnp.sum(first_row[:, None] >= g_end[None, :], axis=1)
    last = jnp.sum((n_tiles - 1) * MOE_TM >= g_end)
    tile_expert = jnp.minimum(tile_expert, last).astype(jnp.int32)
    flat = lambda a: a.reshape(-1).astype(jnp.int32)
    return (flat(sorted_pos), flat(off), flat(cnt), tile_expert,
            n_tiles.reshape(1).astype(jnp.int32))


def _gla_conv_kernel(p_ref, wgu_ref, bgu_ref, gh_ref, wc_ref, mixed_ref, st_ref, cst_ref,
                     s_scr, uext_scr, *, tt, n_t, heads, dk, dv, cw):
    ti = pl.program_id(1)
    hk = heads * dk
    hv = heads * dv
    o_q, o_k, o_v, o_og = 0, hk, 2 * hk, 2 * hk + hv
    o_ch = o_og + hv
    o_cb, o_cc, o_lr = o_ch + cw, o_ch + 2 * cw, o_ch + 3 * cw
    L = GLA_CHUNK

    @pl.when(ti == 0)
    def _():
        s_scr[...] = jnp.zeros_like(s_scr)
        uext_scr[0:8, :] = jnp.zeros((8, cw), F32)

    logit = _dot(p_ref[:, o_lr:o_lr + LANES].astype(BF16), wgu_ref[...]) + bgu_ref[...]
    logg = _log_sigmoid(logit) / GATE_NORMALIZER

    r = lax.broadcasted_iota(jnp.int32, (tt, tt), 0)
    c = lax.broadcasted_iota(jnp.int32, (tt, tt), 1)
    tri = jnp.where((r // L == c // L) & (c <= r), 1.0, 0.0).astype(BF16)
    g_hi = logg.astype(BF16)
    rem = logg - g_hi.astype(F32)
    g_mid = rem.astype(BF16)
    g_lo = (rem - g_mid.astype(F32)).astype(BF16)
    bc = _dot(tri, g_hi) + (_dot(tri, g_mid) + _dot(tri, g_lo))

    rl = lax.broadcasted_iota(jnp.int32, (L, L), 0)
    cl = lax.broadcasted_iota(jnp.int32, (L, L), 1)
    tril = cl <= rl
    lane = lax.broadcasted_iota(jnp.int32, (L, LANES), 1)
    heads_per_blk = LANES // dk
    nt_dims = ((1,), (1,))

    for ck in range(tt // L):
        rows = slice(L * ck, L * ck + L)
        b = bc[rows]
        bl = b[L - 1:L]
        q_in = (p_ref[rows, o_q:o_q + hk] * (dk ** -0.5)) * jnp.exp(b)
        k = p_ref[rows, o_k:o_k + hk]
        k_out = (k * jnp.exp(-b)).astype(BF16)
        k_dec = (k * jnp.exp(bl - b)).astype(BF16)
        dec = jnp.exp(bl)
        q_in = q_in.astype(BF16)
        for h in range(heads):
            blk = slice(LANES * (h // heads_per_blk), LANES * (h // heads_per_blk) + LANES)
            in_head = (lane // dk) == (h % heads_per_blk)
            qm = jnp.where(in_head, q_in[:, blk], jnp.zeros_like(q_in[:, blk]))
            att = _dot(qm, k_out[:, blk], nt_dims)
            att = jnp.where(tril, att, 0.0).astype(BF16)
            vh = p_ref[rows, o_v + dv * h:o_v + dv * h + dv].astype(BF16)
            s_t = s_scr[h]
            o = _dot(att, vh) + _dot(qm, s_t.astype(BF16), nt_dims)
            ds_t = _dot(vh, k_dec[:, blk], ((0,), (0,)))
            s_scr[h] = s_t * dec[:, blk] + ds_t
            og = p_ref[rows, o_og + dv * h:o_og + dv * h + dv]
            res = (_rms(o) * gh_ref[...]) * _silu(og)
            mixed_ref[rows, dv * h:dv * h + dv] = res.astype(mixed_ref.dtype)

    u = p_ref[:, o_cc:o_cc + cw] * p_ref[:, o_ch:o_ch + cw]
    uext_scr[8:8 + tt, :] = u
    y = (wc_ref[0:1, :] * uext_scr[6:6 + tt, :] + wc_ref[1:2, :] * uext_scr[7:7 + tt, :]
         + wc_ref[2:3, :] * u)
    mixed_ref[:, hv:hv + cw] = (p_ref[:, o_cb:o_cb + cw] * y).astype(mixed_ref.dtype)
    tail = uext_scr[tt + 6:tt + 8, :]
    uext_scr[6:8, :] = tail

    @pl.when(ti == n_t - 1)
    def _():
        st_ref[0] = s_scr[...]
        cst_ref[0] = tail


def _gla_conv_prompt(proj, wgu_pad, bgu, g_head, w_conv, *, bsz, t, tt, heads, dk, dv, cw):
    n_t = t // tt
    n_in = proj.shape[1]
    width = heads * dv + cw
    return pl.pallas_call(
        functools.partial(_gla_conv_kernel, tt=tt, n_t=n_t, heads=heads, dk=dk, dv=dv, cw=cw),
        grid=(bsz, n_t),
        in_specs=[pl.BlockSpec((tt, n_in), lambda b, i: (b * n_t + i, 0)),
                  pl.BlockSpec(wgu_pad.shape, lambda b, i: (0, 0)),
                  pl.BlockSpec(bgu.shape, lambda b, i: (0, 0)),
                  pl.BlockSpec(g_head.shape, lambda b, i: (0, 0)),
                  pl.BlockSpec(w_conv.shape, lambda b, i: (0, 0))],
        out_specs=[pl.BlockSpec((tt, width), lambda b, i: (b * n_t + i, 0)),
                   pl.BlockSpec((1, heads, dv, LANES), lambda b, i: (b, 0, 0, 0)),
                   pl.BlockSpec((1, 2, cw), lambda b, i: (b, 0, 0))],
        out_shape=[jax.ShapeDtypeStruct((bsz * t, width), BF16),
                   jax.ShapeDtypeStruct((bsz, heads, dv, LANES), F32),
                   jax.ShapeDtypeStruct((bsz, 2, cw), F32)],
        scratch_shapes=[pltpu.VMEM((heads, dv, LANES), F32), pltpu.VMEM((tt + 8, cw), F32)],
        compiler_params=_params(("parallel", "arbitrary"), 48),
    )(proj, wgu_pad, bgu, g_head, w_conv)


def _gla_conv_step_kernel(q_ref, k_ref, v_ref, og_ref, ch_ref, cb_ref, cc_ref, lr_ref,
                          wgu_ref, bgu_ref, gh_ref, wc_ref, s_ref, cbuf_ref,
                          o_ref, y_ref, sn_ref, cn_ref, a_scr, *, dk):
    logit = _mm(wgu_ref[...], lr_ref[...], "f32") + bgu_ref[...]
    a_scr[...] = jnp.exp(_log_sigmoid(logit) / GATE_NORMALIZER)
    v_t = v_ref[...]

    def body(d, acc):
        a = a_scr[pl.ds(d, 1), :]
        kd = k_ref[pl.ds(d, 1), :]
        qd = q_ref[pl.ds(d, 1), :]
        s_new = a * s_ref[0, d] + kd * v_t
        sn_ref[0, d] = s_new
        return acc + (qd * (dk ** -0.5)) * s_new

    o = lax.fori_loop(0, dk, body, jnp.zeros(v_t.shape, F32))
    o = o * lax.rsqrt(jnp.mean(o * o, axis=0, keepdims=True) + EPS) * gh_ref[...]
    o_ref[...] = o * _silu(og_ref[...])
    u = cc_ref[...] * ch_ref[...]
    y = wc_ref[0] * cbuf_ref[0] + wc_ref[1] * cbuf_ref[1] + wc_ref[2] * u
    y_ref[...] = cb_ref[...] * y
    cn_ref[0] = cbuf_ref[1]
    cn_ref[1] = u


def _gla_conv_step(proj_t, wgu_t_pad, bgu_col, gh_col, wc_col, s_t, cbuf_t, *, heads, dk, dv, cw):
    bsz = proj_t.shape[1]
    hk, hv = heads * dk, heads * dv
    cs = cw // heads
    assert dv == LANES and cs == LANES and dk * 2 == LANES
    o_k, o_v, o_og = hk, 2 * hk, 2 * hk + hv
    o_ch = o_og + hv
    o_cb, o_cc, o_lr = o_ch + cw, o_ch + 2 * cw, o_ch + 3 * cw
    row = lambda off, size: (lambda h: (off // size + h, 0))
    blk = lambda size, off: pl.BlockSpec((size, bsz), row(off, size))
    return pl.pallas_call(
        functools.partial(_gla_conv_step_kernel, dk=dk),
        grid=(heads,),
        in_specs=[blk(dk, 0), blk(dk, o_k), blk(dv, o_v), blk(dv, o_og),
                  blk(cs, o_ch), blk(cs, o_cb), blk(cs, o_cc),
                  pl.BlockSpec((LANES, bsz), lambda h: (o_lr // LANES, 0)),
                  pl.BlockSpec((dk, LANES), lambda h: (h, 0)),
                  pl.BlockSpec((dk, 1), lambda h: (h, 0)),
                  pl.BlockSpec((dv, 1), lambda h: (0, 0)),
                  pl.BlockSpec((3, cs, 1), lambda h: (0, h, 0)),
                  pl.BlockSpec((1, dk, dv, bsz), lambda h: (h, 0, 0, 0)),
                  pl.BlockSpec((2, cs, bsz), lambda h: (0, h, 0))],
        out_specs=[pl.BlockSpec((dv, bsz), lambda h: (h, 0)),
                   pl.BlockSpec((cs, bsz), lambda h: (h, 0)),
                   pl.BlockSpec((1, dk, dv, bsz), lambda h: (h, 0, 0, 0)),
                   pl.BlockSpec((2, cs, bsz), lambda h: (0, h, 0))],
        out_shape=[jax.ShapeDtypeStruct((hv, bsz), F32),
                   jax.ShapeDtypeStruct((cw, bsz), F32),
                   jax.ShapeDtypeStruct(s_t.shape, F32),
                   jax.ShapeDtypeStruct(cbuf_t.shape, F32)],
        scratch_shapes=[pltpu.VMEM((dk, bsz), F32)],
        compiler_params=_params(("parallel",)),
    )(proj_t, proj_t, proj_t, proj_t, proj_t, proj_t, proj_t, proj_t,
      wgu_t_pad, bgu_col, gh_col, wc_col, s_t, cbuf_t)


S5_L = 16
S5_STEP_GROUPS = 8


def _s5_chunk_kernel(u_ref, t_ref, e_ref, f_ref, alr_ref, ali_ref, y_ref, sre_ref, sim_ref,
                     xs_scr, dx_scr, *, n_chunk, nb):
    n_pairs = e_ref.shape[0]
    gw = t_ref.shape[1]
    for pr in range(n_pairs):
        dx_scr[pr] = _dot(u_ref[:, 2 * gw * pr:2 * gw * (pr + 1)], e_ref[pr])
    a_re = [jnp.broadcast_to(alr_ref[pr], (nb, LANES)) for pr in range(n_pairs)]
    a_im = [jnp.broadcast_to(ali_ref[pr], (nb, LANES)) for pr in range(n_pairs)]

    def body(n, carry):
        rows = pl.ds(pl.multiple_of(n * nb, nb), nb)
        out = []
        for pr in range(n_pairs):
            xr, xi = carry[2 * pr], carry[2 * pr + 1]
            xs_scr[pr, rows, 0:LANES] = xr
            xs_scr[pr, rows, LANES:2 * LANES] = xi
            out.append((a_re[pr] * xr - a_im[pr] * xi) + dx_scr[pr, rows, 0:LANES])
            out.append((a_re[pr] * xi + a_im[pr] * xr) + dx_scr[pr, rows, LANES:2 * LANES])
        return tuple(out)

    zero = jnp.zeros((nb, LANES), F32)
    fin = lax.fori_loop(0, n_chunk, body, (zero,) * (2 * n_pairs))
    for pr in range(n_pairs):
        y0 = _dot(xs_scr[pr].astype(BF16), f_ref[pr])
        for gl in range(2):
            cols = slice((2 * pr + gl) * gw, (2 * pr + gl + 1) * gw)
            y = _dot(u_ref[:, cols], t_ref[2 * pr + gl]) + y0[:, gl * gw:(gl + 1) * gw]
            y_ref[:, cols] = _gelu_tanh(y).astype(y_ref.dtype)
        sre_ref[:, pr * LANES:(pr + 1) * LANES] = fin[2 * pr]
        sim_ref[:, pr * LANES:(pr + 1) * LANES] = fin[2 * pr + 1]


def _s5_prompt(u2, t_mat, e_mat, f_mat, al_re, al_im, *, nb):
    rows, width = u2.shape
    gw = t_mat.shape[1]
    n_g = width // gw
    sg = S5_STEP_GROUPS
    n_j = n_g // sg
    return pl.pallas_call(
        functools.partial(_s5_chunk_kernel, n_chunk=rows // nb, nb=nb),
        grid=(n_j,),
        in_specs=[pl.BlockSpec((rows, sg * gw), lambda j: (0, j)),
                  pl.BlockSpec((sg, gw, gw), lambda j: (j, 0, 0)),
                  pl.BlockSpec((sg // 2, 2 * gw, 2 * LANES), lambda j: (j, 0, 0)),
                  pl.BlockSpec((sg // 2, 2 * LANES, 2 * gw), lambda j: (j, 0, 0)),
                  pl.BlockSpec((sg // 2, 1, LANES), lambda j: (j, 0, 0)),
                  pl.BlockSpec((sg // 2, 1, LANES), lambda j: (j, 0, 0))],
        out_specs=[pl.BlockSpec((rows, sg * gw), lambda j: (0, j)),
                   pl.BlockSpec((nb, sg // 2 * LANES), lambda j: (0, j)),
                   pl.BlockSpec((nb, sg // 2 * LANES), lambda j: (0, j))],
        out_shape=[jax.ShapeDtypeStruct((rows, width), BF16),
                   jax.ShapeDtypeStruct((nb, n_g // 2 * LANES), F32),
                   jax.ShapeDtypeStruct((nb, n_g // 2 * LANES), F32)],
        scratch_shapes=[pltpu.VMEM((sg // 2, rows, 2 * LANES), F32),
                        pltpu.VMEM((sg // 2, rows, 2 * LANES), F32)],
        compiler_params=_params(("parallel",), 48),
    )(u2, t_mat, e_mat, f_mat, al_re, al_im)


def _s5_chunk_mats(a_re, a_im, log_dt, bb_re, bb_im, c_re, c_im, d_skip):
    g, p = a_re.shape
    n_i = S5_GROUP
    hp = lax.Precision.HIGHEST
    dt = jnp.exp(log_dt)[:, None, None]
    m = jnp.arange(S5_L + 1, dtype=F32)[None, :, None]
    mag = jnp.exp(m * dt * a_re[:, None, :])
    pw_re = mag * jnp.cos(m * dt * a_im[:, None, :])
    pw_im = mag * jnp.sin(m * dt * a_im[:, None, :])
    cl_re = c_re[:, None] * pw_re[:, :, None, :] - c_im[:, None] * pw_im[:, :, None, :]
    cl_im = c_re[:, None] * pw_im[:, :, None, :] + c_im[:, None] * pw_re[:, :, None, :]
    k = (jnp.einsum("gmip,gpj->gmij", cl_re[:, :S5_L], bb_re, precision=hp)
         - jnp.einsum("gmip,gpj->gmij", cl_im[:, :S5_L], bb_im, precision=hp))
    k = k.at[:, 0].add(d_skip[:, :, None] * jnp.eye(n_i, dtype=F32))
    s_idx = jnp.arange(S5_L)[:, None]
    j_idx = jnp.arange(S5_L)[None, :]
    lag = jnp.clip(j_idx - s_idx, 0, S5_L - 1)
    t_mat = jnp.where((j_idx >= s_idx)[None, :, :, None, None], k[:, lag], 0.0)
    t_mat = jnp.transpose(t_mat, (0, 1, 4, 2, 3)).reshape(g, S5_L * n_i, S5_L * n_i)
    rev_re, rev_im = pw_re[:, S5_L - 1::-1][:, :S5_L], pw_im[:, S5_L - 1::-1][:, :S5_L]
    e_re = rev_re[:, :, None, :] * jnp.swapaxes(bb_re, 1, 2)[:, None] - rev_im[:, :, None, :] * jnp.swapaxes(bb_im, 1, 2)[:, None]
    e_im = rev_re[:, :, None, :] * jnp.swapaxes(bb_im, 1, 2)[:, None] + rev_im[:, :, None, :] * jnp.swapaxes(bb_re, 1, 2)[:, None]
    eye2 = jnp.eye(2, dtype=F32)

    def pair_rows(x):
        r, c = x.shape[1:]
        return jnp.einsum("qgrc,gh->qgrhc", x.reshape(g // 2, 2, r, c), eye2).reshape(g // 2, 2 * r, 2 * c)

    e_mat = jnp.concatenate([pair_rows(e_re.reshape(g, S5_L * n_i, p)),
                             pair_rows(e_im.reshape(g, S5_L * n_i, p))], axis=2)
    f_re = jnp.transpose(cl_re[:, 1:], (0, 3, 1, 2)).reshape(g, p, S5_L * n_i)
    f_im = jnp.transpose(-cl_im[:, 1:], (0, 3, 1, 2)).reshape(g, p, S5_L * n_i)
    f_mat = jnp.concatenate([pair_rows(f_re), pair_rows(f_im)], axis=1)
    al_re = pw_re[:, S5_L].reshape(g // 2, 1, 2 * p)
    al_im = pw_im[:, S5_L].reshape(g // 2, 1, 2 * p)
    return t_mat, e_mat, f_mat, al_re, al_im


def _s5_step_kernel(u_ref, bb_ref, cm_ref, abr_ref, abi_ref, d_ref, sre_ref, sim_ref,
                    y_ref, nre_ref, nim_ref, *, half):
    u = u_ref[...]
    bu = _mm(u, bb_ref[0], "f32")
    ar, ai = abr_ref[0], abi_ref[0]
    sr, si = sre_ref[...], sim_ref[...]
    nr = (ar * sr - ai * si) + bu[:, 0:half]
    ni = (ar * si + ai * sr) + bu[:, half:2 * half]
    nre_ref[...] = nr
    nim_ref[...] = ni
    y = _mm(jnp.concatenate([nr, ni], axis=1), cm_ref[0], "f32") + d_ref[0] * u
    y_ref[...] = _gelu_tanh(y)


def _s5_step(u2d, bb_blk, c_blk, ab_re, ab_im, d_blk, s_re, s_im):
    bsz, d = u2d.shape
    n_j = d // LANES
    half = bb_blk.shape[2] // 2
    st = pl.BlockSpec((bsz, half), lambda j: (0, j))
    return pl.pallas_call(
        functools.partial(_s5_step_kernel, half=half),
        grid=(n_j,),
        in_specs=[pl.BlockSpec((bsz, LANES), lambda j: (0, j)),
                  pl.BlockSpec((1, LANES, 2 * half), lambda j: (j, 0, 0)),
                  pl.BlockSpec((1, 2 * half, LANES), lambda j: (j, 0, 0)),
                  pl.BlockSpec((1, 1, half), lambda j: (j, 0, 0)),
                  pl.BlockSpec((1, 1, half), lambda j: (j, 0, 0)),
                  pl.BlockSpec((1, 1, LANES), lambda j: (j, 0, 0)),
                  st, st],
        out_specs=[pl.BlockSpec((bsz, LANES), lambda j: (0, j)), st, st],
        out_shape=[jax.ShapeDtypeStruct((bsz, d), F32),
                   jax.ShapeDtypeStruct(s_re.shape, F32),
                   jax.ShapeDtypeStruct(s_im.shape, F32)],
        compiler_params=_params(("parallel",)),
    )(u2d, bb_blk, c_blk, ab_re, ab_im, d_blk, s_re, s_im)


def _s5_zoh(a_re, a_im, log_dt, b_re, b_im):
    dt = jnp.exp(log_dt)[:, None]
    mag = jnp.exp(dt * a_re)
    ab_re, ab_im = mag * jnp.cos(dt * a_im), mag * jnp.sin(dt * a_im)
    den = a_re * a_re + a_im * a_im
    nr, ni = ab_re - 1.0, ab_im
    f_re = (nr * a_re + ni * a_im) / den
    f_im = (ni * a_re - nr * a_im) / den
    bb_re = f_re[..., None] * b_re - f_im[..., None] * b_im
    bb_im = f_re[..., None] * b_im + f_im[..., None] * b_re
    return ab_re, ab_im, bb_re, bb_im


def _s5_step_blocks(ab_re, ab_im, bb_re, bb_im, c_re, c_im, d_skip):
    g, p = ab_re.shape
    cg = S5_CHUNK_GROUPS
    n_j = g // cg
    eye = jnp.eye(cg, dtype=F32)

    def in_blk(m):
        m = m.reshape(n_j, cg, p, S5_GROUP)
        return jnp.einsum("jgpi,gh->jgihp", m, eye).reshape(n_j, cg * S5_GROUP, cg * p)

    def out_blk(m):
        m = m.reshape(n_j, cg, S5_GROUP, p)
        return jnp.einsum("jgip,gh->jgphi", m, eye).reshape(n_j, cg * p, cg * S5_GROUP)

    bb_blk = jnp.concatenate([in_blk(bb_re), in_blk(bb_im)], axis=2)
    c_blk = jnp.concatenate([out_blk(c_re), out_blk(-c_im)], axis=1)
    return (bb_blk, c_blk, ab_re.reshape(n_j, 1, cg * p), ab_im.reshape(n_j, 1, cg * p),
            d_skip.reshape(n_j, 1, cg * S5_GROUP))


def kernel(x_prompt, x_sample, c_prompt, c_sample, state_gla, state_conv, state_s5_re, state_s5_im, w_ada, b_ada, g_norm, w_in0, w_gate_up, b_gate_up, g_head_norm, w_conv, w_out0, w_ffn_gate, w_ffn_up, w_ffn_down, w_in1, s5_a_re, s5_a_im, s5_log_dt, s5_b_re, s5_b_im, s5_c_re, s5_c_im, s5_d, w_glu, w_router, w_exp_gate, w_exp_up, w_exp_down):
    bsz, t, d = x_prompt.shape
    bs = x_sample.shape[0]
    heads, dk, dv = state_gla.shape[2:]
    cw = state_conv.shape[3]
    lowrank = w_gate_up.shape[1]
    hk, hv = heads * dk, heads * dv
    n_experts = w_router.shape[2]
    g_s5, p_s5 = s5_a_re.shape[1:]

    mod = _ada(jnp.concatenate([c_prompt, c_sample], axis=0), w_ada, b_ada)
    mod_p = [mod[l, :bsz].reshape(bsz, 1, N_MOD * d) for l in range(2)]
    mod_s = [mod[l, bsz:] for l in range(2)]
    SH_M, SC_M, GT_M, SH_F, SC_F, GT_F = range(N_MOD)
    gn = lambda l, k: g_norm[l, k].reshape(1, d)

    w0 = w_in0[0]
    o_lr = 2 * hk + 2 * hv
    w0r = jnp.concatenate([w0[:, :o_lr], w0[:, o_lr + lowrank:], w0[:, o_lr:o_lr + lowrank],
                           jnp.zeros((d, LANES - lowrank), F32)], axis=1)
    wgu_pad = jnp.concatenate([w_gate_up[0], jnp.zeros((LANES - lowrank, hk), F32)], axis=0)
    zoh = _s5_zoh(s5_a_re[0], s5_a_im[0], s5_log_dt[0], s5_b_re[0], s5_b_im[0])
    bb_blk, c_blk, ab_re, ab_im, d_blk = _s5_step_blocks(*zoh, s5_c_re[0], s5_c_im[0], s5_d[0])
    t_mat, e_mat, f_mat, al_re, al_im = _s5_chunk_mats(
        s5_a_re[0], s5_a_im[0], s5_log_dt[0], zoh[2], zoh[3], s5_c_re[0], s5_c_im[0], s5_d[0])
    wr_pad = jnp.concatenate([w_router[0], jnp.zeros((d, LANES - n_experts), F32)], axis=1)
    bf = lambda a: a.astype(BF16)
    w_eg, w_eu, w_ed = bf(w_exp_gate[0]), bf(w_exp_up[0]), bf(w_exp_down[0])

    tm = 512
    tpb = t // tm
    pk = dict(per_token=False, tiles_per_batch=tpb)
    xp = x_prompt.reshape(bsz * t, d)
    proj = _inproj(xp, gn(0, 0), mod_p[0], SC_M, SH_M, bf(w0r), tm=tm, bn=w0r.shape[1],
                   prec="bf16", vmem_mb=48, **pk)
    mixed, st_t, cst = _gla_conv_prompt(
        proj, bf(wgu_pad), b_gate_up[0].reshape(1, hk), g_head_norm[0].reshape(1, dv), w_conv[0],
        bsz=bsz, t=t, tt=256, heads=heads, dk=dk, dv=dv, cw=cw)
    st_t = st_t.reshape(bsz, heads // 2, 2, dv, 2, dk)
    gla_p = jnp.stack([st_t[:, :, 0, :, 0, :], st_t[:, :, 1, :, 1, :]], axis=2)
    gla_p = jnp.swapaxes(gla_p.reshape(bsz, heads, dv, dk), 2, 3)
    x1 = _outproj(mixed, bf(w_out0[0]), xp, mod_p[0], GT_M, gn(0, 1), tm=tm, prec="bf16",
                  glu=False, **pk)
    x2 = _ffn(x1, gn(0, 2), mod_p[0], SC_F, SH_F, GT_F, gn(0, 3), bf(w_ffn_gate[0]),
              bf(w_ffn_up[0]), bf(w_ffn_down[0]), tm=tm, fc=1408, prec="bf16", vmem_mb=48, **pk)
    u = _inproj(x2, gn(1, 0), mod_p[1], SC_M, SH_M, bf(w_in1[0]), tm=tm, bn=d, prec="bf16",
                out_dtype=BF16, **pk)
    n_ck = t // S5_L
    chunked = (bsz, n_ck, S5_L, g_s5, S5_GROUP)
    u2 = jnp.transpose(u.reshape(chunked), (1, 0, 3, 2, 4)).reshape(n_ck * bsz, d * S5_L)
    yg2, re_p, im_p = _s5_prompt(u2, bf(t_mat), bf(e_mat), bf(f_mat), al_re, al_im, nb=bsz)
    yg = jnp.transpose(yg2.reshape(n_ck, bsz, g_s5, S5_L, S5_GROUP), (1, 0, 3, 2, 4))
    x3 = _outproj(yg.reshape(bsz * t, d), bf(w_glu[0]), x2, mod_p[1], GT_M, gn(1, 1), tm=tm,
                  prec="bf16", glu=True, vmem_mb=48, **pk)
    comb, hb = _router(x3, gn(1, 2), mod_p[1], SC_F, SH_F, wr_pad, tm=tm, n_experts=n_experts,
                       **pk)

    sk = dict(per_token=True, tiles_per_batch=1)
    xs = x_sample.reshape(bs, d)
    proj_s = _inproj(xs, gn(0, 0), mod_s[0], SC_M, SH_M, w0r, tm=bs, bn=640, prec="f32", **sk)
    s_t = jnp.transpose(state_gla[:, 0], (1, 2, 3, 0))
    cbuf_t = jnp.transpose(state_conv[:, 0], (1, 2, 0))
    o_t, y_t, sn_t, cn_t = _gla_conv_step(
        proj_s.T, wgu_pad.T, b_gate_up[0].reshape(hk, 1), g_head_norm[0].reshape(dv, 1),
        w_conv[0].reshape(3, cw, 1), s_t, cbuf_t, heads=heads, dk=dk, dv=dv, cw=cw)
    mixed_s = jnp.concatenate([o_t, y_t], axis=0).T
    gla_s = jnp.transpose(sn_t, (3, 0, 1, 2))
    conv_s = jnp.transpose(cn_t, (2, 0, 1))
    x1s = _outproj(mixed_s, w_out0[0], xs, mod_s[0], GT_M, gn(0, 1), tm=bs, prec="f32", glu=False,
                   **sk)
    x2s = _ffn(x1s, gn(0, 2), mod_s[0], SC_F, SH_F, GT_F, gn(0, 3), w_ffn_gate[0], w_ffn_up[0],
               w_ffn_down[0], tm=bs, fc=256, prec="f32", **sk)
    us = _inproj(x2s, gn(1, 0), mod_s[1], SC_M, SH_M, w_in1[0], tm=bs, bn=256, prec="f32", **sk)
    ygs, re_s, im_s = _s5_step(us, bb_blk, c_blk, ab_re, ab_im, d_blk,
                               state_s5_re[:, 0].reshape(bs, g_s5 * p_s5),
                               state_s5_im[:, 0].reshape(bs, g_s5 * p_s5))
    x3s = _outproj(ygs, w_glu[0], x2s, mod_s[1], GT_M, gn(1, 1), tm=bs, prec="f32", glu=True,
                   vmem_mb=48, **sk)
    comb_s, hb_s = _router(x3s, gn(1, 2), mod_s[1], SC_F, SH_F, wr_pad, tm=bs, n_experts=n_experts,
                           **sk)

    n_p = bsz * t
    pad = -(bs) % MOE_TB
    n_all = n_p + bs + pad
    n_blk = n_all // MOE_TB
    rows_blk = _moe_block_rows(n_experts)
    n_tiles_max = -(-(n_blk * rows_blk) // MOE_TM) + n_experts
    zrow = lambda w, dt: jnp.zeros((pad, w), dt)
    hb_all = jnp.concatenate([hb, hb_s, zrow(d, BF16)], axis=0)
    comb_all = jnp.concatenate([comb, comb_s, zrow(LANES, F32)], axis=0)
    sorted_pos, block_off, seg_cnt, tile_expert, n_tiles = _moe_schedule(comb_all, n_experts,
                                                                         n_tiles_max)
    seg = (sorted_pos, block_off, seg_cnt)
    dest, xs_sorted = _moe_sort(*seg, hb_all, comb_all, n_tiles_max * MOE_TM, n_experts=n_experts)
    ys_sorted = _experts(tile_expert, n_tiles, xs_sorted, w_eg, w_eu, w_ed, fc=896)
    x4 = _moe_combine(*seg, ys_sorted, comb_all, dest, x3, mod_p[1], GT_F, gn(1, 3), blk0=0,
                      n_experts=n_experts, per_token=False, tiles_per_batch=t // MOE_TB)
    pad_rows = lambda a: jnp.concatenate([a, jnp.zeros((pad, a.shape[1]), a.dtype)], axis=0)
    x4s = _moe_combine(*seg, ys_sorted, comb_all, dest, pad_rows(x3s), pad_rows(mod_s[1]), GT_F,
                       gn(1, 3), blk0=n_p // MOE_TB, n_experts=n_experts, per_token=True,
                       tiles_per_batch=1)[:bs]

    return (x4.reshape(bsz, t, d), x4s.reshape(bs, 1, d),
            gla_p[:, None], cst[:, None],
            re_p.reshape(bsz, 1, g_s5, p_s5), im_p.reshape(bsz, 1, g_s5, p_s5),
            gla_s[:, None], conv_s[:, None],
            re_s.reshape(bs, 1, g_s5, p_s5), im_s.reshape(bs, 1, g_s5, p_s5))
```

```python
import functools
import math

import jax
import jax.numpy as jnp
from jax import lax
from jax.experimental import pallas as pl
from jax.experimental.pallas import tpu as pltpu

F32 = jnp.float32
BF16 = jnp.bfloat16
EPS = 1e-6
LANES = 128
GLA_CHUNK = 64
GATE_NORMALIZER = 16.0
N_MOD = 6
S5_GROUP = 16
S5_CHUNK_GROUPS = 8


def _dot(a, b, dims=None):
    if dims is None:
        return jnp.dot(a, b, preferred_element_type=F32)
    return lax.dot_general(a, b, (dims, ((), ())), preferred_element_type=F32)


def _split(a, parts=2):
    rem = a.astype(F32)
    out = []
    for _ in range(parts - 1):
        piece = rem.astype(BF16)
        out.append(piece)
        rem = rem - piece.astype(F32)
    out.append(rem.astype(BF16))
    return out


def _mm(a, w, prec):
    if prec == "bf16":
        return _dot(a.astype(BF16), w.astype(BF16))
    a1, a2, a3 = _split(a, 3)
    w1, w2, w3 = _split(w, 3)
    small = (_dot(a1, w3) + _dot(a3, w1)) + _dot(a2, w2)
    return _dot(a1, w1) + ((_dot(a1, w2) + _dot(a2, w1)) + small)


def _silu(x):
    return x * jax.nn.sigmoid(x)


def _gelu_tanh(x):
    return 0.5 * x * (1.0 + jnp.tanh(math.sqrt(2.0 / math.pi) * (x + 0.044715 * (x * x * x))))


def _log_sigmoid(x):
    return -(jnp.maximum(-x, 0.0) + jnp.log1p(jnp.exp(-jnp.abs(x))))


def _rms(x):
    return x * lax.rsqrt(jnp.mean(x * x, axis=-1, keepdims=True) + EPS)


def _mod_row(ref):
    return ref[0] if len(ref.shape) == 3 else ref[...]


def _normmod(x, g, sc, sh):
    return (_rms(x) * g) * (1.0 + sc) + sh


def _mod_spec(per_token, tm, d, col, tiles_per_batch):
    if per_token:
        return pl.BlockSpec((tm, d), lambda i, *_: (i, col))
    return pl.BlockSpec((1, 1, d), lambda i, *_: (i // tiles_per_batch, 0, col))


def _params(sem, vmem_mb=None):
    kw = dict(dimension_semantics=sem)
    if vmem_mb is not None:
        kw["vmem_limit_bytes"] = vmem_mb << 20
    return pltpu.CompilerParams(**kw)


def _ada_kernel(c_ref, w_ref, b_ref, o_ref):
    o_ref[0] = _mm(_silu(c_ref[...]), w_ref[0], "f32") + b_ref[0]


def _ada(c_all, w_ada, b_ada):
    depth, d, n6 = w_ada.shape
    rows = c_all.shape[0]
    return pl.pallas_call(
        _ada_kernel,
        grid=(depth, n6 // d),
        in_specs=[pl.BlockSpec((rows, d), lambda l, j: (0, 0)),
                  pl.BlockSpec((1, d, d), lambda l, j: (l, 0, j)),
                  pl.BlockSpec((1, 1, d), lambda l, j: (l, 0, j))],
        out_specs=pl.BlockSpec((1, rows, d), lambda l, j: (l, 0, j)),
        out_shape=jax.ShapeDtypeStruct((depth, rows, n6), F32),
        compiler_params=_params(("parallel", "parallel")),
    )(c_all, w_ada, b_ada.reshape(depth, 1, n6))


def _inproj_kernel(x_ref, g_ref, sc_ref, sh_ref, w_ref, o_ref, *, prec):
    h = _normmod(x_ref[...], g_ref[...], _mod_row(sc_ref), _mod_row(sh_ref))
    o_ref[...] = _mm(h, w_ref[...], prec).astype(o_ref.dtype)


def _inproj(x2d, g, mod, sc_col, sh_col, w, *, tm, bn, prec, per_token, tiles_per_batch,
            out_dtype=F32, vmem_mb=None):
    n_tok, d = x2d.shape
    n_out = w.shape[1]
    return pl.pallas_call(
        functools.partial(_inproj_kernel, prec=prec),
        grid=(n_tok // tm, n_out // bn),
        in_specs=[pl.BlockSpec((tm, d), lambda i, j: (i, 0)),
                  pl.BlockSpec((1, d), lambda i, j: (0, 0)),
                  _mod_spec(per_token, tm, d, sc_col, tiles_per_batch),
                  _mod_spec(per_token, tm, d, sh_col, tiles_per_batch),
                  pl.BlockSpec((d, bn), lambda i, j: (0, j))],
        out_specs=pl.BlockSpec((tm, bn), lambda i, j: (i, j)),
        out_shape=jax.ShapeDtypeStruct((n_tok, n_out), out_dtype),
        compiler_params=_params(("parallel", "arbitrary"), vmem_mb),
    )(x2d, g, mod, mod, w)


def _outproj_kernel(a_ref, w_ref, x_ref, gt_ref, g_ref, o_ref, *, prec, glu):
    z = _mm(a_ref[...], w_ref[...], prec)
    if glu:
        d = z.shape[1] // 2
        z = z[:, :d] * jax.nn.sigmoid(z[:, d:])
    o_ref[...] = x_ref[...] + _mod_row(gt_ref) * (_rms(z) * g_ref[...])


def _outproj(a2d, w, x2d, mod, gt_col, g, *, tm, prec, glu, per_token, tiles_per_batch,
             vmem_mb=None):
    n_tok, d = x2d.shape
    k, n_out = w.shape
    return pl.pallas_call(
        functools.partial(_outproj_kernel, prec=prec, glu=glu),
        grid=(n_tok // tm,),
        in_specs=[pl.BlockSpec((tm, k), lambda i: (i, 0)),
                  pl.BlockSpec((k, n_out), lambda i: (0, 0)),
                  pl.BlockSpec((tm, d), lambda i: (i, 0)),
                  _mod_spec(per_token, tm, d, gt_col, tiles_per_batch),
                  pl.BlockSpec((1, d), lambda i: (0, 0))],
        out_specs=pl.BlockSpec((tm, d), lambda i: (i, 0)),
        out_shape=jax.ShapeDtypeStruct((n_tok, d), F32),
        compiler_params=_params(("parallel",), vmem_mb),
    )(a2d, w, x2d, mod, g)


def _ffn_kernel(x_ref, g1_ref, sc_ref, sh_ref, wg_ref, wu_ref, wd_ref, gt_ref, g2_ref,
                o_ref, h_scr, acc_scr, *, prec, n_c):
    c = pl.program_id(1)

    @pl.when(c == 0)
    def _():
        h = _normmod(x_ref[...], g1_ref[...], _mod_row(sc_ref), _mod_row(sh_ref))
        h_scr[...] = h.astype(h_scr.dtype)
        acc_scr[...] = jnp.zeros_like(acc_scr)

    h = h_scr[...]
    act = _silu(_mm(h, wg_ref[...], prec)) * _mm(h, wu_ref[...], prec)
    acc_scr[...] += _mm(act, wd_ref[...], prec)

    @pl.when(c == n_c - 1)
    def _():
        o_ref[...] = x_ref[...] + _mod_row(gt_ref) * (_rms(acc_scr[...]) * g2_ref[...])


def _ffn(x2d, g1, mod, sc_col, sh_col, gt_col, g2, wg, wu, wd, *, tm, fc, prec,
         per_token, tiles_per_batch, vmem_mb=None):
    n_tok, d = x2d.shape
    n_c = wg.shape[1] // fc
    tok = pl.BlockSpec((tm, d), lambda i, c: (i, 0))
    vec = pl.BlockSpec((1, d), lambda i, c: (0, 0))
    h_dtype = BF16 if prec == "bf16" else F32
    w_mode = dict(pipeline_mode=pl.Buffered(1)) if n_c == 1 else {}
    return pl.pallas_call(
        functools.partial(_ffn_kernel, prec=prec, n_c=n_c),
        grid=(n_tok // tm, n_c),
        in_specs=[tok, vec,
                  _mod_spec(per_token, tm, d, sc_col, tiles_per_batch),
                  _mod_spec(per_token, tm, d, sh_col, tiles_per_batch),
                  pl.BlockSpec((d, fc), lambda i, c: (0, c), **w_mode),
                  pl.BlockSpec((d, fc), lambda i, c: (0, c), **w_mode),
                  pl.BlockSpec((fc, d), lambda i, c: (c, 0), **w_mode),
                  _mod_spec(per_token, tm, d, gt_col, tiles_per_batch), vec],
        out_specs=tok,
        out_shape=jax.ShapeDtypeStruct((n_tok, d), F32),
        scratch_shapes=[pltpu.VMEM((tm, d), h_dtype), pltpu.VMEM((tm, d), F32)],
        compiler_params=_params(("parallel", "arbitrary"), vmem_mb),
    )(x2d, g1, mod, mod, wg, wu, wd, mod, g2)


def _router_kernel(x_ref, g_ref, sc_ref, sh_ref, wr_ref, comb_ref, hb_ref, *, n_experts):
    h = _normmod(x_ref[...], g_ref[...], _mod_row(sc_ref), _mod_row(sh_ref))
    hb_ref[...] = h.astype(hb_ref.dtype)
    logits = _mm(h, wr_ref[...], "f32")
    lane = lax.broadcasted_iota(jnp.int32, logits.shape, 1).astype(F32)
    neg = -jnp.inf
    l1 = jnp.where(lane < n_experts, logits, neg)
    m1 = jnp.max(l1, axis=-1, keepdims=True)
    i1 = jnp.min(jnp.where(l1 == m1, lane, float(LANES)), axis=-1, keepdims=True)
    l2 = jnp.where(lane == i1, neg, l1)
    m2 = jnp.max(l2, axis=-1, keepdims=True)
    i2 = jnp.min(jnp.where(l2 == m2, lane, float(LANES)), axis=-1, keepdims=True)
    e2 = jnp.exp(m2 - m1)
    den = 1.0 + e2
    comb_ref[...] = jnp.where(lane == i1, 1.0 / den, 0.0) + jnp.where(lane == i2, e2 / den, 0.0)


def _router(x2d, g, mod, sc_col, sh_col, wr_pad, *, tm, n_experts, per_token, tiles_per_batch):
    n_tok, d = x2d.shape
    return pl.pallas_call(
        functools.partial(_router_kernel, n_experts=n_experts),
        grid=(n_tok // tm,),
        in_specs=[pl.BlockSpec((tm, d), lambda i: (i, 0)),
                  pl.BlockSpec((1, d), lambda i: (0, 0)),
                  _mod_spec(per_token, tm, d, sc_col, tiles_per_batch),
                  _mod_spec(per_token, tm, d, sh_col, tiles_per_batch),
                  pl.BlockSpec((d, LANES), lambda i: (0, 0))],
        out_specs=[pl.BlockSpec((tm, LANES), lambda i: (i, 0)),
                   pl.BlockSpec((tm, d), lambda i: (i, 0))],
        out_shape=[jax.ShapeDtypeStruct((n_tok, LANES), F32),
                   jax.ShapeDtypeStruct((n_tok, d), BF16)],
        compiler_params=_params(("parallel",)),
    )(x2d, g, mod, mod, wr_pad)


MOE_TB = 256
MOE_SEG = 16
MOE_TM = 512


def _moe_block_rows(n_experts):
    return -(-(2 * MOE_TB + n_experts * (MOE_SEG - 1)) // LANES) * LANES


def _scatter_matrix(dest, comb, weighted, n_rows, n_experts):
    tb = dest.shape[0]
    lane = lax.broadcasted_iota(jnp.int32, (tb, n_rows), 1).astype(F32)
    m = jnp.zeros((tb, n_rows), F32)
    for e in range(n_experts):
        w = comb[:, e:e + 1]
        val = jnp.where(w > 0.0, w if weighted else 1.0, 0.0)
        m = m + jnp.where(lane == dest[:, e:e + 1], val, 0.0)
    return m


_SEG_BITS = (256, 128, 64, 32, 16)


def _segment_copies(make_copy, blk, pos_ref, off_ref, cnt_ref, n_experts, start):
    for e in range(n_experts):
        seg = blk * n_experts + e
        in_blk, in_sorted, n = off_ref[seg], pos_ref[seg], cnt_ref[seg]
        for size in _SEG_BITS:
            has = (n & size) != 0

            @pl.when(has)
            def _(in_blk=in_blk, in_sorted=in_sorted, size=size):
                cp = make_copy(pl.multiple_of(in_blk, MOE_SEG), pl.multiple_of(in_sorted, MOE_SEG), size)
                cp.start() if start else cp.wait()

            step = jnp.where(has, size, 0)
            in_blk, in_sorted = in_blk + step, in_sorted + step


def _moe_sort_kernel(pos_ref, off_ref, cnt_ref, hb_ref, comb_ref, init_hbm, dest_ref, xs_hbm,
                     buf, sems, *, n_rows, n_experts, n_blk):
    del init_hbm
    b = pl.program_id(0)
    slot = b % 2

    def writes(blk, slot, start):
        def make_copy(block_row, sorted_row, size):
            return pltpu.make_async_copy(buf.at[slot, pl.ds(block_row, size)],
                                         xs_hbm.at[pl.ds(sorted_row, size)], sems.at[slot])
        _segment_copies(make_copy, blk, pos_ref, off_ref, cnt_ref, n_experts, start)

    @pl.when(b >= 2)
    def _():
        writes(b - 2, slot, False)

    comb = comb_ref[...]
    tb = comb.shape[0]
    sel = jnp.where(comb > 0.0, 1.0, 0.0)
    r = lax.broadcasted_iota(jnp.int32, (tb, tb), 0)
    c = lax.broadcasted_iota(jnp.int32, (tb, tb), 1)
    rank = _dot(jnp.where(c < r, 1.0, 0.0).astype(BF16), sel.astype(BF16))
    cnt = jnp.sum(sel, axis=0, keepdims=True)
    cnt = jnp.floor((cnt + (MOE_SEG - 1)) / MOE_SEG) * MOE_SEG
    ru = lax.broadcasted_iota(jnp.int32, (LANES, LANES), 0)
    cu = lax.broadcasted_iota(jnp.int32, (LANES, LANES), 1)
    off = _dot(jnp.broadcast_to(cnt, (8, LANES)).astype(BF16),
               jnp.where(ru < cu, 1.0, 0.0).astype(BF16))[0:1]
    dest = off + rank
    dest_ref[...] = dest
    pt = _scatter_matrix(dest, comb, False, n_rows, n_experts).astype(BF16)
    buf[slot] = _dot(pt, hb_ref[...], ((0,), (0,))).astype(buf.dtype)
    writes(b, slot, True)

    @pl.when(b == n_blk - 1)
    def _():
        writes(b, slot, False)
        if n_blk > 1:
            writes(b - 1, 1 - slot, False)


def _moe_sort(sorted_pos, block_off, seg_cnt, hb, comb, n_sorted_rows, *, n_experts):
    n_tok, d = hb.shape
    n_blk = n_tok // MOE_TB
    n_rows = _moe_block_rows(n_experts)
    init = jnp.zeros((n_sorted_rows, d), BF16)
    any_spec = pl.BlockSpec(memory_space=pl.ANY)
    return pl.pallas_call(
        functools.partial(_moe_sort_kernel, n_rows=n_rows, n_experts=n_experts, n_blk=n_blk),
        grid_spec=pltpu.PrefetchScalarGridSpec(
            num_scalar_prefetch=3, grid=(n_blk,),
            in_specs=[pl.BlockSpec((MOE_TB, d), lambda b, *_: (b, 0)),
                      pl.BlockSpec((MOE_TB, LANES), lambda b, *_: (b, 0)),
                      any_spec],
            out_specs=[pl.BlockSpec((MOE_TB, LANES), lambda b, *_: (b, 0)), any_spec],
            scratch_shapes=[pltpu.VMEM((2, n_rows, d), BF16), pltpu.SemaphoreType.DMA((2,))]),
        out_shape=[jax.ShapeDtypeStruct((n_tok, LANES), F32),
                   jax.ShapeDtypeStruct((n_sorted_rows, d), BF16)],
        input_output_aliases={5: 1},
        compiler_params=_params(("arbitrary",)),
    )(sorted_pos, block_off, seg_cnt, hb, comb, init)


def _experts_kernel(te_ref, nt_ref, x_ref, wg_ref, wu_ref, wd_ref, o_ref, acc_scr, *, n_c):
    del te_ref
    k = pl.program_id(0)
    c = pl.program_id(1)
    active = k < nt_ref[0]

    @pl.when(active)
    def _():
        h = x_ref[...]
        act = _silu(_dot(h, wg_ref[0])) * _dot(h, wu_ref[0])
        y = _dot(act.astype(BF16), wd_ref[0])

        @pl.when(c == 0)
        def _():
            acc_scr[...] = y

        @pl.when(c > 0)
        def _():
            acc_scr[...] += y

    @pl.when(c == n_c - 1)
    def _():
        o_ref[...] = jnp.where(active, acc_scr[...], 0.0).astype(o_ref.dtype)


def _experts(tile_expert, n_tiles, xs, wg, wu, wd, *, fc):
    n_rows, d = xs.shape
    n_c = wg.shape[2] // fc

    def w_map(k, c, te, nt):
        return (te[k], 0, jnp.where(k < nt[0], c, n_c - 1))

    def wd_map(k, c, te, nt):
        return (te[k], jnp.where(k < nt[0], c, n_c - 1), 0)

    return pl.pallas_call(
        functools.partial(_experts_kernel, n_c=n_c),
        grid_spec=pltpu.PrefetchScalarGridSpec(
            num_scalar_prefetch=2, grid=(n_rows // MOE_TM, n_c),
            in_specs=[pl.BlockSpec((MOE_TM, d), lambda k, c, te, nt: (k, 0)),
                      pl.BlockSpec((1, d, fc), w_map), pl.BlockSpec((1, d, fc), w_map),
                      pl.BlockSpec((1, fc, d), wd_map)],
            out_specs=pl.BlockSpec((MOE_TM, d), lambda k, c, te, nt: (k, 0)),
            scratch_shapes=[pltpu.VMEM((MOE_TM, d), F32)]),
        out_shape=jax.ShapeDtypeStruct((n_rows, d), BF16),
        compiler_params=_params(("arbitrary", "arbitrary"), 48),
    )(tile_expert, n_tiles, xs, wg, wu, wd)


def _moe_combine_kernel(pos_ref, off_ref, cnt_ref, ys_hbm, comb_ref, dest_ref, x_ref, gt_ref, g_ref,
                        o_ref, buf, sems, *, n_experts, n_blk, blk0):
    b = pl.program_id(0)
    slot = b % 2

    def reads(blk, slot, start):
        def make_copy(block_row, sorted_row, size):
            return pltpu.make_async_copy(ys_hbm.at[pl.ds(sorted_row, size)],
                                         buf.at[slot, pl.ds(block_row, size)], sems.at[slot])
        _segment_copies(make_copy, blk0 + blk, pos_ref, off_ref, cnt_ref, n_experts, start)

    @pl.when(b == 0)
    def _():
        buf[...] = jnp.zeros_like(buf)
        reads(b, slot, True)

    if n_blk > 1:
        @pl.when(b + 1 < n_blk)
        def _():
            reads(b + 1, 1 - slot, True)

    reads(b, slot, False)
    yb = buf[slot]
    wt = _scatter_matrix(dest_ref[...], comb_ref[...], True, yb.shape[0], n_experts)
    w_hi, w_lo = _split(wt)
    y = _dot(w_hi, yb) + _dot(w_lo, yb)
    o_ref[...] = x_ref[...] + _mod_row(gt_ref) * (_rms(y) * g_ref[...])


def _moe_combine(sorted_pos, block_off, seg_cnt, ys, comb, dest, x2d, mod, gt_col, g, *, blk0,
                 n_experts, per_token, tiles_per_batch):
    n_tok, d = x2d.shape
    n_blk = n_tok // MOE_TB
    n_rows = _moe_block_rows(n_experts)
    return pl.pallas_call(
        functools.partial(_moe_combine_kernel, n_experts=n_experts, n_blk=n_blk, blk0=blk0),
        grid_spec=pltpu.PrefetchScalarGridSpec(
            num_scalar_prefetch=3, grid=(n_blk,),
            in_specs=[pl.BlockSpec(memory_space=pl.ANY),
                      pl.BlockSpec((MOE_TB, LANES), lambda b, *_: (blk0 + b, 0)),
                      pl.BlockSpec((MOE_TB, LANES), lambda b, *_: (blk0 + b, 0)),
                      pl.BlockSpec((MOE_TB, d), lambda b, *_: (b, 0)),
                      _mod_spec(per_token, MOE_TB, d, gt_col, tiles_per_batch),
                      pl.BlockSpec((1, d), lambda b, *_: (0, 0))],
            out_specs=pl.BlockSpec((MOE_TB, d), lambda b, *_: (b, 0)),
            scratch_shapes=[pltpu.VMEM((2, n_rows, d), BF16), pltpu.SemaphoreType.DMA((2,))]),
        out_shape=jax.ShapeDtypeStruct((n_tok, d), F32),
        compiler_params=_params(("arbitrary",)),
    )(sorted_pos, block_off, seg_cnt, ys, comb, dest, x2d, mod, g)


def _moe_schedule(comb, n_experts, n_tiles_max):
    n_blk = comb.shape[0] // MOE_TB
    cnt = jnp.sum((comb[:, :n_experts] > 0.0).reshape(n_blk, MOE_TB, n_experts).astype(jnp.int32),
                  axis=1)
    cnt = -(-cnt // MOE_SEG) * MOE_SEG
    off = jnp.cumsum(cnt, axis=1) - cnt
    tot = jnp.sum(cnt, axis=0)
    grp = -(-tot // MOE_TM) * MOE_TM
    g_end = jnp.cumsum(grp)
    sorted_pos = (g_end - grp)[None, :] + (jnp.cumsum(cnt, axis=0) - cnt)
    n_tiles = g_end[-1] // MOE_TM
    first_row = jnp.arange(n_tiles_max, dtype=jnp.int32) * MOE_TM
    tile_expert = jnp.sum(first_row[:, None] >= g_end[None, :], axis=1)
    last = jnp.sum((n_tiles - 1) * MOE_TM >= g_end)
    tile_expert = jnp.minimum(tile_expert, last).astype(jnp.int32)
    flat = lambda a: a.reshape(-1).astype(jnp.int32)
    return (flat(sorted_pos), flat(off), flat(cnt), tile_expert,
            n_tiles.reshape(1).astype(jnp.int32))


def _gla_conv_kernel(p_ref, wgu_ref, bgu_ref, gh_ref, wc_ref, mixed_ref, st_ref, cst_ref,
                     s_scr, uext_scr, *, tt, n_t, heads, dk, dv, cw):
    ti = pl.program_id(1)
    hk = heads * dk
    hv = heads * dv
    o_q, o_k, o_v, o_og = 0, hk, 2 * hk, 2 * hk + hv
    o_ch = o_og + hv
    o_cb, o_cc, o_lr = o_ch + cw, o_ch + 2 * cw, o_ch + 3 * cw
    L = GLA_CHUNK

    @pl.when(ti == 0)
    def _():
        s_scr[...] = jnp.zeros_like(s_scr)
        uext_scr[0:8, :] = jnp.zeros((8, cw), F32)

    logit = _dot(p_ref[:, o_lr:o_lr + LANES].astype(BF16), wgu_ref[...]) + bgu_ref[...]
    logg = _log_sigmoid(logit) / GATE_NORMALIZER

    r = lax.broadcasted_iota(jnp.int32, (tt, tt), 0)
    c = lax.broadcasted_iota(jnp.int32, (tt, tt), 1)
    tri = jnp.where((r // L == c // L) & (c <= r), 1.0, 0.0).astype(BF16)
    g_hi = logg.astype(BF16)
    rem = logg - g_hi.astype(F32)
    g_mid = rem.astype(BF16)
    g_lo = (rem - g_mid.astype(F32)).astype(BF16)
    bc = _dot(tri, g_hi) + (_dot(tri, g_mid) + _dot(tri, g_lo))

    rl = lax.broadcasted_iota(jnp.int32, (L, L), 0)
    cl = lax.broadcasted_iota(jnp.int32, (L, L), 1)
    tril = cl <= rl
    lane = lax.broadcasted_iota(jnp.int32, (L, LANES), 1)
    heads_per_blk = LANES // dk
    nt_dims = ((1,), (1,))

    for ck in range(tt // L):
        rows = slice(L * ck, L * ck + L)
        b = bc[rows]
        bl = b[L - 1:L]
        q_in = (p_ref[rows, o_q:o_q + hk] * (dk ** -0.5)) * jnp.exp(b)
        k = p_ref[rows, o_k:o_k + hk]
        k_out = (k * jnp.exp(-b)).astype(BF16)
        k_dec = (k * jnp.exp(bl - b)).astype(BF16)
        dec = jnp.exp(bl)
        q_in = q_in.astype(BF16)
        for h in range(heads):
            blk = slice(LANES * (h // heads_per_blk), LANES * (h // heads_per_blk) + LANES)
            in_head = (lane // dk) == (h % heads_per_blk)
            qm = jnp.where(in_head, q_in[:, blk], jnp.zeros_like(q_in[:, blk]))
            att = _dot(qm, k_out[:, blk], nt_dims)
            att = jnp.where(tril, att, 0.0).astype(BF16)
            vh = p_ref[rows, o_v + dv * h:o_v + dv * h + dv].astype(BF16)
            s_t = s_scr[h]
            o = _dot(att, vh) + _dot(qm, s_t.astype(BF16), nt_dims)
            ds_t = _dot(vh, k_dec[:, blk], ((0,), (0,)))
            s_scr[h] = s_t * dec[:, blk] + ds_t
            og = p_ref[rows, o_og + dv * h:o_og + dv * h + dv]
            res = (_rms(o) * gh_ref[...]) * _silu(og)
            mixed_ref[rows, dv * h:dv * h + dv] = res.astype(mixed_ref.dtype)

    u = p_ref[:, o_cc:o_cc + cw] * p_ref[:, o_ch:o_ch + cw]
    uext_scr[8:8 + tt, :] = u
    y = (wc_ref[0:1, :] * uext_scr[6:6 + tt, :] + wc_ref[1:2, :] * uext_scr[7:7 + tt, :]
         + wc_ref[2:3, :] * u)
    mixed_ref[:, hv:hv + cw] = (p_ref[:, o_cb:o_cb + cw] * y).astype(mixed_ref.dtype)
    tail = uext_scr[tt + 6:tt + 8, :]
    uext_scr[6:8, :] = tail

    @pl.when(ti == n_t - 1)
    def _():
        st_ref[0] = s_scr[...]
        cst_ref[0] = tail


def _gla_conv_prompt(proj, wgu_pad, bgu, g_head, w_conv, *, bsz, t, tt, heads, dk, dv, cw):
    n_t = t // tt
    n_in = proj.shape[1]
    width = heads * dv + cw
    return pl.pallas_call(
        functools.partial(_gla_conv_kernel, tt=tt, n_t=n_t, heads=heads, dk=dk, dv=dv, cw=cw),
        grid=(bsz, n_t),
        in_specs=[pl.BlockSpec((tt, n_in), lambda b, i: (b * n_t + i, 0)),
                  pl.BlockSpec(wgu_pad.shape, lambda b, i: (0, 0)),
                  pl.BlockSpec(bgu.shape, lambda b, i: (0, 0)),
                  pl.BlockSpec(g_head.shape, lambda b, i: (0, 0)),
                  pl.BlockSpec(w_conv.shape, lambda b, i: (0, 0))],
        out_specs=[pl.BlockSpec((tt, width), lambda b, i: (b * n_t + i, 0)),
                   pl.BlockSpec((1, heads, dv, LANES), lambda b, i: (b, 0, 0, 0)),
                   pl.BlockSpec((1, 2, cw), lambda b, i: (b, 0, 0))],
        out_shape=[jax.ShapeDtypeStruct((bsz * t, width), BF16),
                   jax.ShapeDtypeStruct((bsz, heads, dv, LANES), F32),
                   jax.ShapeDtypeStruct((bsz, 2, cw), F32)],
        scratch_shapes=[pltpu.VMEM((heads, dv, LANES), F32), pltpu.VMEM((tt + 8, cw), F32)],
        compiler_params=_params(("parallel", "arbitrary"), 48),
    )(proj, wgu_pad, bgu, g_head, w_conv)


def _gla_conv_step_kernel(q_ref, k_ref, v_ref, og_ref, ch_ref, cb_ref, cc_ref, lr_ref,
                          wgu_ref, bgu_ref, gh_ref, wc_ref, s_ref, cbuf_ref,
                          o_ref, y_ref, sn_ref, cn_ref, a_scr, *, dk):
    logit = _mm(wgu_ref[...], lr_ref[...], "f32") + bgu_ref[...]
    a_scr[...] = jnp.exp(_log_sigmoid(logit) / GATE_NORMALIZER)
    v_t = v_ref[...]

    def body(d, acc):
        a = a_scr[pl.ds(d, 1), :]
        kd = k_ref[pl.ds(d, 1), :]
        qd = q_ref[pl.ds(d, 1), :]
        s_new = a * s_ref[0, d] + kd * v_t
        sn_ref[0, d] = s_new
        return acc + (qd * (dk ** -0.5)) * s_new

    o = lax.fori_loop(0, dk, body, jnp.zeros(v_t.shape, F32))
    o = o * lax.rsqrt(jnp.mean(o * o, axis=0, keepdims=True) + EPS) * gh_ref[...]
    o_ref[...] = o * _silu(og_ref[...])
    u = cc_ref[...] * ch_ref[...]
    y = wc_ref[0] * cbuf_ref[0] + wc_ref[1] * cbuf_ref[1] + wc_ref[2] * u
    y_ref[...] = cb_ref[...] * y
    cn_ref[0] = cbuf_ref[1]
    cn_ref[1] = u


def _gla_conv_step(proj_t, wgu_t_pad, bgu_col, gh_col, wc_col, s_t, cbuf_t, *, heads, dk, dv, cw):
    bsz = proj_t.shape[1]
    hk, hv = heads * dk, heads * dv
    cs = cw // heads
    assert dv == LANES and cs == LANES and dk * 2 == LANES
    o_k, o_v, o_og = hk, 2 * hk, 2 * hk + hv
    o_ch = o_og + hv
    o_cb, o_cc, o_lr = o_ch + cw, o_ch + 2 * cw, o_ch + 3 * cw
    row = lambda off, size: (lambda h: (off // size + h, 0))
    blk = lambda size, off: pl.BlockSpec((size, bsz), row(off, size))
    return pl.pallas_call(
        functools.partial(_gla_conv_step_kernel, dk=dk),
        grid=(heads,),
        in_specs=[blk(dk, 0), blk(dk, o_k), blk(dv, o_v), blk(dv, o_og),
                  blk(cs, o_ch), blk(cs, o_cb), blk(cs, o_cc),
                  pl.BlockSpec((LANES, bsz), lambda h: (o_lr // LANES, 0)),
                  pl.BlockSpec((dk, LANES), lambda h: (h, 0)),
                  pl.BlockSpec((dk, 1), lambda h: (h, 0)),
                  pl.BlockSpec((dv, 1), lambda h: (0, 0)),
                  pl.BlockSpec((3, cs, 1), lambda h: (0, h, 0)),
                  pl.BlockSpec((1, dk, dv, bsz), lambda h: (h, 0, 0, 0)),
                  pl.BlockSpec((2, cs, bsz), lambda h: (0, h, 0))],
        out_specs=[pl.BlockSpec((dv, bsz), lambda h: (h, 0)),
                   pl.BlockSpec((cs, bsz), lambda h: (h, 0)),
                   pl.BlockSpec((1, dk, dv, bsz), lambda h: (h, 0, 0, 0)),
                   pl.BlockSpec((2, cs, bsz), lambda h: (0, h, 0))],
        out_shape=[jax.ShapeDtypeStruct((hv, bsz), F32),
                   jax.ShapeDtypeStruct((cw, bsz), F32),
                   jax.ShapeDtypeStruct(s_t.shape, F32),
                   jax.ShapeDtypeStruct(cbuf_t.shape, F32)],
        scratch_shapes=[pltpu.VMEM((dk, bsz), F32)],
        compiler_params=_params(("parallel",)),
    )(proj_t, proj_t, proj_t, proj_t, proj_t, proj_t, proj_t, proj_t,
      wgu_t_pad, bgu_col, gh_col, wc_col, s_t, cbuf_t)


S5_ROW_PAD = 8


def _s5_scan_kernel(u_ref, bb_ref, cm_ref, abr_ref, abi_ref, d_ref, y_ref, sre_ref, sim_ref,
                    scr, xr_scr, xi_scr, *, tt, n_t, nb, half):
    ti = pl.program_id(1)
    stride = tt + S5_ROW_PAD
    n_blk = half // LANES

    @pl.when(ti == 0)
    def _():
        xr_scr[...] = jnp.zeros_like(xr_scr)
        xi_scr[...] = jnp.zeros_like(xi_scr)

    for b in range(nb):
        bu = _dot(u_ref[b].astype(BF16), bb_ref[0])
        for cb in range(2 * n_blk):
            scr[cb, b * stride:b * stride + tt, :] = bu[:, cb * LANES:(cb + 1) * LANES]
    a_re = [jnp.broadcast_to(abr_ref[0][:, cb * LANES:(cb + 1) * LANES], (nb, LANES))
            for cb in range(n_blk)]
    a_im = [jnp.broadcast_to(abi_ref[0][:, cb * LANES:(cb + 1) * LANES], (nb, LANES))
            for cb in range(n_blk)]

    def body(t, carry):
        rows = pl.ds(t, nb, stride=stride)
        out = []
        for cb in range(n_blk):
            xr, xi = carry[2 * cb], carry[2 * cb + 1]
            nr = (a_re[cb] * xr - a_im[cb] * xi) + scr[cb, rows, :]
            ni = (a_re[cb] * xi + a_im[cb] * xr) + scr[n_blk + cb, rows, :]
            scr[cb, rows, :] = nr
            scr[n_blk + cb, rows, :] = ni
            out += [nr, ni]
        return tuple(out)

    init = []
    for cb in range(n_blk):
        init += [xr_scr[:, cb * LANES:(cb + 1) * LANES], xi_scr[:, cb * LANES:(cb + 1) * LANES]]
    fin = lax.fori_loop(0, tt, body, tuple(init), unroll=2)
    xr = jnp.concatenate([fin[2 * cb] for cb in range(n_blk)], axis=1)
    xi = jnp.concatenate([fin[2 * cb + 1] for cb in range(n_blk)], axis=1)
    xr_scr[...] = xr
    xi_scr[...] = xi
    for b in range(nb):
        xs = jnp.concatenate([scr[cb, b * stride:b * stride + tt, :] for cb in range(2 * n_blk)],
                             axis=1)
        y = _dot(xs.astype(BF16), cm_ref[0]) + d_ref[0] * u_ref[b]
        y_ref[b] = _gelu_tanh(y).astype(y_ref.dtype)

    @pl.when(ti == n_t - 1)
    def _():
        sre_ref[...] = xr
        sim_ref[...] = xi


def _s5_prompt(u3d, bb_blk, c_blk, ab_re, ab_im, d_blk, *, tt):
    nb, t, d = u3d.shape
    n_j = d // LANES
    n_t = t // tt
    half = bb_blk.shape[2] // 2
    return pl.pallas_call(
        functools.partial(_s5_scan_kernel, tt=tt, n_t=n_t, nb=nb, half=half),
        grid=(n_j, n_t),
        in_specs=[pl.BlockSpec((nb, tt, LANES), lambda j, i: (0, i, j)),
                  pl.BlockSpec((1, LANES, 2 * half), lambda j, i: (j, 0, 0)),
                  pl.BlockSpec((1, 2 * half, LANES), lambda j, i: (j, 0, 0)),
                  pl.BlockSpec((1, 1, half), lambda j, i: (j, 0, 0)),
                  pl.BlockSpec((1, 1, half), lambda j, i: (j, 0, 0)),
                  pl.BlockSpec((1, 1, LANES), lambda j, i: (j, 0, 0))],
        out_specs=[pl.BlockSpec((nb, tt, LANES), lambda j, i: (0, i, j)),
                   pl.BlockSpec((nb, half), lambda j, i: (0, j)),
                   pl.BlockSpec((nb, half), lambda j, i: (0, j))],
        out_shape=[jax.ShapeDtypeStruct((nb, t, d), BF16),
                   jax.ShapeDtypeStruct((nb, n_j * half), F32),
                   jax.ShapeDtypeStruct((nb, n_j * half), F32)],
        scratch_shapes=[pltpu.VMEM((2 * half // LANES, nb * (tt + S5_ROW_PAD), LANES), F32),
                        pltpu.VMEM((nb, half), F32), pltpu.VMEM((nb, half), F32)],
        compiler_params=_params(("parallel", "arbitrary")),
    )(u3d, bb_blk, c_blk, ab_re, ab_im, d_blk)


def _s5_step_kernel(u_ref, bb_ref, cm_ref, abr_ref, abi_ref, d_ref, sre_ref, sim_ref,
                    y_ref, nre_ref, nim_ref, *, half):
    u = u_ref[...]
    bu = _mm(u, bb_ref[0], "f32")
    ar, ai = abr_ref[0], abi_ref[0]
    sr, si = sre_ref[...], sim_ref[...]
    nr = (ar * sr - ai * si) + bu[:, 0:half]
    ni = (ar * si + ai * sr) + bu[:, half:2 * half]
    nre_ref[...] = nr
    nim_ref[...] = ni
    y = _mm(jnp.concatenate([nr, ni], axis=1), cm_ref[0], "f32") + d_ref[0] * u
    y_ref[...] = _gelu_tanh(y)


def _s5_step(u2d, bb_blk, c_blk, ab_re, ab_im, d_blk, s_re, s_im):
    bsz, d = u2d.shape
    n_j = d // LANES
    half = bb_blk.shape[2] // 2
    st = pl.BlockSpec((bsz, half), lambda j: (0, j))
    return pl.pallas_call(
        functools.partial(_s5_step_kernel, half=half),
        grid=(n_j,),
        in_specs=[pl.BlockSpec((bsz, LANES), lambda j: (0, j)),
                  pl.BlockSpec((1, LANES, 2 * half), lambda j: (j, 0, 0)),
                  pl.BlockSpec((1, 2 * half, LANES), lambda j: (j, 0, 0)),
                  pl.BlockSpec((1, 1, half), lambda j: (j, 0, 0)),
                  pl.BlockSpec((1, 1, half), lambda j: (j, 0, 0)),
                  pl.BlockSpec((1, 1, LANES), lambda j: (j, 0, 0)),
                  st, st],
        out_specs=[pl.BlockSpec((bsz, LANES), lambda j: (0, j)), st, st],
        out_shape=[jax.ShapeDtypeStruct((bsz, d), F32),
                   jax.ShapeDtypeStruct(s_re.shape, F32),
                   jax.ShapeDtypeStruct(s_im.shape, F32)],
        compiler_params=_params(("parallel",)),
    )(u2d, bb_blk, c_blk, ab_re, ab_im, d_blk, s_re, s_im)


def _s5_zoh(a_re, a_im, log_dt, b_re, b_im):
    dt = jnp.exp(log_dt)[:, None]
    mag = jnp.exp(dt * a_re)
    ab_re, ab_im = mag * jnp.cos(dt * a_im), mag * jnp.sin(dt * a_im)
    den = a_re * a_re + a_im * a_im
    nr, ni = ab_re - 1.0, ab_im
    f_re = (nr * a_re + ni * a_im) / den
    f_im = (ni * a_re - nr * a_im) / den
    bb_re = f_re[..., None] * b_re - f_im[..., None] * b_im
    bb_im = f_re[..., None] * b_im + f_im[..., None] * b_re
    return ab_re, ab_im, bb_re, bb_im


def _s5_step_blocks(ab_re, ab_im, bb_re, bb_im, c_re, c_im, d_skip):
    g, p = ab_re.shape
    cg = S5_CHUNK_GROUPS
    n_j = g // cg
    eye = jnp.eye(cg, dtype=F32)

    def in_blk(m):
        m = m.reshape(n_j, cg, p, S5_GROUP)
        return jnp.einsum("jgpi,gh->jgihp", m, eye).reshape(n_j, cg * S5_GROUP, cg * p)

    def out_blk(m):
        m = m.reshape(n_j, cg, S5_GROUP, p)
        return jnp.einsum("jgip,gh->jgphi", m, eye).reshape(n_j, cg * p, cg * S5_GROUP)

    bb_blk = jnp.concatenate([in_blk(bb_re), in_blk(bb_im)], axis=2)
    c_blk = jnp.concatenate([out_blk(c_re), out_blk(-c_im)], axis=1)
    return (bb_blk, c_blk, ab_re.reshape(n_j, 1, cg * p), ab_im.reshape(n_j, 1, cg * p),
            d_skip.reshape(n_j, 1, cg * S5_GROUP))


def kernel(x_prompt, x_sample, c_prompt, c_sample, state_gla, state_conv, state_s5_re, state_s5_im, w_ada, b_ada, g_norm, w_in0, w_gate_up, b_gate_up, g_head_norm, w_conv, w_out0, w_ffn_gate, w_ffn_up, w_ffn_down, w_in1, s5_a_re, s5_a_im, s5_log_dt, s5_b_re, s5_b_im, s5_c_re, s5_c_im, s5_d, w_glu, w_router, w_exp_gate, w_exp_up, w_exp_down):
    bsz, t, d = x_prompt.shape
    bs = x_sample.shape[0]
    heads, dk, dv = state_gla.shape[2:]
    cw = state_conv.shape[3]
    lowrank = w_gate_up.shape[1]
    hk, hv = heads * dk, heads * dv
    n_experts = w_router.shape[2]
    g_s5, p_s5 = s5_a_re.shape[1:]

    mod = _ada(jnp.concatenate([c_prompt, c_sample], axis=0), w_ada, b_ada)
    mod_p = [mod[l, :bsz].reshape(bsz, 1, N_MOD * d) for l in range(2)]
    mod_s = [mod[l, bsz:] for l in range(2)]
    SH_M, SC_M, GT_M, SH_F, SC_F, GT_F = range(N_MOD)
    gn = lambda l, k: g_norm[l, k].reshape(1, d)

    w0 = w_in0[0]
    o_lr = 2 * hk + 2 * hv
    w0r = jnp.concatenate([w0[:, :o_lr], w0[:, o_lr + lowrank:], w0[:, o_lr:o_lr + lowrank],
                           jnp.zeros((d, LANES - lowrank), F32)], axis=1)
    wgu_pad = jnp.concatenate([w_gate_up[0], jnp.zeros((LANES - lowrank, hk), F32)], axis=0)
    zoh = _s5_zoh(s5_a_re[0], s5_a_im[0], s5_log_dt[0], s5_b_re[0], s5_b_im[0])
    bb_blk, c_blk, ab_re, ab_im, d_blk = _s5_step_blocks(*zoh, s5_c_re[0], s5_c_im[0], s5_d[0])
    wr_pad = jnp.concatenate([w_router[0], jnp.zeros((d, LANES - n_experts), F32)], axis=1)
    bf = lambda a: a.astype(BF16)
    w_eg, w_eu, w_ed = bf(w_exp_gate[0]), bf(w_exp_up[0]), bf(w_exp_down[0])

    tm = 512
    tpb = t // tm
    pk = dict(per_token=False, tiles_per_batch=tpb)
    xp = x_prompt.reshape(bsz * t, d)
    proj = _inproj(xp, gn(0, 0), mod_p[0], SC_M, SH_M, bf(w0r), tm=tm, bn=w0r.shape[1],
                   prec="bf16", vmem_mb=48, **pk)
    mixed, st_t, cst = _gla_conv_prompt(
        proj, bf(wgu_pad), b_gate_up[0].reshape(1, hk), g_head_norm[0].reshape(1, dv), w_conv[0],
        bsz=bsz, t=t, tt=256, heads=heads, dk=dk, dv=dv, cw=cw)
    st_t = st_t.reshape(bsz, heads // 2, 2, dv, 2, dk)
    gla_p = jnp.stack([st_t[:, :, 0, :, 0, :], st_t[:, :, 1, :, 1, :]], axis=2)
    gla_p = jnp.swapaxes(gla_p.reshape(bsz, heads, dv, dk), 2, 3)
    x1 = _outproj(mixed, bf(w_out0[0]), xp, mod_p[0], GT_M, gn(0, 1), tm=tm, prec="bf16",
                  glu=False, **pk)
    x2 = _ffn(x1, gn(0, 2), mod_p[0], SC_F, SH_F, GT_F, gn(0, 3), bf(w_ffn_gate[0]),
              bf(w_ffn_up[0]), bf(w_ffn_down[0]), tm=tm, fc=2816, prec="bf16", vmem_mb=56, **pk)
    u = _inproj(x2, gn(1, 0), mod_p[1], SC_M, SH_M, bf(w_in1[0]), tm=tm, bn=d, prec="bf16", **pk)
    yg, re_p, im_p = _s5_prompt(u.reshape(bsz, t, d), bf(bb_blk), bf(c_blk), ab_re, ab_im, d_blk,
                                tt=128)
    x3 = _outproj(yg.reshape(bsz * t, d), bf(w_glu[0]), x2, mod_p[1], GT_M, gn(1, 1), tm=tm,
                  prec="bf16", glu=True, vmem_mb=48, **pk)
    comb, hb = _router(x3, gn(1, 2), mod_p[1], SC_F, SH_F, wr_pad, tm=tm, n_experts=n_experts,
                       **pk)

    sk = dict(per_token=True, tiles_per_batch=1)
    xs = x_sample.reshape(bs, d)
    proj_s = _inproj(xs, gn(0, 0), mod_s[0], SC_M, SH_M, w0r, tm=bs, bn=640, prec="f32", **sk)
    s_t = jnp.transpose(state_gla[:, 0], (1, 2, 3, 0))
    cbuf_t = jnp.transpose(state_conv[:, 0], (1, 2, 0))
    o_t, y_t, sn_t, cn_t = _gla_conv_step(
        proj_s.T, wgu_pad.T, b_gate_up[0].reshape(hk, 1), g_head_norm[0].reshape(dv, 1),
        w_conv[0].reshape(3, cw, 1), s_t, cbuf_t, heads=heads, dk=dk, dv=dv, cw=cw)
    mixed_s = jnp.concatenate([o_t, y_t], axis=0).T
    gla_s = jnp.transpose(sn_t, (3, 0, 1, 2))
    conv_s = jnp.transpose(cn_t, (2, 0, 1))
    x1s = _outproj(mixed_s, w_out0[0], xs, mod_s[0], GT_M, gn(0, 1), tm=bs, prec="f32", glu=False,
                   **sk)
    x2s = _ffn(x1s, gn(0, 2), mod_s[0], SC_F, SH_F, GT_F, gn(0, 3), w_ffn_gate[0], w_ffn_up[0],
               w_ffn_down[0], tm=bs, fc=256, prec="f32", **sk)
    us = _inproj(x2s, gn(1, 0), mod_s[1], SC_M, SH_M, w_in1[0], tm=bs, bn=256, prec="f32", **sk)
    ygs, re_s, im_s = _s5_step(us, bb_blk, c_blk, ab_re, ab_im, d_blk,
                               state_s5_re[:, 0].reshape(bs, g_s5 * p_s5),
                               state_s5_im[:, 0].reshape(bs, g_s5 * p_s5))
    x3s = _outproj(ygs, w_glu[0], x2s, mod_s[1], GT_M, gn(1, 1), tm=bs, prec="f32", glu=True,
                   vmem_mb=48, **sk)
    comb_s, hb_s = _router(x3s, gn(1, 2), mod_s[1], SC_F, SH_F, wr_pad, tm=bs, n_experts=n_experts,
                           **sk)

    n_p = bsz * t
    pad = -(bs) % MOE_TB
    n_all = n_p + bs + pad
    n_blk = n_all // MOE_TB
    rows_blk = _moe_block_rows(n_experts)
    n_tiles_max = -(-(n_blk * rows_blk) // MOE_TM) + n_experts
    zrow = lambda w, dt: jnp.zeros((pad, w), dt)
    hb_all = jnp.concatenate([hb, hb_s, zrow(d, BF16)], axis=0)
    comb_all = jnp.concatenate([comb, comb_s, zrow(LANES, F32)], axis=0)
    sorted_pos, block_off, seg_cnt, tile_expert, n_tiles = _moe_schedule(comb_all, n_experts,
                                                                         n_tiles_max)
    seg = (sorted_pos, block_off, seg_cnt)
    dest, xs_sorted = _moe_sort(*seg, hb_all, comb_all, n_tiles_max * MOE_TM, n_experts=n_experts)
    ys_sorted = _experts(tile_expert, n_tiles, xs_sorted, w_eg, w_eu, w_ed, fc=1792)
    x4 = _moe_combine(*seg, ys_sorted, comb_all, dest, x3, mod_p[1], GT_F, gn(1, 3), blk0=0,
                      n_experts=n_experts, per_token=False, tiles_per_batch=t // MOE_TB)
    pad_rows = lambda a: jnp.concatenate([a, jnp.zeros((pad, a.shape[1]), a.dtype)], axis=0)
    x4s = _moe_combine(*seg, ys_sorted, comb_all, dest, pad_rows(x3s), pad_rows(mod_s[1]), GT_F,
                       gn(1, 3), blk0=n_p // MOE_TB, n_experts=n_experts, per_token=True,
                       tiles_per_batch=1)[:bs]

    return (x4.reshape(bsz, t, d), x4s.reshape(bs, 1, d),
            gla_p[:, None], cst[:, None],
            re_p.reshape(bsz, 1, g_s5, p_s5), im_p.reshape(bsz, 1, g_s5, p_s5),
            gla_s[:, None], conv_s[:, None],
            re_s.reshape(bs, 1, g_s5, p_s5), im_s.reshape(bs, 1, g_s5, p_s5))
```

```python
import functools
import math

import jax
import jax.numpy as jnp
from jax import lax
from jax.experimental import pallas as pl
from jax.experimental.pallas import tpu as pltpu

F32 = jnp.float32
BF16 = jnp.bfloat16
EPS = 1e-6
LANES = 128
GLA_CHUNK = 64
GATE_NORMALIZER = 16.0
N_MOD = 6
S5_GROUP = 16
S5_CHUNK_GROUPS = 8


def _dot(a, b, dims=None):
    if dims is None:
        return jnp.dot(a, b, preferred_element_type=F32)
    return lax.dot_general(a, b, (dims, ((), ())), preferred_element_type=F32)


def _split(a, parts=2):
    rem = a.astype(F32)
    out = []
    for _ in range(parts - 1):
        piece = rem.astype(BF16)
        out.append(piece)
        rem = rem - piece.astype(F32)
    out.append(rem.astype(BF16))
    return out


def _mm(a, w, prec):
    if prec == "bf16":
        return _dot(a.astype(BF16), w.astype(BF16))
    a1, a2, a3 = _split(a, 3)
    w1, w2, w3 = _split(w, 3)
    small = (_dot(a1, w3) + _dot(a3, w1)) + _dot(a2, w2)
    return _dot(a1, w1) + ((_dot(a1, w2) + _dot(a2, w1)) + small)


def _silu(x):
    return x * jax.nn.sigmoid(x)


def _gelu_tanh(x):
    return 0.5 * x * (1.0 + jnp.tanh(math.sqrt(2.0 / math.pi) * (x + 0.044715 * (x * x * x))))


def _log_sigmoid(x):
    return -(jnp.maximum(-x, 0.0) + jnp.log1p(jnp.exp(-jnp.abs(x))))


def _rms(x):
    return x * lax.rsqrt(jnp.mean(x * x, axis=-1, keepdims=True) + EPS)


def _mod_row(ref):
    return ref[0] if len(ref.shape) == 3 else ref[...]


def _normmod(x, g, sc, sh):
    return (_rms(x) * g) * (1.0 + sc) + sh


def _mod_spec(per_token, tm, d, col, tiles_per_batch):
    if per_token:
        return pl.BlockSpec((tm, d), lambda i, *_: (i, col))
    return pl.BlockSpec((1, 1, d), lambda i, *_: (i // tiles_per_batch, 0, col))


def _params(sem, vmem_mb=None):
    kw = dict(dimension_semantics=sem)
    if vmem_mb is not None:
        kw["vmem_limit_bytes"] = vmem_mb << 20
    return pltpu.CompilerParams(**kw)


def _ada_kernel(c_ref, w_ref, b_ref, o_ref):
    o_ref[0] = _mm(_silu(c_ref[...]), w_ref[0], "f32") + b_ref[0]


def _ada(c_all, w_ada, b_ada):
    depth, d, n6 = w_ada.shape
    rows = c_all.shape[0]
    return pl.pallas_call(
        _ada_kernel,
        grid=(depth, n6 // d),
        in_specs=[pl.BlockSpec((rows, d), lambda l, j: (0, 0)),
                  pl.BlockSpec((1, d, d), lambda l, j: (l, 0, j)),
                  pl.BlockSpec((1, 1, d), lambda l, j: (l, 0, j))],
        out_specs=pl.BlockSpec((1, rows, d), lambda l, j: (l, 0, j)),
        out_shape=jax.ShapeDtypeStruct((depth, rows, n6), F32),
        compiler_params=_params(("parallel", "parallel")),
    )(c_all, w_ada, b_ada.reshape(depth, 1, n6))


def _inproj_kernel(x_ref, g_ref, sc_ref, sh_ref, w_ref, o_ref, *, prec):
    h = _normmod(x_ref[...], g_ref[...], _mod_row(sc_ref), _mod_row(sh_ref))
    o_ref[...] = _mm(h, w_ref[...], prec).astype(o_ref.dtype)


def _inproj(x2d, g, mod, sc_col, sh_col, w, *, tm, bn, prec, per_token, tiles_per_batch,
            out_dtype=F32, vmem_mb=None):
    n_tok, d = x2d.shape
    n_out = w.shape[1]
    return pl.pallas_call(
        functools.partial(_inproj_kernel, prec=prec),
        grid=(n_tok // tm, n_out // bn),
        in_specs=[pl.BlockSpec((tm, d), lambda i, j: (i, 0)),
                  pl.BlockSpec((1, d), lambda i, j: (0, 0)),
                  _mod_spec(per_token, tm, d, sc_col, tiles_per_batch),
                  _mod_spec(per_token, tm, d, sh_col, tiles_per_batch),
                  pl.BlockSpec((d, bn), lambda i, j: (0, j))],
        out_specs=pl.BlockSpec((tm, bn), lambda i, j: (i, j)),
        out_shape=jax.ShapeDtypeStruct((n_tok, n_out), out_dtype),
        compiler_params=_params(("parallel", "arbitrary"), vmem_mb),
    )(x2d, g, mod, mod, w)


def _outproj_kernel(a_ref, w_ref, x_ref, gt_ref, g_ref, o_ref, *, prec, glu):
    z = _mm(a_ref[...], w_ref[...], prec)
    if glu:
        d = z.shape[1] // 2
        z = z[:, :d] * jax.nn.sigmoid(z[:, d:])
    o_ref[...] = x_ref[...] + _mod_row(gt_ref) * (_rms(z) * g_ref[...])


def _outproj(a2d, w, x2d, mod, gt_col, g, *, tm, prec, glu, per_token, tiles_per_batch,
             vmem_mb=None):
    n_tok, d = x2d.shape
    k, n_out = w.shape
    return pl.pallas_call(
        functools.partial(_outproj_kernel, prec=prec, glu=glu),
        grid=(n_tok // tm,),
        in_specs=[pl.BlockSpec((tm, k), lambda i: (i, 0)),
                  pl.BlockSpec((k, n_out), lambda i: (0, 0)),
                  pl.BlockSpec((tm, d), lambda i: (i, 0)),
                  _mod_spec(per_token, tm, d, gt_col, tiles_per_batch),
                  pl.BlockSpec((1, d), lambda i: (0, 0))],
        out_specs=pl.BlockSpec((tm, d), lambda i: (i, 0)),
        out_shape=jax.ShapeDtypeStruct((n_tok, d), F32),
        compiler_params=_params(("parallel",), vmem_mb),
    )(a2d, w, x2d, mod, g)


def _ffn_kernel(x_ref, g1_ref, sc_ref, sh_ref, wg_ref, wu_ref, wd_ref, gt_ref, g2_ref,
                o_ref, h_scr, acc_scr, *, prec, n_c):
    c = pl.program_id(1)

    @pl.when(c == 0)
    def _():
        h = _normmod(x_ref[...], g1_ref[...], _mod_row(sc_ref), _mod_row(sh_ref))
        h_scr[...] = h.astype(h_scr.dtype)
        acc_scr[...] = jnp.zeros_like(acc_scr)

    h = h_scr[...]
    act = _silu(_mm(h, wg_ref[...], prec)) * _mm(h, wu_ref[...], prec)
    acc_scr[...] += _mm(act, wd_ref[...], prec)

    @pl.when(c == n_c - 1)
    def _():
        o_ref[...] = x_ref[...] + _mod_row(gt_ref) * (_rms(acc_scr[...]) * g2_ref[...])


def _ffn(x2d, g1, mod, sc_col, sh_col, gt_col, g2, wg, wu, wd, *, tm, fc, prec,
         per_token, tiles_per_batch, vmem_mb=None):
    n_tok, d = x2d.shape
    n_c = wg.shape[1] // fc
    tok = pl.BlockSpec((tm, d), lambda i, c: (i, 0))
    vec = pl.BlockSpec((1, d), lambda i, c: (0, 0))
    h_dtype = BF16 if prec == "bf16" else F32
    w_mode = dict(pipeline_mode=pl.Buffered(1)) if n_c == 1 else {}
    return pl.pallas_call(
        functools.partial(_ffn_kernel, prec=prec, n_c=n_c),
        grid=(n_tok // tm, n_c),
        in_specs=[tok, vec,
                  _mod_spec(per_token, tm, d, sc_col, tiles_per_batch),
                  _mod_spec(per_token, tm, d, sh_col, tiles_per_batch),
                  pl.BlockSpec((d, fc), lambda i, c: (0, c), **w_mode),
                  pl.BlockSpec((d, fc), lambda i, c: (0, c), **w_mode),
                  pl.BlockSpec((fc, d), lambda i, c: (c, 0), **w_mode),
                  _mod_spec(per_token, tm, d, gt_col, tiles_per_batch), vec],
        out_specs=tok,
        out_shape=jax.ShapeDtypeStruct((n_tok, d), F32),
        scratch_shapes=[pltpu.VMEM((tm, d), h_dtype), pltpu.VMEM((tm, d), F32)],
        compiler_params=_params(("parallel", "arbitrary"), vmem_mb),
    )(x2d, g1, mod, mod, wg, wu, wd, mod, g2)


def _router_kernel(x_ref, g_ref, sc_ref, sh_ref, wr_ref, comb_ref, hb_ref, *, n_experts):
    h = _normmod(x_ref[...], g_ref[...], _mod_row(sc_ref), _mod_row(sh_ref))
    hb_ref[...] = h.astype(hb_ref.dtype)
    logits = _mm(h, wr_ref[...], "f32")
    lane = lax.broadcasted_iota(jnp.int32, logits.shape, 1).astype(F32)
    neg = -jnp.inf
    l1 = jnp.where(lane < n_experts, logits, neg)
    m1 = jnp.max(l1, axis=-1, keepdims=True)
    i1 = jnp.min(jnp.where(l1 == m1, lane, float(LANES)), axis=-1, keepdims=True)
    l2 = jnp.where(lane == i1, neg, l1)
    m2 = jnp.max(l2, axis=-1, keepdims=True)
    i2 = jnp.min(jnp.where(l2 == m2, lane, float(LANES)), axis=-1, keepdims=True)
    e2 = jnp.exp(m2 - m1)
    den = 1.0 + e2
    comb_ref[...] = jnp.where(lane == i1, 1.0 / den, 0.0) + jnp.where(lane == i2, e2 / den, 0.0)


def _router(x2d, g, mod, sc_col, sh_col, wr_pad, *, tm, n_experts, per_token, tiles_per_batch):
    n_tok, d = x2d.shape
    return pl.pallas_call(
        functools.partial(_router_kernel, n_experts=n_experts),
        grid=(n_tok // tm,),
        in_specs=[pl.BlockSpec((tm, d), lambda i: (i, 0)),
                  pl.BlockSpec((1, d), lambda i: (0, 0)),
                  _mod_spec(per_token, tm, d, sc_col, tiles_per_batch),
                  _mod_spec(per_token, tm, d, sh_col, tiles_per_batch),
                  pl.BlockSpec((d, LANES), lambda i: (0, 0))],
        out_specs=[pl.BlockSpec((tm, LANES), lambda i: (i, 0)),
                   pl.BlockSpec((tm, d), lambda i: (i, 0))],
        out_shape=[jax.ShapeDtypeStruct((n_tok, LANES), F32),
                   jax.ShapeDtypeStruct((n_tok, d), BF16)],
        compiler_params=_params(("parallel",)),
    )(x2d, g, mod, mod, wr_pad)


MOE_TB = 256
MOE_SEG = 16
MOE_TM = 512


def _moe_block_rows(n_experts):
    return -(-(2 * MOE_TB + n_experts * (MOE_SEG - 1)) // LANES) * LANES


def _scatter_matrix(dest, comb, weighted, n_rows, n_experts):
    tb = dest.shape[0]
    lane = lax.broadcasted_iota(jnp.int32, (tb, n_rows), 1).astype(F32)
    m = jnp.zeros((tb, n_rows), F32)
    for e in range(n_experts):
        w = comb[:, e:e + 1]
        val = jnp.where(w > 0.0, w if weighted else 1.0, 0.0)
        m = m + jnp.where(lane == dest[:, e:e + 1], val, 0.0)
    return m


_SEG_BITS = (256, 128, 64, 32, 16)


def _segment_copies(make_copy, blk, pos_ref, off_ref, cnt_ref, n_experts, start):
    for e in range(n_experts):
        seg = blk * n_experts + e
        in_blk, in_sorted, n = off_ref[seg], pos_ref[seg], cnt_ref[seg]
        for size in _SEG_BITS:
            has = (n & size) != 0

            @pl.when(has)
            def _(in_blk=in_blk, in_sorted=in_sorted, size=size):
                cp = make_copy(pl.multiple_of(in_blk, MOE_SEG), pl.multiple_of(in_sorted, MOE_SEG), size)
                cp.start() if start else cp.wait()

            step = jnp.where(has, size, 0)
            in_blk, in_sorted = in_blk + step, in_sorted + step


def _moe_sort_kernel(pos_ref, off_ref, cnt_ref, gpos_ref, glen_ref, hbp_ref, combp_ref, hbs_ref,
                     combs_ref, dest_ref, xs_hbm, buf, zbuf, sems, *, n_rows, n_experts, n_blk, n_blk_p):
    b = pl.program_id(0)
    slot = b % 2

    def writes(blk, slot, start):
        def make_copy(block_row, sorted_row, size):
            return pltpu.make_async_copy(buf.at[slot, pl.ds(block_row, size)],
                                         xs_hbm.at[pl.ds(sorted_row, size)], sems.at[slot])
        _segment_copies(make_copy, blk, pos_ref, off_ref, cnt_ref, n_experts, start)

    def zero_gaps(start):
        for e in range(n_experts):
            row, n = gpos_ref[e], glen_ref[e]
            for size in _SEG_BITS:
                has = (n & size) != 0

                @pl.when(has)
                def _(row=row, size=size):
                    cp = pltpu.make_async_copy(
                        zbuf.at[pl.ds(0, size)],
                        xs_hbm.at[pl.ds(pl.multiple_of(row, MOE_SEG), size)], sems.at[2])
                    cp.start() if start else cp.wait()

                row = row + jnp.where(has, size, 0)
        tail_row, tail_n, big = gpos_ref[n_experts], glen_ref[n_experts], _SEG_BITS[0]

        def piece(i, carry):
            cp = pltpu.make_async_copy(
                zbuf.at[pl.ds(0, big)],
                xs_hbm.at[pl.ds(pl.multiple_of(tail_row + i * big, MOE_SEG), big)], sems.at[2])
            cp.start() if start else cp.wait()
            return carry

        lax.fori_loop(0, tail_n // big, piece, 0)

    @pl.when(b == 0)
    def _():
        zbuf[...] = jnp.zeros_like(zbuf)
        zero_gaps(True)

    @pl.when(b >= 2)
    def _():
        writes(b - 2, slot, False)

    in_prompt = b < n_blk_p
    comb = jnp.where(in_prompt, combp_ref[...], combs_ref[...])
    hb = jnp.where(in_prompt, hbp_ref[...], hbs_ref[...])
    tb = comb.shape[0]
    sel = jnp.where(comb > 0.0, 1.0, 0.0)
    r = lax.broadcasted_iota(jnp.int32, (tb, tb), 0)
    c = lax.broadcasted_iota(jnp.int32, (tb, tb), 1)
    rank = _dot(jnp.where(c < r, 1.0, 0.0).astype(BF16), sel.astype(BF16))
    cnt = jnp.sum(sel, axis=0, keepdims=True)
    cnt = jnp.floor((cnt + (MOE_SEG - 1)) / MOE_SEG) * MOE_SEG
    ru = lax.broadcasted_iota(jnp.int32, (LANES, LANES), 0)
    cu = lax.broadcasted_iota(jnp.int32, (LANES, LANES), 1)
    off = _dot(jnp.broadcast_to(cnt, (8, LANES)).astype(BF16),
               jnp.where(ru < cu, 1.0, 0.0).astype(BF16))[0:1]
    dest = off + rank
    dest_ref[...] = dest
    pt = _scatter_matrix(dest, comb, False, n_rows, n_experts).astype(BF16)
    buf[slot] = _dot(pt, hb, ((0,), (0,))).astype(buf.dtype)
    writes(b, slot, True)

    @pl.when(b == n_blk - 1)
    def _():
        writes(b, slot, False)
        if n_blk > 1:
            writes(b - 1, 1 - slot, False)
        zero_gaps(False)


def _moe_sort(seg, gap_pos, gap_len, hb_p, comb_p, hb_s, comb_s, n_sorted_rows, *, n_experts):
    n_p, d = hb_p.shape
    n_blk_p = n_p // MOE_TB
    n_blk = n_blk_p + 1
    n_rows = _moe_block_rows(n_experts)
    last_p = n_blk_p - 1
    return pl.pallas_call(
        functools.partial(_moe_sort_kernel, n_rows=n_rows, n_experts=n_experts, n_blk=n_blk,
                          n_blk_p=n_blk_p),
        grid_spec=pltpu.PrefetchScalarGridSpec(
            num_scalar_prefetch=5, grid=(n_blk,),
            in_specs=[pl.BlockSpec((MOE_TB, d), lambda b, *_: (jnp.minimum(b, last_p), 0)),
                      pl.BlockSpec((MOE_TB, LANES), lambda b, *_: (jnp.minimum(b, last_p), 0)),
                      pl.BlockSpec((MOE_TB, d), lambda b, *_: (0, 0)),
                      pl.BlockSpec((MOE_TB, LANES), lambda b, *_: (0, 0))],
            out_specs=[pl.BlockSpec((MOE_TB, LANES), lambda b, *_: (b, 0)),
                       pl.BlockSpec(memory_space=pl.ANY)],
            scratch_shapes=[pltpu.VMEM((2, n_rows, d), BF16), pltpu.VMEM((_SEG_BITS[0], d), BF16),
                            pltpu.SemaphoreType.DMA((3,))]),
        out_shape=[jax.ShapeDtypeStruct((n_blk * MOE_TB, LANES), F32),
                   jax.ShapeDtypeStruct((n_sorted_rows, d), BF16)],
        compiler_params=_params(("arbitrary",)),
    )(*seg, gap_pos, gap_len, hb_p, comb_p, hb_s, comb_s)


def _experts_kernel(te_ref, nt_ref, x_ref, wg_ref, wu_ref, wd_ref, o_ref, acc_scr, *, n_c):
    del te_ref
    k = pl.program_id(0)
    c = pl.program_id(1)
    active = k < nt_ref[0]

    @pl.when(active)
    def _():
        h = x_ref[...]
        act = _silu(_dot(h, wg_ref[0])) * _dot(h, wu_ref[0])
        y = _dot(act.astype(BF16), wd_ref[0])

        @pl.when(c == 0)
        def _():
            acc_scr[...] = y

        @pl.when(c > 0)
        def _():
            acc_scr[...] += y

        @pl.when(c == n_c - 1)
        def _():
            o_ref[...] = acc_scr[...].astype(o_ref.dtype)

    @pl.when(jnp.logical_not(active) & (c == n_c - 1))
    def _():
        o_ref[...] = jnp.zeros_like(o_ref)


def _experts(tile_expert, n_tiles, xs, wg, wu, wd, *, fc):
    n_rows, d = xs.shape
    n_c = wg.shape[2] // fc

    def row_map(k, c, te, nt):
        return (jnp.minimum(k, nt[0] - 1), 0)

    def w_map(k, c, te, nt):
        return (te[k], 0, jnp.where(k < nt[0], c, n_c - 1))

    def wd_map(k, c, te, nt):
        return (te[k], jnp.where(k < nt[0], c, n_c - 1), 0)

    return pl.pallas_call(
        functools.partial(_experts_kernel, n_c=n_c),
        grid_spec=pltpu.PrefetchScalarGridSpec(
            num_scalar_prefetch=2, grid=(n_rows // MOE_TM, n_c),
            in_specs=[pl.BlockSpec((MOE_TM, d), row_map),
                      pl.BlockSpec((1, d, fc), w_map), pl.BlockSpec((1, d, fc), w_map),
                      pl.BlockSpec((1, fc, d), wd_map)],
            out_specs=pl.BlockSpec((MOE_TM, d), lambda k, c, te, nt: (k, 0)),
            scratch_shapes=[pltpu.VMEM((MOE_TM, d), F32)]),
        out_shape=jax.ShapeDtypeStruct((n_rows, d), BF16),
        compiler_params=_params(("arbitrary", "arbitrary"), 48),
    )(tile_expert, n_tiles, xs, wg, wu, wd)


def _moe_combine_kernel(pos_ref, off_ref, cnt_ref, ys_hbm, comb_ref, dest_ref, x_ref, gt_ref, g_ref,
                        o_ref, buf, sems, *, n_experts, n_blk, blk0):
    b = pl.program_id(0)
    slot = b % 2

    def reads(blk, slot, start):
        def make_copy(block_row, sorted_row, size):
            return pltpu.make_async_copy(ys_hbm.at[pl.ds(sorted_row, size)],
                                         buf.at[slot, pl.ds(block_row, size)], sems.at[slot])
        _segment_copies(make_copy, blk0 + blk, pos_ref, off_ref, cnt_ref, n_experts, start)

    @pl.when(b == 0)
    def _():
        buf[...] = jnp.zeros_like(buf)
        reads(b, slot, True)

    if n_blk > 1:
        @pl.when(b + 1 < n_blk)
        def _():
            reads(b + 1, 1 - slot, True)

    reads(b, slot, False)
    yb = buf[slot]
    wt = _scatter_matrix(dest_ref[...], comb_ref[...], True, yb.shape[0], n_experts)
    w_hi, w_lo = _split(wt)
    y = _dot(w_hi, yb) + _dot(w_lo, yb)
    o_ref[...] = x_ref[...] + _mod_row(gt_ref) * (_rms(y) * g_ref[...])


def _moe_combine(sorted_pos, block_off, seg_cnt, ys, comb, dest, x2d, mod, gt_col, g, *, blk0,
                 n_experts, per_token, tiles_per_batch):
    n_tok, d = x2d.shape
    n_blk = n_tok // MOE_TB
    n_rows = _moe_block_rows(n_experts)
    return pl.pallas_call(
        functools.partial(_moe_combine_kernel, n_experts=n_experts, n_blk=n_blk, blk0=blk0),
        grid_spec=pltpu.PrefetchScalarGridSpec(
            num_scalar_prefetch=3, grid=(n_blk,),
            in_specs=[pl.BlockSpec(memory_space=pl.ANY),
                      pl.BlockSpec((MOE_TB, LANES), lambda b, *_: (b, 0)),
                      pl.BlockSpec((MOE_TB, LANES), lambda b, *_: (blk0 + b, 0)),
                      pl.BlockSpec((MOE_TB, d), lambda b, *_: (b, 0)),
                      _mod_spec(per_token, MOE_TB, d, gt_col, tiles_per_batch),
                      pl.BlockSpec((1, d), lambda b, *_: (0, 0))],
            out_specs=pl.BlockSpec((MOE_TB, d), lambda b, *_: (b, 0)),
            scratch_shapes=[pltpu.VMEM((2, n_rows, d), BF16), pltpu.SemaphoreType.DMA((2,))]),
        out_shape=jax.ShapeDtypeStruct((n_tok, d), F32),
        compiler_params=_params(("arbitrary",)),
    )(sorted_pos, block_off, seg_cnt, ys, comb, dest, x2d, mod, g)


def _moe_schedule(combs, n_experts, n_tiles_max):
    def block_counts(comb):
        n_blk = comb.shape[0] // MOE_TB
        sel = (comb[:, :n_experts] > 0.0).astype(jnp.int32)
        return jnp.sum(sel.reshape(n_blk, MOE_TB, n_experts), axis=1)

    cnt = jnp.concatenate([block_counts(c) for c in combs], axis=0)
    cnt = -(-cnt // MOE_SEG) * MOE_SEG
    off = jnp.cumsum(cnt, axis=1) - cnt
    tot = jnp.sum(cnt, axis=0)
    grp = -(-tot // MOE_TM) * MOE_TM
    g_end = jnp.cumsum(grp)
    sorted_pos = (g_end - grp)[None, :] + (jnp.cumsum(cnt, axis=0) - cnt)
    n_tiles = g_end[-1] // MOE_TM
    first_row = jnp.arange(n_tiles_max, dtype=jnp.int32) * MOE_TM
    tile_expert = jnp.sum(first_row[:, None] >= g_end[None, :], axis=1)
    last = jnp.sum((n_tiles - 1) * MOE_TM >= g_end)
    tile_expert = jnp.minimum(tile_expert, last).astype(jnp.int32)
    flat = lambda a: a.reshape(-1).astype(jnp.int32)
    seg = (flat(sorted_pos), flat(off), flat(cnt))
    gap_pos = flat(jnp.concatenate([g_end - grp + tot, g_end[-1:]]))
    gap_len = flat(jnp.concatenate([grp - tot, n_tiles_max * MOE_TM - g_end[-1:]]))
    return seg, gap_pos, gap_len, tile_expert, n_tiles.reshape(1).astype(jnp.int32)


def _gla_conv_kernel(p_ref, wgu_ref, bgu_ref, gh_ref, wc_ref, mixed_ref, st_ref, cst_ref,
                     s_scr, uext_scr, *, tt, n_t, heads, dk, dv, cw):
    ti = pl.program_id(1)
    hk = heads * dk
    hv = heads * dv
    o_q, o_k, o_v, o_og = 0, hk, 2 * hk, 2 * hk + hv
    o_ch = o_og + hv
    o_cb, o_cc, o_lr = o_ch + cw, o_ch + 2 * cw, o_ch + 3 * cw
    L = GLA_CHUNK

    @pl.when(ti == 0)
    def _():
        s_scr[...] = jnp.zeros_like(s_scr)
        uext_scr[0:8, :] = jnp.zeros((8, cw), F32)

    logit = _dot(p_ref[:, o_lr:o_lr + LANES].astype(BF16), wgu_ref[...]) + bgu_ref[...]
    logg = _log_sigmoid(logit) / GATE_NORMALIZER

    r = lax.broadcasted_iota(jnp.int32, (tt, tt), 0)
    c = lax.broadcasted_iota(jnp.int32, (tt, tt), 1)
    tri = jnp.where((r // L == c // L) & (c <= r), 1.0, 0.0).astype(BF16)
    g_hi = logg.astype(BF16)
    rem = logg - g_hi.astype(F32)
    g_mid = rem.astype(BF16)
    g_lo = (rem - g_mid.astype(F32)).astype(BF16)
    bc = _dot(tri, g_hi) + (_dot(tri, g_mid) + _dot(tri, g_lo))

    rl = lax.broadcasted_iota(jnp.int32, (L, L), 0)
    cl = lax.broadcasted_iota(jnp.int32, (L, L), 1)
    tril = cl <= rl
    lane = lax.broadcasted_iota(jnp.int32, (L, LANES), 1)
    heads_per_blk = LANES // dk
    nt_dims = ((1,), (1,))

    for ck in range(tt // L):
        rows = slice(L * ck, L * ck + L)
        b = bc[rows]
        bl = b[L - 1:L]
        q_in = (p_ref[rows, o_q:o_q + hk] * (dk ** -0.5)) * jnp.exp(b)
        k = p_ref[rows, o_k:o_k + hk]
        k_out = (k * jnp.exp(-b)).astype(BF16)
        k_dec = (k * jnp.exp(bl - b)).astype(BF16)
        dec = jnp.exp(bl)
        q_in = q_in.astype(BF16)
        for h in range(heads):
            blk = slice(LANES * (h // heads_per_blk), LANES * (h // heads_per_blk) + LANES)
            in_head = (lane // dk) == (h % heads_per_blk)
            qm = jnp.where(in_head, q_in[:, blk], jnp.zeros_like(q_in[:, blk]))
            att = _dot(qm, k_out[:, blk], nt_dims)
            att = jnp.where(tril, att, 0.0).astype(BF16)
            vh = p_ref[rows, o_v + dv * h:o_v + dv * h + dv].astype(BF16)
            s_t = s_scr[h]
            o = _dot(att, vh) + _dot(qm, s_t.astype(BF16), nt_dims)
            ds_t = _dot(vh, k_dec[:, blk], ((0,), (0,)))
            s_scr[h] = s_t * dec[:, blk] + ds_t
            og = p_ref[rows, o_og + dv * h:o_og + dv * h + dv]
            res = (_rms(o) * gh_ref[...]) * _silu(og)
            mixed_ref[rows, dv * h:dv * h + dv] = res.astype(mixed_ref.dtype)

    u = p_ref[:, o_cc:o_cc + cw] * p_ref[:, o_ch:o_ch + cw]
    uext_scr[8:8 + tt, :] = u
    y = (wc_ref[0:1, :] * uext_scr[6:6 + tt, :] + wc_ref[1:2, :] * uext_scr[7:7 + tt, :]
         + wc_ref[2:3, :] * u)
    mixed_ref[:, hv:hv + cw] = (p_ref[:, o_cb:o_cb + cw] * y).astype(mixed_ref.dtype)
    tail = uext_scr[tt + 6:tt + 8, :]
    uext_scr[6:8, :] = tail

    @pl.when(ti == n_t - 1)
    def _():
        st_ref[0] = s_scr[...]
        cst_ref[0] = tail


def _gla_conv_prompt(proj, wgu_pad, bgu, g_head, w_conv, *, bsz, t, tt, heads, dk, dv, cw):
    n_t = t // tt
    n_in = proj.shape[1]
    width = heads * dv + cw
    return pl.pallas_call(
        functools.partial(_gla_conv_kernel, tt=tt, n_t=n_t, heads=heads, dk=dk, dv=dv, cw=cw),
        grid=(bsz, n_t),
        in_specs=[pl.BlockSpec((tt, n_in), lambda b, i: (b * n_t + i, 0)),
                  pl.BlockSpec(wgu_pad.shape, lambda b, i: (0, 0)),
                  pl.BlockSpec(bgu.shape, lambda b, i: (0, 0)),
                  pl.BlockSpec(g_head.shape, lambda b, i: (0, 0)),
                  pl.BlockSpec(w_conv.shape, lambda b, i: (0, 0))],
        out_specs=[pl.BlockSpec((tt, width), lambda b, i: (b * n_t + i, 0)),
                   pl.BlockSpec((1, heads, dv, LANES), lambda b, i: (b, 0, 0, 0)),
                   pl.BlockSpec((1, 2, cw), lambda b, i: (b, 0, 0))],
        out_shape=[jax.ShapeDtypeStruct((bsz * t, width), BF16),
                   jax.ShapeDtypeStruct((bsz, heads, dv, LANES), F32),
                   jax.ShapeDtypeStruct((bsz, 2, cw), F32)],
        scratch_shapes=[pltpu.VMEM((heads, dv, LANES), F32), pltpu.VMEM((tt + 8, cw), F32)],
        compiler_params=_params(("parallel", "arbitrary"), 48),
    )(proj, wgu_pad, bgu, g_head, w_conv)


def _gla_conv_step_kernel(q_ref, k_ref, v_ref, og_ref, ch_ref, cb_ref, cc_ref, lr_ref,
                          wgu_ref, bgu_ref, gh_ref, wc_ref, s_ref, cbuf_ref,
                          o_ref, y_ref, sn_ref, cn_ref, a_scr, *, dk):
    logit = _mm(wgu_ref[...], lr_ref[...], "f32") + bgu_ref[...]
    a_scr[...] = jnp.exp(_log_sigmoid(logit) / GATE_NORMALIZER)
    v_t = v_ref[...]

    def body(d, acc):
        a = a_scr[pl.ds(d, 1), :]
        kd = k_ref[pl.ds(d, 1), :]
        qd = q_ref[pl.ds(d, 1), :]
        s_new = a * s_ref[0, d] + kd * v_t
        sn_ref[0, d] = s_new
        return acc + (qd * (dk ** -0.5)) * s_new

    o = lax.fori_loop(0, dk, body, jnp.zeros(v_t.shape, F32))
    o = o * lax.rsqrt(jnp.mean(o * o, axis=0, keepdims=True) + EPS) * gh_ref[...]
    o_ref[...] = o * _silu(og_ref[...])
    u = cc_ref[...] * ch_ref[...]
    y = wc_ref[0] * cbuf_ref[0] + wc_ref[1] * cbuf_ref[1] + wc_ref[2] * u
    y_ref[...] = cb_ref[...] * y
    cn_ref[0] = cbuf_ref[1]
    cn_ref[1] = u


def _gla_conv_step(proj_t, wgu_t_pad, bgu_col, gh_col, wc_col, s_t, cbuf_t, *, heads, dk, dv, cw):
    bsz = proj_t.shape[1]
    hk, hv = heads * dk, heads * dv
    cs = cw // heads
    assert dv == LANES and cs == LANES and dk * 2 == LANES
    o_k, o_v, o_og = hk, 2 * hk, 2 * hk + hv
    o_ch = o_og + hv
    o_cb, o_cc, o_lr = o_ch + cw, o_ch + 2 * cw, o_ch + 3 * cw
    row = lambda off, size: (lambda h: (off // size + h, 0))
    blk = lambda size, off: pl.BlockSpec((size, bsz), row(off, size))
    return pl.pallas_call(
        functools.partial(_gla_conv_step_kernel, dk=dk),
        grid=(heads,),
        in_specs=[blk(dk, 0), blk(dk, o_k), blk(dv, o_v), blk(dv, o_og),
                  blk(cs, o_ch), blk(cs, o_cb), blk(cs, o_cc),
                  pl.BlockSpec((LANES, bsz), lambda h: (o_lr // LANES, 0)),
                  pl.BlockSpec((dk, LANES), lambda h: (h, 0)),
                  pl.BlockSpec((dk, 1), lambda h: (h, 0)),
                  pl.BlockSpec((dv, 1), lambda h: (0, 0)),
                  pl.BlockSpec((3, cs, 1), lambda h: (0, h, 0)),
                  pl.BlockSpec((1, dk, dv, bsz), lambda h: (h, 0, 0, 0)),
                  pl.BlockSpec((2, cs, bsz), lambda h: (0, h, 0))],
        out_specs=[pl.BlockSpec((dv, bsz), lambda h: (h, 0)),
                   pl.BlockSpec((cs, bsz), lambda h: (h, 0)),
                   pl.BlockSpec((1, dk, dv, bsz), lambda h: (h, 0, 0, 0)),
                   pl.BlockSpec((2, cs, bsz), lambda h: (0, h, 0))],
        out_shape=[jax.ShapeDtypeStruct((hv, bsz), F32),
                   jax.ShapeDtypeStruct((cw, bsz), F32),
                   jax.ShapeDtypeStruct(s_t.shape, F32),
                   jax.ShapeDtypeStruct(cbuf_t.shape, F32)],
        scratch_shapes=[pltpu.VMEM((dk, bsz), F32)],
        compiler_params=_params(("parallel",)),
    )(proj_t, proj_t, proj_t, proj_t, proj_t, proj_t, proj_t, proj_t,
      wgu_t_pad, bgu_col, gh_col, wc_col, s_t, cbuf_t)


S5_ROW_PAD = 8


def _s5_scan_kernel(u_ref, bb_ref, cm_ref, abr_ref, abi_ref, d_ref, wa_ref, wb_ref, wc_ref,
                    y_ref, sre_ref, sim_ref, wa_out, wb_out, wc_out,
                    scr, xr_scr, xi_scr, *, tt, n_t, nb, half):
    ti = pl.program_id(1)
    stride = tt + S5_ROW_PAD
    n_blk = half // LANES
    wa_out[...] = wa_ref[...].astype(wa_out.dtype)
    wb_out[...] = wb_ref[...].astype(wb_out.dtype)
    wc_out[...] = wc_ref[...].astype(wc_out.dtype)

    @pl.when(ti == 0)
    def _():
        xr_scr[...] = jnp.zeros_like(xr_scr)
        xi_scr[...] = jnp.zeros_like(xi_scr)

    for b in range(nb):
        bu = _dot(u_ref[b].astype(BF16), bb_ref[0])
        for cb in range(2 * n_blk):
            scr[cb, b * stride:b * stride + tt, :] = bu[:, cb * LANES:(cb + 1) * LANES]
    a_re = [jnp.broadcast_to(abr_ref[0][:, cb * LANES:(cb + 1) * LANES], (nb, LANES))
            for cb in range(n_blk)]
    a_im = [jnp.broadcast_to(abi_ref[0][:, cb * LANES:(cb + 1) * LANES], (nb, LANES))
            for cb in range(n_blk)]

    def body(t, carry):
        rows = pl.ds(t, nb, stride=stride)
        out = []
        for cb in range(n_blk):
            xr, xi = carry[2 * cb], carry[2 * cb + 1]
            nr = (a_re[cb] * xr - a_im[cb] * xi) + scr[cb, rows, :]
            ni = (a_re[cb] * xi + a_im[cb] * xr) + scr[n_blk + cb, rows, :]
            scr[cb, rows, :] = nr
            scr[n_blk + cb, rows, :] = ni
            out += [nr, ni]
        return tuple(out)

    init = []
    for cb in range(n_blk):
        init += [xr_scr[:, cb * LANES:(cb + 1) * LANES], xi_scr[:, cb * LANES:(cb + 1) * LANES]]
    fin = lax.fori_loop(0, tt, body, tuple(init), unroll=2)
    xr = jnp.concatenate([fin[2 * cb] for cb in range(n_blk)], axis=1)
    xi = jnp.concatenate([fin[2 * cb + 1] for cb in range(n_blk)], axis=1)
    xr_scr[...] = xr
    xi_scr[...] = xi
    for b in range(nb):
        xs = jnp.concatenate([scr[cb, b * stride:b * stride + tt, :] for cb in range(2 * n_blk)],
                             axis=1)
        y = _dot(xs.astype(BF16), cm_ref[0]) + d_ref[0] * u_ref[b]
        y_ref[b] = _gelu_tanh(y).astype(y_ref.dtype)

    @pl.when(ti == n_t - 1)
    def _():
        sre_ref[...] = xr
        sim_ref[...] = xi


def _s5_prompt(u3d, bb_blk, c_blk, ab_re, ab_im, d_blk, cast_ws, *, tt):
    nb, t, d = u3d.shape
    n_j = d // LANES
    n_t = t // tt
    half = bb_blk.shape[2] // 2
    n_steps = n_j * n_t

    def slab_spec(w):
        n_e, rows, cols = w.shape
        per_e = n_steps // n_e
        slab = rows // per_e
        assert n_e * per_e == n_steps and slab * per_e == rows and slab % 16 == 0
        return pl.BlockSpec((1, slab, cols),
                            lambda j, i: ((j * n_t + i) // per_e, (j * n_t + i) % per_e, 0))

    w_specs = [slab_spec(w) for w in cast_ws]
    return pl.pallas_call(
        functools.partial(_s5_scan_kernel, tt=tt, n_t=n_t, nb=nb, half=half),
        grid=(n_j, n_t),
        in_specs=[pl.BlockSpec((nb, tt, LANES), lambda j, i: (0, i, j)),
                  pl.BlockSpec((1, LANES, 2 * half), lambda j, i: (j, 0, 0)),
                  pl.BlockSpec((1, 2 * half, LANES), lambda j, i: (j, 0, 0)),
                  pl.BlockSpec((1, 1, half), lambda j, i: (j, 0, 0)),
                  pl.BlockSpec((1, 1, half), lambda j, i: (j, 0, 0)),
                  pl.BlockSpec((1, 1, LANES), lambda j, i: (j, 0, 0))] + w_specs,
        out_specs=[pl.BlockSpec((nb, tt, LANES), lambda j, i: (0, i, j)),
                   pl.BlockSpec((nb, half), lambda j, i: (0, j)),
                   pl.BlockSpec((nb, half), lambda j, i: (0, j))] + w_specs,
        out_shape=[jax.ShapeDtypeStruct((nb, t, d), BF16),
                   jax.ShapeDtypeStruct((nb, n_j * half), F32),
                   jax.ShapeDtypeStruct((nb, n_j * half), F32)]
        + [jax.ShapeDtypeStruct(w.shape, BF16) for w in cast_ws],
        scratch_shapes=[pltpu.VMEM((2 * half // LANES, nb * (tt + S5_ROW_PAD), LANES), F32),
                        pltpu.VMEM((nb, half), F32), pltpu.VMEM((nb, half), F32)],
        compiler_params=_params(("arbitrary", "arbitrary"), 48),
    )(u3d, bb_blk, c_blk, ab_re, ab_im, d_blk, *cast_ws)


def _s5_step_kernel(u_ref, bb_ref, cm_ref, abr_ref, abi_ref, d_ref, sre_ref, sim_ref,
                    y_ref, nre_ref, nim_ref, *, half):
    u = u_ref[...]
    bu = _mm(u, bb_ref[0], "f32")
    ar, ai = abr_ref[0], abi_ref[0]
    sr, si = sre_ref[...], sim_ref[...]
    nr = (ar * sr - ai * si) + bu[:, 0:half]
    ni = (ar * si + ai * sr) + bu[:, half:2 * half]
    nre_ref[...] = nr
    nim_ref[...] = ni
    y = _mm(jnp.concatenate([nr, ni], axis=1), cm_ref[0], "f32") + d_ref[0] * u
    y_ref[...] = _gelu_tanh(y)


def _s5_step(u2d, bb_blk, c_blk, ab_re, ab_im, d_blk, s_re, s_im):
    bsz, d = u2d.shape
    n_j = d // LANES
    half = bb_blk.shape[2] // 2
    st = pl.BlockSpec((bsz, half), lambda j: (0, j))
    return pl.pallas_call(
        functools.partial(_s5_step_kernel, half=half),
        grid=(n_j,),
        in_specs=[pl.BlockSpec((bsz, LANES), lambda j: (0, j)),
                  pl.BlockSpec((1, LANES, 2 * half), lambda j: (j, 0, 0)),
                  pl.BlockSpec((1, 2 * half, LANES), lambda j: (j, 0, 0)),
                  pl.BlockSpec((1, 1, half), lambda j: (j, 0, 0)),
                  pl.BlockSpec((1, 1, half), lambda j: (j, 0, 0)),
                  pl.BlockSpec((1, 1, LANES), lambda j: (j, 0, 0)),
                  st, st],
        out_specs=[pl.BlockSpec((bsz, LANES), lambda j: (0, j)), st, st],
        out_shape=[jax.ShapeDtypeStruct((bsz, d), F32),
                   jax.ShapeDtypeStruct(s_re.shape, F32),
                   jax.ShapeDtypeStruct(s_im.shape, F32)],
        compiler_params=_params(("parallel",)),
    )(u2d, bb_blk, c_blk, ab_re, ab_im, d_blk, s_re, s_im)


def _s5_zoh(a_re, a_im, log_dt, b_re, b_im):
    dt = jnp.exp(log_dt)[:, None]
    mag = jnp.exp(dt * a_re)
    ab_re, ab_im = mag * jnp.cos(dt * a_im), mag * jnp.sin(dt * a_im)
    den = a_re * a_re + a_im * a_im
    nr, ni = ab_re - 1.0, ab_im
    f_re = (nr * a_re + ni * a_im) / den
    f_im = (ni * a_re - nr * a_im) / den
    bb_re = f_re[..., None] * b_re - f_im[..., None] * b_im
    bb_im = f_re[..., None] * b_im + f_im[..., None] * b_re
    return ab_re, ab_im, bb_re, bb_im


def _s5_step_blocks(ab_re, ab_im, bb_re, bb_im, c_re, c_im, d_skip):
    g, p = ab_re.shape
    cg = S5_CHUNK_GROUPS
    n_j = g // cg
    eye = jnp.eye(cg, dtype=F32)

    def in_blk(m):
        m = m.reshape(n_j, cg, p, S5_GROUP)
        return jnp.einsum("jgpi,gh->jgihp", m, eye).reshape(n_j, cg * S5_GROUP, cg * p)

    def out_blk(m):
        m = m.reshape(n_j, cg, S5_GROUP, p)
        return jnp.einsum("jgip,gh->jgphi", m, eye).reshape(n_j, cg * p, cg * S5_GROUP)

    bb_blk = jnp.concatenate([in_blk(bb_re), in_blk(bb_im)], axis=2)
    c_blk = jnp.concatenate([out_blk(c_re), out_blk(-c_im)], axis=1)
    return (bb_blk, c_blk, ab_re.reshape(n_j, 1, cg * p), ab_im.reshape(n_j, 1, cg * p),
            d_skip.reshape(n_j, 1, cg * S5_GROUP))


def kernel(x_prompt, x_sample, c_prompt, c_sample, state_gla, state_conv, state_s5_re, state_s5_im, w_ada, b_ada, g_norm, w_in0, w_gate_up, b_gate_up, g_head_norm, w_conv, w_out0, w_ffn_gate, w_ffn_up, w_ffn_down, w_in1, s5_a_re, s5_a_im, s5_log_dt, s5_b_re, s5_b_im, s5_c_re, s5_c_im, s5_d, w_glu, w_router, w_exp_gate, w_exp_up, w_exp_down):
    bsz, t, d = x_prompt.shape
    bs = x_sample.shape[0]
    heads, dk, dv = state_gla.shape[2:]
    cw = state_conv.shape[3]
    lowrank = w_gate_up.shape[1]
    hk, hv = heads * dk, heads * dv
    n_experts = w_router.shape[2]
    g_s5, p_s5 = s5_a_re.shape[1:]

    mod = _ada(jnp.concatenate([c_prompt, c_sample], axis=0), w_ada, b_ada)
    mod_p = [mod[l, :bsz].reshape(bsz, 1, N_MOD * d) for l in range(2)]
    mod_s = [mod[l, bsz:] for l in range(2)]
    SH_M, SC_M, GT_M, SH_F, SC_F, GT_F = range(N_MOD)
    gn = lambda l, k: g_norm[l, k].reshape(1, d)

    w0 = w_in0[0]
    o_lr = 2 * hk + 2 * hv
    w0r = jnp.concatenate([w0[:, :o_lr], w0[:, o_lr + lowrank:], w0[:, o_lr:o_lr + lowrank],
                           jnp.zeros((d, LANES - lowrank), F32)], axis=1)
    wgu_pad = jnp.concatenate([w_gate_up[0], jnp.zeros((LANES - lowrank, hk), F32)], axis=0)
    zoh = _s5_zoh(s5_a_re[0], s5_a_im[0], s5_log_dt[0], s5_b_re[0], s5_b_im[0])
    bb_blk, c_blk, ab_re, ab_im, d_blk = _s5_step_blocks(*zoh, s5_c_re[0], s5_c_im[0], s5_d[0])
    wr_pad = jnp.concatenate([w_router[0], jnp.zeros((d, LANES - n_experts), F32)], axis=1)
    bf = lambda a: a.astype(BF16)

    tm = 512
    tpb = t // tm
    pk = dict(per_token=False, tiles_per_batch=tpb)
    xp = x_prompt.reshape(bsz * t, d)
    proj = _inproj(xp, gn(0, 0), mod_p[0], SC_M, SH_M, bf(w0r), tm=tm, bn=w0r.shape[1],
                   prec="bf16", vmem_mb=48, **pk)
    mixed, st_t, cst = _gla_conv_prompt(
        proj, bf(wgu_pad), b_gate_up[0].reshape(1, hk), g_head_norm[0].reshape(1, dv), w_conv[0],
        bsz=bsz, t=t, tt=256, heads=heads, dk=dk, dv=dv, cw=cw)
    st_t = st_t.reshape(bsz, heads // 2, 2, dv, 2, dk)
    gla_p = jnp.stack([st_t[:, :, 0, :, 0, :], st_t[:, :, 1, :, 1, :]], axis=2)
    gla_p = jnp.swapaxes(gla_p.reshape(bsz, heads, dv, dk), 2, 3)
    x1 = _outproj(mixed, bf(w_out0[0]), xp, mod_p[0], GT_M, gn(0, 1), tm=tm, prec="bf16",
                  glu=False, **pk)
    x2 = _ffn(x1, gn(0, 2), mod_p[0], SC_F, SH_F, GT_F, gn(0, 3), bf(w_ffn_gate[0]),
              bf(w_ffn_up[0]), bf(w_ffn_down[0]), tm=tm, fc=2816, prec="bf16", vmem_mb=56, **pk)
    u = _inproj(x2, gn(1, 0), mod_p[1], SC_M, SH_M, bf(w_in1[0]), tm=tm, bn=d, prec="bf16", **pk)
    yg, re_p, im_p, w_eg, w_eu, w_ed = _s5_prompt(
        u.reshape(bsz, t, d), bf(bb_blk), bf(c_blk), ab_re, ab_im, d_blk,
        (w_exp_gate[0], w_exp_up[0], w_exp_down[0]), tt=256)
    x3 = _outproj(yg.reshape(bsz * t, d), bf(w_glu[0]), x2, mod_p[1], GT_M, gn(1, 1), tm=tm,
                  prec="bf16", glu=True, vmem_mb=48, **pk)
    comb, hb = _router(x3, gn(1, 2), mod_p[1], SC_F, SH_F, wr_pad, tm=tm, n_experts=n_experts,
                       **pk)

    sk = dict(per_token=True, tiles_per_batch=1)
    xs = x_sample.reshape(bs, d)
    proj_s = _inproj(xs, gn(0, 0), mod_s[0], SC_M, SH_M, w0r, tm=bs, bn=640, prec="f32", **sk)
    s_t = jnp.transpose(state_gla[:, 0], (1, 2, 3, 0))
    cbuf_t = jnp.transpose(state_conv[:, 0], (1, 2, 0))
    o_t, y_t, sn_t, cn_t = _gla_conv_step(
        proj_s.T, wgu_pad.T, b_gate_up[0].reshape(hk, 1), g_head_norm[0].reshape(dv, 1),
        w_conv[0].reshape(3, cw, 1), s_t, cbuf_t, heads=heads, dk=dk, dv=dv, cw=cw)
    mixed_s = jnp.concatenate([o_t, y_t], axis=0).T
    gla_s = jnp.transpose(sn_t, (3, 0, 1, 2))
    conv_s = jnp.transpose(cn_t, (2, 0, 1))
    x1s = _outproj(mixed_s, w_out0[0], xs, mod_s[0], GT_M, gn(0, 1), tm=bs, prec="f32", glu=False,
                   **sk)
    x2s = _ffn(x1s, gn(0, 2), mod_s[0], SC_F, SH_F, GT_F, gn(0, 3), w_ffn_gate[0], w_ffn_up[0],
               w_ffn_down[0], tm=bs, fc=256, prec="f32", **sk)
    us = _inproj(x2s, gn(1, 0), mod_s[1], SC_M, SH_M, w_in1[0], tm=bs, bn=256, prec="f32", **sk)
    ygs, re_s, im_s = _s5_step(us, bb_blk, c_blk, ab_re, ab_im, d_blk,
                               state_s5_re[:, 0].reshape(bs, g_s5 * p_s5),
                               state_s5_im[:, 0].reshape(bs, g_s5 * p_s5))
    x3s = _outproj(ygs, w_glu[0], x2s, mod_s[1], GT_M, gn(1, 1), tm=bs, prec="f32", glu=True,
                   vmem_mb=48, **sk)
    comb_s, hb_s = _router(x3s, gn(1, 2), mod_s[1], SC_F, SH_F, wr_pad, tm=bs, n_experts=n_experts,
                           **sk)

    assert bs <= MOE_TB
    n_blk = bsz * t // MOE_TB + 1
    n_tiles_max = -(-(n_blk * _moe_block_rows(n_experts)) // MOE_TM) + n_experts
    pad_rows = lambda a: jnp.concatenate([a, jnp.zeros((MOE_TB - bs, a.shape[1]), a.dtype)], axis=0)
    hb_s, comb_s = pad_rows(hb_s), pad_rows(comb_s)
    seg, gap_pos, gap_len, tile_expert, n_tiles = _moe_schedule((comb, comb_s), n_experts,
                                                                n_tiles_max)
    dest, xs_sorted = _moe_sort(seg, gap_pos, gap_len, hb, comb, hb_s, comb_s,
                                n_tiles_max * MOE_TM, n_experts=n_experts)
    ys_sorted = _experts(tile_expert, n_tiles, xs_sorted, w_eg, w_eu, w_ed, fc=1792)
    x4 = _moe_combine(*seg, ys_sorted, comb, dest, x3, mod_p[1], GT_F, gn(1, 3), blk0=0,
                      n_experts=n_experts, per_token=False, tiles_per_batch=t // MOE_TB)
    x4s = _moe_combine(*seg, ys_sorted, comb_s, dest, pad_rows(x3s), pad_rows(mod_s[1]), GT_F,
                       gn(1, 3), blk0=n_blk - 1, n_experts=n_experts, per_token=True,
                       tiles_per_batch=1)[:bs]

    return (x4.reshape(bsz, t, d), x4s.reshape(bs, 1, d),
            gla_p[:, None], cst[:, None],
            re_p.reshape(bsz, 1, g_s5, p_s5), im_p.reshape(bsz, 1, g_s5, p_s5),
            gla_s[:, None], conv_s[:, None],
            re_s.reshape(bs, 1, g_s5, p_s5), im_s.reshape(bs, 1, g_s5, p_s5))
```

```python
import functools
import math

import jax
import jax.numpy as jnp
from jax import lax
from jax.experimental import pallas as pl
from jax.experimental.pallas import tpu as pltpu

F32 = jnp.float32
BF16 = jnp.bfloat16
EPS = 1e-6
LANES = 128
GLA_CHUNK = 64
GATE_NORMALIZER = 16.0
N_MOD = 6
S5_GROUP = 16
S5_CHUNK_GROUPS = 8


def _dot(a, b, dims=None):
    if dims is None:
        return jnp.dot(a, b, preferred_element_type=F32)
    return lax.dot_general(a, b, (dims, ((), ())), preferred_element_type=F32)


def _split(a, parts=2):
    rem = a.astype(F32)
    out = []
    for _ in range(parts - 1):
        piece = rem.astype(BF16)
        out.append(piece)
        rem = rem - piece.astype(F32)
    out.append(rem.astype(BF16))
    return out


def _mm(a, w, prec):
    if prec == "bf16":
        return _dot(a.astype(BF16), w.astype(BF16))
    a1, a2, a3 = _split(a, 3)
    w1, w2, w3 = _split(w, 3)
    small = (_dot(a1, w3) + _dot(a3, w1)) + _dot(a2, w2)
    return _dot(a1, w1) + ((_dot(a1, w2) + _dot(a2, w1)) + small)


def _silu(x):
    return x * jax.nn.sigmoid(x)


def _gelu_tanh(x):
    return 0.5 * x * (1.0 + jnp.tanh(math.sqrt(2.0 / math.pi) * (x + 0.044715 * (x * x * x))))


def _log_sigmoid(x):
    return -(jnp.maximum(-x, 0.0) + jnp.log1p(jnp.exp(-jnp.abs(x))))


def _rms(x):
    return x * lax.rsqrt(jnp.mean(x * x, axis=-1, keepdims=True) + EPS)


def _mod_row(ref):
    return ref[0] if len(ref.shape) == 3 else ref[...]


def _normmod(x, g, sc, sh):
    return (_rms(x) * g) * (1.0 + sc) + sh


def _mod_spec(per_token, tm, d, col, tiles_per_batch):
    if per_token:
        return pl.BlockSpec((tm, d), lambda i, *_: (i, col))
    return pl.BlockSpec((1, 1, d), lambda i, *_: (i // tiles_per_batch, 0, col))


def _params(sem, vmem_mb=None):
    kw = dict(dimension_semantics=sem)
    if vmem_mb is not None:
        kw["vmem_limit_bytes"] = vmem_mb << 20
    return pltpu.CompilerParams(**kw)


def _ada_kernel(c_ref, w_ref, b_ref, o_ref):
    o_ref[0] = _mm(_silu(c_ref[...]), w_ref[0], "f32") + b_ref[0]


def _ada(c_all, w_ada, b_ada):
    depth, d, n6 = w_ada.shape
    rows = c_all.shape[0]
    return pl.pallas_call(
        _ada_kernel,
        grid=(depth, n6 // d),
        in_specs=[pl.BlockSpec((rows, d), lambda l, j: (0, 0)),
                  pl.BlockSpec((1, d, d), lambda l, j: (l, 0, j)),
                  pl.BlockSpec((1, 1, d), lambda l, j: (l, 0, j))],
        out_specs=pl.BlockSpec((1, rows, d), lambda l, j: (l, 0, j)),
        out_shape=jax.ShapeDtypeStruct((depth, rows, n6), F32),
        compiler_params=_params(("parallel", "parallel")),
    )(c_all, w_ada, b_ada.reshape(depth, 1, n6))


def _inproj_kernel(x_ref, g_ref, sc_ref, sh_ref, w_ref, o_ref, *, prec):
    h = _normmod(x_ref[...], g_ref[...], _mod_row(sc_ref), _mod_row(sh_ref))
    o_ref[...] = _mm(h, w_ref[...], prec).astype(o_ref.dtype)


def _inproj(x2d, g, mod, sc_col, sh_col, w, *, tm, bn, prec, per_token, tiles_per_batch,
            out_dtype=F32, vmem_mb=None):
    n_tok, d = x2d.shape
    n_out = w.shape[1]
    return pl.pallas_call(
        functools.partial(_inproj_kernel, prec=prec),
        grid=(n_tok // tm, n_out // bn),
        in_specs=[pl.BlockSpec((tm, d), lambda i, j: (i, 0)),
                  pl.BlockSpec((1, d), lambda i, j: (0, 0)),
                  _mod_spec(per_token, tm, d, sc_col, tiles_per_batch),
                  _mod_spec(per_token, tm, d, sh_col, tiles_per_batch),
                  pl.BlockSpec((d, bn), lambda i, j: (0, j))],
        out_specs=pl.BlockSpec((tm, bn), lambda i, j: (i, j)),
        out_shape=jax.ShapeDtypeStruct((n_tok, n_out), out_dtype),
        compiler_params=_params(("parallel", "arbitrary"), vmem_mb),
    )(x2d, g, mod, mod, w)


def _outproj_kernel(a_ref, w_ref, x_ref, gt_ref, g_ref, o_ref, *, prec, glu):
    z = _mm(a_ref[...], w_ref[...], prec)
    if glu:
        d = z.shape[1] // 2
        z = z[:, :d] * jax.nn.sigmoid(z[:, d:])
    o_ref[...] = x_ref[...] + _mod_row(gt_ref) * (_rms(z) * g_ref[...])


def _outproj(a2d, w, x2d, mod, gt_col, g, *, tm, prec, glu, per_token, tiles_per_batch,
             vmem_mb=None):
    n_tok, d = x2d.shape
    k, n_out = w.shape
    return pl.pallas_call(
        functools.partial(_outproj_kernel, prec=prec, glu=glu),
        grid=(n_tok // tm,),
        in_specs=[pl.BlockSpec((tm, k), lambda i: (i, 0)),
                  pl.BlockSpec((k, n_out), lambda i: (0, 0)),
                  pl.BlockSpec((tm, d), lambda i: (i, 0)),
                  _mod_spec(per_token, tm, d, gt_col, tiles_per_batch),
                  pl.BlockSpec((1, d), lambda i: (0, 0))],
        out_specs=pl.BlockSpec((tm, d), lambda i: (i, 0)),
        out_shape=jax.ShapeDtypeStruct((n_tok, d), F32),
        compiler_params=_params(("parallel",), vmem_mb),
    )(a2d, w, x2d, mod, g)


def _ffn_kernel(x_ref, g1_ref, sc_ref, sh_ref, wg_ref, wu_ref, wd_ref, gt_ref, g2_ref,
                o_ref, h_scr, acc_scr, *, prec, n_c):
    c = pl.program_id(1)

    @pl.when(c == 0)
    def _():
        h = _normmod(x_ref[...], g1_ref[...], _mod_row(sc_ref), _mod_row(sh_ref))
        h_scr[...] = h.astype(h_scr.dtype)
        acc_scr[...] = jnp.zeros_like(acc_scr)

    h = h_scr[...]
    act = _silu(_mm(h, wg_ref[...], prec)) * _mm(h, wu_ref[...], prec)
    acc_scr[...] += _mm(act, wd_ref[...], prec)

    @pl.when(c == n_c - 1)
    def _():
        o_ref[...] = x_ref[...] + _mod_row(gt_ref) * (_rms(acc_scr[...]) * g2_ref[...])


def _ffn(x2d, g1, mod, sc_col, sh_col, gt_col, g2, wg, wu, wd, *, tm, fc, prec,
         per_token, tiles_per_batch, vmem_mb=None):
    n_tok, d = x2d.shape
    n_c = wg.shape[1] // fc
    tok = pl.BlockSpec((tm, d), lambda i, c: (i, 0))
    vec = pl.BlockSpec((1, d), lambda i, c: (0, 0))
    h_dtype = BF16 if prec == "bf16" else F32
    w_mode = dict(pipeline_mode=pl.Buffered(1)) if n_c == 1 else {}
    return pl.pallas_call(
        functools.partial(_ffn_kernel, prec=prec, n_c=n_c),
        grid=(n_tok // tm, n_c),
        in_specs=[tok, vec,
                  _mod_spec(per_token, tm, d, sc_col, tiles_per_batch),
                  _mod_spec(per_token, tm, d, sh_col, tiles_per_batch),
                  pl.BlockSpec((d, fc), lambda i, c: (0, c), **w_mode),
                  pl.BlockSpec((d, fc), lambda i, c: (0, c), **w_mode),
                  pl.BlockSpec((fc, d), lambda i, c: (c, 0), **w_mode),
                  _mod_spec(per_token, tm, d, gt_col, tiles_per_batch), vec],
        out_specs=tok,
        out_shape=jax.ShapeDtypeStruct((n_tok, d), F32),
        scratch_shapes=[pltpu.VMEM((tm, d), h_dtype), pltpu.VMEM((tm, d), F32)],
        compiler_params=_params(("parallel", "arbitrary"), vmem_mb),
    )(x2d, g1, mod, mod, wg, wu, wd, mod, g2)


def _router_kernel(x_ref, g_ref, sc_ref, sh_ref, wr_ref, comb_ref, hb_ref, *, n_experts):
    h = _normmod(x_ref[...], g_ref[...], _mod_row(sc_ref), _mod_row(sh_ref))
    hb_ref[...] = h.astype(hb_ref.dtype)
    logits = _mm(h, wr_ref[...], "f32")
    lane = lax.broadcasted_iota(jnp.int32, logits.shape, 1).astype(F32)
    neg = -jnp.inf
    l1 = jnp.where(lane < n_experts, logits, neg)
    m1 = jnp.max(l1, axis=-1, keepdims=True)
    i1 = jnp.min(jnp.where(l1 == m1, lane, float(LANES)), axis=-1, keepdims=True)
    l2 = jnp.where(lane == i1, neg, l1)
    m2 = jnp.max(l2, axis=-1, keepdims=True)
    i2 = jnp.min(jnp.where(l2 == m2, lane, float(LANES)), axis=-1, keepdims=True)
    e2 = jnp.exp(m2 - m1)
    den = 1.0 + e2
    comb_ref[...] = jnp.where(lane == i1, 1.0 / den, 0.0) + jnp.where(lane == i2, e2 / den, 0.0)


def _router(x2d, g, mod, sc_col, sh_col, wr_pad, *, tm, n_experts, per_token, tiles_per_batch):
    n_tok, d = x2d.shape
    return pl.pallas_call(
        functools.partial(_router_kernel, n_experts=n_experts),
        grid=(n_tok // tm,),
        in_specs=[pl.BlockSpec((tm, d), lambda i: (i, 0)),
                  pl.BlockSpec((1, d), lambda i: (0, 0)),
                  _mod_spec(per_token, tm, d, sc_col, tiles_per_batch),
                  _mod_spec(per_token, tm, d, sh_col, tiles_per_batch),
                  pl.BlockSpec((d, LANES), lambda i: (0, 0))],
        out_specs=[pl.BlockSpec((tm, LANES), lambda i: (i, 0)),
                   pl.BlockSpec((tm, d), lambda i: (i, 0))],
        out_shape=[jax.ShapeDtypeStruct((n_tok, LANES), F32),
                   jax.ShapeDtypeStruct((n_tok, d), BF16)],
        compiler_params=_params(("parallel",)),
    )(x2d, g, mod, mod, wr_pad)


MOE_TB = 256
MOE_SEG = 16
MOE_TM = 512


def _moe_block_rows(n_experts):
    return -(-(2 * MOE_TB + n_experts * (MOE_SEG - 1)) // LANES) * LANES


def _routing_slots(comb, dest):
    lane = lax.broadcasted_iota(jnp.int32, comb.shape, 1).astype(F32)
    sel = comb > 0.0
    i_lo = jnp.min(jnp.where(sel, lane, float(LANES)), axis=-1, keepdims=True)
    i_hi = jnp.max(jnp.where(sel, lane, -1.0), axis=-1, keepdims=True)
    pick = lambda i, v: jnp.sum(jnp.where(lane == i, v, 0.0), axis=-1, keepdims=True)
    has_lo = i_lo < float(LANES)
    has_hi = i_hi > i_lo
    d_lo = jnp.where(has_lo, pick(i_lo, dest), -1.0)
    d_hi = jnp.where(has_hi, pick(i_hi, dest), -1.0)
    w_lo = pick(i_lo, comb)
    w_hi = jnp.where(has_hi, pick(i_hi, comb), 0.0)
    out = jnp.where(lane == 0.0, d_lo, 0.0)
    for k, v in ((1.0, d_hi), (2.0, w_lo), (3.0, w_hi)):
        out = jnp.where(lane == k, v, out)
    return out


def _one_hot_rows(row, n_rows):
    tb = row.shape[0]
    lane = lax.broadcasted_iota(jnp.int32, (1, LANES), 1).astype(F32)
    row = jnp.broadcast_to(row, (tb, LANES))
    blocks = [jnp.where(row == lane + float(cb * LANES), 1.0, 0.0) for cb in range(n_rows // LANES)]
    return jnp.concatenate(blocks, axis=1).astype(BF16)


_SEG_BITS = (256, 128, 64, 32, 16)


def _segment_copies(make_copy, blk, pos_ref, off_ref, cnt_ref, n_experts, start):
    for e in range(n_experts):
        seg = blk * n_experts + e
        in_blk, in_sorted, n = off_ref[seg], pos_ref[seg], cnt_ref[seg]
        for size in _SEG_BITS:
            has = (n & size) != 0

            @pl.when(has)
            def _(in_blk=in_blk, in_sorted=in_sorted, size=size):
                cp = make_copy(pl.multiple_of(in_blk, MOE_SEG), pl.multiple_of(in_sorted, MOE_SEG), size)
                cp.start() if start else cp.wait()

            step = jnp.where(has, size, 0)
            in_blk, in_sorted = in_blk + step, in_sorted + step


def _moe_sort_kernel(pos_ref, off_ref, cnt_ref, gpos_ref, glen_ref, hbp_ref, combp_ref, hbs_ref,
                     combs_ref, dest_ref, xs_hbm, buf, zbuf, sems, *, n_rows, n_experts, n_blk, n_blk_p):
    b = pl.program_id(0)
    slot = b % 2

    def writes(blk, slot, start):
        def make_copy(block_row, sorted_row, size):
            return pltpu.make_async_copy(buf.at[slot, pl.ds(block_row, size)],
                                         xs_hbm.at[pl.ds(sorted_row, size)], sems.at[slot])
        _segment_copies(make_copy, blk, pos_ref, off_ref, cnt_ref, n_experts, start)

    def zero_gaps(start):
        for e in range(n_experts):
            row, n = gpos_ref[e], glen_ref[e]
            for size in _SEG_BITS:
                has = (n & size) != 0

                @pl.when(has)
                def _(row=row, size=size):
                    cp = pltpu.make_async_copy(
                        zbuf.at[pl.ds(0, size)],
                        xs_hbm.at[pl.ds(pl.multiple_of(row, MOE_SEG), size)], sems.at[2])
                    cp.start() if start else cp.wait()

                row = row + jnp.where(has, size, 0)
        tail_row, tail_n, big = gpos_ref[n_experts], glen_ref[n_experts], _SEG_BITS[0]

        def piece(i, carry):
            cp = pltpu.make_async_copy(
                zbuf.at[pl.ds(0, big)],
                xs_hbm.at[pl.ds(pl.multiple_of(tail_row + i * big, MOE_SEG), big)], sems.at[2])
            cp.start() if start else cp.wait()
            return carry

        lax.fori_loop(0, tail_n // big, piece, 0)

    @pl.when(b == 0)
    def _():
        zbuf[...] = jnp.zeros_like(zbuf)
        zero_gaps(True)

    @pl.when(b >= 2)
    def _():
        writes(b - 2, slot, False)

    in_prompt = b < n_blk_p
    comb = jnp.where(in_prompt, combp_ref[...], combs_ref[...])
    hb = jnp.where(in_prompt, hbp_ref[...], hbs_ref[...])
    tb = comb.shape[0]
    sel = jnp.where(comb > 0.0, 1.0, 0.0)
    r = lax.broadcasted_iota(jnp.int32, (tb, tb), 0)
    c = lax.broadcasted_iota(jnp.int32, (tb, tb), 1)
    rank = _dot(jnp.where(c < r, 1.0, 0.0).astype(BF16), sel.astype(BF16))
    cnt = jnp.sum(sel, axis=0, keepdims=True)
    cnt = jnp.floor((cnt + (MOE_SEG - 1)) / MOE_SEG) * MOE_SEG
    ru = lax.broadcasted_iota(jnp.int32, (LANES, LANES), 0)
    cu = lax.broadcasted_iota(jnp.int32, (LANES, LANES), 1)
    off = _dot(jnp.broadcast_to(cnt, (8, LANES)).astype(BF16),
               jnp.where(ru < cu, 1.0, 0.0).astype(BF16))[0:1]
    slots = _routing_slots(comb, off + rank)
    dest_ref[...] = slots
    pt = _one_hot_rows(slots[:, 0:1], n_rows) + _one_hot_rows(slots[:, 1:2], n_rows)
    buf[slot] = _dot(pt, hb, ((0,), (0,))).astype(buf.dtype)
    writes(b, slot, True)

    @pl.when(b == n_blk - 1)
    def _():
        writes(b, slot, False)
        if n_blk > 1:
            writes(b - 1, 1 - slot, False)
        zero_gaps(False)


def _moe_sort(seg, gap_pos, gap_len, hb_p, comb_p, hb_s, comb_s, n_sorted_rows, *, n_experts):
    n_p, d = hb_p.shape
    n_blk_p = n_p // MOE_TB
    n_blk = n_blk_p + 1
    n_rows = _moe_block_rows(n_experts)
    last_p = n_blk_p - 1
    return pl.pallas_call(
        functools.partial(_moe_sort_kernel, n_rows=n_rows, n_experts=n_experts, n_blk=n_blk,
                          n_blk_p=n_blk_p),
        grid_spec=pltpu.PrefetchScalarGridSpec(
            num_scalar_prefetch=5, grid=(n_blk,),
            in_specs=[pl.BlockSpec((MOE_TB, d), lambda b, *_: (jnp.minimum(b, last_p), 0)),
                      pl.BlockSpec((MOE_TB, LANES), lambda b, *_: (jnp.minimum(b, last_p), 0)),
                      pl.BlockSpec((MOE_TB, d), lambda b, *_: (0, 0)),
                      pl.BlockSpec((MOE_TB, LANES), lambda b, *_: (0, 0))],
            out_specs=[pl.BlockSpec((MOE_TB, LANES), lambda b, *_: (b, 0)),
                       pl.BlockSpec(memory_space=pl.ANY)],
            scratch_shapes=[pltpu.VMEM((2, n_rows, d), BF16), pltpu.VMEM((_SEG_BITS[0], d), BF16),
                            pltpu.SemaphoreType.DMA((3,))]),
        out_shape=[jax.ShapeDtypeStruct((n_blk * MOE_TB, LANES), F32),
                   jax.ShapeDtypeStruct((n_sorted_rows, d), BF16)],
        compiler_params=_params(("arbitrary",)),
    )(*seg, gap_pos, gap_len, hb_p, comb_p, hb_s, comb_s)


def _experts_kernel(te_ref, nt_ref, x_ref, wg_ref, wu_ref, wd_ref, o_ref, acc_scr, *, n_c):
    del te_ref
    k = pl.program_id(0)
    c = pl.program_id(1)
    active = k < nt_ref[0]

    @pl.when(active)
    def _():
        h = x_ref[...]
        act = _silu(_dot(h, wg_ref[0])) * _dot(h, wu_ref[0])
        y = _dot(act.astype(BF16), wd_ref[0])

        @pl.when(c == 0)
        def _():
            acc_scr[...] = y

        @pl.when(c > 0)
        def _():
            acc_scr[...] += y

        @pl.when(c == n_c - 1)
        def _():
            o_ref[...] = acc_scr[...].astype(o_ref.dtype)

    @pl.when(jnp.logical_not(active) & (c == n_c - 1))
    def _():
        o_ref[...] = jnp.zeros_like(o_ref)


def _experts(tile_expert, n_tiles, xs, wg, wu, wd, *, fc):
    n_rows, d = xs.shape
    n_c = wg.shape[2] // fc

    def row_map(k, c, te, nt):
        return (jnp.minimum(k, nt[0] - 1), 0)

    def w_map(k, c, te, nt):
        return (te[k], 0, jnp.where(k < nt[0], c, n_c - 1))

    def wd_map(k, c, te, nt):
        return (te[k], jnp.where(k < nt[0], c, n_c - 1), 0)

    return pl.pallas_call(
        functools.partial(_experts_kernel, n_c=n_c),
        grid_spec=pltpu.PrefetchScalarGridSpec(
            num_scalar_prefetch=2, grid=(n_rows // MOE_TM, n_c),
            in_specs=[pl.BlockSpec((MOE_TM, d), row_map),
                      pl.BlockSpec((1, d, fc), w_map), pl.BlockSpec((1, d, fc), w_map),
                      pl.BlockSpec((1, fc, d), wd_map)],
            out_specs=pl.BlockSpec((MOE_TM, d), lambda k, c, te, nt: (k, 0)),
            scratch_shapes=[pltpu.VMEM((MOE_TM, d), F32)]),
        out_shape=jax.ShapeDtypeStruct((n_rows, d), BF16),
        compiler_params=_params(("arbitrary", "arbitrary"), 48),
    )(tile_expert, n_tiles, xs, wg, wu, wd)


def _moe_combine_kernel(pos_ref, off_ref, cnt_ref, ys_hbm, dest_ref, x_ref, gt_ref, g_ref,
                        o_ref, buf, sems, *, n_experts, n_blk, blk0):
    b = pl.program_id(0)
    slot = b % 2

    def reads(blk, slot, start):
        def make_copy(block_row, sorted_row, size):
            return pltpu.make_async_copy(ys_hbm.at[pl.ds(sorted_row, size)],
                                         buf.at[slot, pl.ds(block_row, size)], sems.at[slot])
        _segment_copies(make_copy, blk0 + blk, pos_ref, off_ref, cnt_ref, n_experts, start)

    @pl.when(b == 0)
    def _():
        buf[...] = jnp.zeros_like(buf)
        reads(b, slot, True)

    if n_blk > 1:
        @pl.when(b + 1 < n_blk)
        def _():
            reads(b + 1, 1 - slot, True)

    reads(b, slot, False)
    yb = buf[slot]
    slots = dest_ref[...]
    n_rows = yb.shape[0]
    y = (slots[:, 2:3] * _dot(_one_hot_rows(slots[:, 0:1], n_rows), yb)
         + slots[:, 3:4] * _dot(_one_hot_rows(slots[:, 1:2], n_rows), yb))
    o_ref[...] = x_ref[...] + _mod_row(gt_ref) * (_rms(y) * g_ref[...])


def _moe_combine(sorted_pos, block_off, seg_cnt, ys, dest, x2d, mod, gt_col, g, *, blk0,
                 n_experts, per_token, tiles_per_batch):
    n_tok, d = x2d.shape
    n_blk = n_tok // MOE_TB
    n_rows = _moe_block_rows(n_experts)
    return pl.pallas_call(
        functools.partial(_moe_combine_kernel, n_experts=n_experts, n_blk=n_blk, blk0=blk0),
        grid_spec=pltpu.PrefetchScalarGridSpec(
            num_scalar_prefetch=3, grid=(n_blk,),
            in_specs=[pl.BlockSpec(memory_space=pl.ANY),
                      pl.BlockSpec((MOE_TB, LANES), lambda b, *_: (blk0 + b, 0)),
                      pl.BlockSpec((MOE_TB, d), lambda b, *_: (b, 0)),
                      _mod_spec(per_token, MOE_TB, d, gt_col, tiles_per_batch),
                      pl.BlockSpec((1, d), lambda b, *_: (0, 0))],
            out_specs=pl.BlockSpec((MOE_TB, d), lambda b, *_: (b, 0)),
            scratch_shapes=[pltpu.VMEM((2, n_rows, d), BF16), pltpu.SemaphoreType.DMA((2,))]),
        out_shape=jax.ShapeDtypeStruct((n_tok, d), F32),
        compiler_params=_params(("arbitrary",)),
    )(sorted_pos, block_off, seg_cnt, ys, dest, x2d, mod, g)


def _moe_schedule(combs, n_experts, n_tiles_max):
    def block_counts(comb):
        n_blk = comb.shape[0] // MOE_TB
        sel = (comb[:, :n_experts] > 0.0).astype(jnp.int32)
        return jnp.sum(sel.reshape(n_blk, MOE_TB, n_experts), axis=1)

    cnt = jnp.concatenate([block_counts(c) for c in combs], axis=0)
    cnt = -(-cnt // MOE_SEG) * MOE_SEG
    off = jnp.cumsum(cnt, axis=1) - cnt
    tot = jnp.sum(cnt, axis=0)
    grp = -(-tot // MOE_TM) * MOE_TM
    g_end = jnp.cumsum(grp)
    sorted_pos = (g_end - grp)[None, :] + (jnp.cumsum(cnt, axis=0) - cnt)
    n_tiles = g_end[-1] // MOE_TM
    first_row = jnp.arange(n_tiles_max, dtype=jnp.int32) * MOE_TM
    tile_expert = jnp.sum(first_row[:, None] >= g_end[None, :], axis=1)
    last = jnp.sum((n_tiles - 1) * MOE_TM >= g_end)
    tile_expert = jnp.minimum(tile_expert, last).astype(jnp.int32)
    flat = lambda a: a.reshape(-1).astype(jnp.int32)
    seg = (flat(sorted_pos), flat(off), flat(cnt))
    gap_pos = flat(jnp.concatenate([g_end - grp + tot, g_end[-1:]]))
    gap_len = flat(jnp.concatenate([grp - tot, n_tiles_max * MOE_TM - g_end[-1:]]))
    return seg, gap_pos, gap_len, tile_expert, n_tiles.reshape(1).astype(jnp.int32)


def _gla_conv_kernel(p_ref, wgu_ref, bgu_ref, gh_ref, wc_ref, mixed_ref, st_ref, cst_ref,
                     s_scr, uext_scr, *, tt, n_t, heads, dk, dv, cw):
    ti = pl.program_id(1)
    hk = heads * dk
    hv = heads * dv
    o_q, o_k, o_v, o_og = 0, hk, 2 * hk, 2 * hk + hv
    o_ch = o_og + hv
    o_cb, o_cc, o_lr = o_ch + cw, o_ch + 2 * cw, o_ch + 3 * cw
    L = GLA_CHUNK

    @pl.when(ti == 0)
    def _():
        s_scr[...] = jnp.zeros_like(s_scr)
        uext_scr[0:8, :] = jnp.zeros((8, cw), F32)

    logit = _dot(p_ref[:, o_lr:o_lr + LANES].astype(BF16), wgu_ref[...]) + bgu_ref[...]
    logg = _log_sigmoid(logit) / GATE_NORMALIZER

    r = lax.broadcasted_iota(jnp.int32, (tt, tt), 0)
    c = lax.broadcasted_iota(jnp.int32, (tt, tt), 1)
    tri = jnp.where((r // L == c // L) & (c <= r), 1.0, 0.0).astype(BF16)
    g_hi = logg.astype(BF16)
    rem = logg - g_hi.astype(F32)
    g_mid = rem.astype(BF16)
    g_lo = (rem - g_mid.astype(F32)).astype(BF16)
    bc = _dot(tri, g_hi) + (_dot(tri, g_mid) + _dot(tri, g_lo))

    rl = lax.broadcasted_iota(jnp.int32, (L, L), 0)
    cl = lax.broadcasted_iota(jnp.int32, (L, L), 1)
    tril = cl <= rl
    lane = lax.broadcasted_iota(jnp.int32, (L, LANES), 1)
    heads_per_blk = LANES // dk
    nt_dims = ((1,), (1,))

    for ck in range(tt // L):
        rows = slice(L * ck, L * ck + L)
        b = bc[rows]
        bl = b[L - 1:L]
        q_in = (p_ref[rows, o_q:o_q + hk] * (dk ** -0.5)) * jnp.exp(b)
        k = p_ref[rows, o_k:o_k + hk]
        k_out = (k * jnp.exp(-b)).astype(BF16)
        k_dec = (k * jnp.exp(bl - b)).astype(BF16)
        dec = jnp.exp(bl)
        q_in = q_in.astype(BF16)
        for h in range(heads):
            blk = slice(LANES * (h // heads_per_blk), LANES * (h // heads_per_blk) + LANES)
            in_head = (lane // dk) == (h % heads_per_blk)
            qm = jnp.where(in_head, q_in[:, blk], jnp.zeros_like(q_in[:, blk]))
            att = _dot(qm, k_out[:, blk], nt_dims)
            att = jnp.where(tril, att, 0.0).astype(BF16)
            vh = p_ref[rows, o_v + dv * h:o_v + dv * h + dv].astype(BF16)
            s_t = s_scr[h]
            o = _dot(att, vh) + _dot(qm, s_t.astype(BF16), nt_dims)
            ds_t = _dot(vh, k_dec[:, blk], ((0,), (0,)))
            s_scr[h] = s_t * dec[:, blk] + ds_t
            og = p_ref[rows, o_og + dv * h:o_og + dv * h + dv]
            res = (_rms(o) * gh_ref[...]) * _silu(og)
            mixed_ref[rows, dv * h:dv * h + dv] = res.astype(mixed_ref.dtype)

    u = p_ref[:, o_cc:o_cc + cw] * p_ref[:, o_ch:o_ch + cw]
    uext_scr[8:8 + tt, :] = u
    y = (wc_ref[0:1, :] * uext_scr[6:6 + tt, :] + wc_ref[1:2, :] * uext_scr[7:7 + tt, :]
         + wc_ref[2:3, :] * u)
    mixed_ref[:, hv:hv + cw] = (p_ref[:, o_cb:o_cb + cw] * y).astype(mixed_ref.dtype)
    tail = uext_scr[tt + 6:tt + 8, :]
    uext_scr[6:8, :] = tail

    @pl.when(ti == n_t - 1)
    def _():
        st_ref[0] = s_scr[...]
        cst_ref[0] = tail


def _gla_conv_prompt(proj, wgu_pad, bgu, g_head, w_conv, *, bsz, t, tt, heads, dk, dv, cw):
    n_t = t // tt
    n_in = proj.shape[1]
    width = heads * dv + cw
    return pl.pallas_call(
        functools.partial(_gla_conv_kernel, tt=tt, n_t=n_t, heads=heads, dk=dk, dv=dv, cw=cw),
        grid=(bsz, n_t),
        in_specs=[pl.BlockSpec((tt, n_in), lambda b, i: (b * n_t + i, 0)),
                  pl.BlockSpec(wgu_pad.shape, lambda b, i: (0, 0)),
                  pl.BlockSpec(bgu.shape, lambda b, i: (0, 0)),
                  pl.BlockSpec(g_head.shape, lambda b, i: (0, 0)),
                  pl.BlockSpec(w_conv.shape, lambda b, i: (0, 0))],
        out_specs=[pl.BlockSpec((tt, width), lambda b, i: (b * n_t + i, 0)),
                   pl.BlockSpec((1, heads, dv, LANES), lambda b, i: (b, 0, 0, 0)),
                   pl.BlockSpec((1, 2, cw), lambda b, i: (b, 0, 0))],
        out_shape=[jax.ShapeDtypeStruct((bsz * t, width), BF16),
                   jax.ShapeDtypeStruct((bsz, heads, dv, LANES), F32),
                   jax.ShapeDtypeStruct((bsz, 2, cw), F32)],
        scratch_shapes=[pltpu.VMEM((heads, dv, LANES), F32), pltpu.VMEM((tt + 8, cw), F32)],
        compiler_params=_params(("parallel", "arbitrary"), 48),
    )(proj, wgu_pad, bgu, g_head, w_conv)


def _gla_conv_step_kernel(q_ref, k_ref, v_ref, og_ref, ch_ref, cb_ref, cc_ref, lr_ref,
                          wgu_ref, bgu_ref, gh_ref, wc_ref, s_ref, cbuf_ref,
                          o_ref, y_ref, sn_ref, cn_ref, a_scr, *, dk):
    logit = _mm(wgu_ref[...], lr_ref[...], "f32") + bgu_ref[...]
    a_scr[...] = jnp.exp(_log_sigmoid(logit) / GATE_NORMALIZER)
    v_t = v_ref[...]

    def body(d, acc):
        a = a_scr[pl.ds(d, 1), :]
        kd = k_ref[pl.ds(d, 1), :]
        qd = q_ref[pl.ds(d, 1), :]
        s_new = a * s_ref[0, d] + kd * v_t
        sn_ref[0, d] = s_new
        return acc + (qd * (dk ** -0.5)) * s_new

    o = lax.fori_loop(0, dk, body, jnp.zeros(v_t.shape, F32))
    o = o * lax.rsqrt(jnp.mean(o * o, axis=0, keepdims=True) + EPS) * gh_ref[...]
    o_ref[...] = o * _silu(og_ref[...])
    u = cc_ref[...] * ch_ref[...]
    y = wc_ref[0] * cbuf_ref[0] + wc_ref[1] * cbuf_ref[1] + wc_ref[2] * u
    y_ref[...] = cb_ref[...] * y
    cn_ref[0] = cbuf_ref[1]
    cn_ref[1] = u


def _gla_conv_step(proj_t, wgu_t_pad, bgu_col, gh_col, wc_col, s_t, cbuf_t, *, heads, dk, dv, cw):
    bsz = proj_t.shape[1]
    hk, hv = heads * dk, heads * dv
    cs = cw // heads
    assert dv == LANES and cs == LANES and dk * 2 == LANES
    o_k, o_v, o_og = hk, 2 * hk, 2 * hk + hv
    o_ch = o_og + hv
    o_cb, o_cc, o_lr = o_ch + cw, o_ch + 2 * cw, o_ch + 3 * cw
    row = lambda off, size: (lambda h: (off // size + h, 0))
    blk = lambda size, off: pl.BlockSpec((size, bsz), row(off, size))
    return pl.pallas_call(
        functools.partial(_gla_conv_step_kernel, dk=dk),
        grid=(heads,),
        in_specs=[blk(dk, 0), blk(dk, o_k), blk(dv, o_v), blk(dv, o_og),
                  blk(cs, o_ch), blk(cs, o_cb), blk(cs, o_cc),
                  pl.BlockSpec((LANES, bsz), lambda h: (o_lr // LANES, 0)),
                  pl.BlockSpec((dk, LANES), lambda h: (h, 0)),
                  pl.BlockSpec((dk, 1), lambda h: (h, 0)),
                  pl.BlockSpec((dv, 1), lambda h: (0, 0)),
                  pl.BlockSpec((3, cs, 1), lambda h: (0, h, 0)),
                  pl.BlockSpec((1, dk, dv, bsz), lambda h: (h, 0, 0, 0)),
                  pl.BlockSpec((2, cs, bsz), lambda h: (0, h, 0))],
        out_specs=[pl.BlockSpec((dv, bsz), lambda h: (h, 0)),
                   pl.BlockSpec((cs, bsz), lambda h: (h, 0)),
                   pl.BlockSpec((1, dk, dv, bsz), lambda h: (h, 0, 0, 0)),
                   pl.BlockSpec((2, cs, bsz), lambda h: (0, h, 0))],
        out_shape=[jax.ShapeDtypeStruct((hv, bsz), F32),
                   jax.ShapeDtypeStruct((cw, bsz), F32),
                   jax.ShapeDtypeStruct(s_t.shape, F32),
                   jax.ShapeDtypeStruct(cbuf_t.shape, F32)],
        scratch_shapes=[pltpu.VMEM((dk, bsz), F32)],
        compiler_params=_params(("parallel",)),
    )(proj_t, proj_t, proj_t, proj_t, proj_t, proj_t, proj_t, proj_t,
      wgu_t_pad, bgu_col, gh_col, wc_col, s_t, cbuf_t)


S5_ROW_PAD = 8


def _s5_scan_kernel(u_ref, bb_ref, cm_ref, abr_ref, abi_ref, d_ref, wa_ref, wb_ref, wc_ref,
                    y_ref, sre_ref, sim_ref, wa_out, wb_out, wc_out,
                    scr, xr_scr, xi_scr, *, tt, n_t, nb, half):
    ti = pl.program_id(1)
    stride = tt + S5_ROW_PAD
    n_blk = half // LANES
    wa_out[...] = wa_ref[...].astype(wa_out.dtype)
    wb_out[...] = wb_ref[...].astype(wb_out.dtype)
    wc_out[...] = wc_ref[...].astype(wc_out.dtype)

    @pl.when(ti == 0)
    def _():
        xr_scr[...] = jnp.zeros_like(xr_scr)
        xi_scr[...] = jnp.zeros_like(xi_scr)

    for b in range(nb):
        bu = _dot(u_ref[b].astype(BF16), bb_ref[0])
        for cb in range(2 * n_blk):
            scr[cb, b * stride:b * stride + tt, :] = bu[:, cb * LANES:(cb + 1) * LANES]
    a_re = [jnp.broadcast_to(abr_ref[0][:, cb * LANES:(cb + 1) * LANES], (nb, LANES))
            for cb in range(n_blk)]
    a_im = [jnp.broadcast_to(abi_ref[0][:, cb * LANES:(cb + 1) * LANES], (nb, LANES))
            for cb in range(n_blk)]

    def body(t, carry):
        rows = pl.ds(t, nb, stride=stride)
        out = []
        for cb in range(n_blk):
            xr, xi = carry[2 * cb], carry[2 * cb + 1]
            nr = (a_re[cb] * xr - a_im[cb] * xi) + scr[cb, rows, :]
            ni = (a_re[cb] * xi + a_im[cb] * xr) + scr[n_blk + cb, rows, :]
            scr[cb, rows, :] = nr
            scr[n_blk + cb, rows, :] = ni
            out += [nr, ni]
        return tuple(out)

    init = []
    for cb in range(n_blk):
        init += [xr_scr[:, cb * LANES:(cb + 1) * LANES], xi_scr[:, cb * LANES:(cb + 1) * LANES]]
    fin = lax.fori_loop(0, tt, body, tuple(init), unroll=4)
    xr = jnp.concatenate([fin[2 * cb] for cb in range(n_blk)], axis=1)
    xi = jnp.concatenate([fin[2 * cb + 1] for cb in range(n_blk)], axis=1)
    xr_scr[...] = xr
    xi_scr[...] = xi
    for b in range(nb):
        xs = jnp.concatenate([scr[cb, b * stride:b * stride + tt, :] for cb in range(2 * n_blk)],
                             axis=1)
        y = _dot(xs.astype(BF16), cm_ref[0]) + d_ref[0] * u_ref[b]
        y_ref[b] = _gelu_tanh(y).astype(y_ref.dtype)

    @pl.when(ti == n_t - 1)
    def _():
        sre_ref[...] = xr
        sim_ref[...] = xi


def _s5_prompt(u3d, bb_blk, c_blk, ab_re, ab_im, d_blk, cast_ws, *, tt):
    nb, t, d = u3d.shape
    n_j = d // LANES
    n_t = t // tt
    half = bb_blk.shape[2] // 2
    n_steps = n_j * n_t

    def slab_spec(w):
        n_e, rows, cols = w.shape
        per_e = n_steps // n_e
        slab = rows // per_e
        assert n_e * per_e == n_steps and slab * per_e == rows and slab % 16 == 0
        return pl.BlockSpec((1, slab, cols),
                            lambda j, i: ((j * n_t + i) // per_e, (j * n_t + i) % per_e, 0))

    w_specs = [slab_spec(w) for w in cast_ws]
    return pl.pallas_call(
        functools.partial(_s5_scan_kernel, tt=tt, n_t=n_t, nb=nb, half=half),
        grid=(n_j, n_t),
        in_specs=[pl.BlockSpec((nb, tt, LANES), lambda j, i: (0, i, j)),
                  pl.BlockSpec((1, LANES, 2 * half), lambda j, i: (j, 0, 0)),
                  pl.BlockSpec((1, 2 * half, LANES), lambda j, i: (j, 0, 0)),
                  pl.BlockSpec((1, 1, half), lambda j, i: (j, 0, 0)),
                  pl.BlockSpec((1, 1, half), lambda j, i: (j, 0, 0)),
                  pl.BlockSpec((1, 1, LANES), lambda j, i: (j, 0, 0))] + w_specs,
        out_specs=[pl.BlockSpec((nb, tt, LANES), lambda j, i: (0, i, j)),
                   pl.BlockSpec((nb, half), lambda j, i: (0, j)),
                   pl.BlockSpec((nb, half), lambda j, i: (0, j))] + w_specs,
        out_shape=[jax.ShapeDtypeStruct((nb, t, d), BF16),
                   jax.ShapeDtypeStruct((nb, n_j * half), F32),
                   jax.ShapeDtypeStruct((nb, n_j * half), F32)]
        + [jax.ShapeDtypeStruct(w.shape, BF16) for w in cast_ws],
        scratch_shapes=[pltpu.VMEM((2 * half // LANES, nb * (tt + S5_ROW_PAD), LANES), F32),
                        pltpu.VMEM((nb, half), F32), pltpu.VMEM((nb, half), F32)],
        compiler_params=_params(("arbitrary", "arbitrary"), 48),
    )(u3d, bb_blk, c_blk, ab_re, ab_im, d_blk, *cast_ws)


def _s5_step_kernel(u_ref, bb_ref, cm_ref, abr_ref, abi_ref, d_ref, sre_ref, sim_ref,
                    y_ref, nre_ref, nim_ref, *, half):
    u = u_ref[...]
    bu = _mm(u, bb_ref[0], "f32")
    ar, ai = abr_ref[0], abi_ref[0]
    sr, si = sre_ref[...], sim_ref[...]
    nr = (ar * sr - ai * si) + bu[:, 0:half]
    ni = (ar * si + ai * sr) + bu[:, half:2 * half]
    nre_ref[...] = nr
    nim_ref[...] = ni
    y = _mm(jnp.concatenate([nr, ni], axis=1), cm_ref[0], "f32") + d_ref[0] * u
    y_ref[...] = _gelu_tanh(y)


def _s5_step(u2d, bb_blk, c_blk, ab_re, ab_im, d_blk, s_re, s_im):
    bsz, d = u2d.shape
    n_j = d // LANES
    half = bb_blk.shape[2] // 2
    st = pl.BlockSpec((bsz, half), lambda j: (0, j))
    return pl.pallas_call(
        functools.partial(_s5_step_kernel, half=half),
        grid=(n_j,),
        in_specs=[pl.BlockSpec((bsz, LANES), lambda j: (0, j)),
                  pl.BlockSpec((1, LANES, 2 * half), lambda j: (j, 0, 0)),
                  pl.BlockSpec((1, 2 * half, LANES), lambda j: (j, 0, 0)),
                  pl.BlockSpec((1, 1, half), lambda j: (j, 0, 0)),
                  pl.BlockSpec((1, 1, half), lambda j: (j, 0, 0)),
                  pl.BlockSpec((1, 1, LANES), lambda j: (j, 0, 0)),
                  st, st],
        out_specs=[pl.BlockSpec((bsz, LANES), lambda j: (0, j)), st, st],
        out_shape=[jax.ShapeDtypeStruct((bsz, d), F32),
                   jax.ShapeDtypeStruct(s_re.shape, F32),
                   jax.ShapeDtypeStruct(s_im.shape, F32)],
        compiler_params=_params(("parallel",)),
    )(u2d, bb_blk, c_blk, ab_re, ab_im, d_blk, s_re, s_im)


def _s5_zoh(a_re, a_im, log_dt, b_re, b_im):
    dt = jnp.exp(log_dt)[:, None]
    mag = jnp.exp(dt * a_re)
    ab_re, ab_im = mag * jnp.cos(dt * a_im), mag * jnp.sin(dt * a_im)
    den = a_re * a_re + a_im * a_im
    nr, ni = ab_re - 1.0, ab_im
    f_re = (nr * a_re + ni * a_im) / den
    f_im = (ni * a_re - nr * a_im) / den
    bb_re = f_re[..., None] * b_re - f_im[..., None] * b_im
    bb_im = f_re[..., None] * b_im + f_im[..., None] * b_re
    return ab_re, ab_im, bb_re, bb_im


def _s5_step_blocks(ab_re, ab_im, bb_re, bb_im, c_re, c_im, d_skip):
    g, p = ab_re.shape
    cg = S5_CHUNK_GROUPS
    n_j = g // cg
    eye = jnp.eye(cg, dtype=F32)

    def in_blk(m):
        m = m.reshape(n_j, cg, p, S5_GROUP)
        return jnp.einsum("jgpi,gh->jgihp", m, eye).reshape(n_j, cg * S5_GROUP, cg * p)

    def out_blk(m):
        m = m.reshape(n_j, cg, S5_GROUP, p)
        return jnp.einsum("jgip,gh->jgphi", m, eye).reshape(n_j, cg * p, cg * S5_GROUP)

    bb_blk = jnp.concatenate([in_blk(bb_re), in_blk(bb_im)], axis=2)
    c_blk = jnp.concatenate([out_blk(c_re), out_blk(-c_im)], axis=1)
    return (bb_blk, c_blk, ab_re.reshape(n_j, 1, cg * p), ab_im.reshape(n_j, 1, cg * p),
            d_skip.reshape(n_j, 1, cg * S5_GROUP))


def kernel(x_prompt, x_sample, c_prompt, c_sample, state_gla, state_conv, state_s5_re, state_s5_im, w_ada, b_ada, g_norm, w_in0, w_gate_up, b_gate_up, g_head_norm, w_conv, w_out0, w_ffn_gate, w_ffn_up, w_ffn_down, w_in1, s5_a_re, s5_a_im, s5_log_dt, s5_b_re, s5_b_im, s5_c_re, s5_c_im, s5_d, w_glu, w_router, w_exp_gate, w_exp_up, w_exp_down):
    bsz, t, d = x_prompt.shape
    bs = x_sample.shape[0]
    heads, dk, dv = state_gla.shape[2:]
    cw = state_conv.shape[3]
    lowrank = w_gate_up.shape[1]
    hk, hv = heads * dk, heads * dv
    n_experts = w_router.shape[2]
    g_s5, p_s5 = s5_a_re.shape[1:]

    mod = _ada(jnp.concatenate([c_prompt, c_sample], axis=0), w_ada, b_ada)
    mod_p = [mod[l, :bsz].reshape(bsz, 1, N_MOD * d) for l in range(2)]
    mod_s = [mod[l, bsz:] for l in range(2)]
    SH_M, SC_M, GT_M, SH_F, SC_F, GT_F = range(N_MOD)
    gn = lambda l, k: g_norm[l, k].reshape(1, d)

    w0 = w_in0[0]
    o_lr = 2 * hk + 2 * hv
    w0r = jnp.concatenate([w0[:, :o_lr], w0[:, o_lr + lowrank:], w0[:, o_lr:o_lr + lowrank],
                           jnp.zeros((d, LANES - lowrank), F32)], axis=1)
    wgu_pad = jnp.concatenate([w_gate_up[0], jnp.zeros((LANES - lowrank, hk), F32)], axis=0)
    zoh = _s5_zoh(s5_a_re[0], s5_a_im[0], s5_log_dt[0], s5_b_re[0], s5_b_im[0])
    bb_blk, c_blk, ab_re, ab_im, d_blk = _s5_step_blocks(*zoh, s5_c_re[0], s5_c_im[0], s5_d[0])
    wr_pad = jnp.concatenate([w_router[0], jnp.zeros((d, LANES - n_experts), F32)], axis=1)
    bf = lambda a: a.astype(BF16)

    tm = 512
    tpb = t // tm
    pk = dict(per_token=False, tiles_per_batch=tpb)
    xp = x_prompt.reshape(bsz * t, d)
    proj = _inproj(xp, gn(0, 0), mod_p[0], SC_M, SH_M, bf(w0r), tm=tm, bn=w0r.shape[1],
                   prec="bf16", vmem_mb=48, **pk)
    mixed, st_t, cst = _gla_conv_prompt(
        proj, bf(wgu_pad), b_gate_up[0].reshape(1, hk), g_head_norm[0].reshape(1, dv), w_conv[0],
        bsz=bsz, t=t, tt=256, heads=heads, dk=dk, dv=dv, cw=cw)
    st_t = st_t.reshape(bsz, heads // 2, 2, dv, 2, dk)
    gla_p = jnp.stack([st_t[:, :, 0, :, 0, :], st_t[:, :, 1, :, 1, :]], axis=2)
    gla_p = jnp.swapaxes(gla_p.reshape(bsz, heads, dv, dk), 2, 3)
    x1 = _outproj(mixed, bf(w_out0[0]), xp, mod_p[0], GT_M, gn(0, 1), tm=tm, prec="bf16",
                  glu=False, **pk)
    x2 = _ffn(x1, gn(0, 2), mod_p[0], SC_F, SH_F, GT_F, gn(0, 3), bf(w_ffn_gate[0]),
              bf(w_ffn_up[0]), bf(w_ffn_down[0]), tm=tm, fc=2816, prec="bf16", vmem_mb=56, **pk)
    u = _inproj(x2, gn(1, 0), mod_p[1], SC_M, SH_M, bf(w_in1[0]), tm=tm, bn=d, prec="bf16", **pk)
    yg, re_p, im_p, w_eg, w_eu, w_ed = _s5_prompt(
        u.reshape(bsz, t, d), bf(bb_blk), bf(c_blk), ab_re, ab_im, d_blk,
        (w_exp_gate[0], w_exp_up[0], w_exp_down[0]), tt=256)
    x3 = _outproj(yg.reshape(bsz * t, d), bf(w_glu[0]), x2, mod_p[1], GT_M, gn(1, 1), tm=tm,
                  prec="bf16", glu=True, vmem_mb=48, **pk)
    comb, hb = _router(x3, gn(1, 2), mod_p[1], SC_F, SH_F, wr_pad, tm=tm, n_experts=n_experts,
                       **pk)

    sk = dict(per_token=True, tiles_per_batch=1)
    xs = x_sample.reshape(bs, d)
    proj_s = _inproj(xs, gn(0, 0), mod_s[0], SC_M, SH_M, w0r, tm=bs, bn=640, prec="f32", **sk)
    s_t = jnp.transpose(state_gla[:, 0], (1, 2, 3, 0))
    cbuf_t = jnp.transpose(state_conv[:, 0], (1, 2, 0))
    o_t, y_t, sn_t, cn_t = _gla_conv_step(
        proj_s.T, wgu_pad.T, b_gate_up[0].reshape(hk, 1), g_head_norm[0].reshape(dv, 1),
        w_conv[0].reshape(3, cw, 1), s_t, cbuf_t, heads=heads, dk=dk, dv=dv, cw=cw)
    mixed_s = jnp.concatenate([o_t, y_t], axis=0).T
    gla_s = jnp.transpose(sn_t, (3, 0, 1, 2))
    conv_s = jnp.transpose(cn_t, (2, 0, 1))
    x1s = _outproj(mixed_s, w_out0[0], xs, mod_s[0], GT_M, gn(0, 1), tm=bs, prec="f32", glu=False,
                   **sk)
    x2s = _ffn(x1s, gn(0, 2), mod_s[0], SC_F, SH_F, GT_F, gn(0, 3), w_ffn_gate[0], w_ffn_up[0],
               w_ffn_down[0], tm=bs, fc=256, prec="f32", **sk)
    us = _inproj(x2s, gn(1, 0), mod_s[1], SC_M, SH_M, w_in1[0], tm=bs, bn=256, prec="f32", **sk)
    ygs, re_s, im_s = _s5_step(us, bb_blk, c_blk, ab_re, ab_im, d_blk,
                               state_s5_re[:, 0].reshape(bs, g_s5 * p_s5),
                               state_s5_im[:, 0].reshape(bs, g_s5 * p_s5))
    x3s = _outproj(ygs, w_glu[0], x2s, mod_s[1], GT_M, gn(1, 1), tm=bs, prec="f32", glu=True,
                   vmem_mb=48, **sk)
    comb_s, hb_s = _router(x3s, gn(1, 2), mod_s[1], SC_F, SH_F, wr_pad, tm=bs, n_experts=n_experts,
                           **sk)

    assert bs <= MOE_TB
    n_blk = bsz * t // MOE_TB + 1
    n_tiles_max = -(-(n_blk * _moe_block_rows(n_experts)) // MOE_TM) + n_experts
    pad_rows = lambda a: jnp.concatenate([a, jnp.zeros((MOE_TB - bs, a.shape[1]), a.dtype)], axis=0)
    hb_s, comb_s = pad_rows(hb_s), pad_rows(comb_s)
    seg, gap_pos, gap_len, tile_expert, n_tiles = _moe_schedule((comb, comb_s), n_experts,
                                                                n_tiles_max)
    dest, xs_sorted = _moe_sort(seg, gap_pos, gap_len, hb, comb, hb_s, comb_s,
                                n_tiles_max * MOE_TM, n_experts=n_experts)
    ys_sorted = _experts(tile_expert, n_tiles, xs_sorted, w_eg, w_eu, w_ed, fc=1792)
    x4 = _moe_combine(*seg, ys_sorted, dest, x3, mod_p[1], GT_F, gn(1, 3), blk0=0,
                      n_experts=n_experts, per_token=False, tiles_per_batch=t // MOE_TB)
    x4s = _moe_combine(*seg, ys_sorted, dest, pad_rows(x3s), pad_rows(mod_s[1]), GT_F,
                       gn(1, 3), blk0=n_blk - 1, n_experts=n_experts, per_token=True,
                       tiles_per_batch=1)[:bs]

    return (x4.reshape(bsz, t, d), x4s.reshape(bs, 1, d),
            gla_p[:, None], cst[:, None],
            re_p.reshape(bsz, 1, g_s5, p_s5), im_p.reshape(bsz, 1, g_s5, p_s5),
            gla_s[:, None], conv_s[:, None],
            re_s.reshape(bs, 1, g_s5, p_s5), im_s.reshape(bs, 1, g_s5, p_s5))
```

```python
import functools
import math

import jax
import jax.numpy as jnp
from jax import lax
from jax.experimental import pallas as pl
from jax.experimental.pallas import tpu as pltpu

F32 = jnp.float32
BF16 = jnp.bfloat16
EPS = 1e-6
LANES = 128
GLA_CHUNK = 64
GATE_NORMALIZER = 16.0
N_MOD = 6
S5_GROUP = 16
S5_CHUNK_GROUPS = 8


def _dot(a, b, dims=None):
    if dims is None:
        return jnp.dot(a, b, preferred_element_type=F32)
    return lax.dot_general(a, b, (dims, ((), ())), preferred_element_type=F32)


def _split(a, parts=2):
    rem = a.astype(F32)
    out = []
    for _ in range(parts - 1):
        piece = rem.astype(BF16)
        out.append(piece)
        rem = rem - piece.astype(F32)
    out.append(rem.astype(BF16))
    return out


def _mm(a, w, prec):
    if prec == "bf16":
        return _dot(a.astype(BF16), w.astype(BF16))
    a1, a2, a3 = _split(a, 3)
    w1, w2, w3 = _split(w, 3)
    small = (_dot(a1, w3) + _dot(a3, w1)) + _dot(a2, w2)
    return _dot(a1, w1) + ((_dot(a1, w2) + _dot(a2, w1)) + small)


def _silu(x):
    return x * jax.nn.sigmoid(x)


def _gelu_tanh(x):
    return 0.5 * x * (1.0 + jnp.tanh(math.sqrt(2.0 / math.pi) * (x + 0.044715 * (x * x * x))))


def _log_sigmoid(x):
    return -(jnp.maximum(-x, 0.0) + jnp.log1p(jnp.exp(-jnp.abs(x))))


def _rms(x):
    return x * lax.rsqrt(jnp.mean(x * x, axis=-1, keepdims=True) + EPS)


def _mod_row(ref):
    return ref[0] if len(ref.shape) == 3 else ref[...]


def _normmod(x, g, sc, sh):
    return (_rms(x) * g) * (1.0 + sc) + sh


def _mod_spec(per_token, tm, d, col, tiles_per_batch):
    if per_token:
        return pl.BlockSpec((tm, d), lambda i, *_: (i, col))
    return pl.BlockSpec((1, 1, d), lambda i, *_: (i // tiles_per_batch, 0, col))


def _params(sem, vmem_mb=None):
    kw = dict(dimension_semantics=sem)
    if vmem_mb is not None:
        kw["vmem_limit_bytes"] = vmem_mb << 20
    return pltpu.CompilerParams(**kw)


def _ada_kernel(c_ref, w_ref, b_ref, o_ref):
    o_ref[0] = _mm(_silu(c_ref[...]), w_ref[0], "f32") + b_ref[0]


def _ada(c_all, w_ada, b_ada):
    depth, d, n6 = w_ada.shape
    rows = c_all.shape[0]
    return pl.pallas_call(
        _ada_kernel,
        grid=(depth, n6 // d),
        in_specs=[pl.BlockSpec((rows, d), lambda l, j: (0, 0)),
                  pl.BlockSpec((1, d, d), lambda l, j: (l, 0, j)),
                  pl.BlockSpec((1, 1, d), lambda l, j: (l, 0, j))],
        out_specs=pl.BlockSpec((1, rows, d), lambda l, j: (l, 0, j)),
        out_shape=jax.ShapeDtypeStruct((depth, rows, n6), F32),
        compiler_params=_params(("parallel", "parallel")),
    )(c_all, w_ada, b_ada.reshape(depth, 1, n6))


BF16_ROWS = 16


def _passenger_specs(ws, n_steps, step):
    specs = []
    for w in ws:
        rows, cols = w.shape
        units = rows // BF16_ROWS
        n_used = max(k for k in range(1, n_steps + 1) if units % k == 0)
        specs.append(pl.BlockSpec(
            (rows // n_used, cols),
            lambda *idx, n_used=n_used: (jnp.minimum(step(*idx), n_used - 1), 0)))
    return specs


def _cast_passengers(in_refs, out_refs):
    for src, dst in zip(in_refs, out_refs):
        dst[...] = src[...].astype(dst.dtype)


def _inproj_kernel(*refs, prec, n_cast):
    x_ref, g_ref, sc_ref, sh_ref, w_ref = refs[:5]
    o_ref = refs[5 + n_cast]
    _cast_passengers(refs[5:5 + n_cast], refs[6 + n_cast:])
    h = _normmod(x_ref[...], g_ref[...], _mod_row(sc_ref), _mod_row(sh_ref))
    o_ref[...] = _mm(h, w_ref[...], prec).astype(o_ref.dtype)


def _inproj(x2d, g, mod, sc_col, sh_col, w, *, tm, bn, prec, per_token, tiles_per_batch,
            out_dtype=F32, vmem_mb=None, cast_ws=()):
    n_tok, d = x2d.shape
    n_out = w.shape[1]
    n_j = n_out // bn
    p_specs = _passenger_specs(cast_ws, n_tok // tm * n_j, lambda i, j: i * n_j + j)
    out = pl.pallas_call(
        functools.partial(_inproj_kernel, prec=prec, n_cast=len(cast_ws)),
        grid=(n_tok // tm, n_j),
        in_specs=[pl.BlockSpec((tm, d), lambda i, j: (i, 0)),
                  pl.BlockSpec((1, d), lambda i, j: (0, 0)),
                  _mod_spec(per_token, tm, d, sc_col, tiles_per_batch),
                  _mod_spec(per_token, tm, d, sh_col, tiles_per_batch),
                  pl.BlockSpec((d, bn), lambda i, j: (0, j))] + p_specs,
        out_specs=[pl.BlockSpec((tm, bn), lambda i, j: (i, j))] + p_specs,
        out_shape=[jax.ShapeDtypeStruct((n_tok, n_out), out_dtype)]
        + [jax.ShapeDtypeStruct(c.shape, BF16) for c in cast_ws],
        compiler_params=_params(("arbitrary", "arbitrary"), vmem_mb),
    )(x2d, g, mod, mod, w, *cast_ws)
    return out if cast_ws else out[0]


def _outproj_kernel(a_ref, w_ref, x_ref, gt_ref, g_ref, o_ref, *, prec, glu):
    z = _mm(a_ref[...], w_ref[...], prec)
    if glu:
        d = z.shape[1] // 2
        z = z[:, :d] * jax.nn.sigmoid(z[:, d:])
    o_ref[...] = x_ref[...] + _mod_row(gt_ref) * (_rms(z) * g_ref[...])


def _outproj(a2d, w, x2d, mod, gt_col, g, *, tm, prec, glu, per_token, tiles_per_batch,
             vmem_mb=None):
    n_tok, d = x2d.shape
    k, n_out = w.shape
    return pl.pallas_call(
        functools.partial(_outproj_kernel, prec=prec, glu=glu),
        grid=(n_tok // tm,),
        in_specs=[pl.BlockSpec((tm, k), lambda i: (i, 0)),
                  pl.BlockSpec((k, n_out), lambda i: (0, 0)),
                  pl.BlockSpec((tm, d), lambda i: (i, 0)),
                  _mod_spec(per_token, tm, d, gt_col, tiles_per_batch),
                  pl.BlockSpec((1, d), lambda i: (0, 0))],
        out_specs=pl.BlockSpec((tm, d), lambda i: (i, 0)),
        out_shape=jax.ShapeDtypeStruct((n_tok, d), F32),
        compiler_params=_params(("parallel",), vmem_mb),
    )(a2d, w, x2d, mod, g)


def _ffn_kernel(x_ref, g1_ref, sc_ref, sh_ref, wg_ref, wu_ref, wd_ref, gt_ref, g2_ref,
                o_ref, h_scr, acc_scr, *, prec, n_c):
    c = pl.program_id(1)

    @pl.when(c == 0)
    def _():
        h = _normmod(x_ref[...], g1_ref[...], _mod_row(sc_ref), _mod_row(sh_ref))
        h_scr[...] = h.astype(h_scr.dtype)
        acc_scr[...] = jnp.zeros_like(acc_scr)

    h = h_scr[...]
    act = _silu(_mm(h, wg_ref[...], prec)) * _mm(h, wu_ref[...], prec)
    acc_scr[...] += _mm(act, wd_ref[...], prec)

    @pl.when(c == n_c - 1)
    def _():
        o_ref[...] = x_ref[...] + _mod_row(gt_ref) * (_rms(acc_scr[...]) * g2_ref[...])


def _ffn(x2d, g1, mod, sc_col, sh_col, gt_col, g2, wg, wu, wd, *, tm, fc, prec,
         per_token, tiles_per_batch, vmem_mb=None):
    n_tok, d = x2d.shape
    n_c = wg.shape[1] // fc
    tok = pl.BlockSpec((tm, d), lambda i, c: (i, 0))
    vec = pl.BlockSpec((1, d), lambda i, c: (0, 0))
    h_dtype = BF16 if prec == "bf16" else F32
    w_mode = dict(pipeline_mode=pl.Buffered(1)) if n_c == 1 else {}
    return pl.pallas_call(
        functools.partial(_ffn_kernel, prec=prec, n_c=n_c),
        grid=(n_tok // tm, n_c),
        in_specs=[tok, vec,
                  _mod_spec(per_token, tm, d, sc_col, tiles_per_batch),
                  _mod_spec(per_token, tm, d, sh_col, tiles_per_batch),
                  pl.BlockSpec((d, fc), lambda i, c: (0, c), **w_mode),
                  pl.BlockSpec((d, fc), lambda i, c: (0, c), **w_mode),
                  pl.BlockSpec((fc, d), lambda i, c: (c, 0), **w_mode),
                  _mod_spec(per_token, tm, d, gt_col, tiles_per_batch), vec],
        out_specs=tok,
        out_shape=jax.ShapeDtypeStruct((n_tok, d), F32),
        scratch_shapes=[pltpu.VMEM((tm, d), h_dtype), pltpu.VMEM((tm, d), F32)],
        compiler_params=_params(("parallel", "arbitrary"), vmem_mb),
    )(x2d, g1, mod, mod, wg, wu, wd, mod, g2)


def _router_kernel(x_ref, g_ref, sc_ref, sh_ref, wr_ref, comb_ref, hb_ref, *, n_experts):
    h = _normmod(x_ref[...], g_ref[...], _mod_row(sc_ref), _mod_row(sh_ref))
    hb_ref[...] = h.astype(hb_ref.dtype)
    h1, h2, h3 = _split(h, 3)
    w = wr_ref[...]
    p1, p2, p3 = _dot(h1, w), _dot(h2, w), _dot(h3, w)
    back = lambda x, k: pltpu.roll(x, LANES - k * n_experts, 1)
    small = (back(p1, 2) + p3) + back(p2, 1)
    logits = p1 + ((back(p1, 1) + p2) + small)
    lane = lax.broadcasted_iota(jnp.int32, logits.shape, 1).astype(F32)
    neg = -jnp.inf
    l1 = jnp.where(lane < n_experts, logits, neg)
    m1 = jnp.max(l1, axis=-1, keepdims=True)
    i1 = jnp.min(jnp.where(l1 == m1, lane, float(LANES)), axis=-1, keepdims=True)
    l2 = jnp.where(lane == i1, neg, l1)
    m2 = jnp.max(l2, axis=-1, keepdims=True)
    i2 = jnp.min(jnp.where(l2 == m2, lane, float(LANES)), axis=-1, keepdims=True)
    e2 = jnp.exp(m2 - m1)
    den = 1.0 + e2
    comb_ref[...] = jnp.where(lane == i1, 1.0 / den, 0.0) + jnp.where(lane == i2, e2 / den, 0.0)


def _router(x2d, g, mod, sc_col, sh_col, wr_pad, *, tm, n_experts, per_token, tiles_per_batch):
    n_tok, d = x2d.shape
    return pl.pallas_call(
        functools.partial(_router_kernel, n_experts=n_experts),
        grid=(n_tok // tm,),
        in_specs=[pl.BlockSpec((tm, d), lambda i: (i, 0)),
                  pl.BlockSpec((1, d), lambda i: (0, 0)),
                  _mod_spec(per_token, tm, d, sc_col, tiles_per_batch),
                  _mod_spec(per_token, tm, d, sh_col, tiles_per_batch),
                  pl.BlockSpec((d, LANES), lambda i: (0, 0))],
        out_specs=[pl.BlockSpec((tm, LANES), lambda i: (i, 0)),
                   pl.BlockSpec((tm, d), lambda i: (i, 0))],
        out_shape=[jax.ShapeDtypeStruct((n_tok, LANES), F32),
                   jax.ShapeDtypeStruct((n_tok, d), BF16)],
        compiler_params=_params(("parallel",)),
    )(x2d, g, mod, mod, wr_pad)


MOE_TB = 256
MOE_SEG = 16
MOE_TM = 512


def _moe_block_rows(n_experts):
    return -(-(2 * MOE_TB + n_experts * (MOE_SEG - 1)) // LANES) * LANES


def _routing_slots(comb, dest):
    lane = lax.broadcasted_iota(jnp.int32, comb.shape, 1).astype(F32)
    sel = comb > 0.0
    i_lo = jnp.min(jnp.where(sel, lane, float(LANES)), axis=-1, keepdims=True)
    i_hi = jnp.max(jnp.where(sel, lane, -1.0), axis=-1, keepdims=True)
    pick = lambda i, v: jnp.sum(jnp.where(lane == i, v, 0.0), axis=-1, keepdims=True)
    has_lo = i_lo < float(LANES)
    has_hi = i_hi > i_lo
    d_lo = jnp.where(has_lo, pick(i_lo, dest), -1.0)
    d_hi = jnp.where(has_hi, pick(i_hi, dest), -1.0)
    w_lo = pick(i_lo, comb)
    w_hi = jnp.where(has_hi, pick(i_hi, comb), 0.0)
    out = jnp.where(lane == 0.0, d_lo, 0.0)
    for k, v in ((1.0, d_hi), (2.0, w_lo), (3.0, w_hi)):
        out = jnp.where(lane == k, v, out)
    return out


def _one_hot_rows(row, n_rows):
    tb = row.shape[0]
    lane = lax.broadcasted_iota(jnp.int32, (1, LANES), 1).astype(F32)
    row = jnp.broadcast_to(row, (tb, LANES))
    blocks = [jnp.where(row == lane + float(cb * LANES), 1.0, 0.0) for cb in range(n_rows // LANES)]
    return jnp.concatenate(blocks, axis=1).astype(BF16)


_SEG_BITS = (256, 128, 64, 32, 16)


def _segment_copies(make_copy, blk, pos_ref, off_ref, cnt_ref, n_experts, start):
    for e in range(n_experts):
        seg = blk * n_experts + e
        in_blk, in_sorted, n = off_ref[seg], pos_ref[seg], cnt_ref[seg]
        for size in _SEG_BITS:
            has = (n & size) != 0

            @pl.when(has)
            def _(in_blk=in_blk, in_sorted=in_sorted, size=size):
                cp = make_copy(pl.multiple_of(in_blk, MOE_SEG), pl.multiple_of(in_sorted, MOE_SEG), size)
                cp.start() if start else cp.wait()

            step = jnp.where(has, size, 0)
            in_blk, in_sorted = in_blk + step, in_sorted + step


def _moe_sort_kernel(pos_ref, off_ref, cnt_ref, gpos_ref, glen_ref, hbp_ref, combp_ref, hbs_ref,
                     combs_ref, dest_ref, xs_hbm, buf, zbuf, sems, *, n_rows, n_experts, n_blk, n_blk_p):
    b = pl.program_id(0)
    slot = b % 2

    def writes(blk, slot, start):
        def make_copy(block_row, sorted_row, size):
            return pltpu.make_async_copy(buf.at[slot, pl.ds(block_row, size)],
                                         xs_hbm.at[pl.ds(sorted_row, size)], sems.at[slot])
        _segment_copies(make_copy, blk, pos_ref, off_ref, cnt_ref, n_experts, start)

    def zero_gaps(start):
        for e in range(n_experts):
            row, n = gpos_ref[e], glen_ref[e]
            for size in _SEG_BITS:
                has = (n & size) != 0

                @pl.when(has)
                def _(row=row, size=size):
                    cp = pltpu.make_async_copy(
                        zbuf.at[pl.ds(0, size)],
                        xs_hbm.at[pl.ds(pl.multiple_of(row, MOE_SEG), size)], sems.at[2])
                    cp.start() if start else cp.wait()

                row = row + jnp.where(has, size, 0)
        tail_row, tail_n, big = gpos_ref[n_experts], glen_ref[n_experts], _SEG_BITS[0]

        def piece(i, carry):
            cp = pltpu.make_async_copy(
                zbuf.at[pl.ds(0, big)],
                xs_hbm.at[pl.ds(pl.multiple_of(tail_row + i * big, MOE_SEG), big)], sems.at[2])
            cp.start() if start else cp.wait()
            return carry

        lax.fori_loop(0, tail_n // big, piece, 0)

    @pl.when(b == 0)
    def _():
        zbuf[...] = jnp.zeros_like(zbuf)
        zero_gaps(True)

    @pl.when(b >= 2)
    def _():
        writes(b - 2, slot, False)

    in_prompt = b < n_blk_p
    comb = jnp.where(in_prompt, combp_ref[...], combs_ref[...])
    hb = jnp.where(in_prompt, hbp_ref[...], hbs_ref[...])
    tb = comb.shape[0]
    sel = jnp.where(comb > 0.0, 1.0, 0.0)
    r = lax.broadcasted_iota(jnp.int32, (tb, tb), 0)
    c = lax.broadcasted_iota(jnp.int32, (tb, tb), 1)
    rank = _dot(jnp.where(c < r, 1.0, 0.0).astype(BF16), sel.astype(BF16))
    cnt = jnp.sum(sel, axis=0, keepdims=True)
    cnt = jnp.floor((cnt + (MOE_SEG - 1)) / MOE_SEG) * MOE_SEG
    ru = lax.broadcasted_iota(jnp.int32, (LANES, LANES), 0)
    cu = lax.broadcasted_iota(jnp.int32, (LANES, LANES), 1)
    off = _dot(jnp.broadcast_to(cnt, (8, LANES)).astype(BF16),
               jnp.where(ru < cu, 1.0, 0.0).astype(BF16))[0:1]
    slots = _routing_slots(comb, off + rank)
    dest_ref[...] = slots
    pt = _one_hot_rows(slots[:, 0:1], n_rows) + _one_hot_rows(slots[:, 1:2], n_rows)
    buf[slot] = _dot(pt, hb, ((0,), (0,))).astype(buf.dtype)
    writes(b, slot, True)

    @pl.when(b == n_blk - 1)
    def _():
        writes(b, slot, False)
        if n_blk > 1:
            writes(b - 1, 1 - slot, False)
        zero_gaps(False)


def _moe_sort(seg, gap_pos, gap_len, hb_p, comb_p, hb_s, comb_s, n_sorted_rows, *, n_experts):
    n_p, d = hb_p.shape
    n_blk_p = n_p // MOE_TB
    n_blk = n_blk_p + 1
    n_rows = _moe_block_rows(n_experts)
    last_p = n_blk_p - 1
    return pl.pallas_call(
        functools.partial(_moe_sort_kernel, n_rows=n_rows, n_experts=n_experts, n_blk=n_blk,
                          n_blk_p=n_blk_p),
        grid_spec=pltpu.PrefetchScalarGridSpec(
            num_scalar_prefetch=5, grid=(n_blk,),
            in_specs=[pl.BlockSpec((MOE_TB, d), lambda b, *_: (jnp.minimum(b, last_p), 0)),
                      pl.BlockSpec((MOE_TB, LANES), lambda b, *_: (jnp.minimum(b, last_p), 0)),
                      pl.BlockSpec((MOE_TB, d), lambda b, *_: (0, 0)),
                      pl.BlockSpec((MOE_TB, LANES), lambda b, *_: (0, 0))],
            out_specs=[pl.BlockSpec((MOE_TB, LANES), lambda b, *_: (b, 0)),
                       pl.BlockSpec(memory_space=pl.ANY)],
            scratch_shapes=[pltpu.VMEM((2, n_rows, d), BF16), pltpu.VMEM((_SEG_BITS[0], d), BF16),
                            pltpu.SemaphoreType.DMA((3,))]),
        out_shape=[jax.ShapeDtypeStruct((n_blk * MOE_TB, LANES), F32),
                   jax.ShapeDtypeStruct((n_sorted_rows, d), BF16)],
        compiler_params=_params(("arbitrary",)),
    )(*seg, gap_pos, gap_len, hb_p, comb_p, hb_s, comb_s)


def _experts_kernel(te_ref, nt_ref, x_ref, wg_ref, wu_ref, wd_ref, o_ref, acc_scr, *, n_c):
    del te_ref
    k = pl.program_id(0)
    c = pl.program_id(1)
    active = k < nt_ref[0]

    @pl.when(active)
    def _():
        h = x_ref[...]
        act = _silu(_dot(h, wg_ref[0])) * _dot(h, wu_ref[0])
        y = _dot(act.astype(BF16), wd_ref[0])

        @pl.when(c == 0)
        def _():
            acc_scr[...] = y

        @pl.when(c > 0)
        def _():
            acc_scr[...] += y

        @pl.when(c == n_c - 1)
        def _():
            o_ref[...] = acc_scr[...].astype(o_ref.dtype)

    @pl.when(jnp.logical_not(active) & (c == n_c - 1))
    def _():
        o_ref[...] = jnp.zeros_like(o_ref)


def _experts(tile_expert, n_tiles, xs, wg, wu, wd, *, fc):
    n_rows, d = xs.shape
    n_c = wg.shape[2] // fc

    def row_map(k, c, te, nt):
        return (jnp.minimum(k, nt[0] - 1), 0)

    def w_map(k, c, te, nt):
        return (te[k], 0, jnp.where(k < nt[0], c, n_c - 1))

    def wd_map(k, c, te, nt):
        return (te[k], jnp.where(k < nt[0], c, n_c - 1), 0)

    return pl.pallas_call(
        functools.partial(_experts_kernel, n_c=n_c),
        grid_spec=pltpu.PrefetchScalarGridSpec(
            num_scalar_prefetch=2, grid=(n_rows // MOE_TM, n_c),
            in_specs=[pl.BlockSpec((MOE_TM, d), row_map),
                      pl.BlockSpec((1, d, fc), w_map), pl.BlockSpec((1, d, fc), w_map),
                      pl.BlockSpec((1, fc, d), wd_map)],
            out_specs=pl.BlockSpec((MOE_TM, d), lambda k, c, te, nt: (k, 0)),
            scratch_shapes=[pltpu.VMEM((MOE_TM, d), F32)]),
        out_shape=jax.ShapeDtypeStruct((n_rows, d), BF16),
        compiler_params=_params(("arbitrary", "arbitrary"), 48),
    )(tile_expert, n_tiles, xs, wg, wu, wd)


def _moe_combine_kernel(pos_ref, off_ref, cnt_ref, ys_hbm, dest_ref, x_ref, gt_ref, g_ref,
                        o_ref, buf, sems, *, n_experts, n_blk, blk0):
    b = pl.program_id(0)
    slot = b % 2

    def reads(blk, slot, start):
        def make_copy(block_row, sorted_row, size):
            return pltpu.make_async_copy(ys_hbm.at[pl.ds(sorted_row, size)],
                                         buf.at[slot, pl.ds(block_row, size)], sems.at[slot])
        _segment_copies(make_copy, blk0 + blk, pos_ref, off_ref, cnt_ref, n_experts, start)

    @pl.when(b == 0)
    def _():
        buf[...] = jnp.zeros_like(buf)
        reads(b, slot, True)

    if n_blk > 1:
        @pl.when(b + 1 < n_blk)
        def _():
            reads(b + 1, 1 - slot, True)

    reads(b, slot, False)
    yb = buf[slot]
    slots = dest_ref[...]
    n_rows = yb.shape[0]
    y = (slots[:, 2:3] * _dot(_one_hot_rows(slots[:, 0:1], n_rows), yb)
         + slots[:, 3:4] * _dot(_one_hot_rows(slots[:, 1:2], n_rows), yb))
    o_ref[...] = x_ref[...] + _mod_row(gt_ref) * (_rms(y) * g_ref[...])


def _moe_combine(sorted_pos, block_off, seg_cnt, ys, dest, x2d, mod, gt_col, g, *, blk0,
                 n_experts, per_token, tiles_per_batch):
    n_tok, d = x2d.shape
    n_blk = n_tok // MOE_TB
    n_rows = _moe_block_rows(n_experts)
    return pl.pallas_call(
        functools.partial(_moe_combine_kernel, n_experts=n_experts, n_blk=n_blk, blk0=blk0),
        grid_spec=pltpu.PrefetchScalarGridSpec(
            num_scalar_prefetch=3, grid=(n_blk,),
            in_specs=[pl.BlockSpec(memory_space=pl.ANY),
                      pl.BlockSpec((MOE_TB, LANES), lambda b, *_: (blk0 + b, 0)),
                      pl.BlockSpec((MOE_TB, d), lambda b, *_: (b, 0)),
                      _mod_spec(per_token, MOE_TB, d, gt_col, tiles_per_batch),
                      pl.BlockSpec((1, d), lambda b, *_: (0, 0))],
            out_specs=pl.BlockSpec((MOE_TB, d), lambda b, *_: (b, 0)),
            scratch_shapes=[pltpu.VMEM((2, n_rows, d), BF16), pltpu.SemaphoreType.DMA((2,))]),
        out_shape=jax.ShapeDtypeStruct((n_tok, d), F32),
        compiler_params=_params(("arbitrary",)),
    )(sorted_pos, block_off, seg_cnt, ys, dest, x2d, mod, g)


def _moe_schedule(combs, n_experts, n_tiles_max):
    def block_counts(comb):
        n_blk = comb.shape[0] // MOE_TB
        sel = (comb[:, :n_experts] > 0.0).astype(jnp.int32)
        return jnp.sum(sel.reshape(n_blk, MOE_TB, n_experts), axis=1)

    cnt = jnp.concatenate([block_counts(c) for c in combs], axis=0)
    cnt = -(-cnt // MOE_SEG) * MOE_SEG
    off = jnp.cumsum(cnt, axis=1) - cnt
    tot = jnp.sum(cnt, axis=0)
    grp = -(-tot // MOE_TM) * MOE_TM
    g_end = jnp.cumsum(grp)
    sorted_pos = (g_end - grp)[None, :] + (jnp.cumsum(cnt, axis=0) - cnt)
    n_tiles = g_end[-1] // MOE_TM
    first_row = jnp.arange(n_tiles_max, dtype=jnp.int32) * MOE_TM
    tile_expert = jnp.sum(first_row[:, None] >= g_end[None, :], axis=1)
    last = jnp.sum((n_tiles - 1) * MOE_TM >= g_end)
    tile_expert = jnp.minimum(tile_expert, last).astype(jnp.int32)
    flat = lambda a: a.reshape(-1).astype(jnp.int32)
    seg = (flat(sorted_pos), flat(off), flat(cnt))
    gap_pos = flat(jnp.concatenate([g_end - grp + tot, g_end[-1:]]))
    gap_len = flat(jnp.concatenate([grp - tot, n_tiles_max * MOE_TM - g_end[-1:]]))
    return seg, gap_pos, gap_len, tile_expert, n_tiles.reshape(1).astype(jnp.int32)


def _gla_conv_kernel(p_ref, wgu_ref, bgu_ref, gh_ref, wc_ref, mixed_ref, st_ref, cst_ref,
                     s_scr, uext_scr, *, tt, n_t, heads, dk, dv, cw):
    ti = pl.program_id(1)
    hk = heads * dk
    hv = heads * dv
    o_q, o_k, o_v, o_og = 0, hk, 2 * hk, 2 * hk + hv
    o_ch = o_og + hv
    o_cb, o_cc, o_lr = o_ch + cw, o_ch + 2 * cw, o_ch + 3 * cw
    L = GLA_CHUNK

    @pl.when(ti == 0)
    def _():
        s_scr[...] = jnp.zeros_like(s_scr)
        uext_scr[0:8, :] = jnp.zeros((8, cw), F32)

    logit = _dot(p_ref[:, o_lr:o_lr + LANES].astype(BF16), wgu_ref[...]) + bgu_ref[...]
    logg = _log_sigmoid(logit) / GATE_NORMALIZER

    r = lax.broadcasted_iota(jnp.int32, (tt, tt), 0)
    c = lax.broadcasted_iota(jnp.int32, (tt, tt), 1)
    tri = jnp.where((r // L == c // L) & (c <= r), 1.0, 0.0).astype(BF16)
    g_hi = logg.astype(BF16)
    rem = logg - g_hi.astype(F32)
    g_mid = rem.astype(BF16)
    g_lo = (rem - g_mid.astype(F32)).astype(BF16)
    bc = _dot(tri, g_hi) + (_dot(tri, g_mid) + _dot(tri, g_lo))

    rl = lax.broadcasted_iota(jnp.int32, (L, L), 0)
    cl = lax.broadcasted_iota(jnp.int32, (L, L), 1)
    tril = cl <= rl
    lane = lax.broadcasted_iota(jnp.int32, (L, LANES), 1)
    heads_per_blk = LANES // dk
    nt_dims = ((1,), (1,))

    for ck in range(tt // L):
        rows = slice(L * ck, L * ck + L)
        b = bc[rows]
        bl = b[L - 1:L]
        q_in = (p_ref[rows, o_q:o_q + hk] * (dk ** -0.5)) * jnp.exp(b)
        k = p_ref[rows, o_k:o_k + hk]
        k_out = (k * jnp.exp(-b)).astype(BF16)
        k_dec = (k * jnp.exp(bl - b)).astype(BF16)
        dec = jnp.exp(bl)
        q_in = q_in.astype(BF16)
        for h in range(heads):
            blk = slice(LANES * (h // heads_per_blk), LANES * (h // heads_per_blk) + LANES)
            in_head = (lane // dk) == (h % heads_per_blk)
            qm = jnp.where(in_head, q_in[:, blk], jnp.zeros_like(q_in[:, blk]))
            att = _dot(qm, k_out[:, blk], nt_dims)
            att = jnp.where(tril, att, 0.0).astype(BF16)
            vh = p_ref[rows, o_v + dv * h:o_v + dv * h + dv].astype(BF16)
            s_t = s_scr[h]
            o = _dot(att, vh) + _dot(qm, s_t.astype(BF16), nt_dims)
            ds_t = _dot(vh, k_dec[:, blk], ((0,), (0,)))
            s_scr[h] = s_t * dec[:, blk] + ds_t
            og = p_ref[rows, o_og + dv * h:o_og + dv * h + dv]
            res = (_rms(o) * gh_ref[...]) * _silu(og)
            mixed_ref[rows, dv * h:dv * h + dv] = res.astype(mixed_ref.dtype)

    u = p_ref[:, o_cc:o_cc + cw] * p_ref[:, o_ch:o_ch + cw]
    uext_scr[8:8 + tt, :] = u
    y = (wc_ref[0:1, :] * uext_scr[6:6 + tt, :] + wc_ref[1:2, :] * uext_scr[7:7 + tt, :]
         + wc_ref[2:3, :] * u)
    mixed_ref[:, hv:hv + cw] = (p_ref[:, o_cb:o_cb + cw] * y).astype(mixed_ref.dtype)
    tail = uext_scr[tt + 6:tt + 8, :]
    uext_scr[6:8, :] = tail

    @pl.when(ti == n_t - 1)
    def _():
        st_ref[0] = s_scr[...]
        cst_ref[0] = tail


def _gla_conv_prompt(proj, wgu_pad, bgu, g_head, w_conv, *, bsz, t, tt, heads, dk, dv, cw):
    n_t = t // tt
    n_in = proj.shape[1]
    width = heads * dv + cw
    return pl.pallas_call(
        functools.partial(_gla_conv_kernel, tt=tt, n_t=n_t, heads=heads, dk=dk, dv=dv, cw=cw),
        grid=(bsz, n_t),
        in_specs=[pl.BlockSpec((tt, n_in), lambda b, i: (b * n_t + i, 0)),
                  pl.BlockSpec(wgu_pad.shape, lambda b, i: (0, 0)),
                  pl.BlockSpec(bgu.shape, lambda b, i: (0, 0)),
                  pl.BlockSpec(g_head.shape, lambda b, i: (0, 0)),
                  pl.BlockSpec(w_conv.shape, lambda b, i: (0, 0))],
        out_specs=[pl.BlockSpec((tt, width), lambda b, i: (b * n_t + i, 0)),
                   pl.BlockSpec((1, heads, dv, LANES), lambda b, i: (b, 0, 0, 0)),
                   pl.BlockSpec((1, 2, cw), lambda b, i: (b, 0, 0))],
        out_shape=[jax.ShapeDtypeStruct((bsz * t, width), BF16),
                   jax.ShapeDtypeStruct((bsz, heads, dv, LANES), F32),
                   jax.ShapeDtypeStruct((bsz, 2, cw), F32)],
        scratch_shapes=[pltpu.VMEM((heads, dv, LANES), F32), pltpu.VMEM((tt + 8, cw), F32)],
        compiler_params=_params(("parallel", "arbitrary"), 48),
    )(proj, wgu_pad, bgu, g_head, w_conv)


def _gla_conv_step_kernel(q_ref, k_ref, v_ref, og_ref, ch_ref, cb_ref, cc_ref, lr_ref,
                          wgu_ref, bgu_ref, gh_ref, wc_ref, s_ref, cbuf_ref,
                          o_ref, y_ref, sn_ref, cn_ref, a_scr, *, dk):
    logit = _mm(wgu_ref[...], lr_ref[...], "f32") + bgu_ref[...]
    a_scr[...] = jnp.exp(_log_sigmoid(logit) / GATE_NORMALIZER)
    v_t = v_ref[...]

    def body(d, acc):
        a = a_scr[pl.ds(d, 1), :]
        kd = k_ref[pl.ds(d, 1), :]
        qd = q_ref[pl.ds(d, 1), :]
        s_new = a * s_ref[0, d] + kd * v_t
        sn_ref[0, d] = s_new
        return acc + (qd * (dk ** -0.5)) * s_new

    o = lax.fori_loop(0, dk, body, jnp.zeros(v_t.shape, F32))
    o = o * lax.rsqrt(jnp.mean(o * o, axis=0, keepdims=True) + EPS) * gh_ref[...]
    o_ref[...] = o * _silu(og_ref[...])
    u = cc_ref[...] * ch_ref[...]
    y = wc_ref[0] * cbuf_ref[0] + wc_ref[1] * cbuf_ref[1] + wc_ref[2] * u
    y_ref[...] = cb_ref[...] * y
    cn_ref[0] = cbuf_ref[1]
    cn_ref[1] = u


def _gla_conv_step(proj_t, wgu_t_pad, bgu_col, gh_col, wc_col, s_t, cbuf_t, *, heads, dk, dv, cw):
    bsz = proj_t.shape[1]
    hk, hv = heads * dk, heads * dv
    cs = cw // heads
    assert dv == LANES and cs == LANES and dk * 2 == LANES
    o_k, o_v, o_og = hk, 2 * hk, 2 * hk + hv
    o_ch = o_og + hv
    o_cb, o_cc, o_lr = o_ch + cw, o_ch + 2 * cw, o_ch + 3 * cw
    row = lambda off, size: (lambda h: (off // size + h, 0))
    blk = lambda size, off: pl.BlockSpec((size, bsz), row(off, size))
    return pl.pallas_call(
        functools.partial(_gla_conv_step_kernel, dk=dk),
        grid=(heads,),
        in_specs=[blk(dk, 0), blk(dk, o_k), blk(dv, o_v), blk(dv, o_og),
                  blk(cs, o_ch), blk(cs, o_cb), blk(cs, o_cc),
                  pl.BlockSpec((LANES, bsz), lambda h: (o_lr // LANES, 0)),
                  pl.BlockSpec((dk, LANES), lambda h: (h, 0)),
                  pl.BlockSpec((dk, 1), lambda h: (h, 0)),
                  pl.BlockSpec((dv, 1), lambda h: (0, 0)),
                  pl.BlockSpec((3, cs, 1), lambda h: (0, h, 0)),
                  pl.BlockSpec((1, dk, dv, bsz), lambda h: (h, 0, 0, 0)),
                  pl.BlockSpec((2, cs, bsz), lambda h: (0, h, 0))],
        out_specs=[pl.BlockSpec((dv, bsz), lambda h: (h, 0)),
                   pl.BlockSpec((cs, bsz), lambda h: (h, 0)),
                   pl.BlockSpec((1, dk, dv, bsz), lambda h: (h, 0, 0, 0)),
                   pl.BlockSpec((2, cs, bsz), lambda h: (0, h, 0))],
        out_shape=[jax.ShapeDtypeStruct((hv, bsz), F32),
                   jax.ShapeDtypeStruct((cw, bsz), F32),
                   jax.ShapeDtypeStruct(s_t.shape, F32),
                   jax.ShapeDtypeStruct(cbuf_t.shape, F32)],
        scratch_shapes=[pltpu.VMEM((dk, bsz), F32)],
        compiler_params=_params(("parallel",)),
    )(proj_t, proj_t, proj_t, proj_t, proj_t, proj_t, proj_t, proj_t,
      wgu_t_pad, bgu_col, gh_col, wc_col, s_t, cbuf_t)


S5_ROW_PAD = 8


def _s5_scan_kernel(*refs, tt, n_t, nb, half, n_cast):
    u_ref, bb_ref, cm_ref, abr_ref, abi_ref, d_ref = refs[:6]
    y_ref, sre_ref, sim_ref = refs[6 + n_cast:9 + n_cast]
    scr, xr_scr, xi_scr = refs[9 + 2 * n_cast:]
    _cast_passengers(refs[6:6 + n_cast], refs[9 + n_cast:9 + 2 * n_cast])
    ti = pl.program_id(1)
    stride = tt + S5_ROW_PAD
    n_blk = half // LANES

    @pl.when(ti == 0)
    def _():
        xr_scr[...] = jnp.zeros_like(xr_scr)
        xi_scr[...] = jnp.zeros_like(xi_scr)

    for b in range(nb):
        bu = _dot(u_ref[b].astype(BF16), bb_ref[0])
        for cb in range(2 * n_blk):
            scr[cb, b * stride:b * stride + tt, :] = bu[:, cb * LANES:(cb + 1) * LANES]
    a_re = [jnp.broadcast_to(abr_ref[0][:, cb * LANES:(cb + 1) * LANES], (nb, LANES))
            for cb in range(n_blk)]
    a_im = [jnp.broadcast_to(abi_ref[0][:, cb * LANES:(cb + 1) * LANES], (nb, LANES))
            for cb in range(n_blk)]

    def body(t, carry):
        rows = pl.ds(t, nb, stride=stride)
        out = []
        for cb in range(n_blk):
            xr, xi = carry[2 * cb], carry[2 * cb + 1]
            nr = (a_re[cb] * xr - a_im[cb] * xi) + scr[cb, rows, :]
            ni = (a_re[cb] * xi + a_im[cb] * xr) + scr[n_blk + cb, rows, :]
            scr[cb, rows, :] = nr
            scr[n_blk + cb, rows, :] = ni
            out += [nr, ni]
        return tuple(out)

    init = []
    for cb in range(n_blk):
        init += [xr_scr[:, cb * LANES:(cb + 1) * LANES], xi_scr[:, cb * LANES:(cb + 1) * LANES]]
    fin = lax.fori_loop(0, tt, body, tuple(init), unroll=4)
    xr = jnp.concatenate([fin[2 * cb] for cb in range(n_blk)], axis=1)
    xi = jnp.concatenate([fin[2 * cb + 1] for cb in range(n_blk)], axis=1)
    xr_scr[...] = xr
    xi_scr[...] = xi
    for b in range(nb):
        xs = jnp.concatenate([scr[cb, b * stride:b * stride + tt, :] for cb in range(2 * n_blk)],
                             axis=1)
        y = _dot(xs.astype(BF16), cm_ref[0]) + d_ref[0] * u_ref[b]
        y_ref[b] = _gelu_tanh(y).astype(y_ref.dtype)

    @pl.when(ti == n_t - 1)
    def _():
        sre_ref[...] = xr
        sim_ref[...] = xi


def _s5_prompt(u3d, bb_blk, c_blk, ab_re, ab_im, d_blk, cast_ws, *, tt):
    nb, t, d = u3d.shape
    n_j = d // LANES
    n_t = t // tt
    half = bb_blk.shape[2] // 2
    w_specs = _passenger_specs(cast_ws, n_j * n_t, lambda j, i: j * n_t + i)
    return pl.pallas_call(
        functools.partial(_s5_scan_kernel, tt=tt, n_t=n_t, nb=nb, half=half, n_cast=len(cast_ws)),
        grid=(n_j, n_t),
        in_specs=[pl.BlockSpec((nb, tt, LANES), lambda j, i: (0, i, j)),
                  pl.BlockSpec((1, LANES, 2 * half), lambda j, i: (j, 0, 0)),
                  pl.BlockSpec((1, 2 * half, LANES), lambda j, i: (j, 0, 0)),
                  pl.BlockSpec((1, 1, half), lambda j, i: (j, 0, 0)),
                  pl.BlockSpec((1, 1, half), lambda j, i: (j, 0, 0)),
                  pl.BlockSpec((1, 1, LANES), lambda j, i: (j, 0, 0))] + w_specs,
        out_specs=[pl.BlockSpec((nb, tt, LANES), lambda j, i: (0, i, j)),
                   pl.BlockSpec((nb, half), lambda j, i: (0, j)),
                   pl.BlockSpec((nb, half), lambda j, i: (0, j))] + w_specs,
        out_shape=[jax.ShapeDtypeStruct((nb, t, d), BF16),
                   jax.ShapeDtypeStruct((nb, n_j * half), F32),
                   jax.ShapeDtypeStruct((nb, n_j * half), F32)]
        + [jax.ShapeDtypeStruct(w.shape, BF16) for w in cast_ws],
        scratch_shapes=[pltpu.VMEM((2 * half // LANES, nb * (tt + S5_ROW_PAD), LANES), F32),
                        pltpu.VMEM((nb, half), F32), pltpu.VMEM((nb, half), F32)],
        compiler_params=_params(("arbitrary", "arbitrary"), 48),
    )(u3d, bb_blk, c_blk, ab_re, ab_im, d_blk, *cast_ws)


def _s5_step_kernel(u_ref, bb_ref, cm_ref, abr_ref, abi_ref, d_ref, sre_ref, sim_ref,
                    y_ref, nre_ref, nim_ref, *, half):
    u = u_ref[...]
    bu = _mm(u, bb_ref[0], "f32")
    ar, ai = abr_ref[0], abi_ref[0]
    sr, si = sre_ref[...], sim_ref[...]
    nr = (ar * sr - ai * si) + bu[:, 0:half]
    ni = (ar * si + ai * sr) + bu[:, half:2 * half]
    nre_ref[...] = nr
    nim_ref[...] = ni
    y = _mm(jnp.concatenate([nr, ni], axis=1), cm_ref[0], "f32") + d_ref[0] * u
    y_ref[...] = _gelu_tanh(y)


def _s5_step(u2d, bb_blk, c_blk, ab_re, ab_im, d_blk, s_re, s_im):
    bsz, d = u2d.shape
    n_j = d // LANES
    half = bb_blk.shape[2] // 2
    st = pl.BlockSpec((bsz, half), lambda j: (0, j))
    return pl.pallas_call(
        functools.partial(_s5_step_kernel, half=half),
        grid=(n_j,),
        in_specs=[pl.BlockSpec((bsz, LANES), lambda j: (0, j)),
                  pl.BlockSpec((1, LANES, 2 * half), lambda j: (j, 0, 0)),
                  pl.BlockSpec((1, 2 * half, LANES), lambda j: (j, 0, 0)),
                  pl.BlockSpec((1, 1, half), lambda j: (j, 0, 0)),
                  pl.BlockSpec((1, 1, half), lambda j: (j, 0, 0)),
                  pl.BlockSpec((1, 1, LANES), lambda j: (j, 0, 0)),
                  st, st],
        out_specs=[pl.BlockSpec((bsz, LANES), lambda j: (0, j)), st, st],
        out_shape=[jax.ShapeDtypeStruct((bsz, d), F32),
                   jax.ShapeDtypeStruct(s_re.shape, F32),
                   jax.ShapeDtypeStruct(s_im.shape, F32)],
        compiler_params=_params(("parallel",)),
    )(u2d, bb_blk, c_blk, ab_re, ab_im, d_blk, s_re, s_im)


def _s5_zoh(a_re, a_im, log_dt, b_re, b_im):
    dt = jnp.exp(log_dt)[:, None]
    mag = jnp.exp(dt * a_re)
    ab_re, ab_im = mag * jnp.cos(dt * a_im), mag * jnp.sin(dt * a_im)
    den = a_re * a_re + a_im * a_im
    nr, ni = ab_re - 1.0, ab_im
    f_re = (nr * a_re + ni * a_im) / den
    f_im = (ni * a_re - nr * a_im) / den
    bb_re = f_re[..., None] * b_re - f_im[..., None] * b_im
    bb_im = f_re[..., None] * b_im + f_im[..., None] * b_re
    return ab_re, ab_im, bb_re, bb_im


def _s5_step_blocks(ab_re, ab_im, bb_re, bb_im, c_re, c_im, d_skip):
    g, p = ab_re.shape
    cg = S5_CHUNK_GROUPS
    n_j = g // cg
    eye = jnp.eye(cg, dtype=F32)

    def in_blk(m):
        m = m.reshape(n_j, cg, p, S5_GROUP)
        return jnp.einsum("jgpi,gh->jgihp", m, eye).reshape(n_j, cg * S5_GROUP, cg * p)

    def out_blk(m):
        m = m.reshape(n_j, cg, S5_GROUP, p)
        return jnp.einsum("jgip,gh->jgphi", m, eye).reshape(n_j, cg * p, cg * S5_GROUP)

    bb_blk = jnp.concatenate([in_blk(bb_re), in_blk(bb_im)], axis=2)
    c_blk = jnp.concatenate([out_blk(c_re), out_blk(-c_im)], axis=1)
    return (bb_blk, c_blk, ab_re.reshape(n_j, 1, cg * p), ab_im.reshape(n_j, 1, cg * p),
            d_skip.reshape(n_j, 1, cg * S5_GROUP))


def kernel(x_prompt, x_sample, c_prompt, c_sample, state_gla, state_conv, state_s5_re, state_s5_im, w_ada, b_ada, g_norm, w_in0, w_gate_up, b_gate_up, g_head_norm, w_conv, w_out0, w_ffn_gate, w_ffn_up, w_ffn_down, w_in1, s5_a_re, s5_a_im, s5_log_dt, s5_b_re, s5_b_im, s5_c_re, s5_c_im, s5_d, w_glu, w_router, w_exp_gate, w_exp_up, w_exp_down):
    bsz, t, d = x_prompt.shape
    bs = x_sample.shape[0]
    heads, dk, dv = state_gla.shape[2:]
    cw = state_conv.shape[3]
    lowrank = w_gate_up.shape[1]
    hk, hv = heads * dk, heads * dv
    n_experts = w_router.shape[2]
    g_s5, p_s5 = s5_a_re.shape[1:]

    mod = _ada(jnp.concatenate([c_prompt, c_sample], axis=0), w_ada, b_ada)
    mod_p = [mod[l, :bsz].reshape(bsz, 1, N_MOD * d) for l in range(2)]
    mod_s = [mod[l, bsz:] for l in range(2)]
    SH_M, SC_M, GT_M, SH_F, SC_F, GT_F = range(N_MOD)
    gn = lambda l, k: g_norm[l, k].reshape(1, d)

    w0 = w_in0[0]
    o_lr = 2 * hk + 2 * hv
    w0r = jnp.concatenate([w0[:, :o_lr], w0[:, o_lr + lowrank:], w0[:, o_lr:o_lr + lowrank],
                           jnp.zeros((d, LANES - lowrank), F32)], axis=1)
    wgu_pad = jnp.concatenate([w_gate_up[0], jnp.zeros((LANES - lowrank, hk), F32)], axis=0)
    zoh = _s5_zoh(s5_a_re[0], s5_a_im[0], s5_log_dt[0], s5_b_re[0], s5_b_im[0])
    bb_blk, c_blk, ab_re, ab_im, d_blk = _s5_step_blocks(*zoh, s5_c_re[0], s5_c_im[0], s5_d[0])
    wr_pad = jnp.concatenate(_split(w_router[0], 3) + [jnp.zeros((d, LANES - 3 * n_experts), BF16)],
                             axis=1)
    bf = lambda a: a.astype(BF16)

    tm = 512
    tpb = t // tm
    pk = dict(per_token=False, tiles_per_batch=tpb)
    xp = x_prompt.reshape(bsz * t, d)
    proj, w_o0, w_fg, w_fu, w_fd, w_i1, w_gl = _inproj(
        xp, gn(0, 0), mod_p[0], SC_M, SH_M, bf(w0r), tm=tm, bn=w0r.shape[1], prec="bf16", vmem_mb=48,
        cast_ws=(w_out0[0], w_ffn_gate[0], w_ffn_up[0], w_ffn_down[0], w_in1[0], w_glu[0]), **pk)
    mixed, st_t, cst = _gla_conv_prompt(
        proj, bf(wgu_pad), b_gate_up[0].reshape(1, hk), g_head_norm[0].reshape(1, dv), w_conv[0],
        bsz=bsz, t=t, tt=256, heads=heads, dk=dk, dv=dv, cw=cw)
    st_t = st_t.reshape(bsz, heads // 2, 2, dv, 2, dk)
    gla_p = jnp.stack([st_t[:, :, 0, :, 0, :], st_t[:, :, 1, :, 1, :]], axis=2)
    gla_p = jnp.swapaxes(gla_p.reshape(bsz, heads, dv, dk), 2, 3)
    x1 = _outproj(mixed, w_o0, xp, mod_p[0], GT_M, gn(0, 1), tm=tm, prec="bf16", glu=False, **pk)
    x2 = _ffn(x1, gn(0, 2), mod_p[0], SC_F, SH_F, GT_F, gn(0, 3), w_fg, w_fu, w_fd, tm=tm, fc=2816,
              prec="bf16", vmem_mb=56, **pk)
    u = _inproj(x2, gn(1, 0), mod_p[1], SC_M, SH_M, w_i1, tm=tm, bn=d, prec="bf16", **pk)
    ew = (w_exp_gate[0], w_exp_up[0], w_exp_down[0])
    yg, re_p, im_p, *ew_bf = _s5_prompt(
        u.reshape(bsz, t, d), bf(bb_blk), bf(c_blk), ab_re, ab_im, d_blk,
        tuple(w.reshape(-1, w.shape[2]) for w in ew), tt=256)
    w_eg, w_eu, w_ed = (wb.reshape(w.shape) for wb, w in zip(ew_bf, ew))
    x3 = _outproj(yg.reshape(bsz * t, d), w_gl, x2, mod_p[1], GT_M, gn(1, 1), tm=tm,
                  prec="bf16", glu=True, vmem_mb=48, **pk)
    comb, hb = _router(x3, gn(1, 2), mod_p[1], SC_F, SH_F, wr_pad, tm=tm, n_experts=n_experts,
                       **pk)

    sk = dict(per_token=True, tiles_per_batch=1)
    xs = x_sample.reshape(bs, d)
    proj_s = _inproj(xs, gn(0, 0), mod_s[0], SC_M, SH_M, w0r, tm=bs, bn=640, prec="f32", **sk)
    s_t = jnp.transpose(state_gla[:, 0], (1, 2, 3, 0))
    cbuf_t = jnp.transpose(state_conv[:, 0], (1, 2, 0))
    o_t, y_t, sn_t, cn_t = _gla_conv_step(
        proj_s.T, wgu_pad.T, b_gate_up[0].reshape(hk, 1), g_head_norm[0].reshape(dv, 1),
        w_conv[0].reshape(3, cw, 1), s_t, cbuf_t, heads=heads, dk=dk, dv=dv, cw=cw)
    mixed_s = jnp.concatenate([o_t, y_t], axis=0).T
    gla_s = jnp.transpose(sn_t, (3, 0, 1, 2))
    conv_s = jnp.transpose(cn_t, (2, 0, 1))
    x1s = _outproj(mixed_s, w_out0[0], xs, mod_s[0], GT_M, gn(0, 1), tm=bs, prec="f32", glu=False,
                   **sk)
    x2s = _ffn(x1s, gn(0, 2), mod_s[0], SC_F, SH_F, GT_F, gn(0, 3), w_ffn_gate[0], w_ffn_up[0],
               w_ffn_down[0], tm=bs, fc=256, prec="f32", **sk)
    us = _inproj(x2s, gn(1, 0), mod_s[1], SC_M, SH_M, w_in1[0], tm=bs, bn=256, prec="f32", **sk)
    ygs, re_s, im_s = _s5_step(us, bb_blk, c_blk, ab_re, ab_im, d_blk,
                               state_s5_re[:, 0].reshape(bs, g_s5 * p_s5),
                               state_s5_im[:, 0].reshape(bs, g_s5 * p_s5))
    x3s = _outproj(ygs, w_glu[0], x2s, mod_s[1], GT_M, gn(1, 1), tm=bs, prec="f32", glu=True,
                   vmem_mb=48, **sk)
    comb_s, hb_s = _router(x3s, gn(1, 2), mod_s[1], SC_F, SH_F, wr_pad, tm=bs, n_experts=n_experts,
                           **sk)

    assert bs <= MOE_TB
    n_blk = bsz * t // MOE_TB + 1
    n_tiles_max = -(-(n_blk * _moe_block_rows(n_experts)) // MOE_TM) + n_experts
    pad_rows = lambda a: jnp.concatenate([a, jnp.zeros((MOE_TB - bs, a.shape[1]), a.dtype)], axis=0)
    hb_s, comb_s = pad_rows(hb_s), pad_rows(comb_s)
    seg, gap_pos, gap_len, tile_expert, n_tiles = _moe_schedule((comb, comb_s), n_experts,
                                                                n_tiles_max)
    dest, xs_sorted = _moe_sort(seg, gap_pos, gap_len, hb, comb, hb_s, comb_s,
                                n_tiles_max * MOE_TM, n_experts=n_experts)
    ys_sorted = _experts(tile_expert, n_tiles, xs_sorted, w_eg, w_eu, w_ed, fc=1792)
    x4 = _moe_combine(*seg, ys_sorted, dest, x3, mod_p[1], GT_F, gn(1, 3), blk0=0,
                      n_experts=n_experts, per_token=False, tiles_per_batch=t // MOE_TB)
    x4s = _moe_combine(*seg, ys_sorted, dest, pad_rows(x3s), pad_rows(mod_s[1]), GT_F,
                       gn(1, 3), blk0=n_blk - 1, n_experts=n_experts, per_token=True,
                       tiles_per_batch=1)[:bs]

    return (x4.reshape(bsz, t, d), x4s.reshape(bs, 1, d),
            gla_p[:, None], cst[:, None],
            re_p.reshape(bsz, 1, g_s5, p_s5), im_p.reshape(bsz, 1, g_s5, p_s5),
            gla_s[:, None], conv_s[:, None],
            re_s.reshape(bs, 1, g_s5, p_s5), im_s.reshape(bs, 1, g_s5, p_s5))
```

```python
import functools
import math

import jax
import jax.numpy as jnp
from jax import lax
from jax.experimental import pallas as pl
from jax.experimental.pallas import tpu as pltpu

F32 = jnp.float32
BF16 = jnp.bfloat16
EPS = 1e-6
LANES = 128
GLA_CHUNK = 64
GATE_NORMALIZER = 16.0
N_MOD = 6
S5_GROUP = 16
S5_CHUNK_GROUPS = 8


def _dot(a, b, dims=None):
    if dims is None:
        return jnp.dot(a, b, preferred_element_type=F32)
    return lax.dot_general(a, b, (dims, ((), ())), preferred_element_type=F32)


def _split(a, parts=2):
    rem = a.astype(F32)
    out = []
    for _ in range(parts - 1):
        piece = rem.astype(BF16)
        out.append(piece)
        rem = rem - piece.astype(F32)
    out.append(rem.astype(BF16))
    return out


def _mm(a, w, prec):
    if prec == "bf16":
        return _dot(a.astype(BF16), w.astype(BF16))
    a1, a2, a3 = _split(a, 3)
    w1, w2, w3 = _split(w, 3)
    small = (_dot(a1, w3) + _dot(a3, w1)) + _dot(a2, w2)
    return _dot(a1, w1) + ((_dot(a1, w2) + _dot(a2, w1)) + small)


def _silu(x):
    return x * jax.nn.sigmoid(x)


def _gelu_tanh(x):
    return 0.5 * x * (1.0 + jnp.tanh(math.sqrt(2.0 / math.pi) * (x + 0.044715 * (x * x * x))))


def _log_sigmoid(x):
    return -(jnp.maximum(-x, 0.0) + jnp.log1p(jnp.exp(-jnp.abs(x))))


def _rms(x):
    return x * lax.rsqrt(jnp.mean(x * x, axis=-1, keepdims=True) + EPS)


def _mod_row(ref):
    return ref[0] if len(ref.shape) == 3 else ref[...]


def _normmod(x, g, sc, sh):
    return (_rms(x) * g) * (1.0 + sc) + sh


def _mod_spec(per_token, tm, d, col, tiles_per_batch):
    if per_token:
        return pl.BlockSpec((tm, d), lambda i, *_: (i, col))
    return pl.BlockSpec((1, 1, d), lambda i, *_: (i // tiles_per_batch, 0, col))


def _params(sem, vmem_mb=None):
    kw = dict(dimension_semantics=sem)
    if vmem_mb is not None:
        kw["vmem_limit_bytes"] = vmem_mb << 20
    return pltpu.CompilerParams(**kw)


def _ada_kernel(c_ref, w_ref, b_ref, o_ref):
    o_ref[0] = _mm(_silu(c_ref[...]), w_ref[0], "f32") + b_ref[0]


def _ada(c_all, w_ada, b_ada):
    depth, d, n6 = w_ada.shape
    rows = c_all.shape[0]
    return pl.pallas_call(
        _ada_kernel,
        grid=(depth, n6 // d),
        in_specs=[pl.BlockSpec((rows, d), lambda l, j: (0, 0)),
                  pl.BlockSpec((1, d, d), lambda l, j: (l, 0, j)),
                  pl.BlockSpec((1, 1, d), lambda l, j: (l, 0, j))],
        out_specs=pl.BlockSpec((1, rows, d), lambda l, j: (l, 0, j)),
        out_shape=jax.ShapeDtypeStruct((depth, rows, n6), F32),
        compiler_params=_params(("parallel", "parallel")),
    )(c_all, w_ada, b_ada.reshape(depth, 1, n6))


BF16_ROWS = 16


def _passenger_specs(ws, n_steps, step):
    specs = []
    for w in ws:
        rows, cols = w.shape
        units = rows // BF16_ROWS
        n_used = max(k for k in range(1, n_steps + 1) if units % k == 0)
        specs.append(pl.BlockSpec(
            (rows // n_used, cols),
            lambda *idx, n_used=n_used: (jnp.minimum(step(*idx), n_used - 1), 0)))
    return specs


def _cast_passengers(in_refs, out_refs):
    for src, dst in zip(in_refs, out_refs):
        dst[...] = src[...].astype(dst.dtype)


def _inproj_kernel(*refs, prec, n_cast):
    x_ref, g_ref, sc_ref, sh_ref, w_ref = refs[:5]
    o_ref = refs[5 + n_cast]
    _cast_passengers(refs[5:5 + n_cast], refs[6 + n_cast:])
    h = _normmod(x_ref[...], g_ref[...], _mod_row(sc_ref), _mod_row(sh_ref))
    o_ref[...] = _mm(h, w_ref[...], prec).astype(o_ref.dtype)


def _inproj(x2d, g, mod, sc_col, sh_col, w, *, tm, bn, prec, per_token, tiles_per_batch,
            out_dtype=F32, vmem_mb=None, cast_ws=()):
    n_tok, d = x2d.shape
    n_out = w.shape[1]
    n_j = n_out // bn
    p_specs = _passenger_specs(cast_ws, n_tok // tm * n_j, lambda i, j: i * n_j + j)
    out = pl.pallas_call(
        functools.partial(_inproj_kernel, prec=prec, n_cast=len(cast_ws)),
        grid=(n_tok // tm, n_j),
        in_specs=[pl.BlockSpec((tm, d), lambda i, j: (i, 0)),
                  pl.BlockSpec((1, d), lambda i, j: (0, 0)),
                  _mod_spec(per_token, tm, d, sc_col, tiles_per_batch),
                  _mod_spec(per_token, tm, d, sh_col, tiles_per_batch),
                  pl.BlockSpec((d, bn), lambda i, j: (0, j))] + p_specs,
        out_specs=[pl.BlockSpec((tm, bn), lambda i, j: (i, j))] + p_specs,
        out_shape=[jax.ShapeDtypeStruct((n_tok, n_out), out_dtype)]
        + [jax.ShapeDtypeStruct(c.shape, BF16) for c in cast_ws],
        compiler_params=_params(("arbitrary", "arbitrary"), vmem_mb),
    )(x2d, g, mod, mod, w, *cast_ws)
    return out if cast_ws else out[0]


def _outproj_kernel(a_ref, w_ref, x_ref, gt_ref, g_ref, o_ref, *, prec, glu):
    z = _mm(a_ref[...], w_ref[...], prec)
    if glu:
        d = z.shape[1] // 2
        z = z[:, :d] * jax.nn.sigmoid(z[:, d:])
    o_ref[...] = x_ref[...] + _mod_row(gt_ref) * (_rms(z) * g_ref[...])


def _outproj(a2d, w, x2d, mod, gt_col, g, *, tm, prec, glu, per_token, tiles_per_batch,
             vmem_mb=None):
    n_tok, d = x2d.shape
    k, n_out = w.shape
    return pl.pallas_call(
        functools.partial(_outproj_kernel, prec=prec, glu=glu),
        grid=(n_tok // tm,),
        in_specs=[pl.BlockSpec((tm, k), lambda i: (i, 0)),
                  pl.BlockSpec((k, n_out), lambda i: (0, 0)),
                  pl.BlockSpec((tm, d), lambda i: (i, 0)),
                  _mod_spec(per_token, tm, d, gt_col, tiles_per_batch),
                  pl.BlockSpec((1, d), lambda i: (0, 0))],
        out_specs=pl.BlockSpec((tm, d), lambda i: (i, 0)),
        out_shape=jax.ShapeDtypeStruct((n_tok, d), F32),
        compiler_params=_params(("parallel",), vmem_mb),
    )(a2d, w, x2d, mod, g)


def _ffn_kernel(x_ref, g1_ref, sc_ref, sh_ref, wg_ref, wu_ref, wd_ref, gt_ref, g2_ref,
                o_ref, h_scr, acc_scr, *, prec, n_c):
    c = pl.program_id(1)

    @pl.when(c == 0)
    def _():
        h = _normmod(x_ref[...], g1_ref[...], _mod_row(sc_ref), _mod_row(sh_ref))
        h_scr[...] = h.astype(h_scr.dtype)
        acc_scr[...] = jnp.zeros_like(acc_scr)

    h = h_scr[...]
    act = _silu(_mm(h, wg_ref[...], prec)) * _mm(h, wu_ref[...], prec)
    acc_scr[...] += _mm(act, wd_ref[...], prec)

    @pl.when(c == n_c - 1)
    def _():
        o_ref[...] = x_ref[...] + _mod_row(gt_ref) * (_rms(acc_scr[...]) * g2_ref[...])


def _ffn(x2d, g1, mod, sc_col, sh_col, gt_col, g2, wg, wu, wd, *, tm, fc, prec,
         per_token, tiles_per_batch, vmem_mb=None):
    n_tok, d = x2d.shape
    n_c = wg.shape[1] // fc
    tok = pl.BlockSpec((tm, d), lambda i, c: (i, 0))
    vec = pl.BlockSpec((1, d), lambda i, c: (0, 0))
    h_dtype = BF16 if prec == "bf16" else F32
    w_mode = dict(pipeline_mode=pl.Buffered(1)) if n_c == 1 else {}
    return pl.pallas_call(
        functools.partial(_ffn_kernel, prec=prec, n_c=n_c),
        grid=(n_tok // tm, n_c),
        in_specs=[tok, vec,
                  _mod_spec(per_token, tm, d, sc_col, tiles_per_batch),
                  _mod_spec(per_token, tm, d, sh_col, tiles_per_batch),
                  pl.BlockSpec((d, fc), lambda i, c: (0, c), **w_mode),
                  pl.BlockSpec((d, fc), lambda i, c: (0, c), **w_mode),
                  pl.BlockSpec((fc, d), lambda i, c: (c, 0), **w_mode),
                  _mod_spec(per_token, tm, d, gt_col, tiles_per_batch), vec],
        out_specs=tok,
        out_shape=jax.ShapeDtypeStruct((n_tok, d), F32),
        scratch_shapes=[pltpu.VMEM((tm, d), h_dtype), pltpu.VMEM((tm, d), F32)],
        compiler_params=_params(("parallel", "arbitrary"), vmem_mb),
    )(x2d, g1, mod, mod, wg, wu, wd, mod, g2)


def _router_kernel(x_ref, g_ref, sc_ref, sh_ref, wr_ref, comb_ref, hb_ref, *, n_experts):
    h = _normmod(x_ref[...], g_ref[...], _mod_row(sc_ref), _mod_row(sh_ref))
    hb_ref[...] = h.astype(hb_ref.dtype)
    h1, h2, h3 = _split(h, 3)
    w = wr_ref[...]
    p1, p2, p3 = _dot(h1, w), _dot(h2, w), _dot(h3, w)
    back = lambda x, k: jnp.concatenate([x[:, k * n_experts:], x[:, :k * n_experts]], axis=1)
    small = (back(p1, 2) + p3) + back(p2, 1)
    logits = p1 + ((back(p1, 1) + p2) + small)
    lane = lax.broadcasted_iota(jnp.int32, logits.shape, 1).astype(F32)
    neg = -jnp.inf
    l1 = jnp.where(lane < n_experts, logits, neg)
    m1 = jnp.max(l1, axis=-1, keepdims=True)
    i1 = jnp.min(jnp.where(l1 == m1, lane, float(LANES)), axis=-1, keepdims=True)
    l2 = jnp.where(lane == i1, neg, l1)
    m2 = jnp.max(l2, axis=-1, keepdims=True)
    i2 = jnp.min(jnp.where(l2 == m2, lane, float(LANES)), axis=-1, keepdims=True)
    e2 = jnp.exp(m2 - m1)
    den = 1.0 + e2
    comb_ref[...] = jnp.where(lane == i1, 1.0 / den, 0.0) + jnp.where(lane == i2, e2 / den, 0.0)


def _router(x2d, g, mod, sc_col, sh_col, wr_pad, *, tm, n_experts, per_token, tiles_per_batch):
    n_tok, d = x2d.shape
    return pl.pallas_call(
        functools.partial(_router_kernel, n_experts=n_experts),
        grid=(n_tok // tm,),
        in_specs=[pl.BlockSpec((tm, d), lambda i: (i, 0)),
                  pl.BlockSpec((1, d), lambda i: (0, 0)),
                  _mod_spec(per_token, tm, d, sc_col, tiles_per_batch),
                  _mod_spec(per_token, tm, d, sh_col, tiles_per_batch),
                  pl.BlockSpec((d, LANES), lambda i: (0, 0))],
        out_specs=[pl.BlockSpec((tm, LANES), lambda i: (i, 0)),
                   pl.BlockSpec((tm, d), lambda i: (i, 0))],
        out_shape=[jax.ShapeDtypeStruct((n_tok, LANES), F32),
                   jax.ShapeDtypeStruct((n_tok, d), BF16)],
        compiler_params=_params(("parallel",)),
    )(x2d, g, mod, mod, wr_pad)


MOE_TB = 256
MOE_SEG = 16
MOE_TM = 512


def _moe_block_rows(n_experts):
    return -(-(2 * MOE_TB + n_experts * (MOE_SEG - 1)) // LANES) * LANES


def _routing_slots(comb, dest):
    lane = lax.broadcasted_iota(jnp.int32, comb.shape, 1).astype(F32)
    sel = comb > 0.0
    i_lo = jnp.min(jnp.where(sel, lane, float(LANES)), axis=-1, keepdims=True)
    i_hi = jnp.max(jnp.where(sel, lane, -1.0), axis=-1, keepdims=True)
    pick = lambda i, v: jnp.sum(jnp.where(lane == i, v, 0.0), axis=-1, keepdims=True)
    has_lo = i_lo < float(LANES)
    has_hi = i_hi > i_lo
    d_lo = jnp.where(has_lo, pick(i_lo, dest), -1.0)
    d_hi = jnp.where(has_hi, pick(i_hi, dest), -1.0)
    w_lo = pick(i_lo, comb)
    w_hi = jnp.where(has_hi, pick(i_hi, comb), 0.0)
    out = jnp.where(lane == 0.0, d_lo, 0.0)
    for k, v in ((1.0, d_hi), (2.0, w_lo), (3.0, w_hi)):
        out = jnp.where(lane == k, v, out)
    return out


def _one_hot_rows(row, n_rows):
    tb = row.shape[0]
    lane = lax.broadcasted_iota(jnp.int32, (1, LANES), 1).astype(F32)
    row = jnp.broadcast_to(row, (tb, LANES))
    blocks = [jnp.where(row == lane + float(cb * LANES), 1.0, 0.0) for cb in range(n_rows // LANES)]
    return jnp.concatenate(blocks, axis=1).astype(BF16)


_SEG_BITS = (256, 128, 64, 32, 16)


def _segment_copies(make_copy, blk, pos_ref, off_ref, cnt_ref, n_experts, start):
    for e in range(n_experts):
        seg = blk * n_experts + e
        in_blk, in_sorted, n = off_ref[seg], pos_ref[seg], cnt_ref[seg]
        for size in _SEG_BITS:
            has = (n & size) != 0

            @pl.when(has)
            def _(in_blk=in_blk, in_sorted=in_sorted, size=size):
                cp = make_copy(pl.multiple_of(in_blk, MOE_SEG), pl.multiple_of(in_sorted, MOE_SEG), size)
                cp.start() if start else cp.wait()

            step = jnp.where(has, size, 0)
            in_blk, in_sorted = in_blk + step, in_sorted + step


def _moe_sort_kernel(pos_ref, off_ref, cnt_ref, gpos_ref, glen_ref, hbp_ref, combp_ref, hbs_ref,
                     combs_ref, dest_ref, xs_hbm, buf, zbuf, sems, *, n_rows, n_experts, n_blk, n_blk_p):
    b = pl.program_id(0)
    slot = b % 2

    def writes(blk, slot, start):
        def make_copy(block_row, sorted_row, size):
            return pltpu.make_async_copy(buf.at[slot, pl.ds(block_row, size)],
                                         xs_hbm.at[pl.ds(sorted_row, size)], sems.at[slot])
        _segment_copies(make_copy, blk, pos_ref, off_ref, cnt_ref, n_experts, start)

    def zero_gaps(start):
        for e in range(n_experts):
            row, n = gpos_ref[e], glen_ref[e]
            for size in _SEG_BITS:
                has = (n & size) != 0

                @pl.when(has)
                def _(row=row, size=size):
                    cp = pltpu.make_async_copy(
                        zbuf.at[pl.ds(0, size)],
                        xs_hbm.at[pl.ds(pl.multiple_of(row, MOE_SEG), size)], sems.at[2])
                    cp.start() if start else cp.wait()

                row = row + jnp.where(has, size, 0)
        tail_row, tail_n, big = gpos_ref[n_experts], glen_ref[n_experts], _SEG_BITS[0]

        def piece(i, carry):
            cp = pltpu.make_async_copy(
                zbuf.at[pl.ds(0, big)],
                xs_hbm.at[pl.ds(pl.multiple_of(tail_row + i * big, MOE_SEG), big)], sems.at[2])
            cp.start() if start else cp.wait()
            return carry

        lax.fori_loop(0, tail_n // big, piece, 0)

    @pl.when(b == 0)
    def _():
        zbuf[...] = jnp.zeros_like(zbuf)
        zero_gaps(True)

    @pl.when(b >= 2)
    def _():
        writes(b - 2, slot, False)

    in_prompt = b < n_blk_p
    comb = jnp.where(in_prompt, combp_ref[...], combs_ref[...])
    hb = jnp.where(in_prompt, hbp_ref[...], hbs_ref[...])
    tb = comb.shape[0]
    sel = jnp.where(comb > 0.0, 1.0, 0.0)
    r = lax.broadcasted_iota(jnp.int32, (tb, tb), 0)
    c = lax.broadcasted_iota(jnp.int32, (tb, tb), 1)
    rank = _dot(jnp.where(c < r, 1.0, 0.0).astype(BF16), sel.astype(BF16))
    cnt = jnp.sum(sel, axis=0, keepdims=True)
    cnt = jnp.floor((cnt + (MOE_SEG - 1)) / MOE_SEG) * MOE_SEG
    ru = lax.broadcasted_iota(jnp.int32, (LANES, LANES), 0)
    cu = lax.broadcasted_iota(jnp.int32, (LANES, LANES), 1)
    off = _dot(jnp.broadcast_to(cnt, (8, LANES)).astype(BF16),
               jnp.where(ru < cu, 1.0, 0.0).astype(BF16))[0:1]
    slots = _routing_slots(comb, off + rank)
    dest_ref[...] = slots
    pt = _one_hot_rows(slots[:, 0:1], n_rows) + _one_hot_rows(slots[:, 1:2], n_rows)
    buf[slot] = _dot(pt, hb, ((0,), (0,))).astype(buf.dtype)
    writes(b, slot, True)

    @pl.when(b == n_blk - 1)
    def _():
        writes(b, slot, False)
        if n_blk > 1:
            writes(b - 1, 1 - slot, False)
        zero_gaps(False)


def _moe_sort(seg, gap_pos, gap_len, hb_p, comb_p, hb_s, comb_s, n_sorted_rows, *, n_experts):
    n_p, d = hb_p.shape
    n_blk_p = n_p // MOE_TB
    n_blk = n_blk_p + 1
    n_rows = _moe_block_rows(n_experts)
    last_p = n_blk_p - 1
    return pl.pallas_call(
        functools.partial(_moe_sort_kernel, n_rows=n_rows, n_experts=n_experts, n_blk=n_blk,
                          n_blk_p=n_blk_p),
        grid_spec=pltpu.PrefetchScalarGridSpec(
            num_scalar_prefetch=5, grid=(n_blk,),
            in_specs=[pl.BlockSpec((MOE_TB, d), lambda b, *_: (jnp.minimum(b, last_p), 0)),
                      pl.BlockSpec((MOE_TB, LANES), lambda b, *_: (jnp.minimum(b, last_p), 0)),
                      pl.BlockSpec((MOE_TB, d), lambda b, *_: (0, 0)),
                      pl.BlockSpec((MOE_TB, LANES), lambda b, *_: (0, 0))],
            out_specs=[pl.BlockSpec((MOE_TB, LANES), lambda b, *_: (b, 0)),
                       pl.BlockSpec(memory_space=pl.ANY)],
            scratch_shapes=[pltpu.VMEM((2, n_rows, d), BF16), pltpu.VMEM((_SEG_BITS[0], d), BF16),
                            pltpu.SemaphoreType.DMA((3,))]),
        out_shape=[jax.ShapeDtypeStruct((n_blk * MOE_TB, LANES), F32),
                   jax.ShapeDtypeStruct((n_sorted_rows, d), BF16)],
        compiler_params=_params(("arbitrary",)),
    )(*seg, gap_pos, gap_len, hb_p, comb_p, hb_s, comb_s)


def _experts_kernel(te_ref, nt_ref, x_ref, wg_ref, wu_ref, wd_ref, o_ref, acc_scr, *, n_c):
    del te_ref
    k = pl.program_id(0)
    c = pl.program_id(1)
    active = k < nt_ref[0]

    @pl.when(active)
    def _():
        h = x_ref[...]
        act = _silu(_dot(h, wg_ref[0])) * _dot(h, wu_ref[0])
        y = _dot(act.astype(BF16), wd_ref[0])

        @pl.when(c == 0)
        def _():
            acc_scr[...] = y

        @pl.when(c > 0)
        def _():
            acc_scr[...] += y

        @pl.when(c == n_c - 1)
        def _():
            o_ref[...] = acc_scr[...].astype(o_ref.dtype)

    @pl.when(jnp.logical_not(active) & (c == n_c - 1))
    def _():
        o_ref[...] = jnp.zeros_like(o_ref)


def _experts(tile_expert, n_tiles, xs, wg, wu, wd, *, fc):
    n_rows, d = xs.shape
    n_c = wg.shape[2] // fc

    def row_map(k, c, te, nt):
        return (jnp.minimum(k, nt[0] - 1), 0)

    def w_map(k, c, te, nt):
        return (te[k], 0, jnp.where(k < nt[0], c, n_c - 1))

    def wd_map(k, c, te, nt):
        return (te[k], jnp.where(k < nt[0], c, n_c - 1), 0)

    return pl.pallas_call(
        functools.partial(_experts_kernel, n_c=n_c),
        grid_spec=pltpu.PrefetchScalarGridSpec(
            num_scalar_prefetch=2, grid=(n_rows // MOE_TM, n_c),
            in_specs=[pl.BlockSpec((MOE_TM, d), row_map),
                      pl.BlockSpec((1, d, fc), w_map), pl.BlockSpec((1, d, fc), w_map),
                      pl.BlockSpec((1, fc, d), wd_map)],
            out_specs=pl.BlockSpec((MOE_TM, d), lambda k, c, te, nt: (k, 0)),
            scratch_shapes=[pltpu.VMEM((MOE_TM, d), F32)]),
        out_shape=jax.ShapeDtypeStruct((n_rows, d), BF16),
        compiler_params=_params(("arbitrary", "arbitrary"), 48),
    )(tile_expert, n_tiles, xs, wg, wu, wd)


def _moe_combine_kernel(pos_ref, off_ref, cnt_ref, ys_hbm, dest_ref, x_ref, gt_ref, g_ref,
                        o_ref, buf, sems, *, n_experts, n_blk, blk0):
    b = pl.program_id(0)
    slot = b % 2

    def reads(blk, slot, start):
        def make_copy(block_row, sorted_row, size):
            return pltpu.make_async_copy(ys_hbm.at[pl.ds(sorted_row, size)],
                                         buf.at[slot, pl.ds(block_row, size)], sems.at[slot])
        _segment_copies(make_copy, blk0 + blk, pos_ref, off_ref, cnt_ref, n_experts, start)

    @pl.when(b == 0)
    def _():
        buf[...] = jnp.zeros_like(buf)
        reads(b, slot, True)

    if n_blk > 1:
        @pl.when(b + 1 < n_blk)
        def _():
            reads(b + 1, 1 - slot, True)

    reads(b, slot, False)
    yb = buf[slot]
    slots = dest_ref[...]
    n_rows = yb.shape[0]
    y = (slots[:, 2:3] * _dot(_one_hot_rows(slots[:, 0:1], n_rows), yb)
         + slots[:, 3:4] * _dot(_one_hot_rows(slots[:, 1:2], n_rows), yb))
    o_ref[...] = x_ref[...] + _mod_row(gt_ref) * (_rms(y) * g_ref[...])


def _moe_combine(sorted_pos, block_off, seg_cnt, ys, dest, x2d, mod, gt_col, g, *, blk0,
                 n_experts, per_token, tiles_per_batch):
    n_tok, d = x2d.shape
    n_blk = n_tok // MOE_TB
    n_rows = _moe_block_rows(n_experts)
    return pl.pallas_call(
        functools.partial(_moe_combine_kernel, n_experts=n_experts, n_blk=n_blk, blk0=blk0),
        grid_spec=pltpu.PrefetchScalarGridSpec(
            num_scalar_prefetch=3, grid=(n_blk,),
            in_specs=[pl.BlockSpec(memory_space=pl.ANY),
                      pl.BlockSpec((MOE_TB, LANES), lambda b, *_: (blk0 + b, 0)),
                      pl.BlockSpec((MOE_TB, d), lambda b, *_: (b, 0)),
                      _mod_spec(per_token, MOE_TB, d, gt_col, tiles_per_batch),
                      pl.BlockSpec((1, d), lambda b, *_: (0, 0))],
            out_specs=pl.BlockSpec((MOE_TB, d), lambda b, *_: (b, 0)),
            scratch_shapes=[pltpu.VMEM((2, n_rows, d), BF16), pltpu.SemaphoreType.DMA((2,))]),
        out_shape=jax.ShapeDtypeStruct((n_tok, d), F32),
        compiler_params=_params(("arbitrary",)),
    )(sorted_pos, block_off, seg_cnt, ys, dest, x2d, mod, g)


def _moe_schedule(combs, n_experts, n_tiles_max):
    def block_counts(comb):
        n_blk = comb.shape[0] // MOE_TB
        sel = (comb[:, :n_experts] > 0.0).astype(jnp.int32)
        return jnp.sum(sel.reshape(n_blk, MOE_TB, n_experts), axis=1)

    cnt = jnp.concatenate([block_counts(c) for c in combs], axis=0)
    cnt = -(-cnt // MOE_SEG) * MOE_SEG
    off = jnp.cumsum(cnt, axis=1) - cnt
    tot = jnp.sum(cnt, axis=0)
    grp = -(-tot // MOE_TM) * MOE_TM
    g_end = jnp.cumsum(grp)
    sorted_pos = (g_end - grp)[None, :] + (jnp.cumsum(cnt, axis=0) - cnt)
    n_tiles = g_end[-1] // MOE_TM
    first_row = jnp.arange(n_tiles_max, dtype=jnp.int32) * MOE_TM
    tile_expert = jnp.sum(first_row[:, None] >= g_end[None, :], axis=1)
    last = jnp.sum((n_tiles - 1) * MOE_TM >= g_end)
    tile_expert = jnp.minimum(tile_expert, last).astype(jnp.int32)
    flat = lambda a: a.reshape(-1).astype(jnp.int32)
    seg = (flat(sorted_pos), flat(off), flat(cnt))
    gap_pos = flat(jnp.concatenate([g_end - grp + tot, g_end[-1:]]))
    gap_len = flat(jnp.concatenate([grp - tot, n_tiles_max * MOE_TM - g_end[-1:]]))
    return seg, gap_pos, gap_len, tile_expert, n_tiles.reshape(1).astype(jnp.int32)


def _gla_conv_kernel(p_ref, wgu_ref, bgu_ref, gh_ref, wc_ref, mixed_ref, st_ref, cst_ref,
                     s_scr, uext_scr, *, tt, n_t, heads, dk, dv, cw):
    ti = pl.program_id(1)
    hk = heads * dk
    hv = heads * dv
    o_q, o_k, o_v, o_og = 0, hk, 2 * hk, 2 * hk + hv
    o_ch = o_og + hv
    o_cb, o_cc, o_lr = o_ch + cw, o_ch + 2 * cw, o_ch + 3 * cw
    L = GLA_CHUNK

    @pl.when(ti == 0)
    def _():
        s_scr[...] = jnp.zeros_like(s_scr)
        uext_scr[0:8, :] = jnp.zeros((8, cw), F32)

    logit = _dot(p_ref[:, o_lr:o_lr + LANES].astype(BF16), wgu_ref[...]) + bgu_ref[...]
    logg = _log_sigmoid(logit) / GATE_NORMALIZER

    r = lax.broadcasted_iota(jnp.int32, (tt, tt), 0)
    c = lax.broadcasted_iota(jnp.int32, (tt, tt), 1)
    tri = jnp.where((r // L == c // L) & (c <= r), 1.0, 0.0).astype(BF16)
    g_hi = logg.astype(BF16)
    rem = logg - g_hi.astype(F32)
    g_mid = rem.astype(BF16)
    g_lo = (rem - g_mid.astype(F32)).astype(BF16)
    bc = _dot(tri, g_hi) + (_dot(tri, g_mid) + _dot(tri, g_lo))

    rl = lax.broadcasted_iota(jnp.int32, (L, L), 0)
    cl = lax.broadcasted_iota(jnp.int32, (L, L), 1)
    tril = cl <= rl
    lane = lax.broadcasted_iota(jnp.int32, (L, LANES), 1)
    heads_per_blk = LANES // dk
    nt_dims = ((1,), (1,))

    for ck in range(tt // L):
        rows = slice(L * ck, L * ck + L)
        b = bc[rows]
        bl = b[L - 1:L]
        q_in = (p_ref[rows, o_q:o_q + hk] * (dk ** -0.5)) * jnp.exp(b)
        k = p_ref[rows, o_k:o_k + hk]
        k_out = (k * jnp.exp(-b)).astype(BF16)
        k_dec = (k * jnp.exp(bl - b)).astype(BF16)
        dec = jnp.exp(bl)
        q_in = q_in.astype(BF16)
        for h in range(heads):
            blk = slice(LANES * (h // heads_per_blk), LANES * (h // heads_per_blk) + LANES)
            in_head = (lane // dk) == (h % heads_per_blk)
            qm = jnp.where(in_head, q_in[:, blk], jnp.zeros_like(q_in[:, blk]))
            att = _dot(qm, k_out[:, blk], nt_dims)
            att = jnp.where(tril, att, 0.0).astype(BF16)
            vh = p_ref[rows, o_v + dv * h:o_v + dv * h + dv].astype(BF16)
            s_t = s_scr[h]
            o = _dot(att, vh) + _dot(qm, s_t.astype(BF16), nt_dims)
            ds_t = _dot(vh, k_dec[:, blk], ((0,), (0,)))
            s_scr[h] = s_t * dec[:, blk] + ds_t
            og = p_ref[rows, o_og + dv * h:o_og + dv * h + dv]
            res = (_rms(o) * gh_ref[...]) * _silu(og)
            mixed_ref[rows, dv * h:dv * h + dv] = res.astype(mixed_ref.dtype)

    u = p_ref[:, o_cc:o_cc + cw] * p_ref[:, o_ch:o_ch + cw]
    uext_scr[8:8 + tt, :] = u
    y = (wc_ref[0:1, :] * uext_scr[6:6 + tt, :] + wc_ref[1:2, :] * uext_scr[7:7 + tt, :]
         + wc_ref[2:3, :] * u)
    mixed_ref[:, hv:hv + cw] = (p_ref[:, o_cb:o_cb + cw] * y).astype(mixed_ref.dtype)
    tail = uext_scr[tt + 6:tt + 8, :]
    uext_scr[6:8, :] = tail

    @pl.when(ti == n_t - 1)
    def _():
        st_ref[0] = s_scr[...]
        cst_ref[0] = tail


def _gla_conv_prompt(proj, wgu_pad, bgu, g_head, w_conv, *, bsz, t, tt, heads, dk, dv, cw):
    n_t = t // tt
    n_in = proj.shape[1]
    width = heads * dv + cw
    return pl.pallas_call(
        functools.partial(_gla_conv_kernel, tt=tt, n_t=n_t, heads=heads, dk=dk, dv=dv, cw=cw),
        grid=(bsz, n_t),
        in_specs=[pl.BlockSpec((tt, n_in), lambda b, i: (b * n_t + i, 0)),
                  pl.BlockSpec(wgu_pad.shape, lambda b, i: (0, 0)),
                  pl.BlockSpec(bgu.shape, lambda b, i: (0, 0)),
                  pl.BlockSpec(g_head.shape, lambda b, i: (0, 0)),
                  pl.BlockSpec(w_conv.shape, lambda b, i: (0, 0))],
        out_specs=[pl.BlockSpec((tt, width), lambda b, i: (b * n_t + i, 0)),
                   pl.BlockSpec((1, heads, dv, LANES), lambda b, i: (b, 0, 0, 0)),
                   pl.BlockSpec((1, 2, cw), lambda b, i: (b, 0, 0))],
        out_shape=[jax.ShapeDtypeStruct((bsz * t, width), BF16),
                   jax.ShapeDtypeStruct((bsz, heads, dv, LANES), F32),
                   jax.ShapeDtypeStruct((bsz, 2, cw), F32)],
        scratch_shapes=[pltpu.VMEM((heads, dv, LANES), F32), pltpu.VMEM((tt + 8, cw), F32)],
        compiler_params=_params(("parallel", "arbitrary"), 48),
    )(proj, wgu_pad, bgu, g_head, w_conv)


def _gla_conv_step_kernel(q_ref, k_ref, v_ref, og_ref, ch_ref, cb_ref, cc_ref, lr_ref,
                          wgu_ref, bgu_ref, gh_ref, wc_ref, s_ref, cbuf_ref,
                          o_ref, y_ref, sn_ref, cn_ref, a_scr, *, dk):
    logit = _mm(wgu_ref[...], lr_ref[...], "f32") + bgu_ref[...]
    a_scr[...] = jnp.exp(_log_sigmoid(logit) / GATE_NORMALIZER)
    v_t = v_ref[...]

    def body(d, acc):
        a = a_scr[pl.ds(d, 1), :]
        kd = k_ref[pl.ds(d, 1), :]
        qd = q_ref[pl.ds(d, 1), :]
        s_new = a * s_ref[0, d] + kd * v_t
        sn_ref[0, d] = s_new
        return acc + (qd * (dk ** -0.5)) * s_new

    o = lax.fori_loop(0, dk, body, jnp.zeros(v_t.shape, F32))
    o = o * lax.rsqrt(jnp.mean(o * o, axis=0, keepdims=True) + EPS) * gh_ref[...]
    o_ref[...] = o * _silu(og_ref[...])
    u = cc_ref[...] * ch_ref[...]
    y = wc_ref[0] * cbuf_ref[0] + wc_ref[1] * cbuf_ref[1] + wc_ref[2] * u
    y_ref[...] = cb_ref[...] * y
    cn_ref[0] = cbuf_ref[1]
    cn_ref[1] = u


def _gla_conv_step(proj_t, wgu_t_pad, bgu_col, gh_col, wc_col, s_t, cbuf_t, *, heads, dk, dv, cw):
    bsz = proj_t.shape[1]
    hk, hv = heads * dk, heads * dv
    cs = cw // heads
    assert dv == LANES and cs == LANES and dk * 2 == LANES
    o_k, o_v, o_og = hk, 2 * hk, 2 * hk + hv
    o_ch = o_og + hv
    o_cb, o_cc, o_lr = o_ch + cw, o_ch + 2 * cw, o_ch + 3 * cw
    row = lambda off, size: (lambda h: (off // size + h, 0))
    blk = lambda size, off: pl.BlockSpec((size, bsz), row(off, size))
    return pl.pallas_call(
        functools.partial(_gla_conv_step_kernel, dk=dk),
        grid=(heads,),
        in_specs=[blk(dk, 0), blk(dk, o_k), blk(dv, o_v), blk(dv, o_og),
                  blk(cs, o_ch), blk(cs, o_cb), blk(cs, o_cc),
                  pl.BlockSpec((LANES, bsz), lambda h: (o_lr // LANES, 0)),
                  pl.BlockSpec((dk, LANES), lambda h: (h, 0)),
                  pl.BlockSpec((dk, 1), lambda h: (h, 0)),
                  pl.BlockSpec((dv, 1), lambda h: (0, 0)),
                  pl.BlockSpec((3, cs, 1), lambda h: (0, h, 0)),
                  pl.BlockSpec((1, dk, dv, bsz), lambda h: (h, 0, 0, 0)),
                  pl.BlockSpec((2, cs, bsz), lambda h: (0, h, 0))],
        out_specs=[pl.BlockSpec((dv, bsz), lambda h: (h, 0)),
                   pl.BlockSpec((cs, bsz), lambda h: (h, 0)),
                   pl.BlockSpec((1, dk, dv, bsz), lambda h: (h, 0, 0, 0)),
                   pl.BlockSpec((2, cs, bsz), lambda h: (0, h, 0))],
        out_shape=[jax.ShapeDtypeStruct((hv, bsz), F32),
                   jax.ShapeDtypeStruct((cw, bsz), F32),
                   jax.ShapeDtypeStruct(s_t.shape, F32),
                   jax.ShapeDtypeStruct(cbuf_t.shape, F32)],
        scratch_shapes=[pltpu.VMEM((dk, bsz), F32)],
        compiler_params=_params(("parallel",)),
    )(proj_t, proj_t, proj_t, proj_t, proj_t, proj_t, proj_t, proj_t,
      wgu_t_pad, bgu_col, gh_col, wc_col, s_t, cbuf_t)


def _s5_scan_kernel(*refs, tt, n_t, nb, half, n_cast):
    u_ref, bb_ref, cm_ref, abr_ref, abi_ref, d_ref = refs[:6]
    y_ref, sre_ref, sim_ref = refs[6 + n_cast:9 + n_cast]
    scr, tb_scr, xr_scr, xi_scr = refs[9 + 2 * n_cast:]
    _cast_passengers(refs[6:6 + n_cast], refs[9 + n_cast:9 + 2 * n_cast])
    ti = pl.program_id(1)
    n_blk = half // LANES

    @pl.when(ti == 0)
    def _():
        xr_scr[...] = jnp.zeros_like(xr_scr)
        xi_scr[...] = jnp.zeros_like(xi_scr)

    for b in range(nb):
        tb_scr[pl.ds(b, tt, stride=nb), :] = u_ref[b]
    for k in range(nb):
        rows = slice(k * tt, (k + 1) * tt)
        bu = _dot(tb_scr[rows, :].astype(BF16), bb_ref[0])
        for cb in range(2 * n_blk):
            scr[cb, rows, :] = bu[:, cb * LANES:(cb + 1) * LANES]
    a_re = [jnp.broadcast_to(abr_ref[0][:, cb * LANES:(cb + 1) * LANES], (nb, LANES))
            for cb in range(n_blk)]
    a_im = [jnp.broadcast_to(abi_ref[0][:, cb * LANES:(cb + 1) * LANES], (nb, LANES))
            for cb in range(n_blk)]

    def body(t, carry):
        rows = pl.ds(pl.multiple_of(t * nb, nb), nb)
        out = []
        for cb in range(n_blk):
            xr, xi = carry[2 * cb], carry[2 * cb + 1]
            nr = (a_re[cb] * xr - a_im[cb] * xi) + scr[cb, rows, :]
            ni = (a_re[cb] * xi + a_im[cb] * xr) + scr[n_blk + cb, rows, :]
            scr[cb, rows, :] = nr
            scr[n_blk + cb, rows, :] = ni
            out += [nr, ni]
        return tuple(out)

    init = []
    for cb in range(n_blk):
        init += [xr_scr[:, cb * LANES:(cb + 1) * LANES], xi_scr[:, cb * LANES:(cb + 1) * LANES]]
    fin = lax.fori_loop(0, tt, body, tuple(init), unroll=4)
    xr = jnp.concatenate([fin[2 * cb] for cb in range(n_blk)], axis=1)
    xi = jnp.concatenate([fin[2 * cb + 1] for cb in range(n_blk)], axis=1)
    xr_scr[...] = xr
    xi_scr[...] = xi
    for k in range(nb):
        rows = slice(k * tt, (k + 1) * tt)
        xs = jnp.concatenate([scr[cb, rows, :] for cb in range(2 * n_blk)], axis=1)
        y = _dot(xs.astype(BF16), cm_ref[0]) + d_ref[0] * tb_scr[rows, :]
        tb_scr[rows, :] = _gelu_tanh(y)
    for b in range(nb):
        y_ref[b] = tb_scr[pl.ds(b, tt, stride=nb), :].astype(y_ref.dtype)

    @pl.when(ti == n_t - 1)
    def _():
        sre_ref[...] = xr
        sim_ref[...] = xi


def _s5_prompt(u3d, bb_blk, c_blk, ab_re, ab_im, d_blk, cast_ws, *, tt):
    nb, t, d = u3d.shape
    n_j = d // LANES
    n_t = t // tt
    half = bb_blk.shape[2] // 2
    w_specs = _passenger_specs(cast_ws, n_j * n_t, lambda j, i: j * n_t + i)
    return pl.pallas_call(
        functools.partial(_s5_scan_kernel, tt=tt, n_t=n_t, nb=nb, half=half, n_cast=len(cast_ws)),
        grid=(n_j, n_t),
        in_specs=[pl.BlockSpec((nb, tt, LANES), lambda j, i: (0, i, j)),
                  pl.BlockSpec((1, LANES, 2 * half), lambda j, i: (j, 0, 0)),
                  pl.BlockSpec((1, 2 * half, LANES), lambda j, i: (j, 0, 0)),
                  pl.BlockSpec((1, 1, half), lambda j, i: (j, 0, 0)),
                  pl.BlockSpec((1, 1, half), lambda j, i: (j, 0, 0)),
                  pl.BlockSpec((1, 1, LANES), lambda j, i: (j, 0, 0))] + w_specs,
        out_specs=[pl.BlockSpec((nb, tt, LANES), lambda j, i: (0, i, j)),
                   pl.BlockSpec((nb, half), lambda j, i: (0, j)),
                   pl.BlockSpec((nb, half), lambda j, i: (0, j))] + w_specs,
        out_shape=[jax.ShapeDtypeStruct((nb, t, d), BF16),
                   jax.ShapeDtypeStruct((nb, n_j * half), F32),
                   jax.ShapeDtypeStruct((nb, n_j * half), F32)]
        + [jax.ShapeDtypeStruct(w.shape, BF16) for w in cast_ws],
        scratch_shapes=[pltpu.VMEM((2 * half // LANES, nb * tt, LANES), F32),
                        pltpu.VMEM((nb * tt, LANES), F32),
                        pltpu.VMEM((nb, half), F32), pltpu.VMEM((nb, half), F32)],
        compiler_params=_params(("arbitrary", "arbitrary"), 48),
    )(u3d, bb_blk, c_blk, ab_re, ab_im, d_blk, *cast_ws)


def _s5_step_kernel(u_ref, bb_ref, cm_ref, abr_ref, abi_ref, d_ref, sre_ref, sim_ref,
                    y_ref, nre_ref, nim_ref, *, half):
    u = u_ref[...]
    bu = _mm(u, bb_ref[0], "f32")
    ar, ai = abr_ref[0], abi_ref[0]
    sr, si = sre_ref[...], sim_ref[...]
    nr = (ar * sr - ai * si) + bu[:, 0:half]
    ni = (ar * si + ai * sr) + bu[:, half:2 * half]
    nre_ref[...] = nr
    nim_ref[...] = ni
    y = _mm(jnp.concatenate([nr, ni], axis=1), cm_ref[0], "f32") + d_ref[0] * u
    y_ref[...] = _gelu_tanh(y)


def _s5_step(u2d, bb_blk, c_blk, ab_re, ab_im, d_blk, s_re, s_im):
    bsz, d = u2d.shape
    n_j = d // LANES
    half = bb_blk.shape[2] // 2
    st = pl.BlockSpec((bsz, half), lambda j: (0, j))
    return pl.pallas_call(
        functools.partial(_s5_step_kernel, half=half),
        grid=(n_j,),
        in_specs=[pl.BlockSpec((bsz, LANES), lambda j: (0, j)),
                  pl.BlockSpec((1, LANES, 2 * half), lambda j: (j, 0, 0)),
                  pl.BlockSpec((1, 2 * half, LANES), lambda j: (j, 0, 0)),
                  pl.BlockSpec((1, 1, half), lambda j: (j, 0, 0)),
                  pl.BlockSpec((1, 1, half), lambda j: (j, 0, 0)),
                  pl.BlockSpec((1, 1, LANES), lambda j: (j, 0, 0)),
                  st, st],
        out_specs=[pl.BlockSpec((bsz, LANES), lambda j: (0, j)), st, st],
        out_shape=[jax.ShapeDtypeStruct((bsz, d), F32),
                   jax.ShapeDtypeStruct(s_re.shape, F32),
                   jax.ShapeDtypeStruct(s_im.shape, F32)],
        compiler_params=_params(("parallel",)),
    )(u2d, bb_blk, c_blk, ab_re, ab_im, d_blk, s_re, s_im)


def _s5_zoh(a_re, a_im, log_dt, b_re, b_im):
    dt = jnp.exp(log_dt)[:, None]
    mag = jnp.exp(dt * a_re)
    ab_re, ab_im = mag * jnp.cos(dt * a_im), mag * jnp.sin(dt * a_im)
    den = a_re * a_re + a_im * a_im
    nr, ni = ab_re - 1.0, ab_im
    f_re = (nr * a_re + ni * a_im) / den
    f_im = (ni * a_re - nr * a_im) / den
    bb_re = f_re[..., None] * b_re - f_im[..., None] * b_im
    bb_im = f_re[..., None] * b_im + f_im[..., None] * b_re
    return ab_re, ab_im, bb_re, bb_im


def _s5_step_blocks(ab_re, ab_im, bb_re, bb_im, c_re, c_im, d_skip):
    g, p = ab_re.shape
    cg = S5_CHUNK_GROUPS
    n_j = g // cg
    eye = jnp.eye(cg, dtype=F32)

    def in_blk(m):
        m = m.reshape(n_j, cg, p, S5_GROUP)
        return jnp.einsum("jgpi,gh->jgihp", m, eye).reshape(n_j, cg * S5_GROUP, cg * p)

    def out_blk(m):
        m = m.reshape(n_j, cg, S5_GROUP, p)
        return jnp.einsum("jgip,gh->jgphi", m, eye).reshape(n_j, cg * p, cg * S5_GROUP)

    bb_blk = jnp.concatenate([in_blk(bb_re), in_blk(bb_im)], axis=2)
    c_blk = jnp.concatenate([out_blk(c_re), out_blk(-c_im)], axis=1)
    return (bb_blk, c_blk, ab_re.reshape(n_j, 1, cg * p), ab_im.reshape(n_j, 1, cg * p),
            d_skip.reshape(n_j, 1, cg * S5_GROUP))


def kernel(x_prompt, x_sample, c_prompt, c_sample, state_gla, state_conv, state_s5_re, state_s5_im, w_ada, b_ada, g_norm, w_in0, w_gate_up, b_gate_up, g_head_norm, w_conv, w_out0, w_ffn_gate, w_ffn_up, w_ffn_down, w_in1, s5_a_re, s5_a_im, s5_log_dt, s5_b_re, s5_b_im, s5_c_re, s5_c_im, s5_d, w_glu, w_router, w_exp_gate, w_exp_up, w_exp_down):
    bsz, t, d = x_prompt.shape
    bs = x_sample.shape[0]
    heads, dk, dv = state_gla.shape[2:]
    cw = state_conv.shape[3]
    lowrank = w_gate_up.shape[1]
    hk, hv = heads * dk, heads * dv
    n_experts = w_router.shape[2]
    g_s5, p_s5 = s5_a_re.shape[1:]

    mod = _ada(jnp.concatenate([c_prompt, c_sample], axis=0), w_ada, b_ada)
    mod_p = [mod[l, :bsz].reshape(bsz, 1, N_MOD * d) for l in range(2)]
    mod_s = [mod[l, bsz:] for l in range(2)]
    SH_M, SC_M, GT_M, SH_F, SC_F, GT_F = range(N_MOD)
    gn = lambda l, k: g_norm[l, k].reshape(1, d)

    w0 = w_in0[0]
    o_lr = 2 * hk + 2 * hv
    w0r = jnp.concatenate([w0[:, :o_lr], w0[:, o_lr + lowrank:], w0[:, o_lr:o_lr + lowrank],
                           jnp.zeros((d, LANES - lowrank), F32)], axis=1)
    wgu_pad = jnp.concatenate([w_gate_up[0], jnp.zeros((LANES - lowrank, hk), F32)], axis=0)
    zoh = _s5_zoh(s5_a_re[0], s5_a_im[0], s5_log_dt[0], s5_b_re[0], s5_b_im[0])
    bb_blk, c_blk, ab_re, ab_im, d_blk = _s5_step_blocks(*zoh, s5_c_re[0], s5_c_im[0], s5_d[0])
    to_bf16_grid = lambda a: lax.reduce_precision(a, exponent_bits=8, mantissa_bits=7)
    wr_1 = to_bf16_grid(w_router[0])
    wr_2 = to_bf16_grid(w_router[0] - wr_1)
    wr_3 = to_bf16_grid((w_router[0] - wr_1) - wr_2)
    wr_pad = jnp.concatenate([wr_1, wr_2, wr_3, jnp.zeros((d, LANES - 3 * n_experts), F32)],
                             axis=1).astype(BF16)
    bf = lambda a: a.astype(BF16)

    tm = 512
    tpb = t // tm
    pk = dict(per_token=False, tiles_per_batch=tpb)
    xp = x_prompt.reshape(bsz * t, d)
    proj, w_o0, w_fg, w_fu, w_fd, w_i1, w_gl = _inproj(
        xp, gn(0, 0), mod_p[0], SC_M, SH_M, bf(w0r), tm=tm, bn=w0r.shape[1], prec="bf16", vmem_mb=48,
        cast_ws=(w_out0[0], w_ffn_gate[0], w_ffn_up[0], w_ffn_down[0], w_in1[0], w_glu[0]), **pk)
    mixed, st_t, cst = _gla_conv_prompt(
        proj, bf(wgu_pad), b_gate_up[0].reshape(1, hk), g_head_norm[0].reshape(1, dv), w_conv[0],
        bsz=bsz, t=t, tt=256, heads=heads, dk=dk, dv=dv, cw=cw)
    st_t = st_t.reshape(bsz, heads // 2, 2, dv, 2, dk)
    gla_p = jnp.stack([st_t[:, :, 0, :, 0, :], st_t[:, :, 1, :, 1, :]], axis=2)
    gla_p = jnp.swapaxes(gla_p.reshape(bsz, heads, dv, dk), 2, 3)
    x1 = _outproj(mixed, w_o0, xp, mod_p[0], GT_M, gn(0, 1), tm=tm, prec="bf16", glu=False, **pk)
    x2 = _ffn(x1, gn(0, 2), mod_p[0], SC_F, SH_F, GT_F, gn(0, 3), w_fg, w_fu, w_fd, tm=tm, fc=2816,
              prec="bf16", vmem_mb=56, **pk)
    u = _inproj(x2, gn(1, 0), mod_p[1], SC_M, SH_M, w_i1, tm=tm, bn=d, prec="bf16", **pk)
    ew = (w_exp_gate[0], w_exp_up[0], w_exp_down[0])
    yg, re_p, im_p, *ew_bf = _s5_prompt(
        u.reshape(bsz, t, d), bf(bb_blk), bf(c_blk), ab_re, ab_im, d_blk,
        tuple(w.reshape(-1, w.shape[2]) for w in ew), tt=256)
    w_eg, w_eu, w_ed = (wb.reshape(w.shape) for wb, w in zip(ew_bf, ew))
    x3 = _outproj(yg.reshape(bsz * t, d), w_gl, x2, mod_p[1], GT_M, gn(1, 1), tm=tm,
                  prec="bf16", glu=True, vmem_mb=48, **pk)
    comb, hb = _router(x3, gn(1, 2), mod_p[1], SC_F, SH_F, wr_pad, tm=tm, n_experts=n_experts,
                       **pk)

    sk = dict(per_token=True, tiles_per_batch=1)
    xs = x_sample.reshape(bs, d)
    proj_s = _inproj(xs, gn(0, 0), mod_s[0], SC_M, SH_M, w0r, tm=bs, bn=640, prec="f32", **sk)
    s_t = jnp.transpose(state_gla[:, 0], (1, 2, 3, 0))
    cbuf_t = jnp.transpose(state_conv[:, 0], (1, 2, 0))
    o_t, y_t, sn_t, cn_t = _gla_conv_step(
        proj_s.T, wgu_pad.T, b_gate_up[0].reshape(hk, 1), g_head_norm[0].reshape(dv, 1),
        w_conv[0].reshape(3, cw, 1), s_t, cbuf_t, heads=heads, dk=dk, dv=dv, cw=cw)
    mixed_s = jnp.concatenate([o_t, y_t], axis=0).T
    gla_s = jnp.transpose(sn_t, (3, 0, 1, 2))
    conv_s = jnp.transpose(cn_t, (2, 0, 1))
    x1s = _outproj(mixed_s, w_out0[0], xs, mod_s[0], GT_M, gn(0, 1), tm=bs, prec="f32", glu=False,
                   **sk)
    x2s = _ffn(x1s, gn(0, 2), mod_s[0], SC_F, SH_F, GT_F, gn(0, 3), w_ffn_gate[0], w_ffn_up[0],
               w_ffn_down[0], tm=bs, fc=256, prec="f32", **sk)
    us = _inproj(x2s, gn(1, 0), mod_s[1], SC_M, SH_M, w_in1[0], tm=bs, bn=256, prec="f32", **sk)
    ygs, re_s, im_s = _s5_step(us, bb_blk, c_blk, ab_re, ab_im, d_blk,
                               state_s5_re[:, 0].reshape(bs, g_s5 * p_s5),
                               state_s5_im[:, 0].reshape(bs, g_s5 * p_s5))
    x3s = _outproj(ygs, w_glu[0], x2s, mod_s[1], GT_M, gn(1, 1), tm=bs, prec="f32", glu=True,
                   vmem_mb=48, **sk)
    comb_s, hb_s = _router(x3s, gn(1, 2), mod_s[1], SC_F, SH_F, wr_pad, tm=bs, n_experts=n_experts,
                           **sk)

    assert bs <= MOE_TB
    n_blk = bsz * t // MOE_TB + 1
    n_tiles_max = -(-(n_blk * _moe_block_rows(n_experts)) // MOE_TM) + n_experts
    pad_rows = lambda a: jnp.concatenate([a, jnp.zeros((MOE_TB - bs, a.shape[1]), a.dtype)], axis=0)
    hb_s, comb_s = pad_rows(hb_s), pad_rows(comb_s)
    seg, gap_pos, gap_len, tile_expert, n_tiles = _moe_schedule((comb, comb_s), n_experts,
                                                                n_tiles_max)
    dest, xs_sorted = _moe_sort(seg, gap_pos, gap_len, hb, comb, hb_s, comb_s,
                                n_tiles_max * MOE_TM, n_experts=n_experts)
    ys_sorted = _experts(tile_expert, n_tiles, xs_sorted, w_eg, w_eu, w_ed, fc=1792)
    x4 = _moe_combine(*seg, ys_sorted, dest, x3, mod_p[1], GT_F, gn(1, 3), blk0=0,
                      n_experts=n_experts, per_token=False, tiles_per_batch=t // MOE_TB)
    x4s = _moe_combine(*seg, ys_sorted, dest, pad_rows(x3s), pad_rows(mod_s[1]), GT_F,
                       gn(1, 3), blk0=n_blk - 1, n_experts=n_experts, per_token=True,
                       tiles_per_batch=1)[:bs]

    return (x4.reshape(bsz, t, d), x4s.reshape(bs, 1, d),
            gla_p[:, None], cst[:, None],
            re_p.reshape(bsz, 1, g_s5, p_s5), im_p.reshape(bsz, 1, g_s5, p_s5),
            gla_s[:, None], conv_s[:, None],
            re_s.reshape(bs, 1, g_s5, p_s5), im_s.reshape(bs, 1, g_s5, p_s5))
```

```python
import functools
import math

import jax
import jax.numpy as jnp
from jax import lax
from jax.experimental import pallas as pl
from jax.experimental.pallas import tpu as pltpu

F32 = jnp.float32
BF16 = jnp.bfloat16
EPS = 1e-6
LANES = 128
GLA_CHUNK = 64
GATE_NORMALIZER = 16.0
N_MOD = 6
S5_GROUP = 16
S5_CHUNK_GROUPS = 8

TOKEN_TILE = 512
SEQ_TILE = 256
VMEM_LIMIT_MB = 48
VMEM_LIMIT_FFN_MB = 56
EXPERT_FF_CHUNK = 1792
SAMPLE_PROJ_COLS = 640
SAMPLE_COLS = 256


def _dot(a, b, dims=None):
    if dims is None:
        return jnp.dot(a, b, preferred_element_type=F32)
    return lax.dot_general(a, b, (dims, ((), ())), preferred_element_type=F32)


def _split(a, parts=2):
    rem = a.astype(F32)
    out = []
    for _ in range(parts - 1):
        piece = rem.astype(BF16)
        out.append(piece)
        rem = rem - piece.astype(F32)
    out.append(rem.astype(BF16))
    return out


def _mm(a, w, prec):
    if prec == "bf16":
        return _dot(a.astype(BF16), w.astype(BF16))
    a1, a2, a3 = _split(a, 3)
    w1, w2, w3 = _split(w, 3)
    small = (_dot(a1, w3) + _dot(a3, w1)) + _dot(a2, w2)
    return _dot(a1, w1) + ((_dot(a1, w2) + _dot(a2, w1)) + small)


def _silu(x):
    return x * jax.nn.sigmoid(x)


def _gelu_tanh(x):
    return 0.5 * x * (1.0 + jnp.tanh(math.sqrt(2.0 / math.pi) * (x + 0.044715 * (x * x * x))))


def _log_sigmoid(x):
    return -(jnp.maximum(-x, 0.0) + jnp.log1p(jnp.exp(-jnp.abs(x))))


def _rms(x):
    return x * lax.rsqrt(jnp.mean(x * x, axis=-1, keepdims=True) + EPS)


def _mod_row(ref):
    return ref[0] if len(ref.shape) == 3 else ref[...]


def _normmod(x, g, sc, sh):
    return (_rms(x) * g) * (1.0 + sc) + sh


def _mod_spec(per_token, tm, d, col, tiles_per_batch):
    if per_token:
        return pl.BlockSpec((tm, d), lambda i, *_: (i, col))
    return pl.BlockSpec((1, 1, d), lambda i, *_: (i // tiles_per_batch, 0, col))


def _params(sem, vmem_mb=None):
    kw = dict(dimension_semantics=sem)
    if vmem_mb is not None:
        kw["vmem_limit_bytes"] = vmem_mb << 20
    return pltpu.CompilerParams(**kw)


def _ada_kernel(c_ref, w_ref, b_ref, o_ref):
    o_ref[0] = _mm(_silu(c_ref[...]), w_ref[0], "f32") + b_ref[0]


def _ada(c_all, w_ada, b_ada):
    depth, d, n6 = w_ada.shape
    rows = c_all.shape[0]
    return pl.pallas_call(
        _ada_kernel,
        grid=(depth, n6 // d),
        in_specs=[pl.BlockSpec((rows, d), lambda l, j: (0, 0)),
                  pl.BlockSpec((1, d, d), lambda l, j: (l, 0, j)),
                  pl.BlockSpec((1, 1, d), lambda l, j: (l, 0, j))],
        out_specs=pl.BlockSpec((1, rows, d), lambda l, j: (l, 0, j)),
        out_shape=jax.ShapeDtypeStruct((depth, rows, n6), F32),
        compiler_params=_params(("parallel", "parallel")),
    )(c_all, w_ada, b_ada.reshape(depth, 1, n6))


BF16_ROWS = 16


def _passenger_specs(ws, n_steps, step):
    specs = []
    for w in ws:
        rows, cols = w.shape
        units = rows // BF16_ROWS
        n_used = max(k for k in range(1, n_steps + 1) if units % k == 0)
        specs.append(pl.BlockSpec(
            (rows // n_used, cols),
            lambda *idx, n_used=n_used: (jnp.minimum(step(*idx), n_used - 1), 0)))
    return specs


def _cast_passengers(in_refs, out_refs):
    for src, dst in zip(in_refs, out_refs):
        dst[...] = src[...].astype(dst.dtype)


def _inproj_kernel(*refs, prec, n_cast):
    x_ref, g_ref, sc_ref, sh_ref, w_ref = refs[:5]
    o_ref = refs[5 + n_cast]
    _cast_passengers(refs[5:5 + n_cast], refs[6 + n_cast:])
    h = _normmod(x_ref[...], g_ref[...], _mod_row(sc_ref), _mod_row(sh_ref))
    o_ref[...] = _mm(h, w_ref[...], prec).astype(o_ref.dtype)


def _inproj(x2d, g, mod, sc_col, sh_col, w, *, tm, bn, prec, per_token, tiles_per_batch,
            out_dtype=F32, vmem_mb=None, cast_ws=()):
    n_tok, d = x2d.shape
    n_out = w.shape[1]
    n_j = n_out // bn
    p_specs = _passenger_specs(cast_ws, n_tok // tm * n_j, lambda i, j: i * n_j + j)
    out = pl.pallas_call(
        functools.partial(_inproj_kernel, prec=prec, n_cast=len(cast_ws)),
        grid=(n_tok // tm, n_j),
        in_specs=[pl.BlockSpec((tm, d), lambda i, j: (i, 0)),
                  pl.BlockSpec((1, d), lambda i, j: (0, 0)),
                  _mod_spec(per_token, tm, d, sc_col, tiles_per_batch),
                  _mod_spec(per_token, tm, d, sh_col, tiles_per_batch),
                  pl.BlockSpec((d, bn), lambda i, j: (0, j))] + p_specs,
        out_specs=[pl.BlockSpec((tm, bn), lambda i, j: (i, j))] + p_specs,
        out_shape=[jax.ShapeDtypeStruct((n_tok, n_out), out_dtype)]
        + [jax.ShapeDtypeStruct(c.shape, BF16) for c in cast_ws],
        compiler_params=_params(("arbitrary", "arbitrary"), vmem_mb),
    )(x2d, g, mod, mod, w, *cast_ws)
    return out if cast_ws else out[0]


def _outproj_kernel(a_ref, w_ref, x_ref, gt_ref, g_ref, o_ref, *, prec, glu):
    z = _mm(a_ref[...], w_ref[...], prec)
    if glu:
        d = z.shape[1] // 2
        z = z[:, :d] * jax.nn.sigmoid(z[:, d:])
    o_ref[...] = x_ref[...] + _mod_row(gt_ref) * (_rms(z) * g_ref[...])


def _outproj(a2d, w, x2d, mod, gt_col, g, *, tm, prec, glu, per_token, tiles_per_batch,
             vmem_mb=None):
    n_tok, d = x2d.shape
    k, n_out = w.shape
    return pl.pallas_call(
        functools.partial(_outproj_kernel, prec=prec, glu=glu),
        grid=(n_tok // tm,),
        in_specs=[pl.BlockSpec((tm, k), lambda i: (i, 0)),
                  pl.BlockSpec((k, n_out), lambda i: (0, 0)),
                  pl.BlockSpec((tm, d), lambda i: (i, 0)),
                  _mod_spec(per_token, tm, d, gt_col, tiles_per_batch),
                  pl.BlockSpec((1, d), lambda i: (0, 0))],
        out_specs=pl.BlockSpec((tm, d), lambda i: (i, 0)),
        out_shape=jax.ShapeDtypeStruct((n_tok, d), F32),
        compiler_params=_params(("parallel",), vmem_mb),
    )(a2d, w, x2d, mod, g)


def _ffn_kernel(x_ref, g1_ref, sc_ref, sh_ref, wg_ref, wu_ref, wd_ref, gt_ref, g2_ref,
                o_ref, h_scr, acc_scr, *, prec, n_c):
    c = pl.program_id(1)

    @pl.when(c == 0)
    def _():
        h = _normmod(x_ref[...], g1_ref[...], _mod_row(sc_ref), _mod_row(sh_ref))
        h_scr[...] = h.astype(h_scr.dtype)
        acc_scr[...] = jnp.zeros_like(acc_scr)

    h = h_scr[...]
    act = _silu(_mm(h, wg_ref[...], prec)) * _mm(h, wu_ref[...], prec)
    acc_scr[...] += _mm(act, wd_ref[...], prec)

    @pl.when(c == n_c - 1)
    def _():
        o_ref[...] = x_ref[...] + _mod_row(gt_ref) * (_rms(acc_scr[...]) * g2_ref[...])


def _ffn(x2d, g1, mod, sc_col, sh_col, gt_col, g2, wg, wu, wd, *, tm, fc, prec,
         per_token, tiles_per_batch, vmem_mb=None):
    n_tok, d = x2d.shape
    n_c = wg.shape[1] // fc
    tok = pl.BlockSpec((tm, d), lambda i, c: (i, 0))
    vec = pl.BlockSpec((1, d), lambda i, c: (0, 0))
    h_dtype = BF16 if prec == "bf16" else F32
    w_mode = dict(pipeline_mode=pl.Buffered(1)) if n_c == 1 else {}
    return pl.pallas_call(
        functools.partial(_ffn_kernel, prec=prec, n_c=n_c),
        grid=(n_tok // tm, n_c),
        in_specs=[tok, vec,
                  _mod_spec(per_token, tm, d, sc_col, tiles_per_batch),
                  _mod_spec(per_token, tm, d, sh_col, tiles_per_batch),
                  pl.BlockSpec((d, fc), lambda i, c: (0, c), **w_mode),
                  pl.BlockSpec((d, fc), lambda i, c: (0, c), **w_mode),
                  pl.BlockSpec((fc, d), lambda i, c: (c, 0), **w_mode),
                  _mod_spec(per_token, tm, d, gt_col, tiles_per_batch), vec],
        out_specs=tok,
        out_shape=jax.ShapeDtypeStruct((n_tok, d), F32),
        scratch_shapes=[pltpu.VMEM((tm, d), h_dtype), pltpu.VMEM((tm, d), F32)],
        compiler_params=_params(("parallel", "arbitrary"), vmem_mb),
    )(x2d, g1, mod, mod, wg, wu, wd, mod, g2)


def _router_kernel(x_ref, g_ref, sc_ref, sh_ref, wr_ref, comb_ref, hb_ref, wp_scr, *, n_experts):
    @pl.when(pl.program_id(0) == 0)
    def _():
        w1, w2, w3 = _split(wr_ref[...], 3)
        fwd = lambda x, k: jnp.concatenate([x[:, LANES - k * n_experts:], x[:, :LANES - k * n_experts]],
                                           axis=1)
        packed = w1.astype(F32) + (fwd(w2.astype(F32), 1) + fwd(w3.astype(F32), 2))
        wp_scr[...] = packed.astype(wp_scr.dtype)

    h = _normmod(x_ref[...], g_ref[...], _mod_row(sc_ref), _mod_row(sh_ref))
    hb_ref[...] = h.astype(hb_ref.dtype)
    h1, h2, h3 = _split(h, 3)
    w = wp_scr[...]
    p1, p2, p3 = _dot(h1, w), _dot(h2, w), _dot(h3, w)
    back = lambda x, k: jnp.concatenate([x[:, k * n_experts:], x[:, :k * n_experts]], axis=1)
    small = (back(p1, 2) + p3) + back(p2, 1)
    logits = p1 + ((back(p1, 1) + p2) + small)
    lane = lax.broadcasted_iota(jnp.int32, logits.shape, 1).astype(F32)
    neg = -jnp.inf
    l1 = jnp.where(lane < n_experts, logits, neg)
    m1 = jnp.max(l1, axis=-1, keepdims=True)
    i1 = jnp.min(jnp.where(l1 == m1, lane, float(LANES)), axis=-1, keepdims=True)
    l2 = jnp.where(lane == i1, neg, l1)
    m2 = jnp.max(l2, axis=-1, keepdims=True)
    i2 = jnp.min(jnp.where(l2 == m2, lane, float(LANES)), axis=-1, keepdims=True)
    e2 = jnp.exp(m2 - m1)
    den = 1.0 + e2
    comb_ref[...] = jnp.where(lane == i1, 1.0 / den, 0.0) + jnp.where(lane == i2, e2 / den, 0.0)


def _router(x2d, g, mod, sc_col, sh_col, wr_pad, *, tm, n_experts, per_token, tiles_per_batch):
    n_tok, d = x2d.shape
    return pl.pallas_call(
        functools.partial(_router_kernel, n_experts=n_experts),
        grid=(n_tok // tm,),
        in_specs=[pl.BlockSpec((tm, d), lambda i: (i, 0)),
                  pl.BlockSpec((1, d), lambda i: (0, 0)),
                  _mod_spec(per_token, tm, d, sc_col, tiles_per_batch),
                  _mod_spec(per_token, tm, d, sh_col, tiles_per_batch),
                  pl.BlockSpec((d, LANES), lambda i: (0, 0))],
        out_specs=[pl.BlockSpec((tm, LANES), lambda i: (i, 0)),
                   pl.BlockSpec((tm, d), lambda i: (i, 0))],
        out_shape=[jax.ShapeDtypeStruct((n_tok, LANES), F32),
                   jax.ShapeDtypeStruct((n_tok, d), BF16)],
        scratch_shapes=[pltpu.VMEM((d, LANES), BF16)],
        compiler_params=_params(("arbitrary",)),
    )(x2d, g, mod, mod, wr_pad)


MOE_TB = 256
MOE_SEG = 16
MOE_TM = 512


def _moe_block_rows(n_experts):
    return -(-(2 * MOE_TB + n_experts * (MOE_SEG - 1)) // LANES) * LANES


def _routing_slots(comb, dest):
    lane = lax.broadcasted_iota(jnp.int32, comb.shape, 1).astype(F32)
    sel = comb > 0.0
    i_lo = jnp.min(jnp.where(sel, lane, float(LANES)), axis=-1, keepdims=True)
    i_hi = jnp.max(jnp.where(sel, lane, -1.0), axis=-1, keepdims=True)
    pick = lambda i, v: jnp.sum(jnp.where(lane == i, v, 0.0), axis=-1, keepdims=True)
    has_lo = i_lo < float(LANES)
    has_hi = i_hi > i_lo
    d_lo = jnp.where(has_lo, pick(i_lo, dest), -1.0)
    d_hi = jnp.where(has_hi, pick(i_hi, dest), -1.0)
    w_lo = pick(i_lo, comb)
    w_hi = jnp.where(has_hi, pick(i_hi, comb), 0.0)
    out = jnp.where(lane == 0.0, d_lo, 0.0)
    for k, v in ((1.0, d_hi), (2.0, w_lo), (3.0, w_hi)):
        out = jnp.where(lane == k, v, out)
    return out


def _one_hot_rows(row, n_rows):
    tb = row.shape[0]
    lane = lax.broadcasted_iota(jnp.int32, (1, LANES), 1).astype(F32)
    row = jnp.broadcast_to(row, (tb, LANES))
    blocks = [jnp.where(row == lane + float(cb * LANES), 1.0, 0.0) for cb in range(n_rows // LANES)]
    return jnp.concatenate(blocks, axis=1).astype(BF16)


_SEG_BITS = (256, 128, 64, 32, 16)


def _segment_copies(make_copy, blk, pos_ref, off_ref, cnt_ref, n_experts, start):
    for e in range(n_experts):
        seg = blk * n_experts + e
        in_blk, in_sorted, n = off_ref[seg], pos_ref[seg], cnt_ref[seg]
        for size in _SEG_BITS:
            has = (n & size) != 0

            @pl.when(has)
            def _(in_blk=in_blk, in_sorted=in_sorted, size=size):
                cp = make_copy(pl.multiple_of(in_blk, MOE_SEG), pl.multiple_of(in_sorted, MOE_SEG), size)
                cp.start() if start else cp.wait()

            step = jnp.where(has, size, 0)
            in_blk, in_sorted = in_blk + step, in_sorted + step


def _moe_sort_kernel(pos_ref, off_ref, cnt_ref, gpos_ref, glen_ref, hbp_ref, combp_ref, hbs_ref,
                     combs_ref, dest_ref, xs_hbm, buf, zbuf, sems, *, n_rows, n_experts, n_blk, n_blk_p):
    b = pl.program_id(0)
    slot = b % 2

    def writes(blk, slot, start):
        def make_copy(block_row, sorted_row, size):
            return pltpu.make_async_copy(buf.at[slot, pl.ds(block_row, size)],
                                         xs_hbm.at[pl.ds(sorted_row, size)], sems.at[slot])
        _segment_copies(make_copy, blk, pos_ref, off_ref, cnt_ref, n_experts, start)

    def zero_gaps(start):
        for e in range(n_experts):
            row, n = gpos_ref[e], glen_ref[e]
            for size in _SEG_BITS:
                has = (n & size) != 0

                @pl.when(has)
                def _(row=row, size=size):
                    cp = pltpu.make_async_copy(
                        zbuf.at[pl.ds(0, size)],
                        xs_hbm.at[pl.ds(pl.multiple_of(row, MOE_SEG), size)], sems.at[2])
                    cp.start() if start else cp.wait()

                row = row + jnp.where(has, size, 0)
        tail_row, tail_n, big = gpos_ref[n_experts], glen_ref[n_experts], _SEG_BITS[0]

        def piece(i, carry):
            cp = pltpu.make_async_copy(
                zbuf.at[pl.ds(0, big)],
                xs_hbm.at[pl.ds(pl.multiple_of(tail_row + i * big, MOE_SEG), big)], sems.at[2])
            cp.start() if start else cp.wait()
            return carry

        lax.fori_loop(0, tail_n // big, piece, 0)

    @pl.when(b == 0)
    def _():
        zbuf[...] = jnp.zeros_like(zbuf)
        zero_gaps(True)

    @pl.when(b >= 2)
    def _():
        writes(b - 2, slot, False)

    in_prompt = b < n_blk_p
    comb = jnp.where(in_prompt, combp_ref[...], combs_ref[...])
    hb = jnp.where(in_prompt, hbp_ref[...], hbs_ref[...])
    tb = comb.shape[0]
    sel = jnp.where(comb > 0.0, 1.0, 0.0)
    r = lax.broadcasted_iota(jnp.int32, (tb, tb), 0)
    c = lax.broadcasted_iota(jnp.int32, (tb, tb), 1)
    rank = _dot(jnp.where(c < r, 1.0, 0.0).astype(BF16), sel.astype(BF16))
    cnt = jnp.sum(sel, axis=0, keepdims=True)
    cnt = jnp.floor((cnt + (MOE_SEG - 1)) / MOE_SEG) * MOE_SEG
    ru = lax.broadcasted_iota(jnp.int32, (LANES, LANES), 0)
    cu = lax.broadcasted_iota(jnp.int32, (LANES, LANES), 1)
    off = _dot(jnp.broadcast_to(cnt, (8, LANES)).astype(BF16),
               jnp.where(ru < cu, 1.0, 0.0).astype(BF16))[0:1]
    slots = _routing_slots(comb, off + rank)
    dest_ref[...] = slots
    pt = _one_hot_rows(slots[:, 0:1], n_rows) + _one_hot_rows(slots[:, 1:2], n_rows)
    buf[slot] = _dot(pt, hb, ((0,), (0,))).astype(buf.dtype)
    writes(b, slot, True)

    @pl.when(b == n_blk - 1)
    def _():
        writes(b, slot, False)
        if n_blk > 1:
            writes(b - 1, 1 - slot, False)
        zero_gaps(False)


def _moe_sort(seg, gap_pos, gap_len, hb_p, comb_p, hb_s, comb_s, n_sorted_rows, *, n_experts):
    n_p, d = hb_p.shape
    n_blk_p = n_p // MOE_TB
    n_blk = n_blk_p + 1
    n_rows = _moe_block_rows(n_experts)
    last_p = n_blk_p - 1
    return pl.pallas_call(
        functools.partial(_moe_sort_kernel, n_rows=n_rows, n_experts=n_experts, n_blk=n_blk,
                          n_blk_p=n_blk_p),
        grid_spec=pltpu.PrefetchScalarGridSpec(
            num_scalar_prefetch=5, grid=(n_blk,),
            in_specs=[pl.BlockSpec((MOE_TB, d), lambda b, *_: (jnp.minimum(b, last_p), 0)),
                      pl.BlockSpec((MOE_TB, LANES), lambda b, *_: (jnp.minimum(b, last_p), 0)),
                      pl.BlockSpec((MOE_TB, d), lambda b, *_: (0, 0)),
                      pl.BlockSpec((MOE_TB, LANES), lambda b, *_: (0, 0))],
            out_specs=[pl.BlockSpec((MOE_TB, LANES), lambda b, *_: (b, 0)),
                       pl.BlockSpec(memory_space=pl.ANY)],
            scratch_shapes=[pltpu.VMEM((2, n_rows, d), BF16), pltpu.VMEM((_SEG_BITS[0], d), BF16),
                            pltpu.SemaphoreType.DMA((3,))]),
        out_shape=[jax.ShapeDtypeStruct((n_blk * MOE_TB, LANES), F32),
                   jax.ShapeDtypeStruct((n_sorted_rows, d), BF16)],
        compiler_params=_params(("arbitrary",)),
    )(*seg, gap_pos, gap_len, hb_p, comb_p, hb_s, comb_s)


def _experts_kernel(te_ref, nt_ref, x_ref, wg_ref, wu_ref, wd_ref, o_ref, acc_scr, *, n_c):
    del te_ref
    k = pl.program_id(0)
    c = pl.program_id(1)
    active = k < nt_ref[0]

    @pl.when(active)
    def _():
        h = x_ref[...]
        act = _silu(_dot(h, wg_ref[0])) * _dot(h, wu_ref[0])
        y = _dot(act.astype(BF16), wd_ref[0])

        @pl.when(c == 0)
        def _():
            acc_scr[...] = y

        @pl.when(c > 0)
        def _():
            acc_scr[...] += y

        @pl.when(c == n_c - 1)
        def _():
            o_ref[...] = acc_scr[...].astype(o_ref.dtype)

    @pl.when(jnp.logical_not(active) & (c == n_c - 1))
    def _():
        o_ref[...] = jnp.zeros_like(o_ref)


def _experts(tile_expert, n_tiles, xs, wg, wu, wd, *, fc):
    n_rows, d = xs.shape
    n_c = wg.shape[2] // fc

    def row_map(k, c, te, nt):
        return (jnp.minimum(k, nt[0] - 1), 0)

    def w_map(k, c, te, nt):
        return (te[k], 0, jnp.where(k < nt[0], c, n_c - 1))

    def wd_map(k, c, te, nt):
        return (te[k], jnp.where(k < nt[0], c, n_c - 1), 0)

    return pl.pallas_call(
        functools.partial(_experts_kernel, n_c=n_c),
        grid_spec=pltpu.PrefetchScalarGridSpec(
            num_scalar_prefetch=2, grid=(n_rows // MOE_TM, n_c),
            in_specs=[pl.BlockSpec((MOE_TM, d), row_map),
                      pl.BlockSpec((1, d, fc), w_map), pl.BlockSpec((1, d, fc), w_map),
                      pl.BlockSpec((1, fc, d), wd_map)],
            out_specs=pl.BlockSpec((MOE_TM, d), lambda k, c, te, nt: (k, 0)),
            scratch_shapes=[pltpu.VMEM((MOE_TM, d), F32)]),
        out_shape=jax.ShapeDtypeStruct((n_rows, d), BF16),
        compiler_params=_params(("arbitrary", "arbitrary"), VMEM_LIMIT_MB),
    )(tile_expert, n_tiles, xs, wg, wu, wd)


def _moe_combine_kernel(pos_ref, off_ref, cnt_ref, ys_hbm, dest_ref, x_ref, gt_ref, g_ref,
                        o_ref, buf, sems, *, n_experts, n_blk, blk0):
    b = pl.program_id(0)
    slot = b % 2

    def reads(blk, slot, start):
        def make_copy(block_row, sorted_row, size):
            return pltpu.make_async_copy(ys_hbm.at[pl.ds(sorted_row, size)],
                                         buf.at[slot, pl.ds(block_row, size)], sems.at[slot])
        _segment_copies(make_copy, blk0 + blk, pos_ref, off_ref, cnt_ref, n_experts, start)

    @pl.when(b == 0)
    def _():
        buf[...] = jnp.zeros_like(buf)
        reads(b, slot, True)

    if n_blk > 1:
        @pl.when(b + 1 < n_blk)
        def _():
            reads(b + 1, 1 - slot, True)

    reads(b, slot, False)
    yb = buf[slot]
    slots = dest_ref[...]
    n_rows = yb.shape[0]
    y = (slots[:, 2:3] * _dot(_one_hot_rows(slots[:, 0:1], n_rows), yb)
         + slots[:, 3:4] * _dot(_one_hot_rows(slots[:, 1:2], n_rows), yb))
    o_ref[...] = x_ref[...] + _mod_row(gt_ref) * (_rms(y) * g_ref[...])


def _moe_combine(sorted_pos, block_off, seg_cnt, ys, dest, x2d, mod, gt_col, g, *, blk0,
                 n_experts, per_token, tiles_per_batch):
    n_tok, d = x2d.shape
    n_blk = n_tok // MOE_TB
    n_rows = _moe_block_rows(n_experts)
    return pl.pallas_call(
        functools.partial(_moe_combine_kernel, n_experts=n_experts, n_blk=n_blk, blk0=blk0),
        grid_spec=pltpu.PrefetchScalarGridSpec(
            num_scalar_prefetch=3, grid=(n_blk,),
            in_specs=[pl.BlockSpec(memory_space=pl.ANY),
                      pl.BlockSpec((MOE_TB, LANES), lambda b, *_: (blk0 + b, 0)),
                      pl.BlockSpec((MOE_TB, d), lambda b, *_: (b, 0)),
                      _mod_spec(per_token, MOE_TB, d, gt_col, tiles_per_batch),
                      pl.BlockSpec((1, d), lambda b, *_: (0, 0))],
            out_specs=pl.BlockSpec((MOE_TB, d), lambda b, *_: (b, 0)),
            scratch_shapes=[pltpu.VMEM((2, n_rows, d), BF16), pltpu.SemaphoreType.DMA((2,))]),
        out_shape=jax.ShapeDtypeStruct((n_tok, d), F32),
        compiler_params=_params(("arbitrary",)),
    )(sorted_pos, block_off, seg_cnt, ys, dest, x2d, mod, g)


def _moe_schedule(combs, n_experts, n_tiles_max):
    def block_counts(comb):
        n_blk = comb.shape[0] // MOE_TB
        sel = (comb[:, :n_experts] > 0.0).astype(jnp.int32)
        return jnp.sum(sel.reshape(n_blk, MOE_TB, n_experts), axis=1)

    cnt = jnp.concatenate([block_counts(c) for c in combs], axis=0)
    cnt = -(-cnt // MOE_SEG) * MOE_SEG
    off = jnp.cumsum(cnt, axis=1) - cnt
    tot = jnp.sum(cnt, axis=0)
    grp = -(-tot // MOE_TM) * MOE_TM
    g_end = jnp.cumsum(grp)
    sorted_pos = (g_end - grp)[None, :] + (jnp.cumsum(cnt, axis=0) - cnt)
    n_tiles = g_end[-1] // MOE_TM
    first_row = jnp.arange(n_tiles_max, dtype=jnp.int32) * MOE_TM
    tile_expert = jnp.sum(first_row[:, None] >= g_end[None, :], axis=1)
    last = jnp.sum((n_tiles - 1) * MOE_TM >= g_end)
    tile_expert = jnp.minimum(tile_expert, last).astype(jnp.int32)
    flat = lambda a: a.reshape(-1).astype(jnp.int32)
    seg = (flat(sorted_pos), flat(off), flat(cnt))
    gap_pos = flat(jnp.concatenate([g_end - grp + tot, g_end[-1:]]))
    gap_len = flat(jnp.concatenate([grp - tot, n_tiles_max * MOE_TM - g_end[-1:]]))
    return seg, gap_pos, gap_len, tile_expert, n_tiles.reshape(1).astype(jnp.int32)


def _gla_conv_kernel(p_ref, wgu_ref, bgu_ref, gh_ref, wc_ref, mixed_ref, st_ref, cst_ref,
                     s_scr, uext_scr, *, tt, n_t, heads, dk, dv, cw):
    ti = pl.program_id(1)
    hk = heads * dk
    hv = heads * dv
    o_q, o_k, o_v, o_og = 0, hk, 2 * hk, 2 * hk + hv
    o_ch = o_og + hv
    o_cb, o_cc, o_lr = o_ch + cw, o_ch + 2 * cw, o_ch + 3 * cw
    L = GLA_CHUNK

    @pl.when(ti == 0)
    def _():
        s_scr[...] = jnp.zeros_like(s_scr)
        uext_scr[0:8, :] = jnp.zeros((8, cw), F32)

    logit = _dot(p_ref[:, o_lr:o_lr + LANES].astype(BF16), wgu_ref[...]) + bgu_ref[...]
    logg = _log_sigmoid(logit) / GATE_NORMALIZER

    r = lax.broadcasted_iota(jnp.int32, (tt, tt), 0)
    c = lax.broadcasted_iota(jnp.int32, (tt, tt), 1)
    tri = jnp.where((r // L == c // L) & (c <= r), 1.0, 0.0).astype(BF16)
    g_hi = logg.astype(BF16)
    rem = logg - g_hi.astype(F32)
    g_mid = rem.astype(BF16)
    g_lo = (rem - g_mid.astype(F32)).astype(BF16)
    bc = _dot(tri, g_hi) + (_dot(tri, g_mid) + _dot(tri, g_lo))

    rl = lax.broadcasted_iota(jnp.int32, (L, L), 0)
    cl = lax.broadcasted_iota(jnp.int32, (L, L), 1)
    tril = cl <= rl
    lane = lax.broadcasted_iota(jnp.int32, (L, LANES), 1)
    heads_per_blk = LANES // dk
    nt_dims = ((1,), (1,))

    state = [s_scr[h] for h in range(heads)]
    for ck in range(tt // L):
        rows = slice(L * ck, L * ck + L)
        b = bc[rows]
        bl = b[L - 1:L]
        q_in = (p_ref[rows, o_q:o_q + hk] * (dk ** -0.5)) * jnp.exp(b)
        k = p_ref[rows, o_k:o_k + hk]
        k_out = (k * jnp.exp(-b)).astype(BF16)
        k_dec = (k * jnp.exp(bl - b)).astype(BF16)
        dec = jnp.exp(bl)
        q_in = q_in.astype(BF16)
        for h in range(heads):
            blk = slice(LANES * (h // heads_per_blk), LANES * (h // heads_per_blk) + LANES)
            in_head = (lane // dk) == (h % heads_per_blk)
            qm = jnp.where(in_head, q_in[:, blk], jnp.zeros_like(q_in[:, blk]))
            att = _dot(qm, k_out[:, blk], nt_dims)
            att = jnp.where(tril, att, 0.0).astype(BF16)
            vh = p_ref[rows, o_v + dv * h:o_v + dv * h + dv].astype(BF16)
            o = _dot(att, vh) + _dot(qm, state[h].astype(BF16), nt_dims)
            ds_t = _dot(vh, k_dec[:, blk], ((0,), (0,)))
            state[h] = state[h] * dec[:, blk] + ds_t
            og = p_ref[rows, o_og + dv * h:o_og + dv * h + dv]
            res = (_rms(o) * gh_ref[...]) * _silu(og)
            mixed_ref[rows, dv * h:dv * h + dv] = res.astype(mixed_ref.dtype)
    for h in range(heads):
        s_scr[h] = state[h]

    u = p_ref[:, o_cc:o_cc + cw] * p_ref[:, o_ch:o_ch + cw]
    uext_scr[8:8 + tt, :] = u
    y = (wc_ref[0:1, :] * uext_scr[6:6 + tt, :] + wc_ref[1:2, :] * uext_scr[7:7 + tt, :]
         + wc_ref[2:3, :] * u)
    mixed_ref[:, hv:hv + cw] = (p_ref[:, o_cb:o_cb + cw] * y).astype(mixed_ref.dtype)
    tail = uext_scr[tt + 6:tt + 8, :]
    uext_scr[6:8, :] = tail

    @pl.when(ti == n_t - 1)
    def _():
        st_ref[0] = s_scr[...]
        cst_ref[0] = tail


def _gla_conv_prompt(proj, wgu_pad, bgu, g_head, w_conv, *, bsz, t, tt, heads, dk, dv, cw):
    n_t = t // tt
    n_in = proj.shape[1]
    width = heads * dv + cw
    return pl.pallas_call(
        functools.partial(_gla_conv_kernel, tt=tt, n_t=n_t, heads=heads, dk=dk, dv=dv, cw=cw),
        grid=(bsz, n_t),
        in_specs=[pl.BlockSpec((tt, n_in), lambda b, i: (b * n_t + i, 0)),
                  pl.BlockSpec(wgu_pad.shape, lambda b, i: (0, 0)),
                  pl.BlockSpec(bgu.shape, lambda b, i: (0, 0)),
                  pl.BlockSpec(g_head.shape, lambda b, i: (0, 0)),
                  pl.BlockSpec(w_conv.shape, lambda b, i: (0, 0))],
        out_specs=[pl.BlockSpec((tt, width), lambda b, i: (b * n_t + i, 0)),
                   pl.BlockSpec((1, heads, dv, LANES), lambda b, i: (b, 0, 0, 0)),
                   pl.BlockSpec((1, 2, cw), lambda b, i: (b, 0, 0))],
        out_shape=[jax.ShapeDtypeStruct((bsz * t, width), BF16),
                   jax.ShapeDtypeStruct((bsz, heads, dv, LANES), F32),
                   jax.ShapeDtypeStruct((bsz, 2, cw), F32)],
        scratch_shapes=[pltpu.VMEM((heads, dv, LANES), F32), pltpu.VMEM((tt + 8, cw), F32)],
        compiler_params=_params(("parallel", "arbitrary"), VMEM_LIMIT_MB),
    )(proj, wgu_pad, bgu, g_head, w_conv)


def _gla_conv_step_kernel(q_ref, k_ref, v_ref, og_ref, ch_ref, cb_ref, cc_ref, lr_ref,
                          wgu_ref, bgu_ref, gh_ref, wc_ref, s_ref, cbuf_ref,
                          o_ref, y_ref, sn_ref, cn_ref, a_scr, *, dk):
    logit = _mm(wgu_ref[...], lr_ref[...], "f32") + bgu_ref[...]
    a_scr[...] = jnp.exp(_log_sigmoid(logit) / GATE_NORMALIZER)
    v_t = v_ref[...]

    def body(d, acc):
        a = a_scr[pl.ds(d, 1), :]
        kd = k_ref[pl.ds(d, 1), :]
        qd = q_ref[pl.ds(d, 1), :]
        s_new = a * s_ref[0, d] + kd * v_t
        sn_ref[0, d] = s_new
        return acc + (qd * (dk ** -0.5)) * s_new

    o = lax.fori_loop(0, dk, body, jnp.zeros(v_t.shape, F32))
    o = o * lax.rsqrt(jnp.mean(o * o, axis=0, keepdims=True) + EPS) * gh_ref[...]
    o_ref[...] = o * _silu(og_ref[...])
    u = cc_ref[...] * ch_ref[...]
    y = wc_ref[0] * cbuf_ref[0] + wc_ref[1] * cbuf_ref[1] + wc_ref[2] * u
    y_ref[...] = cb_ref[...] * y
    cn_ref[0] = cbuf_ref[1]
    cn_ref[1] = u


def _gla_conv_step(proj_t, wgu_t_pad, bgu_col, gh_col, wc_col, s_t, cbuf_t, *, heads, dk, dv, cw):
    bsz = proj_t.shape[1]
    hk, hv = heads * dk, heads * dv
    cs = cw // heads
    assert dv == LANES and cs == LANES and dk * 2 == LANES
    o_k, o_v, o_og = hk, 2 * hk, 2 * hk + hv
    o_ch = o_og + hv
    o_cb, o_cc, o_lr = o_ch + cw, o_ch + 2 * cw, o_ch + 3 * cw
    row = lambda off, size: (lambda h: (off // size + h, 0))
    blk = lambda size, off: pl.BlockSpec((size, bsz), row(off, size))
    return pl.pallas_call(
        functools.partial(_gla_conv_step_kernel, dk=dk),
        grid=(heads,),
        in_specs=[blk(dk, 0), blk(dk, o_k), blk(dv, o_v), blk(dv, o_og),
                  blk(cs, o_ch), blk(cs, o_cb), blk(cs, o_cc),
                  pl.BlockSpec((LANES, bsz), lambda h: (o_lr // LANES, 0)),
                  pl.BlockSpec((dk, LANES), lambda h: (h, 0)),
                  pl.BlockSpec((dk, 1), lambda h: (h, 0)),
                  pl.BlockSpec((dv, 1), lambda h: (0, 0)),
                  pl.BlockSpec((3, cs, 1), lambda h: (0, h, 0)),
                  pl.BlockSpec((1, dk, dv, bsz), lambda h: (h, 0, 0, 0)),
                  pl.BlockSpec((2, cs, bsz), lambda h: (0, h, 0))],
        out_specs=[pl.BlockSpec((dv, bsz), lambda h: (h, 0)),
                   pl.BlockSpec((cs, bsz), lambda h: (h, 0)),
                   pl.BlockSpec((1, dk, dv, bsz), lambda h: (h, 0, 0, 0)),
                   pl.BlockSpec((2, cs, bsz), lambda h: (0, h, 0))],
        out_shape=[jax.ShapeDtypeStruct((hv, bsz), F32),
                   jax.ShapeDtypeStruct((cw, bsz), F32),
                   jax.ShapeDtypeStruct(s_t.shape, F32),
                   jax.ShapeDtypeStruct(cbuf_t.shape, F32)],
        scratch_shapes=[pltpu.VMEM((dk, bsz), F32)],
        compiler_params=_params(("parallel",)),
    )(proj_t, proj_t, proj_t, proj_t, proj_t, proj_t, proj_t, proj_t,
      wgu_t_pad, bgu_col, gh_col, wc_col, s_t, cbuf_t)


def _s5_scan_kernel(*refs, tt, n_t, nb, half, n_cast):
    u_ref, bb_ref, cm_ref, abr_ref, abi_ref, d_ref = refs[:6]
    y_ref, sre_ref, sim_ref = refs[6 + n_cast:9 + n_cast]
    scr, tb_scr, xr_scr, xi_scr = refs[9 + 2 * n_cast:]
    _cast_passengers(refs[6:6 + n_cast], refs[9 + n_cast:9 + 2 * n_cast])
    ti = pl.program_id(1)
    n_blk = half // LANES

    @pl.when(ti == 0)
    def _():
        xr_scr[...] = jnp.zeros_like(xr_scr)
        xi_scr[...] = jnp.zeros_like(xi_scr)

    for b in range(nb):
        tb_scr[pl.ds(b, tt, stride=nb), :] = u_ref[b]
    for k in range(nb):
        rows = slice(k * tt, (k + 1) * tt)
        bu = _dot(tb_scr[rows, :].astype(BF16), bb_ref[0])
        for cb in range(2 * n_blk):
            scr[cb, rows, :] = bu[:, cb * LANES:(cb + 1) * LANES]
    a_re = [jnp.broadcast_to(abr_ref[0][:, cb * LANES:(cb + 1) * LANES], (nb, LANES))
            for cb in range(n_blk)]
    a_im = [jnp.broadcast_to(abi_ref[0][:, cb * LANES:(cb + 1) * LANES], (nb, LANES))
            for cb in range(n_blk)]

    def body(t, carry):
        rows = pl.ds(pl.multiple_of(t * nb, nb), nb)
        out = []
        for cb in range(n_blk):
            xr, xi = carry[2 * cb], carry[2 * cb + 1]
            nr = (a_re[cb] * xr - a_im[cb] * xi) + scr[cb, rows, :]
            ni = (a_re[cb] * xi + a_im[cb] * xr) + scr[n_blk + cb, rows, :]
            scr[cb, rows, :] = nr
            scr[n_blk + cb, rows, :] = ni
            out += [nr, ni]
        return tuple(out)

    init = []
    for cb in range(n_blk):
        init += [xr_scr[:, cb * LANES:(cb + 1) * LANES], xi_scr[:, cb * LANES:(cb + 1) * LANES]]
    fin = lax.fori_loop(0, tt, body, tuple(init), unroll=4)
    xr = jnp.concatenate([fin[2 * cb] for cb in range(n_blk)], axis=1)
    xi = jnp.concatenate([fin[2 * cb + 1] for cb in range(n_blk)], axis=1)
    xr_scr[...] = xr
    xi_scr[...] = xi
    for k in range(nb):
        rows = slice(k * tt, (k + 1) * tt)
        xs = jnp.concatenate([scr[cb, rows, :] for cb in range(2 * n_blk)], axis=1)
        y = _dot(xs.astype(BF16), cm_ref[0]) + d_ref[0] * tb_scr[rows, :]
        tb_scr[rows, :] = _gelu_tanh(y)
    for b in range(nb):
        y_ref[b] = tb_scr[pl.ds(b, tt, stride=nb), :].astype(y_ref.dtype)

    @pl.when(ti == n_t - 1)
    def _():
        sre_ref[...] = xr
        sim_ref[...] = xi


def _s5_prompt(u3d, bb_blk, c_blk, ab_re, ab_im, d_blk, cast_ws, *, tt):
    nb, t, d = u3d.shape
    n_j = d // LANES
    n_t = t // tt
    half = bb_blk.shape[2] // 2
    w_specs = _passenger_specs(cast_ws, n_j * n_t, lambda j, i: j * n_t + i)
    return pl.pallas_call(
        functools.partial(_s5_scan_kernel, tt=tt, n_t=n_t, nb=nb, half=half, n_cast=len(cast_ws)),
        grid=(n_j, n_t),
        in_specs=[pl.BlockSpec((nb, tt, LANES), lambda j, i: (0, i, j)),
                  pl.BlockSpec((1, LANES, 2 * half), lambda j, i: (j, 0, 0)),
                  pl.BlockSpec((1, 2 * half, LANES), lambda j, i: (j, 0, 0)),
                  pl.BlockSpec((1, 1, half), lambda j, i: (j, 0, 0)),
                  pl.BlockSpec((1, 1, half), lambda j, i: (j, 0, 0)),
                  pl.BlockSpec((1, 1, LANES), lambda j, i: (j, 0, 0))] + w_specs,
        out_specs=[pl.BlockSpec((nb, tt, LANES), lambda j, i: (0, i, j)),
                   pl.BlockSpec((nb, half), lambda j, i: (0, j)),
                   pl.BlockSpec((nb, half), lambda j, i: (0, j))] + w_specs,
        out_shape=[jax.ShapeDtypeStruct((nb, t, d), BF16),
                   jax.ShapeDtypeStruct((nb, n_j * half), F32),
                   jax.ShapeDtypeStruct((nb, n_j * half), F32)]
        + [jax.ShapeDtypeStruct(w.shape, BF16) for w in cast_ws],
        scratch_shapes=[pltpu.VMEM((2 * half // LANES, nb * tt, LANES), F32),
                        pltpu.VMEM((nb * tt, LANES), F32),
                        pltpu.VMEM((nb, half), F32), pltpu.VMEM((nb, half), F32)],
        compiler_params=_params(("arbitrary", "arbitrary"), VMEM_LIMIT_MB),
    )(u3d, bb_blk, c_blk, ab_re, ab_im, d_blk, *cast_ws)


def _s5_step_kernel(u_ref, bb_ref, cm_ref, abr_ref, abi_ref, d_ref, sre_ref, sim_ref,
                    y_ref, nre_ref, nim_ref, *, half):
    u = u_ref[...]
    bu = _mm(u, bb_ref[0], "f32")
    ar, ai = abr_ref[0], abi_ref[0]
    sr, si = sre_ref[...], sim_ref[...]
    nr = (ar * sr - ai * si) + bu[:, 0:half]
    ni = (ar * si + ai * sr) + bu[:, half:2 * half]
    nre_ref[...] = nr
    nim_ref[...] = ni
    y = _mm(jnp.concatenate([nr, ni], axis=1), cm_ref[0], "f32") + d_ref[0] * u
    y_ref[...] = _gelu_tanh(y)


def _s5_step(u2d, bb_blk, c_blk, ab_re, ab_im, d_blk, s_re, s_im):
    bsz, d = u2d.shape
    n_j = d // LANES
    half = bb_blk.shape[2] // 2
    st = pl.BlockSpec((bsz, half), lambda j: (0, j))
    return pl.pallas_call(
        functools.partial(_s5_step_kernel, half=half),
        grid=(n_j,),
        in_specs=[pl.BlockSpec((bsz, LANES), lambda j: (0, j)),
                  pl.BlockSpec((1, LANES, 2 * half), lambda j: (j, 0, 0)),
                  pl.BlockSpec((1, 2 * half, LANES), lambda j: (j, 0, 0)),
                  pl.BlockSpec((1, 1, half), lambda j: (j, 0, 0)),
                  pl.BlockSpec((1, 1, half), lambda j: (j, 0, 0)),
                  pl.BlockSpec((1, 1, LANES), lambda j: (j, 0, 0)),
                  st, st],
        out_specs=[pl.BlockSpec((bsz, LANES), lambda j: (0, j)), st, st],
        out_shape=[jax.ShapeDtypeStruct((bsz, d), F32),
                   jax.ShapeDtypeStruct(s_re.shape, F32),
                   jax.ShapeDtypeStruct(s_im.shape, F32)],
        compiler_params=_params(("parallel",)),
    )(u2d, bb_blk, c_blk, ab_re, ab_im, d_blk, s_re, s_im)


def _s5_zoh(a_re, a_im, log_dt, b_re, b_im):
    dt = jnp.exp(log_dt)[:, None]
    mag = jnp.exp(dt * a_re)
    ab_re, ab_im = mag * jnp.cos(dt * a_im), mag * jnp.sin(dt * a_im)
    den = a_re * a_re + a_im * a_im
    nr, ni = ab_re - 1.0, ab_im
    f_re = (nr * a_re + ni * a_im) / den
    f_im = (ni * a_re - nr * a_im) / den
    bb_re = f_re[..., None] * b_re - f_im[..., None] * b_im
    bb_im = f_re[..., None] * b_im + f_im[..., None] * b_re
    return ab_re, ab_im, bb_re, bb_im


def _s5_step_blocks(ab_re, ab_im, bb_re, bb_im, c_re, c_im, d_skip):
    g, p = ab_re.shape
    cg = S5_CHUNK_GROUPS
    n_j = g // cg
    eye = jnp.eye(cg, dtype=F32)

    def in_blk(m):
        m = m.reshape(n_j, cg, p, S5_GROUP)
        return jnp.einsum("jgpi,gh->jgihp", m, eye).reshape(n_j, cg * S5_GROUP, cg * p)

    def out_blk(m):
        m = m.reshape(n_j, cg, S5_GROUP, p)
        return jnp.einsum("jgip,gh->jgphi", m, eye).reshape(n_j, cg * p, cg * S5_GROUP)

    bb_blk = jnp.concatenate([in_blk(bb_re), in_blk(bb_im)], axis=2)
    c_blk = jnp.concatenate([out_blk(c_re), out_blk(-c_im)], axis=1)
    return (bb_blk, c_blk, ab_re.reshape(n_j, 1, cg * p), ab_im.reshape(n_j, 1, cg * p),
            d_skip.reshape(n_j, 1, cg * S5_GROUP))


def kernel(x_prompt, x_sample, c_prompt, c_sample, state_gla, state_conv, state_s5_re, state_s5_im, w_ada, b_ada, g_norm, w_in0, w_gate_up, b_gate_up, g_head_norm, w_conv, w_out0, w_ffn_gate, w_ffn_up, w_ffn_down, w_in1, s5_a_re, s5_a_im, s5_log_dt, s5_b_re, s5_b_im, s5_c_re, s5_c_im, s5_d, w_glu, w_router, w_exp_gate, w_exp_up, w_exp_down):
    bsz, t, d = x_prompt.shape
    bs = x_sample.shape[0]
    heads, dk, dv = state_gla.shape[2:]
    cw = state_conv.shape[3]
    lowrank = w_gate_up.shape[1]
    hk, hv = heads * dk, heads * dv
    n_experts = w_router.shape[2]
    g_s5, p_s5 = s5_a_re.shape[1:]

    mod = _ada(jnp.concatenate([c_prompt, c_sample], axis=0), w_ada, b_ada)
    mod_p = [mod[l, :bsz].reshape(bsz, 1, N_MOD * d) for l in range(2)]
    mod_s = [mod[l, bsz:] for l in range(2)]
    SH_M, SC_M, GT_M, SH_F, SC_F, GT_F = range(N_MOD)
    gn = lambda l, k: g_norm[l, k].reshape(1, d)

    w0 = w_in0[0]
    o_lr = 2 * hk + 2 * hv
    w0r = jnp.concatenate([w0[:, :o_lr], w0[:, o_lr + lowrank:], w0[:, o_lr:o_lr + lowrank],
                           jnp.zeros((d, LANES - lowrank), F32)], axis=1)
    wgu_pad = jnp.concatenate([w_gate_up[0], jnp.zeros((LANES - lowrank, hk), F32)], axis=0)
    zoh = _s5_zoh(s5_a_re[0], s5_a_im[0], s5_log_dt[0], s5_b_re[0], s5_b_im[0])
    bb_blk, c_blk, ab_re, ab_im, d_blk = _s5_step_blocks(*zoh, s5_c_re[0], s5_c_im[0], s5_d[0])
    wr_pad = jnp.concatenate([w_router[0], jnp.zeros((d, LANES - n_experts), F32)], axis=1)
    bf = lambda a: a.astype(BF16)

    tm = TOKEN_TILE
    tpb = t // tm
    pk = dict(per_token=False, tiles_per_batch=tpb)
    xp = x_prompt.reshape(bsz * t, d)
    proj, w_o0, w_fg, w_fu, w_fd, w_i1, w_gl = _inproj(
        xp, gn(0, 0), mod_p[0], SC_M, SH_M, bf(w0r), tm=tm, bn=w0r.shape[1], prec="bf16",
        vmem_mb=VMEM_LIMIT_MB, cast_ws=(w_out0[0], w_ffn_gate[0], w_ffn_up[0], w_ffn_down[0], w_in1[0], w_glu[0]), **pk)
    mixed, st_t, cst = _gla_conv_prompt(
        proj, bf(wgu_pad), b_gate_up[0].reshape(1, hk), g_head_norm[0].reshape(1, dv), w_conv[0],
        bsz=bsz, t=t, tt=SEQ_TILE, heads=heads, dk=dk, dv=dv, cw=cw)
    st_t = st_t.reshape(bsz, heads // 2, 2, dv, 2, dk)
    gla_p = jnp.stack([st_t[:, :, 0, :, 0, :], st_t[:, :, 1, :, 1, :]], axis=2)
    gla_p = jnp.swapaxes(gla_p.reshape(bsz, heads, dv, dk), 2, 3)
    x1 = _outproj(mixed, w_o0, xp, mod_p[0], GT_M, gn(0, 1), tm=tm, prec="bf16", glu=False, **pk)
    x2 = _ffn(x1, gn(0, 2), mod_p[0], SC_F, SH_F, GT_F, gn(0, 3), w_fg, w_fu, w_fd, tm=tm,
              fc=w_fg.shape[1], prec="bf16", vmem_mb=VMEM_LIMIT_FFN_MB, **pk)
    u = _inproj(x2, gn(1, 0), mod_p[1], SC_M, SH_M, w_i1, tm=tm, bn=d, prec="bf16", **pk)
    ew = (w_exp_gate[0], w_exp_up[0], w_exp_down[0])
    yg, re_p, im_p, *ew_bf = _s5_prompt(
        u.reshape(bsz, t, d), bf(bb_blk), bf(c_blk), ab_re, ab_im, d_blk,
        tuple(w.reshape(-1, w.shape[2]) for w in ew), tt=SEQ_TILE)
    w_eg, w_eu, w_ed = (wb.reshape(w.shape) for wb, w in zip(ew_bf, ew))
    x3 = _outproj(yg.reshape(bsz * t, d), w_gl, x2, mod_p[1], GT_M, gn(1, 1), tm=tm,
                  prec="bf16", glu=True, vmem_mb=VMEM_LIMIT_MB, **pk)
    comb, hb = _router(x3, gn(1, 2), mod_p[1], SC_F, SH_F, wr_pad, tm=tm, n_experts=n_experts,
                       **pk)

    sk = dict(per_token=True, tiles_per_batch=1)
    xs = x_sample.reshape(bs, d)
    proj_s = _inproj(xs, gn(0, 0), mod_s[0], SC_M, SH_M, w0r, tm=bs, bn=SAMPLE_PROJ_COLS, prec="f32", **sk)
    s_t = jnp.transpose(state_gla[:, 0], (1, 2, 3, 0))
    cbuf_t = jnp.transpose(state_conv[:, 0], (1, 2, 0))
    o_t, y_t, sn_t, cn_t = _gla_conv_step(
        proj_s.T, wgu_pad.T, b_gate_up[0].reshape(hk, 1), g_head_norm[0].reshape(dv, 1),
        w_conv[0].reshape(3, cw, 1), s_t, cbuf_t, heads=heads, dk=dk, dv=dv, cw=cw)
    mixed_s = jnp.concatenate([o_t, y_t], axis=0).T
    gla_s = jnp.transpose(sn_t, (3, 0, 1, 2))
    conv_s = jnp.transpose(cn_t, (2, 0, 1))
    x1s = _outproj(mixed_s, w_out0[0], xs, mod_s[0], GT_M, gn(0, 1), tm=bs, prec="f32", glu=False,
                   **sk)
    x2s = _ffn(x1s, gn(0, 2), mod_s[0], SC_F, SH_F, GT_F, gn(0, 3), w_ffn_gate[0], w_ffn_up[0],
               w_ffn_down[0], tm=bs, fc=SAMPLE_COLS, prec="f32", **sk)
    us = _inproj(x2s, gn(1, 0), mod_s[1], SC_M, SH_M, w_in1[0], tm=bs, bn=SAMPLE_COLS, prec="f32", **sk)
    ygs, re_s, im_s = _s5_step(us, bb_blk, c_blk, ab_re, ab_im, d_blk,
                               state_s5_re[:, 0].reshape(bs, g_s5 * p_s5),
                               state_s5_im[:, 0].reshape(bs, g_s5 * p_s5))
    x3s = _outproj(ygs, w_glu[0], x2s, mod_s[1], GT_M, gn(1, 1), tm=bs, prec="f32", glu=True,
                   vmem_mb=VMEM_LIMIT_MB, **sk)
    comb_s, hb_s = _router(x3s, gn(1, 2), mod_s[1], SC_F, SH_F, wr_pad, tm=bs, n_experts=n_experts,
                           **sk)

    assert bs <= MOE_TB
    n_blk = bsz * t // MOE_TB + 1
    n_tiles_max = -(-(n_blk * _moe_block_rows(n_experts)) // MOE_TM) + n_experts
    pad_rows = lambda a: jnp.concatenate([a, jnp.zeros((MOE_TB - bs, a.shape[1]), a.dtype)], axis=0)
    hb_s, comb_s = pad_rows(hb_s), pad_rows(comb_s)
    seg, gap_pos, gap_len, tile_expert, n_tiles = _moe_schedule((comb, comb_s), n_experts,
                                                                n_tiles_max)
    dest, xs_sorted = _moe_sort(seg, gap_pos, gap_len, hb, comb, hb_s, comb_s,
                                n_tiles_max * MOE_TM, n_experts=n_experts)
    ys_sorted = _experts(tile_expert, n_tiles, xs_sorted, w_eg, w_eu, w_ed, fc=EXPERT_FF_CHUNK)
    x4 = _moe_combine(*seg, ys_sorted, dest, x3, mod_p[1], GT_F, gn(1, 3), blk0=0,
                      n_experts=n_experts, per_token=False, tiles_per_batch=t // MOE_TB)
    x4s = _moe_combine(*seg, ys_sorted, dest, pad_rows(x3s), pad_rows(mod_s[1]), GT_F,
                       gn(1, 3), blk0=n_blk - 1, n_experts=n_experts, per_token=True,
                       tiles_per_batch=1)[:bs]

    return (x4.reshape(bsz, t, d), x4s.reshape(bs, 1, d),
            gla_p[:, None], cst[:, None],
            re_p.reshape(bsz, 1, g_s5, p_s5), im_p.reshape(bsz, 1, g_s5, p_s5),
            gla_s[:, None], conv_s[:, None],
            re_s.reshape(bs, 1, g_s5, p_s5), im_s.reshape(bs, 1, g_s5, p_s5))
```

```python
import functools
import math

import jax
import jax.numpy as jnp
from jax import lax
from jax.experimental import pallas as pl
from jax.experimental.pallas import tpu as pltpu

F32 = jnp.float32
BF16 = jnp.bfloat16
EPS = 1e-6
LANES = 128
GLA_CHUNK = 64
GATE_NORMALIZER = 16.0
N_MOD = 6
S5_GROUP = 16
S5_CHUNK_GROUPS = 8

TOKEN_TILE = 512
SEQ_TILE = 256
VMEM_LIMIT_MB = 48
VMEM_LIMIT_FFN_MB = 56
EXPERT_FF_CHUNK = 1792
SAMPLE_PROJ_COLS = 640
SAMPLE_COLS = 256


def _dot(a, b, dims=None):
    if dims is None:
        return jnp.dot(a, b, preferred_element_type=F32)
    return lax.dot_general(a, b, (dims, ((), ())), preferred_element_type=F32)


def _split(a, parts=2):
    rem = a.astype(F32)
    out = []
    for _ in range(parts - 1):
        piece = rem.astype(BF16)
        out.append(piece)
        rem = rem - piece.astype(F32)
    out.append(rem.astype(BF16))
    return out


def _mm(a, w, prec):
    if prec == "bf16":
        return _dot(a.astype(BF16), w.astype(BF16))
    a1, a2, a3 = _split(a, 3)
    w1, w2, w3 = _split(w, 3)
    small = (_dot(a1, w3) + _dot(a3, w1)) + _dot(a2, w2)
    return _dot(a1, w1) + ((_dot(a1, w2) + _dot(a2, w1)) + small)


def _silu(x):
    return x * jax.nn.sigmoid(x)


def _gelu_tanh(x):
    return 0.5 * x * (1.0 + jnp.tanh(math.sqrt(2.0 / math.pi) * (x + 0.044715 * (x * x * x))))


def _log_sigmoid(x):
    return -(jnp.maximum(-x, 0.0) + jnp.log1p(jnp.exp(-jnp.abs(x))))


def _rms(x):
    return x * lax.rsqrt(jnp.mean(x * x, axis=-1, keepdims=True) + EPS)


def _mod_row(ref):
    return ref[0] if len(ref.shape) == 3 else ref[...]


def _normmod(x, g, sc, sh):
    return (_rms(x) * g) * (1.0 + sc) + sh


def _mod_spec(per_token, tm, d, col, tiles_per_batch):
    if per_token:
        return pl.BlockSpec((tm, d), lambda i, *_: (i, col))
    return pl.BlockSpec((1, 1, d), lambda i, *_: (i // tiles_per_batch, 0, col))


def _params(sem, vmem_mb=None):
    kw = dict(dimension_semantics=sem)
    if vmem_mb is not None:
        kw["vmem_limit_bytes"] = vmem_mb << 20
    return pltpu.CompilerParams(**kw)


def _ada_kernel(c_ref, w_ref, b_ref, o_ref):
    o_ref[0] = _mm(_silu(c_ref[...]), w_ref[0], "f32") + b_ref[0]


def _ada(c_all, w_ada, b_ada):
    depth, d, n6 = w_ada.shape
    rows = c_all.shape[0]
    return pl.pallas_call(
        _ada_kernel,
        grid=(depth, n6 // d),
        in_specs=[pl.BlockSpec((rows, d), lambda l, j: (0, 0)),
                  pl.BlockSpec((1, d, d), lambda l, j: (l, 0, j)),
                  pl.BlockSpec((1, 1, d), lambda l, j: (l, 0, j))],
        out_specs=pl.BlockSpec((1, rows, d), lambda l, j: (l, 0, j)),
        out_shape=jax.ShapeDtypeStruct((depth, rows, n6), F32),
        compiler_params=_params(("parallel", "parallel")),
    )(c_all, w_ada, b_ada.reshape(depth, 1, n6))


BF16_ROWS = 16


def _passenger_specs(ws, n_steps, step):
    specs = []
    for w in ws:
        rows, cols = w.shape
        units = rows // BF16_ROWS
        n_used = max(k for k in range(1, n_steps + 1) if units % k == 0)
        specs.append(pl.BlockSpec(
            (rows // n_used, cols),
            lambda *idx, n_used=n_used: (jnp.minimum(step(*idx), n_used - 1), 0)))
    return specs


def _cast_passengers(in_refs, out_refs):
    for src, dst in zip(in_refs, out_refs):
        dst[...] = src[...].astype(dst.dtype)


def _inproj_kernel(*refs, prec, n_cast):
    x_ref, g_ref, sc_ref, sh_ref, w_ref = refs[:5]
    o_ref = refs[5 + n_cast]
    _cast_passengers(refs[5:5 + n_cast], refs[6 + n_cast:])
    h = _normmod(x_ref[...], g_ref[...], _mod_row(sc_ref), _mod_row(sh_ref))
    o_ref[...] = _mm(h, w_ref[...], prec).astype(o_ref.dtype)


def _inproj(x2d, g, mod, sc_col, sh_col, w, *, tm, bn, prec, per_token, tiles_per_batch,
            out_dtype=F32, vmem_mb=None, cast_ws=()):
    n_tok, d = x2d.shape
    n_out = w.shape[1]
    n_j = n_out // bn
    p_specs = _passenger_specs(cast_ws, n_tok // tm * n_j, lambda i, j: i * n_j + j)
    out = pl.pallas_call(
        functools.partial(_inproj_kernel, prec=prec, n_cast=len(cast_ws)),
        grid=(n_tok // tm, n_j),
        in_specs=[pl.BlockSpec((tm, d), lambda i, j: (i, 0)),
                  pl.BlockSpec((1, d), lambda i, j: (0, 0)),
                  _mod_spec(per_token, tm, d, sc_col, tiles_per_batch),
                  _mod_spec(per_token, tm, d, sh_col, tiles_per_batch),
                  pl.BlockSpec((d, bn), lambda i, j: (0, j))] + p_specs,
        out_specs=[pl.BlockSpec((tm, bn), lambda i, j: (i, j))] + p_specs,
        out_shape=[jax.ShapeDtypeStruct((n_tok, n_out), out_dtype)]
        + [jax.ShapeDtypeStruct(c.shape, BF16) for c in cast_ws],
        compiler_params=_params(("arbitrary", "arbitrary"), vmem_mb),
    )(x2d, g, mod, mod, w, *cast_ws)
    return out if cast_ws else out[0]


def _outproj_kernel(a_ref, w_ref, x_ref, gt_ref, g_ref, o_ref, *, prec, glu):
    z = _mm(a_ref[...], w_ref[...], prec)
    if glu:
        d = z.shape[1] // 2
        z = z[:, :d] * jax.nn.sigmoid(z[:, d:])
    o_ref[...] = x_ref[...] + _mod_row(gt_ref) * (_rms(z) * g_ref[...])


def _outproj(a2d, w, x2d, mod, gt_col, g, *, tm, prec, glu, per_token, tiles_per_batch,
             vmem_mb=None):
    n_tok, d = x2d.shape
    k, n_out = w.shape
    return pl.pallas_call(
        functools.partial(_outproj_kernel, prec=prec, glu=glu),
        grid=(n_tok // tm,),
        in_specs=[pl.BlockSpec((tm, k), lambda i: (i, 0)),
                  pl.BlockSpec((k, n_out), lambda i: (0, 0)),
                  pl.BlockSpec((tm, d), lambda i: (i, 0)),
                  _mod_spec(per_token, tm, d, gt_col, tiles_per_batch),
                  pl.BlockSpec((1, d), lambda i: (0, 0))],
        out_specs=pl.BlockSpec((tm, d), lambda i: (i, 0)),
        out_shape=jax.ShapeDtypeStruct((n_tok, d), F32),
        compiler_params=_params(("parallel",), vmem_mb),
    )(a2d, w, x2d, mod, g)


def _mix_ffn_proj_kernel(a_ref, wo_ref, x_ref, gtm_ref, g01_ref, g02_ref, scf_ref, shf_ref, wg_ref,
                         wu_ref, wd_ref, gtf_ref, g03_ref, g10_ref, sc1_ref, sh1_ref, wi_ref,
                         x2_ref, u_ref):
    x1 = x_ref[...] + _mod_row(gtm_ref) * (_rms(_mm(a_ref[...], wo_ref[...], "bf16")) * g01_ref[...])
    h = _normmod(x1, g02_ref[...], _mod_row(scf_ref), _mod_row(shf_ref))
    act = _silu(_mm(h, wg_ref[...], "bf16")) * _mm(h, wu_ref[...], "bf16")
    x2 = x1 + _mod_row(gtf_ref) * (_rms(_mm(act, wd_ref[...], "bf16")) * g03_ref[...])
    x2_ref[...] = x2
    h1 = _normmod(x2, g10_ref[...], _mod_row(sc1_ref), _mod_row(sh1_ref))
    u_ref[...] = _mm(h1, wi_ref[...], "bf16").astype(u_ref.dtype)


def _mix_ffn_proj(mixed, w_out, x2d, mod0, mod1, g_norm, w_gate, w_up, w_down, w_in1, *, tm,
                  tiles_per_batch, vmem_mb):
    n_tok, d = x2d.shape
    sh_m, sc_m, gt_m, sh_f, sc_f, gt_f = range(N_MOD)
    tok = lambda w: pl.BlockSpec((tm, w), lambda i: (i, 0))
    vec = pl.BlockSpec((1, d), lambda i: (0, 0))
    mod = lambda col: _mod_spec(False, tm, d, col, tiles_per_batch)
    once = lambda w: pl.BlockSpec(w.shape, lambda i: (0, 0), pipeline_mode=pl.Buffered(1))
    g = lambda l, k: g_norm[l, k].reshape(1, d)
    return pl.pallas_call(
        _mix_ffn_proj_kernel,
        grid=(n_tok // tm,),
        in_specs=[tok(mixed.shape[1]), once(w_out), tok(d), mod(gt_m), vec, vec, mod(sc_f), mod(sh_f),
                  once(w_gate), once(w_up), once(w_down), mod(gt_f), vec, vec, mod(sc_m), mod(sh_m),
                  once(w_in1)],
        out_specs=[tok(d), tok(w_in1.shape[1])],
        out_shape=[jax.ShapeDtypeStruct((n_tok, d), F32),
                   jax.ShapeDtypeStruct((n_tok, w_in1.shape[1]), BF16)],
        compiler_params=_params(("parallel",), vmem_mb),
    )(mixed, w_out, x2d, mod0, g(0, 1), g(0, 2), mod0, mod0, w_gate, w_up, w_down, mod0, g(0, 3),
      g(1, 0), mod1, mod1, w_in1)


def _glu_route_kernel(a_ref, w_ref, x_ref, gtm_ref, g1_ref, g2_ref, scf_ref, shf_ref, wr_ref,
                      x3_ref, comb_ref, hb_ref, wp_scr, *, n_experts):
    @pl.when(pl.program_id(0) == 0)
    def _():
        _pack_router_weight(wr_ref, wp_scr, n_experts)

    z = _mm(a_ref[...], w_ref[...], "bf16")
    d = z.shape[1] // 2
    x3 = x_ref[...] + _mod_row(gtm_ref) * (_rms(z[:, :d] * jax.nn.sigmoid(z[:, d:])) * g1_ref[...])
    x3_ref[...] = x3
    h = _normmod(x3, g2_ref[...], _mod_row(scf_ref), _mod_row(shf_ref))
    hb_ref[...] = h.astype(hb_ref.dtype)
    comb_ref[...] = _route(h, wp_scr[...], n_experts)


def _glu_route(a2d, w_glu, x2d, mod1, g_norm, wr_pad, *, tm, n_experts, tiles_per_batch, vmem_mb):
    n_tok, d = x2d.shape
    sh_m, sc_m, gt_m, sh_f, sc_f, gt_f = range(N_MOD)
    tok = lambda w: pl.BlockSpec((tm, w), lambda i: (i, 0))
    vec = pl.BlockSpec((1, d), lambda i: (0, 0))
    mod = lambda col: _mod_spec(False, tm, d, col, tiles_per_batch)
    g = lambda k: g_norm[1, k].reshape(1, d)
    return pl.pallas_call(
        functools.partial(_glu_route_kernel, n_experts=n_experts),
        grid=(n_tok // tm,),
        in_specs=[tok(a2d.shape[1]), pl.BlockSpec(w_glu.shape, lambda i: (0, 0)), tok(d), mod(gt_m),
                  vec, vec, mod(sc_f), mod(sh_f), pl.BlockSpec(wr_pad.shape, lambda i: (0, 0))],
        out_specs=[tok(d), tok(LANES), tok(d)],
        out_shape=[jax.ShapeDtypeStruct((n_tok, d), F32),
                   jax.ShapeDtypeStruct((n_tok, LANES), F32),
                   jax.ShapeDtypeStruct((n_tok, d), BF16)],
        scratch_shapes=[pltpu.VMEM(wr_pad.shape, BF16)],
        compiler_params=_params(("arbitrary",), vmem_mb),
    )(a2d, w_glu, x2d, mod1, g(1), g(2), mod1, mod1, wr_pad)


def _ffn_kernel(x_ref, g1_ref, sc_ref, sh_ref, wg_ref, wu_ref, wd_ref, gt_ref, g2_ref,
                o_ref, h_scr, acc_scr, *, prec, n_c):
    c = pl.program_id(1)

    @pl.when(c == 0)
    def _():
        h = _normmod(x_ref[...], g1_ref[...], _mod_row(sc_ref), _mod_row(sh_ref))
        h_scr[...] = h.astype(h_scr.dtype)
        acc_scr[...] = jnp.zeros_like(acc_scr)

    h = h_scr[...]
    act = _silu(_mm(h, wg_ref[...], prec)) * _mm(h, wu_ref[...], prec)
    acc_scr[...] += _mm(act, wd_ref[...], prec)

    @pl.when(c == n_c - 1)
    def _():
        o_ref[...] = x_ref[...] + _mod_row(gt_ref) * (_rms(acc_scr[...]) * g2_ref[...])


def _ffn(x2d, g1, mod, sc_col, sh_col, gt_col, g2, wg, wu, wd, *, tm, fc, prec,
         per_token, tiles_per_batch, vmem_mb=None):
    n_tok, d = x2d.shape
    n_c = wg.shape[1] // fc
    tok = pl.BlockSpec((tm, d), lambda i, c: (i, 0))
    vec = pl.BlockSpec((1, d), lambda i, c: (0, 0))
    h_dtype = BF16 if prec == "bf16" else F32
    w_mode = dict(pipeline_mode=pl.Buffered(1)) if n_c == 1 else {}
    return pl.pallas_call(
        functools.partial(_ffn_kernel, prec=prec, n_c=n_c),
        grid=(n_tok // tm, n_c),
        in_specs=[tok, vec,
                  _mod_spec(per_token, tm, d, sc_col, tiles_per_batch),
                  _mod_spec(per_token, tm, d, sh_col, tiles_per_batch),
                  pl.BlockSpec((d, fc), lambda i, c: (0, c), **w_mode),
                  pl.BlockSpec((d, fc), lambda i, c: (0, c), **w_mode),
                  pl.BlockSpec((fc, d), lambda i, c: (c, 0), **w_mode),
                  _mod_spec(per_token, tm, d, gt_col, tiles_per_batch), vec],
        out_specs=tok,
        out_shape=jax.ShapeDtypeStruct((n_tok, d), F32),
        scratch_shapes=[pltpu.VMEM((tm, d), h_dtype), pltpu.VMEM((tm, d), F32)],
        compiler_params=_params(("parallel", "arbitrary"), vmem_mb),
    )(x2d, g1, mod, mod, wg, wu, wd, mod, g2)


def _pack_router_weight(wr_ref, wp_scr, n_experts):
    w1, w2, w3 = _split(wr_ref[...], 3)
    fwd = lambda x, k: jnp.concatenate([x[:, LANES - k * n_experts:], x[:, :LANES - k * n_experts]], axis=1)
    packed = w1.astype(F32) + (fwd(w2.astype(F32), 1) + fwd(w3.astype(F32), 2))
    wp_scr[...] = packed.astype(wp_scr.dtype)


def _route(h, w, n_experts):
    h1, h2, h3 = _split(h, 3)
    p1, p2, p3 = _dot(h1, w), _dot(h2, w), _dot(h3, w)
    back = lambda x, k: jnp.concatenate([x[:, k * n_experts:], x[:, :k * n_experts]], axis=1)
    small = (back(p1, 2) + p3) + back(p2, 1)
    logits = p1 + ((back(p1, 1) + p2) + small)
    lane = lax.broadcasted_iota(jnp.int32, logits.shape, 1).astype(F32)
    neg = -jnp.inf
    l1 = jnp.where(lane < n_experts, logits, neg)
    m1 = jnp.max(l1, axis=-1, keepdims=True)
    i1 = jnp.min(jnp.where(l1 == m1, lane, float(LANES)), axis=-1, keepdims=True)
    l2 = jnp.where(lane == i1, neg, l1)
    m2 = jnp.max(l2, axis=-1, keepdims=True)
    i2 = jnp.min(jnp.where(l2 == m2, lane, float(LANES)), axis=-1, keepdims=True)
    e2 = jnp.exp(m2 - m1)
    den = 1.0 + e2
    return jnp.where(lane == i1, 1.0 / den, 0.0) + jnp.where(lane == i2, e2 / den, 0.0)


def _router_kernel(x_ref, g_ref, sc_ref, sh_ref, wr_ref, comb_ref, hb_ref, wp_scr, *, n_experts):
    @pl.when(pl.program_id(0) == 0)
    def _():
        _pack_router_weight(wr_ref, wp_scr, n_experts)

    h = _normmod(x_ref[...], g_ref[...], _mod_row(sc_ref), _mod_row(sh_ref))
    hb_ref[...] = h.astype(hb_ref.dtype)
    comb_ref[...] = _route(h, wp_scr[...], n_experts)


def _router(x2d, g, mod, sc_col, sh_col, wr_pad, *, tm, n_experts, per_token, tiles_per_batch):
    n_tok, d = x2d.shape
    return pl.pallas_call(
        functools.partial(_router_kernel, n_experts=n_experts),
        grid=(n_tok // tm,),
        in_specs=[pl.BlockSpec((tm, d), lambda i: (i, 0)),
                  pl.BlockSpec((1, d), lambda i: (0, 0)),
                  _mod_spec(per_token, tm, d, sc_col, tiles_per_batch),
                  _mod_spec(per_token, tm, d, sh_col, tiles_per_batch),
                  pl.BlockSpec((d, LANES), lambda i: (0, 0))],
        out_specs=[pl.BlockSpec((tm, LANES), lambda i: (i, 0)),
                   pl.BlockSpec((tm, d), lambda i: (i, 0))],
        out_shape=[jax.ShapeDtypeStruct((n_tok, LANES), F32),
                   jax.ShapeDtypeStruct((n_tok, d), BF16)],
        scratch_shapes=[pltpu.VMEM((d, LANES), BF16)],
        compiler_params=_params(("arbitrary",)),
    )(x2d, g, mod, mod, wr_pad)


MOE_TB = 256
MOE_SEG = 16
MOE_TM = 512


def _moe_block_rows(n_experts):
    return -(-(2 * MOE_TB + n_experts * (MOE_SEG - 1)) // LANES) * LANES


def _routing_slots(comb, dest):
    lane = lax.broadcasted_iota(jnp.int32, comb.shape, 1).astype(F32)
    sel = comb > 0.0
    i_lo = jnp.min(jnp.where(sel, lane, float(LANES)), axis=-1, keepdims=True)
    i_hi = jnp.max(jnp.where(sel, lane, -1.0), axis=-1, keepdims=True)
    pick = lambda i, v: jnp.sum(jnp.where(lane == i, v, 0.0), axis=-1, keepdims=True)
    has_lo = i_lo < float(LANES)
    has_hi = i_hi > i_lo
    d_lo = jnp.where(has_lo, pick(i_lo, dest), -1.0)
    d_hi = jnp.where(has_hi, pick(i_hi, dest), -1.0)
    w_lo = pick(i_lo, comb)
    w_hi = jnp.where(has_hi, pick(i_hi, comb), 0.0)
    out = jnp.where(lane == 0.0, d_lo, 0.0)
    for k, v in ((1.0, d_hi), (2.0, w_lo), (3.0, w_hi)):
        out = jnp.where(lane == k, v, out)
    return out


def _one_hot_rows(row, n_rows):
    tb = row.shape[0]
    lane = lax.broadcasted_iota(jnp.int32, (1, LANES), 1).astype(F32)
    row = jnp.broadcast_to(row, (tb, LANES))
    blocks = [jnp.where(row == lane + float(cb * LANES), 1.0, 0.0) for cb in range(n_rows // LANES)]
    return jnp.concatenate(blocks, axis=1).astype(BF16)


_SEG_BITS = (256, 128, 64, 32, 16)


def _segment_copies(make_copy, blk, pos_ref, off_ref, cnt_ref, n_experts, start):
    for e in range(n_experts):
        seg = blk * n_experts + e
        in_blk, in_sorted, n = off_ref[seg], pos_ref[seg], cnt_ref[seg]
        for size in _SEG_BITS:
            has = (n & size) != 0

            @pl.when(has)
            def _(in_blk=in_blk, in_sorted=in_sorted, size=size):
                cp = make_copy(pl.multiple_of(in_blk, MOE_SEG), pl.multiple_of(in_sorted, MOE_SEG), size)
                cp.start() if start else cp.wait()

            step = jnp.where(has, size, 0)
            in_blk, in_sorted = in_blk + step, in_sorted + step


def _moe_sort_kernel(pos_ref, off_ref, cnt_ref, gpos_ref, glen_ref, hbp_ref, combp_ref, hbs_ref,
                     combs_ref, dest_ref, xs_hbm, buf, zbuf, sems, *, n_rows, n_experts, n_blk, n_blk_p):
    b = pl.program_id(0)
    slot = b % 2

    def writes(blk, slot, start):
        def make_copy(block_row, sorted_row, size):
            return pltpu.make_async_copy(buf.at[slot, pl.ds(block_row, size)],
                                         xs_hbm.at[pl.ds(sorted_row, size)], sems.at[slot])
        _segment_copies(make_copy, blk, pos_ref, off_ref, cnt_ref, n_experts, start)

    def zero_gaps(start):
        for e in range(n_experts):
            row, n = gpos_ref[e], glen_ref[e]
            for size in _SEG_BITS:
                has = (n & size) != 0

                @pl.when(has)
                def _(row=row, size=size):
                    cp = pltpu.make_async_copy(
                        zbuf.at[pl.ds(0, size)],
                        xs_hbm.at[pl.ds(pl.multiple_of(row, MOE_SEG), size)], sems.at[2])
                    cp.start() if start else cp.wait()

                row = row + jnp.where(has, size, 0)
        tail_row, tail_n, big = gpos_ref[n_experts], glen_ref[n_experts], _SEG_BITS[0]

        def piece(i, carry):
            cp = pltpu.make_async_copy(
                zbuf.at[pl.ds(0, big)],
                xs_hbm.at[pl.ds(pl.multiple_of(tail_row + i * big, MOE_SEG), big)], sems.at[2])
            cp.start() if start else cp.wait()
            return carry

        lax.fori_loop(0, tail_n // big, piece, 0)

    @pl.when(b == 0)
    def _():
        zbuf[...] = jnp.zeros_like(zbuf)
        zero_gaps(True)

    @pl.when(b >= 2)
    def _():
        writes(b - 2, slot, False)

    in_prompt = b < n_blk_p
    comb = jnp.where(in_prompt, combp_ref[...], combs_ref[...])
    hb = jnp.where(in_prompt, hbp_ref[...], hbs_ref[...])
    tb = comb.shape[0]
    sel = jnp.where(comb > 0.0, 1.0, 0.0)
    r = lax.broadcasted_iota(jnp.int32, (tb, tb), 0)
    c = lax.broadcasted_iota(jnp.int32, (tb, tb), 1)
    rank = _dot(jnp.where(c < r, 1.0, 0.0).astype(BF16), sel.astype(BF16))
    cnt = jnp.sum(sel, axis=0, keepdims=True)
    cnt = jnp.floor((cnt + (MOE_SEG - 1)) / MOE_SEG) * MOE_SEG
    ru = lax.broadcasted_iota(jnp.int32, (LANES, LANES), 0)
    cu = lax.broadcasted_iota(jnp.int32, (LANES, LANES), 1)
    off = _dot(jnp.broadcast_to(cnt, (8, LANES)).astype(BF16),
               jnp.where(ru < cu, 1.0, 0.0).astype(BF16))[0:1]
    slots = _routing_slots(comb, off + rank)
    dest_ref[...] = slots
    pt = _one_hot_rows(slots[:, 0:1], n_rows) + _one_hot_rows(slots[:, 1:2], n_rows)
    buf[slot] = _dot(pt, hb, ((0,), (0,))).astype(buf.dtype)
    writes(b, slot, True)

    @pl.when(b == n_blk - 1)
    def _():
        writes(b, slot, False)
        if n_blk > 1:
            writes(b - 1, 1 - slot, False)
        zero_gaps(False)


def _moe_sort(seg, gap_pos, gap_len, hb_p, comb_p, hb_s, comb_s, n_sorted_rows, *, n_experts):
    n_p, d = hb_p.shape
    n_blk_p = n_p // MOE_TB
    n_blk = n_blk_p + 1
    n_rows = _moe_block_rows(n_experts)
    last_p = n_blk_p - 1
    return pl.pallas_call(
        functools.partial(_moe_sort_kernel, n_rows=n_rows, n_experts=n_experts, n_blk=n_blk,
                          n_blk_p=n_blk_p),
        grid_spec=pltpu.PrefetchScalarGridSpec(
            num_scalar_prefetch=5, grid=(n_blk,),
            in_specs=[pl.BlockSpec((MOE_TB, d), lambda b, *_: (jnp.minimum(b, last_p), 0)),
                      pl.BlockSpec((MOE_TB, LANES), lambda b, *_: (jnp.minimum(b, last_p), 0)),
                      pl.BlockSpec((MOE_TB, d), lambda b, *_: (0, 0)),
                      pl.BlockSpec((MOE_TB, LANES), lambda b, *_: (0, 0))],
            out_specs=[pl.BlockSpec((MOE_TB, LANES), lambda b, *_: (b, 0)),
                       pl.BlockSpec(memory_space=pl.ANY)],
            scratch_shapes=[pltpu.VMEM((2, n_rows, d), BF16), pltpu.VMEM((_SEG_BITS[0], d), BF16),
                            pltpu.SemaphoreType.DMA((3,))]),
        out_shape=[jax.ShapeDtypeStruct((n_blk * MOE_TB, LANES), F32),
                   jax.ShapeDtypeStruct((n_sorted_rows, d), BF16)],
        compiler_params=_params(("arbitrary",)),
    )(*seg, gap_pos, gap_len, hb_p, comb_p, hb_s, comb_s)


def _experts_kernel(te_ref, nt_ref, x_ref, wg_ref, wu_ref, wd_ref, o_ref, acc_scr, *, n_c):
    del te_ref
    k = pl.program_id(0)
    c = pl.program_id(1)
    active = k < nt_ref[0]

    @pl.when(active)
    def _():
        h = x_ref[...]
        act = _silu(_dot(h, wg_ref[0])) * _dot(h, wu_ref[0])
        y = _dot(act.astype(BF16), wd_ref[0])

        @pl.when(c == 0)
        def _():
            acc_scr[...] = y

        @pl.when(c > 0)
        def _():
            acc_scr[...] += y

        @pl.when(c == n_c - 1)
        def _():
            o_ref[...] = acc_scr[...].astype(o_ref.dtype)

    @pl.when(jnp.logical_not(active) & (c == n_c - 1))
    def _():
        o_ref[...] = jnp.zeros_like(o_ref)


def _experts(tile_expert, n_tiles, xs, wg, wu, wd, *, fc):
    n_rows, d = xs.shape
    n_c = wg.shape[2] // fc

    def row_map(k, c, te, nt):
        return (jnp.minimum(k, nt[0] - 1), 0)

    def w_map(k, c, te, nt):
        return (te[k], 0, jnp.where(k < nt[0], c, n_c - 1))

    def wd_map(k, c, te, nt):
        return (te[k], jnp.where(k < nt[0], c, n_c - 1), 0)

    return pl.pallas_call(
        functools.partial(_experts_kernel, n_c=n_c),
        grid_spec=pltpu.PrefetchScalarGridSpec(
            num_scalar_prefetch=2, grid=(n_rows // MOE_TM, n_c),
            in_specs=[pl.BlockSpec((MOE_TM, d), row_map),
                      pl.BlockSpec((1, d, fc), w_map), pl.BlockSpec((1, d, fc), w_map),
                      pl.BlockSpec((1, fc, d), wd_map)],
            out_specs=pl.BlockSpec((MOE_TM, d), lambda k, c, te, nt: (k, 0)),
            scratch_shapes=[pltpu.VMEM((MOE_TM, d), F32)]),
        out_shape=jax.ShapeDtypeStruct((n_rows, d), BF16),
        compiler_params=_params(("arbitrary", "arbitrary"), VMEM_LIMIT_MB),
    )(tile_expert, n_tiles, xs, wg, wu, wd)


def _moe_combine_kernel(pos_ref, off_ref, cnt_ref, ys_hbm, dest_ref, x_ref, gt_ref, g_ref,
                        o_ref, buf, sems, *, n_experts, n_blk, blk0):
    b = pl.program_id(0)
    slot = b % 2

    def reads(blk, slot, start):
        def make_copy(block_row, sorted_row, size):
            return pltpu.make_async_copy(ys_hbm.at[pl.ds(sorted_row, size)],
                                         buf.at[slot, pl.ds(block_row, size)], sems.at[slot])
        _segment_copies(make_copy, blk0 + blk, pos_ref, off_ref, cnt_ref, n_experts, start)

    @pl.when(b == 0)
    def _():
        buf[...] = jnp.zeros_like(buf)
        reads(b, slot, True)

    if n_blk > 1:
        @pl.when(b + 1 < n_blk)
        def _():
            reads(b + 1, 1 - slot, True)

    reads(b, slot, False)
    yb = buf[slot]
    slots = dest_ref[...]
    n_rows = yb.shape[0]
    y = (slots[:, 2:3] * _dot(_one_hot_rows(slots[:, 0:1], n_rows), yb)
         + slots[:, 3:4] * _dot(_one_hot_rows(slots[:, 1:2], n_rows), yb))
    o_ref[...] = x_ref[...] + _mod_row(gt_ref) * (_rms(y) * g_ref[...])


def _moe_combine(sorted_pos, block_off, seg_cnt, ys, dest, x2d, mod, gt_col, g, *, blk0,
                 n_experts, per_token, tiles_per_batch):
    n_tok, d = x2d.shape
    n_blk = n_tok // MOE_TB
    n_rows = _moe_block_rows(n_experts)
    return pl.pallas_call(
        functools.partial(_moe_combine_kernel, n_experts=n_experts, n_blk=n_blk, blk0=blk0),
        grid_spec=pltpu.PrefetchScalarGridSpec(
            num_scalar_prefetch=3, grid=(n_blk,),
            in_specs=[pl.BlockSpec(memory_space=pl.ANY),
                      pl.BlockSpec((MOE_TB, LANES), lambda b, *_: (blk0 + b, 0)),
                      pl.BlockSpec((MOE_TB, d), lambda b, *_: (b, 0)),
                      _mod_spec(per_token, MOE_TB, d, gt_col, tiles_per_batch),
                      pl.BlockSpec((1, d), lambda b, *_: (0, 0))],
            out_specs=pl.BlockSpec((MOE_TB, d), lambda b, *_: (b, 0)),
            scratch_shapes=[pltpu.VMEM((2, n_rows, d), BF16), pltpu.SemaphoreType.DMA((2,))]),
        out_shape=jax.ShapeDtypeStruct((n_tok, d), F32),
        compiler_params=_params(("arbitrary",)),
    )(sorted_pos, block_off, seg_cnt, ys, dest, x2d, mod, g)


def _moe_schedule(combs, n_experts, n_tiles_max):
    def block_counts(comb):
        n_blk = comb.shape[0] // MOE_TB
        sel = (comb[:, :n_experts] > 0.0).astype(jnp.int32)
        return jnp.sum(sel.reshape(n_blk, MOE_TB, n_experts), axis=1)

    cnt = jnp.concatenate([block_counts(c) for c in combs], axis=0)
    cnt = -(-cnt // MOE_SEG) * MOE_SEG
    off = jnp.cumsum(cnt, axis=1) - cnt
    tot = jnp.sum(cnt, axis=0)
    grp = -(-tot // MOE_TM) * MOE_TM
    g_end = jnp.cumsum(grp)
    sorted_pos = (g_end - grp)[None, :] + (jnp.cumsum(cnt, axis=0) - cnt)
    n_tiles = g_end[-1] // MOE_TM
    first_row = jnp.arange(n_tiles_max, dtype=jnp.int32) * MOE_TM
    tile_expert = jnp.sum(first_row[:, None] >= g_end[None, :], axis=1)
    last = jnp.sum((n_tiles - 1) * MOE_TM >= g_end)
    tile_expert = jnp.minimum(tile_expert, last).astype(jnp.int32)
    flat = lambda a: a.reshape(-1).astype(jnp.int32)
    seg = (flat(sorted_pos), flat(off), flat(cnt))
    gap_pos = flat(jnp.concatenate([g_end - grp + tot, g_end[-1:]]))
    gap_len = flat(jnp.concatenate([grp - tot, n_tiles_max * MOE_TM - g_end[-1:]]))
    return seg, gap_pos, gap_len, tile_expert, n_tiles.reshape(1).astype(jnp.int32)


def _gla_conv_kernel(p_ref, wgu_ref, bgu_ref, gh_ref, wc_ref, mixed_ref, st_ref, cst_ref,
                     s_scr, uext_scr, *, tt, n_t, heads, dk, dv, cw):
    ti = pl.program_id(1)
    hk = heads * dk
    hv = heads * dv
    o_q, o_k, o_v, o_og = 0, hk, 2 * hk, 2 * hk + hv
    o_ch = o_og + hv
    o_cb, o_cc, o_lr = o_ch + cw, o_ch + 2 * cw, o_ch + 3 * cw
    L = GLA_CHUNK

    @pl.when(ti == 0)
    def _():
        s_scr[...] = jnp.zeros_like(s_scr)
        uext_scr[0:8, :] = jnp.zeros((8, cw), F32)

    logit = _dot(p_ref[:, o_lr:o_lr + LANES].astype(BF16), wgu_ref[...]) + bgu_ref[...]
    logg = _log_sigmoid(logit) / GATE_NORMALIZER

    r = lax.broadcasted_iota(jnp.int32, (tt, tt), 0)
    c = lax.broadcasted_iota(jnp.int32, (tt, tt), 1)
    tri = jnp.where((r // L == c // L) & (c <= r), 1.0, 0.0).astype(BF16)
    g_hi = logg.astype(BF16)
    rem = logg - g_hi.astype(F32)
    g_mid = rem.astype(BF16)
    g_lo = (rem - g_mid.astype(F32)).astype(BF16)
    bc = _dot(tri, g_hi) + (_dot(tri, g_mid) + _dot(tri, g_lo))

    rl = lax.broadcasted_iota(jnp.int32, (L, L), 0)
    cl = lax.broadcasted_iota(jnp.int32, (L, L), 1)
    tril = cl <= rl
    lane = lax.broadcasted_iota(jnp.int32, (L, LANES), 1)
    heads_per_blk = LANES // dk
    nt_dims = ((1,), (1,))

    state = [s_scr[h] for h in range(heads)]
    for ck in range(tt // L):
        rows = slice(L * ck, L * ck + L)
        b = bc[rows]
        bl = b[L - 1:L]
        q_in = (p_ref[rows, o_q:o_q + hk] * (dk ** -0.5)) * jnp.exp(b)
        k = p_ref[rows, o_k:o_k + hk]
        k_out = (k * jnp.exp(-b)).astype(BF16)
        k_dec = (k * jnp.exp(bl - b)).astype(BF16)
        dec = jnp.exp(bl)
        q_in = q_in.astype(BF16)
        for h in range(heads):
            blk = slice(LANES * (h // heads_per_blk), LANES * (h // heads_per_blk) + LANES)
            in_head = (lane // dk) == (h % heads_per_blk)
            qm = jnp.where(in_head, q_in[:, blk], jnp.zeros_like(q_in[:, blk]))
            att = _dot(qm, k_out[:, blk], nt_dims)
            att = jnp.where(tril, att, 0.0).astype(BF16)
            vh = p_ref[rows, o_v + dv * h:o_v + dv * h + dv].astype(BF16)
            o = _dot(att, vh) + _dot(qm, state[h].astype(BF16), nt_dims)
            ds_t = _dot(vh, k_dec[:, blk], ((0,), (0,)))
            state[h] = state[h] * dec[:, blk] + ds_t
            og = p_ref[rows, o_og + dv * h:o_og + dv * h + dv]
            res = (_rms(o) * gh_ref[...]) * _silu(og)
            mixed_ref[rows, dv * h:dv * h + dv] = res.astype(mixed_ref.dtype)
    for h in range(heads):
        s_scr[h] = state[h]

    u = p_ref[:, o_cc:o_cc + cw] * p_ref[:, o_ch:o_ch + cw]
    uext_scr[8:8 + tt, :] = u
    y = (wc_ref[0:1, :] * uext_scr[6:6 + tt, :] + wc_ref[1:2, :] * uext_scr[7:7 + tt, :]
         + wc_ref[2:3, :] * u)
    mixed_ref[:, hv:hv + cw] = (p_ref[:, o_cb:o_cb + cw] * y).astype(mixed_ref.dtype)
    tail = uext_scr[tt + 6:tt + 8, :]
    uext_scr[6:8, :] = tail

    @pl.when(ti == n_t - 1)
    def _():
        st_ref[0] = s_scr[...]
        cst_ref[0] = tail


def _gla_conv_prompt(proj, wgu_pad, bgu, g_head, w_conv, *, bsz, t, tt, heads, dk, dv, cw):
    n_t = t // tt
    n_in = proj.shape[1]
    width = heads * dv + cw
    return pl.pallas_call(
        functools.partial(_gla_conv_kernel, tt=tt, n_t=n_t, heads=heads, dk=dk, dv=dv, cw=cw),
        grid=(bsz, n_t),
        in_specs=[pl.BlockSpec((tt, n_in), lambda b, i: (b * n_t + i, 0)),
                  pl.BlockSpec(wgu_pad.shape, lambda b, i: (0, 0)),
                  pl.BlockSpec(bgu.shape, lambda b, i: (0, 0)),
                  pl.BlockSpec(g_head.shape, lambda b, i: (0, 0)),
                  pl.BlockSpec(w_conv.shape, lambda b, i: (0, 0))],
        out_specs=[pl.BlockSpec((tt, width), lambda b, i: (b * n_t + i, 0)),
                   pl.BlockSpec((1, heads, dv, LANES), lambda b, i: (b, 0, 0, 0)),
                   pl.BlockSpec((1, 2, cw), lambda b, i: (b, 0, 0))],
        out_shape=[jax.ShapeDtypeStruct((bsz * t, width), BF16),
                   jax.ShapeDtypeStruct((bsz, heads, dv, LANES), F32),
                   jax.ShapeDtypeStruct((bsz, 2, cw), F32)],
        scratch_shapes=[pltpu.VMEM((heads, dv, LANES), F32), pltpu.VMEM((tt + 8, cw), F32)],
        compiler_params=_params(("parallel", "arbitrary"), VMEM_LIMIT_MB),
    )(proj, wgu_pad, bgu, g_head, w_conv)


def _gla_conv_step_kernel(q_ref, k_ref, v_ref, og_ref, ch_ref, cb_ref, cc_ref, lr_ref,
                          wgu_ref, bgu_ref, gh_ref, wc_ref, s_ref, cbuf_ref,
                          o_ref, y_ref, sn_ref, cn_ref, a_scr, *, dk):
    logit = _mm(wgu_ref[...], lr_ref[...], "f32") + bgu_ref[...]
    a_scr[...] = jnp.exp(_log_sigmoid(logit) / GATE_NORMALIZER)
    v_t = v_ref[...]

    def body(d, acc):
        a = a_scr[pl.ds(d, 1), :]
        kd = k_ref[pl.ds(d, 1), :]
        qd = q_ref[pl.ds(d, 1), :]
        s_new = a * s_ref[0, d] + kd * v_t
        sn_ref[0, d] = s_new
        return acc + (qd * (dk ** -0.5)) * s_new

    o = lax.fori_loop(0, dk, body, jnp.zeros(v_t.shape, F32))
    o = o * lax.rsqrt(jnp.mean(o * o, axis=0, keepdims=True) + EPS) * gh_ref[...]
    o_ref[...] = o * _silu(og_ref[...])
    u = cc_ref[...] * ch_ref[...]
    y = wc_ref[0] * cbuf_ref[0] + wc_ref[1] * cbuf_ref[1] + wc_ref[2] * u
    y_ref[...] = cb_ref[...] * y
    cn_ref[0] = cbuf_ref[1]
    cn_ref[1] = u


def _gla_conv_step(proj_t, wgu_t_pad, bgu_col, gh_col, wc_col, s_t, cbuf_t, *, heads, dk, dv, cw):
    bsz = proj_t.shape[1]
    hk, hv = heads * dk, heads * dv
    cs = cw // heads
    assert dv == LANES and cs == LANES and dk * 2 == LANES
    o_k, o_v, o_og = hk, 2 * hk, 2 * hk + hv
    o_ch = o_og + hv
    o_cb, o_cc, o_lr = o_ch + cw, o_ch + 2 * cw, o_ch + 3 * cw
    row = lambda off, size: (lambda h: (off // size + h, 0))
    blk = lambda size, off: pl.BlockSpec((size, bsz), row(off, size))
    return pl.pallas_call(
        functools.partial(_gla_conv_step_kernel, dk=dk),
        grid=(heads,),
        in_specs=[blk(dk, 0), blk(dk, o_k), blk(dv, o_v), blk(dv, o_og),
                  blk(cs, o_ch), blk(cs, o_cb), blk(cs, o_cc),
                  pl.BlockSpec((LANES, bsz), lambda h: (o_lr // LANES, 0)),
                  pl.BlockSpec((dk, LANES), lambda h: (h, 0)),
                  pl.BlockSpec((dk, 1), lambda h: (h, 0)),
                  pl.BlockSpec((dv, 1), lambda h: (0, 0)),
                  pl.BlockSpec((3, cs, 1), lambda h: (0, h, 0)),
                  pl.BlockSpec((1, dk, dv, bsz), lambda h: (h, 0, 0, 0)),
                  pl.BlockSpec((2, cs, bsz), lambda h: (0, h, 0))],
        out_specs=[pl.BlockSpec((dv, bsz), lambda h: (h, 0)),
                   pl.BlockSpec((cs, bsz), lambda h: (h, 0)),
                   pl.BlockSpec((1, dk, dv, bsz), lambda h: (h, 0, 0, 0)),
                   pl.BlockSpec((2, cs, bsz), lambda h: (0, h, 0))],
        out_shape=[jax.ShapeDtypeStruct((hv, bsz), F32),
                   jax.ShapeDtypeStruct((cw, bsz), F32),
                   jax.ShapeDtypeStruct(s_t.shape, F32),
                   jax.ShapeDtypeStruct(cbuf_t.shape, F32)],
        scratch_shapes=[pltpu.VMEM((dk, bsz), F32)],
        compiler_params=_params(("parallel",)),
    )(proj_t, proj_t, proj_t, proj_t, proj_t, proj_t, proj_t, proj_t,
      wgu_t_pad, bgu_col, gh_col, wc_col, s_t, cbuf_t)


def _s5_scan_kernel(*refs, tt, n_t, nb, half, n_cast):
    u_ref, bb_ref, cm_ref, abr_ref, abi_ref, d_ref = refs[:6]
    y_ref, sre_ref, sim_ref = refs[6 + n_cast:9 + n_cast]
    scr, tb_scr, xr_scr, xi_scr = refs[9 + 2 * n_cast:]
    _cast_passengers(refs[6:6 + n_cast], refs[9 + n_cast:9 + 2 * n_cast])
    ti = pl.program_id(1)
    n_blk = half // LANES

    @pl.when(ti == 0)
    def _():
        xr_scr[...] = jnp.zeros_like(xr_scr)
        xi_scr[...] = jnp.zeros_like(xi_scr)

    for b in range(nb):
        tb_scr[pl.ds(b, tt, stride=nb), :] = u_ref[b].astype(F32)
    for k in range(nb):
        rows = slice(k * tt, (k + 1) * tt)
        bu = _dot(tb_scr[rows, :].astype(BF16), bb_ref[0])
        for cb in range(2 * n_blk):
            scr[cb, rows, :] = bu[:, cb * LANES:(cb + 1) * LANES]
    a_re = [jnp.broadcast_to(abr_ref[0][:, cb * LANES:(cb + 1) * LANES], (nb, LANES))
            for cb in range(n_blk)]
    a_im = [jnp.broadcast_to(abi_ref[0][:, cb * LANES:(cb + 1) * LANES], (nb, LANES))
            for cb in range(n_blk)]

    def body(t, carry):
        rows = pl.ds(pl.multiple_of(t * nb, nb), nb)
        out = []
        for cb in range(n_blk):
            xr, xi = carry[2 * cb], carry[2 * cb + 1]
            nr = (a_re[cb] * xr - a_im[cb] * xi) + scr[cb, rows, :]
            ni = (a_re[cb] * xi + a_im[cb] * xr) + scr[n_blk + cb, rows, :]
            scr[cb, rows, :] = nr
            scr[n_blk + cb, rows, :] = ni
            out += [nr, ni]
        return tuple(out)

    init = []
    for cb in range(n_blk):
        init += [xr_scr[:, cb * LANES:(cb + 1) * LANES], xi_scr[:, cb * LANES:(cb + 1) * LANES]]
    fin = lax.fori_loop(0, tt, body, tuple(init), unroll=4)
    xr = jnp.concatenate([fin[2 * cb] for cb in range(n_blk)], axis=1)
    xi = jnp.concatenate([fin[2 * cb + 1] for cb in range(n_blk)], axis=1)
    xr_scr[...] = xr
    xi_scr[...] = xi
    for k in range(nb):
        rows = slice(k * tt, (k + 1) * tt)
        xs = jnp.concatenate([scr[cb, rows, :] for cb in range(2 * n_blk)], axis=1)
        y = _dot(xs.astype(BF16), cm_ref[0]) + d_ref[0] * tb_scr[rows, :]
        tb_scr[rows, :] = _gelu_tanh(y)
    for b in range(nb):
        y_ref[b] = tb_scr[pl.ds(b, tt, stride=nb), :].astype(y_ref.dtype)

    @pl.when(ti == n_t - 1)
    def _():
        sre_ref[...] = xr
        sim_ref[...] = xi


def _s5_prompt(u3d, bb_blk, c_blk, ab_re, ab_im, d_blk, cast_ws, *, tt):
    nb, t, d = u3d.shape
    n_j = d // LANES
    n_t = t // tt
    half = bb_blk.shape[2] // 2
    w_specs = _passenger_specs(cast_ws, n_j * n_t, lambda j, i: j * n_t + i)
    return pl.pallas_call(
        functools.partial(_s5_scan_kernel, tt=tt, n_t=n_t, nb=nb, half=half, n_cast=len(cast_ws)),
        grid=(n_j, n_t),
        in_specs=[pl.BlockSpec((nb, tt, LANES), lambda j, i: (0, i, j)),
                  pl.BlockSpec((1, LANES, 2 * half), lambda j, i: (j, 0, 0)),
                  pl.BlockSpec((1, 2 * half, LANES), lambda j, i: (j, 0, 0)),
                  pl.BlockSpec((1, 1, half), lambda j, i: (j, 0, 0)),
                  pl.BlockSpec((1, 1, half), lambda j, i: (j, 0, 0)),
                  pl.BlockSpec((1, 1, LANES), lambda j, i: (j, 0, 0))] + w_specs,
        out_specs=[pl.BlockSpec((nb, tt, LANES), lambda j, i: (0, i, j)),
                   pl.BlockSpec((nb, half), lambda j, i: (0, j)),
                   pl.BlockSpec((nb, half), lambda j, i: (0, j))] + w_specs,
        out_shape=[jax.ShapeDtypeStruct((nb, t, d), BF16),
                   jax.ShapeDtypeStruct((nb, n_j * half), F32),
                   jax.ShapeDtypeStruct((nb, n_j * half), F32)]
        + [jax.ShapeDtypeStruct(w.shape, BF16) for w in cast_ws],
        scratch_shapes=[pltpu.VMEM((2 * half // LANES, nb * tt, LANES), F32),
                        pltpu.VMEM((nb * tt, LANES), F32),
                        pltpu.VMEM((nb, half), F32), pltpu.VMEM((nb, half), F32)],
        compiler_params=_params(("arbitrary", "arbitrary"), VMEM_LIMIT_MB),
    )(u3d, bb_blk, c_blk, ab_re, ab_im, d_blk, *cast_ws)


def _s5_step_kernel(u_ref, bb_ref, cm_ref, abr_ref, abi_ref, d_ref, sre_ref, sim_ref,
                    y_ref, nre_ref, nim_ref, *, half):
    u = u_ref[...]
    bu = _mm(u, bb_ref[0], "f32")
    ar, ai = abr_ref[0], abi_ref[0]
    sr, si = sre_ref[...], sim_ref[...]
    nr = (ar * sr - ai * si) + bu[:, 0:half]
    ni = (ar * si + ai * sr) + bu[:, half:2 * half]
    nre_ref[...] = nr
    nim_ref[...] = ni
    y = _mm(jnp.concatenate([nr, ni], axis=1), cm_ref[0], "f32") + d_ref[0] * u
    y_ref[...] = _gelu_tanh(y)


def _s5_step(u2d, bb_blk, c_blk, ab_re, ab_im, d_blk, s_re, s_im):
    bsz, d = u2d.shape
    n_j = d // LANES
    half = bb_blk.shape[2] // 2
    st = pl.BlockSpec((bsz, half), lambda j: (0, j))
    return pl.pallas_call(
        functools.partial(_s5_step_kernel, half=half),
        grid=(n_j,),
        in_specs=[pl.BlockSpec((bsz, LANES), lambda j: (0, j)),
                  pl.BlockSpec((1, LANES, 2 * half), lambda j: (j, 0, 0)),
                  pl.BlockSpec((1, 2 * half, LANES), lambda j: (j, 0, 0)),
                  pl.BlockSpec((1, 1, half), lambda j: (j, 0, 0)),
                  pl.BlockSpec((1, 1, half), lambda j: (j, 0, 0)),
                  pl.BlockSpec((1, 1, LANES), lambda j: (j, 0, 0)),
                  st, st],
        out_specs=[pl.BlockSpec((bsz, LANES), lambda j: (0, j)), st, st],
        out_shape=[jax.ShapeDtypeStruct((bsz, d), F32),
                   jax.ShapeDtypeStruct(s_re.shape, F32),
                   jax.ShapeDtypeStruct(s_im.shape, F32)],
        compiler_params=_params(("parallel",)),
    )(u2d, bb_blk, c_blk, ab_re, ab_im, d_blk, s_re, s_im)


def _s5_zoh(a_re, a_im, log_dt, b_re, b_im):
    dt = jnp.exp(log_dt)[:, None]
    mag = jnp.exp(dt * a_re)
    ab_re, ab_im = mag * jnp.cos(dt * a_im), mag * jnp.sin(dt * a_im)
    den = a_re * a_re + a_im * a_im
    nr, ni = ab_re - 1.0, ab_im
    f_re = (nr * a_re + ni * a_im) / den
    f_im = (ni * a_re - nr * a_im) / den
    bb_re = f_re[..., None] * b_re - f_im[..., None] * b_im
    bb_im = f_re[..., None] * b_im + f_im[..., None] * b_re
    return ab_re, ab_im, bb_re, bb_im


def _s5_step_blocks(ab_re, ab_im, bb_re, bb_im, c_re, c_im, d_skip):
    g, p = ab_re.shape
    cg = S5_CHUNK_GROUPS
    n_j = g // cg
    eye = jnp.eye(cg, dtype=F32)

    def in_blk(m):
        m = m.reshape(n_j, cg, p, S5_GROUP)
        return jnp.einsum("jgpi,gh->jgihp", m, eye).reshape(n_j, cg * S5_GROUP, cg * p)

    def out_blk(m):
        m = m.reshape(n_j, cg, S5_GROUP, p)
        return jnp.einsum("jgip,gh->jgphi", m, eye).reshape(n_j, cg * p, cg * S5_GROUP)

    bb_blk = jnp.concatenate([in_blk(bb_re), in_blk(bb_im)], axis=2)
    c_blk = jnp.concatenate([out_blk(c_re), out_blk(-c_im)], axis=1)
    return (bb_blk, c_blk, ab_re.reshape(n_j, 1, cg * p), ab_im.reshape(n_j, 1, cg * p),
            d_skip.reshape(n_j, 1, cg * S5_GROUP))


def kernel(x_prompt, x_sample, c_prompt, c_sample, state_gla, state_conv, state_s5_re, state_s5_im, w_ada, b_ada, g_norm, w_in0, w_gate_up, b_gate_up, g_head_norm, w_conv, w_out0, w_ffn_gate, w_ffn_up, w_ffn_down, w_in1, s5_a_re, s5_a_im, s5_log_dt, s5_b_re, s5_b_im, s5_c_re, s5_c_im, s5_d, w_glu, w_router, w_exp_gate, w_exp_up, w_exp_down):
    bsz, t, d = x_prompt.shape
    bs = x_sample.shape[0]
    heads, dk, dv = state_gla.shape[2:]
    cw = state_conv.shape[3]
    lowrank = w_gate_up.shape[1]
    hk, hv = heads * dk, heads * dv
    n_experts = w_router.shape[2]
    g_s5, p_s5 = s5_a_re.shape[1:]

    mod = _ada(jnp.concatenate([c_prompt, c_sample], axis=0), w_ada, b_ada)
    mod_p = [mod[l, :bsz].reshape(bsz, 1, N_MOD * d) for l in range(2)]
    mod_s = [mod[l, bsz:] for l in range(2)]
    SH_M, SC_M, GT_M, SH_F, SC_F, GT_F = range(N_MOD)
    gn = lambda l, k: g_norm[l, k].reshape(1, d)

    w0 = w_in0[0]
    o_lr = 2 * hk + 2 * hv
    w0r = jnp.concatenate([w0[:, :o_lr], w0[:, o_lr + lowrank:], w0[:, o_lr:o_lr + lowrank],
                           jnp.zeros((d, LANES - lowrank), F32)], axis=1)
    wgu_pad = jnp.concatenate([w_gate_up[0], jnp.zeros((LANES - lowrank, hk), F32)], axis=0)
    zoh = _s5_zoh(s5_a_re[0], s5_a_im[0], s5_log_dt[0], s5_b_re[0], s5_b_im[0])
    bb_blk, c_blk, ab_re, ab_im, d_blk = _s5_step_blocks(*zoh, s5_c_re[0], s5_c_im[0], s5_d[0])
    wr_pad = jnp.concatenate([w_router[0], jnp.zeros((d, LANES - n_experts), F32)], axis=1)
    bf = lambda a: a.astype(BF16)

    tm = TOKEN_TILE
    tpb = t // tm
    pk = dict(per_token=False, tiles_per_batch=tpb)
    xp = x_prompt.reshape(bsz * t, d)
    proj, w_o0, w_fg, w_fu, w_fd, w_i1, w_gl = _inproj(
        xp, gn(0, 0), mod_p[0], SC_M, SH_M, bf(w0r), tm=tm, bn=w0r.shape[1], prec="bf16",
        vmem_mb=VMEM_LIMIT_MB, cast_ws=(w_out0[0], w_ffn_gate[0], w_ffn_up[0], w_ffn_down[0], w_in1[0], w_glu[0]), **pk)
    mixed, st_t, cst = _gla_conv_prompt(
        proj, bf(wgu_pad), b_gate_up[0].reshape(1, hk), g_head_norm[0].reshape(1, dv), w_conv[0],
        bsz=bsz, t=t, tt=SEQ_TILE, heads=heads, dk=dk, dv=dv, cw=cw)
    st_t = st_t.reshape(bsz, heads // 2, 2, dv, 2, dk)
    gla_p = jnp.stack([st_t[:, :, 0, :, 0, :], st_t[:, :, 1, :, 1, :]], axis=2)
    gla_p = jnp.swapaxes(gla_p.reshape(bsz, heads, dv, dk), 2, 3)
    x2, u = _mix_ffn_proj(mixed, w_o0, xp, mod_p[0], mod_p[1], g_norm, w_fg, w_fu, w_fd, w_i1, tm=tm,
                          tiles_per_batch=tpb, vmem_mb=VMEM_LIMIT_FFN_MB)
    ew = (w_exp_gate[0], w_exp_up[0], w_exp_down[0])
    yg, re_p, im_p, *ew_bf = _s5_prompt(
        u.reshape(bsz, t, d), bf(bb_blk), bf(c_blk), ab_re, ab_im, d_blk,
        tuple(w.reshape(-1, w.shape[2]) for w in ew), tt=SEQ_TILE)
    w_eg, w_eu, w_ed = (wb.reshape(w.shape) for wb, w in zip(ew_bf, ew))
    x3, comb, hb = _glu_route(yg.reshape(bsz * t, d), w_gl, x2, mod_p[1], g_norm, wr_pad, tm=tm,
                              n_experts=n_experts, tiles_per_batch=tpb, vmem_mb=VMEM_LIMIT_MB)

    sk = dict(per_token=True, tiles_per_batch=1)
    xs = x_sample.reshape(bs, d)
    proj_s = _inproj(xs, gn(0, 0), mod_s[0], SC_M, SH_M, w0r, tm=bs, bn=SAMPLE_PROJ_COLS, prec="f32", **sk)
    s_t = jnp.transpose(state_gla[:, 0], (1, 2, 3, 0))
    cbuf_t = jnp.transpose(state_conv[:, 0], (1, 2, 0))
    o_t, y_t, sn_t, cn_t = _gla_conv_step(
        proj_s.T, wgu_pad.T, b_gate_up[0].reshape(hk, 1), g_head_norm[0].reshape(dv, 1),
        w_conv[0].reshape(3, cw, 1), s_t, cbuf_t, heads=heads, dk=dk, dv=dv, cw=cw)
    mixed_s = jnp.concatenate([o_t, y_t], axis=0).T
    gla_s = jnp.transpose(sn_t, (3, 0, 1, 2))
    conv_s = jnp.transpose(cn_t, (2, 0, 1))
    x1s = _outproj(mixed_s, w_out0[0], xs, mod_s[0], GT_M, gn(0, 1), tm=bs, prec="f32", glu=False,
                   **sk)
    x2s = _ffn(x1s, gn(0, 2), mod_s[0], SC_F, SH_F, GT_F, gn(0, 3), w_ffn_gate[0], w_ffn_up[0],
               w_ffn_down[0], tm=bs, fc=SAMPLE_COLS, prec="f32", **sk)
    us = _inproj(x2s, gn(1, 0), mod_s[1], SC_M, SH_M, w_in1[0], tm=bs, bn=SAMPLE_COLS, prec="f32", **sk)
    ygs, re_s, im_s = _s5_step(us, bb_blk, c_blk, ab_re, ab_im, d_blk,
                               state_s5_re[:, 0].reshape(bs, g_s5 * p_s5),
                               state_s5_im[:, 0].reshape(bs, g_s5 * p_s5))
    x3s = _outproj(ygs, w_glu[0], x2s, mod_s[1], GT_M, gn(1, 1), tm=bs, prec="f32", glu=True,
                   vmem_mb=VMEM_LIMIT_MB, **sk)
    comb_s, hb_s = _router(x3s, gn(1, 2), mod_s[1], SC_F, SH_F, wr_pad, tm=bs, n_experts=n_experts,
                           **sk)

    assert bs <= MOE_TB
    n_blk = bsz * t // MOE_TB + 1
    n_tiles_max = -(-(n_blk * _moe_block_rows(n_experts)) // MOE_TM) + n_experts
    pad_rows = lambda a: jnp.concatenate([a, jnp.zeros((MOE_TB - bs, a.shape[1]), a.dtype)], axis=0)
    hb_s, comb_s = pad_rows(hb_s), pad_rows(comb_s)
    seg, gap_pos, gap_len, tile_expert, n_tiles = _moe_schedule((comb, comb_s), n_experts,
                                                                n_tiles_max)
    dest, xs_sorted = _moe_sort(seg, gap_pos, gap_len, hb, comb, hb_s, comb_s,
                                n_tiles_max * MOE_TM, n_experts=n_experts)
    ys_sorted = _experts(tile_expert, n_tiles, xs_sorted, w_eg, w_eu, w_ed, fc=EXPERT_FF_CHUNK)
    x4 = _moe_combine(*seg, ys_sorted, dest, x3, mod_p[1], GT_F, gn(1, 3), blk0=0,
                      n_experts=n_experts, per_token=False, tiles_per_batch=t // MOE_TB)
    x4s = _moe_combine(*seg, ys_sorted, dest, pad_rows(x3s), pad_rows(mod_s[1]), GT_F,
                       gn(1, 3), blk0=n_blk - 1, n_experts=n_experts, per_token=True,
                       tiles_per_batch=1)[:bs]

    return (x4.reshape(bsz, t, d), x4s.reshape(bs, 1, d),
            gla_p[:, None], cst[:, None],
            re_p.reshape(bsz, 1, g_s5, p_s5), im_p.reshape(bsz, 1, g_s5, p_s5),
            gla_s[:, None], conv_s[:, None],
            re_s.reshape(bs, 1, g_s5, p_s5), im_s.reshape(bs, 1, g_s5, p_s5))
```

```python
import functools
import math

import jax
import jax.numpy as jnp
from jax import lax
from jax.experimental import pallas as pl
from jax.experimental.pallas import tpu as pltpu

F32 = jnp.float32
BF16 = jnp.bfloat16
EPS = 1e-6
LANES = 128
GLA_CHUNK = 64
GATE_NORMALIZER = 16.0
N_MOD = 6
S5_GROUP = 16
S5_CHUNK_GROUPS = 8

TOKEN_TILE = 512
SEQ_TILE = 256
VMEM_LIMIT_MB = 48
VMEM_LIMIT_FFN_MB = 56
EXPERT_FF_CHUNK = 1792
SAMPLE_PROJ_COLS = 640
SAMPLE_COLS = 256


def _dot(a, b, dims=None):
    if dims is None:
        return jnp.dot(a, b, preferred_element_type=F32)
    return lax.dot_general(a, b, (dims, ((), ())), preferred_element_type=F32)


def _split(a, parts=2):
    rem = a.astype(F32)
    out = []
    for _ in range(parts - 1):
        piece = rem.astype(BF16)
        out.append(piece)
        rem = rem - piece.astype(F32)
    out.append(rem.astype(BF16))
    return out


def _mm(a, w, prec):
    if prec == "bf16":
        return _dot(a.astype(BF16), w.astype(BF16))
    a1, a2, a3 = _split(a, 3)
    w1, w2, w3 = _split(w, 3)
    small = (_dot(a1, w3) + _dot(a3, w1)) + _dot(a2, w2)
    return _dot(a1, w1) + ((_dot(a1, w2) + _dot(a2, w1)) + small)


def _silu(x):
    return x * jax.nn.sigmoid(x)


def _gelu_tanh(x):
    return 0.5 * x * (1.0 + jnp.tanh(math.sqrt(2.0 / math.pi) * (x + 0.044715 * (x * x * x))))


def _log_sigmoid(x):
    return -(jnp.maximum(-x, 0.0) + jnp.log1p(jnp.exp(-jnp.abs(x))))


def _rms(x):
    return x * lax.rsqrt(jnp.mean(x * x, axis=-1, keepdims=True) + EPS)


def _mod_row(ref):
    return ref[0] if len(ref.shape) == 3 else ref[...]


def _normmod(x, g, sc, sh):
    return (_rms(x) * g) * (1.0 + sc) + sh


def _mod_spec(per_token, tm, d, col, tiles_per_batch):
    if per_token:
        return pl.BlockSpec((tm, d), lambda i, *_: (i, col))
    return pl.BlockSpec((1, 1, d), lambda i, *_: (i // tiles_per_batch, 0, col))


def _params(sem, vmem_mb=None):
    kw = dict(dimension_semantics=sem)
    if vmem_mb is not None:
        kw["vmem_limit_bytes"] = vmem_mb << 20
    return pltpu.CompilerParams(**kw)


def _ada_kernel(c_ref, w_ref, b_ref, o_ref):
    o_ref[0] = _mm(_silu(c_ref[...]), w_ref[0], "f32") + b_ref[0]


def _ada(c_all, w_ada, b_ada):
    depth, d, n6 = w_ada.shape
    rows = c_all.shape[0]
    return pl.pallas_call(
        _ada_kernel,
        grid=(depth, n6 // d),
        in_specs=[pl.BlockSpec((rows, d), lambda l, j: (0, 0)),
                  pl.BlockSpec((1, d, d), lambda l, j: (l, 0, j)),
                  pl.BlockSpec((1, 1, d), lambda l, j: (l, 0, j))],
        out_specs=pl.BlockSpec((1, rows, d), lambda l, j: (l, 0, j)),
        out_shape=jax.ShapeDtypeStruct((depth, rows, n6), F32),
        compiler_params=_params(("parallel", "parallel")),
    )(c_all, w_ada, b_ada.reshape(depth, 1, n6))


BF16_ROWS = 16


def _passenger_specs(ws, n_steps, step):
    specs = []
    for w in ws:
        rows, cols = w.shape
        units = rows // BF16_ROWS
        n_used = max(k for k in range(1, n_steps + 1) if units % k == 0)
        specs.append(pl.BlockSpec(
            (rows // n_used, cols),
            lambda *idx, n_used=n_used: (jnp.minimum(step(*idx), n_used - 1), 0)))
    return specs


def _cast_passengers(in_refs, out_refs):
    for src, dst in zip(in_refs, out_refs):
        dst[...] = src[...].astype(dst.dtype)


def _inproj_kernel(x_ref, g_ref, sc_ref, sh_ref, w_ref, o_ref, *, prec):
    h = _normmod(x_ref[...], g_ref[...], _mod_row(sc_ref), _mod_row(sh_ref))
    o_ref[...] = _mm(h, w_ref[...], prec).astype(o_ref.dtype)


def _inproj(x2d, g, mod, sc_col, sh_col, w, *, tm, bn, prec, per_token, tiles_per_batch):
    n_tok, d = x2d.shape
    n_out = w.shape[1]
    return pl.pallas_call(
        functools.partial(_inproj_kernel, prec=prec),
        grid=(n_tok // tm, n_out // bn),
        in_specs=[pl.BlockSpec((tm, d), lambda i, j: (i, 0)),
                  pl.BlockSpec((1, d), lambda i, j: (0, 0)),
                  _mod_spec(per_token, tm, d, sc_col, tiles_per_batch),
                  _mod_spec(per_token, tm, d, sh_col, tiles_per_batch),
                  pl.BlockSpec((d, bn), lambda i, j: (0, j))],
        out_specs=pl.BlockSpec((tm, bn), lambda i, j: (i, j)),
        out_shape=jax.ShapeDtypeStruct((n_tok, n_out), F32),
        compiler_params=_params(("parallel", "arbitrary")),
    )(x2d, g, mod, mod, w)


def _outproj_kernel(a_ref, w_ref, x_ref, gt_ref, g_ref, o_ref, *, prec, glu):
    z = _mm(a_ref[...], w_ref[...], prec)
    if glu:
        d = z.shape[1] // 2
        z = z[:, :d] * jax.nn.sigmoid(z[:, d:])
    o_ref[...] = x_ref[...] + _mod_row(gt_ref) * (_rms(z) * g_ref[...])


def _outproj(a2d, w, x2d, mod, gt_col, g, *, tm, prec, glu, per_token, tiles_per_batch,
             vmem_mb=None):
    n_tok, d = x2d.shape
    k, n_out = w.shape
    return pl.pallas_call(
        functools.partial(_outproj_kernel, prec=prec, glu=glu),
        grid=(n_tok // tm,),
        in_specs=[pl.BlockSpec((tm, k), lambda i: (i, 0)),
                  pl.BlockSpec((k, n_out), lambda i: (0, 0)),
                  pl.BlockSpec((tm, d), lambda i: (i, 0)),
                  _mod_spec(per_token, tm, d, gt_col, tiles_per_batch),
                  pl.BlockSpec((1, d), lambda i: (0, 0))],
        out_specs=pl.BlockSpec((tm, d), lambda i: (i, 0)),
        out_shape=jax.ShapeDtypeStruct((n_tok, d), F32),
        compiler_params=_params(("parallel",), vmem_mb),
    )(a2d, w, x2d, mod, g)


def _mix_ffn_proj_kernel(a_ref, wo_ref, x_ref, gtm_ref, g01_ref, g02_ref, scf_ref, shf_ref, wg_ref,
                         wu_ref, wd_ref, gtf_ref, g03_ref, g10_ref, sc1_ref, sh1_ref, wi_ref,
                         x2_ref, u_ref):
    x1 = x_ref[...] + _mod_row(gtm_ref) * (_rms(_mm(a_ref[...], wo_ref[...], "bf16")) * g01_ref[...])
    h = _normmod(x1, g02_ref[...], _mod_row(scf_ref), _mod_row(shf_ref))
    act = _silu(_mm(h, wg_ref[...], "bf16")) * _mm(h, wu_ref[...], "bf16")
    x2 = x1 + _mod_row(gtf_ref) * (_rms(_mm(act, wd_ref[...], "bf16")) * g03_ref[...])
    x2_ref[...] = x2
    h1 = _normmod(x2, g10_ref[...], _mod_row(sc1_ref), _mod_row(sh1_ref))
    u_ref[...] = _mm(h1, wi_ref[...], "bf16").astype(u_ref.dtype)


def _mix_ffn_proj(mixed, w_out, x2d, mod0, mod1, g_norm, w_gate, w_up, w_down, w_in1, *, tm,
                  tiles_per_batch, vmem_mb):
    n_tok, d = x2d.shape
    sh_m, sc_m, gt_m, sh_f, sc_f, gt_f = range(N_MOD)
    tok = lambda w: pl.BlockSpec((tm, w), lambda i: (i, 0))
    vec = pl.BlockSpec((1, d), lambda i: (0, 0))
    mod = lambda col: _mod_spec(False, tm, d, col, tiles_per_batch)
    once = lambda w: pl.BlockSpec(w.shape, lambda i: (0, 0), pipeline_mode=pl.Buffered(1))
    g = lambda l, k: g_norm[l, k].reshape(1, d)
    return pl.pallas_call(
        _mix_ffn_proj_kernel,
        grid=(n_tok // tm,),
        in_specs=[tok(mixed.shape[1]), once(w_out), tok(d), mod(gt_m), vec, vec, mod(sc_f), mod(sh_f),
                  once(w_gate), once(w_up), once(w_down), mod(gt_f), vec, vec, mod(sc_m), mod(sh_m),
                  once(w_in1)],
        out_specs=[tok(d), tok(w_in1.shape[1])],
        out_shape=[jax.ShapeDtypeStruct((n_tok, d), F32),
                   jax.ShapeDtypeStruct((n_tok, w_in1.shape[1]), BF16)],
        compiler_params=_params(("parallel",), vmem_mb),
    )(mixed, w_out, x2d, mod0, g(0, 1), g(0, 2), mod0, mod0, w_gate, w_up, w_down, mod0, g(0, 3),
      g(1, 0), mod1, mod1, w_in1)


def _glu_route_kernel(a_ref, w_ref, x_ref, gtm_ref, g1_ref, g2_ref, scf_ref, shf_ref, wr_ref,
                      x3_ref, comb_ref, hb_ref, wp_scr, *, n_experts):
    @pl.when(pl.program_id(0) == 0)
    def _():
        _pack_router_weight(wr_ref, wp_scr, n_experts)

    z = _mm(a_ref[...], w_ref[...], "bf16")
    d = z.shape[1] // 2
    x3 = x_ref[...] + _mod_row(gtm_ref) * (_rms(z[:, :d] * jax.nn.sigmoid(z[:, d:])) * g1_ref[...])
    x3_ref[...] = x3
    h = _normmod(x3, g2_ref[...], _mod_row(scf_ref), _mod_row(shf_ref))
    hb_ref[...] = h.astype(hb_ref.dtype)
    comb_ref[...] = _route(h, wp_scr[...], n_experts)


def _glu_route(a2d, w_glu, x2d, mod1, g_norm, wr_pad, *, tm, n_experts, tiles_per_batch, vmem_mb):
    n_tok, d = x2d.shape
    sh_m, sc_m, gt_m, sh_f, sc_f, gt_f = range(N_MOD)
    tok = lambda w: pl.BlockSpec((tm, w), lambda i: (i, 0))
    vec = pl.BlockSpec((1, d), lambda i: (0, 0))
    mod = lambda col: _mod_spec(False, tm, d, col, tiles_per_batch)
    g = lambda k: g_norm[1, k].reshape(1, d)
    return pl.pallas_call(
        functools.partial(_glu_route_kernel, n_experts=n_experts),
        grid=(n_tok // tm,),
        in_specs=[tok(a2d.shape[1]), pl.BlockSpec(w_glu.shape, lambda i: (0, 0)), tok(d), mod(gt_m),
                  vec, vec, mod(sc_f), mod(sh_f), pl.BlockSpec(wr_pad.shape, lambda i: (0, 0))],
        out_specs=[tok(d), tok(LANES), tok(d)],
        out_shape=[jax.ShapeDtypeStruct((n_tok, d), F32),
                   jax.ShapeDtypeStruct((n_tok, LANES), F32),
                   jax.ShapeDtypeStruct((n_tok, d), BF16)],
        scratch_shapes=[pltpu.VMEM(wr_pad.shape, BF16)],
        compiler_params=_params(("arbitrary",), vmem_mb),
    )(a2d, w_glu, x2d, mod1, g(1), g(2), mod1, mod1, wr_pad)


def _ffn_kernel(x_ref, g1_ref, sc_ref, sh_ref, wg_ref, wu_ref, wd_ref, gt_ref, g2_ref,
                o_ref, h_scr, acc_scr, *, prec, n_c):
    c = pl.program_id(1)

    @pl.when(c == 0)
    def _():
        h = _normmod(x_ref[...], g1_ref[...], _mod_row(sc_ref), _mod_row(sh_ref))
        h_scr[...] = h.astype(h_scr.dtype)
        acc_scr[...] = jnp.zeros_like(acc_scr)

    h = h_scr[...]
    act = _silu(_mm(h, wg_ref[...], prec)) * _mm(h, wu_ref[...], prec)
    acc_scr[...] += _mm(act, wd_ref[...], prec)

    @pl.when(c == n_c - 1)
    def _():
        o_ref[...] = x_ref[...] + _mod_row(gt_ref) * (_rms(acc_scr[...]) * g2_ref[...])


def _ffn(x2d, g1, mod, sc_col, sh_col, gt_col, g2, wg, wu, wd, *, tm, fc, prec,
         per_token, tiles_per_batch, vmem_mb=None):
    n_tok, d = x2d.shape
    n_c = wg.shape[1] // fc
    tok = pl.BlockSpec((tm, d), lambda i, c: (i, 0))
    vec = pl.BlockSpec((1, d), lambda i, c: (0, 0))
    h_dtype = BF16 if prec == "bf16" else F32
    w_mode = dict(pipeline_mode=pl.Buffered(1)) if n_c == 1 else {}
    return pl.pallas_call(
        functools.partial(_ffn_kernel, prec=prec, n_c=n_c),
        grid=(n_tok // tm, n_c),
        in_specs=[tok, vec,
                  _mod_spec(per_token, tm, d, sc_col, tiles_per_batch),
                  _mod_spec(per_token, tm, d, sh_col, tiles_per_batch),
                  pl.BlockSpec((d, fc), lambda i, c: (0, c), **w_mode),
                  pl.BlockSpec((d, fc), lambda i, c: (0, c), **w_mode),
                  pl.BlockSpec((fc, d), lambda i, c: (c, 0), **w_mode),
                  _mod_spec(per_token, tm, d, gt_col, tiles_per_batch), vec],
        out_specs=tok,
        out_shape=jax.ShapeDtypeStruct((n_tok, d), F32),
        scratch_shapes=[pltpu.VMEM((tm, d), h_dtype), pltpu.VMEM((tm, d), F32)],
        compiler_params=_params(("parallel", "arbitrary"), vmem_mb),
    )(x2d, g1, mod, mod, wg, wu, wd, mod, g2)


def _pack_router_weight(wr_ref, wp_scr, n_experts):
    w1, w2, w3 = _split(wr_ref[...], 3)
    fwd = lambda x, k: jnp.concatenate([x[:, LANES - k * n_experts:], x[:, :LANES - k * n_experts]], axis=1)
    packed = w1.astype(F32) + (fwd(w2.astype(F32), 1) + fwd(w3.astype(F32), 2))
    wp_scr[...] = packed.astype(wp_scr.dtype)


def _route(h, w, n_experts):
    h1, h2, h3 = _split(h, 3)
    p1, p2, p3 = _dot(h1, w), _dot(h2, w), _dot(h3, w)
    back = lambda x, k: jnp.concatenate([x[:, k * n_experts:], x[:, :k * n_experts]], axis=1)
    small = (back(p1, 2) + p3) + back(p2, 1)
    logits = p1 + ((back(p1, 1) + p2) + small)
    lane = lax.broadcasted_iota(jnp.int32, logits.shape, 1).astype(F32)
    neg = -jnp.inf
    l1 = jnp.where(lane < n_experts, logits, neg)
    m1 = jnp.max(l1, axis=-1, keepdims=True)
    i1 = jnp.min(jnp.where(l1 == m1, lane, float(LANES)), axis=-1, keepdims=True)
    l2 = jnp.where(lane == i1, neg, l1)
    m2 = jnp.max(l2, axis=-1, keepdims=True)
    i2 = jnp.min(jnp.where(l2 == m2, lane, float(LANES)), axis=-1, keepdims=True)
    e2 = jnp.exp(m2 - m1)
    den = 1.0 + e2
    return jnp.where(lane == i1, 1.0 / den, 0.0) + jnp.where(lane == i2, e2 / den, 0.0)


def _router_kernel(x_ref, g_ref, sc_ref, sh_ref, wr_ref, comb_ref, hb_ref, wp_scr, *, n_experts):
    @pl.when(pl.program_id(0) == 0)
    def _():
        _pack_router_weight(wr_ref, wp_scr, n_experts)

    h = _normmod(x_ref[...], g_ref[...], _mod_row(sc_ref), _mod_row(sh_ref))
    hb_ref[...] = h.astype(hb_ref.dtype)
    comb_ref[...] = _route(h, wp_scr[...], n_experts)


def _router(x2d, g, mod, sc_col, sh_col, wr_pad, *, tm, n_experts, per_token, tiles_per_batch):
    n_tok, d = x2d.shape
    return pl.pallas_call(
        functools.partial(_router_kernel, n_experts=n_experts),
        grid=(n_tok // tm,),
        in_specs=[pl.BlockSpec((tm, d), lambda i: (i, 0)),
                  pl.BlockSpec((1, d), lambda i: (0, 0)),
                  _mod_spec(per_token, tm, d, sc_col, tiles_per_batch),
                  _mod_spec(per_token, tm, d, sh_col, tiles_per_batch),
                  pl.BlockSpec((d, LANES), lambda i: (0, 0))],
        out_specs=[pl.BlockSpec((tm, LANES), lambda i: (i, 0)),
                   pl.BlockSpec((tm, d), lambda i: (i, 0))],
        out_shape=[jax.ShapeDtypeStruct((n_tok, LANES), F32),
                   jax.ShapeDtypeStruct((n_tok, d), BF16)],
        scratch_shapes=[pltpu.VMEM((d, LANES), BF16)],
        compiler_params=_params(("arbitrary",)),
    )(x2d, g, mod, mod, wr_pad)


MOE_TB = 256
MOE_SEG = 16
MOE_TM = 512


def _moe_block_rows(n_experts):
    return -(-(2 * MOE_TB + n_experts * (MOE_SEG - 1)) // LANES) * LANES


def _routing_slots(comb, dest):
    lane = lax.broadcasted_iota(jnp.int32, comb.shape, 1).astype(F32)
    sel = comb > 0.0
    i_lo = jnp.min(jnp.where(sel, lane, float(LANES)), axis=-1, keepdims=True)
    i_hi = jnp.max(jnp.where(sel, lane, -1.0), axis=-1, keepdims=True)
    pick = lambda i, v: jnp.sum(jnp.where(lane == i, v, 0.0), axis=-1, keepdims=True)
    has_lo = i_lo < float(LANES)
    has_hi = i_hi > i_lo
    d_lo = jnp.where(has_lo, pick(i_lo, dest), -1.0)
    d_hi = jnp.where(has_hi, pick(i_hi, dest), -1.0)
    w_lo = pick(i_lo, comb)
    w_hi = jnp.where(has_hi, pick(i_hi, comb), 0.0)
    out = jnp.where(lane == 0.0, d_lo, 0.0)
    for k, v in ((1.0, d_hi), (2.0, w_lo), (3.0, w_hi)):
        out = jnp.where(lane == k, v, out)
    return out


def _one_hot_rows(row, n_rows):
    tb = row.shape[0]
    lane = lax.broadcasted_iota(jnp.int32, (1, LANES), 1).astype(F32)
    row = jnp.broadcast_to(row, (tb, LANES))
    blocks = [jnp.where(row == lane + float(cb * LANES), 1.0, 0.0) for cb in range(n_rows // LANES)]
    return jnp.concatenate(blocks, axis=1).astype(BF16)


_SEG_BITS = (256, 128, 64, 32, 16)


def _segment_copies(make_copy, blk, pos_ref, off_ref, cnt_ref, n_experts, start):
    for e in range(n_experts):
        seg = blk * n_experts + e
        in_blk, in_sorted, n = off_ref[seg], pos_ref[seg], cnt_ref[seg]
        for size in _SEG_BITS:
            has = (n & size) != 0

            @pl.when(has)
            def _(in_blk=in_blk, in_sorted=in_sorted, size=size):
                cp = make_copy(pl.multiple_of(in_blk, MOE_SEG), pl.multiple_of(in_sorted, MOE_SEG), size)
                cp.start() if start else cp.wait()

            step = jnp.where(has, size, 0)
            in_blk, in_sorted = in_blk + step, in_sorted + step


def _moe_sort_kernel(pos_ref, off_ref, cnt_ref, gpos_ref, glen_ref, hbp_ref, combp_ref, hbs_ref,
                     combs_ref, dest_ref, xs_hbm, buf, zbuf, sems, *, n_rows, n_experts, n_blk, n_blk_p):
    b = pl.program_id(0)
    slot = b % 2

    def writes(blk, slot, start):
        def make_copy(block_row, sorted_row, size):
            return pltpu.make_async_copy(buf.at[slot, pl.ds(block_row, size)],
                                         xs_hbm.at[pl.ds(sorted_row, size)], sems.at[slot])
        _segment_copies(make_copy, blk, pos_ref, off_ref, cnt_ref, n_experts, start)

    def zero_gaps(start):
        for e in range(n_experts):
            row, n = gpos_ref[e], glen_ref[e]
            for size in _SEG_BITS:
                has = (n & size) != 0

                @pl.when(has)
                def _(row=row, size=size):
                    cp = pltpu.make_async_copy(
                        zbuf.at[pl.ds(0, size)],
                        xs_hbm.at[pl.ds(pl.multiple_of(row, MOE_SEG), size)], sems.at[2])
                    cp.start() if start else cp.wait()

                row = row + jnp.where(has, size, 0)
        tail_row, tail_n, big = gpos_ref[n_experts], glen_ref[n_experts], _SEG_BITS[0]

        def piece(i, carry):
            cp = pltpu.make_async_copy(
                zbuf.at[pl.ds(0, big)],
                xs_hbm.at[pl.ds(pl.multiple_of(tail_row + i * big, MOE_SEG), big)], sems.at[2])
            cp.start() if start else cp.wait()
            return carry

        lax.fori_loop(0, tail_n // big, piece, 0)

    @pl.when(b == 0)
    def _():
        zbuf[...] = jnp.zeros_like(zbuf)
        zero_gaps(True)

    @pl.when(b >= 2)
    def _():
        writes(b - 2, slot, False)

    in_prompt = b < n_blk_p
    comb = jnp.where(in_prompt, combp_ref[...], combs_ref[...])
    hb = jnp.where(in_prompt, hbp_ref[...], hbs_ref[...])
    tb = comb.shape[0]
    sel = jnp.where(comb > 0.0, 1.0, 0.0)
    r = lax.broadcasted_iota(jnp.int32, (tb, tb), 0)
    c = lax.broadcasted_iota(jnp.int32, (tb, tb), 1)
    rank = _dot(jnp.where(c < r, 1.0, 0.0).astype(BF16), sel.astype(BF16))
    cnt = jnp.sum(sel, axis=0, keepdims=True)
    cnt = jnp.floor((cnt + (MOE_SEG - 1)) / MOE_SEG) * MOE_SEG
    ru = lax.broadcasted_iota(jnp.int32, (LANES, LANES), 0)
    cu = lax.broadcasted_iota(jnp.int32, (LANES, LANES), 1)
    off = _dot(jnp.broadcast_to(cnt, (8, LANES)).astype(BF16),
               jnp.where(ru < cu, 1.0, 0.0).astype(BF16))[0:1]
    slots = _routing_slots(comb, off + rank)
    dest_ref[...] = slots
    pt = _one_hot_rows(slots[:, 0:1], n_rows) + _one_hot_rows(slots[:, 1:2], n_rows)
    buf[slot] = _dot(pt, hb, ((0,), (0,))).astype(buf.dtype)
    writes(b, slot, True)

    @pl.when(b == n_blk - 1)
    def _():
        writes(b, slot, False)
        if n_blk > 1:
            writes(b - 1, 1 - slot, False)
        zero_gaps(False)


def _moe_sort(seg, gap_pos, gap_len, hb_p, comb_p, hb_s, comb_s, n_sorted_rows, *, n_experts):
    n_p, d = hb_p.shape
    n_blk_p = n_p // MOE_TB
    n_blk = n_blk_p + 1
    n_rows = _moe_block_rows(n_experts)
    last_p = n_blk_p - 1
    return pl.pallas_call(
        functools.partial(_moe_sort_kernel, n_rows=n_rows, n_experts=n_experts, n_blk=n_blk,
                          n_blk_p=n_blk_p),
        grid_spec=pltpu.PrefetchScalarGridSpec(
            num_scalar_prefetch=5, grid=(n_blk,),
            in_specs=[pl.BlockSpec((MOE_TB, d), lambda b, *_: (jnp.minimum(b, last_p), 0)),
                      pl.BlockSpec((MOE_TB, LANES), lambda b, *_: (jnp.minimum(b, last_p), 0)),
                      pl.BlockSpec((MOE_TB, d), lambda b, *_: (0, 0)),
                      pl.BlockSpec((MOE_TB, LANES), lambda b, *_: (0, 0))],
            out_specs=[pl.BlockSpec((MOE_TB, LANES), lambda b, *_: (b, 0)),
                       pl.BlockSpec(memory_space=pl.ANY)],
            scratch_shapes=[pltpu.VMEM((2, n_rows, d), BF16), pltpu.VMEM((_SEG_BITS[0], d), BF16),
                            pltpu.SemaphoreType.DMA((3,))]),
        out_shape=[jax.ShapeDtypeStruct((n_blk * MOE_TB, LANES), F32),
                   jax.ShapeDtypeStruct((n_sorted_rows, d), BF16)],
        compiler_params=_params(("arbitrary",)),
    )(*seg, gap_pos, gap_len, hb_p, comb_p, hb_s, comb_s)


def _experts_kernel(te_ref, nt_ref, x_ref, wg_ref, wu_ref, wd_ref, o_ref, acc_scr, *, n_c):
    del te_ref
    k = pl.program_id(0)
    c = pl.program_id(1)
    active = k < nt_ref[0]

    @pl.when(active)
    def _():
        h = x_ref[...]
        act = _silu(_dot(h, wg_ref[0])) * _dot(h, wu_ref[0])
        y = _dot(act.astype(BF16), wd_ref[0])

        @pl.when(c == 0)
        def _():
            acc_scr[...] = y

        @pl.when(c > 0)
        def _():
            acc_scr[...] += y

        @pl.when(c == n_c - 1)
        def _():
            o_ref[...] = acc_scr[...].astype(o_ref.dtype)

    @pl.when(jnp.logical_not(active) & (c == n_c - 1))
    def _():
        o_ref[...] = jnp.zeros_like(o_ref)


def _experts(tile_expert, n_tiles, xs, wg, wu, wd, *, fc):
    n_rows, d = xs.shape
    n_c = wg.shape[2] // fc

    def row_map(k, c, te, nt):
        return (jnp.minimum(k, nt[0] - 1), 0)

    def w_map(k, c, te, nt):
        return (te[k], 0, jnp.where(k < nt[0], c, n_c - 1))

    def wd_map(k, c, te, nt):
        return (te[k], jnp.where(k < nt[0], c, n_c - 1), 0)

    return pl.pallas_call(
        functools.partial(_experts_kernel, n_c=n_c),
        grid_spec=pltpu.PrefetchScalarGridSpec(
            num_scalar_prefetch=2, grid=(n_rows // MOE_TM, n_c),
            in_specs=[pl.BlockSpec((MOE_TM, d), row_map),
                      pl.BlockSpec((1, d, fc), w_map), pl.BlockSpec((1, d, fc), w_map),
                      pl.BlockSpec((1, fc, d), wd_map)],
            out_specs=pl.BlockSpec((MOE_TM, d), lambda k, c, te, nt: (k, 0)),
            scratch_shapes=[pltpu.VMEM((MOE_TM, d), F32)]),
        out_shape=jax.ShapeDtypeStruct((n_rows, d), BF16),
        compiler_params=_params(("arbitrary", "arbitrary"), VMEM_LIMIT_MB),
    )(tile_expert, n_tiles, xs, wg, wu, wd)


def _moe_combine_kernel(pos_ref, off_ref, cnt_ref, ys_hbm, dest_ref, x_ref, gt_ref, g_ref,
                        o_ref, buf, sems, *, n_experts, n_blk, blk0):
    b = pl.program_id(0)
    slot = b % 2

    def reads(blk, slot, start):
        def make_copy(block_row, sorted_row, size):
            return pltpu.make_async_copy(ys_hbm.at[pl.ds(sorted_row, size)],
                                         buf.at[slot, pl.ds(block_row, size)], sems.at[slot])
        _segment_copies(make_copy, blk0 + blk, pos_ref, off_ref, cnt_ref, n_experts, start)

    @pl.when(b == 0)
    def _():
        buf[...] = jnp.zeros_like(buf)
        reads(b, slot, True)

    if n_blk > 1:
        @pl.when(b + 1 < n_blk)
        def _():
            reads(b + 1, 1 - slot, True)

    reads(b, slot, False)
    yb = buf[slot]
    slots = dest_ref[...]
    n_rows = yb.shape[0]
    y = (slots[:, 2:3] * _dot(_one_hot_rows(slots[:, 0:1], n_rows), yb)
         + slots[:, 3:4] * _dot(_one_hot_rows(slots[:, 1:2], n_rows), yb))
    o_ref[...] = x_ref[...] + _mod_row(gt_ref) * (_rms(y) * g_ref[...])


def _moe_combine(sorted_pos, block_off, seg_cnt, ys, dest, x2d, mod, gt_col, g, *, blk0,
                 n_experts, per_token, tiles_per_batch):
    n_tok, d = x2d.shape
    n_blk = n_tok // MOE_TB
    n_rows = _moe_block_rows(n_experts)
    return pl.pallas_call(
        functools.partial(_moe_combine_kernel, n_experts=n_experts, n_blk=n_blk, blk0=blk0),
        grid_spec=pltpu.PrefetchScalarGridSpec(
            num_scalar_prefetch=3, grid=(n_blk,),
            in_specs=[pl.BlockSpec(memory_space=pl.ANY),
                      pl.BlockSpec((MOE_TB, LANES), lambda b, *_: (blk0 + b, 0)),
                      pl.BlockSpec((MOE_TB, d), lambda b, *_: (b, 0)),
                      _mod_spec(per_token, MOE_TB, d, gt_col, tiles_per_batch),
                      pl.BlockSpec((1, d), lambda b, *_: (0, 0))],
            out_specs=pl.BlockSpec((MOE_TB, d), lambda b, *_: (b, 0)),
            scratch_shapes=[pltpu.VMEM((2, n_rows, d), BF16), pltpu.SemaphoreType.DMA((2,))]),
        out_shape=jax.ShapeDtypeStruct((n_tok, d), F32),
        compiler_params=_params(("arbitrary",)),
    )(sorted_pos, block_off, seg_cnt, ys, dest, x2d, mod, g)


def _moe_schedule(combs, n_experts, n_tiles_max):
    def block_counts(comb):
        n_blk = comb.shape[0] // MOE_TB
        sel = (comb[:, :n_experts] > 0.0).astype(jnp.int32)
        return jnp.sum(sel.reshape(n_blk, MOE_TB, n_experts), axis=1)

    cnt = jnp.concatenate([block_counts(c) for c in combs], axis=0)
    cnt = -(-cnt // MOE_SEG) * MOE_SEG
    off = jnp.cumsum(cnt, axis=1) - cnt
    tot = jnp.sum(cnt, axis=0)
    grp = -(-tot // MOE_TM) * MOE_TM
    g_end = jnp.cumsum(grp)
    sorted_pos = (g_end - grp)[None, :] + (jnp.cumsum(cnt, axis=0) - cnt)
    n_tiles = g_end[-1] // MOE_TM
    first_row = jnp.arange(n_tiles_max, dtype=jnp.int32) * MOE_TM
    tile_expert = jnp.sum(first_row[:, None] >= g_end[None, :], axis=1)
    last = jnp.sum((n_tiles - 1) * MOE_TM >= g_end)
    tile_expert = jnp.minimum(tile_expert, last).astype(jnp.int32)
    flat = lambda a: a.reshape(-1).astype(jnp.int32)
    seg = (flat(sorted_pos), flat(off), flat(cnt))
    gap_pos = flat(jnp.concatenate([g_end - grp + tot, g_end[-1:]]))
    gap_len = flat(jnp.concatenate([grp - tot, n_tiles_max * MOE_TM - g_end[-1:]]))
    return seg, gap_pos, gap_len, tile_expert, n_tiles.reshape(1).astype(jnp.int32)


def _proj_gla_conv_kernel(*refs, tt, n_t, heads, dk, dv, cw, n_cast):
    (xf_ref, xn_ref, g_ref, scf_ref, shf_ref, scn_ref, shn_ref, w_ref,
     wgu_ref, bgu_ref, gh_ref, wc_ref) = refs[:12]
    mixed_ref, st_ref, cst_ref = refs[12 + n_cast:15 + n_cast]
    proj_scr, s_scr, uext_scr = refs[15 + 2 * n_cast:]
    _cast_passengers(refs[12:12 + n_cast], refs[15 + n_cast:15 + 2 * n_cast])
    ti = pl.program_id(1)
    lin = pl.program_id(0) * n_t + ti
    slot = lin % 2

    @pl.when(lin == 0)
    def _():
        h0 = _normmod(xf_ref[...], g_ref[...], _mod_row(scf_ref), _mod_row(shf_ref))
        proj_scr[0] = _mm(h0, w_ref[...], "bf16")

    h_next = _normmod(xn_ref[...], g_ref[...], _mod_row(scn_ref), _mod_row(shn_ref))
    proj_scr[1 - slot] = _mm(h_next, w_ref[...], "bf16")
    p_ref = proj_scr.at[slot]
    hk = heads * dk
    hv = heads * dv
    o_q, o_k, o_v, o_og = 0, hk, 2 * hk, 2 * hk + hv
    o_ch = o_og + hv
    o_cb, o_cc, o_lr = o_ch + cw, o_ch + 2 * cw, o_ch + 3 * cw
    L = GLA_CHUNK

    @pl.when(ti == 0)
    def _():
        s_scr[...] = jnp.zeros_like(s_scr)
        uext_scr[0:8, :] = jnp.zeros((8, cw), F32)

    logit = _dot(p_ref[:, o_lr:o_lr + LANES].astype(BF16), wgu_ref[...]) + bgu_ref[...]
    logg = _log_sigmoid(logit) / GATE_NORMALIZER

    r = lax.broadcasted_iota(jnp.int32, (tt, tt), 0)
    c = lax.broadcasted_iota(jnp.int32, (tt, tt), 1)
    tri = jnp.where((r // L == c // L) & (c <= r), 1.0, 0.0).astype(BF16)
    g_hi = logg.astype(BF16)
    rem = logg - g_hi.astype(F32)
    g_mid = rem.astype(BF16)
    g_lo = (rem - g_mid.astype(F32)).astype(BF16)
    bc = _dot(tri, g_hi) + (_dot(tri, g_mid) + _dot(tri, g_lo))

    rl = lax.broadcasted_iota(jnp.int32, (L, L), 0)
    cl = lax.broadcasted_iota(jnp.int32, (L, L), 1)
    tril = cl <= rl
    lane = lax.broadcasted_iota(jnp.int32, (L, LANES), 1)
    heads_per_blk = LANES // dk
    nt_dims = ((1,), (1,))

    state = [s_scr[h] for h in range(heads)]
    for ck in range(tt // L):
        rows = slice(L * ck, L * ck + L)
        b = bc[rows]
        bl = b[L - 1:L]
        q_in = (p_ref[rows, o_q:o_q + hk] * (dk ** -0.5)) * jnp.exp(b)
        k = p_ref[rows, o_k:o_k + hk]
        k_out = (k * jnp.exp(-b)).astype(BF16)
        k_dec = (k * jnp.exp(bl - b)).astype(BF16)
        dec = jnp.exp(bl)
        q_in = q_in.astype(BF16)
        for h in range(heads):
            blk = slice(LANES * (h // heads_per_blk), LANES * (h // heads_per_blk) + LANES)
            in_head = (lane // dk) == (h % heads_per_blk)
            qm = jnp.where(in_head, q_in[:, blk], jnp.zeros_like(q_in[:, blk]))
            att = _dot(qm, k_out[:, blk], nt_dims)
            att = jnp.where(tril, att, 0.0).astype(BF16)
            vh = p_ref[rows, o_v + dv * h:o_v + dv * h + dv].astype(BF16)
            o = _dot(att, vh) + _dot(qm, state[h].astype(BF16), nt_dims)
            ds_t = _dot(vh, k_dec[:, blk], ((0,), (0,)))
            state[h] = state[h] * dec[:, blk] + ds_t
            og = p_ref[rows, o_og + dv * h:o_og + dv * h + dv]
            res = (_rms(o) * gh_ref[...]) * _silu(og)
            mixed_ref[rows, dv * h:dv * h + dv] = res.astype(mixed_ref.dtype)
    for h in range(heads):
        s_scr[h] = state[h]

    u = p_ref[:, o_cc:o_cc + cw] * p_ref[:, o_ch:o_ch + cw]
    uext_scr[8:8 + tt, :] = u
    y = (wc_ref[0:1, :] * uext_scr[6:6 + tt, :] + wc_ref[1:2, :] * uext_scr[7:7 + tt, :]
         + wc_ref[2:3, :] * u)
    mixed_ref[:, hv:hv + cw] = (p_ref[:, o_cb:o_cb + cw] * y).astype(mixed_ref.dtype)
    tail = uext_scr[tt + 6:tt + 8, :]
    uext_scr[6:8, :] = tail

    @pl.when(ti == n_t - 1)
    def _():
        st_ref[0] = s_scr[...]
        cst_ref[0] = tail


def _proj_gla_conv_prompt(x2d, g, mod, sc_col, sh_col, w, wgu_pad, bgu, g_head, w_conv, cast_ws, *,
                          bsz, t, tt, heads, dk, dv, cw):
    n_t = t // tt
    d = x2d.shape[1]
    n_in = w.shape[1]
    width = heads * dv + cw
    last = bsz * n_t - 1
    nxt = lambda b, i: jnp.minimum(b * n_t + i + 1, last)
    const = lambda a: pl.BlockSpec(a.shape, lambda b, i: (0,) * a.ndim)
    p_specs = _passenger_specs(cast_ws, bsz * n_t, lambda b, i: b * n_t + i)
    return pl.pallas_call(
        functools.partial(_proj_gla_conv_kernel, tt=tt, n_t=n_t, heads=heads, dk=dk, dv=dv, cw=cw,
                          n_cast=len(cast_ws)),
        grid=(bsz, n_t),
        in_specs=[pl.BlockSpec((tt, d), lambda b, i: (0, 0)),
                  pl.BlockSpec((tt, d), lambda b, i: (nxt(b, i), 0)),
                  const(g),
                  pl.BlockSpec((1, 1, d), lambda b, i: (0, 0, sc_col)),
                  pl.BlockSpec((1, 1, d), lambda b, i: (0, 0, sh_col)),
                  pl.BlockSpec((1, 1, d), lambda b, i: (nxt(b, i) // n_t, 0, sc_col)),
                  pl.BlockSpec((1, 1, d), lambda b, i: (nxt(b, i) // n_t, 0, sh_col)),
                  pl.BlockSpec(w.shape, lambda b, i: (0, 0), pipeline_mode=pl.Buffered(1)),
                  const(wgu_pad), const(bgu), const(g_head), const(w_conv)] + p_specs,
        out_specs=[pl.BlockSpec((tt, width), lambda b, i: (b * n_t + i, 0)),
                   pl.BlockSpec((1, heads, dv, LANES), lambda b, i: (b, 0, 0, 0)),
                   pl.BlockSpec((1, 2, cw), lambda b, i: (b, 0, 0))] + p_specs,
        out_shape=[jax.ShapeDtypeStruct((bsz * t, width), BF16),
                   jax.ShapeDtypeStruct((bsz, heads, dv, LANES), F32),
                   jax.ShapeDtypeStruct((bsz, 2, cw), F32)]
        + [jax.ShapeDtypeStruct(c.shape, BF16) for c in cast_ws],
        scratch_shapes=[pltpu.VMEM((2, tt, n_in), F32), pltpu.VMEM((heads, dv, LANES), F32),
                        pltpu.VMEM((tt + 8, cw), F32)],
        compiler_params=_params(("arbitrary", "arbitrary"), VMEM_LIMIT_MB),
    )(x2d, x2d, g, mod, mod, mod, mod, w, wgu_pad, bgu, g_head, w_conv, *cast_ws)


def _gla_conv_step_kernel(q_ref, k_ref, v_ref, og_ref, ch_ref, cb_ref, cc_ref, lr_ref,
                          wgu_ref, bgu_ref, gh_ref, wc_ref, s_ref, cbuf_ref,
                          o_ref, y_ref, sn_ref, cn_ref, a_scr, *, dk):
    logit = _mm(wgu_ref[...], lr_ref[...], "f32") + bgu_ref[...]
    a_scr[...] = jnp.exp(_log_sigmoid(logit) / GATE_NORMALIZER)
    v_t = v_ref[...]

    def body(d, acc):
        a = a_scr[pl.ds(d, 1), :]
        kd = k_ref[pl.ds(d, 1), :]
        qd = q_ref[pl.ds(d, 1), :]
        s_new = a * s_ref[0, d] + kd * v_t
        sn_ref[0, d] = s_new
        return acc + (qd * (dk ** -0.5)) * s_new

    o = lax.fori_loop(0, dk, body, jnp.zeros(v_t.shape, F32))
    o = o * lax.rsqrt(jnp.mean(o * o, axis=0, keepdims=True) + EPS) * gh_ref[...]
    o_ref[...] = o * _silu(og_ref[...])
    u = cc_ref[...] * ch_ref[...]
    y = wc_ref[0] * cbuf_ref[0] + wc_ref[1] * cbuf_ref[1] + wc_ref[2] * u
    y_ref[...] = cb_ref[...] * y
    cn_ref[0] = cbuf_ref[1]
    cn_ref[1] = u


def _gla_conv_step(proj_t, wgu_t_pad, bgu_col, gh_col, wc_col, s_t, cbuf_t, *, heads, dk, dv, cw):
    bsz = proj_t.shape[1]
    hk, hv = heads * dk, heads * dv
    cs = cw // heads
    assert dv == LANES and cs == LANES and dk * 2 == LANES
    o_k, o_v, o_og = hk, 2 * hk, 2 * hk + hv
    o_ch = o_og + hv
    o_cb, o_cc, o_lr = o_ch + cw, o_ch + 2 * cw, o_ch + 3 * cw
    row = lambda off, size: (lambda h: (off // size + h, 0))
    blk = lambda size, off: pl.BlockSpec((size, bsz), row(off, size))
    return pl.pallas_call(
        functools.partial(_gla_conv_step_kernel, dk=dk),
        grid=(heads,),
        in_specs=[blk(dk, 0), blk(dk, o_k), blk(dv, o_v), blk(dv, o_og),
                  blk(cs, o_ch), blk(cs, o_cb), blk(cs, o_cc),
                  pl.BlockSpec((LANES, bsz), lambda h: (o_lr // LANES, 0)),
                  pl.BlockSpec((dk, LANES), lambda h: (h, 0)),
                  pl.BlockSpec((dk, 1), lambda h: (h, 0)),
                  pl.BlockSpec((dv, 1), lambda h: (0, 0)),
                  pl.BlockSpec((3, cs, 1), lambda h: (0, h, 0)),
                  pl.BlockSpec((1, dk, dv, bsz), lambda h: (h, 0, 0, 0)),
                  pl.BlockSpec((2, cs, bsz), lambda h: (0, h, 0))],
        out_specs=[pl.BlockSpec((dv, bsz), lambda h: (h, 0)),
                   pl.BlockSpec((cs, bsz), lambda h: (h, 0)),
                   pl.BlockSpec((1, dk, dv, bsz), lambda h: (h, 0, 0, 0)),
                   pl.BlockSpec((2, cs, bsz), lambda h: (0, h, 0))],
        out_shape=[jax.ShapeDtypeStruct((hv, bsz), F32),
                   jax.ShapeDtypeStruct((cw, bsz), F32),
                   jax.ShapeDtypeStruct(s_t.shape, F32),
                   jax.ShapeDtypeStruct(cbuf_t.shape, F32)],
        scratch_shapes=[pltpu.VMEM((dk, bsz), F32)],
        compiler_params=_params(("parallel",)),
    )(proj_t, proj_t, proj_t, proj_t, proj_t, proj_t, proj_t, proj_t,
      wgu_t_pad, bgu_col, gh_col, wc_col, s_t, cbuf_t)


def _s5_scan_kernel(*refs, tt, n_t, nb, half, n_cast):
    u_ref, bb_ref, cm_ref, abr_ref, abi_ref, d_ref = refs[:6]
    y_ref, sre_ref, sim_ref = refs[6 + n_cast:9 + n_cast]
    scr, tb_scr, xr_scr, xi_scr = refs[9 + 2 * n_cast:]
    _cast_passengers(refs[6:6 + n_cast], refs[9 + n_cast:9 + 2 * n_cast])
    ti = pl.program_id(1)
    n_blk = half // LANES

    @pl.when(ti == 0)
    def _():
        xr_scr[...] = jnp.zeros_like(xr_scr)
        xi_scr[...] = jnp.zeros_like(xi_scr)

    for b in range(nb):
        tb_scr[pl.ds(b, tt, stride=nb), :] = u_ref[b].astype(F32)
    for k in range(nb):
        rows = slice(k * tt, (k + 1) * tt)
        bu = _dot(tb_scr[rows, :].astype(BF16), bb_ref[0])
        for cb in range(2 * n_blk):
            scr[cb, rows, :] = bu[:, cb * LANES:(cb + 1) * LANES]
    a_re = [jnp.broadcast_to(abr_ref[0][:, cb * LANES:(cb + 1) * LANES], (nb, LANES))
            for cb in range(n_blk)]
    a_im = [jnp.broadcast_to(abi_ref[0][:, cb * LANES:(cb + 1) * LANES], (nb, LANES))
            for cb in range(n_blk)]

    def body(t, carry):
        rows = pl.ds(pl.multiple_of(t * nb, nb), nb)
        out = []
        for cb in range(n_blk):
            xr, xi = carry[2 * cb], carry[2 * cb + 1]
            nr = (a_re[cb] * xr - a_im[cb] * xi) + scr[cb, rows, :]
            ni = (a_re[cb] * xi + a_im[cb] * xr) + scr[n_blk + cb, rows, :]
            scr[cb, rows, :] = nr
            scr[n_blk + cb, rows, :] = ni
            out += [nr, ni]
        return tuple(out)

    init = []
    for cb in range(n_blk):
        init += [xr_scr[:, cb * LANES:(cb + 1) * LANES], xi_scr[:, cb * LANES:(cb + 1) * LANES]]
    fin = lax.fori_loop(0, tt, body, tuple(init), unroll=4)
    xr = jnp.concatenate([fin[2 * cb] for cb in range(n_blk)], axis=1)
    xi = jnp.concatenate([fin[2 * cb + 1] for cb in range(n_blk)], axis=1)
    xr_scr[...] = xr
    xi_scr[...] = xi
    for k in range(nb):
        rows = slice(k * tt, (k + 1) * tt)
        xs = jnp.concatenate([scr[cb, rows, :] for cb in range(2 * n_blk)], axis=1)
        y = _dot(xs.astype(BF16), cm_ref[0]) + d_ref[0] * tb_scr[rows, :]
        tb_scr[rows, :] = _gelu_tanh(y)
    for b in range(nb):
        y_ref[b] = tb_scr[pl.ds(b, tt, stride=nb), :].astype(y_ref.dtype)

    @pl.when(ti == n_t - 1)
    def _():
        sre_ref[...] = xr
        sim_ref[...] = xi


def _s5_prompt(u3d, bb_blk, c_blk, ab_re, ab_im, d_blk, cast_ws, *, tt):
    nb, t, d = u3d.shape
    n_j = d // LANES
    n_t = t // tt
    half = bb_blk.shape[2] // 2
    w_specs = _passenger_specs(cast_ws, n_j * n_t, lambda j, i: j * n_t + i)
    return pl.pallas_call(
        functools.partial(_s5_scan_kernel, tt=tt, n_t=n_t, nb=nb, half=half, n_cast=len(cast_ws)),
        grid=(n_j, n_t),
        in_specs=[pl.BlockSpec((nb, tt, LANES), lambda j, i: (0, i, j)),
                  pl.BlockSpec((1, LANES, 2 * half), lambda j, i: (j, 0, 0)),
                  pl.BlockSpec((1, 2 * half, LANES), lambda j, i: (j, 0, 0)),
                  pl.BlockSpec((1, 1, half), lambda j, i: (j, 0, 0)),
                  pl.BlockSpec((1, 1, half), lambda j, i: (j, 0, 0)),
                  pl.BlockSpec((1, 1, LANES), lambda j, i: (j, 0, 0))] + w_specs,
        out_specs=[pl.BlockSpec((nb, tt, LANES), lambda j, i: (0, i, j)),
                   pl.BlockSpec((nb, half), lambda j, i: (0, j)),
                   pl.BlockSpec((nb, half), lambda j, i: (0, j))] + w_specs,
        out_shape=[jax.ShapeDtypeStruct((nb, t, d), BF16),
                   jax.ShapeDtypeStruct((nb, n_j * half), F32),
                   jax.ShapeDtypeStruct((nb, n_j * half), F32)]
        + [jax.ShapeDtypeStruct(w.shape, BF16) for w in cast_ws],
        scratch_shapes=[pltpu.VMEM((2 * half // LANES, nb * tt, LANES), F32),
                        pltpu.VMEM((nb * tt, LANES), F32),
                        pltpu.VMEM((nb, half), F32), pltpu.VMEM((nb, half), F32)],
        compiler_params=_params(("arbitrary", "arbitrary"), VMEM_LIMIT_MB),
    )(u3d, bb_blk, c_blk, ab_re, ab_im, d_blk, *cast_ws)


def _s5_step_kernel(u_ref, bb_ref, cm_ref, abr_ref, abi_ref, d_ref, sre_ref, sim_ref,
                    y_ref, nre_ref, nim_ref, *, half):
    u = u_ref[...]
    bu = _mm(u, bb_ref[0], "f32")
    ar, ai = abr_ref[0], abi_ref[0]
    sr, si = sre_ref[...], sim_ref[...]
    nr = (ar * sr - ai * si) + bu[:, 0:half]
    ni = (ar * si + ai * sr) + bu[:, half:2 * half]
    nre_ref[...] = nr
    nim_ref[...] = ni
    y = _mm(jnp.concatenate([nr, ni], axis=1), cm_ref[0], "f32") + d_ref[0] * u
    y_ref[...] = _gelu_tanh(y)


def _s5_step(u2d, bb_blk, c_blk, ab_re, ab_im, d_blk, s_re, s_im):
    bsz, d = u2d.shape
    n_j = d // LANES
    half = bb_blk.shape[2] // 2
    st = pl.BlockSpec((bsz, half), lambda j: (0, j))
    return pl.pallas_call(
        functools.partial(_s5_step_kernel, half=half),
        grid=(n_j,),
        in_specs=[pl.BlockSpec((bsz, LANES), lambda j: (0, j)),
                  pl.BlockSpec((1, LANES, 2 * half), lambda j: (j, 0, 0)),
                  pl.BlockSpec((1, 2 * half, LANES), lambda j: (j, 0, 0)),
                  pl.BlockSpec((1, 1, half), lambda j: (j, 0, 0)),
                  pl.BlockSpec((1, 1, half), lambda j: (j, 0, 0)),
                  pl.BlockSpec((1, 1, LANES), lambda j: (j, 0, 0)),
                  st, st],
        out_specs=[pl.BlockSpec((bsz, LANES), lambda j: (0, j)), st, st],
        out_shape=[jax.ShapeDtypeStruct((bsz, d), F32),
                   jax.ShapeDtypeStruct(s_re.shape, F32),
                   jax.ShapeDtypeStruct(s_im.shape, F32)],
        compiler_params=_params(("parallel",)),
    )(u2d, bb_blk, c_blk, ab_re, ab_im, d_blk, s_re, s_im)


def _s5_zoh(a_re, a_im, log_dt, b_re, b_im):
    dt = jnp.exp(log_dt)[:, None]
    mag = jnp.exp(dt * a_re)
    ab_re, ab_im = mag * jnp.cos(dt * a_im), mag * jnp.sin(dt * a_im)
    den = a_re * a_re + a_im * a_im
    nr, ni = ab_re - 1.0, ab_im
    f_re = (nr * a_re + ni * a_im) / den
    f_im = (ni * a_re - nr * a_im) / den
    bb_re = f_re[..., None] * b_re - f_im[..., None] * b_im
    bb_im = f_re[..., None] * b_im + f_im[..., None] * b_re
    return ab_re, ab_im, bb_re, bb_im


def _s5_step_blocks(ab_re, ab_im, bb_re, bb_im, c_re, c_im, d_skip):
    g, p = ab_re.shape
    cg = S5_CHUNK_GROUPS
    n_j = g // cg
    eye = jnp.eye(cg, dtype=F32)

    def in_blk(m):
        m = m.reshape(n_j, cg, p, S5_GROUP)
        return jnp.einsum("jgpi,gh->jgihp", m, eye).reshape(n_j, cg * S5_GROUP, cg * p)

    def out_blk(m):
        m = m.reshape(n_j, cg, S5_GROUP, p)
        return jnp.einsum("jgip,gh->jgphi", m, eye).reshape(n_j, cg * p, cg * S5_GROUP)

    bb_blk = jnp.concatenate([in_blk(bb_re), in_blk(bb_im)], axis=2)
    c_blk = jnp.concatenate([out_blk(c_re), out_blk(-c_im)], axis=1)
    return (bb_blk, c_blk, ab_re.reshape(n_j, 1, cg * p), ab_im.reshape(n_j, 1, cg * p),
            d_skip.reshape(n_j, 1, cg * S5_GROUP))


def kernel(x_prompt, x_sample, c_prompt, c_sample, state_gla, state_conv, state_s5_re, state_s5_im, w_ada, b_ada, g_norm, w_in0, w_gate_up, b_gate_up, g_head_norm, w_conv, w_out0, w_ffn_gate, w_ffn_up, w_ffn_down, w_in1, s5_a_re, s5_a_im, s5_log_dt, s5_b_re, s5_b_im, s5_c_re, s5_c_im, s5_d, w_glu, w_router, w_exp_gate, w_exp_up, w_exp_down):
    bsz, t, d = x_prompt.shape
    bs = x_sample.shape[0]
    heads, dk, dv = state_gla.shape[2:]
    cw = state_conv.shape[3]
    lowrank = w_gate_up.shape[1]
    hk, hv = heads * dk, heads * dv
    n_experts = w_router.shape[2]
    g_s5, p_s5 = s5_a_re.shape[1:]

    mod = _ada(jnp.concatenate([c_prompt, c_sample], axis=0), w_ada, b_ada)
    mod_p = [mod[l, :bsz].reshape(bsz, 1, N_MOD * d) for l in range(2)]
    mod_s = [mod[l, bsz:] for l in range(2)]
    SH_M, SC_M, GT_M, SH_F, SC_F, GT_F = range(N_MOD)
    gn = lambda l, k: g_norm[l, k].reshape(1, d)

    w0 = w_in0[0]
    o_lr = 2 * hk + 2 * hv
    w0r = jnp.concatenate([w0[:, :o_lr], w0[:, o_lr + lowrank:], w0[:, o_lr:o_lr + lowrank],
                           jnp.zeros((d, LANES - lowrank), F32)], axis=1)
    wgu_pad = jnp.concatenate([w_gate_up[0], jnp.zeros((LANES - lowrank, hk), F32)], axis=0)
    zoh = _s5_zoh(s5_a_re[0], s5_a_im[0], s5_log_dt[0], s5_b_re[0], s5_b_im[0])
    bb_blk, c_blk, ab_re, ab_im, d_blk = _s5_step_blocks(*zoh, s5_c_re[0], s5_c_im[0], s5_d[0])
    wr_pad = jnp.concatenate([w_router[0], jnp.zeros((d, LANES - n_experts), F32)], axis=1)
    bf = lambda a: a.astype(BF16)

    tm = TOKEN_TILE
    tpb = t // tm
    pk = dict(per_token=False, tiles_per_batch=tpb)
    xp = x_prompt.reshape(bsz * t, d)
    mixed, st_t, cst, w_o0, w_fg, w_fu, w_fd, w_i1, w_gl = _proj_gla_conv_prompt(
        xp, gn(0, 0), mod_p[0], SC_M, SH_M, bf(w0r), bf(wgu_pad), b_gate_up[0].reshape(1, hk),
        g_head_norm[0].reshape(1, dv), w_conv[0],
        (w_out0[0], w_ffn_gate[0], w_ffn_up[0], w_ffn_down[0], w_in1[0], w_glu[0]),
        bsz=bsz, t=t, tt=SEQ_TILE, heads=heads, dk=dk, dv=dv, cw=cw)
    st_t = st_t.reshape(bsz, heads // 2, 2, dv, 2, dk)
    gla_p = jnp.stack([st_t[:, :, 0, :, 0, :], st_t[:, :, 1, :, 1, :]], axis=2)
    gla_p = jnp.swapaxes(gla_p.reshape(bsz, heads, dv, dk), 2, 3)
    x2, u = _mix_ffn_proj(mixed, w_o0, xp, mod_p[0], mod_p[1], g_norm, w_fg, w_fu, w_fd, w_i1, tm=tm,
                          tiles_per_batch=tpb, vmem_mb=VMEM_LIMIT_FFN_MB)
    ew = (w_exp_gate[0], w_exp_up[0], w_exp_down[0])
    yg, re_p, im_p, *ew_bf = _s5_prompt(
        u.reshape(bsz, t, d), bf(bb_blk), bf(c_blk), ab_re, ab_im, d_blk,
        tuple(w.reshape(-1, w.shape[2]) for w in ew), tt=SEQ_TILE)
    w_eg, w_eu, w_ed = (wb.reshape(w.shape) for wb, w in zip(ew_bf, ew))
    x3, comb, hb = _glu_route(yg.reshape(bsz * t, d), w_gl, x2, mod_p[1], g_norm, wr_pad, tm=tm,
                              n_experts=n_experts, tiles_per_batch=tpb, vmem_mb=VMEM_LIMIT_MB)

    sk = dict(per_token=True, tiles_per_batch=1)
    xs = x_sample.reshape(bs, d)
    proj_s = _inproj(xs, gn(0, 0), mod_s[0], SC_M, SH_M, w0r, tm=bs, bn=SAMPLE_PROJ_COLS, prec="f32", **sk)
    s_t = jnp.transpose(state_gla[:, 0], (1, 2, 3, 0))
    cbuf_t = jnp.transpose(state_conv[:, 0], (1, 2, 0))
    o_t, y_t, sn_t, cn_t = _gla_conv_step(
        proj_s.T, wgu_pad.T, b_gate_up[0].reshape(hk, 1), g_head_norm[0].reshape(dv, 1),
        w_conv[0].reshape(3, cw, 1), s_t, cbuf_t, heads=heads, dk=dk, dv=dv, cw=cw)
    mixed_s = jnp.concatenate([o_t, y_t], axis=0).T
    gla_s = jnp.transpose(sn_t, (3, 0, 1, 2))
    conv_s = jnp.transpose(cn_t, (2, 0, 1))
    x1s = _outproj(mixed_s, w_out0[0], xs, mod_s[0], GT_M, gn(0, 1), tm=bs, prec="f32", glu=False,
                   **sk)
    x2s = _ffn(x1s, gn(0, 2), mod_s[0], SC_F, SH_F, GT_F, gn(0, 3), w_ffn_gate[0], w_ffn_up[0],
               w_ffn_down[0], tm=bs, fc=SAMPLE_COLS, prec="f32", **sk)
    us = _inproj(x2s, gn(1, 0), mod_s[1], SC_M, SH_M, w_in1[0], tm=bs, bn=SAMPLE_COLS, prec="f32", **sk)
    ygs, re_s, im_s = _s5_step(us, bb_blk, c_blk, ab_re, ab_im, d_blk,
                               state_s5_re[:, 0].reshape(bs, g_s5 * p_s5),
                               state_s5_im[:, 0].reshape(bs, g_s5 * p_s5))
    x3s = _outproj(ygs, w_glu[0], x2s, mod_s[1], GT_M, gn(1, 1), tm=bs, prec="f32", glu=True,
                   vmem_mb=VMEM_LIMIT_MB, **sk)
    comb_s, hb_s = _router(x3s, gn(1, 2), mod_s[1], SC_F, SH_F, wr_pad, tm=bs, n_experts=n_experts,
                           **sk)

    assert bs <= MOE_TB
    n_blk = bsz * t // MOE_TB + 1
    n_tiles_max = -(-(n_blk * _moe_block_rows(n_experts)) // MOE_TM) + n_experts
    pad_rows = lambda a: jnp.concatenate([a, jnp.zeros((MOE_TB - bs, a.shape[1]), a.dtype)], axis=0)
    hb_s, comb_s = pad_rows(hb_s), pad_rows(comb_s)
    seg, gap_pos, gap_len, tile_expert, n_tiles = _moe_schedule((comb, comb_s), n_experts,
                                                                n_tiles_max)
    dest, xs_sorted = _moe_sort(seg, gap_pos, gap_len, hb, comb, hb_s, comb_s,
                                n_tiles_max * MOE_TM, n_experts=n_experts)
    ys_sorted = _experts(tile_expert, n_tiles, xs_sorted, w_eg, w_eu, w_ed, fc=EXPERT_FF_CHUNK)
    x4 = _moe_combine(*seg, ys_sorted, dest, x3, mod_p[1], GT_F, gn(1, 3), blk0=0,
                      n_experts=n_experts, per_token=False, tiles_per_batch=t // MOE_TB)
    x4s = _moe_combine(*seg, ys_sorted, dest, pad_rows(x3s), pad_rows(mod_s[1]), GT_F,
                       gn(1, 3), blk0=n_blk - 1, n_experts=n_experts, per_token=True,
                       tiles_per_batch=1)[:bs]

    return (x4.reshape(bsz, t, d), x4s.reshape(bs, 1, d),
            gla_p[:, None], cst[:, None],
            re_p.reshape(bsz, 1, g_s5, p_s5), im_p.reshape(bsz, 1, g_s5, p_s5),
            gla_s[:, None], conv_s[:, None],
            re_s.reshape(bs, 1, g_s5, p_s5), im_s.reshape(bs, 1, g_s5, p_s5))
```

```python
import functools
import math

import jax
import jax.numpy as jnp
from jax import lax
from jax.experimental import pallas as pl
from jax.experimental.pallas import tpu as pltpu

F32 = jnp.float32
BF16 = jnp.bfloat16
EPS = 1e-6
LANES = 128
GLA_CHUNK = 64
GATE_NORMALIZER = 16.0
N_MOD = 6
S5_GROUP = 16
S5_CHUNK_GROUPS = 8

TOKEN_TILE = 512
SEQ_TILE = 256
VMEM_LIMIT_MB = 48
VMEM_LIMIT_FFN_MB = 56
EXPERT_FF_CHUNK = 1792
SAMPLE_PROJ_COLS = 640
SAMPLE_COLS = 256


def _dot(a, b, dims=None):
    if dims is None:
        return jnp.dot(a, b, preferred_element_type=F32)
    return lax.dot_general(a, b, (dims, ((), ())), preferred_element_type=F32)


def _split(a, parts=2):
    rem = a.astype(F32)
    out = []
    for _ in range(parts - 1):
        piece = rem.astype(BF16)
        out.append(piece)
        rem = rem - piece.astype(F32)
    out.append(rem.astype(BF16))
    return out


def _mm(a, w, prec):
    if prec == "bf16":
        return _dot(a.astype(BF16), w.astype(BF16))
    a1, a2, a3 = _split(a, 3)
    w1, w2, w3 = _split(w, 3)
    small = (_dot(a1, w3) + _dot(a3, w1)) + _dot(a2, w2)
    return _dot(a1, w1) + ((_dot(a1, w2) + _dot(a2, w1)) + small)


def _silu(x):
    return x * jax.nn.sigmoid(x)


def _gelu_tanh(x):
    return 0.5 * x * (1.0 + jnp.tanh(math.sqrt(2.0 / math.pi) * (x + 0.044715 * (x * x * x))))


def _log_sigmoid(x):
    return -(jnp.maximum(-x, 0.0) + jnp.log1p(jnp.exp(-jnp.abs(x))))


def _rms(x):
    return x * lax.rsqrt(jnp.mean(x * x, axis=-1, keepdims=True) + EPS)


def _mod_row(ref):
    return ref[0] if len(ref.shape) == 3 else ref[...]


def _normmod(x, g, sc, sh):
    return (_rms(x) * g) * (1.0 + sc) + sh


def _mod_spec(per_token, tm, d, col, tiles_per_batch):
    if per_token:
        return pl.BlockSpec((tm, d), lambda i, *_: (i, col))
    return pl.BlockSpec((1, 1, d), lambda i, *_: (i // tiles_per_batch, 0, col))


def _params(sem, vmem_mb=None):
    kw = dict(dimension_semantics=sem)
    if vmem_mb is not None:
        kw["vmem_limit_bytes"] = vmem_mb << 20
    return pltpu.CompilerParams(**kw)


def _ada_kernel(c_ref, w_ref, b_ref, o_ref):
    o_ref[0] = _mm(_silu(c_ref[...]), w_ref[0], "f32") + b_ref[0]


def _ada(c_all, w_ada, b_ada):
    depth, d, n6 = w_ada.shape
    rows = c_all.shape[0]
    return pl.pallas_call(
        _ada_kernel,
        grid=(depth, n6 // d),
        in_specs=[pl.BlockSpec((rows, d), lambda l, j: (0, 0)),
                  pl.BlockSpec((1, d, d), lambda l, j: (l, 0, j)),
                  pl.BlockSpec((1, 1, d), lambda l, j: (l, 0, j))],
        out_specs=pl.BlockSpec((1, rows, d), lambda l, j: (l, 0, j)),
        out_shape=jax.ShapeDtypeStruct((depth, rows, n6), F32),
        compiler_params=_params(("parallel", "parallel")),
    )(c_all, w_ada, b_ada.reshape(depth, 1, n6))


BF16_ROWS = 16


def _passenger_specs(ws, n_steps, step):
    specs = []
    for w in ws:
        rows, cols = w.shape
        units = rows // BF16_ROWS
        n_used = max(k for k in range(1, n_steps + 1) if units % k == 0)
        specs.append(pl.BlockSpec(
            (rows // n_used, cols),
            lambda *idx, n_used=n_used: (jnp.minimum(step(*idx), n_used - 1), 0)))
    return specs


def _cast_passengers(in_refs, out_refs):
    for src, dst in zip(in_refs, out_refs):
        dst[...] = src[...].astype(dst.dtype)


def _inproj_kernel(*refs, prec, n_cast):
    x_ref, g_ref, sc_ref, sh_ref, w_ref = refs[:5]
    o_ref = refs[5 + n_cast]
    _cast_passengers(refs[5:5 + n_cast], refs[6 + n_cast:])
    h = _normmod(x_ref[...], g_ref[...], _mod_row(sc_ref), _mod_row(sh_ref))
    o_ref[...] = _mm(h, w_ref[...], prec).astype(o_ref.dtype)


def _inproj(x2d, g, mod, sc_col, sh_col, w, *, tm, bn, prec, per_token, tiles_per_batch,
            out_dtype=F32, vmem_mb=None, cast_ws=()):
    n_tok, d = x2d.shape
    n_out = w.shape[1]
    n_j = n_out // bn
    p_specs = _passenger_specs(cast_ws, n_tok // tm * n_j, lambda i, j: i * n_j + j)
    out = pl.pallas_call(
        functools.partial(_inproj_kernel, prec=prec, n_cast=len(cast_ws)),
        grid=(n_tok // tm, n_j),
        in_specs=[pl.BlockSpec((tm, d), lambda i, j: (i, 0)),
                  pl.BlockSpec((1, d), lambda i, j: (0, 0)),
                  _mod_spec(per_token, tm, d, sc_col, tiles_per_batch),
                  _mod_spec(per_token, tm, d, sh_col, tiles_per_batch),
                  pl.BlockSpec((d, bn), lambda i, j: (0, j))] + p_specs,
        out_specs=[pl.BlockSpec((tm, bn), lambda i, j: (i, j))] + p_specs,
        out_shape=[jax.ShapeDtypeStruct((n_tok, n_out), out_dtype)]
        + [jax.ShapeDtypeStruct(c.shape, BF16) for c in cast_ws],
        compiler_params=_params(("arbitrary", "arbitrary"), vmem_mb),
    )(x2d, g, mod, mod, w, *cast_ws)
    return out if cast_ws else out[0]


def _outproj_kernel(a_ref, w_ref, x_ref, gt_ref, g_ref, o_ref, *, prec, glu):
    z = _mm(a_ref[...], w_ref[...], prec)
    if glu:
        d = z.shape[1] // 2
        z = z[:, :d] * jax.nn.sigmoid(z[:, d:])
    o_ref[...] = x_ref[...] + _mod_row(gt_ref) * (_rms(z) * g_ref[...])


def _outproj(a2d, w, x2d, mod, gt_col, g, *, tm, prec, glu, per_token, tiles_per_batch,
             vmem_mb=None):
    n_tok, d = x2d.shape
    k, n_out = w.shape
    return pl.pallas_call(
        functools.partial(_outproj_kernel, prec=prec, glu=glu),
        grid=(n_tok // tm,),
        in_specs=[pl.BlockSpec((tm, k), lambda i: (i, 0)),
                  pl.BlockSpec((k, n_out), lambda i: (0, 0)),
                  pl.BlockSpec((tm, d), lambda i: (i, 0)),
                  _mod_spec(per_token, tm, d, gt_col, tiles_per_batch),
                  pl.BlockSpec((1, d), lambda i: (0, 0))],
        out_specs=pl.BlockSpec((tm, d), lambda i: (i, 0)),
        out_shape=jax.ShapeDtypeStruct((n_tok, d), F32),
        compiler_params=_params(("parallel",), vmem_mb),
    )(a2d, w, x2d, mod, g)


def _mix_ffn_proj_kernel(a_ref, wo_ref, x_ref, gtm_ref, g01_ref, g02_ref, scf_ref, shf_ref, wg_ref,
                         wu_ref, wd_ref, gtf_ref, g03_ref, g10_ref, sc1_ref, sh1_ref, wi_ref,
                         x2_ref, u_ref):
    x1 = x_ref[...] + _mod_row(gtm_ref) * (_rms(_mm(a_ref[...], wo_ref[...], "bf16")) * g01_ref[...])
    h = _normmod(x1, g02_ref[...], _mod_row(scf_ref), _mod_row(shf_ref))
    act = _silu(_mm(h, wg_ref[...], "bf16")) * _mm(h, wu_ref[...], "bf16")
    x2 = x1 + _mod_row(gtf_ref) * (_rms(_mm(act, wd_ref[...], "bf16")) * g03_ref[...])
    x2_ref[...] = x2
    h1 = _normmod(x2, g10_ref[...], _mod_row(sc1_ref), _mod_row(sh1_ref))
    u_ref[...] = _mm(h1, wi_ref[...], "bf16").astype(u_ref.dtype)


def _mix_ffn_proj(mixed, w_out, x2d, mod0, mod1, g_norm, w_gate, w_up, w_down, w_in1, *, tm,
                  tiles_per_batch, vmem_mb):
    n_tok, d = x2d.shape
    sh_m, sc_m, gt_m, sh_f, sc_f, gt_f = range(N_MOD)
    tok = lambda w: pl.BlockSpec((tm, w), lambda i: (i, 0))
    vec = pl.BlockSpec((1, d), lambda i: (0, 0))
    mod = lambda col: _mod_spec(False, tm, d, col, tiles_per_batch)
    once = lambda w: pl.BlockSpec(w.shape, lambda i: (0, 0), pipeline_mode=pl.Buffered(1))
    g = lambda l, k: g_norm[l, k].reshape(1, d)
    return pl.pallas_call(
        _mix_ffn_proj_kernel,
        grid=(n_tok // tm,),
        in_specs=[tok(mixed.shape[1]), once(w_out), tok(d), mod(gt_m), vec, vec, mod(sc_f), mod(sh_f),
                  once(w_gate), once(w_up), once(w_down), mod(gt_f), vec, vec, mod(sc_m), mod(sh_m),
                  once(w_in1)],
        out_specs=[tok(d), tok(w_in1.shape[1])],
        out_shape=[jax.ShapeDtypeStruct((n_tok, d), F32),
                   jax.ShapeDtypeStruct((n_tok, w_in1.shape[1]), BF16)],
        compiler_params=_params(("parallel",), vmem_mb),
    )(mixed, w_out, x2d, mod0, g(0, 1), g(0, 2), mod0, mod0, w_gate, w_up, w_down, mod0, g(0, 3),
      g(1, 0), mod1, mod1, w_in1)


def _glu_route_kernel(a_ref, w_ref, x_ref, gtm_ref, g1_ref, g2_ref, scf_ref, shf_ref, wr_ref,
                      x3_ref, comb_ref, hb_ref, wp_scr, *, n_experts):
    @pl.when(pl.program_id(0) == 0)
    def _():
        _pack_router_weight(wr_ref, wp_scr, n_experts)

    z = _mm(a_ref[...], w_ref[...], "bf16")
    d = z.shape[1] // 2
    x3 = x_ref[...] + _mod_row(gtm_ref) * (_rms(z[:, :d] * jax.nn.sigmoid(z[:, d:])) * g1_ref[...])
    x3_ref[...] = x3
    h = _normmod(x3, g2_ref[...], _mod_row(scf_ref), _mod_row(shf_ref))
    hb_ref[...] = h.astype(hb_ref.dtype)
    comb_ref[...] = _route(h, wp_scr[...], n_experts)


def _glu_route(a2d, w_glu, x2d, mod1, g_norm, wr_pad, *, tm, n_experts, tiles_per_batch, vmem_mb):
    n_tok, d = x2d.shape
    sh_m, sc_m, gt_m, sh_f, sc_f, gt_f = range(N_MOD)
    tok = lambda w: pl.BlockSpec((tm, w), lambda i: (i, 0))
    vec = pl.BlockSpec((1, d), lambda i: (0, 0))
    mod = lambda col: _mod_spec(False, tm, d, col, tiles_per_batch)
    g = lambda k: g_norm[1, k].reshape(1, d)
    return pl.pallas_call(
        functools.partial(_glu_route_kernel, n_experts=n_experts),
        grid=(n_tok // tm,),
        in_specs=[tok(a2d.shape[1]), pl.BlockSpec(w_glu.shape, lambda i: (0, 0)), tok(d), mod(gt_m),
                  vec, vec, mod(sc_f), mod(sh_f), pl.BlockSpec(wr_pad.shape, lambda i: (0, 0))],
        out_specs=[tok(d), tok(LANES), tok(d)],
        out_shape=[jax.ShapeDtypeStruct((n_tok, d), F32),
                   jax.ShapeDtypeStruct((n_tok, LANES), F32),
                   jax.ShapeDtypeStruct((n_tok, d), BF16)],
        scratch_shapes=[pltpu.VMEM(wr_pad.shape, BF16)],
        compiler_params=_params(("arbitrary",), vmem_mb),
    )(a2d, w_glu, x2d, mod1, g(1), g(2), mod1, mod1, wr_pad)


def _ffn_kernel(x_ref, g1_ref, sc_ref, sh_ref, wg_ref, wu_ref, wd_ref, gt_ref, g2_ref,
                o_ref, h_scr, acc_scr, *, prec, n_c):
    c = pl.program_id(1)

    @pl.when(c == 0)
    def _():
        h = _normmod(x_ref[...], g1_ref[...], _mod_row(sc_ref), _mod_row(sh_ref))
        h_scr[...] = h.astype(h_scr.dtype)
        acc_scr[...] = jnp.zeros_like(acc_scr)

    h = h_scr[...]
    act = _silu(_mm(h, wg_ref[...], prec)) * _mm(h, wu_ref[...], prec)
    acc_scr[...] += _mm(act, wd_ref[...], prec)

    @pl.when(c == n_c - 1)
    def _():
        o_ref[...] = x_ref[...] + _mod_row(gt_ref) * (_rms(acc_scr[...]) * g2_ref[...])


def _ffn(x2d, g1, mod, sc_col, sh_col, gt_col, g2, wg, wu, wd, *, tm, fc, prec,
         per_token, tiles_per_batch, vmem_mb=None):
    n_tok, d = x2d.shape
    n_c = wg.shape[1] // fc
    tok = pl.BlockSpec((tm, d), lambda i, c: (i, 0))
    vec = pl.BlockSpec((1, d), lambda i, c: (0, 0))
    h_dtype = BF16 if prec == "bf16" else F32
    w_mode = dict(pipeline_mode=pl.Buffered(1)) if n_c == 1 else {}
    return pl.pallas_call(
        functools.partial(_ffn_kernel, prec=prec, n_c=n_c),
        grid=(n_tok // tm, n_c),
        in_specs=[tok, vec,
                  _mod_spec(per_token, tm, d, sc_col, tiles_per_batch),
                  _mod_spec(per_token, tm, d, sh_col, tiles_per_batch),
                  pl.BlockSpec((d, fc), lambda i, c: (0, c), **w_mode),
                  pl.BlockSpec((d, fc), lambda i, c: (0, c), **w_mode),
                  pl.BlockSpec((fc, d), lambda i, c: (c, 0), **w_mode),
                  _mod_spec(per_token, tm, d, gt_col, tiles_per_batch), vec],
        out_specs=tok,
        out_shape=jax.ShapeDtypeStruct((n_tok, d), F32),
        scratch_shapes=[pltpu.VMEM((tm, d), h_dtype), pltpu.VMEM((tm, d), F32)],
        compiler_params=_params(("parallel", "arbitrary"), vmem_mb),
    )(x2d, g1, mod, mod, wg, wu, wd, mod, g2)


def _pack_router_weight(wr_ref, wp_scr, n_experts):
    w1, w2, w3 = _split(wr_ref[...], 3)
    fwd = lambda x, k: jnp.concatenate([x[:, LANES - k * n_experts:], x[:, :LANES - k * n_experts]], axis=1)
    packed = w1.astype(F32) + (fwd(w2.astype(F32), 1) + fwd(w3.astype(F32), 2))
    wp_scr[...] = packed.astype(wp_scr.dtype)


def _route(h, w, n_experts):
    h1, h2, h3 = _split(h, 3)
    p1, p2, p3 = _dot(h1, w), _dot(h2, w), _dot(h3, w)
    back = lambda x, k: jnp.concatenate([x[:, k * n_experts:], x[:, :k * n_experts]], axis=1)
    small = (back(p1, 2) + p3) + back(p2, 1)
    logits = p1 + ((back(p1, 1) + p2) + small)
    lane = lax.broadcasted_iota(jnp.int32, logits.shape, 1).astype(F32)
    neg = -jnp.inf
    l1 = jnp.where(lane < n_experts, logits, neg)
    m1 = jnp.max(l1, axis=-1, keepdims=True)
    i1 = jnp.min(jnp.where(l1 == m1, lane, float(LANES)), axis=-1, keepdims=True)
    l2 = jnp.where(lane == i1, neg, l1)
    m2 = jnp.max(l2, axis=-1, keepdims=True)
    i2 = jnp.min(jnp.where(l2 == m2, lane, float(LANES)), axis=-1, keepdims=True)
    e2 = jnp.exp(m2 - m1)
    den = 1.0 + e2
    return jnp.where(lane == i1, 1.0 / den, 0.0) + jnp.where(lane == i2, e2 / den, 0.0)


def _router_kernel(x_ref, g_ref, sc_ref, sh_ref, wr_ref, comb_ref, hb_ref, wp_scr, *, n_experts):
    @pl.when(pl.program_id(0) == 0)
    def _():
        _pack_router_weight(wr_ref, wp_scr, n_experts)

    h = _normmod(x_ref[...], g_ref[...], _mod_row(sc_ref), _mod_row(sh_ref))
    hb_ref[...] = h.astype(hb_ref.dtype)
    comb_ref[...] = _route(h, wp_scr[...], n_experts)


def _router(x2d, g, mod, sc_col, sh_col, wr_pad, *, tm, n_experts, per_token, tiles_per_batch):
    n_tok, d = x2d.shape
    return pl.pallas_call(
        functools.partial(_router_kernel, n_experts=n_experts),
        grid=(n_tok // tm,),
        in_specs=[pl.BlockSpec((tm, d), lambda i: (i, 0)),
                  pl.BlockSpec((1, d), lambda i: (0, 0)),
                  _mod_spec(per_token, tm, d, sc_col, tiles_per_batch),
                  _mod_spec(per_token, tm, d, sh_col, tiles_per_batch),
                  pl.BlockSpec((d, LANES), lambda i: (0, 0))],
        out_specs=[pl.BlockSpec((tm, LANES), lambda i: (i, 0)),
                   pl.BlockSpec((tm, d), lambda i: (i, 0))],
        out_shape=[jax.ShapeDtypeStruct((n_tok, LANES), F32),
                   jax.ShapeDtypeStruct((n_tok, d), BF16)],
        scratch_shapes=[pltpu.VMEM((d, LANES), BF16)],
        compiler_params=_params(("arbitrary",)),
    )(x2d, g, mod, mod, wr_pad)


MOE_TB = 256
MOE_SEG = 8
MOE_DTYPE = F32
MOE_TM = 512


def _moe_block_rows(n_experts):
    return -(-(2 * MOE_TB + n_experts * (MOE_SEG - 1)) // LANES) * LANES


def _routing_slots(comb, dest):
    lane = lax.broadcasted_iota(jnp.int32, comb.shape, 1).astype(F32)
    sel = comb > 0.0
    i_lo = jnp.min(jnp.where(sel, lane, float(LANES)), axis=-1, keepdims=True)
    i_hi = jnp.max(jnp.where(sel, lane, -1.0), axis=-1, keepdims=True)
    pick = lambda i, v: jnp.sum(jnp.where(lane == i, v, 0.0), axis=-1, keepdims=True)
    has_lo = i_lo < float(LANES)
    has_hi = i_hi > i_lo
    d_lo = jnp.where(has_lo, pick(i_lo, dest), -1.0)
    d_hi = jnp.where(has_hi, pick(i_hi, dest), -1.0)
    w_lo = pick(i_lo, comb)
    w_hi = jnp.where(has_hi, pick(i_hi, comb), 0.0)
    out = jnp.where(lane == 0.0, d_lo, 0.0)
    for k, v in ((1.0, d_hi), (2.0, w_lo), (3.0, w_hi)):
        out = jnp.where(lane == k, v, out)
    return out


def _one_hot_rows(row, n_rows):
    tb = row.shape[0]
    lane = lax.broadcasted_iota(jnp.int32, (1, LANES), 1).astype(F32)
    row = jnp.broadcast_to(row, (tb, LANES))
    blocks = [jnp.where(row == lane + float(cb * LANES), 1.0, 0.0) for cb in range(n_rows // LANES)]
    return jnp.concatenate(blocks, axis=1).astype(BF16)


_SEG_BITS = (256, 128, 64, 32, 16, 8)


def _segment_copies(make_copy, blk, pos_ref, off_ref, cnt_ref, n_experts, start):
    for e in range(n_experts):
        seg = blk * n_experts + e
        in_blk, in_sorted, n = off_ref[seg], pos_ref[seg], cnt_ref[seg]
        for size in _SEG_BITS:
            has = (n & size) != 0

            @pl.when(has)
            def _(in_blk=in_blk, in_sorted=in_sorted, size=size):
                cp = make_copy(pl.multiple_of(in_blk, MOE_SEG), pl.multiple_of(in_sorted, MOE_SEG), size)
                cp.start() if start else cp.wait()

            step = jnp.where(has, size, 0)
            in_blk, in_sorted = in_blk + step, in_sorted + step


def _moe_sort_kernel(pos_ref, off_ref, cnt_ref, gpos_ref, glen_ref, hbp_ref, combp_ref, hbs_ref,
                     combs_ref, dest_ref, xs_hbm, buf, zbuf, sems, *, n_rows, n_experts, n_blk, n_blk_p):
    b = pl.program_id(0)
    slot = b % 2

    def writes(blk, slot, start):
        def make_copy(block_row, sorted_row, size):
            return pltpu.make_async_copy(buf.at[slot, pl.ds(block_row, size)],
                                         xs_hbm.at[pl.ds(sorted_row, size)], sems.at[slot])
        _segment_copies(make_copy, blk, pos_ref, off_ref, cnt_ref, n_experts, start)

    def zero_gaps(start):
        for e in range(n_experts):
            row, n = gpos_ref[e], glen_ref[e]
            for size in _SEG_BITS:
                has = (n & size) != 0

                @pl.when(has)
                def _(row=row, size=size):
                    cp = pltpu.make_async_copy(
                        zbuf.at[pl.ds(0, size)],
                        xs_hbm.at[pl.ds(pl.multiple_of(row, MOE_SEG), size)], sems.at[2])
                    cp.start() if start else cp.wait()

                row = row + jnp.where(has, size, 0)
        tail_row, tail_n, big = gpos_ref[n_experts], glen_ref[n_experts], _SEG_BITS[0]

        def piece(i, carry):
            cp = pltpu.make_async_copy(
                zbuf.at[pl.ds(0, big)],
                xs_hbm.at[pl.ds(pl.multiple_of(tail_row + i * big, MOE_SEG), big)], sems.at[2])
            cp.start() if start else cp.wait()
            return carry

        lax.fori_loop(0, tail_n // big, piece, 0)

    @pl.when(b == 0)
    def _():
        zbuf[...] = jnp.zeros_like(zbuf)
        zero_gaps(True)

    @pl.when(b >= 2)
    def _():
        writes(b - 2, slot, False)

    in_prompt = b < n_blk_p
    comb = jnp.where(in_prompt, combp_ref[...], combs_ref[...])
    hb = jnp.where(in_prompt, hbp_ref[...], hbs_ref[...])
    tb = comb.shape[0]
    sel = jnp.where(comb > 0.0, 1.0, 0.0)
    r = lax.broadcasted_iota(jnp.int32, (tb, tb), 0)
    c = lax.broadcasted_iota(jnp.int32, (tb, tb), 1)
    rank = _dot(jnp.where(c < r, 1.0, 0.0).astype(BF16), sel.astype(BF16))
    cnt = jnp.sum(sel, axis=0, keepdims=True)
    cnt = jnp.floor((cnt + (MOE_SEG - 1)) / MOE_SEG) * MOE_SEG
    ru = lax.broadcasted_iota(jnp.int32, (LANES, LANES), 0)
    cu = lax.broadcasted_iota(jnp.int32, (LANES, LANES), 1)
    off = _dot(jnp.broadcast_to(cnt, (8, LANES)).astype(BF16),
               jnp.where(ru < cu, 1.0, 0.0).astype(BF16))[0:1]
    slots = _routing_slots(comb, off + rank)
    dest_ref[...] = slots
    pt = _one_hot_rows(slots[:, 0:1], n_rows) + _one_hot_rows(slots[:, 1:2], n_rows)
    buf[slot] = _dot(pt, hb, ((0,), (0,))).astype(buf.dtype)
    writes(b, slot, True)

    @pl.when(b == n_blk - 1)
    def _():
        writes(b, slot, False)
        if n_blk > 1:
            writes(b - 1, 1 - slot, False)
        zero_gaps(False)


def _moe_sort(seg, gap_pos, gap_len, hb_p, comb_p, hb_s, comb_s, n_sorted_rows, *, n_experts):
    n_p, d = hb_p.shape
    n_blk_p = n_p // MOE_TB
    n_blk = n_blk_p + 1
    n_rows = _moe_block_rows(n_experts)
    last_p = n_blk_p - 1
    return pl.pallas_call(
        functools.partial(_moe_sort_kernel, n_rows=n_rows, n_experts=n_experts, n_blk=n_blk,
                          n_blk_p=n_blk_p),
        grid_spec=pltpu.PrefetchScalarGridSpec(
            num_scalar_prefetch=5, grid=(n_blk,),
            in_specs=[pl.BlockSpec((MOE_TB, d), lambda b, *_: (jnp.minimum(b, last_p), 0)),
                      pl.BlockSpec((MOE_TB, LANES), lambda b, *_: (jnp.minimum(b, last_p), 0)),
                      pl.BlockSpec((MOE_TB, d), lambda b, *_: (0, 0)),
                      pl.BlockSpec((MOE_TB, LANES), lambda b, *_: (0, 0))],
            out_specs=[pl.BlockSpec((MOE_TB, LANES), lambda b, *_: (b, 0)),
                       pl.BlockSpec(memory_space=pl.ANY)],
            scratch_shapes=[pltpu.VMEM((2, n_rows, d), MOE_DTYPE),
                            pltpu.VMEM((_SEG_BITS[0], d), MOE_DTYPE),
                            pltpu.SemaphoreType.DMA((3,))]),
        out_shape=[jax.ShapeDtypeStruct((n_blk * MOE_TB, LANES), F32),
                   jax.ShapeDtypeStruct((n_sorted_rows, d), MOE_DTYPE)],
        compiler_params=_params(("arbitrary",)),
    )(*seg, gap_pos, gap_len, hb_p, comb_p, hb_s, comb_s)


def _experts_kernel(te_ref, nt_ref, x_ref, wg_ref, wu_ref, wd_ref, o_ref, acc_scr, *, n_c):
    del te_ref
    k = pl.program_id(0)
    c = pl.program_id(1)
    active = k < nt_ref[0]

    @pl.when(active)
    def _():
        h = x_ref[...].astype(BF16)
        act = _silu(_dot(h, wg_ref[0])) * _dot(h, wu_ref[0])
        y = _dot(act.astype(BF16), wd_ref[0])

        @pl.when(c == 0)
        def _():
            acc_scr[...] = y

        @pl.when(c > 0)
        def _():
            acc_scr[...] += y

        @pl.when(c == n_c - 1)
        def _():
            o_ref[...] = acc_scr[...].astype(o_ref.dtype)

    @pl.when(jnp.logical_not(active) & (c == n_c - 1))
    def _():
        o_ref[...] = jnp.zeros_like(o_ref)


def _experts(tile_expert, n_tiles, xs, wg, wu, wd, *, fc):
    n_rows, d = xs.shape
    n_c = wg.shape[2] // fc

    def row_map(k, c, te, nt):
        return (jnp.minimum(k, nt[0] - 1), 0)

    def w_map(k, c, te, nt):
        return (te[k], 0, jnp.where(k < nt[0], c, n_c - 1))

    def wd_map(k, c, te, nt):
        return (te[k], jnp.where(k < nt[0], c, n_c - 1), 0)

    return pl.pallas_call(
        functools.partial(_experts_kernel, n_c=n_c),
        grid_spec=pltpu.PrefetchScalarGridSpec(
            num_scalar_prefetch=2, grid=(n_rows // MOE_TM, n_c),
            in_specs=[pl.BlockSpec((MOE_TM, d), row_map),
                      pl.BlockSpec((1, d, fc), w_map), pl.BlockSpec((1, d, fc), w_map),
                      pl.BlockSpec((1, fc, d), wd_map)],
            out_specs=pl.BlockSpec((MOE_TM, d), lambda k, c, te, nt: (k, 0)),
            scratch_shapes=[pltpu.VMEM((MOE_TM, d), F32)]),
        out_shape=jax.ShapeDtypeStruct((n_rows, d), MOE_DTYPE),
        compiler_params=_params(("arbitrary", "arbitrary"), VMEM_LIMIT_MB),
    )(tile_expert, n_tiles, xs, wg, wu, wd)


def _moe_combine_kernel(pos_ref, off_ref, cnt_ref, ys_hbm, dest_ref, x_ref, gt_ref, g_ref,
                        o_ref, buf, sems, *, n_experts, n_blk, blk0):
    b = pl.program_id(0)
    slot = b % 2

    def reads(blk, slot, start):
        def make_copy(block_row, sorted_row, size):
            return pltpu.make_async_copy(ys_hbm.at[pl.ds(sorted_row, size)],
                                         buf.at[slot, pl.ds(block_row, size)], sems.at[slot])
        _segment_copies(make_copy, blk0 + blk, pos_ref, off_ref, cnt_ref, n_experts, start)

    @pl.when(b == 0)
    def _():
        buf[...] = jnp.zeros_like(buf)
        reads(b, slot, True)

    if n_blk > 1:
        @pl.when(b + 1 < n_blk)
        def _():
            reads(b + 1, 1 - slot, True)

    reads(b, slot, False)
    yb = buf[slot].astype(BF16)
    slots = dest_ref[...]
    n_rows = yb.shape[0]
    y = (slots[:, 2:3] * _dot(_one_hot_rows(slots[:, 0:1], n_rows), yb)
         + slots[:, 3:4] * _dot(_one_hot_rows(slots[:, 1:2], n_rows), yb))
    o_ref[...] = x_ref[...] + _mod_row(gt_ref) * (_rms(y) * g_ref[...])


def _moe_combine(sorted_pos, block_off, seg_cnt, ys, dest, x2d, mod, gt_col, g, *, blk0,
                 n_experts, per_token, tiles_per_batch):
    n_tok, d = x2d.shape
    n_blk = n_tok // MOE_TB
    n_rows = _moe_block_rows(n_experts)
    return pl.pallas_call(
        functools.partial(_moe_combine_kernel, n_experts=n_experts, n_blk=n_blk, blk0=blk0),
        grid_spec=pltpu.PrefetchScalarGridSpec(
            num_scalar_prefetch=3, grid=(n_blk,),
            in_specs=[pl.BlockSpec(memory_space=pl.ANY),
                      pl.BlockSpec((MOE_TB, LANES), lambda b, *_: (blk0 + b, 0)),
                      pl.BlockSpec((MOE_TB, d), lambda b, *_: (b, 0)),
                      _mod_spec(per_token, MOE_TB, d, gt_col, tiles_per_batch),
                      pl.BlockSpec((1, d), lambda b, *_: (0, 0))],
            out_specs=pl.BlockSpec((MOE_TB, d), lambda b, *_: (b, 0)),
            scratch_shapes=[pltpu.VMEM((2, n_rows, d), MOE_DTYPE), pltpu.SemaphoreType.DMA((2,))]),
        out_shape=jax.ShapeDtypeStruct((n_tok, d), F32),
        compiler_params=_params(("arbitrary",)),
    )(sorted_pos, block_off, seg_cnt, ys, dest, x2d, mod, g)


def _moe_schedule(combs, n_experts, n_tiles_max):
    def block_counts(comb):
        n_blk = comb.shape[0] // MOE_TB
        sel = (comb[:, :n_experts] > 0.0).astype(jnp.int32)
        return jnp.sum(sel.reshape(n_blk, MOE_TB, n_experts), axis=1)

    cnt = jnp.concatenate([block_counts(c) for c in combs], axis=0)
    cnt = -(-cnt // MOE_SEG) * MOE_SEG
    off = jnp.cumsum(cnt, axis=1) - cnt
    tot = jnp.sum(cnt, axis=0)
    grp = -(-tot // MOE_TM) * MOE_TM
    g_end = jnp.cumsum(grp)
    sorted_pos = (g_end - grp)[None, :] + (jnp.cumsum(cnt, axis=0) - cnt)
    n_tiles = g_end[-1] // MOE_TM
    first_row = jnp.arange(n_tiles_max, dtype=jnp.int32) * MOE_TM
    tile_expert = jnp.sum(first_row[:, None] >= g_end[None, :], axis=1)
    last = jnp.sum((n_tiles - 1) * MOE_TM >= g_end)
    tile_expert = jnp.minimum(tile_expert, last).astype(jnp.int32)
    flat = lambda a: a.reshape(-1).astype(jnp.int32)
    seg = (flat(sorted_pos), flat(off), flat(cnt))
    gap_pos = flat(jnp.concatenate([g_end - grp + tot, g_end[-1:]]))
    gap_len = flat(jnp.concatenate([grp - tot, n_tiles_max * MOE_TM - g_end[-1:]]))
    return seg, gap_pos, gap_len, tile_expert, n_tiles.reshape(1).astype(jnp.int32)


def _gla_conv_kernel(p_ref, wgu_ref, bgu_ref, gh_ref, wc_ref, mixed_ref, st_ref, cst_ref,
                     s_scr, uext_scr, *, tt, n_t, heads, dk, dv, cw):
    ti = pl.program_id(1)
    hk = heads * dk
    hv = heads * dv
    o_q, o_k, o_v, o_og = 0, hk, 2 * hk, 2 * hk + hv
    o_ch = o_og + hv
    o_cb, o_cc, o_lr = o_ch + cw, o_ch + 2 * cw, o_ch + 3 * cw
    L = GLA_CHUNK

    @pl.when(ti == 0)
    def _():
        s_scr[...] = jnp.zeros_like(s_scr)
        uext_scr[0:8, :] = jnp.zeros((8, cw), F32)

    logit = _dot(p_ref[:, o_lr:o_lr + LANES].astype(BF16), wgu_ref[...]) + bgu_ref[...]
    logg = _log_sigmoid(logit) / GATE_NORMALIZER

    r = lax.broadcasted_iota(jnp.int32, (tt, tt), 0)
    c = lax.broadcasted_iota(jnp.int32, (tt, tt), 1)
    tri = jnp.where((r // L == c // L) & (c <= r), 1.0, 0.0).astype(BF16)
    g_hi = logg.astype(BF16)
    rem = logg - g_hi.astype(F32)
    g_mid = rem.astype(BF16)
    g_lo = (rem - g_mid.astype(F32)).astype(BF16)
    bc = _dot(tri, g_hi) + (_dot(tri, g_mid) + _dot(tri, g_lo))

    rl = lax.broadcasted_iota(jnp.int32, (L, L), 0)
    cl = lax.broadcasted_iota(jnp.int32, (L, L), 1)
    tril = cl <= rl
    lane = lax.broadcasted_iota(jnp.int32, (L, LANES), 1)
    heads_per_blk = LANES // dk
    nt_dims = ((1,), (1,))

    state = [s_scr[h] for h in range(heads)]
    for ck in range(tt // L):
        rows = slice(L * ck, L * ck + L)
        b = bc[rows]
        bl = b[L - 1:L]
        q_in = (p_ref[rows, o_q:o_q + hk] * (dk ** -0.5)) * jnp.exp(b)
        k = p_ref[rows, o_k:o_k + hk]
        k_out = (k * jnp.exp(-b)).astype(BF16)
        k_dec = (k * jnp.exp(bl - b)).astype(BF16)
        dec = jnp.exp(bl)
        q_in = q_in.astype(BF16)
        for h in range(heads):
            blk = slice(LANES * (h // heads_per_blk), LANES * (h // heads_per_blk) + LANES)
            in_head = (lane // dk) == (h % heads_per_blk)
            qm = jnp.where(in_head, q_in[:, blk], jnp.zeros_like(q_in[:, blk]))
            att = _dot(qm, k_out[:, blk], nt_dims)
            att = jnp.where(tril, att, 0.0).astype(BF16)
            vh = p_ref[rows, o_v + dv * h:o_v + dv * h + dv].astype(BF16)
            o = _dot(att, vh) + _dot(qm, state[h].astype(BF16), nt_dims)
            ds_t = _dot(vh, k_dec[:, blk], ((0,), (0,)))
            state[h] = state[h] * dec[:, blk] + ds_t
            og = p_ref[rows, o_og + dv * h:o_og + dv * h + dv]
            res = (_rms(o) * gh_ref[...]) * _silu(og)
            mixed_ref[rows, dv * h:dv * h + dv] = res.astype(mixed_ref.dtype)
    for h in range(heads):
        s_scr[h] = state[h]

    u = p_ref[:, o_cc:o_cc + cw] * p_ref[:, o_ch:o_ch + cw]
    uext_scr[8:8 + tt, :] = u
    y = (wc_ref[0:1, :] * uext_scr[6:6 + tt, :] + wc_ref[1:2, :] * uext_scr[7:7 + tt, :]
         + wc_ref[2:3, :] * u)
    mixed_ref[:, hv:hv + cw] = (p_ref[:, o_cb:o_cb + cw] * y).astype(mixed_ref.dtype)
    tail = uext_scr[tt + 6:tt + 8, :]
    uext_scr[6:8, :] = tail

    @pl.when(ti == n_t - 1)
    def _():
        st_ref[0] = s_scr[...]
        cst_ref[0] = tail


def _gla_conv_prompt(proj, wgu_pad, bgu, g_head, w_conv, *, bsz, t, tt, heads, dk, dv, cw):
    n_t = t // tt
    n_in = proj.shape[1]
    width = heads * dv + cw
    return pl.pallas_call(
        functools.partial(_gla_conv_kernel, tt=tt, n_t=n_t, heads=heads, dk=dk, dv=dv, cw=cw),
        grid=(bsz, n_t),
        in_specs=[pl.BlockSpec((tt, n_in), lambda b, i: (b * n_t + i, 0)),
                  pl.BlockSpec(wgu_pad.shape, lambda b, i: (0, 0)),
                  pl.BlockSpec(bgu.shape, lambda b, i: (0, 0)),
                  pl.BlockSpec(g_head.shape, lambda b, i: (0, 0)),
                  pl.BlockSpec(w_conv.shape, lambda b, i: (0, 0))],
        out_specs=[pl.BlockSpec((tt, width), lambda b, i: (b * n_t + i, 0)),
                   pl.BlockSpec((1, heads, dv, LANES), lambda b, i: (b, 0, 0, 0)),
                   pl.BlockSpec((1, 2, cw), lambda b, i: (b, 0, 0))],
        out_shape=[jax.ShapeDtypeStruct((bsz * t, width), BF16),
                   jax.ShapeDtypeStruct((bsz, heads, dv, LANES), F32),
                   jax.ShapeDtypeStruct((bsz, 2, cw), F32)],
        scratch_shapes=[pltpu.VMEM((heads, dv, LANES), F32), pltpu.VMEM((tt + 8, cw), F32)],
        compiler_params=_params(("parallel", "arbitrary"), VMEM_LIMIT_MB),
    )(proj, wgu_pad, bgu, g_head, w_conv)


def _gla_conv_step_kernel(q_ref, k_ref, v_ref, og_ref, ch_ref, cb_ref, cc_ref, lr_ref,
                          wgu_ref, bgu_ref, gh_ref, wc_ref, s_ref, cbuf_ref,
                          o_ref, y_ref, sn_ref, cn_ref, a_scr, *, dk):
    logit = _mm(wgu_ref[...], lr_ref[...], "f32") + bgu_ref[...]
    a_scr[...] = jnp.exp(_log_sigmoid(logit) / GATE_NORMALIZER)
    v_t = v_ref[...]

    def body(d, acc):
        a = a_scr[pl.ds(d, 1), :]
        kd = k_ref[pl.ds(d, 1), :]
        qd = q_ref[pl.ds(d, 1), :]
        s_new = a * s_ref[0, d] + kd * v_t
        sn_ref[0, d] = s_new
        return acc + (qd * (dk ** -0.5)) * s_new

    o = lax.fori_loop(0, dk, body, jnp.zeros(v_t.shape, F32))
    o = o * lax.rsqrt(jnp.mean(o * o, axis=0, keepdims=True) + EPS) * gh_ref[...]
    o_ref[...] = o * _silu(og_ref[...])
    u = cc_ref[...] * ch_ref[...]
    y = wc_ref[0] * cbuf_ref[0] + wc_ref[1] * cbuf_ref[1] + wc_ref[2] * u
    y_ref[...] = cb_ref[...] * y
    cn_ref[0] = cbuf_ref[1]
    cn_ref[1] = u


def _gla_conv_step(proj_t, wgu_t_pad, bgu_col, gh_col, wc_col, s_t, cbuf_t, *, heads, dk, dv, cw):
    bsz = proj_t.shape[1]
    hk, hv = heads * dk, heads * dv
    cs = cw // heads
    assert dv == LANES and cs == LANES and dk * 2 == LANES
    o_k, o_v, o_og = hk, 2 * hk, 2 * hk + hv
    o_ch = o_og + hv
    o_cb, o_cc, o_lr = o_ch + cw, o_ch + 2 * cw, o_ch + 3 * cw
    row = lambda off, size: (lambda h: (off // size + h, 0))
    blk = lambda size, off: pl.BlockSpec((size, bsz), row(off, size))
    return pl.pallas_call(
        functools.partial(_gla_conv_step_kernel, dk=dk),
        grid=(heads,),
        in_specs=[blk(dk, 0), blk(dk, o_k), blk(dv, o_v), blk(dv, o_og),
                  blk(cs, o_ch), blk(cs, o_cb), blk(cs, o_cc),
                  pl.BlockSpec((LANES, bsz), lambda h: (o_lr // LANES, 0)),
                  pl.BlockSpec((dk, LANES), lambda h: (h, 0)),
                  pl.BlockSpec((dk, 1), lambda h: (h, 0)),
                  pl.BlockSpec((dv, 1), lambda h: (0, 0)),
                  pl.BlockSpec((3, cs, 1), lambda h: (0, h, 0)),
                  pl.BlockSpec((1, dk, dv, bsz), lambda h: (h, 0, 0, 0)),
                  pl.BlockSpec((2, cs, bsz), lambda h: (0, h, 0))],
        out_specs=[pl.BlockSpec((dv, bsz), lambda h: (h, 0)),
                   pl.BlockSpec((cs, bsz), lambda h: (h, 0)),
                   pl.BlockSpec((1, dk, dv, bsz), lambda h: (h, 0, 0, 0)),
                   pl.BlockSpec((2, cs, bsz), lambda h: (0, h, 0))],
        out_shape=[jax.ShapeDtypeStruct((hv, bsz), F32),
                   jax.ShapeDtypeStruct((cw, bsz), F32),
                   jax.ShapeDtypeStruct(s_t.shape, F32),
                   jax.ShapeDtypeStruct(cbuf_t.shape, F32)],
        scratch_shapes=[pltpu.VMEM((dk, bsz), F32)],
        compiler_params=_params(("parallel",)),
    )(proj_t, proj_t, proj_t, proj_t, proj_t, proj_t, proj_t, proj_t,
      wgu_t_pad, bgu_col, gh_col, wc_col, s_t, cbuf_t)


def _s5_scan_kernel(*refs, tt, n_t, nb, half, n_cast):
    u_ref, bb_ref, cm_ref, abr_ref, abi_ref, d_ref = refs[:6]
    y_ref, sre_ref, sim_ref = refs[6 + n_cast:9 + n_cast]
    scr, tb_scr, xr_scr, xi_scr = refs[9 + 2 * n_cast:]
    _cast_passengers(refs[6:6 + n_cast], refs[9 + n_cast:9 + 2 * n_cast])
    ti = pl.program_id(1)
    n_blk = half // LANES

    @pl.when(ti == 0)
    def _():
        xr_scr[...] = jnp.zeros_like(xr_scr)
        xi_scr[...] = jnp.zeros_like(xi_scr)

    for b in range(nb):
        tb_scr[pl.ds(b, tt, stride=nb), :] = u_ref[b].astype(F32)
    for k in range(nb):
        rows = slice(k * tt, (k + 1) * tt)
        bu = _dot(tb_scr[rows, :].astype(BF16), bb_ref[0])
        for cb in range(2 * n_blk):
            scr[cb, rows, :] = bu[:, cb * LANES:(cb + 1) * LANES]
    a_re = [jnp.broadcast_to(abr_ref[0][:, cb * LANES:(cb + 1) * LANES], (nb, LANES))
            for cb in range(n_blk)]
    a_im = [jnp.broadcast_to(abi_ref[0][:, cb * LANES:(cb + 1) * LANES], (nb, LANES))
            for cb in range(n_blk)]

    def body(t, carry):
        rows = pl.ds(pl.multiple_of(t * nb, nb), nb)
        out = []
        for cb in range(n_blk):
            xr, xi = carry[2 * cb], carry[2 * cb + 1]
            nr = (a_re[cb] * xr - a_im[cb] * xi) + scr[cb, rows, :]
            ni = (a_re[cb] * xi + a_im[cb] * xr) + scr[n_blk + cb, rows, :]
            scr[cb, rows, :] = nr
            scr[n_blk + cb, rows, :] = ni
            out += [nr, ni]
        return tuple(out)

    init = []
    for cb in range(n_blk):
        init += [xr_scr[:, cb * LANES:(cb + 1) * LANES], xi_scr[:, cb * LANES:(cb + 1) * LANES]]
    fin = lax.fori_loop(0, tt, body, tuple(init), unroll=4)
    xr = jnp.concatenate([fin[2 * cb] for cb in range(n_blk)], axis=1)
    xi = jnp.concatenate([fin[2 * cb + 1] for cb in range(n_blk)], axis=1)
    xr_scr[...] = xr
    xi_scr[...] = xi
    for k in range(nb):
        rows = slice(k * tt, (k + 1) * tt)
        xs = jnp.concatenate([scr[cb, rows, :] for cb in range(2 * n_blk)], axis=1)
        y = _dot(xs.astype(BF16), cm_ref[0]) + d_ref[0] * tb_scr[rows, :]
        tb_scr[rows, :] = _gelu_tanh(y)
    for b in range(nb):
        y_ref[b] = tb_scr[pl.ds(b, tt, stride=nb), :].astype(y_ref.dtype)

    @pl.when(ti == n_t - 1)
    def _():
        sre_ref[...] = xr
        sim_ref[...] = xi


def _s5_prompt(u3d, bb_blk, c_blk, ab_re, ab_im, d_blk, cast_ws, *, tt):
    nb, t, d = u3d.shape
    n_j = d // LANES
    n_t = t // tt
    half = bb_blk.shape[2] // 2
    w_specs = _passenger_specs(cast_ws, n_j * n_t, lambda j, i: j * n_t + i)
    return pl.pallas_call(
        functools.partial(_s5_scan_kernel, tt=tt, n_t=n_t, nb=nb, half=half, n_cast=len(cast_ws)),
        grid=(n_j, n_t),
        in_specs=[pl.BlockSpec((nb, tt, LANES), lambda j, i: (0, i, j)),
                  pl.BlockSpec((1, LANES, 2 * half), lambda j, i: (j, 0, 0)),
                  pl.BlockSpec((1, 2 * half, LANES), lambda j, i: (j, 0, 0)),
                  pl.BlockSpec((1, 1, half), lambda j, i: (j, 0, 0)),
                  pl.BlockSpec((1, 1, half), lambda j, i: (j, 0, 0)),
                  pl.BlockSpec((1, 1, LANES), lambda j, i: (j, 0, 0))] + w_specs,
        out_specs=[pl.BlockSpec((nb, tt, LANES), lambda j, i: (0, i, j)),
                   pl.BlockSpec((nb, half), lambda j, i: (0, j)),
                   pl.BlockSpec((nb, half), lambda j, i: (0, j))] + w_specs,
        out_shape=[jax.ShapeDtypeStruct((nb, t, d), BF16),
                   jax.ShapeDtypeStruct((nb, n_j * half), F32),
                   jax.ShapeDtypeStruct((nb, n_j * half), F32)]
        + [jax.ShapeDtypeStruct(w.shape, BF16) for w in cast_ws],
        scratch_shapes=[pltpu.VMEM((2 * half // LANES, nb * tt, LANES), F32),
                        pltpu.VMEM((nb * tt, LANES), F32),
                        pltpu.VMEM((nb, half), F32), pltpu.VMEM((nb, half), F32)],
        compiler_params=_params(("arbitrary", "arbitrary"), VMEM_LIMIT_MB),
    )(u3d, bb_blk, c_blk, ab_re, ab_im, d_blk, *cast_ws)


def _s5_step_kernel(u_ref, bb_ref, cm_ref, abr_ref, abi_ref, d_ref, sre_ref, sim_ref,
                    y_ref, nre_ref, nim_ref, *, half):
    u = u_ref[...]
    bu = _mm(u, bb_ref[0], "f32")
    ar, ai = abr_ref[0], abi_ref[0]
    sr, si = sre_ref[...], sim_ref[...]
    nr = (ar * sr - ai * si) + bu[:, 0:half]
    ni = (ar * si + ai * sr) + bu[:, half:2 * half]
    nre_ref[...] = nr
    nim_ref[...] = ni
    y = _mm(jnp.concatenate([nr, ni], axis=1), cm_ref[0], "f32") + d_ref[0] * u
    y_ref[...] = _gelu_tanh(y)


def _s5_step(u2d, bb_blk, c_blk, ab_re, ab_im, d_blk, s_re, s_im):
    bsz, d = u2d.shape
    n_j = d // LANES
    half = bb_blk.shape[2] // 2
    st = pl.BlockSpec((bsz, half), lambda j: (0, j))
    return pl.pallas_call(
        functools.partial(_s5_step_kernel, half=half),
        grid=(n_j,),
        in_specs=[pl.BlockSpec((bsz, LANES), lambda j: (0, j)),
                  pl.BlockSpec((1, LANES, 2 * half), lambda j: (j, 0, 0)),
                  pl.BlockSpec((1, 2 * half, LANES), lambda j: (j, 0, 0)),
                  pl.BlockSpec((1, 1, half), lambda j: (j, 0, 0)),
                  pl.BlockSpec((1, 1, half), lambda j: (j, 0, 0)),
                  pl.BlockSpec((1, 1, LANES), lambda j: (j, 0, 0)),
                  st, st],
        out_specs=[pl.BlockSpec((bsz, LANES), lambda j: (0, j)), st, st],
        out_shape=[jax.ShapeDtypeStruct((bsz, d), F32),
                   jax.ShapeDtypeStruct(s_re.shape, F32),
                   jax.ShapeDtypeStruct(s_im.shape, F32)],
        compiler_params=_params(("parallel",)),
    )(u2d, bb_blk, c_blk, ab_re, ab_im, d_blk, s_re, s_im)


def _s5_zoh(a_re, a_im, log_dt, b_re, b_im):
    dt = jnp.exp(log_dt)[:, None]
    mag = jnp.exp(dt * a_re)
    ab_re, ab_im = mag * jnp.cos(dt * a_im), mag * jnp.sin(dt * a_im)
    den = a_re * a_re + a_im * a_im
    nr, ni = ab_re - 1.0, ab_im
    f_re = (nr * a_re + ni * a_im) / den
    f_im = (ni * a_re - nr * a_im) / den
    bb_re = f_re[..., None] * b_re - f_im[..., None] * b_im
    bb_im = f_re[..., None] * b_im + f_im[..., None] * b_re
    return ab_re, ab_im, bb_re, bb_im


def _s5_step_blocks(ab_re, ab_im, bb_re, bb_im, c_re, c_im, d_skip):
    g, p = ab_re.shape
    cg = S5_CHUNK_GROUPS
    n_j = g // cg
    eye = jnp.eye(cg, dtype=F32)

    def in_blk(m):
        m = m.reshape(n_j, cg, p, S5_GROUP)
        return jnp.einsum("jgpi,gh->jgihp", m, eye).reshape(n_j, cg * S5_GROUP, cg * p)

    def out_blk(m):
        m = m.reshape(n_j, cg, S5_GROUP, p)
        return jnp.einsum("jgip,gh->jgphi", m, eye).reshape(n_j, cg * p, cg * S5_GROUP)

    bb_blk = jnp.concatenate([in_blk(bb_re), in_blk(bb_im)], axis=2)
    c_blk = jnp.concatenate([out_blk(c_re), out_blk(-c_im)], axis=1)
    return (bb_blk, c_blk, ab_re.reshape(n_j, 1, cg * p), ab_im.reshape(n_j, 1, cg * p),
            d_skip.reshape(n_j, 1, cg * S5_GROUP))


def kernel(x_prompt, x_sample, c_prompt, c_sample, state_gla, state_conv, state_s5_re, state_s5_im, w_ada, b_ada, g_norm, w_in0, w_gate_up, b_gate_up, g_head_norm, w_conv, w_out0, w_ffn_gate, w_ffn_up, w_ffn_down, w_in1, s5_a_re, s5_a_im, s5_log_dt, s5_b_re, s5_b_im, s5_c_re, s5_c_im, s5_d, w_glu, w_router, w_exp_gate, w_exp_up, w_exp_down):
    bsz, t, d = x_prompt.shape
    bs = x_sample.shape[0]
    heads, dk, dv = state_gla.shape[2:]
    cw = state_conv.shape[3]
    lowrank = w_gate_up.shape[1]
    hk, hv = heads * dk, heads * dv
    n_experts = w_router.shape[2]
    g_s5, p_s5 = s5_a_re.shape[1:]

    mod = _ada(jnp.concatenate([c_prompt, c_sample], axis=0), w_ada, b_ada)
    mod_p = [mod[l, :bsz].reshape(bsz, 1, N_MOD * d) for l in range(2)]
    mod_s = [mod[l, bsz:] for l in range(2)]
    SH_M, SC_M, GT_M, SH_F, SC_F, GT_F = range(N_MOD)
    gn = lambda l, k: g_norm[l, k].reshape(1, d)

    w0 = w_in0[0]
    o_lr = 2 * hk + 2 * hv
    w0r = jnp.concatenate([w0[:, :o_lr], w0[:, o_lr + lowrank:], w0[:, o_lr:o_lr + lowrank],
                           jnp.zeros((d, LANES - lowrank), F32)], axis=1)
    wgu_pad = jnp.concatenate([w_gate_up[0], jnp.zeros((LANES - lowrank, hk), F32)], axis=0)
    zoh = _s5_zoh(s5_a_re[0], s5_a_im[0], s5_log_dt[0], s5_b_re[0], s5_b_im[0])
    bb_blk, c_blk, ab_re, ab_im, d_blk = _s5_step_blocks(*zoh, s5_c_re[0], s5_c_im[0], s5_d[0])
    wr_pad = jnp.concatenate([w_router[0], jnp.zeros((d, LANES - n_experts), F32)], axis=1)
    bf = lambda a: a.astype(BF16)

    tm = TOKEN_TILE
    tpb = t // tm
    pk = dict(per_token=False, tiles_per_batch=tpb)
    xp = x_prompt.reshape(bsz * t, d)
    proj, w_o0, w_fg, w_fu, w_fd, w_i1, w_gl = _inproj(
        xp, gn(0, 0), mod_p[0], SC_M, SH_M, bf(w0r), tm=tm, bn=w0r.shape[1], prec="bf16",
        vmem_mb=VMEM_LIMIT_MB, cast_ws=(w_out0[0], w_ffn_gate[0], w_ffn_up[0], w_ffn_down[0], w_in1[0], w_glu[0]), **pk)
    mixed, st_t, cst = _gla_conv_prompt(
        proj, bf(wgu_pad), b_gate_up[0].reshape(1, hk), g_head_norm[0].reshape(1, dv), w_conv[0],
        bsz=bsz, t=t, tt=SEQ_TILE, heads=heads, dk=dk, dv=dv, cw=cw)
    st_t = st_t.reshape(bsz, heads // 2, 2, dv, 2, dk)
    gla_p = jnp.stack([st_t[:, :, 0, :, 0, :], st_t[:, :, 1, :, 1, :]], axis=2)
    gla_p = jnp.swapaxes(gla_p.reshape(bsz, heads, dv, dk), 2, 3)
    x2, u = _mix_ffn_proj(mixed, w_o0, xp, mod_p[0], mod_p[1], g_norm, w_fg, w_fu, w_fd, w_i1, tm=tm,
                          tiles_per_batch=tpb, vmem_mb=VMEM_LIMIT_FFN_MB)
    ew = (w_exp_gate[0], w_exp_up[0], w_exp_down[0])
    yg, re_p, im_p, *ew_bf = _s5_prompt(
        u.reshape(bsz, t, d), bf(bb_blk), bf(c_blk), ab_re, ab_im, d_blk,
        tuple(w.reshape(-1, w.shape[2]) for w in ew), tt=SEQ_TILE)
    w_eg, w_eu, w_ed = (wb.reshape(w.shape) for wb, w in zip(ew_bf, ew))
    x3, comb, hb = _glu_route(yg.reshape(bsz * t, d), w_gl, x2, mod_p[1], g_norm, wr_pad, tm=tm,
                              n_experts=n_experts, tiles_per_batch=tpb, vmem_mb=VMEM_LIMIT_MB)

    sk = dict(per_token=True, tiles_per_batch=1)
    xs = x_sample.reshape(bs, d)
    proj_s = _inproj(xs, gn(0, 0), mod_s[0], SC_M, SH_M, w0r, tm=bs, bn=SAMPLE_PROJ_COLS, prec="f32", **sk)
    s_t = jnp.transpose(state_gla[:, 0], (1, 2, 3, 0))
    cbuf_t = jnp.transpose(state_conv[:, 0], (1, 2, 0))
    o_t, y_t, sn_t, cn_t = _gla_conv_step(
        proj_s.T, wgu_pad.T, b_gate_up[0].reshape(hk, 1), g_head_norm[0].reshape(dv, 1),
        w_conv[0].reshape(3, cw, 1), s_t, cbuf_t, heads=heads, dk=dk, dv=dv, cw=cw)
    mixed_s = jnp.concatenate([o_t, y_t], axis=0).T
    gla_s = jnp.transpose(sn_t, (3, 0, 1, 2))
    conv_s = jnp.transpose(cn_t, (2, 0, 1))
    x1s = _outproj(mixed_s, w_out0[0], xs, mod_s[0], GT_M, gn(0, 1), tm=bs, prec="f32", glu=False,
                   **sk)
    x2s = _ffn(x1s, gn(0, 2), mod_s[0], SC_F, SH_F, GT_F, gn(0, 3), w_ffn_gate[0], w_ffn_up[0],
               w_ffn_down[0], tm=bs, fc=SAMPLE_COLS, prec="f32", **sk)
    us = _inproj(x2s, gn(1, 0), mod_s[1], SC_M, SH_M, w_in1[0], tm=bs, bn=SAMPLE_COLS, prec="f32", **sk)
    ygs, re_s, im_s = _s5_step(us, bb_blk, c_blk, ab_re, ab_im, d_blk,
                               state_s5_re[:, 0].reshape(bs, g_s5 * p_s5),
                               state_s5_im[:, 0].reshape(bs, g_s5 * p_s5))
    x3s = _outproj(ygs, w_glu[0], x2s, mod_s[1], GT_M, gn(1, 1), tm=bs, prec="f32", glu=True,
                   vmem_mb=VMEM_LIMIT_MB, **sk)
    comb_s, hb_s = _router(x3s, gn(1, 2), mod_s[1], SC_F, SH_F, wr_pad, tm=bs, n_experts=n_experts,
                           **sk)

    assert bs <= MOE_TB
    n_blk = bsz * t // MOE_TB + 1
    n_tiles_max = -(-(n_blk * _moe_block_rows(n_experts)) // MOE_TM) + n_experts
    pad_rows = lambda a: jnp.concatenate([a, jnp.zeros((MOE_TB - bs, a.shape[1]), a.dtype)], axis=0)
    hb_s, comb_s = pad_rows(hb_s), pad_rows(comb_s)
    seg, gap_pos, gap_len, tile_expert, n_tiles = _moe_schedule((comb, comb_s), n_experts,
                                                                n_tiles_max)
    dest, xs_sorted = _moe_sort(seg, gap_pos, gap_len, hb, comb, hb_s, comb_s,
                                n_tiles_max * MOE_TM, n_experts=n_experts)
    ys_sorted = _experts(tile_expert, n_tiles, xs_sorted, w_eg, w_eu, w_ed, fc=EXPERT_FF_CHUNK)
    x4 = _moe_combine(*seg, ys_sorted, dest, x3, mod_p[1], GT_F, gn(1, 3), blk0=0,
                      n_experts=n_experts, per_token=False, tiles_per_batch=t // MOE_TB)
    x4s = _moe_combine(*seg, ys_sorted, dest, pad_rows(x3s), pad_rows(mod_s[1]), GT_F,
                       gn(1, 3), blk0=n_blk - 1, n_experts=n_experts, per_token=True,
                       tiles_per_batch=1)[:bs]

    return (x4.reshape(bsz, t, d), x4s.reshape(bs, 1, d),
            gla_p[:, None], cst[:, None],
            re_p.reshape(bsz, 1, g_s5, p_s5), im_p.reshape(bsz, 1, g_s5, p_s5),
            gla_s[:, None], conv_s[:, None],
            re_s.reshape(bs, 1, g_s5, p_s5), im_s.reshape(bs, 1, g_s5, p_s5))
```

```python
import functools
import math

import jax
import jax.numpy as jnp
from jax import lax
from jax.experimental import pallas as pl
from jax.experimental.pallas import tpu as pltpu

F32 = jnp.float32
BF16 = jnp.bfloat16
EPS = 1e-6
LANES = 128
GLA_CHUNK = 64
GATE_NORMALIZER = 16.0
N_MOD = 6
S5_GROUP = 16
S5_CHUNK_GROUPS = 8

TOKEN_TILE = 512
SEQ_TILE = 256
VMEM_LIMIT_MB = 48
VMEM_LIMIT_FFN_MB = 56
EXPERT_FF_CHUNK = 1792
SAMPLE_PROJ_COLS = 640
SAMPLE_COLS = 256


def _dot(a, b, dims=None):
    if dims is None:
        return jnp.dot(a, b, preferred_element_type=F32)
    return lax.dot_general(a, b, (dims, ((), ())), preferred_element_type=F32)


def _split(a, parts=2):
    rem = a.astype(F32)
    out = []
    for _ in range(parts - 1):
        piece = rem.astype(BF16)
        out.append(piece)
        rem = rem - piece.astype(F32)
    out.append(rem.astype(BF16))
    return out


def _mm(a, w, prec):
    if prec == "bf16":
        return _dot(a.astype(BF16), w.astype(BF16))
    a1, a2, a3 = _split(a, 3)
    w1, w2, w3 = _split(w, 3)
    small = (_dot(a1, w3) + _dot(a3, w1)) + _dot(a2, w2)
    return _dot(a1, w1) + ((_dot(a1, w2) + _dot(a2, w1)) + small)


def _silu(x):
    return x * jax.nn.sigmoid(x)


def _gelu_tanh(x):
    return 0.5 * x * (1.0 + jnp.tanh(math.sqrt(2.0 / math.pi) * (x + 0.044715 * (x * x * x))))


def _log_sigmoid(x):
    return -(jnp.maximum(-x, 0.0) + jnp.log1p(jnp.exp(-jnp.abs(x))))


def _rms(x):
    return x * lax.rsqrt(jnp.mean(x * x, axis=-1, keepdims=True) + EPS)


def _mod_row(ref):
    return ref[0] if len(ref.shape) == 3 else ref[...]


def _normmod(x, g, sc, sh):
    return (_rms(x) * g) * (1.0 + sc) + sh


def _mod_spec(per_token, tm, d, col, tiles_per_batch):
    if per_token:
        return pl.BlockSpec((tm, d), lambda i, *_: (i, col))
    return pl.BlockSpec((1, 1, d), lambda i, *_: (i // tiles_per_batch, 0, col))


def _params(sem, vmem_mb=None):
    kw = dict(dimension_semantics=sem)
    if vmem_mb is not None:
        kw["vmem_limit_bytes"] = vmem_mb << 20
    return pltpu.CompilerParams(**kw)


def _ada_kernel(c_ref, w_ref, b_ref, o_ref):
    o_ref[0] = _mm(_silu(c_ref[...]), w_ref[0], "f32") + b_ref[0]


def _ada(c_all, w_ada, b_ada):
    depth, d, n6 = w_ada.shape
    rows = c_all.shape[0]
    return pl.pallas_call(
        _ada_kernel,
        grid=(depth, n6 // d),
        in_specs=[pl.BlockSpec((rows, d), lambda l, j: (0, 0)),
                  pl.BlockSpec((1, d, d), lambda l, j: (l, 0, j)),
                  pl.BlockSpec((1, 1, d), lambda l, j: (l, 0, j))],
        out_specs=pl.BlockSpec((1, rows, d), lambda l, j: (l, 0, j)),
        out_shape=jax.ShapeDtypeStruct((depth, rows, n6), F32),
        compiler_params=_params(("parallel", "parallel")),
    )(c_all, w_ada, b_ada.reshape(depth, 1, n6))


BF16_ROWS = 16


def _passenger_specs(ws, n_steps, step):
    specs = []
    for w in ws:
        rows, cols = w.shape
        units = rows // BF16_ROWS
        n_used = max(k for k in range(1, n_steps + 1) if units % k == 0)
        specs.append(pl.BlockSpec(
            (rows // n_used, cols),
            lambda *idx, n_used=n_used: (jnp.minimum(step(*idx), n_used - 1), 0)))
    return specs


def _cast_passengers(in_refs, out_refs):
    for src, dst in zip(in_refs, out_refs):
        dst[...] = src[...].astype(dst.dtype)


def _inproj_kernel(*refs, prec, n_cast):
    x_ref, g_ref, sc_ref, sh_ref, w_ref = refs[:5]
    o_ref = refs[5 + n_cast]
    _cast_passengers(refs[5:5 + n_cast], refs[6 + n_cast:])
    h = _normmod(x_ref[...], g_ref[...], _mod_row(sc_ref), _mod_row(sh_ref))
    o_ref[...] = _mm(h, w_ref[...], prec).astype(o_ref.dtype)


def _inproj(x2d, g, mod, sc_col, sh_col, w, *, tm, bn, prec, per_token, tiles_per_batch,
            out_dtype=F32, vmem_mb=None, cast_ws=()):
    n_tok, d = x2d.shape
    n_out = w.shape[1]
    n_j = n_out // bn
    p_specs = _passenger_specs(cast_ws, n_tok // tm * n_j, lambda i, j: i * n_j + j)
    out = pl.pallas_call(
        functools.partial(_inproj_kernel, prec=prec, n_cast=len(cast_ws)),
        grid=(n_tok // tm, n_j),
        in_specs=[pl.BlockSpec((tm, d), lambda i, j: (i, 0)),
                  pl.BlockSpec((1, d), lambda i, j: (0, 0)),
                  _mod_spec(per_token, tm, d, sc_col, tiles_per_batch),
                  _mod_spec(per_token, tm, d, sh_col, tiles_per_batch),
                  pl.BlockSpec((d, bn), lambda i, j: (0, j))] + p_specs,
        out_specs=[pl.BlockSpec((tm, bn), lambda i, j: (i, j))] + p_specs,
        out_shape=[jax.ShapeDtypeStruct((n_tok, n_out), out_dtype)]
        + [jax.ShapeDtypeStruct(c.shape, BF16) for c in cast_ws],
        compiler_params=_params(("arbitrary", "arbitrary"), vmem_mb),
    )(x2d, g, mod, mod, w, *cast_ws)
    return out if cast_ws else out[0]


def _outproj_kernel(a_ref, w_ref, x_ref, gt_ref, g_ref, o_ref, *, prec, glu):
    z = _mm(a_ref[...], w_ref[...], prec)
    if glu:
        d = z.shape[1] // 2
        z = z[:, :d] * jax.nn.sigmoid(z[:, d:])
    o_ref[...] = x_ref[...] + _mod_row(gt_ref) * (_rms(z) * g_ref[...])


def _outproj(a2d, w, x2d, mod, gt_col, g, *, tm, prec, glu, per_token, tiles_per_batch,
             vmem_mb=None):
    n_tok, d = x2d.shape
    k, n_out = w.shape
    return pl.pallas_call(
        functools.partial(_outproj_kernel, prec=prec, glu=glu),
        grid=(n_tok // tm,),
        in_specs=[pl.BlockSpec((tm, k), lambda i: (i, 0)),
                  pl.BlockSpec((k, n_out), lambda i: (0, 0)),
                  pl.BlockSpec((tm, d), lambda i: (i, 0)),
                  _mod_spec(per_token, tm, d, gt_col, tiles_per_batch),
                  pl.BlockSpec((1, d), lambda i: (0, 0))],
        out_specs=pl.BlockSpec((tm, d), lambda i: (i, 0)),
        out_shape=jax.ShapeDtypeStruct((n_tok, d), F32),
        compiler_params=_params(("parallel",), vmem_mb),
    )(a2d, w, x2d, mod, g)


def _mix_ffn_proj_kernel(a_ref, wo_ref, x_ref, gtm_ref, g01_ref, g02_ref, scf_ref, shf_ref, wg_ref,
                         wu_ref, wd_ref, gtf_ref, g03_ref, g10_ref, sc1_ref, sh1_ref, wi_ref,
                         x2_ref, u_ref):
    x1 = x_ref[...] + _mod_row(gtm_ref) * (_rms(_mm(a_ref[...], wo_ref[...], "bf16")) * g01_ref[...])
    h = _normmod(x1, g02_ref[...], _mod_row(scf_ref), _mod_row(shf_ref))
    act = _silu(_mm(h, wg_ref[...], "bf16")) * _mm(h, wu_ref[...], "bf16")
    x2 = x1 + _mod_row(gtf_ref) * (_rms(_mm(act, wd_ref[...], "bf16")) * g03_ref[...])
    x2_ref[...] = x2
    h1 = _normmod(x2, g10_ref[...], _mod_row(sc1_ref), _mod_row(sh1_ref))
    u_ref[...] = _mm(h1, wi_ref[...], "bf16").astype(u_ref.dtype)


def _mix_ffn_proj(mixed, w_out, x2d, mod0, mod1, g_norm, w_gate, w_up, w_down, w_in1, *, tm,
                  tiles_per_batch, vmem_mb):
    n_tok, d = x2d.shape
    sh_m, sc_m, gt_m, sh_f, sc_f, gt_f = range(N_MOD)
    tok = lambda w: pl.BlockSpec((tm, w), lambda i: (i, 0))
    vec = pl.BlockSpec((1, d), lambda i: (0, 0))
    mod = lambda col: _mod_spec(False, tm, d, col, tiles_per_batch)
    once = lambda w: pl.BlockSpec(w.shape, lambda i: (0, 0), pipeline_mode=pl.Buffered(1))
    g = lambda l, k: g_norm[l, k].reshape(1, d)
    return pl.pallas_call(
        _mix_ffn_proj_kernel,
        grid=(n_tok // tm,),
        in_specs=[tok(mixed.shape[1]), once(w_out), tok(d), mod(gt_m), vec, vec, mod(sc_f), mod(sh_f),
                  once(w_gate), once(w_up), once(w_down), mod(gt_f), vec, vec, mod(sc_m), mod(sh_m),
                  once(w_in1)],
        out_specs=[tok(d), tok(w_in1.shape[1])],
        out_shape=[jax.ShapeDtypeStruct((n_tok, d), F32),
                   jax.ShapeDtypeStruct((n_tok, w_in1.shape[1]), BF16)],
        compiler_params=_params(("parallel",), vmem_mb),
    )(mixed, w_out, x2d, mod0, g(0, 1), g(0, 2), mod0, mod0, w_gate, w_up, w_down, mod0, g(0, 3),
      g(1, 0), mod1, mod1, w_in1)


def _glu_route_kernel(a_ref, w_ref, x_ref, gtm_ref, g1_ref, g2_ref, scf_ref, shf_ref, wr_ref,
                      x3_ref, comb_ref, hb_ref, wp_scr, *, n_experts):
    @pl.when(pl.program_id(0) == 0)
    def _():
        _pack_router_weight(wr_ref, wp_scr, n_experts)

    z = _mm(a_ref[...], w_ref[...], "bf16")
    d = z.shape[1] // 2
    x3 = x_ref[...] + _mod_row(gtm_ref) * (_rms(z[:, :d] * jax.nn.sigmoid(z[:, d:])) * g1_ref[...])
    x3_ref[...] = x3
    h = _normmod(x3, g2_ref[...], _mod_row(scf_ref), _mod_row(shf_ref))
    hb_ref[...] = h.astype(hb_ref.dtype)
    comb_ref[...] = _route(h, wp_scr[...], n_experts)


def _glu_route(a2d, w_glu, x2d, mod1, g_norm, wr_pad, *, tm, n_experts, tiles_per_batch, vmem_mb):
    n_tok, d = x2d.shape
    sh_m, sc_m, gt_m, sh_f, sc_f, gt_f = range(N_MOD)
    tok = lambda w: pl.BlockSpec((tm, w), lambda i: (i, 0))
    vec = pl.BlockSpec((1, d), lambda i: (0, 0))
    mod = lambda col: _mod_spec(False, tm, d, col, tiles_per_batch)
    g = lambda k: g_norm[1, k].reshape(1, d)
    return pl.pallas_call(
        functools.partial(_glu_route_kernel, n_experts=n_experts),
        grid=(n_tok // tm,),
        in_specs=[tok(a2d.shape[1]), pl.BlockSpec(w_glu.shape, lambda i: (0, 0)), tok(d), mod(gt_m),
                  vec, vec, mod(sc_f), mod(sh_f), pl.BlockSpec(wr_pad.shape, lambda i: (0, 0))],
        out_specs=[tok(d), tok(LANES), tok(d)],
        out_shape=[jax.ShapeDtypeStruct((n_tok, d), F32),
                   jax.ShapeDtypeStruct((n_tok, LANES), F32),
                   jax.ShapeDtypeStruct((n_tok, d), BF16)],
        scratch_shapes=[pltpu.VMEM(wr_pad.shape, BF16)],
        compiler_params=_params(("arbitrary",), vmem_mb),
    )(a2d, w_glu, x2d, mod1, g(1), g(2), mod1, mod1, wr_pad)


def _ffn_kernel(x_ref, g1_ref, sc_ref, sh_ref, wg_ref, wu_ref, wd_ref, gt_ref, g2_ref,
                o_ref, h_scr, acc_scr, *, prec, n_c):
    c = pl.program_id(1)

    @pl.when(c == 0)
    def _():
        h = _normmod(x_ref[...], g1_ref[...], _mod_row(sc_ref), _mod_row(sh_ref))
        h_scr[...] = h.astype(h_scr.dtype)
        acc_scr[...] = jnp.zeros_like(acc_scr)

    h = h_scr[...]
    act = _silu(_mm(h, wg_ref[...], prec)) * _mm(h, wu_ref[...], prec)
    acc_scr[...] += _mm(act, wd_ref[...], prec)

    @pl.when(c == n_c - 1)
    def _():
        o_ref[...] = x_ref[...] + _mod_row(gt_ref) * (_rms(acc_scr[...]) * g2_ref[...])


def _ffn(x2d, g1, mod, sc_col, sh_col, gt_col, g2, wg, wu, wd, *, tm, fc, prec,
         per_token, tiles_per_batch, vmem_mb=None):
    n_tok, d = x2d.shape
    n_c = wg.shape[1] // fc
    tok = pl.BlockSpec((tm, d), lambda i, c: (i, 0))
    vec = pl.BlockSpec((1, d), lambda i, c: (0, 0))
    h_dtype = BF16 if prec == "bf16" else F32
    w_mode = dict(pipeline_mode=pl.Buffered(1)) if n_c == 1 else {}
    return pl.pallas_call(
        functools.partial(_ffn_kernel, prec=prec, n_c=n_c),
        grid=(n_tok // tm, n_c),
        in_specs=[tok, vec,
                  _mod_spec(per_token, tm, d, sc_col, tiles_per_batch),
                  _mod_spec(per_token, tm, d, sh_col, tiles_per_batch),
                  pl.BlockSpec((d, fc), lambda i, c: (0, c), **w_mode),
                  pl.BlockSpec((d, fc), lambda i, c: (0, c), **w_mode),
                  pl.BlockSpec((fc, d), lambda i, c: (c, 0), **w_mode),
                  _mod_spec(per_token, tm, d, gt_col, tiles_per_batch), vec],
        out_specs=tok,
        out_shape=jax.ShapeDtypeStruct((n_tok, d), F32),
        scratch_shapes=[pltpu.VMEM((tm, d), h_dtype), pltpu.VMEM((tm, d), F32)],
        compiler_params=_params(("parallel", "arbitrary"), vmem_mb),
    )(x2d, g1, mod, mod, wg, wu, wd, mod, g2)


def _pack_router_weight(wr_ref, wp_scr, n_experts):
    w1, w2, w3 = _split(wr_ref[...], 3)
    fwd = lambda x, k: jnp.concatenate([x[:, LANES - k * n_experts:], x[:, :LANES - k * n_experts]], axis=1)
    packed = w1.astype(F32) + (fwd(w2.astype(F32), 1) + fwd(w3.astype(F32), 2))
    wp_scr[...] = packed.astype(wp_scr.dtype)


def _route(h, w, n_experts):
    h1, h2, h3 = _split(h, 3)
    p1, p2, p3 = _dot(h1, w), _dot(h2, w), _dot(h3, w)
    back = lambda x, k: jnp.concatenate([x[:, k * n_experts:], x[:, :k * n_experts]], axis=1)
    small = (back(p1, 2) + p3) + back(p2, 1)
    logits = p1 + ((back(p1, 1) + p2) + small)
    lane = lax.broadcasted_iota(jnp.int32, logits.shape, 1).astype(F32)
    neg = -jnp.inf
    l1 = jnp.where(lane < n_experts, logits, neg)
    m1 = jnp.max(l1, axis=-1, keepdims=True)
    i1 = jnp.min(jnp.where(l1 == m1, lane, float(LANES)), axis=-1, keepdims=True)
    l2 = jnp.where(lane == i1, neg, l1)
    m2 = jnp.max(l2, axis=-1, keepdims=True)
    i2 = jnp.min(jnp.where(l2 == m2, lane, float(LANES)), axis=-1, keepdims=True)
    e2 = jnp.exp(m2 - m1)
    den = 1.0 + e2
    return jnp.where(lane == i1, 1.0 / den, 0.0) + jnp.where(lane == i2, e2 / den, 0.0)


def _router_kernel(x_ref, g_ref, sc_ref, sh_ref, wr_ref, comb_ref, hb_ref, wp_scr, *, n_experts):
    @pl.when(pl.program_id(0) == 0)
    def _():
        _pack_router_weight(wr_ref, wp_scr, n_experts)

    h = _normmod(x_ref[...], g_ref[...], _mod_row(sc_ref), _mod_row(sh_ref))
    hb_ref[...] = h.astype(hb_ref.dtype)
    comb_ref[...] = _route(h, wp_scr[...], n_experts)


def _router(x2d, g, mod, sc_col, sh_col, wr_pad, *, tm, n_experts, per_token, tiles_per_batch):
    n_tok, d = x2d.shape
    return pl.pallas_call(
        functools.partial(_router_kernel, n_experts=n_experts),
        grid=(n_tok // tm,),
        in_specs=[pl.BlockSpec((tm, d), lambda i: (i, 0)),
                  pl.BlockSpec((1, d), lambda i: (0, 0)),
                  _mod_spec(per_token, tm, d, sc_col, tiles_per_batch),
                  _mod_spec(per_token, tm, d, sh_col, tiles_per_batch),
                  pl.BlockSpec((d, LANES), lambda i: (0, 0))],
        out_specs=[pl.BlockSpec((tm, LANES), lambda i: (i, 0)),
                   pl.BlockSpec((tm, d), lambda i: (i, 0))],
        out_shape=[jax.ShapeDtypeStruct((n_tok, LANES), F32),
                   jax.ShapeDtypeStruct((n_tok, d), BF16)],
        scratch_shapes=[pltpu.VMEM((d, LANES), BF16)],
        compiler_params=_params(("arbitrary",)),
    )(x2d, g, mod, mod, wr_pad)


MOE_TB = 256
MOE_SEG = 8
MOE_DTYPE = F32
MOE_TM = 512


def _moe_block_rows(n_experts):
    return -(-(2 * MOE_TB + n_experts * (MOE_SEG - 1)) // LANES) * LANES


def _routing_slots(comb, dest):
    lane = lax.broadcasted_iota(jnp.int32, comb.shape, 1).astype(F32)
    sel = comb > 0.0
    i_lo = jnp.min(jnp.where(sel, lane, float(LANES)), axis=-1, keepdims=True)
    i_hi = jnp.max(jnp.where(sel, lane, -1.0), axis=-1, keepdims=True)
    pick = lambda i, v: jnp.sum(jnp.where(lane == i, v, 0.0), axis=-1, keepdims=True)
    has_lo = i_lo < float(LANES)
    has_hi = i_hi > i_lo
    d_lo = jnp.where(has_lo, pick(i_lo, dest), -1.0)
    d_hi = jnp.where(has_hi, pick(i_hi, dest), -1.0)
    w_lo = pick(i_lo, comb)
    w_hi = jnp.where(has_hi, pick(i_hi, comb), 0.0)
    out = jnp.where(lane == 0.0, d_lo, 0.0)
    for k, v in ((1.0, d_hi), (2.0, w_lo), (3.0, w_hi)):
        out = jnp.where(lane == k, v, out)
    return out


def _one_hot_rows(row, n_rows):
    tb = row.shape[0]
    lane = lax.broadcasted_iota(jnp.int32, (1, LANES), 1).astype(F32)
    row = jnp.broadcast_to(row, (tb, LANES))
    blocks = [jnp.where(row == lane + float(cb * LANES), 1.0, 0.0) for cb in range(n_rows // LANES)]
    return jnp.concatenate(blocks, axis=1).astype(BF16)


_SEG_BITS = (256, 128, 64, 32, 16, 8)


def _segment_copies(make_copy, blk, pos_ref, off_ref, cnt_ref, n_experts, start):
    for e in range(n_experts):
        seg = blk * n_experts + e
        in_blk, in_sorted, n = off_ref[seg], pos_ref[seg], cnt_ref[seg]
        for size in _SEG_BITS:
            has = (n & size) != 0

            @pl.when(has)
            def _(in_blk=in_blk, in_sorted=in_sorted, size=size):
                cp = make_copy(pl.multiple_of(in_blk, MOE_SEG), pl.multiple_of(in_sorted, MOE_SEG), size)
                cp.start() if start else cp.wait()

            step = jnp.where(has, size, 0)
            in_blk, in_sorted = in_blk + step, in_sorted + step


def _moe_sort_kernel(pos_ref, off_ref, cnt_ref, gpos_ref, glen_ref, hbp_ref, combp_ref, hbs_ref,
                     combs_ref, dest_ref, xs_hbm, buf, zbuf, sems, *, n_rows, n_experts, n_blk, n_blk_p):
    b = pl.program_id(0)
    slot = b % 2

    def writes(blk, slot, start):
        def make_copy(block_row, sorted_row, size):
            return pltpu.make_async_copy(buf.at[slot, pl.ds(block_row, size)],
                                         xs_hbm.at[pl.ds(sorted_row, size)], sems.at[slot])
        _segment_copies(make_copy, blk, pos_ref, off_ref, cnt_ref, n_experts, start)

    def zero_gaps(start):
        for e in range(n_experts):
            row, n = gpos_ref[e], glen_ref[e]
            for size in _SEG_BITS:
                has = (n & size) != 0

                @pl.when(has)
                def _(row=row, size=size):
                    cp = pltpu.make_async_copy(
                        zbuf.at[pl.ds(0, size)],
                        xs_hbm.at[pl.ds(pl.multiple_of(row, MOE_SEG), size)], sems.at[2])
                    cp.start() if start else cp.wait()

                row = row + jnp.where(has, size, 0)
        tail_row, tail_n, big = gpos_ref[n_experts], glen_ref[n_experts], _SEG_BITS[0]

        def piece(i, carry):
            cp = pltpu.make_async_copy(
                zbuf.at[pl.ds(0, big)],
                xs_hbm.at[pl.ds(pl.multiple_of(tail_row + i * big, MOE_SEG), big)], sems.at[2])
            cp.start() if start else cp.wait()
            return carry

        lax.fori_loop(0, tail_n // big, piece, 0)

    @pl.when(b == 0)
    def _():
        zbuf[...] = jnp.zeros_like(zbuf)
        zero_gaps(True)

    @pl.when(b >= 2)
    def _():
        writes(b - 2, slot, False)

    in_prompt = b < n_blk_p
    comb = jnp.where(in_prompt, combp_ref[...], combs_ref[...])
    hb = jnp.where(in_prompt, hbp_ref[...], hbs_ref[...])
    tb = comb.shape[0]
    sel = jnp.where(comb > 0.0, 1.0, 0.0)
    r = lax.broadcasted_iota(jnp.int32, (tb, tb), 0)
    c = lax.broadcasted_iota(jnp.int32, (tb, tb), 1)
    rank = _dot(jnp.where(c < r, 1.0, 0.0).astype(BF16), sel.astype(BF16))
    cnt = jnp.sum(sel, axis=0, keepdims=True)
    cnt = jnp.floor((cnt + (MOE_SEG - 1)) / MOE_SEG) * MOE_SEG
    ru = lax.broadcasted_iota(jnp.int32, (LANES, LANES), 0)
    cu = lax.broadcasted_iota(jnp.int32, (LANES, LANES), 1)
    off = _dot(jnp.broadcast_to(cnt, (8, LANES)).astype(BF16),
               jnp.where(ru < cu, 1.0, 0.0).astype(BF16))[0:1]
    slots = _routing_slots(comb, off + rank)
    dest_ref[...] = slots
    pt = _one_hot_rows(slots[:, 0:1], n_rows) + _one_hot_rows(slots[:, 1:2], n_rows)
    buf[slot] = _dot(pt, hb, ((0,), (0,))).astype(buf.dtype)
    writes(b, slot, True)

    @pl.when(b == n_blk - 1)
    def _():
        writes(b, slot, False)
        if n_blk > 1:
            writes(b - 1, 1 - slot, False)
        zero_gaps(False)


def _moe_sort(seg, gap_pos, gap_len, hb_p, comb_p, hb_s, comb_s, n_sorted_rows, *, n_experts):
    n_p, d = hb_p.shape
    n_blk_p = n_p // MOE_TB
    n_blk = n_blk_p + 1
    n_rows = _moe_block_rows(n_experts)
    last_p = n_blk_p - 1
    return pl.pallas_call(
        functools.partial(_moe_sort_kernel, n_rows=n_rows, n_experts=n_experts, n_blk=n_blk,
                          n_blk_p=n_blk_p),
        grid_spec=pltpu.PrefetchScalarGridSpec(
            num_scalar_prefetch=5, grid=(n_blk,),
            in_specs=[pl.BlockSpec((MOE_TB, d), lambda b, *_: (jnp.minimum(b, last_p), 0)),
                      pl.BlockSpec((MOE_TB, LANES), lambda b, *_: (jnp.minimum(b, last_p), 0)),
                      pl.BlockSpec((MOE_TB, d), lambda b, *_: (0, 0)),
                      pl.BlockSpec((MOE_TB, LANES), lambda b, *_: (0, 0))],
            out_specs=[pl.BlockSpec((MOE_TB, LANES), lambda b, *_: (b, 0)),
                       pl.BlockSpec(memory_space=pl.ANY)],
            scratch_shapes=[pltpu.VMEM((2, n_rows, d), MOE_DTYPE),
                            pltpu.VMEM((_SEG_BITS[0], d), MOE_DTYPE),
                            pltpu.SemaphoreType.DMA((3,))]),
        out_shape=[jax.ShapeDtypeStruct((n_blk * MOE_TB, LANES), F32),
                   jax.ShapeDtypeStruct((n_sorted_rows, d), MOE_DTYPE)],
        compiler_params=_params(("arbitrary",)),
    )(*seg, gap_pos, gap_len, hb_p, comb_p, hb_s, comb_s)


def _experts_kernel(te_ref, nt_ref, x_ref, wg_ref, wu_ref, wd_ref, o_ref, acc_scr, *, n_c):
    del te_ref
    k = pl.program_id(0)
    c = pl.program_id(1)
    active = k < nt_ref[0]

    @pl.when(active)
    def _():
        h = x_ref[...].astype(BF16)
        act = _silu(_dot(h, wg_ref[0])) * _dot(h, wu_ref[0])
        y = _dot(act.astype(BF16), wd_ref[0])

        @pl.when(c == 0)
        def _():
            acc_scr[...] = y

        @pl.when(c > 0)
        def _():
            acc_scr[...] += y

        @pl.when(c == n_c - 1)
        def _():
            o_ref[...] = acc_scr[...].astype(o_ref.dtype)

    @pl.when(jnp.logical_not(active) & (c == n_c - 1))
    def _():
        o_ref[...] = jnp.zeros_like(o_ref)


def _experts(tile_expert, n_tiles, xs, wg, wu, wd, *, fc):
    n_rows, d = xs.shape
    n_c = wg.shape[2] // fc

    def row_map(k, c, te, nt):
        return (jnp.minimum(k, nt[0] - 1), 0)

    def w_map(k, c, te, nt):
        return (te[k], 0, jnp.where(k < nt[0], c, n_c - 1))

    def wd_map(k, c, te, nt):
        return (te[k], jnp.where(k < nt[0], c, n_c - 1), 0)

    return pl.pallas_call(
        functools.partial(_experts_kernel, n_c=n_c),
        grid_spec=pltpu.PrefetchScalarGridSpec(
            num_scalar_prefetch=2, grid=(n_rows // MOE_TM, n_c),
            in_specs=[pl.BlockSpec((MOE_TM, d), row_map),
                      pl.BlockSpec((1, d, fc), w_map), pl.BlockSpec((1, d, fc), w_map),
                      pl.BlockSpec((1, fc, d), wd_map)],
            out_specs=pl.BlockSpec((MOE_TM, d), lambda k, c, te, nt: (k, 0)),
            scratch_shapes=[pltpu.VMEM((MOE_TM, d), F32)]),
        out_shape=jax.ShapeDtypeStruct((n_rows, d), MOE_DTYPE),
        compiler_params=_params(("arbitrary", "arbitrary"), VMEM_LIMIT_MB),
    )(tile_expert, n_tiles, xs, wg, wu, wd)


def _moe_combine_kernel(pos_ref, off_ref, cnt_ref, ys_hbm, dest_ref, x_ref, gt_ref, g_ref,
                        o_ref, buf, sems, *, n_experts, n_blk, blk0):
    b = pl.program_id(0)
    slot = b % 2

    def reads(blk, slot, start):
        def make_copy(block_row, sorted_row, size):
            return pltpu.make_async_copy(ys_hbm.at[pl.ds(sorted_row, size)],
                                         buf.at[slot, pl.ds(block_row, size)], sems.at[slot])
        _segment_copies(make_copy, blk0 + blk, pos_ref, off_ref, cnt_ref, n_experts, start)

    @pl.when(b == 0)
    def _():
        buf[...] = jnp.zeros_like(buf)
        reads(b, slot, True)

    if n_blk > 1:
        @pl.when(b + 1 < n_blk)
        def _():
            reads(b + 1, 1 - slot, True)

    reads(b, slot, False)
    yb = buf[slot].astype(BF16)
    slots = dest_ref[...]
    n_rows = yb.shape[0]
    y = (slots[:, 2:3] * _dot(_one_hot_rows(slots[:, 0:1], n_rows), yb)
         + slots[:, 3:4] * _dot(_one_hot_rows(slots[:, 1:2], n_rows), yb))
    o_ref[...] = x_ref[...] + _mod_row(gt_ref) * (_rms(y) * g_ref[...])


def _moe_combine(sorted_pos, block_off, seg_cnt, ys, dest, x2d, mod, gt_col, g, *, blk0,
                 n_experts, per_token, tiles_per_batch):
    n_tok, d = x2d.shape
    n_blk = n_tok // MOE_TB
    n_rows = _moe_block_rows(n_experts)
    return pl.pallas_call(
        functools.partial(_moe_combine_kernel, n_experts=n_experts, n_blk=n_blk, blk0=blk0),
        grid_spec=pltpu.PrefetchScalarGridSpec(
            num_scalar_prefetch=3, grid=(n_blk,),
            in_specs=[pl.BlockSpec(memory_space=pl.ANY),
                      pl.BlockSpec((MOE_TB, LANES), lambda b, *_: (blk0 + b, 0)),
                      pl.BlockSpec((MOE_TB, d), lambda b, *_: (b, 0)),
                      _mod_spec(per_token, MOE_TB, d, gt_col, tiles_per_batch),
                      pl.BlockSpec((1, d), lambda b, *_: (0, 0))],
            out_specs=pl.BlockSpec((MOE_TB, d), lambda b, *_: (b, 0)),
            scratch_shapes=[pltpu.VMEM((2, n_rows, d), MOE_DTYPE), pltpu.SemaphoreType.DMA((2,))]),
        out_shape=jax.ShapeDtypeStruct((n_tok, d), F32),
        compiler_params=_params(("arbitrary",)),
    )(sorted_pos, block_off, seg_cnt, ys, dest, x2d, mod, g)


def _moe_schedule(combs, n_experts, n_tiles_max):
    def block_counts(comb):
        n_blk = comb.shape[0] // MOE_TB
        sel = (comb[:, :n_experts] > 0.0).astype(jnp.int32)
        return jnp.sum(sel.reshape(n_blk, MOE_TB, n_experts), axis=1)

    cnt = jnp.concatenate([block_counts(c) for c in combs], axis=0)
    cnt = -(-cnt // MOE_SEG) * MOE_SEG
    off = jnp.cumsum(cnt, axis=1) - cnt
    tot = jnp.sum(cnt, axis=0)
    grp = -(-tot // MOE_TM) * MOE_TM
    g_end = jnp.cumsum(grp)
    sorted_pos = (g_end - grp)[None, :] + (jnp.cumsum(cnt, axis=0) - cnt)
    n_tiles = g_end[-1] // MOE_TM
    first_row = jnp.arange(n_tiles_max, dtype=jnp.int32) * MOE_TM
    tile_expert = jnp.sum(first_row[:, None] >= g_end[None, :], axis=1)
    last = jnp.sum((n_tiles - 1) * MOE_TM >= g_end)
    tile_expert = jnp.minimum(tile_expert, last).astype(jnp.int32)
    flat = lambda a: a.reshape(-1).astype(jnp.int32)
    seg = (flat(sorted_pos), flat(off), flat(cnt))
    gap_pos = flat(jnp.concatenate([g_end - grp + tot, g_end[-1:]]))
    gap_len = flat(jnp.concatenate([grp - tot, n_tiles_max * MOE_TM - g_end[-1:]]))
    return seg, gap_pos, gap_len, tile_expert, n_tiles.reshape(1).astype(jnp.int32)


def _gla_conv_kernel(p_ref, wgu_ref, bgu_ref, gh_ref, wc_ref, mixed_ref, st_ref, cst_ref,
                     s_scr, uext_scr, *, tt, n_t, heads, dk, dv, cw):
    ti = pl.program_id(1)
    hk = heads * dk
    hv = heads * dv
    o_q, o_k, o_v, o_og = 0, hk, 2 * hk, 2 * hk + hv
    o_ch = o_og + hv
    o_cb, o_cc, o_lr = o_ch + cw, o_ch + 2 * cw, o_ch + 3 * cw
    L = GLA_CHUNK

    @pl.when(ti == 0)
    def _():
        s_scr[...] = jnp.zeros_like(s_scr)
        uext_scr[0:8, :] = jnp.zeros((8, cw), F32)

    logit = _dot(p_ref[:, o_lr:o_lr + LANES].astype(BF16), wgu_ref[...]) + bgu_ref[...]
    logg = _log_sigmoid(logit) / GATE_NORMALIZER

    r = lax.broadcasted_iota(jnp.int32, (tt, tt), 0)
    c = lax.broadcasted_iota(jnp.int32, (tt, tt), 1)
    tri = jnp.where((r // L == c // L) & (c <= r), 1.0, 0.0).astype(BF16)
    g_hi = logg.astype(BF16)
    rem = logg - g_hi.astype(F32)
    g_mid = rem.astype(BF16)
    g_lo = (rem - g_mid.astype(F32)).astype(BF16)
    bc = _dot(tri, g_hi) + (_dot(tri, g_mid) + _dot(tri, g_lo))

    rl = lax.broadcasted_iota(jnp.int32, (L, L), 0)
    cl = lax.broadcasted_iota(jnp.int32, (L, L), 1)
    tril = cl <= rl
    lane = lax.broadcasted_iota(jnp.int32, (L, LANES), 1)
    heads_per_blk = LANES // dk
    nt_dims = ((1,), (1,))

    state = [s_scr[h] for h in range(heads)]
    for ck in range(tt // L):
        rows = slice(L * ck, L * ck + L)
        b = bc[rows]
        bl = b[L - 1:L]
        q_in = (p_ref[rows, o_q:o_q + hk] * (dk ** -0.5)) * jnp.exp(b)
        k = p_ref[rows, o_k:o_k + hk]
        k_out = (k * jnp.exp(-b)).astype(BF16)
        k_dec = (k * jnp.exp(bl - b)).astype(BF16)
        dec = jnp.exp(bl)
        q_in = q_in.astype(BF16)
        for h in range(heads):
            blk = slice(LANES * (h // heads_per_blk), LANES * (h // heads_per_blk) + LANES)
            in_head = (lane // dk) == (h % heads_per_blk)
            qm = jnp.where(in_head, q_in[:, blk], jnp.zeros_like(q_in[:, blk]))
            att = _dot(qm, k_out[:, blk], nt_dims)
            att = jnp.where(tril, att, 0.0).astype(BF16)
            vh = p_ref[rows, o_v + dv * h:o_v + dv * h + dv].astype(BF16)
            o = _dot(att, vh) + _dot(qm, state[h].astype(BF16), nt_dims)
            ds_t = _dot(vh, k_dec[:, blk], ((0,), (0,)))
            state[h] = state[h] * dec[:, blk] + ds_t
            og = p_ref[rows, o_og + dv * h:o_og + dv * h + dv]
            res = (_rms(o) * gh_ref[...]) * _silu(og)
            mixed_ref[rows, dv * h:dv * h + dv] = res.astype(mixed_ref.dtype)
    for h in range(heads):
        s_scr[h] = state[h]

    u = p_ref[:, o_cc:o_cc + cw] * p_ref[:, o_ch:o_ch + cw]
    uext_scr[8:8 + tt, :] = u
    y = (wc_ref[0:1, :] * uext_scr[6:6 + tt, :] + wc_ref[1:2, :] * uext_scr[7:7 + tt, :]
         + wc_ref[2:3, :] * u)
    mixed_ref[:, hv:hv + cw] = (p_ref[:, o_cb:o_cb + cw] * y).astype(mixed_ref.dtype)
    tail = uext_scr[tt + 6:tt + 8, :]
    uext_scr[6:8, :] = tail

    @pl.when(ti == n_t - 1)
    def _():
        st_ref[0] = s_scr[...]
        cst_ref[0] = tail


def _gla_conv_prompt(proj, wgu_pad, bgu, g_head, w_conv, *, bsz, t, tt, heads, dk, dv, cw):
    n_t = t // tt
    n_in = proj.shape[1]
    width = heads * dv + cw
    return pl.pallas_call(
        functools.partial(_gla_conv_kernel, tt=tt, n_t=n_t, heads=heads, dk=dk, dv=dv, cw=cw),
        grid=(bsz, n_t),
        in_specs=[pl.BlockSpec((tt, n_in), lambda b, i: (b * n_t + i, 0)),
                  pl.BlockSpec(wgu_pad.shape, lambda b, i: (0, 0)),
                  pl.BlockSpec(bgu.shape, lambda b, i: (0, 0)),
                  pl.BlockSpec(g_head.shape, lambda b, i: (0, 0)),
                  pl.BlockSpec(w_conv.shape, lambda b, i: (0, 0))],
        out_specs=[pl.BlockSpec((tt, width), lambda b, i: (b * n_t + i, 0)),
                   pl.BlockSpec((1, heads, dv, LANES), lambda b, i: (b, 0, 0, 0)),
                   pl.BlockSpec((1, 2, cw), lambda b, i: (b, 0, 0))],
        out_shape=[jax.ShapeDtypeStruct((bsz * t, width), BF16),
                   jax.ShapeDtypeStruct((bsz, heads, dv, LANES), F32),
                   jax.ShapeDtypeStruct((bsz, 2, cw), F32)],
        scratch_shapes=[pltpu.VMEM((heads, dv, LANES), F32), pltpu.VMEM((tt + 8, cw), F32)],
        compiler_params=_params(("parallel", "arbitrary"), VMEM_LIMIT_MB),
    )(proj, wgu_pad, bgu, g_head, w_conv)


def _gla_conv_step_kernel(q_ref, k_ref, v_ref, og_ref, ch_ref, cb_ref, cc_ref, lr_ref,
                          wgu_ref, bgu_ref, gh_ref, wc_ref, s_ref, cbuf_ref,
                          o_ref, y_ref, sn_ref, cn_ref, a_scr, *, dk):
    logit = _mm(wgu_ref[...], lr_ref[...], "f32") + bgu_ref[...]
    a_scr[...] = jnp.exp(_log_sigmoid(logit) / GATE_NORMALIZER)
    v_t = v_ref[...]

    def body(d, acc):
        a = a_scr[pl.ds(d, 1), :]
        kd = k_ref[pl.ds(d, 1), :]
        qd = q_ref[pl.ds(d, 1), :]
        s_new = a * s_ref[0, d] + kd * v_t
        sn_ref[0, d] = s_new
        return acc + (qd * (dk ** -0.5)) * s_new

    o = lax.fori_loop(0, dk, body, jnp.zeros(v_t.shape, F32))
    o = o * lax.rsqrt(jnp.mean(o * o, axis=0, keepdims=True) + EPS) * gh_ref[...]
    o_ref[...] = o * _silu(og_ref[...])
    u = cc_ref[...] * ch_ref[...]
    y = wc_ref[0] * cbuf_ref[0] + wc_ref[1] * cbuf_ref[1] + wc_ref[2] * u
    y_ref[...] = cb_ref[...] * y
    cn_ref[0] = cbuf_ref[1]
    cn_ref[1] = u


def _gla_conv_step(proj_t, wgu_t_pad, bgu_col, gh_col, wc_col, s_t, cbuf_t, *, heads, dk, dv, cw):
    bsz = proj_t.shape[1]
    hk, hv = heads * dk, heads * dv
    cs = cw // heads
    assert dv == LANES and cs == LANES and dk * 2 == LANES
    o_k, o_v, o_og = hk, 2 * hk, 2 * hk + hv
    o_ch = o_og + hv
    o_cb, o_cc, o_lr = o_ch + cw, o_ch + 2 * cw, o_ch + 3 * cw
    row = lambda off, size: (lambda h: (off // size + h, 0))
    blk = lambda size, off: pl.BlockSpec((size, bsz), row(off, size))
    return pl.pallas_call(
        functools.partial(_gla_conv_step_kernel, dk=dk),
        grid=(heads,),
        in_specs=[blk(dk, 0), blk(dk, o_k), blk(dv, o_v), blk(dv, o_og),
                  blk(cs, o_ch), blk(cs, o_cb), blk(cs, o_cc),
                  pl.BlockSpec((LANES, bsz), lambda h: (o_lr // LANES, 0)),
                  pl.BlockSpec((dk, LANES), lambda h: (h, 0)),
                  pl.BlockSpec((dk, 1), lambda h: (h, 0)),
                  pl.BlockSpec((dv, 1), lambda h: (0, 0)),
                  pl.BlockSpec((3, cs, 1), lambda h: (0, h, 0)),
                  pl.BlockSpec((1, dk, dv, bsz), lambda h: (h, 0, 0, 0)),
                  pl.BlockSpec((2, cs, bsz), lambda h: (0, h, 0))],
        out_specs=[pl.BlockSpec((dv, bsz), lambda h: (h, 0)),
                   pl.BlockSpec((cs, bsz), lambda h: (h, 0)),
                   pl.BlockSpec((1, dk, dv, bsz), lambda h: (h, 0, 0, 0)),
                   pl.BlockSpec((2, cs, bsz), lambda h: (0, h, 0))],
        out_shape=[jax.ShapeDtypeStruct((hv, bsz), F32),
                   jax.ShapeDtypeStruct((cw, bsz), F32),
                   jax.ShapeDtypeStruct(s_t.shape, F32),
                   jax.ShapeDtypeStruct(cbuf_t.shape, F32)],
        scratch_shapes=[pltpu.VMEM((dk, bsz), F32)],
        compiler_params=_params(("parallel",)),
    )(proj_t, proj_t, proj_t, proj_t, proj_t, proj_t, proj_t, proj_t,
      wgu_t_pad, bgu_col, gh_col, wc_col, s_t, cbuf_t)


def _s5_scan_kernel(*refs, tt, n_t, nb, half, n_cast):
    u_ref, bb_ref, cm_ref, abr_ref, abi_ref, d_ref = refs[:6]
    y_ref, sre_ref, sim_ref = refs[6 + n_cast:9 + n_cast]
    scr, tb_scr, xr_scr, xi_scr = refs[9 + 2 * n_cast:]
    _cast_passengers(refs[6:6 + n_cast], refs[9 + n_cast:9 + 2 * n_cast])
    ti = pl.program_id(1)
    n_blk = half // LANES

    @pl.when(ti == 0)
    def _():
        xr_scr[...] = jnp.zeros_like(xr_scr)
        xi_scr[...] = jnp.zeros_like(xi_scr)

    for b in range(nb):
        tb_scr[pl.ds(b, tt, stride=nb), :] = u_ref[b].astype(F32)
    for k in range(nb):
        rows = slice(k * tt, (k + 1) * tt)
        bu = _dot(tb_scr[rows, :].astype(BF16), bb_ref[0])
        for cb in range(2 * n_blk):
            scr[cb, rows, :] = bu[:, cb * LANES:(cb + 1) * LANES]
    a_re = [jnp.broadcast_to(abr_ref[0][:, cb * LANES:(cb + 1) * LANES], (nb, LANES))
            for cb in range(n_blk)]
    a_im = [jnp.broadcast_to(abi_ref[0][:, cb * LANES:(cb + 1) * LANES], (nb, LANES))
            for cb in range(n_blk)]

    def body(t, carry):
        rows = pl.ds(pl.multiple_of(t * nb, nb), nb)
        out = []
        for cb in range(n_blk):
            xr, xi = carry[2 * cb], carry[2 * cb + 1]
            nr = (a_re[cb] * xr - a_im[cb] * xi) + scr[cb, rows, :]
            ni = (a_re[cb] * xi + a_im[cb] * xr) + scr[n_blk + cb, rows, :]
            scr[cb, rows, :] = nr
            scr[n_blk + cb, rows, :] = ni
            out += [nr, ni]
        return tuple(out)

    init = []
    for cb in range(n_blk):
        init += [xr_scr[:, cb * LANES:(cb + 1) * LANES], xi_scr[:, cb * LANES:(cb + 1) * LANES]]
    fin = lax.fori_loop(0, tt, body, tuple(init), unroll=8)
    xr = jnp.concatenate([fin[2 * cb] for cb in range(n_blk)], axis=1)
    xi = jnp.concatenate([fin[2 * cb + 1] for cb in range(n_blk)], axis=1)
    xr_scr[...] = xr
    xi_scr[...] = xi
    for k in range(nb):
        rows = slice(k * tt, (k + 1) * tt)
        xs = jnp.concatenate([scr[cb, rows, :] for cb in range(2 * n_blk)], axis=1)
        y = _dot(xs.astype(BF16), cm_ref[0]) + d_ref[0] * tb_scr[rows, :]
        tb_scr[rows, :] = _gelu_tanh(y)
    for b in range(nb):
        y_ref[b] = tb_scr[pl.ds(b, tt, stride=nb), :].astype(y_ref.dtype)

    @pl.when(ti == n_t - 1)
    def _():
        sre_ref[...] = xr
        sim_ref[...] = xi


def _s5_prompt(u3d, bb_blk, c_blk, ab_re, ab_im, d_blk, cast_ws, *, tt):
    nb, t, d = u3d.shape
    n_j = d // LANES
    n_t = t // tt
    half = bb_blk.shape[2] // 2
    w_specs = _passenger_specs(cast_ws, n_j * n_t, lambda j, i: j * n_t + i)
    return pl.pallas_call(
        functools.partial(_s5_scan_kernel, tt=tt, n_t=n_t, nb=nb, half=half, n_cast=len(cast_ws)),
        grid=(n_j, n_t),
        in_specs=[pl.BlockSpec((nb, tt, LANES), lambda j, i: (0, i, j)),
                  pl.BlockSpec((1, LANES, 2 * half), lambda j, i: (j, 0, 0)),
                  pl.BlockSpec((1, 2 * half, LANES), lambda j, i: (j, 0, 0)),
                  pl.BlockSpec((1, 1, half), lambda j, i: (j, 0, 0)),
                  pl.BlockSpec((1, 1, half), lambda j, i: (j, 0, 0)),
                  pl.BlockSpec((1, 1, LANES), lambda j, i: (j, 0, 0))] + w_specs,
        out_specs=[pl.BlockSpec((nb, tt, LANES), lambda j, i: (0, i, j)),
                   pl.BlockSpec((nb, half), lambda j, i: (0, j)),
                   pl.BlockSpec((nb, half), lambda j, i: (0, j))] + w_specs,
        out_shape=[jax.ShapeDtypeStruct((nb, t, d), BF16),
                   jax.ShapeDtypeStruct((nb, n_j * half), F32),
                   jax.ShapeDtypeStruct((nb, n_j * half), F32)]
        + [jax.ShapeDtypeStruct(w.shape, BF16) for w in cast_ws],
        scratch_shapes=[pltpu.VMEM((2 * half // LANES, nb * tt, LANES), F32),
                        pltpu.VMEM((nb * tt, LANES), F32),
                        pltpu.VMEM((nb, half), F32), pltpu.VMEM((nb, half), F32)],
        compiler_params=_params(("arbitrary", "arbitrary"), VMEM_LIMIT_MB),
    )(u3d, bb_blk, c_blk, ab_re, ab_im, d_blk, *cast_ws)


def _s5_step_kernel(u_ref, bb_ref, cm_ref, abr_ref, abi_ref, d_ref, sre_ref, sim_ref,
                    y_ref, nre_ref, nim_ref, *, half):
    u = u_ref[...]
    bu = _mm(u, bb_ref[0], "f32")
    ar, ai = abr_ref[0], abi_ref[0]
    sr, si = sre_ref[...], sim_ref[...]
    nr = (ar * sr - ai * si) + bu[:, 0:half]
    ni = (ar * si + ai * sr) + bu[:, half:2 * half]
    nre_ref[...] = nr
    nim_ref[...] = ni
    y = _mm(jnp.concatenate([nr, ni], axis=1), cm_ref[0], "f32") + d_ref[0] * u
    y_ref[...] = _gelu_tanh(y)


def _s5_step(u2d, bb_blk, c_blk, ab_re, ab_im, d_blk, s_re, s_im):
    bsz, d = u2d.shape
    n_j = d // LANES
    half = bb_blk.shape[2] // 2
    st = pl.BlockSpec((bsz, half), lambda j: (0, j))
    return pl.pallas_call(
        functools.partial(_s5_step_kernel, half=half),
        grid=(n_j,),
        in_specs=[pl.BlockSpec((bsz, LANES), lambda j: (0, j)),
                  pl.BlockSpec((1, LANES, 2 * half), lambda j: (j, 0, 0)),
                  pl.BlockSpec((1, 2 * half, LANES), lambda j: (j, 0, 0)),
                  pl.BlockSpec((1, 1, half), lambda j: (j, 0, 0)),
                  pl.BlockSpec((1, 1, half), lambda j: (j, 0, 0)),
                  pl.BlockSpec((1, 1, LANES), lambda j: (j, 0, 0)),
                  st, st],
        out_specs=[pl.BlockSpec((bsz, LANES), lambda j: (0, j)), st, st],
        out_shape=[jax.ShapeDtypeStruct((bsz, d), F32),
                   jax.ShapeDtypeStruct(s_re.shape, F32),
                   jax.ShapeDtypeStruct(s_im.shape, F32)],
        compiler_params=_params(("parallel",)),
    )(u2d, bb_blk, c_blk, ab_re, ab_im, d_blk, s_re, s_im)


def _s5_zoh(a_re, a_im, log_dt, b_re, b_im):
    dt = jnp.exp(log_dt)[:, None]
    mag = jnp.exp(dt * a_re)
    ab_re, ab_im = mag * jnp.cos(dt * a_im), mag * jnp.sin(dt * a_im)
    den = a_re * a_re + a_im * a_im
    nr, ni = ab_re - 1.0, ab_im
    f_re = (nr * a_re + ni * a_im) / den
    f_im = (ni * a_re - nr * a_im) / den
    bb_re = f_re[..., None] * b_re - f_im[..., None] * b_im
    bb_im = f_re[..., None] * b_im + f_im[..., None] * b_re
    return ab_re, ab_im, bb_re, bb_im


def _s5_step_blocks(ab_re, ab_im, bb_re, bb_im, c_re, c_im, d_skip):
    g, p = ab_re.shape
    cg = S5_CHUNK_GROUPS
    n_j = g // cg
    eye = jnp.eye(cg, dtype=F32)

    def in_blk(m):
        m = m.reshape(n_j, cg, p, S5_GROUP)
        return jnp.einsum("jgpi,gh->jgihp", m, eye).reshape(n_j, cg * S5_GROUP, cg * p)

    def out_blk(m):
        m = m.reshape(n_j, cg, S5_GROUP, p)
        return jnp.einsum("jgip,gh->jgphi", m, eye).reshape(n_j, cg * p, cg * S5_GROUP)

    bb_blk = jnp.concatenate([in_blk(bb_re), in_blk(bb_im)], axis=2)
    c_blk = jnp.concatenate([out_blk(c_re), out_blk(-c_im)], axis=1)
    return (bb_blk, c_blk, ab_re.reshape(n_j, 1, cg * p), ab_im.reshape(n_j, 1, cg * p),
            d_skip.reshape(n_j, 1, cg * S5_GROUP))


def kernel(x_prompt, x_sample, c_prompt, c_sample, state_gla, state_conv, state_s5_re, state_s5_im, w_ada, b_ada, g_norm, w_in0, w_gate_up, b_gate_up, g_head_norm, w_conv, w_out0, w_ffn_gate, w_ffn_up, w_ffn_down, w_in1, s5_a_re, s5_a_im, s5_log_dt, s5_b_re, s5_b_im, s5_c_re, s5_c_im, s5_d, w_glu, w_router, w_exp_gate, w_exp_up, w_exp_down):
    bsz, t, d = x_prompt.shape
    bs = x_sample.shape[0]
    heads, dk, dv = state_gla.shape[2:]
    cw = state_conv.shape[3]
    lowrank = w_gate_up.shape[1]
    hk, hv = heads * dk, heads * dv
    n_experts = w_router.shape[2]
    g_s5, p_s5 = s5_a_re.shape[1:]

    mod = _ada(jnp.concatenate([c_prompt, c_sample], axis=0), w_ada, b_ada)
    mod_p = [mod[l, :bsz].reshape(bsz, 1, N_MOD * d) for l in range(2)]
    mod_s = [mod[l, bsz:] for l in range(2)]
    SH_M, SC_M, GT_M, SH_F, SC_F, GT_F = range(N_MOD)
    gn = lambda l, k: g_norm[l, k].reshape(1, d)

    w0 = w_in0[0]
    o_lr = 2 * hk + 2 * hv
    w0r = jnp.concatenate([w0[:, :o_lr], w0[:, o_lr + lowrank:], w0[:, o_lr:o_lr + lowrank],
                           jnp.zeros((d, LANES - lowrank), F32)], axis=1)
    wgu_pad = jnp.concatenate([w_gate_up[0], jnp.zeros((LANES - lowrank, hk), F32)], axis=0)
    zoh = _s5_zoh(s5_a_re[0], s5_a_im[0], s5_log_dt[0], s5_b_re[0], s5_b_im[0])
    bb_blk, c_blk, ab_re, ab_im, d_blk = _s5_step_blocks(*zoh, s5_c_re[0], s5_c_im[0], s5_d[0])
    wr_pad = jnp.concatenate([w_router[0], jnp.zeros((d, LANES - n_experts), F32)], axis=1)
    bf = lambda a: a.astype(BF16)

    tm = TOKEN_TILE
    tpb = t // tm
    pk = dict(per_token=False, tiles_per_batch=tpb)
    xp = x_prompt.reshape(bsz * t, d)
    proj, w_o0, w_fg, w_fu, w_fd, w_i1, w_gl = _inproj(
        xp, gn(0, 0), mod_p[0], SC_M, SH_M, bf(w0r), tm=tm, bn=w0r.shape[1], prec="bf16",
        vmem_mb=VMEM_LIMIT_MB, cast_ws=(w_out0[0], w_ffn_gate[0], w_ffn_up[0], w_ffn_down[0], w_in1[0], w_glu[0]), **pk)
    mixed, st_t, cst = _gla_conv_prompt(
        proj, bf(wgu_pad), b_gate_up[0].reshape(1, hk), g_head_norm[0].reshape(1, dv), w_conv[0],
        bsz=bsz, t=t, tt=SEQ_TILE, heads=heads, dk=dk, dv=dv, cw=cw)
    st_t = st_t.reshape(bsz, heads // 2, 2, dv, 2, dk)
    gla_p = jnp.stack([st_t[:, :, 0, :, 0, :], st_t[:, :, 1, :, 1, :]], axis=2)
    gla_p = jnp.swapaxes(gla_p.reshape(bsz, heads, dv, dk), 2, 3)
    x2, u = _mix_ffn_proj(mixed, w_o0, xp, mod_p[0], mod_p[1], g_norm, w_fg, w_fu, w_fd, w_i1, tm=tm,
                          tiles_per_batch=tpb, vmem_mb=VMEM_LIMIT_FFN_MB)
    ew = (w_exp_gate[0], w_exp_up[0], w_exp_down[0])
    yg, re_p, im_p, *ew_bf = _s5_prompt(
        u.reshape(bsz, t, d), bf(bb_blk), bf(c_blk), ab_re, ab_im, d_blk,
        tuple(w.reshape(-1, w.shape[2]) for w in ew), tt=SEQ_TILE)
    w_eg, w_eu, w_ed = (wb.reshape(w.shape) for wb, w in zip(ew_bf, ew))
    x3, comb, hb = _glu_route(yg.reshape(bsz * t, d), w_gl, x2, mod_p[1], g_norm, wr_pad, tm=tm,
                              n_experts=n_experts, tiles_per_batch=tpb, vmem_mb=VMEM_LIMIT_MB)

    sk = dict(per_token=True, tiles_per_batch=1)
    xs = x_sample.reshape(bs, d)
    proj_s = _inproj(xs, gn(0, 0), mod_s[0], SC_M, SH_M, w0r, tm=bs, bn=SAMPLE_PROJ_COLS, prec="f32", **sk)
    s_t = jnp.transpose(state_gla[:, 0], (1, 2, 3, 0))
    cbuf_t = jnp.transpose(state_conv[:, 0], (1, 2, 0))
    o_t, y_t, sn_t, cn_t = _gla_conv_step(
        proj_s.T, wgu_pad.T, b_gate_up[0].reshape(hk, 1), g_head_norm[0].reshape(dv, 1),
        w_conv[0].reshape(3, cw, 1), s_t, cbuf_t, heads=heads, dk=dk, dv=dv, cw=cw)
    mixed_s = jnp.concatenate([o_t, y_t], axis=0).T
    gla_s = jnp.transpose(sn_t, (3, 0, 1, 2))
    conv_s = jnp.transpose(cn_t, (2, 0, 1))
    x1s = _outproj(mixed_s, w_out0[0], xs, mod_s[0], GT_M, gn(0, 1), tm=bs, prec="f32", glu=False,
                   **sk)
    x2s = _ffn(x1s, gn(0, 2), mod_s[0], SC_F, SH_F, GT_F, gn(0, 3), w_ffn_gate[0], w_ffn_up[0],
               w_ffn_down[0], tm=bs, fc=SAMPLE_COLS, prec="f32", **sk)
    us = _inproj(x2s, gn(1, 0), mod_s[1], SC_M, SH_M, w_in1[0], tm=bs, bn=SAMPLE_COLS, prec="f32", **sk)
    ygs, re_s, im_s = _s5_step(us, bb_blk, c_blk, ab_re, ab_im, d_blk,
                               state_s5_re[:, 0].reshape(bs, g_s5 * p_s5),
                               state_s5_im[:, 0].reshape(bs, g_s5 * p_s5))
    x3s = _outproj(ygs, w_glu[0], x2s, mod_s[1], GT_M, gn(1, 1), tm=bs, prec="f32", glu=True,
                   vmem_mb=VMEM_LIMIT_MB, **sk)
    comb_s, hb_s = _router(x3s, gn(1, 2), mod_s[1], SC_F, SH_F, wr_pad, tm=bs, n_experts=n_experts,
                           **sk)

    assert bs <= MOE_TB
    n_blk = bsz * t // MOE_TB + 1
    n_tiles_max = -(-(n_blk * _moe_block_rows(n_experts)) // MOE_TM) + n_experts
    pad_rows = lambda a: jnp.concatenate([a, jnp.zeros((MOE_TB - bs, a.shape[1]), a.dtype)], axis=0)
    hb_s, comb_s = pad_rows(hb_s), pad_rows(comb_s)
    seg, gap_pos, gap_len, tile_expert, n_tiles = _moe_schedule((comb, comb_s), n_experts,
                                                                n_tiles_max)
    dest, xs_sorted = _moe_sort(seg, gap_pos, gap_len, hb, comb, hb_s, comb_s,
                                n_tiles_max * MOE_TM, n_experts=n_experts)
    ys_sorted = _experts(tile_expert, n_tiles, xs_sorted, w_eg, w_eu, w_ed, fc=EXPERT_FF_CHUNK)
    x4 = _moe_combine(*seg, ys_sorted, dest, x3, mod_p[1], GT_F, gn(1, 3), blk0=0,
                      n_experts=n_experts, per_token=False, tiles_per_batch=t // MOE_TB)
    x4s = _moe_combine(*seg, ys_sorted, dest, pad_rows(x3s), pad_rows(mod_s[1]), GT_F,
                       gn(1, 3), blk0=n_blk - 1, n_experts=n_experts, per_token=True,
                       tiles_per_batch=1)[:bs]

    return (x4.reshape(bsz, t, d), x4s.reshape(bs, 1, d),
            gla_p[:, None], cst[:, None],
            re_p.reshape(bsz, 1, g_s5, p_s5), im_p.reshape(bsz, 1, g_s5, p_s5),
            gla_s[:, None], conv_s[:, None],
            re_s.reshape(bs, 1, g_s5, p_s5), im_s.reshape(bs, 1, g_s5, p_s5))
```

```python
import functools
import math

import jax
import jax.numpy as jnp
from jax import lax
from jax.experimental import pallas as pl
from jax.experimental.pallas import tpu as pltpu

F32 = jnp.float32
BF16 = jnp.bfloat16
EPS = 1e-6
LANES = 128
GLA_CHUNK = 64
GATE_NORMALIZER = 16.0
N_MOD = 6
S5_GROUP = 16
S5_CHUNK_GROUPS = 8

TOKEN_TILE = 512
SEQ_TILE = 256
VMEM_LIMIT_MB = 48
VMEM_LIMIT_FFN_MB = 56
EXPERT_FF_CHUNK = 1792
SAMPLE_PROJ_COLS = 640
SAMPLE_COLS = 256


def _dot(a, b, dims=None):
    if dims is None:
        return jnp.dot(a, b, preferred_element_type=F32)
    return lax.dot_general(a, b, (dims, ((), ())), preferred_element_type=F32)


def _split(a, parts=2):
    rem = a.astype(F32)
    out = []
    for _ in range(parts - 1):
        piece = rem.astype(BF16)
        out.append(piece)
        rem = rem - piece.astype(F32)
    out.append(rem.astype(BF16))
    return out


def _mm(a, w, prec):
    if prec == "bf16":
        return _dot(a.astype(BF16), w.astype(BF16))
    a1, a2, a3 = _split(a, 3)
    w1, w2, w3 = _split(w, 3)
    small = (_dot(a1, w3) + _dot(a3, w1)) + _dot(a2, w2)
    return _dot(a1, w1) + ((_dot(a1, w2) + _dot(a2, w1)) + small)


def _silu(x):
    return x * jax.nn.sigmoid(x)


def _gelu_tanh(x):
    return 0.5 * x * (1.0 + jnp.tanh(math.sqrt(2.0 / math.pi) * (x + 0.044715 * (x * x * x))))


def _log_sigmoid(x):
    return -(jnp.maximum(-x, 0.0) + jnp.log1p(jnp.exp(-jnp.abs(x))))


def _rms(x):
    return x * lax.rsqrt(jnp.mean(x * x, axis=-1, keepdims=True) + EPS)


def _mod_row(ref):
    return ref[0] if len(ref.shape) == 3 else ref[...]


def _normmod(x, g, sc, sh):
    return (_rms(x) * g) * (1.0 + sc) + sh


def _mod_spec(per_token, tm, d, col, tiles_per_batch):
    if per_token:
        return pl.BlockSpec((tm, d), lambda i, *_: (i, col))
    return pl.BlockSpec((1, 1, d), lambda i, *_: (i // tiles_per_batch, 0, col))


def _params(sem, vmem_mb=None):
    kw = dict(dimension_semantics=sem)
    if vmem_mb is not None:
        kw["vmem_limit_bytes"] = vmem_mb << 20
    return pltpu.CompilerParams(**kw)


def _ada_kernel(c_ref, w_ref, b_ref, p_ref, o_ref, po_ref):
    _cast_passengers([p_ref], [po_ref])
    o_ref[0] = _mm(_silu(c_ref[...]), w_ref[0], "f32") + b_ref[0]


def _ada(c_all, w_ada, b_ada, cast_w):
    depth, d, n6 = w_ada.shape
    rows = c_all.shape[0]
    n_j = n6 // d
    p_spec, = _passenger_specs((cast_w,), depth * n_j, lambda l, j: l * n_j + j)
    return pl.pallas_call(
        _ada_kernel,
        grid=(depth, n_j),
        in_specs=[pl.BlockSpec((rows, d), lambda l, j: (0, 0)),
                  pl.BlockSpec((1, d, d), lambda l, j: (l, 0, j)),
                  pl.BlockSpec((1, 1, d), lambda l, j: (l, 0, j)), p_spec],
        out_specs=[pl.BlockSpec((1, rows, d), lambda l, j: (l, 0, j)), p_spec],
        out_shape=[jax.ShapeDtypeStruct((depth, rows, n6), F32),
                   jax.ShapeDtypeStruct(cast_w.shape, BF16)],
        compiler_params=_params(("arbitrary", "arbitrary")),
    )(c_all, w_ada, b_ada.reshape(depth, 1, n6), cast_w)


BF16_ROWS = 16


def _passenger_specs(ws, n_steps, step):
    specs = []
    for w in ws:
        rows, cols = w.shape
        units = rows // BF16_ROWS
        n_used = max(k for k in range(1, n_steps + 1) if units % k == 0)
        specs.append(pl.BlockSpec(
            (rows // n_used, cols),
            lambda *idx, n_used=n_used: (jnp.minimum(step(*idx), n_used - 1), 0)))
    return specs


def _cast_passengers(in_refs, out_refs):
    for src, dst in zip(in_refs, out_refs):
        dst[...] = src[...].astype(dst.dtype)


def _inproj_kernel(*refs, prec, n_cast):
    x_ref, g_ref, sc_ref, sh_ref, w_ref = refs[:5]
    o_ref = refs[5 + n_cast]
    _cast_passengers(refs[5:5 + n_cast], refs[6 + n_cast:])
    h = _normmod(x_ref[...], g_ref[...], _mod_row(sc_ref), _mod_row(sh_ref))
    o_ref[...] = _mm(h, w_ref[...], prec).astype(o_ref.dtype)


def _inproj(x2d, g, mod, sc_col, sh_col, w, *, tm, bn, prec, per_token, tiles_per_batch,
            out_dtype=F32, vmem_mb=None, cast_ws=()):
    n_tok, d = x2d.shape
    n_out = w.shape[1]
    n_j = n_out // bn
    p_specs = _passenger_specs(cast_ws, n_tok // tm * n_j, lambda i, j: i * n_j + j)
    out = pl.pallas_call(
        functools.partial(_inproj_kernel, prec=prec, n_cast=len(cast_ws)),
        grid=(n_tok // tm, n_j),
        in_specs=[pl.BlockSpec((tm, d), lambda i, j: (i, 0)),
                  pl.BlockSpec((1, d), lambda i, j: (0, 0)),
                  _mod_spec(per_token, tm, d, sc_col, tiles_per_batch),
                  _mod_spec(per_token, tm, d, sh_col, tiles_per_batch),
                  pl.BlockSpec((d, bn), lambda i, j: (0, j))] + p_specs,
        out_specs=[pl.BlockSpec((tm, bn), lambda i, j: (i, j))] + p_specs,
        out_shape=[jax.ShapeDtypeStruct((n_tok, n_out), out_dtype)]
        + [jax.ShapeDtypeStruct(c.shape, BF16) for c in cast_ws],
        compiler_params=_params(("arbitrary", "arbitrary"), vmem_mb),
    )(x2d, g, mod, mod, w, *cast_ws)
    return out if cast_ws else out[0]


def _outproj_kernel(a_ref, w_ref, x_ref, gt_ref, g_ref, o_ref, *, prec, glu):
    z = _mm(a_ref[...], w_ref[...], prec)
    if glu:
        d = z.shape[1] // 2
        z = z[:, :d] * jax.nn.sigmoid(z[:, d:])
    o_ref[...] = x_ref[...] + _mod_row(gt_ref) * (_rms(z) * g_ref[...])


def _outproj(a2d, w, x2d, mod, gt_col, g, *, tm, prec, glu, per_token, tiles_per_batch,
             vmem_mb=None):
    n_tok, d = x2d.shape
    k, n_out = w.shape
    return pl.pallas_call(
        functools.partial(_outproj_kernel, prec=prec, glu=glu),
        grid=(n_tok // tm,),
        in_specs=[pl.BlockSpec((tm, k), lambda i: (i, 0)),
                  pl.BlockSpec((k, n_out), lambda i: (0, 0)),
                  pl.BlockSpec((tm, d), lambda i: (i, 0)),
                  _mod_spec(per_token, tm, d, gt_col, tiles_per_batch),
                  pl.BlockSpec((1, d), lambda i: (0, 0))],
        out_specs=pl.BlockSpec((tm, d), lambda i: (i, 0)),
        out_shape=jax.ShapeDtypeStruct((n_tok, d), F32),
        compiler_params=_params(("parallel",), vmem_mb),
    )(a2d, w, x2d, mod, g)


def _mix_ffn_proj_kernel(a_ref, wo_ref, x_ref, gtm_ref, g01_ref, g02_ref, scf_ref, shf_ref, wg_ref,
                         wu_ref, wd_ref, gtf_ref, g03_ref, g10_ref, sc1_ref, sh1_ref, wi_ref,
                         x2_ref, u_ref):
    x1 = x_ref[...] + _mod_row(gtm_ref) * (_rms(_mm(a_ref[...], wo_ref[...], "bf16")) * g01_ref[...])
    h = _normmod(x1, g02_ref[...], _mod_row(scf_ref), _mod_row(shf_ref))
    act = _silu(_mm(h, wg_ref[...], "bf16")) * _mm(h, wu_ref[...], "bf16")
    x2 = x1 + _mod_row(gtf_ref) * (_rms(_mm(act, wd_ref[...], "bf16")) * g03_ref[...])
    x2_ref[...] = x2
    h1 = _normmod(x2, g10_ref[...], _mod_row(sc1_ref), _mod_row(sh1_ref))
    u_ref[...] = _mm(h1, wi_ref[...], "bf16").astype(u_ref.dtype)


def _mix_ffn_proj(mixed, w_out, x2d, mod0, mod1, g_norm, w_gate, w_up, w_down, w_in1, *, tm,
                  tiles_per_batch, vmem_mb):
    n_tok, d = x2d.shape
    sh_m, sc_m, gt_m, sh_f, sc_f, gt_f = range(N_MOD)
    tok = lambda w: pl.BlockSpec((tm, w), lambda i: (i, 0))
    vec = pl.BlockSpec((1, d), lambda i: (0, 0))
    mod = lambda col: _mod_spec(False, tm, d, col, tiles_per_batch)
    once = lambda w: pl.BlockSpec(w.shape, lambda i: (0, 0), pipeline_mode=pl.Buffered(1))
    g = lambda l, k: g_norm[l, k].reshape(1, d)
    return pl.pallas_call(
        _mix_ffn_proj_kernel,
        grid=(n_tok // tm,),
        in_specs=[tok(mixed.shape[1]), once(w_out), tok(d), mod(gt_m), vec, vec, mod(sc_f), mod(sh_f),
                  once(w_gate), once(w_up), once(w_down), mod(gt_f), vec, vec, mod(sc_m), mod(sh_m),
                  once(w_in1)],
        out_specs=[tok(d), tok(w_in1.shape[1])],
        out_shape=[jax.ShapeDtypeStruct((n_tok, d), F32),
                   jax.ShapeDtypeStruct((n_tok, w_in1.shape[1]), BF16)],
        compiler_params=_params(("parallel",), vmem_mb),
    )(mixed, w_out, x2d, mod0, g(0, 1), g(0, 2), mod0, mod0, w_gate, w_up, w_down, mod0, g(0, 3),
      g(1, 0), mod1, mod1, w_in1)


def _glu_route_kernel(a_ref, w_ref, x_ref, gtm_ref, g1_ref, g2_ref, scf_ref, shf_ref, wr_ref,
                      x3_ref, comb_ref, hb_ref, wp_scr, *, n_experts):
    @pl.when(pl.program_id(0) == 0)
    def _():
        _pack_router_weight(wr_ref, wp_scr, n_experts)

    z = _mm(a_ref[...], w_ref[...], "bf16")
    d = z.shape[1] // 2
    x3 = x_ref[...] + _mod_row(gtm_ref) * (_rms(z[:, :d] * jax.nn.sigmoid(z[:, d:])) * g1_ref[...])
    x3_ref[...] = x3
    h = _normmod(x3, g2_ref[...], _mod_row(scf_ref), _mod_row(shf_ref))
    hb_ref[...] = h.astype(hb_ref.dtype)
    comb_ref[...] = _route(h, wp_scr[...], n_experts)


def _glu_route(a2d, w_glu, x2d, mod1, g_norm, wr_pad, *, tm, n_experts, tiles_per_batch, vmem_mb):
    n_tok, d = x2d.shape
    sh_m, sc_m, gt_m, sh_f, sc_f, gt_f = range(N_MOD)
    tok = lambda w: pl.BlockSpec((tm, w), lambda i: (i, 0))
    vec = pl.BlockSpec((1, d), lambda i: (0, 0))
    mod = lambda col: _mod_spec(False, tm, d, col, tiles_per_batch)
    g = lambda k: g_norm[1, k].reshape(1, d)
    return pl.pallas_call(
        functools.partial(_glu_route_kernel, n_experts=n_experts),
        grid=(n_tok // tm,),
        in_specs=[tok(a2d.shape[1]), pl.BlockSpec(w_glu.shape, lambda i: (0, 0)), tok(d), mod(gt_m),
                  vec, vec, mod(sc_f), mod(sh_f), pl.BlockSpec(wr_pad.shape, lambda i: (0, 0))],
        out_specs=[tok(d), tok(LANES), tok(d)],
        out_shape=[jax.ShapeDtypeStruct((n_tok, d), F32),
                   jax.ShapeDtypeStruct((n_tok, LANES), F32),
                   jax.ShapeDtypeStruct((n_tok, d), BF16)],
        scratch_shapes=[pltpu.VMEM(wr_pad.shape, BF16)],
        compiler_params=_params(("arbitrary",), vmem_mb),
    )(a2d, w_glu, x2d, mod1, g(1), g(2), mod1, mod1, wr_pad)


def _ffn_kernel(x_ref, g1_ref, sc_ref, sh_ref, wg_ref, wu_ref, wd_ref, gt_ref, g2_ref,
                o_ref, h_scr, acc_scr, *, prec, n_c):
    c = pl.program_id(1)

    @pl.when(c == 0)
    def _():
        h = _normmod(x_ref[...], g1_ref[...], _mod_row(sc_ref), _mod_row(sh_ref))
        h_scr[...] = h.astype(h_scr.dtype)
        acc_scr[...] = jnp.zeros_like(acc_scr)

    h = h_scr[...]
    act = _silu(_mm(h, wg_ref[...], prec)) * _mm(h, wu_ref[...], prec)
    acc_scr[...] += _mm(act, wd_ref[...], prec)

    @pl.when(c == n_c - 1)
    def _():
        o_ref[...] = x_ref[...] + _mod_row(gt_ref) * (_rms(acc_scr[...]) * g2_ref[...])


def _ffn(x2d, g1, mod, sc_col, sh_col, gt_col, g2, wg, wu, wd, *, tm, fc, prec,
         per_token, tiles_per_batch, vmem_mb=None):
    n_tok, d = x2d.shape
    n_c = wg.shape[1] // fc
    tok = pl.BlockSpec((tm, d), lambda i, c: (i, 0))
    vec = pl.BlockSpec((1, d), lambda i, c: (0, 0))
    h_dtype = BF16 if prec == "bf16" else F32
    w_mode = dict(pipeline_mode=pl.Buffered(1)) if n_c == 1 else {}
    return pl.pallas_call(
        functools.partial(_ffn_kernel, prec=prec, n_c=n_c),
        grid=(n_tok // tm, n_c),
        in_specs=[tok, vec,
                  _mod_spec(per_token, tm, d, sc_col, tiles_per_batch),
                  _mod_spec(per_token, tm, d, sh_col, tiles_per_batch),
                  pl.BlockSpec((d, fc), lambda i, c: (0, c), **w_mode),
                  pl.BlockSpec((d, fc), lambda i, c: (0, c), **w_mode),
                  pl.BlockSpec((fc, d), lambda i, c: (c, 0), **w_mode),
                  _mod_spec(per_token, tm, d, gt_col, tiles_per_batch), vec],
        out_specs=tok,
        out_shape=jax.ShapeDtypeStruct((n_tok, d), F32),
        scratch_shapes=[pltpu.VMEM((tm, d), h_dtype), pltpu.VMEM((tm, d), F32)],
        compiler_params=_params(("parallel", "arbitrary"), vmem_mb),
    )(x2d, g1, mod, mod, wg, wu, wd, mod, g2)


def _pack_router_weight(wr_ref, wp_scr, n_experts):
    w1, w2, w3 = _split(wr_ref[...], 3)
    fwd = lambda x, k: jnp.concatenate([x[:, LANES - k * n_experts:], x[:, :LANES - k * n_experts]], axis=1)
    packed = w1.astype(F32) + (fwd(w2.astype(F32), 1) + fwd(w3.astype(F32), 2))
    wp_scr[...] = packed.astype(wp_scr.dtype)


def _route(h, w, n_experts):
    h1, h2, h3 = _split(h, 3)
    p1, p2, p3 = _dot(h1, w), _dot(h2, w), _dot(h3, w)
    back = lambda x, k: jnp.concatenate([x[:, k * n_experts:], x[:, :k * n_experts]], axis=1)
    small = (back(p1, 2) + p3) + back(p2, 1)
    logits = p1 + ((back(p1, 1) + p2) + small)
    lane = lax.broadcasted_iota(jnp.int32, logits.shape, 1).astype(F32)
    neg = -jnp.inf
    l1 = jnp.where(lane < n_experts, logits, neg)
    m1 = jnp.max(l1, axis=-1, keepdims=True)
    i1 = jnp.min(jnp.where(l1 == m1, lane, float(LANES)), axis=-1, keepdims=True)
    l2 = jnp.where(lane == i1, neg, l1)
    m2 = jnp.max(l2, axis=-1, keepdims=True)
    i2 = jnp.min(jnp.where(l2 == m2, lane, float(LANES)), axis=-1, keepdims=True)
    e2 = jnp.exp(m2 - m1)
    den = 1.0 + e2
    return jnp.where(lane == i1, 1.0 / den, 0.0) + jnp.where(lane == i2, e2 / den, 0.0)


def _router_kernel(x_ref, g_ref, sc_ref, sh_ref, wr_ref, comb_ref, hb_ref, wp_scr, *, n_experts):
    @pl.when(pl.program_id(0) == 0)
    def _():
        _pack_router_weight(wr_ref, wp_scr, n_experts)

    h = _normmod(x_ref[...], g_ref[...], _mod_row(sc_ref), _mod_row(sh_ref))
    hb_ref[...] = h.astype(hb_ref.dtype)
    comb_ref[...] = _route(h, wp_scr[...], n_experts)


def _router(x2d, g, mod, sc_col, sh_col, wr_pad, *, tm, n_experts, per_token, tiles_per_batch):
    n_tok, d = x2d.shape
    return pl.pallas_call(
        functools.partial(_router_kernel, n_experts=n_experts),
        grid=(n_tok // tm,),
        in_specs=[pl.BlockSpec((tm, d), lambda i: (i, 0)),
                  pl.BlockSpec((1, d), lambda i: (0, 0)),
                  _mod_spec(per_token, tm, d, sc_col, tiles_per_batch),
                  _mod_spec(per_token, tm, d, sh_col, tiles_per_batch),
                  pl.BlockSpec((d, LANES), lambda i: (0, 0))],
        out_specs=[pl.BlockSpec((tm, LANES), lambda i: (i, 0)),
                   pl.BlockSpec((tm, d), lambda i: (i, 0))],
        out_shape=[jax.ShapeDtypeStruct((n_tok, LANES), F32),
                   jax.ShapeDtypeStruct((n_tok, d), BF16)],
        scratch_shapes=[pltpu.VMEM((d, LANES), BF16)],
        compiler_params=_params(("arbitrary",)),
    )(x2d, g, mod, mod, wr_pad)


MOE_TB = 256
MOE_SEG = 8
MOE_DTYPE = F32
MOE_TM = 512


def _moe_block_rows(n_experts):
    return -(-(2 * MOE_TB + n_experts * (MOE_SEG - 1)) // LANES) * LANES


def _routing_slots(comb, dest):
    lane = lax.broadcasted_iota(jnp.int32, comb.shape, 1).astype(F32)
    sel = comb > 0.0
    i_lo = jnp.min(jnp.where(sel, lane, float(LANES)), axis=-1, keepdims=True)
    i_hi = jnp.max(jnp.where(sel, lane, -1.0), axis=-1, keepdims=True)
    pick = lambda i, v: jnp.sum(jnp.where(lane == i, v, 0.0), axis=-1, keepdims=True)
    has_lo = i_lo < float(LANES)
    has_hi = i_hi > i_lo
    d_lo = jnp.where(has_lo, pick(i_lo, dest), -1.0)
    d_hi = jnp.where(has_hi, pick(i_hi, dest), -1.0)
    w_lo = pick(i_lo, comb)
    w_hi = jnp.where(has_hi, pick(i_hi, comb), 0.0)
    out = jnp.where(lane == 0.0, d_lo, 0.0)
    for k, v in ((1.0, d_hi), (2.0, w_lo), (3.0, w_hi)):
        out = jnp.where(lane == k, v, out)
    return out


def _one_hot_rows(row, n_rows):
    tb = row.shape[0]
    lane = lax.broadcasted_iota(jnp.int32, (1, LANES), 1).astype(F32)
    row = jnp.broadcast_to(row, (tb, LANES))
    blocks = [jnp.where(row == lane + float(cb * LANES), 1.0, 0.0) for cb in range(n_rows // LANES)]
    return jnp.concatenate(blocks, axis=1).astype(BF16)


_SEG_BITS = (256, 128, 64, 32, 16, 8)


def _segment_copies(make_copy, blk, pos_ref, off_ref, cnt_ref, n_experts, start):
    for e in range(n_experts):
        seg = blk * n_experts + e
        in_blk, in_sorted, n = off_ref[seg], pos_ref[seg], cnt_ref[seg]
        for size in _SEG_BITS:
            has = (n & size) != 0

            @pl.when(has)
            def _(in_blk=in_blk, in_sorted=in_sorted, size=size):
                cp = make_copy(pl.multiple_of(in_blk, MOE_SEG), pl.multiple_of(in_sorted, MOE_SEG), size)
                cp.start() if start else cp.wait()

            step = jnp.where(has, size, 0)
            in_blk, in_sorted = in_blk + step, in_sorted + step


def _moe_sort_kernel(pos_ref, off_ref, cnt_ref, gpos_ref, glen_ref, hbp_ref, combp_ref, hbs_ref,
                     combs_ref, dest_ref, xs_hbm, buf, zbuf, sems, *, n_rows, n_experts, n_blk, n_blk_p):
    b = pl.program_id(0)
    slot = b % 2

    def writes(blk, slot, start):
        def make_copy(block_row, sorted_row, size):
            return pltpu.make_async_copy(buf.at[slot, pl.ds(block_row, size)],
                                         xs_hbm.at[pl.ds(sorted_row, size)], sems.at[slot])
        _segment_copies(make_copy, blk, pos_ref, off_ref, cnt_ref, n_experts, start)

    def zero_gaps(start):
        for e in range(n_experts):
            row, n = gpos_ref[e], glen_ref[e]
            for size in _SEG_BITS:
                has = (n & size) != 0

                @pl.when(has)
                def _(row=row, size=size):
                    cp = pltpu.make_async_copy(
                        zbuf.at[pl.ds(0, size)],
                        xs_hbm.at[pl.ds(pl.multiple_of(row, MOE_SEG), size)], sems.at[2])
                    cp.start() if start else cp.wait()

                row = row + jnp.where(has, size, 0)
        tail_row, tail_n, big = gpos_ref[n_experts], glen_ref[n_experts], _SEG_BITS[0]

        def piece(i, carry):
            cp = pltpu.make_async_copy(
                zbuf.at[pl.ds(0, big)],
                xs_hbm.at[pl.ds(pl.multiple_of(tail_row + i * big, MOE_SEG), big)], sems.at[2])
            cp.start() if start else cp.wait()
            return carry

        lax.fori_loop(0, tail_n // big, piece, 0)

    @pl.when(b == 0)
    def _():
        zbuf[...] = jnp.zeros_like(zbuf)
        zero_gaps(True)

    @pl.when(b >= 2)
    def _():
        writes(b - 2, slot, False)

    in_prompt = b < n_blk_p
    comb = jnp.where(in_prompt, combp_ref[...], combs_ref[...])
    hb = jnp.where(in_prompt, hbp_ref[...], hbs_ref[...])
    tb = comb.shape[0]
    sel = jnp.where(comb > 0.0, 1.0, 0.0)
    r = lax.broadcasted_iota(jnp.int32, (tb, tb), 0)
    c = lax.broadcasted_iota(jnp.int32, (tb, tb), 1)
    rank = _dot(jnp.where(c < r, 1.0, 0.0).astype(BF16), sel.astype(BF16))
    cnt = jnp.sum(sel, axis=0, keepdims=True)
    cnt = jnp.floor((cnt + (MOE_SEG - 1)) / MOE_SEG) * MOE_SEG
    ru = lax.broadcasted_iota(jnp.int32, (LANES, LANES), 0)
    cu = lax.broadcasted_iota(jnp.int32, (LANES, LANES), 1)
    off = _dot(jnp.broadcast_to(cnt, (8, LANES)).astype(BF16),
               jnp.where(ru < cu, 1.0, 0.0).astype(BF16))[0:1]
    slots = _routing_slots(comb, off + rank)
    dest_ref[...] = slots
    pt = _one_hot_rows(slots[:, 0:1], n_rows) + _one_hot_rows(slots[:, 1:2], n_rows)
    buf[slot] = _dot(pt, hb, ((0,), (0,))).astype(buf.dtype)
    writes(b, slot, True)

    @pl.when(b == n_blk - 1)
    def _():
        writes(b, slot, False)
        if n_blk > 1:
            writes(b - 1, 1 - slot, False)
        zero_gaps(False)


def _moe_sort(seg, gap_pos, gap_len, hb_p, comb_p, hb_s, comb_s, n_sorted_rows, *, n_experts):
    n_p, d = hb_p.shape
    n_blk_p = n_p // MOE_TB
    n_blk = n_blk_p + 1
    n_rows = _moe_block_rows(n_experts)
    last_p = n_blk_p - 1
    return pl.pallas_call(
        functools.partial(_moe_sort_kernel, n_rows=n_rows, n_experts=n_experts, n_blk=n_blk,
                          n_blk_p=n_blk_p),
        grid_spec=pltpu.PrefetchScalarGridSpec(
            num_scalar_prefetch=5, grid=(n_blk,),
            in_specs=[pl.BlockSpec((MOE_TB, d), lambda b, *_: (jnp.minimum(b, last_p), 0)),
                      pl.BlockSpec((MOE_TB, LANES), lambda b, *_: (jnp.minimum(b, last_p), 0)),
                      pl.BlockSpec((MOE_TB, d), lambda b, *_: (0, 0)),
                      pl.BlockSpec((MOE_TB, LANES), lambda b, *_: (0, 0))],
            out_specs=[pl.BlockSpec((MOE_TB, LANES), lambda b, *_: (b, 0)),
                       pl.BlockSpec(memory_space=pl.ANY)],
            scratch_shapes=[pltpu.VMEM((2, n_rows, d), MOE_DTYPE),
                            pltpu.VMEM((_SEG_BITS[0], d), MOE_DTYPE),
                            pltpu.SemaphoreType.DMA((3,))]),
        out_shape=[jax.ShapeDtypeStruct((n_blk * MOE_TB, LANES), F32),
                   jax.ShapeDtypeStruct((n_sorted_rows, d), MOE_DTYPE)],
        compiler_params=_params(("arbitrary",)),
    )(*seg, gap_pos, gap_len, hb_p, comb_p, hb_s, comb_s)


def _experts_kernel(te_ref, nt_ref, x_ref, wg_ref, wu_ref, wd_ref, o_ref, acc_scr, *, n_c):
    del te_ref
    k = pl.program_id(0)
    c = pl.program_id(1)
    active = k < nt_ref[0]

    @pl.when(active)
    def _():
        h = x_ref[...].astype(BF16)
        act = _silu(_dot(h, wg_ref[0])) * _dot(h, wu_ref[0])
        y = _dot(act.astype(BF16), wd_ref[0])

        @pl.when(c == 0)
        def _():
            acc_scr[...] = y

        @pl.when(c > 0)
        def _():
            acc_scr[...] += y

        @pl.when(c == n_c - 1)
        def _():
            o_ref[...] = acc_scr[...].astype(o_ref.dtype)

    @pl.when(jnp.logical_not(active) & (c == n_c - 1))
    def _():
        o_ref[...] = jnp.zeros_like(o_ref)


def _experts(tile_expert, n_tiles, xs, wg, wu, wd, *, fc):
    n_rows, d = xs.shape
    n_c = wg.shape[2] // fc

    def row_map(k, c, te, nt):
        return (jnp.minimum(k, nt[0] - 1), 0)

    def w_map(k, c, te, nt):
        return (te[k], 0, jnp.where(k < nt[0], c, n_c - 1))

    def wd_map(k, c, te, nt):
        return (te[k], jnp.where(k < nt[0], c, n_c - 1), 0)

    return pl.pallas_call(
        functools.partial(_experts_kernel, n_c=n_c),
        grid_spec=pltpu.PrefetchScalarGridSpec(
            num_scalar_prefetch=2, grid=(n_rows // MOE_TM, n_c),
            in_specs=[pl.BlockSpec((MOE_TM, d), row_map),
                      pl.BlockSpec((1, d, fc), w_map), pl.BlockSpec((1, d, fc), w_map),
                      pl.BlockSpec((1, fc, d), wd_map)],
            out_specs=pl.BlockSpec((MOE_TM, d), lambda k, c, te, nt: (k, 0)),
            scratch_shapes=[pltpu.VMEM((MOE_TM, d), F32)]),
        out_shape=jax.ShapeDtypeStruct((n_rows, d), MOE_DTYPE),
        compiler_params=_params(("arbitrary", "arbitrary"), VMEM_LIMIT_MB),
    )(tile_expert, n_tiles, xs, wg, wu, wd)


def _moe_combine_kernel(pos_ref, off_ref, cnt_ref, ys_hbm, dest_ref, x_ref, gt_ref, g_ref,
                        o_ref, buf, sems, *, n_experts, n_blk, blk0):
    b = pl.program_id(0)
    slot = b % 2

    def reads(blk, slot, start):
        def make_copy(block_row, sorted_row, size):
            return pltpu.make_async_copy(ys_hbm.at[pl.ds(sorted_row, size)],
                                         buf.at[slot, pl.ds(block_row, size)], sems.at[slot])
        _segment_copies(make_copy, blk0 + blk, pos_ref, off_ref, cnt_ref, n_experts, start)

    @pl.when(b == 0)
    def _():
        buf[...] = jnp.zeros_like(buf)
        reads(b, slot, True)

    if n_blk > 1:
        @pl.when(b + 1 < n_blk)
        def _():
            reads(b + 1, 1 - slot, True)

    reads(b, slot, False)
    yb = buf[slot].astype(BF16)
    slots = dest_ref[...]
    n_rows = yb.shape[0]
    y = (slots[:, 2:3] * _dot(_one_hot_rows(slots[:, 0:1], n_rows), yb)
         + slots[:, 3:4] * _dot(_one_hot_rows(slots[:, 1:2], n_rows), yb))
    o_ref[...] = x_ref[...] + _mod_row(gt_ref) * (_rms(y) * g_ref[...])


def _moe_combine(sorted_pos, block_off, seg_cnt, ys, dest, x2d, mod, gt_col, g, *, blk0,
                 n_experts, per_token, tiles_per_batch):
    n_tok, d = x2d.shape
    n_blk = n_tok // MOE_TB
    n_rows = _moe_block_rows(n_experts)
    return pl.pallas_call(
        functools.partial(_moe_combine_kernel, n_experts=n_experts, n_blk=n_blk, blk0=blk0),
        grid_spec=pltpu.PrefetchScalarGridSpec(
            num_scalar_prefetch=3, grid=(n_blk,),
            in_specs=[pl.BlockSpec(memory_space=pl.ANY),
                      pl.BlockSpec((MOE_TB, LANES), lambda b, *_: (blk0 + b, 0)),
                      pl.BlockSpec((MOE_TB, d), lambda b, *_: (b, 0)),
                      _mod_spec(per_token, MOE_TB, d, gt_col, tiles_per_batch),
                      pl.BlockSpec((1, d), lambda b, *_: (0, 0))],
            out_specs=pl.BlockSpec((MOE_TB, d), lambda b, *_: (b, 0)),
            scratch_shapes=[pltpu.VMEM((2, n_rows, d), MOE_DTYPE), pltpu.SemaphoreType.DMA((2,))]),
        out_shape=jax.ShapeDtypeStruct((n_tok, d), F32),
        compiler_params=_params(("arbitrary",)),
    )(sorted_pos, block_off, seg_cnt, ys, dest, x2d, mod, g)


def _moe_schedule(combs, n_experts, n_tiles_max):
    def block_counts(comb):
        n_blk = comb.shape[0] // MOE_TB
        sel = (comb[:, :n_experts] > 0.0).astype(jnp.int32)
        return jnp.sum(sel.reshape(n_blk, MOE_TB, n_experts), axis=1)

    cnt = jnp.concatenate([block_counts(c) for c in combs], axis=0)
    cnt = -(-cnt // MOE_SEG) * MOE_SEG
    off = jnp.cumsum(cnt, axis=1) - cnt
    tot = jnp.sum(cnt, axis=0)
    grp = -(-tot // MOE_TM) * MOE_TM
    g_end = jnp.cumsum(grp)
    sorted_pos = (g_end - grp)[None, :] + (jnp.cumsum(cnt, axis=0) - cnt)
    n_tiles = g_end[-1] // MOE_TM
    first_row = jnp.arange(n_tiles_max, dtype=jnp.int32) * MOE_TM
    tile_expert = jnp.sum(first_row[:, None] >= g_end[None, :], axis=1)
    last = jnp.sum((n_tiles - 1) * MOE_TM >= g_end)
    tile_expert = jnp.minimum(tile_expert, last).astype(jnp.int32)
    flat = lambda a: a.reshape(-1).astype(jnp.int32)
    seg = (flat(sorted_pos), flat(off), flat(cnt))
    gap_pos = flat(jnp.concatenate([g_end - grp + tot, g_end[-1:]]))
    gap_len = flat(jnp.concatenate([grp - tot, n_tiles_max * MOE_TM - g_end[-1:]]))
    return seg, gap_pos, gap_len, tile_expert, n_tiles.reshape(1).astype(jnp.int32)


def _gla_conv_kernel(p_ref, wgu_ref, bgu_ref, gh_ref, wc_ref, mixed_ref, st_ref, cst_ref,
                     s_scr, uext_scr, *, tt, n_t, heads, dk, dv, cw):
    ti = pl.program_id(1)
    hk = heads * dk
    hv = heads * dv
    o_q, o_k, o_v, o_og = 0, hk, 2 * hk, 2 * hk + hv
    o_ch = o_og + hv
    o_cb, o_cc, o_lr = o_ch + cw, o_ch + 2 * cw, o_ch + 3 * cw
    L = GLA_CHUNK

    @pl.when(ti == 0)
    def _():
        s_scr[...] = jnp.zeros_like(s_scr)
        uext_scr[0:8, :] = jnp.zeros((8, cw), F32)

    logit = _dot(p_ref[:, o_lr:o_lr + LANES].astype(BF16), wgu_ref[...]) + bgu_ref[...]
    logg = _log_sigmoid(logit) / GATE_NORMALIZER

    r = lax.broadcasted_iota(jnp.int32, (tt, tt), 0)
    c = lax.broadcasted_iota(jnp.int32, (tt, tt), 1)
    tri = jnp.where((r // L == c // L) & (c <= r), 1.0, 0.0).astype(BF16)
    g_hi = logg.astype(BF16)
    rem = logg - g_hi.astype(F32)
    g_mid = rem.astype(BF16)
    g_lo = (rem - g_mid.astype(F32)).astype(BF16)
    bc = _dot(tri, g_hi) + (_dot(tri, g_mid) + _dot(tri, g_lo))

    rl = lax.broadcasted_iota(jnp.int32, (L, L), 0)
    cl = lax.broadcasted_iota(jnp.int32, (L, L), 1)
    tril = cl <= rl
    lane = lax.broadcasted_iota(jnp.int32, (L, LANES), 1)
    heads_per_blk = LANES // dk
    nt_dims = ((1,), (1,))

    state = [s_scr[h] for h in range(heads)]
    for ck in range(tt // L):
        rows = slice(L * ck, L * ck + L)
        b = bc[rows]
        bl = b[L - 1:L]
        q_in = (p_ref[rows, o_q:o_q + hk] * (dk ** -0.5)) * jnp.exp(b)
        k = p_ref[rows, o_k:o_k + hk]
        k_out = (k * jnp.exp(-b)).astype(BF16)
        k_dec = (k * jnp.exp(bl - b)).astype(BF16)
        dec = jnp.exp(bl)
        q_in = q_in.astype(BF16)
        for h in range(heads):
            blk = slice(LANES * (h // heads_per_blk), LANES * (h // heads_per_blk) + LANES)
            in_head = (lane // dk) == (h % heads_per_blk)
            qm = jnp.where(in_head, q_in[:, blk], jnp.zeros_like(q_in[:, blk]))
            att = _dot(qm, k_out[:, blk], nt_dims)
            att = jnp.where(tril, att, 0.0).astype(BF16)
            vh = p_ref[rows, o_v + dv * h:o_v + dv * h + dv].astype(BF16)
            o = _dot(att, vh) + _dot(qm, state[h].astype(BF16), nt_dims)
            ds_t = _dot(vh, k_dec[:, blk], ((0,), (0,)))
            state[h] = state[h] * dec[:, blk] + ds_t
            og = p_ref[rows, o_og + dv * h:o_og + dv * h + dv]
            res = (_rms(o) * gh_ref[...]) * _silu(og)
            mixed_ref[rows, dv * h:dv * h + dv] = res.astype(mixed_ref.dtype)
    for h in range(heads):
        s_scr[h] = state[h]

    u = p_ref[:, o_cc:o_cc + cw] * p_ref[:, o_ch:o_ch + cw]
    uext_scr[8:8 + tt, :] = u
    y = (wc_ref[0:1, :] * uext_scr[6:6 + tt, :] + wc_ref[1:2, :] * uext_scr[7:7 + tt, :]
         + wc_ref[2:3, :] * u)
    mixed_ref[:, hv:hv + cw] = (p_ref[:, o_cb:o_cb + cw] * y).astype(mixed_ref.dtype)
    tail = uext_scr[tt + 6:tt + 8, :]
    uext_scr[6:8, :] = tail

    @pl.when(ti == n_t - 1)
    def _():
        st_ref[0] = s_scr[...]
        cst_ref[0] = tail


def _gla_conv_prompt(proj, wgu_pad, bgu, g_head, w_conv, *, bsz, t, tt, heads, dk, dv, cw):
    n_t = t // tt
    n_in = proj.shape[1]
    width = heads * dv + cw
    return pl.pallas_call(
        functools.partial(_gla_conv_kernel, tt=tt, n_t=n_t, heads=heads, dk=dk, dv=dv, cw=cw),
        grid=(bsz, n_t),
        in_specs=[pl.BlockSpec((tt, n_in), lambda b, i: (b * n_t + i, 0)),
                  pl.BlockSpec(wgu_pad.shape, lambda b, i: (0, 0)),
                  pl.BlockSpec(bgu.shape, lambda b, i: (0, 0)),
                  pl.BlockSpec(g_head.shape, lambda b, i: (0, 0)),
                  pl.BlockSpec(w_conv.shape, lambda b, i: (0, 0))],
        out_specs=[pl.BlockSpec((tt, width), lambda b, i: (b * n_t + i, 0)),
                   pl.BlockSpec((1, heads, dv, LANES), lambda b, i: (b, 0, 0, 0)),
                   pl.BlockSpec((1, 2, cw), lambda b, i: (b, 0, 0))],
        out_shape=[jax.ShapeDtypeStruct((bsz * t, width), BF16),
                   jax.ShapeDtypeStruct((bsz, heads, dv, LANES), F32),
                   jax.ShapeDtypeStruct((bsz, 2, cw), F32)],
        scratch_shapes=[pltpu.VMEM((heads, dv, LANES), F32), pltpu.VMEM((tt + 8, cw), F32)],
        compiler_params=_params(("parallel", "arbitrary"), VMEM_LIMIT_MB),
    )(proj, wgu_pad, bgu, g_head, w_conv)


def _gla_conv_step_kernel(q_ref, k_ref, v_ref, og_ref, ch_ref, cb_ref, cc_ref, lr_ref,
                          wgu_ref, bgu_ref, gh_ref, wc_ref, s_ref, cbuf_ref,
                          o_ref, y_ref, sn_ref, cn_ref, a_scr, *, dk):
    logit = _mm(wgu_ref[...], lr_ref[...], "f32") + bgu_ref[...]
    a_scr[...] = jnp.exp(_log_sigmoid(logit) / GATE_NORMALIZER)
    v_t = v_ref[...]

    def body(d, acc):
        a = a_scr[pl.ds(d, 1), :]
        kd = k_ref[pl.ds(d, 1), :]
        qd = q_ref[pl.ds(d, 1), :]
        s_new = a * s_ref[0, d] + kd * v_t
        sn_ref[0, d] = s_new
        return acc + (qd * (dk ** -0.5)) * s_new

    o = lax.fori_loop(0, dk, body, jnp.zeros(v_t.shape, F32))
    o = o * lax.rsqrt(jnp.mean(o * o, axis=0, keepdims=True) + EPS) * gh_ref[...]
    o_ref[...] = o * _silu(og_ref[...])
    u = cc_ref[...] * ch_ref[...]
    y = wc_ref[0] * cbuf_ref[0] + wc_ref[1] * cbuf_ref[1] + wc_ref[2] * u
    y_ref[...] = cb_ref[...] * y
    cn_ref[0] = cbuf_ref[1]
    cn_ref[1] = u


def _gla_conv_step(proj_t, wgu_t_pad, bgu_col, gh_col, wc_col, s_t, cbuf_t, *, heads, dk, dv, cw):
    bsz = proj_t.shape[1]
    hk, hv = heads * dk, heads * dv
    cs = cw // heads
    assert dv == LANES and cs == LANES and dk * 2 == LANES
    o_k, o_v, o_og = hk, 2 * hk, 2 * hk + hv
    o_ch = o_og + hv
    o_cb, o_cc, o_lr = o_ch + cw, o_ch + 2 * cw, o_ch + 3 * cw
    row = lambda off, size: (lambda h: (off // size + h, 0))
    blk = lambda size, off: pl.BlockSpec((size, bsz), row(off, size))
    return pl.pallas_call(
        functools.partial(_gla_conv_step_kernel, dk=dk),
        grid=(heads,),
        in_specs=[blk(dk, 0), blk(dk, o_k), blk(dv, o_v), blk(dv, o_og),
                  blk(cs, o_ch), blk(cs, o_cb), blk(cs, o_cc),
                  pl.BlockSpec((LANES, bsz), lambda h: (o_lr // LANES, 0)),
                  pl.BlockSpec((dk, LANES), lambda h: (h, 0)),
                  pl.BlockSpec((dk, 1), lambda h: (h, 0)),
                  pl.BlockSpec((dv, 1), lambda h: (0, 0)),
                  pl.BlockSpec((3, cs, 1), lambda h: (0, h, 0)),
                  pl.BlockSpec((1, dk, dv, bsz), lambda h: (h, 0, 0, 0)),
                  pl.BlockSpec((2, cs, bsz), lambda h: (0, h, 0))],
        out_specs=[pl.BlockSpec((dv, bsz), lambda h: (h, 0)),
                   pl.BlockSpec((cs, bsz), lambda h: (h, 0)),
                   pl.BlockSpec((1, dk, dv, bsz), lambda h: (h, 0, 0, 0)),
                   pl.BlockSpec((2, cs, bsz), lambda h: (0, h, 0))],
        out_shape=[jax.ShapeDtypeStruct((hv, bsz), F32),
                   jax.ShapeDtypeStruct((cw, bsz), F32),
                   jax.ShapeDtypeStruct(s_t.shape, F32),
                   jax.ShapeDtypeStruct(cbuf_t.shape, F32)],
        scratch_shapes=[pltpu.VMEM((dk, bsz), F32)],
        compiler_params=_params(("parallel",)),
    )(proj_t, proj_t, proj_t, proj_t, proj_t, proj_t, proj_t, proj_t,
      wgu_t_pad, bgu_col, gh_col, wc_col, s_t, cbuf_t)


def _s5_scan_kernel(*refs, tt, n_t, nb, half, n_cast):
    u_ref, bb_ref, cm_ref, abr_ref, abi_ref, d_ref = refs[:6]
    y_ref, sre_ref, sim_ref = refs[6 + n_cast:9 + n_cast]
    scr, tb_scr, xr_scr, xi_scr = refs[9 + 2 * n_cast:]
    _cast_passengers(refs[6:6 + n_cast], refs[9 + n_cast:9 + 2 * n_cast])
    ti = pl.program_id(1)
    n_blk = half // LANES

    @pl.when(ti == 0)
    def _():
        xr_scr[...] = jnp.zeros_like(xr_scr)
        xi_scr[...] = jnp.zeros_like(xi_scr)

    for b in range(nb):
        tb_scr[pl.ds(b, tt, stride=nb), :] = u_ref[b].astype(F32)
    for k in range(nb):
        rows = slice(k * tt, (k + 1) * tt)
        bu = _dot(tb_scr[rows, :].astype(BF16), bb_ref[0])
        for cb in range(2 * n_blk):
            scr[cb, rows, :] = bu[:, cb * LANES:(cb + 1) * LANES]
    a_re = [jnp.broadcast_to(abr_ref[0][:, cb * LANES:(cb + 1) * LANES], (nb, LANES))
            for cb in range(n_blk)]
    a_im = [jnp.broadcast_to(abi_ref[0][:, cb * LANES:(cb + 1) * LANES], (nb, LANES))
            for cb in range(n_blk)]

    def body(t, carry):
        rows = pl.ds(pl.multiple_of(t * nb, nb), nb)
        out = []
        for cb in range(n_blk):
            xr, xi = carry[2 * cb], carry[2 * cb + 1]
            nr = (a_re[cb] * xr - a_im[cb] * xi) + scr[cb, rows, :]
            ni = (a_re[cb] * xi + a_im[cb] * xr) + scr[n_blk + cb, rows, :]
            scr[cb, rows, :] = nr
            scr[n_blk + cb, rows, :] = ni
            out += [nr, ni]
        return tuple(out)

    init = []
    for cb in range(n_blk):
        init += [xr_scr[:, cb * LANES:(cb + 1) * LANES], xi_scr[:, cb * LANES:(cb + 1) * LANES]]
    fin = lax.fori_loop(0, tt, body, tuple(init), unroll=8)
    xr = jnp.concatenate([fin[2 * cb] for cb in range(n_blk)], axis=1)
    xi = jnp.concatenate([fin[2 * cb + 1] for cb in range(n_blk)], axis=1)
    xr_scr[...] = xr
    xi_scr[...] = xi
    for k in range(nb):
        rows = slice(k * tt, (k + 1) * tt)
        xs = jnp.concatenate([scr[cb, rows, :] for cb in range(2 * n_blk)], axis=1)
        y = _dot(xs.astype(BF16), cm_ref[0]) + d_ref[0] * tb_scr[rows, :]
        tb_scr[rows, :] = _gelu_tanh(y)
    for b in range(nb):
        y_ref[b] = tb_scr[pl.ds(b, tt, stride=nb), :].astype(y_ref.dtype)

    @pl.when(ti == n_t - 1)
    def _():
        sre_ref[...] = xr
        sim_ref[...] = xi


def _s5_prompt(u3d, bb_blk, c_blk, ab_re, ab_im, d_blk, cast_ws, *, tt):
    nb, t, d = u3d.shape
    n_j = d // LANES
    n_t = t // tt
    half = bb_blk.shape[2] // 2
    w_specs = _passenger_specs(cast_ws, n_j * n_t, lambda j, i: j * n_t + i)
    return pl.pallas_call(
        functools.partial(_s5_scan_kernel, tt=tt, n_t=n_t, nb=nb, half=half, n_cast=len(cast_ws)),
        grid=(n_j, n_t),
        in_specs=[pl.BlockSpec((nb, tt, LANES), lambda j, i: (0, i, j)),
                  pl.BlockSpec((1, LANES, 2 * half), lambda j, i: (j, 0, 0)),
                  pl.BlockSpec((1, 2 * half, LANES), lambda j, i: (j, 0, 0)),
                  pl.BlockSpec((1, 1, half), lambda j, i: (j, 0, 0)),
                  pl.BlockSpec((1, 1, half), lambda j, i: (j, 0, 0)),
                  pl.BlockSpec((1, 1, LANES), lambda j, i: (j, 0, 0))] + w_specs,
        out_specs=[pl.BlockSpec((nb, tt, LANES), lambda j, i: (0, i, j)),
                   pl.BlockSpec((nb, half), lambda j, i: (0, j)),
                   pl.BlockSpec((nb, half), lambda j, i: (0, j))] + w_specs,
        out_shape=[jax.ShapeDtypeStruct((nb, t, d), BF16),
                   jax.ShapeDtypeStruct((nb, n_j * half), F32),
                   jax.ShapeDtypeStruct((nb, n_j * half), F32)]
        + [jax.ShapeDtypeStruct(w.shape, BF16) for w in cast_ws],
        scratch_shapes=[pltpu.VMEM((2 * half // LANES, nb * tt, LANES), F32),
                        pltpu.VMEM((nb * tt, LANES), F32),
                        pltpu.VMEM((nb, half), F32), pltpu.VMEM((nb, half), F32)],
        compiler_params=_params(("arbitrary", "arbitrary"), VMEM_LIMIT_MB),
    )(u3d, bb_blk, c_blk, ab_re, ab_im, d_blk, *cast_ws)


def _s5_step_kernel(u_ref, bb_ref, cm_ref, abr_ref, abi_ref, d_ref, sre_ref, sim_ref,
                    y_ref, nre_ref, nim_ref, *, half):
    u = u_ref[...]
    bu = _mm(u, bb_ref[0], "f32")
    ar, ai = abr_ref[0], abi_ref[0]
    sr, si = sre_ref[...], sim_ref[...]
    nr = (ar * sr - ai * si) + bu[:, 0:half]
    ni = (ar * si + ai * sr) + bu[:, half:2 * half]
    nre_ref[...] = nr
    nim_ref[...] = ni
    y = _mm(jnp.concatenate([nr, ni], axis=1), cm_ref[0], "f32") + d_ref[0] * u
    y_ref[...] = _gelu_tanh(y)


def _s5_step(u2d, bb_blk, c_blk, ab_re, ab_im, d_blk, s_re, s_im):
    bsz, d = u2d.shape
    n_j = d // LANES
    half = bb_blk.shape[2] // 2
    st = pl.BlockSpec((bsz, half), lambda j: (0, j))
    return pl.pallas_call(
        functools.partial(_s5_step_kernel, half=half),
        grid=(n_j,),
        in_specs=[pl.BlockSpec((bsz, LANES), lambda j: (0, j)),
                  pl.BlockSpec((1, LANES, 2 * half), lambda j: (j, 0, 0)),
                  pl.BlockSpec((1, 2 * half, LANES), lambda j: (j, 0, 0)),
                  pl.BlockSpec((1, 1, half), lambda j: (j, 0, 0)),
                  pl.BlockSpec((1, 1, half), lambda j: (j, 0, 0)),
                  pl.BlockSpec((1, 1, LANES), lambda j: (j, 0, 0)),
                  st, st],
        out_specs=[pl.BlockSpec((bsz, LANES), lambda j: (0, j)), st, st],
        out_shape=[jax.ShapeDtypeStruct((bsz, d), F32),
                   jax.ShapeDtypeStruct(s_re.shape, F32),
                   jax.ShapeDtypeStruct(s_im.shape, F32)],
        compiler_params=_params(("parallel",)),
    )(u2d, bb_blk, c_blk, ab_re, ab_im, d_blk, s_re, s_im)


def _s5_zoh(a_re, a_im, log_dt, b_re, b_im):
    dt = jnp.exp(log_dt)[:, None]
    mag = jnp.exp(dt * a_re)
    ab_re, ab_im = mag * jnp.cos(dt * a_im), mag * jnp.sin(dt * a_im)
    den = a_re * a_re + a_im * a_im
    nr, ni = ab_re - 1.0, ab_im
    f_re = (nr * a_re + ni * a_im) / den
    f_im = (ni * a_re - nr * a_im) / den
    bb_re = f_re[..., None] * b_re - f_im[..., None] * b_im
    bb_im = f_re[..., None] * b_im + f_im[..., None] * b_re
    return ab_re, ab_im, bb_re, bb_im


def _s5_step_blocks(ab_re, ab_im, bb_re, bb_im, c_re, c_im, d_skip):
    g, p = ab_re.shape
    cg = S5_CHUNK_GROUPS
    n_j = g // cg
    eye = jnp.eye(cg, dtype=F32)

    def in_blk(m):
        m = m.reshape(n_j, cg, p, S5_GROUP)
        return jnp.einsum("jgpi,gh->jgihp", m, eye).reshape(n_j, cg * S5_GROUP, cg * p)

    def out_blk(m):
        m = m.reshape(n_j, cg, S5_GROUP, p)
        return jnp.einsum("jgip,gh->jgphi", m, eye).reshape(n_j, cg * p, cg * S5_GROUP)

    bb_blk = jnp.concatenate([in_blk(bb_re), in_blk(bb_im)], axis=2)
    c_blk = jnp.concatenate([out_blk(c_re), out_blk(-c_im)], axis=1)
    return (bb_blk, c_blk, ab_re.reshape(n_j, 1, cg * p), ab_im.reshape(n_j, 1, cg * p),
            d_skip.reshape(n_j, 1, cg * S5_GROUP))


def kernel(x_prompt, x_sample, c_prompt, c_sample, state_gla, state_conv, state_s5_re, state_s5_im, w_ada, b_ada, g_norm, w_in0, w_gate_up, b_gate_up, g_head_norm, w_conv, w_out0, w_ffn_gate, w_ffn_up, w_ffn_down, w_in1, s5_a_re, s5_a_im, s5_log_dt, s5_b_re, s5_b_im, s5_c_re, s5_c_im, s5_d, w_glu, w_router, w_exp_gate, w_exp_up, w_exp_down):
    bsz, t, d = x_prompt.shape
    bs = x_sample.shape[0]
    heads, dk, dv = state_gla.shape[2:]
    cw = state_conv.shape[3]
    lowrank = w_gate_up.shape[1]
    hk, hv = heads * dk, heads * dv
    n_experts = w_router.shape[2]
    g_s5, p_s5 = s5_a_re.shape[1:]

    w0 = w_in0[0]
    o_lr = 2 * hk + 2 * hv
    w0r = jnp.concatenate([w0[:, :o_lr], w0[:, o_lr + lowrank:], w0[:, o_lr:o_lr + lowrank],
                           jnp.zeros((d, LANES - lowrank), F32)], axis=1)

    mod, w0r_bf = _ada(jnp.concatenate([c_prompt, c_sample], axis=0), w_ada, b_ada, w0r)
    mod_p = [mod[l, :bsz].reshape(bsz, 1, N_MOD * d) for l in range(2)]
    mod_s = [mod[l, bsz:] for l in range(2)]
    SH_M, SC_M, GT_M, SH_F, SC_F, GT_F = range(N_MOD)
    gn = lambda l, k: g_norm[l, k].reshape(1, d)

    wgu_pad = jnp.concatenate([w_gate_up[0], jnp.zeros((LANES - lowrank, hk), F32)], axis=0)
    zoh = _s5_zoh(s5_a_re[0], s5_a_im[0], s5_log_dt[0], s5_b_re[0], s5_b_im[0])
    bb_blk, c_blk, ab_re, ab_im, d_blk = _s5_step_blocks(*zoh, s5_c_re[0], s5_c_im[0], s5_d[0])
    wr_pad = jnp.concatenate([w_router[0], jnp.zeros((d, LANES - n_experts), F32)], axis=1)
    bf = lambda a: a.astype(BF16)

    tm = TOKEN_TILE
    tpb = t // tm
    pk = dict(per_token=False, tiles_per_batch=tpb)
    xp = x_prompt.reshape(bsz * t, d)
    proj, w_o0, w_fg, w_fu, w_fd, w_i1, w_gl = _inproj(
        xp, gn(0, 0), mod_p[0], SC_M, SH_M, w0r_bf, tm=tm, bn=w0r.shape[1], prec="bf16",
        vmem_mb=VMEM_LIMIT_MB, cast_ws=(w_out0[0], w_ffn_gate[0], w_ffn_up[0], w_ffn_down[0], w_in1[0], w_glu[0]), **pk)
    mixed, st_t, cst = _gla_conv_prompt(
        proj, bf(wgu_pad), b_gate_up[0].reshape(1, hk), g_head_norm[0].reshape(1, dv), w_conv[0],
        bsz=bsz, t=t, tt=SEQ_TILE, heads=heads, dk=dk, dv=dv, cw=cw)
    st_t = st_t.reshape(bsz, heads // 2, 2, dv, 2, dk)
    gla_p = jnp.stack([st_t[:, :, 0, :, 0, :], st_t[:, :, 1, :, 1, :]], axis=2)
    gla_p = jnp.swapaxes(gla_p.reshape(bsz, heads, dv, dk), 2, 3)
    x2, u = _mix_ffn_proj(mixed, w_o0, xp, mod_p[0], mod_p[1], g_norm, w_fg, w_fu, w_fd, w_i1, tm=tm,
                          tiles_per_batch=tpb, vmem_mb=VMEM_LIMIT_FFN_MB)
    ew = (w_exp_gate[0], w_exp_up[0], w_exp_down[0])
    yg, re_p, im_p, *ew_bf = _s5_prompt(
        u.reshape(bsz, t, d), bf(bb_blk), bf(c_blk), ab_re, ab_im, d_blk,
        tuple(w.reshape(-1, w.shape[2]) for w in ew), tt=SEQ_TILE)
    w_eg, w_eu, w_ed = (wb.reshape(w.shape) for wb, w in zip(ew_bf, ew))
    x3, comb, hb = _glu_route(yg.reshape(bsz * t, d), w_gl, x2, mod_p[1], g_norm, wr_pad, tm=tm,
                              n_experts=n_experts, tiles_per_batch=tpb, vmem_mb=VMEM_LIMIT_MB)

    sk = dict(per_token=True, tiles_per_batch=1)
    xs = x_sample.reshape(bs, d)
    proj_s = _inproj(xs, gn(0, 0), mod_s[0], SC_M, SH_M, w0r, tm=bs, bn=SAMPLE_PROJ_COLS, prec="f32", **sk)
    s_t = jnp.transpose(state_gla[:, 0], (1, 2, 3, 0))
    cbuf_t = jnp.transpose(state_conv[:, 0], (1, 2, 0))
    o_t, y_t, sn_t, cn_t = _gla_conv_step(
        proj_s.T, wgu_pad.T, b_gate_up[0].reshape(hk, 1), g_head_norm[0].reshape(dv, 1),
        w_conv[0].reshape(3, cw, 1), s_t, cbuf_t, heads=heads, dk=dk, dv=dv, cw=cw)
    mixed_s = jnp.concatenate([o_t, y_t], axis=0).T
    gla_s = jnp.transpose(sn_t, (3, 0, 1, 2))
    conv_s = jnp.transpose(cn_t, (2, 0, 1))
    x1s = _outproj(mixed_s, w_out0[0], xs, mod_s[0], GT_M, gn(0, 1), tm=bs, prec="f32", glu=False,
                   **sk)
    x2s = _ffn(x1s, gn(0, 2), mod_s[0], SC_F, SH_F, GT_F, gn(0, 3), w_ffn_gate[0], w_ffn_up[0],
               w_ffn_down[0], tm=bs, fc=SAMPLE_COLS, prec="f32", **sk)
    us = _inproj(x2s, gn(1, 0), mod_s[1], SC_M, SH_M, w_in1[0], tm=bs, bn=SAMPLE_COLS, prec="f32", **sk)
    ygs, re_s, im_s = _s5_step(us, bb_blk, c_blk, ab_re, ab_im, d_blk,
                               state_s5_re[:, 0].reshape(bs, g_s5 * p_s5),
                               state_s5_im[:, 0].reshape(bs, g_s5 * p_s5))
    x3s = _outproj(ygs, w_glu[0], x2s, mod_s[1], GT_M, gn(1, 1), tm=bs, prec="f32", glu=True,
                   vmem_mb=VMEM_LIMIT_MB, **sk)
    comb_s, hb_s = _router(x3s, gn(1, 2), mod_s[1], SC_F, SH_F, wr_pad, tm=bs, n_experts=n_experts,
                           **sk)

    assert bs <= MOE_TB
    n_blk = bsz * t // MOE_TB + 1
    n_tiles_max = -(-(n_blk * _moe_block_rows(n_experts)) // MOE_TM) + n_experts
    pad_rows = lambda a: jnp.concatenate([a, jnp.zeros((MOE_TB - bs, a.shape[1]), a.dtype)], axis=0)
    hb_s, comb_s = pad_rows(hb_s), pad_rows(comb_s)
    seg, gap_pos, gap_len, tile_expert, n_tiles = _moe_schedule((comb, comb_s), n_experts,
                                                                n_tiles_max)
    dest, xs_sorted = _moe_sort(seg, gap_pos, gap_len, hb, comb, hb_s, comb_s,
                                n_tiles_max * MOE_TM, n_experts=n_experts)
    ys_sorted = _experts(tile_expert, n_tiles, xs_sorted, w_eg, w_eu, w_ed, fc=EXPERT_FF_CHUNK)
    x4 = _moe_combine(*seg, ys_sorted, dest, x3, mod_p[1], GT_F, gn(1, 3), blk0=0,
                      n_experts=n_experts, per_token=False, tiles_per_batch=t // MOE_TB)
    x4s = _moe_combine(*seg, ys_sorted, dest, pad_rows(x3s), pad_rows(mod_s[1]), GT_F,
                       gn(1, 3), blk0=n_blk - 1, n_experts=n_experts, per_token=True,
                       tiles_per_batch=1)[:bs]

    return (x4.reshape(bsz, t, d), x4s.reshape(bs, 1, d),
            gla_p[:, None], cst[:, None],
            re_p.reshape(bsz, 1, g_s5, p_s5), im_p.reshape(bsz, 1, g_s5, p_s5),
            gla_s[:, None], conv_s[:, None],
            re_s.reshape(bs, 1, g_s5, p_s5), im_s.reshape(bs, 1, g_s5, p_s5))
```

```python
import functools
import math

import jax
import jax.numpy as jnp
from jax import lax
from jax.experimental import pallas as pl
from jax.experimental.pallas import tpu as pltpu

F32 = jnp.float32
BF16 = jnp.bfloat16
EPS = 1e-6
LANES = 128
GLA_CHUNK = 64
GATE_NORMALIZER = 16.0
N_MOD = 6
S5_GROUP = 16
S5_CHUNK_GROUPS = 8

TOKEN_TILE = 512
SEQ_TILE = 256
VMEM_LIMIT_MB = 48
VMEM_LIMIT_FFN_MB = 56
EXPERT_FF_CHUNK = 1792
SAMPLE_PROJ_COLS = 640
SAMPLE_COLS = 256


def _dot(a, b, dims=None):
    if dims is None:
        return jnp.dot(a, b, preferred_element_type=F32)
    return lax.dot_general(a, b, (dims, ((), ())), preferred_element_type=F32)


def _split(a, parts=2):
    rem = a.astype(F32)
    out = []
    for _ in range(parts - 1):
        piece = rem.astype(BF16)
        out.append(piece)
        rem = rem - piece.astype(F32)
    out.append(rem.astype(BF16))
    return out


def _mm(a, w, prec):
    if prec == "bf16":
        return _dot(a.astype(BF16), w.astype(BF16))
    a1, a2, a3 = _split(a, 3)
    w1, w2, w3 = _split(w, 3)
    small = (_dot(a1, w3) + _dot(a3, w1)) + _dot(a2, w2)
    return _dot(a1, w1) + ((_dot(a1, w2) + _dot(a2, w1)) + small)


def _silu(x):
    return x * jax.nn.sigmoid(x)


def _gelu_tanh(x):
    return 0.5 * x * (1.0 + jnp.tanh(math.sqrt(2.0 / math.pi) * (x + 0.044715 * (x * x * x))))


def _log_sigmoid(x):
    return -(jnp.maximum(-x, 0.0) + jnp.log1p(jnp.exp(-jnp.abs(x))))


def _rms(x):
    return x * lax.rsqrt(jnp.mean(x * x, axis=-1, keepdims=True) + EPS)


def _mod_row(ref):
    return ref[0] if len(ref.shape) == 3 else ref[...]


def _normmod(x, g, sc, sh):
    return (_rms(x) * g) * (1.0 + sc) + sh


def _mod_spec(per_token, tm, d, col, tiles_per_batch):
    if per_token:
        return pl.BlockSpec((tm, d), lambda i, *_: (i, col))
    return pl.BlockSpec((1, 1, d), lambda i, *_: (i // tiles_per_batch, 0, col))


def _params(sem, vmem_mb=None):
    kw = dict(dimension_semantics=sem)
    if vmem_mb is not None:
        kw["vmem_limit_bytes"] = vmem_mb << 20
    return pltpu.CompilerParams(**kw)


def _ada_kernel(c_ref, w_ref, b_ref, o_ref):
    o_ref[0] = _mm(_silu(c_ref[...]), w_ref[0], "f32") + b_ref[0]


def _ada(c_all, w_ada, b_ada):
    depth, d, n6 = w_ada.shape
    rows = c_all.shape[0]
    return pl.pallas_call(
        _ada_kernel,
        grid=(depth, n6 // d),
        in_specs=[pl.BlockSpec((rows, d), lambda l, j: (0, 0)),
                  pl.BlockSpec((1, d, d), lambda l, j: (l, 0, j)),
                  pl.BlockSpec((1, 1, d), lambda l, j: (l, 0, j))],
        out_specs=pl.BlockSpec((1, rows, d), lambda l, j: (l, 0, j)),
        out_shape=jax.ShapeDtypeStruct((depth, rows, n6), F32),
        compiler_params=_params(("parallel", "parallel")),
    )(c_all, w_ada, b_ada.reshape(depth, 1, n6))


BF16_ROWS = 16


def _passenger_specs(ws, n_steps, step):
    specs = []
    for w in ws:
        rows, cols = w.shape
        units = rows // BF16_ROWS
        n_used = max(k for k in range(1, n_steps + 1) if units % k == 0)
        specs.append(pl.BlockSpec(
            (rows // n_used, cols),
            lambda *idx, n_used=n_used: (jnp.minimum(step(*idx), n_used - 1), 0)))
    return specs


def _cast_passengers(in_refs, out_refs):
    for src, dst in zip(in_refs, out_refs):
        dst[...] = src[...].astype(dst.dtype)


def _inproj_kernel(*refs, prec, n_cast):
    x_ref, g_ref, sc_ref, sh_ref, w_ref = refs[:5]
    o_ref = refs[5 + n_cast]
    _cast_passengers(refs[5:5 + n_cast], refs[6 + n_cast:])
    h = _normmod(x_ref[...], g_ref[...], _mod_row(sc_ref), _mod_row(sh_ref))
    o_ref[...] = _mm(h, w_ref[...], prec).astype(o_ref.dtype)


def _inproj(x2d, g, mod, sc_col, sh_col, w, *, tm, bn, prec, per_token, tiles_per_batch,
            out_dtype=F32, vmem_mb=None, cast_ws=()):
    n_tok, d = x2d.shape
    n_out = w.shape[1]
    n_j = n_out // bn
    p_specs = _passenger_specs(cast_ws, n_tok // tm * n_j, lambda i, j: i * n_j + j)
    out = pl.pallas_call(
        functools.partial(_inproj_kernel, prec=prec, n_cast=len(cast_ws)),
        grid=(n_tok // tm, n_j),
        in_specs=[pl.BlockSpec((tm, d), lambda i, j: (i, 0)),
                  pl.BlockSpec((1, d), lambda i, j: (0, 0)),
                  _mod_spec(per_token, tm, d, sc_col, tiles_per_batch),
                  _mod_spec(per_token, tm, d, sh_col, tiles_per_batch),
                  pl.BlockSpec((d, bn), lambda i, j: (0, j))] + p_specs,
        out_specs=[pl.BlockSpec((tm, bn), lambda i, j: (i, j))] + p_specs,
        out_shape=[jax.ShapeDtypeStruct((n_tok, n_out), out_dtype)]
        + [jax.ShapeDtypeStruct(c.shape, BF16) for c in cast_ws],
        compiler_params=_params(("arbitrary", "arbitrary"), vmem_mb),
    )(x2d, g, mod, mod, w, *cast_ws)
    return out if cast_ws else out[0]


def _outproj_kernel(a_ref, w_ref, x_ref, gt_ref, g_ref, o_ref, *, prec, glu):
    z = _mm(a_ref[...], w_ref[...], prec)
    if glu:
        d = z.shape[1] // 2
        z = z[:, :d] * jax.nn.sigmoid(z[:, d:])
    o_ref[...] = x_ref[...] + _mod_row(gt_ref) * (_rms(z) * g_ref[...])


def _outproj(a2d, w, x2d, mod, gt_col, g, *, tm, prec, glu, per_token, tiles_per_batch,
             vmem_mb=None):
    n_tok, d = x2d.shape
    k, n_out = w.shape
    return pl.pallas_call(
        functools.partial(_outproj_kernel, prec=prec, glu=glu),
        grid=(n_tok // tm,),
        in_specs=[pl.BlockSpec((tm, k), lambda i: (i, 0)),
                  pl.BlockSpec((k, n_out), lambda i: (0, 0)),
                  pl.BlockSpec((tm, d), lambda i: (i, 0)),
                  _mod_spec(per_token, tm, d, gt_col, tiles_per_batch),
                  pl.BlockSpec((1, d), lambda i: (0, 0))],
        out_specs=pl.BlockSpec((tm, d), lambda i: (i, 0)),
        out_shape=jax.ShapeDtypeStruct((n_tok, d), F32),
        compiler_params=_params(("parallel",), vmem_mb),
    )(a2d, w, x2d, mod, g)


def _mix_ffn_proj_kernel(a_ref, wo_ref, x_ref, gtm_ref, g01_ref, g02_ref, scf_ref, shf_ref, wg_ref,
                         wu_ref, wd_ref, gtf_ref, g03_ref, g10_ref, sc1_ref, sh1_ref, wi_ref,
                         x2_ref, u_ref):
    x1 = x_ref[...] + _mod_row(gtm_ref) * (_rms(_mm(a_ref[...], wo_ref[...], "bf16")) * g01_ref[...])
    h = _normmod(x1, g02_ref[...], _mod_row(scf_ref), _mod_row(shf_ref))
    act = _silu(_mm(h, wg_ref[...], "bf16")) * _mm(h, wu_ref[...], "bf16")
    x2 = x1 + _mod_row(gtf_ref) * (_rms(_mm(act, wd_ref[...], "bf16")) * g03_ref[...])
    x2_ref[...] = x2
    h1 = _normmod(x2, g10_ref[...], _mod_row(sc1_ref), _mod_row(sh1_ref))
    u_ref[...] = _mm(h1, wi_ref[...], "bf16").astype(u_ref.dtype)


def _mix_ffn_proj(mixed, w_out, x2d, mod0, mod1, g_norm, w_gate, w_up, w_down, w_in1, *, tm,
                  tiles_per_batch, vmem_mb):
    n_tok, d = x2d.shape
    sh_m, sc_m, gt_m, sh_f, sc_f, gt_f = range(N_MOD)
    tok = lambda w: pl.BlockSpec((tm, w), lambda i: (i, 0))
    vec = pl.BlockSpec((1, d), lambda i: (0, 0))
    mod = lambda col: _mod_spec(False, tm, d, col, tiles_per_batch)
    once = lambda w: pl.BlockSpec(w.shape, lambda i: (0, 0), pipeline_mode=pl.Buffered(1))
    g = lambda l, k: g_norm[l, k].reshape(1, d)
    return pl.pallas_call(
        _mix_ffn_proj_kernel,
        grid=(n_tok // tm,),
        in_specs=[tok(mixed.shape[1]), once(w_out), tok(d), mod(gt_m), vec, vec, mod(sc_f), mod(sh_f),
                  once(w_gate), once(w_up), once(w_down), mod(gt_f), vec, vec, mod(sc_m), mod(sh_m),
                  once(w_in1)],
        out_specs=[tok(d), tok(w_in1.shape[1])],
        out_shape=[jax.ShapeDtypeStruct((n_tok, d), F32),
                   jax.ShapeDtypeStruct((n_tok, w_in1.shape[1]), BF16)],
        compiler_params=_params(("parallel",), vmem_mb),
    )(mixed, w_out, x2d, mod0, g(0, 1), g(0, 2), mod0, mod0, w_gate, w_up, w_down, mod0, g(0, 3),
      g(1, 0), mod1, mod1, w_in1)


def _glu_route_kernel(a_ref, w_ref, x_ref, gtm_ref, g1_ref, g2_ref, scf_ref, shf_ref, wr_ref,
                      x3_ref, comb_ref, hb_ref, wp_scr, *, n_experts):
    @pl.when(pl.program_id(0) == 0)
    def _():
        _pack_router_weight(wr_ref, wp_scr, n_experts)

    z = _mm(a_ref[...], w_ref[...], "bf16")
    d = z.shape[1] // 2
    x3 = x_ref[...] + _mod_row(gtm_ref) * (_rms(z[:, :d] * jax.nn.sigmoid(z[:, d:])) * g1_ref[...])
    x3_ref[...] = x3
    h = _normmod(x3, g2_ref[...], _mod_row(scf_ref), _mod_row(shf_ref))
    hb_ref[...] = h.astype(hb_ref.dtype)
    comb_ref[...] = _route(h, wp_scr[...], n_experts)


def _glu_route(a2d, w_glu, x2d, mod1, g_norm, wr_pad, *, tm, n_experts, tiles_per_batch, vmem_mb):
    n_tok, d = x2d.shape
    sh_m, sc_m, gt_m, sh_f, sc_f, gt_f = range(N_MOD)
    tok = lambda w: pl.BlockSpec((tm, w), lambda i: (i, 0))
    vec = pl.BlockSpec((1, d), lambda i: (0, 0))
    mod = lambda col: _mod_spec(False, tm, d, col, tiles_per_batch)
    g = lambda k: g_norm[1, k].reshape(1, d)
    return pl.pallas_call(
        functools.partial(_glu_route_kernel, n_experts=n_experts),
        grid=(n_tok // tm,),
        in_specs=[tok(a2d.shape[1]), pl.BlockSpec(w_glu.shape, lambda i: (0, 0)), tok(d), mod(gt_m),
                  vec, vec, mod(sc_f), mod(sh_f), pl.BlockSpec(wr_pad.shape, lambda i: (0, 0))],
        out_specs=[tok(d), tok(LANES), tok(d)],
        out_shape=[jax.ShapeDtypeStruct((n_tok, d), F32),
                   jax.ShapeDtypeStruct((n_tok, LANES), F32),
                   jax.ShapeDtypeStruct((n_tok, d), BF16)],
        scratch_shapes=[pltpu.VMEM(wr_pad.shape, BF16)],
        compiler_params=_params(("arbitrary",), vmem_mb),
    )(a2d, w_glu, x2d, mod1, g(1), g(2), mod1, mod1, wr_pad)


def _ffn_kernel(x_ref, g1_ref, sc_ref, sh_ref, wg_ref, wu_ref, wd_ref, gt_ref, g2_ref,
                o_ref, h_scr, acc_scr, *, prec, n_c):
    c = pl.program_id(1)

    @pl.when(c == 0)
    def _():
        h = _normmod(x_ref[...], g1_ref[...], _mod_row(sc_ref), _mod_row(sh_ref))
        h_scr[...] = h.astype(h_scr.dtype)
        acc_scr[...] = jnp.zeros_like(acc_scr)

    h = h_scr[...]
    act = _silu(_mm(h, wg_ref[...], prec)) * _mm(h, wu_ref[...], prec)
    acc_scr[...] += _mm(act, wd_ref[...], prec)

    @pl.when(c == n_c - 1)
    def _():
        o_ref[...] = x_ref[...] + _mod_row(gt_ref) * (_rms(acc_scr[...]) * g2_ref[...])


def _ffn(x2d, g1, mod, sc_col, sh_col, gt_col, g2, wg, wu, wd, *, tm, fc, prec,
         per_token, tiles_per_batch, vmem_mb=None):
    n_tok, d = x2d.shape
    n_c = wg.shape[1] // fc
    tok = pl.BlockSpec((tm, d), lambda i, c: (i, 0))
    vec = pl.BlockSpec((1, d), lambda i, c: (0, 0))
    h_dtype = BF16 if prec == "bf16" else F32
    w_mode = dict(pipeline_mode=pl.Buffered(1)) if n_c == 1 else {}
    return pl.pallas_call(
        functools.partial(_ffn_kernel, prec=prec, n_c=n_c),
        grid=(n_tok // tm, n_c),
        in_specs=[tok, vec,
                  _mod_spec(per_token, tm, d, sc_col, tiles_per_batch),
                  _mod_spec(per_token, tm, d, sh_col, tiles_per_batch),
                  pl.BlockSpec((d, fc), lambda i, c: (0, c), **w_mode),
                  pl.BlockSpec((d, fc), lambda i, c: (0, c), **w_mode),
                  pl.BlockSpec((fc, d), lambda i, c: (c, 0), **w_mode),
                  _mod_spec(per_token, tm, d, gt_col, tiles_per_batch), vec],
        out_specs=tok,
        out_shape=jax.ShapeDtypeStruct((n_tok, d), F32),
        scratch_shapes=[pltpu.VMEM((tm, d), h_dtype), pltpu.VMEM((tm, d), F32)],
        compiler_params=_params(("parallel", "arbitrary"), vmem_mb),
    )(x2d, g1, mod, mod, wg, wu, wd, mod, g2)


def _pack_router_weight(wr_ref, wp_scr, n_experts):
    w1, w2, w3 = _split(wr_ref[...], 3)
    fwd = lambda x, k: jnp.concatenate([x[:, LANES - k * n_experts:], x[:, :LANES - k * n_experts]], axis=1)
    packed = w1.astype(F32) + (fwd(w2.astype(F32), 1) + fwd(w3.astype(F32), 2))
    wp_scr[...] = packed.astype(wp_scr.dtype)


def _route(h, w, n_experts):
    h1, h2, h3 = _split(h, 3)
    p1, p2, p3 = _dot(h1, w), _dot(h2, w), _dot(h3, w)
    back = lambda x, k: jnp.concatenate([x[:, k * n_experts:], x[:, :k * n_experts]], axis=1)
    small = (back(p1, 2) + p3) + back(p2, 1)
    logits = p1 + ((back(p1, 1) + p2) + small)
    lane = lax.broadcasted_iota(jnp.int32, logits.shape, 1).astype(F32)
    neg = -jnp.inf
    l1 = jnp.where(lane < n_experts, logits, neg)
    m1 = jnp.max(l1, axis=-1, keepdims=True)
    i1 = jnp.min(jnp.where(l1 == m1, lane, float(LANES)), axis=-1, keepdims=True)
    l2 = jnp.where(lane == i1, neg, l1)
    m2 = jnp.max(l2, axis=-1, keepdims=True)
    i2 = jnp.min(jnp.where(l2 == m2, lane, float(LANES)), axis=-1, keepdims=True)
    e2 = jnp.exp(m2 - m1)
    den = 1.0 + e2
    return jnp.where(lane == i1, 1.0 / den, 0.0) + jnp.where(lane == i2, e2 / den, 0.0)


def _router_kernel(x_ref, g_ref, sc_ref, sh_ref, wr_ref, comb_ref, hb_ref, wp_scr, *, n_experts):
    @pl.when(pl.program_id(0) == 0)
    def _():
        _pack_router_weight(wr_ref, wp_scr, n_experts)

    h = _normmod(x_ref[...], g_ref[...], _mod_row(sc_ref), _mod_row(sh_ref))
    hb_ref[...] = h.astype(hb_ref.dtype)
    comb_ref[...] = _route(h, wp_scr[...], n_experts)


def _router(x2d, g, mod, sc_col, sh_col, wr_pad, *, tm, n_experts, per_token, tiles_per_batch):
    n_tok, d = x2d.shape
    return pl.pallas_call(
        functools.partial(_router_kernel, n_experts=n_experts),
        grid=(n_tok // tm,),
        in_specs=[pl.BlockSpec((tm, d), lambda i: (i, 0)),
                  pl.BlockSpec((1, d), lambda i: (0, 0)),
                  _mod_spec(per_token, tm, d, sc_col, tiles_per_batch),
                  _mod_spec(per_token, tm, d, sh_col, tiles_per_batch),
                  pl.BlockSpec((d, LANES), lambda i: (0, 0))],
        out_specs=[pl.BlockSpec((tm, LANES), lambda i: (i, 0)),
                   pl.BlockSpec((tm, d), lambda i: (i, 0))],
        out_shape=[jax.ShapeDtypeStruct((n_tok, LANES), F32),
                   jax.ShapeDtypeStruct((n_tok, d), BF16)],
        scratch_shapes=[pltpu.VMEM((d, LANES), BF16)],
        compiler_params=_params(("arbitrary",)),
    )(x2d, g, mod, mod, wr_pad)


MOE_TB = 256
MOE_SEG = 8
MOE_DTYPE = F32
MOE_TM = 512


def _moe_block_rows(n_experts):
    return -(-(2 * MOE_TB + n_experts * (MOE_SEG - 1)) // LANES) * LANES


def _routing_slots(comb, dest):
    lane = lax.broadcasted_iota(jnp.int32, comb.shape, 1).astype(F32)
    sel = comb > 0.0
    i_lo = jnp.min(jnp.where(sel, lane, float(LANES)), axis=-1, keepdims=True)
    i_hi = jnp.max(jnp.where(sel, lane, -1.0), axis=-1, keepdims=True)
    pick = lambda i, v: jnp.sum(jnp.where(lane == i, v, 0.0), axis=-1, keepdims=True)
    has_lo = i_lo < float(LANES)
    has_hi = i_hi > i_lo
    d_lo = jnp.where(has_lo, pick(i_lo, dest), -1.0)
    d_hi = jnp.where(has_hi, pick(i_hi, dest), -1.0)
    w_lo = pick(i_lo, comb)
    w_hi = jnp.where(has_hi, pick(i_hi, comb), 0.0)
    out = jnp.where(lane == 0.0, d_lo, 0.0)
    for k, v in ((1.0, d_hi), (2.0, w_lo), (3.0, w_hi)):
        out = jnp.where(lane == k, v, out)
    return out


def _one_hot_rows(row, n_rows):
    tb = row.shape[0]
    lane = lax.broadcasted_iota(jnp.int32, (1, LANES), 1).astype(F32)
    row = jnp.broadcast_to(row, (tb, LANES))
    blocks = [jnp.where(row == lane + float(cb * LANES), 1.0, 0.0) for cb in range(n_rows // LANES)]
    return jnp.concatenate(blocks, axis=1).astype(BF16)


_SEG_BITS = (256, 128, 64, 32, 16, 8)


def _segment_copies(make_copy, blk, pos_ref, off_ref, cnt_ref, n_experts, start):
    for e in range(n_experts):
        seg = blk * n_experts + e
        in_blk, in_sorted, n = off_ref[seg], pos_ref[seg], cnt_ref[seg]
        for size in _SEG_BITS:
            has = (n & size) != 0

            @pl.when(has)
            def _(in_blk=in_blk, in_sorted=in_sorted, size=size):
                cp = make_copy(pl.multiple_of(in_blk, MOE_SEG), pl.multiple_of(in_sorted, MOE_SEG), size)
                cp.start() if start else cp.wait()

            step = jnp.where(has, size, 0)
            in_blk, in_sorted = in_blk + step, in_sorted + step


def _moe_sort_kernel(pos_ref, off_ref, cnt_ref, gpos_ref, glen_ref, hbp_ref, combp_ref, hbs_ref,
                     combs_ref, dest_ref, xs_hbm, buf, zbuf, sems, *, n_rows, n_experts, n_blk, n_blk_p):
    b = pl.program_id(0)
    slot = b % 2

    def writes(blk, slot, start):
        def make_copy(block_row, sorted_row, size):
            return pltpu.make_async_copy(buf.at[slot, pl.ds(block_row, size)],
                                         xs_hbm.at[pl.ds(sorted_row, size)], sems.at[slot])
        _segment_copies(make_copy, blk, pos_ref, off_ref, cnt_ref, n_experts, start)

    def zero_gaps(start):
        for e in range(n_experts):
            row, n = gpos_ref[e], glen_ref[e]
            for size in _SEG_BITS:
                has = (n & size) != 0

                @pl.when(has)
                def _(row=row, size=size):
                    cp = pltpu.make_async_copy(
                        zbuf.at[pl.ds(0, size)],
                        xs_hbm.at[pl.ds(pl.multiple_of(row, MOE_SEG), size)], sems.at[2])
                    cp.start() if start else cp.wait()

                row = row + jnp.where(has, size, 0)
        tail_row, tail_n, big = gpos_ref[n_experts], glen_ref[n_experts], _SEG_BITS[0]

        def piece(i, carry):
            cp = pltpu.make_async_copy(
                zbuf.at[pl.ds(0, big)],
                xs_hbm.at[pl.ds(pl.multiple_of(tail_row + i * big, MOE_SEG), big)], sems.at[2])
            cp.start() if start else cp.wait()
            return carry

        lax.fori_loop(0, tail_n // big, piece, 0)

    @pl.when(b == 0)
    def _():
        zbuf[...] = jnp.zeros_like(zbuf)
        zero_gaps(True)

    @pl.when(b >= 2)
    def _():
        writes(b - 2, slot, False)

    in_prompt = b < n_blk_p
    comb = jnp.where(in_prompt, combp_ref[...], combs_ref[...])
    hb = jnp.where(in_prompt, hbp_ref[...], hbs_ref[...])
    tb = comb.shape[0]
    sel = jnp.where(comb > 0.0, 1.0, 0.0)
    r = lax.broadcasted_iota(jnp.int32, (tb, tb), 0)
    c = lax.broadcasted_iota(jnp.int32, (tb, tb), 1)
    rank = _dot(jnp.where(c < r, 1.0, 0.0).astype(BF16), sel.astype(BF16))
    cnt = jnp.sum(sel, axis=0, keepdims=True)
    cnt = jnp.floor((cnt + (MOE_SEG - 1)) / MOE_SEG) * MOE_SEG
    ru = lax.broadcasted_iota(jnp.int32, (LANES, LANES), 0)
    cu = lax.broadcasted_iota(jnp.int32, (LANES, LANES), 1)
    off = _dot(jnp.broadcast_to(cnt, (8, LANES)).astype(BF16),
               jnp.where(ru < cu, 1.0, 0.0).astype(BF16))[0:1]
    slots = _routing_slots(comb, off + rank)
    dest_ref[...] = slots
    pt = _one_hot_rows(slots[:, 0:1], n_rows) + _one_hot_rows(slots[:, 1:2], n_rows)
    buf[slot] = _dot(pt, hb, ((0,), (0,))).astype(buf.dtype)
    writes(b, slot, True)

    @pl.when(b == n_blk - 1)
    def _():
        writes(b, slot, False)
        if n_blk > 1:
            writes(b - 1, 1 - slot, False)
        zero_gaps(False)


def _moe_sort(seg, gap_pos, gap_len, hb_p, comb_p, hb_s, comb_s, n_sorted_rows, *, n_experts):
    n_p, d = hb_p.shape
    n_blk_p = n_p // MOE_TB
    n_blk = n_blk_p + 1
    n_rows = _moe_block_rows(n_experts)
    last_p = n_blk_p - 1
    return pl.pallas_call(
        functools.partial(_moe_sort_kernel, n_rows=n_rows, n_experts=n_experts, n_blk=n_blk,
                          n_blk_p=n_blk_p),
        grid_spec=pltpu.PrefetchScalarGridSpec(
            num_scalar_prefetch=5, grid=(n_blk,),
            in_specs=[pl.BlockSpec((MOE_TB, d), lambda b, *_: (jnp.minimum(b, last_p), 0)),
                      pl.BlockSpec((MOE_TB, LANES), lambda b, *_: (jnp.minimum(b, last_p), 0)),
                      pl.BlockSpec((MOE_TB, d), lambda b, *_: (0, 0)),
                      pl.BlockSpec((MOE_TB, LANES), lambda b, *_: (0, 0))],
            out_specs=[pl.BlockSpec((MOE_TB, LANES), lambda b, *_: (b, 0)),
                       pl.BlockSpec(memory_space=pl.ANY)],
            scratch_shapes=[pltpu.VMEM((2, n_rows, d), MOE_DTYPE),
                            pltpu.VMEM((_SEG_BITS[0], d), MOE_DTYPE),
                            pltpu.SemaphoreType.DMA((3,))]),
        out_shape=[jax.ShapeDtypeStruct((n_blk * MOE_TB, LANES), F32),
                   jax.ShapeDtypeStruct((n_sorted_rows, d), MOE_DTYPE)],
        compiler_params=_params(("arbitrary",)),
    )(*seg, gap_pos, gap_len, hb_p, comb_p, hb_s, comb_s)


def _experts_kernel(te_ref, nt_ref, x_ref, wg_ref, wu_ref, wd_ref, o_ref, acc_scr, *, n_c):
    del te_ref
    k = pl.program_id(0)
    c = pl.program_id(1)
    active = k < nt_ref[0]

    @pl.when(active)
    def _():
        h = x_ref[...].astype(BF16)
        half = wg_ref.shape[2] // 2
        y = None
        for s in range(2):
            cols = slice(s * half, (s + 1) * half)
            act = _silu(_dot(h, wg_ref[0, :, cols])) * _dot(h, wu_ref[0, :, cols])
            part = _dot(act.astype(BF16), wd_ref[0, cols, :])
            y = part if y is None else y + part

        @pl.when(c == 0)
        def _():
            acc_scr[...] = y

        @pl.when(c > 0)
        def _():
            acc_scr[...] += y

        @pl.when(c == n_c - 1)
        def _():
            o_ref[...] = acc_scr[...].astype(o_ref.dtype)

    @pl.when(jnp.logical_not(active) & (c == n_c - 1))
    def _():
        o_ref[...] = jnp.zeros_like(o_ref)


def _experts(tile_expert, n_tiles, xs, wg, wu, wd, *, fc):
    n_rows, d = xs.shape
    n_c = wg.shape[2] // fc

    def row_map(k, c, te, nt):
        return (jnp.minimum(k, nt[0] - 1), 0)

    def w_map(k, c, te, nt):
        return (te[k], 0, jnp.where(k < nt[0], c, n_c - 1))

    def wd_map(k, c, te, nt):
        return (te[k], jnp.where(k < nt[0], c, n_c - 1), 0)

    return pl.pallas_call(
        functools.partial(_experts_kernel, n_c=n_c),
        grid_spec=pltpu.PrefetchScalarGridSpec(
            num_scalar_prefetch=2, grid=(n_rows // MOE_TM, n_c),
            in_specs=[pl.BlockSpec((MOE_TM, d), row_map),
                      pl.BlockSpec((1, d, fc), w_map), pl.BlockSpec((1, d, fc), w_map),
                      pl.BlockSpec((1, fc, d), wd_map)],
            out_specs=pl.BlockSpec((MOE_TM, d), lambda k, c, te, nt: (k, 0)),
            scratch_shapes=[pltpu.VMEM((MOE_TM, d), F32)]),
        out_shape=jax.ShapeDtypeStruct((n_rows, d), MOE_DTYPE),
        compiler_params=_params(("arbitrary", "arbitrary"), VMEM_LIMIT_MB),
    )(tile_expert, n_tiles, xs, wg, wu, wd)


def _moe_combine_kernel(pos_ref, off_ref, cnt_ref, ys_hbm, dest_ref, x_ref, gt_ref, g_ref,
                        o_ref, buf, sems, *, n_experts, n_blk, blk0):
    b = pl.program_id(0)
    slot = b % 2

    def reads(blk, slot, start):
        def make_copy(block_row, sorted_row, size):
            return pltpu.make_async_copy(ys_hbm.at[pl.ds(sorted_row, size)],
                                         buf.at[slot, pl.ds(block_row, size)], sems.at[slot])
        _segment_copies(make_copy, blk0 + blk, pos_ref, off_ref, cnt_ref, n_experts, start)

    @pl.when(b == 0)
    def _():
        buf[...] = jnp.zeros_like(buf)
        reads(b, slot, True)

    if n_blk > 1:
        @pl.when(b + 1 < n_blk)
        def _():
            reads(b + 1, 1 - slot, True)

    reads(b, slot, False)
    yb = buf[slot].astype(BF16)
    slots = dest_ref[...]
    n_rows = yb.shape[0]
    y = (slots[:, 2:3] * _dot(_one_hot_rows(slots[:, 0:1], n_rows), yb)
         + slots[:, 3:4] * _dot(_one_hot_rows(slots[:, 1:2], n_rows), yb))
    o_ref[...] = x_ref[...] + _mod_row(gt_ref) * (_rms(y) * g_ref[...])


def _moe_combine(sorted_pos, block_off, seg_cnt, ys, dest, x2d, mod, gt_col, g, *, blk0,
                 n_experts, per_token, tiles_per_batch):
    n_tok, d = x2d.shape
    n_blk = n_tok // MOE_TB
    n_rows = _moe_block_rows(n_experts)
    return pl.pallas_call(
        functools.partial(_moe_combine_kernel, n_experts=n_experts, n_blk=n_blk, blk0=blk0),
        grid_spec=pltpu.PrefetchScalarGridSpec(
            num_scalar_prefetch=3, grid=(n_blk,),
            in_specs=[pl.BlockSpec(memory_space=pl.ANY),
                      pl.BlockSpec((MOE_TB, LANES), lambda b, *_: (blk0 + b, 0)),
                      pl.BlockSpec((MOE_TB, d), lambda b, *_: (b, 0)),
                      _mod_spec(per_token, MOE_TB, d, gt_col, tiles_per_batch),
                      pl.BlockSpec((1, d), lambda b, *_: (0, 0))],
            out_specs=pl.BlockSpec((MOE_TB, d), lambda b, *_: (b, 0)),
            scratch_shapes=[pltpu.VMEM((2, n_rows, d), MOE_DTYPE), pltpu.SemaphoreType.DMA((2,))]),
        out_shape=jax.ShapeDtypeStruct((n_tok, d), F32),
        compiler_params=_params(("arbitrary",)),
    )(sorted_pos, block_off, seg_cnt, ys, dest, x2d, mod, g)


def _moe_schedule(combs, n_experts, n_tiles_max):
    def block_counts(comb):
        n_blk = comb.shape[0] // MOE_TB
        sel = (comb[:, :n_experts] > 0.0).astype(jnp.int32)
        return jnp.sum(sel.reshape(n_blk, MOE_TB, n_experts), axis=1)

    cnt = jnp.concatenate([block_counts(c) for c in combs], axis=0)
    cnt = -(-cnt // MOE_SEG) * MOE_SEG
    off = jnp.cumsum(cnt, axis=1) - cnt
    tot = jnp.sum(cnt, axis=0)
    grp = -(-tot // MOE_TM) * MOE_TM
    g_end = jnp.cumsum(grp)
    sorted_pos = (g_end - grp)[None, :] + (jnp.cumsum(cnt, axis=0) - cnt)
    n_tiles = g_end[-1] // MOE_TM
    first_row = jnp.arange(n_tiles_max, dtype=jnp.int32) * MOE_TM
    tile_expert = jnp.sum(first_row[:, None] >= g_end[None, :], axis=1)
    last = jnp.sum((n_tiles - 1) * MOE_TM >= g_end)
    tile_expert = jnp.minimum(tile_expert, last).astype(jnp.int32)
    flat = lambda a: a.reshape(-1).astype(jnp.int32)
    seg = (flat(sorted_pos), flat(off), flat(cnt))
    gap_pos = flat(jnp.concatenate([g_end - grp + tot, g_end[-1:]]))
    gap_len = flat(jnp.concatenate([grp - tot, n_tiles_max * MOE_TM - g_end[-1:]]))
    return seg, gap_pos, gap_len, tile_expert, n_tiles.reshape(1).astype(jnp.int32)


def _gla_conv_kernel(p_ref, wgu_ref, bgu_ref, gh_ref, wc_ref, mixed_ref, st_ref, cst_ref,
                     s_scr, uext_scr, *, tt, n_t, heads, dk, dv, cw):
    ti = pl.program_id(1)
    hk = heads * dk
    hv = heads * dv
    o_q, o_k, o_v, o_og = 0, hk, 2 * hk, 2 * hk + hv
    o_ch = o_og + hv
    o_cb, o_cc, o_lr = o_ch + cw, o_ch + 2 * cw, o_ch + 3 * cw
    L = GLA_CHUNK

    @pl.when(ti == 0)
    def _():
        s_scr[...] = jnp.zeros_like(s_scr)
        uext_scr[0:8, :] = jnp.zeros((8, cw), F32)

    logit = _dot(p_ref[:, o_lr:o_lr + LANES].astype(BF16), wgu_ref[...]) + bgu_ref[...]
    logg = _log_sigmoid(logit) / GATE_NORMALIZER

    r = lax.broadcasted_iota(jnp.int32, (tt, tt), 0)
    c = lax.broadcasted_iota(jnp.int32, (tt, tt), 1)
    tri = jnp.where((r // L == c // L) & (c <= r), 1.0, 0.0).astype(BF16)
    g_hi = logg.astype(BF16)
    rem = logg - g_hi.astype(F32)
    g_mid = rem.astype(BF16)
    g_lo = (rem - g_mid.astype(F32)).astype(BF16)
    bc = _dot(tri, g_hi) + (_dot(tri, g_mid) + _dot(tri, g_lo))

    rl = lax.broadcasted_iota(jnp.int32, (L, L), 0)
    cl = lax.broadcasted_iota(jnp.int32, (L, L), 1)
    tril = cl <= rl
    lane = lax.broadcasted_iota(jnp.int32, (L, LANES), 1)
    heads_per_blk = LANES // dk
    nt_dims = ((1,), (1,))

    state = [s_scr[h] for h in range(heads)]
    for ck in range(tt // L):
        rows = slice(L * ck, L * ck + L)
        b = bc[rows]
        bl = b[L - 1:L]
        q_in = (p_ref[rows, o_q:o_q + hk] * (dk ** -0.5)) * jnp.exp(b)
        k = p_ref[rows, o_k:o_k + hk]
        k_out = (k * jnp.exp(-b)).astype(BF16)
        k_dec = (k * jnp.exp(bl - b)).astype(BF16)
        dec = jnp.exp(bl)
        q_in = q_in.astype(BF16)
        for h in range(heads):
            blk = slice(LANES * (h // heads_per_blk), LANES * (h // heads_per_blk) + LANES)
            in_head = (lane // dk) == (h % heads_per_blk)
            qm = jnp.where(in_head, q_in[:, blk], jnp.zeros_like(q_in[:, blk]))
            att = _dot(qm, k_out[:, blk], nt_dims)
            att = jnp.where(tril, att, 0.0).astype(BF16)
            vh = p_ref[rows, o_v + dv * h:o_v + dv * h + dv].astype(BF16)
            o = _dot(att, vh) + _dot(qm, state[h].astype(BF16), nt_dims)
            ds_t = _dot(vh, k_dec[:, blk], ((0,), (0,)))
            state[h] = state[h] * dec[:, blk] + ds_t
            og = p_ref[rows, o_og + dv * h:o_og + dv * h + dv]
            res = (_rms(o) * gh_ref[...]) * _silu(og)
            mixed_ref[rows, dv * h:dv * h + dv] = res.astype(mixed_ref.dtype)
    for h in range(heads):
        s_scr[h] = state[h]

    u = p_ref[:, o_cc:o_cc + cw] * p_ref[:, o_ch:o_ch + cw]
    uext_scr[8:8 + tt, :] = u
    y = (wc_ref[0:1, :] * uext_scr[6:6 + tt, :] + wc_ref[1:2, :] * uext_scr[7:7 + tt, :]
         + wc_ref[2:3, :] * u)
    mixed_ref[:, hv:hv + cw] = (p_ref[:, o_cb:o_cb + cw] * y).astype(mixed_ref.dtype)
    tail = uext_scr[tt + 6:tt + 8, :]
    uext_scr[6:8, :] = tail

    @pl.when(ti == n_t - 1)
    def _():
        st_ref[0] = s_scr[...]
        cst_ref[0] = tail


def _gla_conv_prompt(proj, wgu_pad, bgu, g_head, w_conv, *, bsz, t, tt, heads, dk, dv, cw):
    n_t = t // tt
    n_in = proj.shape[1]
    width = heads * dv + cw
    return pl.pallas_call(
        functools.partial(_gla_conv_kernel, tt=tt, n_t=n_t, heads=heads, dk=dk, dv=dv, cw=cw),
        grid=(bsz, n_t),
        in_specs=[pl.BlockSpec((tt, n_in), lambda b, i: (b * n_t + i, 0)),
                  pl.BlockSpec(wgu_pad.shape, lambda b, i: (0, 0)),
                  pl.BlockSpec(bgu.shape, lambda b, i: (0, 0)),
                  pl.BlockSpec(g_head.shape, lambda b, i: (0, 0)),
                  pl.BlockSpec(w_conv.shape, lambda b, i: (0, 0))],
        out_specs=[pl.BlockSpec((tt, width), lambda b, i: (b * n_t + i, 0)),
                   pl.BlockSpec((1, heads, dv, LANES), lambda b, i: (b, 0, 0, 0)),
                   pl.BlockSpec((1, 2, cw), lambda b, i: (b, 0, 0))],
        out_shape=[jax.ShapeDtypeStruct((bsz * t, width), BF16),
                   jax.ShapeDtypeStruct((bsz, heads, dv, LANES), F32),
                   jax.ShapeDtypeStruct((bsz, 2, cw), F32)],
        scratch_shapes=[pltpu.VMEM((heads, dv, LANES), F32), pltpu.VMEM((tt + 8, cw), F32)],
        compiler_params=_params(("parallel", "arbitrary"), VMEM_LIMIT_MB),
    )(proj, wgu_pad, bgu, g_head, w_conv)


def _gla_conv_step_kernel(q_ref, k_ref, v_ref, og_ref, ch_ref, cb_ref, cc_ref, lr_ref,
                          wgu_ref, bgu_ref, gh_ref, wc_ref, s_ref, cbuf_ref,
                          o_ref, y_ref, sn_ref, cn_ref, a_scr, *, dk):
    logit = _mm(wgu_ref[...], lr_ref[...], "f32") + bgu_ref[...]
    a_scr[...] = jnp.exp(_log_sigmoid(logit) / GATE_NORMALIZER)
    v_t = v_ref[...]

    def body(d, acc):
        a = a_scr[pl.ds(d, 1), :]
        kd = k_ref[pl.ds(d, 1), :]
        qd = q_ref[pl.ds(d, 1), :]
        s_new = a * s_ref[0, d] + kd * v_t
        sn_ref[0, d] = s_new
        return acc + (qd * (dk ** -0.5)) * s_new

    o = lax.fori_loop(0, dk, body, jnp.zeros(v_t.shape, F32))
    o = o * lax.rsqrt(jnp.mean(o * o, axis=0, keepdims=True) + EPS) * gh_ref[...]
    o_ref[...] = o * _silu(og_ref[...])
    u = cc_ref[...] * ch_ref[...]
    y = wc_ref[0] * cbuf_ref[0] + wc_ref[1] * cbuf_ref[1] + wc_ref[2] * u
    y_ref[...] = cb_ref[...] * y
    cn_ref[0] = cbuf_ref[1]
    cn_ref[1] = u


def _gla_conv_step(proj_t, wgu_t_pad, bgu_col, gh_col, wc_col, s_t, cbuf_t, *, heads, dk, dv, cw):
    bsz = proj_t.shape[1]
    hk, hv = heads * dk, heads * dv
    cs = cw // heads
    assert dv == LANES and cs == LANES and dk * 2 == LANES
    o_k, o_v, o_og = hk, 2 * hk, 2 * hk + hv
    o_ch = o_og + hv
    o_cb, o_cc, o_lr = o_ch + cw, o_ch + 2 * cw, o_ch + 3 * cw
    row = lambda off, size: (lambda h: (off // size + h, 0))
    blk = lambda size, off: pl.BlockSpec((size, bsz), row(off, size))
    return pl.pallas_call(
        functools.partial(_gla_conv_step_kernel, dk=dk),
        grid=(heads,),
        in_specs=[blk(dk, 0), blk(dk, o_k), blk(dv, o_v), blk(dv, o_og),
                  blk(cs, o_ch), blk(cs, o_cb), blk(cs, o_cc),
                  pl.BlockSpec((LANES, bsz), lambda h: (o_lr // LANES, 0)),
                  pl.BlockSpec((dk, LANES), lambda h: (h, 0)),
                  pl.BlockSpec((dk, 1), lambda h: (h, 0)),
                  pl.BlockSpec((dv, 1), lambda h: (0, 0)),
                  pl.BlockSpec((3, cs, 1), lambda h: (0, h, 0)),
                  pl.BlockSpec((1, dk, dv, bsz), lambda h: (h, 0, 0, 0)),
                  pl.BlockSpec((2, cs, bsz), lambda h: (0, h, 0))],
        out_specs=[pl.BlockSpec((dv, bsz), lambda h: (h, 0)),
                   pl.BlockSpec((cs, bsz), lambda h: (h, 0)),
                   pl.BlockSpec((1, dk, dv, bsz), lambda h: (h, 0, 0, 0)),
                   pl.BlockSpec((2, cs, bsz), lambda h: (0, h, 0))],
        out_shape=[jax.ShapeDtypeStruct((hv, bsz), F32),
                   jax.ShapeDtypeStruct((cw, bsz), F32),
                   jax.ShapeDtypeStruct(s_t.shape, F32),
                   jax.ShapeDtypeStruct(cbuf_t.shape, F32)],
        scratch_shapes=[pltpu.VMEM((dk, bsz), F32)],
        compiler_params=_params(("parallel",)),
    )(proj_t, proj_t, proj_t, proj_t, proj_t, proj_t, proj_t, proj_t,
      wgu_t_pad, bgu_col, gh_col, wc_col, s_t, cbuf_t)


def _s5_scan_kernel(*refs, tt, n_t, nb, half, n_cast):
    u_ref, bb_ref, cm_ref, abr_ref, abi_ref, d_ref = refs[:6]
    y_ref, sre_ref, sim_ref = refs[6 + n_cast:9 + n_cast]
    scr, tb_scr, xr_scr, xi_scr = refs[9 + 2 * n_cast:]
    _cast_passengers(refs[6:6 + n_cast], refs[9 + n_cast:9 + 2 * n_cast])
    ti = pl.program_id(1)
    n_blk = half // LANES

    @pl.when(ti == 0)
    def _():
        xr_scr[...] = jnp.zeros_like(xr_scr)
        xi_scr[...] = jnp.zeros_like(xi_scr)

    for b in range(nb):
        tb_scr[pl.ds(b, tt, stride=nb), :] = u_ref[b].astype(F32)
    for k in range(nb):
        rows = slice(k * tt, (k + 1) * tt)
        bu = _dot(tb_scr[rows, :].astype(BF16), bb_ref[0])
        for cb in range(2 * n_blk):
            scr[cb, rows, :] = bu[:, cb * LANES:(cb + 1) * LANES]
    a_re = [jnp.broadcast_to(abr_ref[0][:, cb * LANES:(cb + 1) * LANES], (nb, LANES))
            for cb in range(n_blk)]
    a_im = [jnp.broadcast_to(abi_ref[0][:, cb * LANES:(cb + 1) * LANES], (nb, LANES))
            for cb in range(n_blk)]

    def body(t, carry):
        rows = pl.ds(pl.multiple_of(t * nb, nb), nb)
        out = []
        for cb in range(n_blk):
            xr, xi = carry[2 * cb], carry[2 * cb + 1]
            nr = (a_re[cb] * xr - a_im[cb] * xi) + scr[cb, rows, :]
            ni = (a_re[cb] * xi + a_im[cb] * xr) + scr[n_blk + cb, rows, :]
            scr[cb, rows, :] = nr
            scr[n_blk + cb, rows, :] = ni
            out += [nr, ni]
        return tuple(out)

    init = []
    for cb in range(n_blk):
        init += [xr_scr[:, cb * LANES:(cb + 1) * LANES], xi_scr[:, cb * LANES:(cb + 1) * LANES]]
    fin = lax.fori_loop(0, tt, body, tuple(init), unroll=8)
    xr = jnp.concatenate([fin[2 * cb] for cb in range(n_blk)], axis=1)
    xi = jnp.concatenate([fin[2 * cb + 1] for cb in range(n_blk)], axis=1)
    xr_scr[...] = xr
    xi_scr[...] = xi
    for k in range(nb):
        rows = slice(k * tt, (k + 1) * tt)
        xs = jnp.concatenate([scr[cb, rows, :] for cb in range(2 * n_blk)], axis=1)
        y = _dot(xs.astype(BF16), cm_ref[0]) + d_ref[0] * tb_scr[rows, :]
        tb_scr[rows, :] = _gelu_tanh(y)
    for b in range(nb):
        y_ref[b] = tb_scr[pl.ds(b, tt, stride=nb), :].astype(y_ref.dtype)

    @pl.when(ti == n_t - 1)
    def _():
        sre_ref[...] = xr
        sim_ref[...] = xi


def _s5_prompt(u3d, bb_blk, c_blk, ab_re, ab_im, d_blk, cast_ws, *, tt):
    nb, t, d = u3d.shape
    n_j = d // LANES
    n_t = t // tt
    half = bb_blk.shape[2] // 2
    w_specs = _passenger_specs(cast_ws, n_j * n_t, lambda j, i: j * n_t + i)
    return pl.pallas_call(
        functools.partial(_s5_scan_kernel, tt=tt, n_t=n_t, nb=nb, half=half, n_cast=len(cast_ws)),
        grid=(n_j, n_t),
        in_specs=[pl.BlockSpec((nb, tt, LANES), lambda j, i: (0, i, j)),
                  pl.BlockSpec((1, LANES, 2 * half), lambda j, i: (j, 0, 0)),
                  pl.BlockSpec((1, 2 * half, LANES), lambda j, i: (j, 0, 0)),
                  pl.BlockSpec((1, 1, half), lambda j, i: (j, 0, 0)),
                  pl.BlockSpec((1, 1, half), lambda j, i: (j, 0, 0)),
                  pl.BlockSpec((1, 1, LANES), lambda j, i: (j, 0, 0))] + w_specs,
        out_specs=[pl.BlockSpec((nb, tt, LANES), lambda j, i: (0, i, j)),
                   pl.BlockSpec((nb, half), lambda j, i: (0, j)),
                   pl.BlockSpec((nb, half), lambda j, i: (0, j))] + w_specs,
        out_shape=[jax.ShapeDtypeStruct((nb, t, d), BF16),
                   jax.ShapeDtypeStruct((nb, n_j * half), F32),
                   jax.ShapeDtypeStruct((nb, n_j * half), F32)]
        + [jax.ShapeDtypeStruct(w.shape, BF16) for w in cast_ws],
        scratch_shapes=[pltpu.VMEM((2 * half // LANES, nb * tt, LANES), F32),
                        pltpu.VMEM((nb * tt, LANES), F32),
                        pltpu.VMEM((nb, half), F32), pltpu.VMEM((nb, half), F32)],
        compiler_params=_params(("arbitrary", "arbitrary"), VMEM_LIMIT_MB),
    )(u3d, bb_blk, c_blk, ab_re, ab_im, d_blk, *cast_ws)


def _s5_step_kernel(u_ref, bb_ref, cm_ref, abr_ref, abi_ref, d_ref, sre_ref, sim_ref,
                    y_ref, nre_ref, nim_ref, *, half):
    u = u_ref[...]
    bu = _mm(u, bb_ref[0], "f32")
    ar, ai = abr_ref[0], abi_ref[0]
    sr, si = sre_ref[...], sim_ref[...]
    nr = (ar * sr - ai * si) + bu[:, 0:half]
    ni = (ar * si + ai * sr) + bu[:, half:2 * half]
    nre_ref[...] = nr
    nim_ref[...] = ni
    y = _mm(jnp.concatenate([nr, ni], axis=1), cm_ref[0], "f32") + d_ref[0] * u
    y_ref[...] = _gelu_tanh(y)


def _s5_step(u2d, bb_blk, c_blk, ab_re, ab_im, d_blk, s_re, s_im):
    bsz, d = u2d.shape
    n_j = d // LANES
    half = bb_blk.shape[2] // 2
    st = pl.BlockSpec((bsz, half), lambda j: (0, j))
    return pl.pallas_call(
        functools.partial(_s5_step_kernel, half=half),
        grid=(n_j,),
        in_specs=[pl.BlockSpec((bsz, LANES), lambda j: (0, j)),
                  pl.BlockSpec((1, LANES, 2 * half), lambda j: (j, 0, 0)),
                  pl.BlockSpec((1, 2 * half, LANES), lambda j: (j, 0, 0)),
                  pl.BlockSpec((1, 1, half), lambda j: (j, 0, 0)),
                  pl.BlockSpec((1, 1, half), lambda j: (j, 0, 0)),
                  pl.BlockSpec((1, 1, LANES), lambda j: (j, 0, 0)),
                  st, st],
        out_specs=[pl.BlockSpec((bsz, LANES), lambda j: (0, j)), st, st],
        out_shape=[jax.ShapeDtypeStruct((bsz, d), F32),
                   jax.ShapeDtypeStruct(s_re.shape, F32),
                   jax.ShapeDtypeStruct(s_im.shape, F32)],
        compiler_params=_params(("parallel",)),
    )(u2d, bb_blk, c_blk, ab_re, ab_im, d_blk, s_re, s_im)


def _s5_zoh(a_re, a_im, log_dt, b_re, b_im):
    dt = jnp.exp(log_dt)[:, None]
    mag = jnp.exp(dt * a_re)
    ab_re, ab_im = mag * jnp.cos(dt * a_im), mag * jnp.sin(dt * a_im)
    den = a_re * a_re + a_im * a_im
    nr, ni = ab_re - 1.0, ab_im
    f_re = (nr * a_re + ni * a_im) / den
    f_im = (ni * a_re - nr * a_im) / den
    bb_re = f_re[..., None] * b_re - f_im[..., None] * b_im
    bb_im = f_re[..., None] * b_im + f_im[..., None] * b_re
    return ab_re, ab_im, bb_re, bb_im


def _s5_step_blocks(ab_re, ab_im, bb_re, bb_im, c_re, c_im, d_skip):
    g, p = ab_re.shape
    cg = S5_CHUNK_GROUPS
    n_j = g // cg
    eye = jnp.eye(cg, dtype=F32)

    def in_blk(m):
        m = m.reshape(n_j, cg, p, S5_GROUP)
        return jnp.einsum("jgpi,gh->jgihp", m, eye).reshape(n_j, cg * S5_GROUP, cg * p)

    def out_blk(m):
        m = m.reshape(n_j, cg, S5_GROUP, p)
        return jnp.einsum("jgip,gh->jgphi", m, eye).reshape(n_j, cg * p, cg * S5_GROUP)

    bb_blk = jnp.concatenate([in_blk(bb_re), in_blk(bb_im)], axis=2)
    c_blk = jnp.concatenate([out_blk(c_re), out_blk(-c_im)], axis=1)
    return (bb_blk, c_blk, ab_re.reshape(n_j, 1, cg * p), ab_im.reshape(n_j, 1, cg * p),
            d_skip.reshape(n_j, 1, cg * S5_GROUP))


def kernel(x_prompt, x_sample, c_prompt, c_sample, state_gla, state_conv, state_s5_re, state_s5_im, w_ada, b_ada, g_norm, w_in0, w_gate_up, b_gate_up, g_head_norm, w_conv, w_out0, w_ffn_gate, w_ffn_up, w_ffn_down, w_in1, s5_a_re, s5_a_im, s5_log_dt, s5_b_re, s5_b_im, s5_c_re, s5_c_im, s5_d, w_glu, w_router, w_exp_gate, w_exp_up, w_exp_down):
    bsz, t, d = x_prompt.shape
    bs = x_sample.shape[0]
    heads, dk, dv = state_gla.shape[2:]
    cw = state_conv.shape[3]
    lowrank = w_gate_up.shape[1]
    hk, hv = heads * dk, heads * dv
    n_experts = w_router.shape[2]
    g_s5, p_s5 = s5_a_re.shape[1:]

    mod = _ada(jnp.concatenate([c_prompt, c_sample], axis=0), w_ada, b_ada)
    mod_p = [mod[l, :bsz].reshape(bsz, 1, N_MOD * d) for l in range(2)]
    mod_s = [mod[l, bsz:] for l in range(2)]
    SH_M, SC_M, GT_M, SH_F, SC_F, GT_F = range(N_MOD)
    gn = lambda l, k: g_norm[l, k].reshape(1, d)

    w0 = w_in0[0]
    o_lr = 2 * hk + 2 * hv
    w0r = jnp.concatenate([w0[:, :o_lr], w0[:, o_lr + lowrank:], w0[:, o_lr:o_lr + lowrank],
                           jnp.zeros((d, LANES - lowrank), F32)], axis=1)
    wgu_pad = jnp.concatenate([w_gate_up[0], jnp.zeros((LANES - lowrank, hk), F32)], axis=0)
    zoh = _s5_zoh(s5_a_re[0], s5_a_im[0], s5_log_dt[0], s5_b_re[0], s5_b_im[0])
    bb_blk, c_blk, ab_re, ab_im, d_blk = _s5_step_blocks(*zoh, s5_c_re[0], s5_c_im[0], s5_d[0])
    wr_pad = jnp.concatenate([w_router[0], jnp.zeros((d, LANES - n_experts), F32)], axis=1)
    bf = lambda a: a.astype(BF16)

    tm = TOKEN_TILE
    tpb = t // tm
    pk = dict(per_token=False, tiles_per_batch=tpb)
    xp = x_prompt.reshape(bsz * t, d)
    proj, w_o0, w_fg, w_fu, w_fd, w_i1, w_gl = _inproj(
        xp, gn(0, 0), mod_p[0], SC_M, SH_M, bf(w0r), tm=tm, bn=w0r.shape[1], prec="bf16",
        vmem_mb=VMEM_LIMIT_MB, cast_ws=(w_out0[0], w_ffn_gate[0], w_ffn_up[0], w_ffn_down[0], w_in1[0], w_glu[0]), **pk)
    mixed, st_t, cst = _gla_conv_prompt(
        proj, bf(wgu_pad), b_gate_up[0].reshape(1, hk), g_head_norm[0].reshape(1, dv), w_conv[0],
        bsz=bsz, t=t, tt=SEQ_TILE, heads=heads, dk=dk, dv=dv, cw=cw)
    st_t = st_t.reshape(bsz, heads // 2, 2, dv, 2, dk)
    gla_p = jnp.stack([st_t[:, :, 0, :, 0, :], st_t[:, :, 1, :, 1, :]], axis=2)
    gla_p = jnp.swapaxes(gla_p.reshape(bsz, heads, dv, dk), 2, 3)
    x2, u = _mix_ffn_proj(mixed, w_o0, xp, mod_p[0], mod_p[1], g_norm, w_fg, w_fu, w_fd, w_i1, tm=tm,
                          tiles_per_batch=tpb, vmem_mb=VMEM_LIMIT_FFN_MB)
    ew = (w_exp_gate[0], w_exp_up[0], w_exp_down[0])
    yg, re_p, im_p, *ew_bf = _s5_prompt(
        u.reshape(bsz, t, d), bf(bb_blk), bf(c_blk), ab_re, ab_im, d_blk,
        tuple(w.reshape(-1, w.shape[2]) for w in ew), tt=SEQ_TILE)
    w_eg, w_eu, w_ed = (wb.reshape(w.shape) for wb, w in zip(ew_bf, ew))
    x3, comb, hb = _glu_route(yg.reshape(bsz * t, d), w_gl, x2, mod_p[1], g_norm, wr_pad, tm=tm,
                              n_experts=n_experts, tiles_per_batch=tpb, vmem_mb=VMEM_LIMIT_MB)

    sk = dict(per_token=True, tiles_per_batch=1)
    xs = x_sample.reshape(bs, d)
    proj_s = _inproj(xs, gn(0, 0), mod_s[0], SC_M, SH_M, w0r, tm=bs, bn=SAMPLE_PROJ_COLS, prec="f32", **sk)
    s_t = jnp.transpose(state_gla[:, 0], (1, 2, 3, 0))
    cbuf_t = jnp.transpose(state_conv[:, 0], (1, 2, 0))
    o_t, y_t, sn_t, cn_t = _gla_conv_step(
        proj_s.T, wgu_pad.T, b_gate_up[0].reshape(hk, 1), g_head_norm[0].reshape(dv, 1),
        w_conv[0].reshape(3, cw, 1), s_t, cbuf_t, heads=heads, dk=dk, dv=dv, cw=cw)
    mixed_s = jnp.concatenate([o_t, y_t], axis=0).T
    gla_s = jnp.transpose(sn_t, (3, 0, 1, 2))
    conv_s = jnp.transpose(cn_t, (2, 0, 1))
    x1s = _outproj(mixed_s, w_out0[0], xs, mod_s[0], GT_M, gn(0, 1), tm=bs, prec="f32", glu=False,
                   **sk)
    x2s = _ffn(x1s, gn(0, 2), mod_s[0], SC_F, SH_F, GT_F, gn(0, 3), w_ffn_gate[0], w_ffn_up[0],
               w_ffn_down[0], tm=bs, fc=SAMPLE_COLS, prec="f32", **sk)
    us = _inproj(x2s, gn(1, 0), mod_s[1], SC_M, SH_M, w_in1[0], tm=bs, bn=SAMPLE_COLS, prec="f32", **sk)
    ygs, re_s, im_s = _s5_step(us, bb_blk, c_blk, ab_re, ab_im, d_blk,
                               state_s5_re[:, 0].reshape(bs, g_s5 * p_s5),
                               state_s5_im[:, 0].reshape(bs, g_s5 * p_s5))
    x3s = _outproj(ygs, w_glu[0], x2s, mod_s[1], GT_M, gn(1, 1), tm=bs, prec="f32", glu=True,
                   vmem_mb=VMEM_LIMIT_MB, **sk)
    comb_s, hb_s = _router(x3s, gn(1, 2), mod_s[1], SC_F, SH_F, wr_pad, tm=bs, n_experts=n_experts,
                           **sk)

    assert bs <= MOE_TB
    n_blk = bsz * t // MOE_TB + 1
    n_tiles_max = -(-(n_blk * _moe_block_rows(n_experts)) // MOE_TM) + n_experts
    pad_rows = lambda a: jnp.concatenate([a, jnp.zeros((MOE_TB - bs, a.shape[1]), a.dtype)], axis=0)
    hb_s, comb_s = pad_rows(hb_s), pad_rows(comb_s)
    seg, gap_pos, gap_len, tile_expert, n_tiles = _moe_schedule((comb, comb_s), n_experts,
                                                                n_tiles_max)
    dest, xs_sorted = _moe_sort(seg, gap_pos, gap_len, hb, comb, hb_s, comb_s,
                                n_tiles_max * MOE_TM, n_experts=n_experts)
    ys_sorted = _experts(tile_expert, n_tiles, xs_sorted, w_eg, w_eu, w_ed, fc=EXPERT_FF_CHUNK)
    x4 = _moe_combine(*seg, ys_sorted, dest, x3, mod_p[1], GT_F, gn(1, 3), blk0=0,
                      n_experts=n_experts, per_token=False, tiles_per_batch=t // MOE_TB)
    x4s = _moe_combine(*seg, ys_sorted, dest, pad_rows(x3s), pad_rows(mod_s[1]), GT_F,
                       gn(1, 3), blk0=n_blk - 1, n_experts=n_experts, per_token=True,
                       tiles_per_batch=1)[:bs]

    return (x4.reshape(bsz, t, d), x4s.reshape(bs, 1, d),
            gla_p[:, None], cst[:, None],
            re_p.reshape(bsz, 1, g_s5, p_s5), im_p.reshape(bsz, 1, g_s5, p_s5),
            gla_s[:, None], conv_s[:, None],
            re_s.reshape(bs, 1, g_s5, p_s5), im_s.reshape(bs, 1, g_s5, p_s5))
```
